```python
import math
import jax
import jax.numpy as jnp
from jax import lax
import numpy as np

D_MODEL = 1024
BATCH = 32
SEQ = 2048
DEPTH = 1

CHUNK = 64
MIX_A = D_MODEL // 2
MIX_B = D_MODEL - MIX_A
HGRN_HEADS = 4
HGRN_KDIM = 128
HGRN_VDIM = MIX_A // HGRN_HEADS
HGRN_FDIM = HGRN_HEADS * HGRN_KDIM
S5_GROUP = 16
S5_GROUPS = MIX_B // S5_GROUP
S5_STATE = 64
PROJ_COLS = 2 * HGRN_FDIM + 2 * MIX_A + MIX_B
N_EXPERTS = 256
N_EXPERT_GROUPS = 8
TOPK_GROUPS = 4
TOP_K = 8
D_EXPERT = 256
D_SHARED = 256
ROUTE_SCALE = 2.5
ROW_BLOCK = 128
EPS = 1e-6
DT_MIN = 1e-3
DT_MAX = 1e-1

kernel_name = 'hgrn2_s5_hymba_moe_adaln_block'


def rms_norm(x, gain=None):
    xf = x.astype(jnp.float32)
    y = xf * lax.rsqrt(jnp.mean(xf * xf, axis=-1, keepdims=True) + EPS)
    if gain is not None:
        y = y * gain.astype(jnp.float32)
    return y.astype(x.dtype)


def hgrn2_mixer(q, f_logit, i, g, lb, gn_gain):
    bsz, seq, _ = q.shape
    n_chunks = seq // CHUNK
    f32 = jnp.float32
    f = lb + (1.0 - lb) * jax.nn.sigmoid(f_logit.astype(f32))
    k_shape = (bsz, n_chunks, CHUNK, HGRN_HEADS, HGRN_KDIM)
    v_shape = (bsz, n_chunks, CHUNK, HGRN_HEADS, HGRN_VDIM)
    qh = jax.nn.silu(q.astype(f32)).reshape(k_shape)
    kh = (1.0 - f).reshape(k_shape)
    vh = i.astype(f32).reshape(v_shape)
    b = jnp.cumsum(jnp.log(f).reshape(k_shape), axis=2)
    b_last = b[:, :, -1:]
    b_ref = b[:, :, CHUNK // 2 - 1:CHUNK // 2]
    att = jnp.einsum('bnthk,bnshk->bnhts', qh * jnp.exp(b - b_ref), kh * jnp.exp(b_ref - b))
    causal = jnp.tril(jnp.ones((CHUNK, CHUNK), dtype=bool))
    att = jnp.where(causal, att, 0.0)
    o = jnp.einsum('bnhts,bnshv->bnthv', att, vh)
    d_state = jnp.einsum('bnshk,bnshv->bnhkv', kh * jnp.exp(b_last - b), vh)
    chunk_decay = jnp.exp(b_last[:, :, 0])

    def carry_state(state, inp):
        decay, ds = inp
        return decay[..., None] * state + ds, state

    s0 = jnp.zeros((bsz, HGRN_HEADS, HGRN_KDIM, HGRN_VDIM), f32)
    _, s_prev = lax.scan(carry_state, s0, (jnp.moveaxis(chunk_decay, 1, 0), jnp.moveaxis(d_state, 1, 0)))
    s_prev = jnp.moveaxis(s_prev, 0, 1)
    o = o + jnp.einsum('bnthk,bnhkv->bnthv', qh * jnp.exp(b), s_prev)
    o = rms_norm(o, gn_gain).reshape(bsz, seq, MIX_A)
    return o * jax.nn.silu(g.astype(f32))


def s5_mixer(u, lam_re, lam_im, log_dt, b_re, b_im, c_re, c_im, d_skip, w_glu, b_glu):
    bsz, seq, _ = u.shape
    f32 = jnp.float32
    uf = u.astype(f32)
    ug = uf.reshape(bsz, seq, S5_GROUPS, S5_GROUP)
    lam = lax.complex(jnp.minimum(lam_re.astype(f32), -1e-4), lam_im.astype(f32))
    dt = jnp.exp(log_dt.astype(f32))[:, None]
    lam_bar = jnp.exp(lam * dt)
    b_bar = ((lam_bar - 1.0) / lam)[..., None] * lax.complex(b_re.astype(f32), b_im.astype(f32))
    bu = jnp.einsum('gpc,bsgc->bsgp', b_bar, ug)
    a = jnp.broadcast_to(lam_bar, (1, seq, S5_GROUPS, S5_STATE))

    def combine(e1, e2):
        a1, x1 = e1
        a2, x2 = e2
        return a1 * a2, a2 * x1 + x2

    _, states = lax.associative_scan(combine, (a, bu), axis=1)
    c_mat = lax.complex(c_re.astype(f32), c_im.astype(f32))
    y = jnp.einsum('gcp,bsgp->bsgc', c_mat, states).real.reshape(bsz, seq, MIX_B)
    y = jax.nn.gelu(y + d_skip.astype(f32) * uf)
    return y * jax.nn.sigmoid(y @ w_glu.astype(f32) + b_glu.astype(f32))


def moe_ffn(h, w_router, router_bias, w_gate, w_up, w_down, ws_gate, ws_up, ws_down):
    bsz, seq, dm = h.shape
    n_tok = bsz * seq
    f32 = jnp.float32
    xt = h.reshape(n_tok, dm)
    scores = jax.nn.sigmoid((xt @ w_router).astype(f32))
    sel = scores + router_bias.astype(f32)
    per_group = N_EXPERTS // N_EXPERT_GROUPS
    grp_score = lax.top_k(sel.reshape(n_tok, N_EXPERT_GROUPS, per_group), 2)[0].sum(-1)
    _, grp_idx = lax.top_k(grp_score, TOPK_GROUPS)
    grp_mask = jnp.any(grp_idx[..., None] == jnp.arange(N_EXPERT_GROUPS), axis=-2)
    sel = jnp.where(jnp.repeat(grp_mask, per_group, axis=-1), sel, -jnp.inf)
    _, top_idx = lax.top_k(sel, TOP_K)
    top_w = jnp.take_along_axis(scores, top_idx, axis=-1)
    top_w = top_w / jnp.sum(top_w, axis=-1, keepdims=True) * ROUTE_SCALE

    n_rows = n_tok * TOP_K
    flat_e = top_idx.reshape(n_rows)
    order = jnp.argsort(flat_e)
    sorted_e = flat_e[order]
    sorted_tok = (order // TOP_K).astype(jnp.int32)
    sorted_w = top_w.reshape(n_rows)[order]
    sizes = jnp.bincount(flat_e, length=N_EXPERTS)
    ends = jnp.cumsum(sizes)
    starts = ends - sizes
    first_blk = starts // ROW_BLOCK
    last_blk = (ends - 1) // ROW_BLOCK
    n_pairs = jnp.where(sizes > 0, last_blk - first_blk + 1, 0)
    pair_end = jnp.cumsum(n_pairs)
    pair_start = pair_end - n_pairs
    max_pairs = n_rows // ROW_BLOCK + N_EXPERTS
    p = jnp.arange(max_pairs)
    valid = p < pair_end[-1]
    pair_e = jnp.minimum(jnp.searchsorted(pair_end, p, side='right'), N_EXPERTS - 1).astype(jnp.int32)
    pair_blk = jnp.where(valid, first_blk[pair_e] + p - pair_start[pair_e], 0).astype(jnp.int32)

    def expert_block(out, inp):
        e, blk, ok = inp
        r0 = blk * ROW_BLOCK
        tok = lax.dynamic_slice(sorted_tok, (r0,), (ROW_BLOCK,))
        eid = lax.dynamic_slice(sorted_e, (r0,), (ROW_BLOCK,))
        wr = lax.dynamic_slice(sorted_w, (r0,), (ROW_BLOCK,))
        xb = xt[tok]
        hid = jax.nn.silu(xb @ w_gate[e]) * (xb @ w_up[e])
        yb = (hid @ w_down[e]).astype(f32)
        coef = jnp.where((eid == e) & ok, wr, 0.0)
        return out.at[tok].add(yb * coef[:, None]), None

    routed, _ = lax.scan(expert_block, jnp.zeros((n_tok, dm), f32), (pair_e, pair_blk, valid))
    shared = (jax.nn.silu(xt @ ws_gate) * (xt @ ws_up)) @ ws_down
    return (routed + shared.astype(f32)).reshape(bsz, seq, dm).astype(h.dtype)


def setup_inputs(seed: int = 0) -> dict:
    key = jax.random.key(seed)
    ks = jax.random.split(key, 32)
    f32 = jnp.float32
    L = DEPTH

    def nrm(k, shp, s):
        return jax.random.normal(k, shp, f32) * s

    n_idx = jnp.arange(S5_STATE, dtype=f32)
    return {
        'x': nrm(ks[0], (BATCH, SEQ, D_MODEL), 1.0),
        'c': nrm(ks[1], (BATCH, D_MODEL), 1.0),
        'w_ada': nrm(ks[2], (L, D_MODEL, 6 * D_MODEL), 0.02),
        'b_ada': nrm(ks[3], (L, 6 * D_MODEL), 0.01),
        'w_in': nrm(ks[4], (L, D_MODEL, PROJ_COLS), D_MODEL ** -0.5),
        'lb_logits': nrm(ks[5], (L + 1, HGRN_FDIM), 0.1),
        'hgrn_norm': 1.0 + nrm(ks[6], (L, HGRN_HEADS, HGRN_VDIM), 0.02),
        'lam_re': -0.5 + nrm(ks[7], (L, S5_GROUPS, S5_STATE), 0.01),
        'lam_im': jnp.pi * n_idx + nrm(ks[8], (L, S5_GROUPS, S5_STATE), 0.01),
        'log_dt': jax.random.uniform(ks[9], (L, S5_GROUPS), f32, math.log(DT_MIN), math.log(DT_MAX)),
        'b_re': nrm(ks[10], (L, S5_GROUPS, S5_STATE, S5_GROUP), (2 * S5_GROUP) ** -0.5),
        'b_im': nrm(ks[11], (L, S5_GROUPS, S5_STATE, S5_GROUP), (2 * S5_GROUP) ** -0.5),
        'c_re': nrm(ks[12], (L, S5_GROUPS, S5_GROUP, S5_STATE), S5_STATE ** -0.5),
        'c_im': nrm(ks[13], (L, S5_GROUPS, S5_GROUP, S5_STATE), S5_STATE ** -0.5),
        'd_skip': nrm(ks[14], (L, MIX_B), 1.0),
        'w_glu': nrm(ks[15], (L, MIX_B, MIX_B), MIX_B ** -0.5),
        'b_glu': nrm(ks[16], (L, MIX_B), 0.01),
        'w_out': nrm(ks[17], (L, MIX_A + MIX_B, D_MODEL), (MIX_A + MIX_B) ** -0.5),
        'w_router': nrm(ks[18], (L, D_MODEL, N_EXPERTS), D_MODEL ** -0.5),
        'router_bias': nrm(ks[19], (L, N_EXPERTS), 0.01),
        'w_gate': nrm(ks[20], (L, N_EXPERTS, D_MODEL, D_EXPERT), D_MODEL ** -0.5),
        'w_up': nrm(ks[21], (L, N_EXPERTS, D_MODEL, D_EXPERT), D_MODEL ** -0.5),
        'w_down': nrm(ks[22], (L, N_EXPERTS, D_EXPERT, D_MODEL), D_EXPERT ** -0.5),
        'ws_gate': nrm(ks[23], (L, D_MODEL, D_SHARED), D_MODEL ** -0.5),
        'ws_up': nrm(ks[24], (L, D_MODEL, D_SHARED), D_MODEL ** -0.5),
        'ws_down': nrm(ks[25], (L, D_SHARED, D_MODEL), D_SHARED ** -0.5),
        'final_gain': 1.0 + nrm(ks[26], (D_MODEL,), 0.02),
    }


def reference(x, c, w_ada, b_ada, w_in, lb_logits, hgrn_norm, lam_re, lam_im, log_dt,
              b_re, b_im, c_re, c_im, d_skip, w_glu, b_glu, w_out, w_router, router_bias,
              w_gate, w_up, w_down, ws_gate, ws_up, ws_down, final_gain):
    lower_bounds = jnp.cumsum(jax.nn.softmax(lb_logits.astype(jnp.float32), axis=0), axis=0)
    splits = [HGRN_FDIM, 2 * HGRN_FDIM, 2 * HGRN_FDIM + MIX_A, 2 * HGRN_FDIM + 2 * MIX_A]
    for l in range(DEPTH):
        mod = (jax.nn.silu(c) @ w_ada[l] + b_ada[l])[:, None, :]
        sh1, sc1, g1, sh2, sc2, g2 = jnp.split(mod, 6, axis=-1)
        h = rms_norm(x) * (1.0 + sc1) + sh1
        proj = h @ w_in[l]
        q, f_logit, i_val, o_gate, u = jnp.split(proj, splits, axis=-1)
        out_a = hgrn2_mixer(q, f_logit, i_val, o_gate, lower_bounds[l], hgrn_norm[l])
        out_b = s5_mixer(u, lam_re[l], lam_im[l], log_dt[l], b_re[l], b_im[l], c_re[l], c_im[l],
                         d_skip[l], w_glu[l], b_glu[l])
        mixed = jnp.concatenate([out_a, out_b], axis=-1).astype(x.dtype)
        x = x + g1 * (mixed @ w_out[l])
        h = rms_norm(x) * (1.0 + sc2) + sh2
        x = x + g2 * moe_ffn(h, w_router[l], router_bias[l], w_gate[l], w_up[l], w_down[l],
                             ws_gate[l], ws_up[l], ws_down[l])
    return rms_norm(x, final_gain)
```

```python
import functools

import jax
import jax.numpy as jnp
from jax import lax
from jax.experimental import pallas as pl
from jax.experimental.pallas import tpu as pltpu

F32 = jnp.float32
BF16 = jnp.bfloat16
I32 = jnp.int32

EPS = 1e-6
CHUNK = 64
HGRN_HEADS = 4
HGRN_KDIM = 128
S5_GROUP = 16
S5_STATE = 64
N_EXPERT_GROUPS = 8
TOPK_GROUPS = 4
TOP_K = 8
ROUTE_SCALE = 2.5

SEQ_TILE = 512
TOK_TILE = 512
ROUTE_TILE = 256
MOVE_TILE = 256
ROW_BLOCK = 512
VMEM_LIMIT = 56 * 1024 * 1024

_NT = (((1,), (1,)), ((), ()))
_TN = (((0,), (0,)), ((), ()))


def _sigmoid(v):
    return 1.0 / (1.0 + jnp.exp(-v))


def _silu(v):
    return v * _sigmoid(v)


def _bdot(a, b):
    return jnp.dot(a.astype(BF16), b.astype(BF16), preferred_element_type=F32)


def _mod_kernel(c_ref, w_ref, b_ref, o_ref):
    o_ref[...] = _bdot(_silu(c_ref[...]), w_ref[...]) + b_ref[...]


def _mod_call(c, w_ada, b_ada):
    bsz, d = c.shape
    n_out = w_ada.shape[1]
    return pl.pallas_call(
        _mod_kernel,
        out_shape=jax.ShapeDtypeStruct((bsz, n_out), F32),
        grid=(n_out // d,),
        in_specs=[pl.BlockSpec((bsz, d), lambda j: (0, 0)),
                  pl.BlockSpec((d, d), lambda j: (0, j)),
                  pl.BlockSpec((1, d), lambda j: (0, j))],
        out_specs=pl.BlockSpec((bsz, d), lambda j: (0, j)),
        compiler_params=pltpu.CompilerParams(vmem_limit_bytes=VMEM_LIMIT),
        name="mod",
    )(c, w_ada, b_ada.reshape(1, n_out))


def _mix_front_kernel(x_ref, mod_ref, win_ref, lb_ref, gn_ref, ltri_ref,
                      u_ref, oa_ref, proj_ref, st_ref):
    fdim = HGRN_HEADS * HGRN_KDIM
    ts = x_ref.shape[1]

    @pl.when(pl.program_id(1) == 0)
    def _():
        st_ref[...] = jnp.zeros_like(st_ref)

    x = x_ref[0]
    ms = jnp.mean(x * x, axis=-1, keepdims=True)
    h = x * lax.rsqrt(ms + EPS) * (1.0 + mod_ref[0, 1:2, :]) + mod_ref[0, 0:1, :]
    proj_ref[...] = jnp.dot(h.astype(BF16), win_ref[...], preferred_element_type=F32)
    u_ref[0] = proj_ref[:, 4 * fdim:5 * fdim]

    lb = lb_ref[...]
    gn = gn_ref[...]
    ltri = ltri_ref[...]
    row = lax.broadcasted_iota(I32, (CHUNK, CHUNK), 0)
    col = lax.broadcasted_iota(I32, (CHUNK, CHUNK), 1)
    causal = row >= col

    def chunk_step(ci, carry):
        r0 = pl.multiple_of(ci * CHUNK, CHUNK)
        q = proj_ref[pl.ds(r0, CHUNK), 0:fdim]
        fl = proj_ref[pl.ds(r0, CHUNK), fdim:2 * fdim]
        iv = proj_ref[pl.ds(r0, CHUNK), 2 * fdim:3 * fdim]
        og = proj_ref[pl.ds(r0, CHUNK), 3 * fdim:4 * fdim]
        f = lb + (1.0 - lb) * _sigmoid(fl)
        b = jnp.dot(ltri, jnp.log(f), precision=lax.Precision.HIGHEST,
                    preferred_element_type=F32)
        b_ref = b[CHUNK // 2 - 1:CHUNK // 2, :]
        b_last = b[CHUNK - 1:CHUNK, :]
        qs = _silu(q)
        kk = 1.0 - f
        qe = (qs * jnp.exp(b - b_ref)).astype(BF16)
        ke = (kk * jnp.exp(b_ref - b)).astype(BF16)
        qb = (qs * jnp.exp(b)).astype(BF16)
        k2 = (kk * jnp.exp(b_last - b)).astype(BF16)
        dec = jnp.exp(b_last)
        ivb = iv.astype(BF16)
        outs = []
        for hh in range(HGRN_HEADS):
            sl = slice(hh * HGRN_KDIM, (hh + 1) * HGRN_KDIM)
            att = lax.dot_general(qe[:, sl], ke[:, sl], _NT, preferred_element_type=F32)
            att = jnp.where(causal, att, 0.0)
            st = st_ref[hh]
            o = jnp.dot(att.astype(BF16), ivb[:, sl], preferred_element_type=F32)
            o = o + lax.dot_general(qb[:, sl], st.astype(BF16), _NT, preferred_element_type=F32)
            st_ref[hh] = st * dec[:, sl] + lax.dot_general(
                ivb[:, sl], k2[:, sl], _TN, preferred_element_type=F32)
            outs.append(o * lax.rsqrt(jnp.mean(o * o, axis=-1, keepdims=True) + EPS))
        o = jnp.concatenate(outs, axis=1) * gn * _silu(og)
        oa_ref[0, pl.ds(r0, CHUNK), :] = o.astype(BF16)
        return carry

    lax.fori_loop(0, ts // CHUNK, chunk_step, 0)


def _mix_front_call(x, mod, w_in_b, lb, gn):
    bsz, seq, d = x.shape
    fdim = HGRN_HEADS * HGRN_KDIM
    ncols = w_in_b.shape[1]
    ltri = jnp.tril(jnp.ones((CHUNK, CHUNK), F32))
    ts = SEQ_TILE
    return pl.pallas_call(
        _mix_front_kernel,
        out_shape=(jax.ShapeDtypeStruct((bsz, seq, ncols - 4 * fdim), F32),
                   jax.ShapeDtypeStruct((bsz, seq, fdim), BF16)),
        grid=(bsz, seq // ts),
        in_specs=[pl.BlockSpec((1, ts, d), lambda b, j: (b, j, 0)),
                  pl.BlockSpec((1, 6, d), lambda b, j: (b, 0, 0)),
                  pl.BlockSpec((d, ncols), lambda b, j: (0, 0)),
                  pl.BlockSpec((1, fdim), lambda b, j: (0, 0)),
                  pl.BlockSpec((1, fdim), lambda b, j: (0, 0)),
                  pl.BlockSpec((CHUNK, CHUNK), lambda b, j: (0, 0))],
        out_specs=(pl.BlockSpec((1, ts, ncols - 4 * fdim), lambda b, j: (b, j, 0)),
                   pl.BlockSpec((1, ts, fdim), lambda b, j: (b, j, 0))),
        scratch_shapes=[pltpu.VMEM((ts, ncols), F32),
                        pltpu.VMEM((HGRN_HEADS, fdim // HGRN_HEADS, HGRN_KDIM), F32)],
        compiler_params=pltpu.CompilerParams(
            dimension_semantics=("arbitrary", "arbitrary"), vmem_limit_bytes=VMEM_LIMIT),
        name="mix_front",
    )(x, mod, w_in_b, lb, gn, ltri)


def _s5_tables(lam_re, lam_im, log_dt, b_re, b_im, c_re, c_im):
    t = CHUNK
    hp = lax.Precision.HIGHEST
    lam = lax.complex(jnp.minimum(lam_re, -1e-4), lam_im)
    lam_dt = lam * jnp.exp(log_dt)[:, None]
    lam_bar = jnp.exp(lam_dt)
    b_bar = ((lam_bar - 1.0) / lam)[..., None] * lax.complex(b_re, b_im)
    c_mat = lax.complex(c_re, c_im)
    taus = jnp.arange(t + 1, dtype=F32)
    lam_pow = jnp.exp(lam_dt[:, None, :] * taus[None, :, None])
    g, p = lam.shape
    c = b_re.shape[-1]
    cl = c_mat[:, None, :, :] * lam_pow[:, :t, None, :]
    kr = (jnp.einsum('gtcp,gpi->gtci', cl.real, b_bar.real, precision=hp)
          - jnp.einsum('gtcp,gpi->gtci', cl.imag, b_bar.imag, precision=hp))
    s_idx = jnp.arange(t)
    d = s_idx[None, :] - s_idx[:, None]
    kg = kr[:, jnp.clip(d, 0, t - 1)]
    kg = jnp.where((d >= 0)[None, :, :, None, None], kg, 0.0)
    m = kg.transpose(0, 1, 4, 2, 3).reshape(g, t * c, t * c)
    pc = lam_pow[:, t - 1::-1][:, :t, :, None] * b_bar[:, None, :, :]
    pc = pc.transpose(0, 1, 3, 2).reshape(g, t * c, p)
    p_tab = jnp.concatenate([pc.real, pc.imag], axis=-1)
    ql = c_mat[:, None, :, :] * lam_pow[:, 1:t + 1, None, :]
    ql = ql.transpose(0, 3, 1, 2).reshape(g, p, t * c)
    q_tab = jnp.concatenate([ql.real, -ql.imag], axis=1)
    lam_t = lam_pow[:, t]
    a1 = jnp.concatenate([lam_t.real, lam_t.real], axis=-1)[:, None, :]
    a2 = jnp.concatenate([-lam_t.imag, lam_t.imag], axis=-1)[:, None, :]
    return m.astype(BF16), p_tab.astype(BF16), q_tab.astype(BF16), a1, a2


def _s5_kernel(u_ref, m_ref, p_ref, q_ref, a1_ref, a2_ref, y_ref, v_ref, xs_ref, *, n_chunks, n_batch):
    u = u_ref[0]
    v_ref[...] = jnp.dot(u, p_ref[0], preferred_element_type=F32)
    a1 = a1_ref[0]
    a2 = a2_ref[0]
    half = xs_ref.shape[1] // 2

    def step(n, state):
        r0 = pl.multiple_of(n * n_batch, n_batch)
        xs_ref[pl.ds(r0, n_batch), :] = state
        return a1 * state + a2 * pltpu.roll(state, half, 1) + v_ref[pl.ds(r0, n_batch), :]

    lax.fori_loop(0, n_chunks, step, jnp.zeros((n_batch, xs_ref.shape[1]), F32))
    y = jnp.dot(u, m_ref[0], preferred_element_type=F32)
    y = y + jnp.dot(xs_ref[...].astype(BF16), q_ref[0], preferred_element_type=F32)
    y_ref[0] = y.astype(BF16)


def _s5_call(ut, m, p_tab, q_tab, a1, a2, n_chunks, n_batch):
    g, rows, width = ut.shape
    p2 = p_tab.shape[-1]
    return pl.pallas_call(
        functools.partial(_s5_kernel, n_chunks=n_chunks, n_batch=n_batch),
        out_shape=jax.ShapeDtypeStruct((g, rows, width), BF16),
        grid=(g,),
        in_specs=[pl.BlockSpec((1, rows, width), lambda i: (i, 0, 0)),
                  pl.BlockSpec((1, width, width), lambda i: (i, 0, 0)),
                  pl.BlockSpec((1, width, p2), lambda i: (i, 0, 0)),
                  pl.BlockSpec((1, p2, width), lambda i: (i, 0, 0)),
                  pl.BlockSpec((1, 1, p2), lambda i: (i, 0, 0)),
                  pl.BlockSpec((1, 1, p2), lambda i: (i, 0, 0))],
        out_specs=pl.BlockSpec((1, rows, width), lambda i: (i, 0, 0)),
        scratch_shapes=[pltpu.VMEM((rows, p2), F32), pltpu.VMEM((rows, p2), F32)],
        compiler_params=pltpu.CompilerParams(
            dimension_semantics=("arbitrary",), vmem_limit_bytes=VMEM_LIMIT),
        name="s5",
    )(ut, m, p_tab, q_tab, a1, a2)


def _mix_back_kernel(x_ref, oa_ref, ys_ref, u_ref, mod_ref, dskip_ref, wglu_ref, bglu_ref,
                     wout_ref, wrh_ref, wrl_ref, wsg_ref, wsu_ref, wsd_ref,
                     x1_ref, h2_ref, lt_ref):
    na = oa_ref.shape[1]
    y = ys_ref[...].astype(F32) + dskip_ref[...] * u_ref[...]
    y = jax.nn.gelu(y)
    ob = y * _sigmoid(_bdot(y, wglu_ref[...]) + bglu_ref[...])
    mixed = (jnp.dot(oa_ref[...], wout_ref[0:na, :], preferred_element_type=F32)
             + jnp.dot(ob.astype(BF16), wout_ref[na:, :], preferred_element_type=F32))
    x1 = x_ref[...] + mod_ref[0, 2:3, :] * mixed
    ms = jnp.mean(x1 * x1, axis=-1, keepdims=True)
    h2 = x1 * lax.rsqrt(ms + EPS) * (1.0 + mod_ref[0, 4:5, :]) + mod_ref[0, 3:4, :]
    h2_ref[...] = h2
    h_hi = h2.astype(BF16)
    h_lo = (h2 - h_hi.astype(F32)).astype(BF16)
    lt = lax.dot_general(wrh_ref[...], h_hi, _NT, preferred_element_type=F32)
    lt = lt + lax.dot_general(wrl_ref[...], h_hi, _NT, preferred_element_type=F32)
    lt = lt + lax.dot_general(wrh_ref[...], h_lo, _NT, preferred_element_type=F32)
    lt_ref[...] = lt
    hid = _silu(jnp.dot(h_hi, wsg_ref[...], preferred_element_type=F32)) * jnp.dot(
        h_hi, wsu_ref[...], preferred_element_type=F32)
    shared = jnp.dot(hid.astype(BF16), wsd_ref[...], preferred_element_type=F32)
    x1_ref[...] = x1 + mod_ref[0, 5:6, :] * shared


def _mix_back_call(x2d, oa, ys, u, mod, dskip, wglu, bglu, wout, wrh, wrl, wsg, wsu, wsd, seq):
    n, d = x2d.shape
    nb = oa.shape[1]
    ne = wrh.shape[0]
    dsh = wsg.shape[1]
    tm = TOK_TILE
    per_b = seq // tm
    const = lambda i: (0, 0)
    return pl.pallas_call(
        _mix_back_kernel,
        out_shape=(jax.ShapeDtypeStruct((n, d), F32),
                   jax.ShapeDtypeStruct((n, d), F32),
                   jax.ShapeDtypeStruct((ne, n), F32)),
        grid=(n // tm,),
        in_specs=[pl.BlockSpec((tm, d), lambda i: (i, 0)),
                  pl.BlockSpec((tm, nb), lambda i: (i, 0)),
                  pl.BlockSpec((tm, nb), lambda i: (i, 0)),
                  pl.BlockSpec((tm, nb), lambda i: (i, 0)),
                  pl.BlockSpec((1, 6, d), lambda i: (i // per_b, 0, 0)),
                  pl.BlockSpec((1, nb), const),
                  pl.BlockSpec((nb, nb), const),
                  pl.BlockSpec((1, nb), const),
                  pl.BlockSpec((d, d), const),
                  pl.BlockSpec((ne, d), const),
                  pl.BlockSpec((ne, d), const),
                  pl.BlockSpec((d, dsh), const),
                  pl.BlockSpec((d, dsh), const),
                  pl.BlockSpec((dsh, d), const)],
        out_specs=(pl.BlockSpec((tm, d), lambda i: (i, 0)),
                   pl.BlockSpec((tm, d), lambda i: (i, 0)),
                   pl.BlockSpec((ne, tm), lambda i: (0, i))),
        compiler_params=pltpu.CompilerParams(
            dimension_semantics=("arbitrary",), vmem_limit_bytes=VMEM_LIMIT),
        name="mix_back",
    )(x2d, oa, ys, u, mod, dskip, wglu, bglu, wout, wrh, wrl, wsg, wsu, wsd)


def _route_kernel(lt_ref, bias_ref, su_ref, idx_ref, w_ref, rank_ref, cnt_ref, run_ref):
    ne, tr = lt_ref.shape
    per_group = ne // N_EXPERT_GROUPS
    neg = -jnp.inf

    @pl.when(pl.program_id(0) == 0)
    def _():
        run_ref[...] = jnp.zeros_like(run_ref)

    s = _sigmoid(lt_ref[...])
    sel = s + bias_ref[...]
    gio = lax.broadcasted_iota(I32, (per_group, tr), 0)
    gscore = []
    for g in range(N_EXPERT_GROUPS):
        v = sel[g * per_group:(g + 1) * per_group, :]
        m1 = jnp.max(v, axis=0, keepdims=True)
        i1 = jnp.min(jnp.where(v == m1, gio, per_group), axis=0, keepdims=True)
        m2 = jnp.max(jnp.where(gio == i1, neg, v), axis=0, keepdims=True)
        gscore.append(m1 + m2)
    masked = []
    for g in range(N_EXPERT_GROUPS):
        ahead = jnp.zeros((1, tr), I32)
        for o in range(N_EXPERT_GROUPS):
            if o == g:
                continue
            wins = (gscore[o] >= gscore[g]) if o < g else (gscore[o] > gscore[g])
            ahead = ahead + wins.astype(I32)
        keep = ahead < TOPK_GROUPS
        masked.append(jnp.where(keep, sel[g * per_group:(g + 1) * per_group, :], neg))
    selm = jnp.concatenate(masked, axis=0)
    eio = lax.broadcasted_iota(I32, (ne, tr), 0)
    hits = jnp.zeros((ne, tr), F32)
    idxs, ws = [], []
    for k in range(TOP_K):
        m = jnp.max(selm, axis=0, keepdims=True)
        ik = jnp.min(jnp.where(selm == m, eio, ne), axis=0, keepdims=True)
        onehot = eio == ik
        ws.append(jnp.sum(jnp.where(onehot, s, 0.0), axis=0, keepdims=True))
        hits = jnp.where(onehot, 1.0, hits)
        selm = jnp.where(onehot, neg, selm)
        idxs.append(ik)
    wsum = ws[0]
    for k in range(1, TOP_K):
        wsum = wsum + ws[k]
    scale = ROUTE_SCALE / wsum
    ranks = jnp.dot(hits.astype(BF16), su_ref[...], preferred_element_type=F32) + run_ref[...]
    for k in range(TOP_K):
        idx_ref[k:k + 1, :] = idxs[k]
        w_ref[k:k + 1, :] = ws[k] * scale
        rk = jnp.sum(jnp.where(eio == idxs[k], ranks, 0.0), axis=0, keepdims=True)
        rank_ref[k:k + 1, :] = rk.astype(I32)
    run_ref[...] = run_ref[...] + jnp.sum(hits, axis=1, keepdims=True)
    cnt_ref[...] = run_ref[...]


def _route_call(lt, bias):
    ne, n = lt.shape
    tr = ROUTE_TILE
    su = jnp.triu(jnp.ones((tr, tr), F32), k=1).astype(BF16)
    return pl.pallas_call(
        _route_kernel,
        out_shape=(jax.ShapeDtypeStruct((TOP_K, n), I32),
                   jax.ShapeDtypeStruct((TOP_K, n), F32),
                   jax.ShapeDtypeStruct((TOP_K, n), I32),
                   jax.ShapeDtypeStruct((ne, 1), F32)),
        grid=(n // tr,),
        in_specs=[pl.BlockSpec((ne, tr), lambda i: (0, i)),
                  pl.BlockSpec((ne, 1), lambda i: (0, 0)),
                  pl.BlockSpec((tr, tr), lambda i: (0, 0))],
        out_specs=(pl.BlockSpec((TOP_K, tr), lambda i: (0, i)),
                   pl.BlockSpec((TOP_K, tr), lambda i: (0, i)),
                   pl.BlockSpec((TOP_K, tr), lambda i: (0, i)),
                   pl.BlockSpec((ne, 1), lambda i: (0, 0))),
        scratch_shapes=[pltpu.VMEM((ne, 1), F32)],
        compiler_params=pltpu.CompilerParams(
            dimension_semantics=("arbitrary",), vmem_limit_bytes=VMEM_LIMIT),
        name="route",
    )(lt, bias.reshape(ne, 1), su)


def _dispatch_kernel(pos_ref, h_ref, xs_ref, sem):
    tm = h_ref.shape[0]

    def row_copy(t, k):
        return pltpu.make_async_copy(h_ref.at[pl.ds(t, 1)], xs_ref.at[pl.ds(pos_ref[k, t], 1)], sem)

    def issue(t, carry):
        for k in range(TOP_K):
            row_copy(t, k).start()
        return carry

    lax.fori_loop(0, tm, issue, 0)
    for k in range(TOP_K):
        pltpu.make_async_copy(h_ref, xs_ref.at[pl.ds(0, tm)], sem).wait()


def _dispatch_call(pos, h2):
    n, d = h2.shape
    tm = MOVE_TILE
    return pl.pallas_call(
        _dispatch_kernel,
        out_shape=jax.ShapeDtypeStruct((n * TOP_K, d), F32),
        grid=(n // tm,),
        in_specs=[pl.BlockSpec((TOP_K, tm), lambda i: (0, i), memory_space=pltpu.SMEM),
                  pl.BlockSpec((tm, d), lambda i: (i, 0))],
        out_specs=pl.BlockSpec(memory_space=pl.ANY),
        scratch_shapes=[pltpu.SemaphoreType.DMA],
        compiler_params=pltpu.CompilerParams(
            dimension_semantics=("arbitrary",), vmem_limit_bytes=VMEM_LIMIT),
        name="dispatch",
    )(pos, h2)


def _experts_kernel(pe_ref, pb_ref, pv_ref, st_ref, en_ref,
                    xs_ref, wg_ref, wu_ref, wd_ref, y_ref):
    p = pl.program_id(0)
    e = pe_ref[p]
    blk = pb_ref[p]
    rb = xs_ref.shape[0]
    first = jnp.logical_or(p == 0, pb_ref[jnp.maximum(p - 1, 0)] != blk)

    @pl.when(first)
    def _():
        y_ref[...] = jnp.zeros_like(y_ref)

    @pl.when(pv_ref[p] == 1)
    def _():
        xb = xs_ref[...].astype(BF16)
        gate = jnp.dot(xb, wg_ref[0].astype(BF16), preferred_element_type=F32)
        up = jnp.dot(xb, wu_ref[0].astype(BF16), preferred_element_type=F32)
        hid = (_silu(gate) * up).astype(BF16)
        yb = jnp.dot(hid, wd_ref[0].astype(BF16), preferred_element_type=F32)
        rows = blk * rb + lax.broadcasted_iota(I32, (rb, 1), 0)
        mine = jnp.logical_and(rows >= st_ref[e], rows < en_ref[e])
        y_ref[...] += jnp.where(mine, yb, 0.0)


def _experts_call(pair_e, pair_blk, pair_ok, starts, ends, xs, w_gate, w_up, w_down):
    rows, d = xs.shape
    ne, _, de = w_gate.shape
    rb = ROW_BLOCK
    n_pairs = pair_e.shape[0]
    grid_spec = pltpu.PrefetchScalarGridSpec(
        num_scalar_prefetch=5,
        grid=(n_pairs,),
        in_specs=[pl.BlockSpec((rb, d), lambda p, pe, pb, pv, st, en: (pb[p], 0)),
                  pl.BlockSpec((1, d, de), lambda p, pe, pb, pv, st, en: (pe[p], 0, 0)),
                  pl.BlockSpec((1, d, de), lambda p, pe, pb, pv, st, en: (pe[p], 0, 0)),
                  pl.BlockSpec((1, de, d), lambda p, pe, pb, pv, st, en: (pe[p], 0, 0))],
        out_specs=pl.BlockSpec((rb, d), lambda p, pe, pb, pv, st, en: (pb[p], 0)),
    )
    return pl.pallas_call(
        _experts_kernel,
        out_shape=jax.ShapeDtypeStruct((rows, d), F32),
        grid_spec=grid_spec,
        compiler_params=pltpu.CompilerParams(
            dimension_semantics=("arbitrary",), vmem_limit_bytes=VMEM_LIMIT),
        name="experts",
    )(pair_e, pair_blk, pair_ok, starts, ends, xs, w_gate, w_up, w_down)


def _combine_kernel(pos_ref, ys_ref, x1_ref, w_ref, g2_ref, fg_ref, o_ref, buf_ref, sem):
    tm = x1_ref.shape[0]

    def issue(t, carry):
        for k in range(TOP_K):
            pltpu.make_async_copy(ys_ref.at[pl.ds(pos_ref[k, t], 1)],
                                  buf_ref.at[k, pl.ds(t, 1)], sem).start()
        return carry

    lax.fori_loop(0, tm, issue, 0)
    for k in range(TOP_K):
        pltpu.make_async_copy(ys_ref.at[pl.ds(0, tm)], buf_ref.at[k], sem).wait()
    w = w_ref[...]
    routed = buf_ref[0] * w[:, 0:1]
    for k in range(1, TOP_K):
        routed = routed + buf_ref[k] * w[:, k:k + 1]
    x2 = x1_ref[...] + g2_ref[0] * routed
    ms = jnp.mean(x2 * x2, axis=-1, keepdims=True)
    o_ref[...] = x2 * lax.rsqrt(ms + EPS) * fg_ref[...]


def _combine_call(pos, ysorted, x1s, w_tok, g2, fgain, seq):
    n, d = x1s.shape
    tm = MOVE_TILE
    per_b = seq // tm
    return pl.pallas_call(
        _combine_kernel,
        out_shape=jax.ShapeDtypeStruct((n, d), F32),
        grid=(n // tm,),
        in_specs=[pl.BlockSpec((TOP_K, tm), lambda i: (0, i), memory_space=pltpu.SMEM),
                  pl.BlockSpec(memory_space=pl.ANY),
                  pl.BlockSpec((tm, d), lambda i: (i, 0)),
                  pl.BlockSpec((tm, TOP_K), lambda i: (i, 0)),
                  pl.BlockSpec((1, 1, d), lambda i: (i // per_b, 0, 0)),
                  pl.BlockSpec((1, d), lambda i: (0, 0))],
        out_specs=pl.BlockSpec((tm, d), lambda i: (i, 0)),
        scratch_shapes=[pltpu.VMEM((TOP_K, tm, d), F32), pltpu.SemaphoreType.DMA],
        compiler_params=pltpu.CompilerParams(
            dimension_semantics=("arbitrary",), vmem_limit_bytes=VMEM_LIMIT),
        name="combine",
    )(pos, ysorted, x1s, w_tok, g2, fgain)


def _pair_tables(counts, n_rows):
    ne = counts.shape[0]
    sizes = counts.astype(I32)
    ends = jnp.cumsum(sizes)
    starts = ends - sizes
    first_blk = starts // ROW_BLOCK
    last_blk = (ends - 1) // ROW_BLOCK
    n_pairs = jnp.where(sizes > 0, last_blk - first_blk + 1, 0)
    pair_end = jnp.cumsum(n_pairs)
    pair_start = pair_end - n_pairs
    max_pairs = n_rows // ROW_BLOCK + ne
    p = jnp.arange(max_pairs, dtype=I32)
    ok = p < pair_end[-1]
    pc = jnp.minimum(p, pair_end[-1] - 1)
    pair_e = jnp.minimum(jnp.searchsorted(pair_end, pc, side='right'), ne - 1).astype(I32)
    pair_blk = (first_blk[pair_e] + pc - pair_start[pair_e]).astype(I32)
    return pair_e, pair_blk, ok.astype(I32), starts.astype(I32), ends.astype(I32)


def kernel(x, c, w_ada, b_ada, w_in, lb_logits, hgrn_norm, lam_re, lam_im, log_dt, b_re, b_im,
           c_re, c_im, d_skip, w_glu, b_glu, w_out, w_router, router_bias, w_gate, w_up, w_down,
           ws_gate, ws_up, ws_down, final_gain):
    bsz, seq, d = x.shape
    n = bsz * seq
    fdim = HGRN_HEADS * HGRN_KDIM
    n_chunks = seq // CHUNK
    lb = jnp.cumsum(jax.nn.softmax(lb_logits.astype(F32), axis=0), axis=0)[0].reshape(1, fdim)

    mod = _mod_call(c, w_ada[0], b_ada[0]).reshape(bsz, 6, d)
    u, out_a = _mix_front_call(x, mod, w_in[0].astype(BF16), lb, hgrn_norm[0].reshape(1, fdim))

    nb = u.shape[-1]
    groups = nb // S5_GROUP
    m_tab, p_tab, q_tab, a1, a2 = _s5_tables(lam_re[0], lam_im[0], log_dt[0], b_re[0], b_im[0],
                                             c_re[0], c_im[0])
    ut = u.reshape(bsz, n_chunks, CHUNK, groups, S5_GROUP).transpose(3, 1, 0, 2, 4)
    ut = ut.reshape(groups, n_chunks * bsz, CHUNK * S5_GROUP).astype(BF16)
    yt = _s5_call(ut, m_tab, p_tab, q_tab, a1, a2, n_chunks, bsz)
    ys = yt.reshape(groups, n_chunks, bsz, CHUNK, S5_GROUP).transpose(2, 1, 3, 0, 4).reshape(n, nb)

    wr_t = w_router[0].T
    wr_hi = wr_t.astype(BF16)
    wr_lo = (wr_t - wr_hi.astype(F32)).astype(BF16)
    x1s, h2, logits_t = _mix_back_call(
        x.reshape(n, d), out_a.reshape(n, fdim), ys, u.reshape(n, nb), mod,
        d_skip[0].reshape(1, nb), w_glu[0].astype(BF16), b_glu[0].reshape(1, nb),
        w_out[0].astype(BF16), wr_hi, wr_lo, ws_gate[0].astype(BF16), ws_up[0].astype(BF16),
        ws_down[0].astype(BF16), seq)

    top_idx, top_w, rank, counts = _route_call(logits_t, router_bias[0])
    pair_e, pair_blk, pair_ok, starts, ends = _pair_tables(counts[:, 0], n * TOP_K)
    pos = starts[top_idx] + rank

    xs = _dispatch_call(pos, h2)
    y_sorted = _experts_call(pair_e, pair_blk, pair_ok, starts, ends, xs,
                             w_gate[0], w_up[0], w_down[0])
    out = _combine_call(pos, y_sorted, x1s, top_w.T, mod[:, 5:6, :], final_gain.reshape(1, d), seq)
    return out.reshape(bsz, seq, d)
```

```python
import functools

import jax
import jax.numpy as jnp
from jax import lax
from jax.experimental import pallas as pl
from jax.experimental.pallas import tpu as pltpu

F32 = jnp.float32
BF16 = jnp.bfloat16
I32 = jnp.int32

EPS = 1e-6
CHUNK = 64
HGRN_HEADS = 4
HGRN_KDIM = 128
S5_GROUP = 16
S5_STATE = 64
N_EXPERT_GROUPS = 8
TOPK_GROUPS = 4
TOP_K = 8
ROUTE_SCALE = 2.5

SEQ_TILE = 512
TOK_TILE = 512
ROUTE_TILE = 256
POS_TILE = 1024
MOVE_TILE = 256
ROW_BLOCK = 512
VMEM_LIMIT = 56 * 1024 * 1024

_NT = (((1,), (1,)), ((), ()))
_TN = (((0,), (0,)), ((), ()))


def _sigmoid(v):
    return 1.0 / (1.0 + jnp.exp(-v))


def _silu(v):
    return v * _sigmoid(v)


def _bdot(a, b):
    return jnp.dot(a.astype(BF16), b.astype(BF16), preferred_element_type=F32)


def _mod_kernel(c_ref, w_ref, b_ref, o_ref):
    o_ref[...] = _bdot(_silu(c_ref[...]), w_ref[...]) + b_ref[...]


def _mod_call(c, w_ada, b_ada):
    bsz, d = c.shape
    n_out = w_ada.shape[1]
    return pl.pallas_call(
        _mod_kernel,
        out_shape=jax.ShapeDtypeStruct((bsz, n_out), F32),
        grid=(n_out // d,),
        in_specs=[pl.BlockSpec((bsz, d), lambda j: (0, 0)),
                  pl.BlockSpec((d, d), lambda j: (0, j)),
                  pl.BlockSpec((1, d), lambda j: (0, j))],
        out_specs=pl.BlockSpec((bsz, d), lambda j: (0, j)),
        compiler_params=pltpu.CompilerParams(vmem_limit_bytes=VMEM_LIMIT),
        name="mod",
    )(c, w_ada, b_ada.reshape(1, n_out))


def _mix_front_kernel(x_ref, mod_ref, win_ref, lb_ref, gn_ref, ltri_ref,
                      u_ref, oa_ref, proj_ref, st_ref):
    fdim = HGRN_HEADS * HGRN_KDIM
    ts = x_ref.shape[1]

    @pl.when(pl.program_id(1) == 0)
    def _():
        st_ref[...] = jnp.zeros_like(st_ref)

    x = x_ref[0]
    ms = jnp.mean(x * x, axis=-1, keepdims=True)
    h = x * lax.rsqrt(ms + EPS) * (1.0 + mod_ref[0, 1:2, :]) + mod_ref[0, 0:1, :]
    proj_ref[...] = jnp.dot(h.astype(BF16), win_ref[...], preferred_element_type=F32)
    u_ref[0] = proj_ref[:, 4 * fdim:5 * fdim]

    lb = lb_ref[...]
    gn = gn_ref[...]
    ltri = ltri_ref[...]
    row = lax.broadcasted_iota(I32, (CHUNK, CHUNK), 0)
    col = lax.broadcasted_iota(I32, (CHUNK, CHUNK), 1)
    causal = row >= col

    def chunk_step(ci, carry):
        r0 = pl.multiple_of(ci * CHUNK, CHUNK)
        q = proj_ref[pl.ds(r0, CHUNK), 0:fdim]
        fl = proj_ref[pl.ds(r0, CHUNK), fdim:2 * fdim]
        iv = proj_ref[pl.ds(r0, CHUNK), 2 * fdim:3 * fdim]
        og = proj_ref[pl.ds(r0, CHUNK), 3 * fdim:4 * fdim]
        f = lb + (1.0 - lb) * _sigmoid(fl)
        b = jnp.dot(ltri, jnp.log(f), precision=lax.Precision.HIGHEST,
                    preferred_element_type=F32)
        b_ref = b[CHUNK // 2 - 1:CHUNK // 2, :]
        b_last = b[CHUNK - 1:CHUNK, :]
        qs = _silu(q)
        kk = 1.0 - f
        qe = (qs * jnp.exp(b - b_ref)).astype(BF16)
        ke = (kk * jnp.exp(b_ref - b)).astype(BF16)
        qb = (qs * jnp.exp(b)).astype(BF16)
        k2 = (kk * jnp.exp(b_last - b)).astype(BF16)
        dec = jnp.exp(b_last)
        ivb = iv.astype(BF16)
        outs = []
        for hh in range(HGRN_HEADS):
            sl = slice(hh * HGRN_KDIM, (hh + 1) * HGRN_KDIM)
            att = lax.dot_general(qe[:, sl], ke[:, sl], _NT, preferred_element_type=F32)
            att = jnp.where(causal, att, 0.0)
            st = st_ref[hh]
            o = jnp.dot(att.astype(BF16), ivb[:, sl], preferred_element_type=F32)
            o = o + lax.dot_general(qb[:, sl], st.astype(BF16), _NT, preferred_element_type=F32)
            st_ref[hh] = st * dec[:, sl] + lax.dot_general(
                ivb[:, sl], k2[:, sl], _TN, preferred_element_type=F32)
            outs.append(o * lax.rsqrt(jnp.mean(o * o, axis=-1, keepdims=True) + EPS))
        o = jnp.concatenate(outs, axis=1) * gn * _silu(og)
        oa_ref[0, pl.ds(r0, CHUNK), :] = o.astype(BF16)
        return carry

    lax.fori_loop(0, ts // CHUNK, chunk_step, 0)


def _mix_front_call(x, mod, w_in_b, lb, gn):
    bsz, seq, d = x.shape
    fdim = HGRN_HEADS * HGRN_KDIM
    ncols = w_in_b.shape[1]
    ltri = jnp.tril(jnp.ones((CHUNK, CHUNK), F32))
    ts = SEQ_TILE
    return pl.pallas_call(
        _mix_front_kernel,
        out_shape=(jax.ShapeDtypeStruct((bsz, seq, ncols - 4 * fdim), F32),
                   jax.ShapeDtypeStruct((bsz, seq, fdim), BF16)),
        grid=(bsz, seq // ts),
        in_specs=[pl.BlockSpec((1, ts, d), lambda b, j: (b, j, 0)),
                  pl.BlockSpec((1, 6, d), lambda b, j: (b, 0, 0)),
                  pl.BlockSpec((d, ncols), lambda b, j: (0, 0)),
                  pl.BlockSpec((1, fdim), lambda b, j: (0, 0)),
                  pl.BlockSpec((1, fdim), lambda b, j: (0, 0)),
                  pl.BlockSpec((CHUNK, CHUNK), lambda b, j: (0, 0))],
        out_specs=(pl.BlockSpec((1, ts, ncols - 4 * fdim), lambda b, j: (b, j, 0)),
                   pl.BlockSpec((1, ts, fdim), lambda b, j: (b, j, 0))),
        scratch_shapes=[pltpu.VMEM((ts, ncols), F32),
                        pltpu.VMEM((HGRN_HEADS, fdim // HGRN_HEADS, HGRN_KDIM), F32)],
        compiler_params=pltpu.CompilerParams(
            dimension_semantics=("arbitrary", "arbitrary"), vmem_limit_bytes=VMEM_LIMIT),
        name="mix_front",
    )(x, mod, w_in_b, lb, gn, ltri)


def _s5_tables(lam_re, lam_im, log_dt, b_re, b_im, c_re, c_im):
    t = CHUNK
    hp = lax.Precision.HIGHEST
    lam = lax.complex(jnp.minimum(lam_re, -1e-4), lam_im)
    lam_dt = lam * jnp.exp(log_dt)[:, None]
    lam_bar = jnp.exp(lam_dt)
    b_bar = ((lam_bar - 1.0) / lam)[..., None] * lax.complex(b_re, b_im)
    c_mat = lax.complex(c_re, c_im)
    taus = jnp.arange(t + 1, dtype=F32)
    lam_pow = jnp.exp(lam_dt[:, None, :] * taus[None, :, None])
    g, p = lam.shape
    c = b_re.shape[-1]
    cl = c_mat[:, None, :, :] * lam_pow[:, :t, None, :]
    kr = (jnp.einsum('gtcp,gpi->gtci', cl.real, b_bar.real, precision=hp)
          - jnp.einsum('gtcp,gpi->gtci', cl.imag, b_bar.imag, precision=hp))
    z = jnp.pad(kr.transpose(0, 3, 2, 1).astype(BF16), ((0, 0), (0, 0), (0, 0), (t - 1, 0)))
    kg = jnp.stack([z[..., t - 1 - s:2 * t - 1 - s] for s in range(t)], axis=1)
    m = kg.transpose(0, 1, 2, 4, 3).reshape(g, t * c, t * c)
    pc = lam_pow[:, t - 1::-1][:, :t, :, None] * b_bar[:, None, :, :]
    pc = pc.transpose(0, 1, 3, 2).reshape(g, t * c, p)
    p_tab = jnp.concatenate([pc.real, pc.imag], axis=-1)
    ql = c_mat[:, None, :, :] * lam_pow[:, 1:t + 1, None, :]
    ql = ql.transpose(0, 3, 1, 2).reshape(g, p, t * c)
    q_tab = jnp.concatenate([ql.real, -ql.imag], axis=1)
    lam_t = lam_pow[:, t]
    a1 = jnp.concatenate([lam_t.real, lam_t.real], axis=-1)[:, None, :]
    a2 = jnp.concatenate([-lam_t.imag, lam_t.imag], axis=-1)[:, None, :]
    return m.astype(BF16), p_tab.astype(BF16), q_tab.astype(BF16), a1, a2


def _s5_kernel(u_ref, m_ref, p_ref, q_ref, a1_ref, a2_ref, y_ref, v_ref, xs_ref, *, n_chunks, n_batch):
    u = u_ref[0]
    v_ref[...] = jnp.dot(u, p_ref[0], preferred_element_type=F32)
    a1 = a1_ref[0]
    a2 = a2_ref[0]
    half = xs_ref.shape[1] // 2

    def step(n, state):
        r0 = pl.multiple_of(n * n_batch, n_batch)
        xs_ref[pl.ds(r0, n_batch), :] = state
        return a1 * state + a2 * pltpu.roll(state, half, 1) + v_ref[pl.ds(r0, n_batch), :]

    lax.fori_loop(0, n_chunks, step, jnp.zeros((n_batch, xs_ref.shape[1]), F32))
    y = jnp.dot(u, m_ref[0], preferred_element_type=F32)
    y = y + jnp.dot(xs_ref[...].astype(BF16), q_ref[0], preferred_element_type=F32)
    y_ref[0] = y.astype(BF16)


def _s5_call(ut, m, p_tab, q_tab, a1, a2, n_chunks, n_batch):
    g, rows, width = ut.shape
    p2 = p_tab.shape[-1]
    return pl.pallas_call(
        functools.partial(_s5_kernel, n_chunks=n_chunks, n_batch=n_batch),
        out_shape=jax.ShapeDtypeStruct((g, rows, width), BF16),
        grid=(g,),
        in_specs=[pl.BlockSpec((1, rows, width), lambda i: (i, 0, 0)),
                  pl.BlockSpec((1, width, width), lambda i: (i, 0, 0)),
                  pl.BlockSpec((1, width, p2), lambda i: (i, 0, 0)),
                  pl.BlockSpec((1, p2, width), lambda i: (i, 0, 0)),
                  pl.BlockSpec((1, 1, p2), lambda i: (i, 0, 0)),
                  pl.BlockSpec((1, 1, p2), lambda i: (i, 0, 0))],
        out_specs=pl.BlockSpec((1, rows, width), lambda i: (i, 0, 0)),
        scratch_shapes=[pltpu.VMEM((rows, p2), F32), pltpu.VMEM((rows, p2), F32)],
        compiler_params=pltpu.CompilerParams(
            dimension_semantics=("arbitrary",), vmem_limit_bytes=VMEM_LIMIT),
        name="s5",
    )(ut, m, p_tab, q_tab, a1, a2)


def _mix_back_kernel(x_ref, oa_ref, ys_ref, u_ref, mod_ref, dskip_ref, wglu_ref, bglu_ref,
                     wout_ref, wrh_ref, wrl_ref, wsg_ref, wsu_ref, wsd_ref,
                     x1_ref, h2_ref, lt_ref):
    na = oa_ref.shape[1]
    y = ys_ref[...].astype(F32) + dskip_ref[...] * u_ref[...]
    y = jax.nn.gelu(y)
    ob = y * _sigmoid(_bdot(y, wglu_ref[...]) + bglu_ref[...])
    mixed = (jnp.dot(oa_ref[...], wout_ref[0:na, :], preferred_element_type=F32)
             + jnp.dot(ob.astype(BF16), wout_ref[na:, :], preferred_element_type=F32))
    x1 = x_ref[...] + mod_ref[0, 2:3, :] * mixed
    ms = jnp.mean(x1 * x1, axis=-1, keepdims=True)
    h2 = x1 * lax.rsqrt(ms + EPS) * (1.0 + mod_ref[0, 4:5, :]) + mod_ref[0, 3:4, :]
    h2_ref[...] = h2
    h_hi = h2.astype(BF16)
    h_lo = (h2 - h_hi.astype(F32)).astype(BF16)
    lt = lax.dot_general(wrh_ref[...], h_hi, _NT, preferred_element_type=F32)
    lt = lt + lax.dot_general(wrl_ref[...], h_hi, _NT, preferred_element_type=F32)
    lt = lt + lax.dot_general(wrh_ref[...], h_lo, _NT, preferred_element_type=F32)
    lt_ref[...] = lt
    hid = _silu(jnp.dot(h_hi, wsg_ref[...], preferred_element_type=F32)) * jnp.dot(
        h_hi, wsu_ref[...], preferred_element_type=F32)
    shared = jnp.dot(hid.astype(BF16), wsd_ref[...], preferred_element_type=F32)
    x1_ref[...] = x1 + mod_ref[0, 5:6, :] * shared


def _mix_back_call(x2d, oa, ys, u, mod, dskip, wglu, bglu, wout, wrh, wrl, wsg, wsu, wsd, seq):
    n, d = x2d.shape
    nb = oa.shape[1]
    ne = wrh.shape[0]
    dsh = wsg.shape[1]
    tm = TOK_TILE
    per_b = seq // tm
    const = lambda i: (0, 0)
    return pl.pallas_call(
        _mix_back_kernel,
        out_shape=(jax.ShapeDtypeStruct((n, d), F32),
                   jax.ShapeDtypeStruct((n, d), F32),
                   jax.ShapeDtypeStruct((ne, n), F32)),
        grid=(n // tm,),
        in_specs=[pl.BlockSpec((tm, d), lambda i: (i, 0)),
                  pl.BlockSpec((tm, nb), lambda i: (i, 0)),
                  pl.BlockSpec((tm, nb), lambda i: (i, 0)),
                  pl.BlockSpec((tm, nb), lambda i: (i, 0)),
                  pl.BlockSpec((1, 6, d), lambda i: (i // per_b, 0, 0)),
                  pl.BlockSpec((1, nb), const),
                  pl.BlockSpec((nb, nb), const),
                  pl.BlockSpec((1, nb), const),
                  pl.BlockSpec((d, d), const),
                  pl.BlockSpec((ne, d), const),
                  pl.BlockSpec((ne, d), const),
                  pl.BlockSpec((d, dsh), const),
                  pl.BlockSpec((d, dsh), const),
                  pl.BlockSpec((dsh, d), const)],
        out_specs=(pl.BlockSpec((tm, d), lambda i: (i, 0)),
                   pl.BlockSpec((tm, d), lambda i: (i, 0)),
                   pl.BlockSpec((ne, tm), lambda i: (0, i))),
        compiler_params=pltpu.CompilerParams(
            dimension_semantics=("arbitrary",), vmem_limit_bytes=VMEM_LIMIT),
        name="mix_back",
    )(x2d, oa, ys, u, mod, dskip, wglu, bglu, wout, wrh, wrl, wsg, wsu, wsd)


def _route_kernel(lt_ref, bias_ref, su_ref, idx_ref, w_ref, rank_ref, cnt_ref, run_ref):
    ne, tr = lt_ref.shape
    per_group = ne // N_EXPERT_GROUPS
    neg = -jnp.inf

    @pl.when(pl.program_id(0) == 0)
    def _():
        run_ref[...] = jnp.zeros_like(run_ref)

    s = _sigmoid(lt_ref[...])
    sel = s + bias_ref[...]
    gio = lax.broadcasted_iota(I32, (per_group, tr), 0)
    gscore = []
    for g in range(N_EXPERT_GROUPS):
        v = sel[g * per_group:(g + 1) * per_group, :]
        m1 = jnp.max(v, axis=0, keepdims=True)
        i1 = jnp.min(jnp.where(v == m1, gio, per_group), axis=0, keepdims=True)
        m2 = jnp.max(jnp.where(gio == i1, neg, v), axis=0, keepdims=True)
        gscore.append(m1 + m2)
    masked = []
    for g in range(N_EXPERT_GROUPS):
        ahead = jnp.zeros((1, tr), I32)
        for o in range(N_EXPERT_GROUPS):
            if o == g:
                continue
            wins = (gscore[o] >= gscore[g]) if o < g else (gscore[o] > gscore[g])
            ahead = ahead + wins.astype(I32)
        keep = ahead < TOPK_GROUPS
        masked.append(jnp.where(keep, sel[g * per_group:(g + 1) * per_group, :], neg))
    selm = jnp.concatenate(masked, axis=0)
    eio = lax.broadcasted_iota(I32, (ne, tr), 0)
    hits = jnp.zeros((ne, tr), F32)
    idxs, ws = [], []
    for k in range(TOP_K):
        m = jnp.max(selm, axis=0, keepdims=True)
        ik = jnp.min(jnp.where(selm == m, eio, ne), axis=0, keepdims=True)
        onehot = eio == ik
        ws.append(jnp.sum(jnp.where(onehot, s, 0.0), axis=0, keepdims=True))
        hits = jnp.where(onehot, 1.0, hits)
        selm = jnp.where(onehot, neg, selm)
        idxs.append(ik)
    wsum = ws[0]
    for k in range(1, TOP_K):
        wsum = wsum + ws[k]
    scale = ROUTE_SCALE / wsum
    ranks = jnp.dot(hits.astype(BF16), su_ref[...], preferred_element_type=F32) + run_ref[...]
    for k in range(TOP_K):
        idx_ref[k:k + 1, :] = idxs[k]
        w_ref[k:k + 1, :] = ws[k] * scale
        rk = jnp.sum(jnp.where(eio == idxs[k], ranks, 0.0), axis=0, keepdims=True)
        rank_ref[k:k + 1, :] = rk.astype(I32)
    run_ref[...] = run_ref[...] + jnp.sum(hits, axis=1, keepdims=True)
    cnt_ref[...] = run_ref[...]


def _route_call(lt, bias):
    ne, n = lt.shape
    tr = ROUTE_TILE
    su = jnp.triu(jnp.ones((tr, tr), F32), k=1).astype(BF16)
    return pl.pallas_call(
        _route_kernel,
        out_shape=(jax.ShapeDtypeStruct((TOP_K, n), I32),
                   jax.ShapeDtypeStruct((TOP_K, n), F32),
                   jax.ShapeDtypeStruct((TOP_K, n), I32),
                   jax.ShapeDtypeStruct((ne, 1), F32)),
        grid=(n // tr,),
        in_specs=[pl.BlockSpec((ne, tr), lambda i: (0, i)),
                  pl.BlockSpec((ne, 1), lambda i: (0, 0)),
                  pl.BlockSpec((tr, tr), lambda i: (0, 0))],
        out_specs=(pl.BlockSpec((TOP_K, tr), lambda i: (0, i)),
                   pl.BlockSpec((TOP_K, tr), lambda i: (0, i)),
                   pl.BlockSpec((TOP_K, tr), lambda i: (0, i)),
                   pl.BlockSpec((ne, 1), lambda i: (0, 0))),
        scratch_shapes=[pltpu.VMEM((ne, 1), F32)],
        compiler_params=pltpu.CompilerParams(
            dimension_semantics=("arbitrary",), vmem_limit_bytes=VMEM_LIMIT),
        name="route",
    )(lt, bias.reshape(ne, 1), su)


def _pos_kernel(idx_ref, rank_ref, st_ref, pos_ref):
    ne = st_ref.shape[0]
    tp = idx_ref.shape[1]
    eio = lax.broadcasted_iota(I32, (ne, tp), 0)
    st = st_ref[...]
    for k in range(TOP_K):
        base = jnp.sum(jnp.where(eio == idx_ref[k:k + 1, :], st, 0), axis=0, keepdims=True)
        pos_ref[k:k + 1, :] = base + rank_ref[k:k + 1, :]


def _pos_call(top_idx, rank, starts):
    kk, n = top_idx.shape
    ne = starts.shape[0]
    tp = POS_TILE
    return pl.pallas_call(
        _pos_kernel,
        out_shape=jax.ShapeDtypeStruct((kk, n), I32),
        grid=(n // tp,),
        in_specs=[pl.BlockSpec((kk, tp), lambda i: (0, i)),
                  pl.BlockSpec((kk, tp), lambda i: (0, i)),
                  pl.BlockSpec((ne, 1), lambda i: (0, 0))],
        out_specs=pl.BlockSpec((kk, tp), lambda i: (0, i)),
        compiler_params=pltpu.CompilerParams(
            dimension_semantics=("arbitrary",), vmem_limit_bytes=VMEM_LIMIT),
        name="pos",
    )(top_idx, rank, starts.reshape(ne, 1))


def _dispatch_kernel(pos_ref, h_ref, xs_ref, sem):
    tm = h_ref.shape[0]

    def row_copy(t, k):
        return pltpu.make_async_copy(h_ref.at[pl.ds(t, 1)], xs_ref.at[pl.ds(pos_ref[k, t], 1)], sem)

    def issue(t, carry):
        for k in range(TOP_K):
            row_copy(t, k).start()
        return carry

    lax.fori_loop(0, tm, issue, 0)
    for k in range(TOP_K):
        pltpu.make_async_copy(h_ref, xs_ref.at[pl.ds(0, tm)], sem).wait()


def _dispatch_call(pos, h2):
    n, d = h2.shape
    tm = MOVE_TILE
    return pl.pallas_call(
        _dispatch_kernel,
        out_shape=jax.ShapeDtypeStruct((n * TOP_K, d), F32),
        grid=(n // tm,),
        in_specs=[pl.BlockSpec((TOP_K, tm), lambda i: (0, i), memory_space=pltpu.SMEM),
                  pl.BlockSpec((tm, d), lambda i: (i, 0))],
        out_specs=pl.BlockSpec(memory_space=pl.ANY),
        scratch_shapes=[pltpu.SemaphoreType.DMA],
        compiler_params=pltpu.CompilerParams(
            dimension_semantics=("arbitrary",), vmem_limit_bytes=VMEM_LIMIT),
        name="dispatch",
    )(pos, h2)


def _experts_kernel(pe_ref, pb_ref, pv_ref, st_ref, en_ref,
                    xs_ref, wg_ref, wu_ref, wd_ref, y_ref, wgb_ref, wub_ref, wdb_ref):
    p = pl.program_id(0)
    e = pe_ref[p]
    blk = pb_ref[p]
    rb = xs_ref.shape[0]
    prev = jnp.maximum(p - 1, 0)
    first = jnp.logical_or(p == 0, pb_ref[prev] != blk)

    @pl.when(first)
    def _():
        y_ref[...] = jnp.zeros_like(y_ref)

    @pl.when(jnp.logical_or(p == 0, pe_ref[prev] != e))
    def _():
        wgb_ref[...] = wg_ref[0].astype(BF16)
        wub_ref[...] = wu_ref[0].astype(BF16)
        wdb_ref[...] = wd_ref[0].astype(BF16)

    @pl.when(pv_ref[p] == 1)
    def _():
        xb = xs_ref[...].astype(BF16)
        gate = jnp.dot(xb, wgb_ref[...], preferred_element_type=F32)
        up = jnp.dot(xb, wub_ref[...], preferred_element_type=F32)
        hid = (_silu(gate) * up).astype(BF16)
        yb = jnp.dot(hid, wdb_ref[...], preferred_element_type=F32)
        rows = blk * rb + lax.broadcasted_iota(I32, (rb, 1), 0)
        mine = jnp.logical_and(rows >= st_ref[e], rows < en_ref[e])
        y_ref[...] += jnp.where(mine, yb, 0.0)


def _experts_call(pair_e, pair_blk, pair_ok, starts, ends, xs, w_gate, w_up, w_down):
    rows, d = xs.shape
    ne, _, de = w_gate.shape
    rb = ROW_BLOCK
    n_pairs = pair_e.shape[0]
    grid_spec = pltpu.PrefetchScalarGridSpec(
        num_scalar_prefetch=5,
        grid=(n_pairs,),
        in_specs=[pl.BlockSpec((rb, d), lambda p, pe, pb, pv, st, en: (pb[p], 0)),
                  pl.BlockSpec((1, d, de), lambda p, pe, pb, pv, st, en: (pe[p], 0, 0)),
                  pl.BlockSpec((1, d, de), lambda p, pe, pb, pv, st, en: (pe[p], 0, 0)),
                  pl.BlockSpec((1, de, d), lambda p, pe, pb, pv, st, en: (pe[p], 0, 0))],
        out_specs=pl.BlockSpec((rb, d), lambda p, pe, pb, pv, st, en: (pb[p], 0)),
        scratch_shapes=[pltpu.VMEM((d, de), BF16), pltpu.VMEM((d, de), BF16),
                        pltpu.VMEM((de, d), BF16)],
    )
    return pl.pallas_call(
        _experts_kernel,
        out_shape=jax.ShapeDtypeStruct((rows, d), F32),
        grid_spec=grid_spec,
        compiler_params=pltpu.CompilerParams(
            dimension_semantics=("arbitrary",), vmem_limit_bytes=VMEM_LIMIT),
        name="experts",
    )(pair_e, pair_blk, pair_ok, starts, ends, xs, w_gate, w_up, w_down)


def _combine_kernel(pos_ref, ys_ref, x1_ref, w_ref, g2_ref, fg_ref, o_ref, buf_ref, sem):
    tm = x1_ref.shape[0]

    def issue(t, carry):
        for k in range(TOP_K):
            pltpu.make_async_copy(ys_ref.at[pl.ds(pos_ref[k, t], 1)],
                                  buf_ref.at[k, pl.ds(t, 1)], sem).start()
        return carry

    lax.fori_loop(0, tm, issue, 0)
    for k in range(TOP_K):
        pltpu.make_async_copy(ys_ref.at[pl.ds(0, tm)], buf_ref.at[k], sem).wait()
    w = w_ref[...]
    routed = buf_ref[0] * w[:, 0:1]
    for k in range(1, TOP_K):
        routed = routed + buf_ref[k] * w[:, k:k + 1]
    x2 = x1_ref[...] + g2_ref[0] * routed
    ms = jnp.mean(x2 * x2, axis=-1, keepdims=True)
    o_ref[...] = x2 * lax.rsqrt(ms + EPS) * fg_ref[...]


def _combine_call(pos, ysorted, x1s, w_tok, g2, fgain, seq):
    n, d = x1s.shape
    tm = MOVE_TILE
    per_b = seq // tm
    return pl.pallas_call(
        _combine_kernel,
        out_shape=jax.ShapeDtypeStruct((n, d), F32),
        grid=(n // tm,),
        in_specs=[pl.BlockSpec((TOP_K, tm), lambda i: (0, i), memory_space=pltpu.SMEM),
                  pl.BlockSpec(memory_space=pl.ANY),
                  pl.BlockSpec((tm, d), lambda i: (i, 0)),
                  pl.BlockSpec((tm, TOP_K), lambda i: (i, 0)),
                  pl.BlockSpec((1, 1, d), lambda i: (i // per_b, 0, 0)),
                  pl.BlockSpec((1, d), lambda i: (0, 0))],
        out_specs=pl.BlockSpec((tm, d), lambda i: (i, 0)),
        scratch_shapes=[pltpu.VMEM((TOP_K, tm, d), F32), pltpu.SemaphoreType.DMA],
        compiler_params=pltpu.CompilerParams(
            dimension_semantics=("arbitrary",), vmem_limit_bytes=VMEM_LIMIT),
        name="combine",
    )(pos, ysorted, x1s, w_tok, g2, fgain)


def _pair_tables(counts, n_rows):
    ne = counts.shape[0]
    sizes = counts.astype(I32)
    ends = jnp.cumsum(sizes)
    starts = ends - sizes
    first_blk = starts // ROW_BLOCK
    last_blk = (ends - 1) // ROW_BLOCK
    n_pairs = jnp.where(sizes > 0, last_blk - first_blk + 1, 0)
    pair_end = jnp.cumsum(n_pairs)
    pair_start = pair_end - n_pairs
    max_pairs = n_rows // ROW_BLOCK + ne
    p = jnp.arange(max_pairs, dtype=I32)
    ok = p < pair_end[-1]
    pc = jnp.minimum(p, pair_end[-1] - 1)
    pair_e = jnp.minimum(jnp.searchsorted(pair_end, pc, side='right'), ne - 1).astype(I32)
    pair_blk = (first_blk[pair_e] + pc - pair_start[pair_e]).astype(I32)
    return pair_e, pair_blk, ok.astype(I32), starts.astype(I32), ends.astype(I32)


def kernel(x, c, w_ada, b_ada, w_in, lb_logits, hgrn_norm, lam_re, lam_im, log_dt, b_re, b_im,
           c_re, c_im, d_skip, w_glu, b_glu, w_out, w_router, router_bias, w_gate, w_up, w_down,
           ws_gate, ws_up, ws_down, final_gain):
    bsz, seq, d = x.shape
    n = bsz * seq
    fdim = HGRN_HEADS * HGRN_KDIM
    n_chunks = seq // CHUNK
    lb = jnp.cumsum(jax.nn.softmax(lb_logits.astype(F32), axis=0), axis=0)[0].reshape(1, fdim)

    mod = _mod_call(c, w_ada[0], b_ada[0]).reshape(bsz, 6, d)
    u, out_a = _mix_front_call(x, mod, w_in[0].astype(BF16), lb, hgrn_norm[0].reshape(1, fdim))

    nb = u.shape[-1]
    groups = nb // S5_GROUP
    m_tab, p_tab, q_tab, a1, a2 = _s5_tables(lam_re[0], lam_im[0], log_dt[0], b_re[0], b_im[0],
                                             c_re[0], c_im[0])
    ut = u.reshape(bsz, n_chunks, CHUNK, groups, S5_GROUP).transpose(3, 1, 0, 2, 4)
    ut = ut.reshape(groups, n_chunks * bsz, CHUNK * S5_GROUP).astype(BF16)
    yt = _s5_call(ut, m_tab, p_tab, q_tab, a1, a2, n_chunks, bsz)
    ys = yt.reshape(groups, n_chunks, bsz, CHUNK, S5_GROUP).transpose(2, 1, 3, 0, 4).reshape(n, nb)

    wr_t = w_router[0].T
    wr_hi = wr_t.astype(BF16)
    wr_lo = (wr_t - wr_hi.astype(F32)).astype(BF16)
    x1s, h2, logits_t = _mix_back_call(
        x.reshape(n, d), out_a.reshape(n, fdim), ys, u.reshape(n, nb), mod,
        d_skip[0].reshape(1, nb), w_glu[0].astype(BF16), b_glu[0].reshape(1, nb),
        w_out[0].astype(BF16), wr_hi, wr_lo, ws_gate[0].astype(BF16), ws_up[0].astype(BF16),
        ws_down[0].astype(BF16), seq)

    top_idx, top_w, rank, counts = _route_call(logits_t, router_bias[0])
    pair_e, pair_blk, pair_ok, starts, ends = _pair_tables(counts[:, 0], n * TOP_K)
    pos = _pos_call(top_idx, rank, starts)

    xs = _dispatch_call(pos, h2)
    y_sorted = _experts_call(pair_e, pair_blk, pair_ok, starts, ends, xs,
                             w_gate[0], w_up[0], w_down[0])
    out = _combine_call(pos, y_sorted, x1s, top_w.T, mod[:, 5:6, :], final_gain.reshape(1, d), seq)
    return out.reshape(bsz, seq, d)
```

```python
import functools

import jax
import jax.numpy as jnp
from jax import lax
from jax.experimental import pallas as pl
from jax.experimental.pallas import tpu as pltpu
from jax.experimental.pallas import tpu_sc as plsc

F32 = jnp.float32
BF16 = jnp.bfloat16
I32 = jnp.int32

EPS = 1e-6
CHUNK = 64
HGRN_HEADS = 4
HGRN_KDIM = 128
S5_GROUP = 16
S5_STATE = 64
N_EXPERT_GROUPS = 8
TOPK_GROUPS = 4
TOP_K = 8
ROUTE_SCALE = 2.5
LANES = 128
PACK_ROWS = 4
HIGH_HALF = -65536
SC_CORES = 2
SC_SUBCORES = 16
SC_WINDOW = 64

SEQ_TILE = 512
TOK_TILE = 512
ROUTE_TILE = 256
POS_TILE = 1024
MOVE_TILE = 256
ROW_BLOCK = 512
VMEM_LIMIT = 56 * 1024 * 1024

_NT = (((1,), (1,)), ((), ()))
_TN = (((0,), (0,)), ((), ()))


def _sigmoid(v):
    return 1.0 / (1.0 + jnp.exp(-v))


def _silu(v):
    return v * _sigmoid(v)


def _bdot(a, b):
    return jnp.dot(a.astype(BF16), b.astype(BF16), preferred_element_type=F32)


def _store_packed(ref, val, n_rows):
    half = val.shape[1] // 2
    bits = lax.bitcast_convert_type(val.astype(BF16).astype(F32), I32)
    word = lax.shift_right_logical(bits[:, :half], 16) | (bits[:, half:] & HIGH_HALF)
    for j in range(PACK_ROWS):
        ref[pl.ds(j, n_rows, stride=PACK_ROWS), :] = word[:, j * LANES:(j + 1) * LANES]


def _load_packed(ref, n_rows):
    lo, hi = [], []
    for j in range(PACK_ROWS):
        w = ref[pl.ds(j, n_rows, stride=PACK_ROWS), :]
        lo.append(lax.bitcast_convert_type(lax.shift_left(w, 16), F32))
        hi.append(lax.bitcast_convert_type(w & HIGH_HALF, F32))
    return lo, hi


def _mod_kernel(c_ref, w_ref, b_ref, o_ref):
    o_ref[...] = _bdot(_silu(c_ref[...]), w_ref[...]) + b_ref[...]


def _mod_call(c, w_ada, b_ada):
    bsz, d = c.shape
    n_out = w_ada.shape[1]
    return pl.pallas_call(
        _mod_kernel,
        out_shape=jax.ShapeDtypeStruct((bsz, n_out), F32),
        grid=(n_out // d,),
        in_specs=[pl.BlockSpec((bsz, d), lambda j: (0, 0)),
                  pl.BlockSpec((d, d), lambda j: (0, j)),
                  pl.BlockSpec((1, d), lambda j: (0, j))],
        out_specs=pl.BlockSpec((bsz, d), lambda j: (0, j)),
        compiler_params=pltpu.CompilerParams(vmem_limit_bytes=VMEM_LIMIT),
        name="mod",
    )(c, w_ada, b_ada.reshape(1, n_out))


def _mix_front_kernel(x_ref, mod_ref, win_ref, lb_ref, gn_ref, ltri_ref,
                      u_ref, oa_ref, proj_ref, st_ref):
    fdim = HGRN_HEADS * HGRN_KDIM
    ts = x_ref.shape[1]

    @pl.when(pl.program_id(1) == 0)
    def _():
        st_ref[...] = jnp.zeros_like(st_ref)

    x = x_ref[0]
    ms = jnp.mean(x * x, axis=-1, keepdims=True)
    h = x * lax.rsqrt(ms + EPS) * (1.0 + mod_ref[0, 1:2, :]) + mod_ref[0, 0:1, :]
    proj_ref[...] = jnp.dot(h.astype(BF16), win_ref[...], preferred_element_type=F32)
    u_ref[0] = proj_ref[:, 4 * fdim:5 * fdim]

    lb = lb_ref[...]
    gn = gn_ref[...]
    ltri = ltri_ref[...]
    row = lax.broadcasted_iota(I32, (CHUNK, CHUNK), 0)
    col = lax.broadcasted_iota(I32, (CHUNK, CHUNK), 1)
    causal = row >= col

    def chunk_step(ci, carry):
        r0 = pl.multiple_of(ci * CHUNK, CHUNK)
        q = proj_ref[pl.ds(r0, CHUNK), 0:fdim]
        fl = proj_ref[pl.ds(r0, CHUNK), fdim:2 * fdim]
        iv = proj_ref[pl.ds(r0, CHUNK), 2 * fdim:3 * fdim]
        og = proj_ref[pl.ds(r0, CHUNK), 3 * fdim:4 * fdim]
        f = lb + (1.0 - lb) * _sigmoid(fl)
        b = jnp.dot(ltri, jnp.log(f), precision=lax.Precision.HIGHEST,
                    preferred_element_type=F32)
        b_ref = b[CHUNK // 2 - 1:CHUNK // 2, :]
        b_last = b[CHUNK - 1:CHUNK, :]
        qs = _silu(q)
        kk = 1.0 - f
        qe = (qs * jnp.exp(b - b_ref)).astype(BF16)
        ke = (kk * jnp.exp(b_ref - b)).astype(BF16)
        qb = (qs * jnp.exp(b)).astype(BF16)
        k2 = (kk * jnp.exp(b_last - b)).astype(BF16)
        dec = jnp.exp(b_last)
        ivb = iv.astype(BF16)
        outs = []
        for hh in range(HGRN_HEADS):
            sl = slice(hh * HGRN_KDIM, (hh + 1) * HGRN_KDIM)
            att = lax.dot_general(qe[:, sl], ke[:, sl], _NT, preferred_element_type=F32)
            att = jnp.where(causal, att, 0.0)
            st = st_ref[hh]
            o = jnp.dot(att.astype(BF16), ivb[:, sl], preferred_element_type=F32)
            o = o + lax.dot_general(qb[:, sl], st.astype(BF16), _NT, preferred_element_type=F32)
            st_ref[hh] = st * dec[:, sl] + lax.dot_general(
                ivb[:, sl], k2[:, sl], _TN, preferred_element_type=F32)
            outs.append(o * lax.rsqrt(jnp.mean(o * o, axis=-1, keepdims=True) + EPS))
        o = jnp.concatenate(outs, axis=1) * gn * _silu(og)
        oa_ref[0, pl.ds(r0, CHUNK), :] = o.astype(BF16)
        return carry

    lax.fori_loop(0, ts // CHUNK, chunk_step, 0)


def _mix_front_call(x, mod, w_in_b, lb, gn):
    bsz, seq, d = x.shape
    fdim = HGRN_HEADS * HGRN_KDIM
    ncols = w_in_b.shape[1]
    ltri = jnp.tril(jnp.ones((CHUNK, CHUNK), F32))
    ts = SEQ_TILE
    return pl.pallas_call(
        _mix_front_kernel,
        out_shape=(jax.ShapeDtypeStruct((bsz, seq, ncols - 4 * fdim), F32),
                   jax.ShapeDtypeStruct((bsz, seq, fdim), BF16)),
        grid=(bsz, seq // ts),
        in_specs=[pl.BlockSpec((1, ts, d), lambda b, j: (b, j, 0)),
                  pl.BlockSpec((1, 6, d), lambda b, j: (b, 0, 0)),
                  pl.BlockSpec((d, ncols), lambda b, j: (0, 0)),
                  pl.BlockSpec((1, fdim), lambda b, j: (0, 0)),
                  pl.BlockSpec((1, fdim), lambda b, j: (0, 0)),
                  pl.BlockSpec((CHUNK, CHUNK), lambda b, j: (0, 0))],
        out_specs=(pl.BlockSpec((1, ts, ncols - 4 * fdim), lambda b, j: (b, j, 0)),
                   pl.BlockSpec((1, ts, fdim), lambda b, j: (b, j, 0))),
        scratch_shapes=[pltpu.VMEM((ts, ncols), F32),
                        pltpu.VMEM((HGRN_HEADS, fdim // HGRN_HEADS, HGRN_KDIM), F32)],
        compiler_params=pltpu.CompilerParams(
            dimension_semantics=("arbitrary", "arbitrary"), vmem_limit_bytes=VMEM_LIMIT),
        name="mix_front",
    )(x, mod, w_in_b, lb, gn, ltri)


def _s5_tables(lam_re, lam_im, log_dt, b_re, b_im, c_re, c_im):
    t = CHUNK
    hp = lax.Precision.HIGHEST
    lam = lax.complex(jnp.minimum(lam_re, -1e-4), lam_im)
    lam_dt = lam * jnp.exp(log_dt)[:, None]
    lam_bar = jnp.exp(lam_dt)
    b_bar = ((lam_bar - 1.0) / lam)[..., None] * lax.complex(b_re, b_im)
    c_mat = lax.complex(c_re, c_im)
    taus = jnp.arange(t + 1, dtype=F32)
    lam_pow = jnp.exp(lam_dt[:, None, :] * taus[None, :, None])
    g, p = lam.shape
    c = b_re.shape[-1]
    cl = c_mat[:, None, :, :] * lam_pow[:, :t, None, :]
    kr = (jnp.einsum('gtcp,gpi->gtci', cl.real, b_bar.real, precision=hp)
          - jnp.einsum('gtcp,gpi->gtci', cl.imag, b_bar.imag, precision=hp))
    z = jnp.pad(kr.transpose(0, 3, 2, 1).astype(BF16), ((0, 0), (0, 0), (0, 0), (t - 1, 0)))
    kg = jnp.stack([z[..., t - 1 - s:2 * t - 1 - s] for s in range(t)], axis=1)
    m = kg.transpose(0, 1, 2, 4, 3).reshape(g, t * c, t * c)
    pc = lam_pow[:, t - 1::-1][:, :t, :, None] * b_bar[:, None, :, :]
    pc = pc.transpose(0, 1, 3, 2).reshape(g, t * c, p)
    p_tab = jnp.concatenate([pc.real, pc.imag], axis=-1)
    ql = c_mat[:, None, :, :] * lam_pow[:, 1:t + 1, None, :]
    ql = ql.transpose(0, 3, 1, 2).reshape(g, p, t * c)
    q_tab = jnp.concatenate([ql.real, -ql.imag], axis=1)
    lam_t = lam_pow[:, t]
    a1 = jnp.concatenate([lam_t.real, lam_t.real], axis=-1)[:, None, :]
    a2 = jnp.concatenate([-lam_t.imag, lam_t.imag], axis=-1)[:, None, :]
    return m.astype(BF16), p_tab.astype(BF16), q_tab.astype(BF16), a1, a2


def _s5_kernel(u_ref, m_ref, p_ref, q_ref, a1_ref, a2_ref, y_ref, v_ref, xs_ref, *, n_chunks, n_batch):
    u = u_ref[0]
    v_ref[...] = jnp.dot(u, p_ref[0], preferred_element_type=F32)
    a1 = a1_ref[0]
    a2 = a2_ref[0]
    half = xs_ref.shape[1] // 2

    def step(n, state):
        r0 = pl.multiple_of(n * n_batch, n_batch)
        xs_ref[pl.ds(r0, n_batch), :] = state
        return a1 * state + a2 * pltpu.roll(state, half, 1) + v_ref[pl.ds(r0, n_batch), :]

    lax.fori_loop(0, n_chunks, step, jnp.zeros((n_batch, xs_ref.shape[1]), F32))
    y = jnp.dot(u, m_ref[0], preferred_element_type=F32)
    y = y + jnp.dot(xs_ref[...].astype(BF16), q_ref[0], preferred_element_type=F32)
    y_ref[0] = y.astype(BF16)


def _s5_call(ut, m, p_tab, q_tab, a1, a2, n_chunks, n_batch):
    g, rows, width = ut.shape
    p2 = p_tab.shape[-1]
    return pl.pallas_call(
        functools.partial(_s5_kernel, n_chunks=n_chunks, n_batch=n_batch),
        out_shape=jax.ShapeDtypeStruct((g, rows, width), BF16),
        grid=(g,),
        in_specs=[pl.BlockSpec((1, rows, width), lambda i: (i, 0, 0)),
                  pl.BlockSpec((1, width, width), lambda i: (i, 0, 0)),
                  pl.BlockSpec((1, width, p2), lambda i: (i, 0, 0)),
                  pl.BlockSpec((1, p2, width), lambda i: (i, 0, 0)),
                  pl.BlockSpec((1, 1, p2), lambda i: (i, 0, 0)),
                  pl.BlockSpec((1, 1, p2), lambda i: (i, 0, 0))],
        out_specs=pl.BlockSpec((1, rows, width), lambda i: (i, 0, 0)),
        scratch_shapes=[pltpu.VMEM((rows, p2), F32), pltpu.VMEM((rows, p2), F32)],
        compiler_params=pltpu.CompilerParams(
            dimension_semantics=("arbitrary",), vmem_limit_bytes=VMEM_LIMIT),
        name="s5",
    )(ut, m, p_tab, q_tab, a1, a2)


def _mix_back_kernel(x_ref, oa_ref, ys_ref, u_ref, mod_ref, dskip_ref, wglu_ref, bglu_ref,
                     wout_ref, wrh_ref, wrl_ref, wsg_ref, wsu_ref, wsd_ref,
                     x1_ref, hp_ref, lt_ref):
    na = oa_ref.shape[1]
    y = ys_ref[...].astype(F32) + dskip_ref[...] * u_ref[...]
    y = jax.nn.gelu(y)
    ob = y * _sigmoid(_bdot(y, wglu_ref[...]) + bglu_ref[...])
    mixed = (jnp.dot(oa_ref[...], wout_ref[0:na, :], preferred_element_type=F32)
             + jnp.dot(ob.astype(BF16), wout_ref[na:, :], preferred_element_type=F32))
    x1 = x_ref[...] + mod_ref[0, 2:3, :] * mixed
    ms = jnp.mean(x1 * x1, axis=-1, keepdims=True)
    h2 = x1 * lax.rsqrt(ms + EPS) * (1.0 + mod_ref[0, 4:5, :]) + mod_ref[0, 3:4, :]
    _store_packed(hp_ref, h2, h2.shape[0])
    h_hi = h2.astype(BF16)
    h_lo = (h2 - h_hi.astype(F32)).astype(BF16)
    lt = lax.dot_general(wrh_ref[...], h_hi, _NT, preferred_element_type=F32)
    lt = lt + lax.dot_general(wrl_ref[...], h_hi, _NT, preferred_element_type=F32)
    lt = lt + lax.dot_general(wrh_ref[...], h_lo, _NT, preferred_element_type=F32)
    lt_ref[...] = lt
    hid = _silu(jnp.dot(h_hi, wsg_ref[...], preferred_element_type=F32)) * jnp.dot(
        h_hi, wsu_ref[...], preferred_element_type=F32)
    shared = jnp.dot(hid.astype(BF16), wsd_ref[...], preferred_element_type=F32)
    x1_ref[...] = x1 + mod_ref[0, 5:6, :] * shared


def _mix_back_call(x2d, oa, ys, u, mod, dskip, wglu, bglu, wout, wrh, wrl, wsg, wsu, wsd, seq):
    n, d = x2d.shape
    nb = oa.shape[1]
    ne = wrh.shape[0]
    dsh = wsg.shape[1]
    tm = TOK_TILE
    per_b = seq // tm
    const = lambda i: (0, 0)
    return pl.pallas_call(
        _mix_back_kernel,
        out_shape=(jax.ShapeDtypeStruct((n, d), F32),
                   jax.ShapeDtypeStruct((n * PACK_ROWS, LANES), I32),
                   jax.ShapeDtypeStruct((ne, n), F32)),
        grid=(n // tm,),
        in_specs=[pl.BlockSpec((tm, d), lambda i: (i, 0)),
                  pl.BlockSpec((tm, nb), lambda i: (i, 0)),
                  pl.BlockSpec((tm, nb), lambda i: (i, 0)),
                  pl.BlockSpec((tm, nb), lambda i: (i, 0)),
                  pl.BlockSpec((1, 6, d), lambda i: (i // per_b, 0, 0)),
                  pl.BlockSpec((1, nb), const),
                  pl.BlockSpec((nb, nb), const),
                  pl.BlockSpec((1, nb), const),
                  pl.BlockSpec((d, d), const),
                  pl.BlockSpec((ne, d), const),
                  pl.BlockSpec((ne, d), const),
                  pl.BlockSpec((d, dsh), const),
                  pl.BlockSpec((d, dsh), const),
                  pl.BlockSpec((dsh, d), const)],
        out_specs=(pl.BlockSpec((tm, d), lambda i: (i, 0)),
                   pl.BlockSpec((tm * PACK_ROWS, LANES), lambda i: (i, 0)),
                   pl.BlockSpec((ne, tm), lambda i: (0, i))),
        compiler_params=pltpu.CompilerParams(
            dimension_semantics=("arbitrary",), vmem_limit_bytes=VMEM_LIMIT),
        name="mix_back",
    )(x2d, oa, ys, u, mod, dskip, wglu, bglu, wout, wrh, wrl, wsg, wsu, wsd)


def _route_kernel(lt_ref, bias_ref, su_ref, idx_ref, w_ref, rank_ref, cnt_ref, run_ref):
    ne, tr = lt_ref.shape
    per_group = ne // N_EXPERT_GROUPS
    neg = -jnp.inf

    @pl.when(pl.program_id(0) == 0)
    def _():
        run_ref[...] = jnp.zeros_like(run_ref)

    s = _sigmoid(lt_ref[...])
    sel = s + bias_ref[...]
    gio = lax.broadcasted_iota(I32, (per_group, tr), 0)
    gscore = []
    for g in range(N_EXPERT_GROUPS):
        v = sel[g * per_group:(g + 1) * per_group, :]
        m1 = jnp.max(v, axis=0, keepdims=True)
        i1 = jnp.min(jnp.where(v == m1, gio, per_group), axis=0, keepdims=True)
        m2 = jnp.max(jnp.where(gio == i1, neg, v), axis=0, keepdims=True)
        gscore.append(m1 + m2)
    masked = []
    for g in range(N_EXPERT_GROUPS):
        ahead = jnp.zeros((1, tr), I32)
        for o in range(N_EXPERT_GROUPS):
            if o == g:
                continue
            wins = (gscore[o] >= gscore[g]) if o < g else (gscore[o] > gscore[g])
            ahead = ahead + wins.astype(I32)
        keep = ahead < TOPK_GROUPS
        masked.append(jnp.where(keep, sel[g * per_group:(g + 1) * per_group, :], neg))
    selm = jnp.concatenate(masked, axis=0)
    eio = lax.broadcasted_iota(I32, (ne, tr), 0)
    hits = jnp.zeros((ne, tr), F32)
    idxs, ws = [], []
    for k in range(TOP_K):
        m = jnp.max(selm, axis=0, keepdims=True)
        ik = jnp.min(jnp.where(selm == m, eio, ne), axis=0, keepdims=True)
        onehot = eio == ik
        ws.append(jnp.sum(jnp.where(onehot, s, 0.0), axis=0, keepdims=True))
        hits = jnp.where(onehot, 1.0, hits)
        selm = jnp.where(onehot, neg, selm)
        idxs.append(ik)
    wsum = ws[0]
    for k in range(1, TOP_K):
        wsum = wsum + ws[k]
    scale = ROUTE_SCALE / wsum
    ranks = jnp.dot(hits.astype(BF16), su_ref[...], preferred_element_type=F32) + run_ref[...]
    for k in range(TOP_K):
        idx_ref[k:k + 1, :] = idxs[k]
        w_ref[k:k + 1, :] = ws[k] * scale
        rk = jnp.sum(jnp.where(eio == idxs[k], ranks, 0.0), axis=0, keepdims=True)
        rank_ref[k:k + 1, :] = rk.astype(I32)
    run_ref[...] = run_ref[...] + jnp.sum(hits, axis=1, keepdims=True)
    cnt_ref[...] = run_ref[...]


def _route_call(lt, bias):
    ne, n = lt.shape
    tr = ROUTE_TILE
    su = jnp.triu(jnp.ones((tr, tr), F32), k=1).astype(BF16)
    return pl.pallas_call(
        _route_kernel,
        out_shape=(jax.ShapeDtypeStruct((TOP_K, n), I32),
                   jax.ShapeDtypeStruct((TOP_K, n), F32),
                   jax.ShapeDtypeStruct((TOP_K, n), I32),
                   jax.ShapeDtypeStruct((ne, 1), F32)),
        grid=(n // tr,),
        in_specs=[pl.BlockSpec((ne, tr), lambda i: (0, i)),
                  pl.BlockSpec((ne, 1), lambda i: (0, 0)),
                  pl.BlockSpec((tr, tr), lambda i: (0, 0))],
        out_specs=(pl.BlockSpec((TOP_K, tr), lambda i: (0, i)),
                   pl.BlockSpec((TOP_K, tr), lambda i: (0, i)),
                   pl.BlockSpec((TOP_K, tr), lambda i: (0, i)),
                   pl.BlockSpec((ne, 1), lambda i: (0, 0))),
        scratch_shapes=[pltpu.VMEM((ne, 1), F32)],
        compiler_params=pltpu.CompilerParams(
            dimension_semantics=("arbitrary",), vmem_limit_bytes=VMEM_LIMIT),
        name="route",
    )(lt, bias.reshape(ne, 1), su)


def _pos_kernel(idx_ref, rank_ref, st_ref, pos_ref):
    ne = st_ref.shape[0]
    tp = idx_ref.shape[1]
    eio = lax.broadcasted_iota(I32, (ne, tp), 0)
    st = st_ref[...]
    for k in range(TOP_K):
        base = jnp.sum(jnp.where(eio == idx_ref[k:k + 1, :], st, 0), axis=0, keepdims=True)
        pos_ref[k:k + 1, :] = base + rank_ref[k:k + 1, :]


def _pos_call(top_idx, rank, starts):
    kk, n = top_idx.shape
    ne = starts.shape[0]
    tp = POS_TILE
    return pl.pallas_call(
        _pos_kernel,
        out_shape=jax.ShapeDtypeStruct((kk, n), I32),
        grid=(n // tp,),
        in_specs=[pl.BlockSpec((kk, tp), lambda i: (0, i)),
                  pl.BlockSpec((kk, tp), lambda i: (0, i)),
                  pl.BlockSpec((ne, 1), lambda i: (0, 0))],
        out_specs=pl.BlockSpec((kk, tp), lambda i: (0, i)),
        compiler_params=pltpu.CompilerParams(
            dimension_semantics=("arbitrary",), vmem_limit_bytes=VMEM_LIMIT),
        name="pos",
    )(top_idx, rank, starts.reshape(ne, 1))


def _sc_mesh():
    return plsc.VectorSubcoreMesh(core_axis_name="c", subcore_axis_name="s",
                                  num_cores=SC_CORES, num_subcores=SC_SUBCORES)


def _sc_worker():
    return lax.axis_index("s") * SC_CORES + lax.axis_index("c")


def _dispatch_call(pos_win, hp):
    n = hp.shape[0]
    n_workers = SC_CORES * SC_SUBCORES
    wins_per_worker = n // SC_WINDOW // n_workers

    def body(hp_hbm, pos_hbm, xs_hbm, idx_v, rows_v, sem):
        first_win = _sc_worker() * wins_per_worker

        @pl.loop(0, wins_per_worker)
        def _(w):
            win = first_win + w
            pltpu.sync_copy(hp_hbm.at[pl.ds(win * SC_WINDOW, SC_WINDOW)], rows_v)
            pltpu.sync_copy(pos_hbm.at[win], idx_v)
            copies = [pltpu.async_copy(rows_v, xs_hbm.at[idx_v.at[k]], sem) for k in range(TOP_K)]
            for cp in copies:
                cp.wait()

    return pl.kernel(
        body,
        out_type=jax.ShapeDtypeStruct((n * TOP_K,) + hp.shape[1:], hp.dtype),
        mesh=_sc_mesh(),
        scratch_types=[pltpu.VMEM((TOP_K, SC_WINDOW), I32),
                       pltpu.VMEM((SC_WINDOW,) + hp.shape[1:], hp.dtype),
                       pltpu.SemaphoreType.DMA],
        name="dispatch",
    )(hp, pos_win)


def _collect_call(pos_win, y_sorted):
    n = pos_win.shape[0] * SC_WINDOW
    n_workers = SC_CORES * SC_SUBCORES
    wins_per_worker = n // SC_WINDOW // n_workers

    def body(ys_hbm, pos_hbm, out_hbm, idx_v, rows_v, sem):
        first_win = _sc_worker() * wins_per_worker

        @pl.loop(0, wins_per_worker)
        def _(w):
            win = first_win + w
            pltpu.sync_copy(pos_hbm.at[win], idx_v)
            for k in range(TOP_K):
                pltpu.async_copy(ys_hbm.at[idx_v.at[k]], rows_v, sem).wait()
                pltpu.sync_copy(rows_v, out_hbm.at[k, pl.ds(win * SC_WINDOW, SC_WINDOW)])

    return pl.kernel(
        body,
        out_type=jax.ShapeDtypeStruct((TOP_K, n) + y_sorted.shape[1:], y_sorted.dtype),
        mesh=_sc_mesh(),
        scratch_types=[pltpu.VMEM((TOP_K, SC_WINDOW), I32),
                       pltpu.VMEM((SC_WINDOW,) + y_sorted.shape[1:], y_sorted.dtype),
                       pltpu.SemaphoreType.DMA],
        name="collect",
    )(y_sorted, pos_win)


def _experts_kernel(pe_ref, pb_ref, pv_ref, st_ref, en_ref,
                    xs_ref, wg_ref, wu_ref, wd_ref, y_ref, wgb_ref, wub_ref, wdb_ref, acc_ref):
    p = pl.program_id(0)
    n_pairs = pl.num_programs(0)
    e = pe_ref[p]
    blk = pb_ref[p]
    rb = acc_ref.shape[0]
    prev = jnp.maximum(p - 1, 0)
    nxt = jnp.minimum(p + 1, n_pairs - 1)
    live = pv_ref[p] == 1
    first = jnp.logical_or(p == 0, pb_ref[prev] != blk)
    last = jnp.logical_or(p == n_pairs - 1,
                          jnp.logical_or(pb_ref[nxt] != blk, pv_ref[nxt] == 0))

    @pl.when(jnp.logical_or(p == 0, pe_ref[prev] != e))
    def _():
        wgb_ref[...] = wg_ref[0].astype(BF16)
        wub_ref[...] = wu_ref[0].astype(BF16)
        wdb_ref[...] = wd_ref[0].astype(BF16)

    @pl.when(live)
    def _():
        lo, hi = _load_packed(xs_ref, rb)
        xb = jnp.concatenate(lo + hi, axis=1).astype(BF16)
        gate = jnp.dot(xb, wgb_ref[...], preferred_element_type=F32)
        up = jnp.dot(xb, wub_ref[...], preferred_element_type=F32)
        hid = (_silu(gate) * up).astype(BF16)
        yb = jnp.dot(hid, wdb_ref[...], preferred_element_type=F32)
        rows = blk * rb + lax.broadcasted_iota(I32, (rb, 1), 0)
        mine = jnp.logical_and(rows >= st_ref[e], rows < en_ref[e])
        yb = jnp.where(mine, yb, 0.0)

        @pl.when(first)
        def _():
            acc_ref[...] = yb

        @pl.when(jnp.logical_not(first))
        def _():
            acc_ref[...] += yb

    @pl.when(jnp.logical_and(live, last))
    def _():
        _store_packed(y_ref, acc_ref[...], rb)


def _experts_call(pair_e, pair_blk, pair_ok, starts, ends, xs, w_gate, w_up, w_down):
    ne, d, de = w_gate.shape
    rb = ROW_BLOCK
    n_pairs = pair_e.shape[0]
    grid_spec = pltpu.PrefetchScalarGridSpec(
        num_scalar_prefetch=5,
        grid=(n_pairs,),
        in_specs=[pl.BlockSpec((rb * PACK_ROWS, LANES), lambda p, pe, pb, pv, st, en: (pb[p], 0)),
                  pl.BlockSpec((1, d, de), lambda p, pe, pb, pv, st, en: (pe[p], 0, 0)),
                  pl.BlockSpec((1, d, de), lambda p, pe, pb, pv, st, en: (pe[p], 0, 0)),
                  pl.BlockSpec((1, de, d), lambda p, pe, pb, pv, st, en: (pe[p], 0, 0))],
        out_specs=pl.BlockSpec((rb * PACK_ROWS, LANES), lambda p, pe, pb, pv, st, en: (pb[p], 0)),
        scratch_shapes=[pltpu.VMEM((d, de), BF16), pltpu.VMEM((d, de), BF16),
                        pltpu.VMEM((de, d), BF16), pltpu.VMEM((rb, d), F32)],
    )
    return pl.pallas_call(
        _experts_kernel,
        out_shape=jax.ShapeDtypeStruct(xs.shape, xs.dtype),
        grid_spec=grid_spec,
        compiler_params=pltpu.CompilerParams(
            dimension_semantics=("arbitrary",), vmem_limit_bytes=VMEM_LIMIT),
        name="experts",
    )(pair_e, pair_blk, pair_ok, starts, ends, xs, w_gate, w_up, w_down)


def _combine_kernel(yk_ref, x1_ref, w_ref, g2_ref, fg_ref, o_ref):
    tm = x1_ref.shape[0]
    w = w_ref[...]
    parts = None
    for k in range(TOP_K):
        lo, hi = _load_packed(yk_ref.at[k], tm)
        wk = w[:, k:k + 1]
        scaled = [piece * wk for piece in lo + hi]
        parts = scaled if parts is None else [a + b for a, b in zip(parts, scaled)]
    x2 = x1_ref[...] + g2_ref[0] * jnp.concatenate(parts, axis=1)
    ms = jnp.mean(x2 * x2, axis=-1, keepdims=True)
    o_ref[...] = x2 * lax.rsqrt(ms + EPS) * fg_ref[...]


def _combine_call(y_tok, x1s, w_tok, g2, fgain, seq):
    n, d = x1s.shape
    tm = MOVE_TILE
    per_b = seq // tm
    return pl.pallas_call(
        _combine_kernel,
        out_shape=jax.ShapeDtypeStruct((n, d), F32),
        grid=(n // tm,),
        in_specs=[pl.BlockSpec((TOP_K, tm * PACK_ROWS, LANES), lambda i: (0, i, 0)),
                  pl.BlockSpec((tm, d), lambda i: (i, 0)),
                  pl.BlockSpec((tm, TOP_K), lambda i: (i, 0)),
                  pl.BlockSpec((1, 1, d), lambda i: (i // per_b, 0, 0)),
                  pl.BlockSpec((1, d), lambda i: (0, 0))],
        out_specs=pl.BlockSpec((tm, d), lambda i: (i, 0)),
        compiler_params=pltpu.CompilerParams(
            dimension_semantics=("arbitrary",), vmem_limit_bytes=VMEM_LIMIT),
        name="combine",
    )(y_tok, x1s, w_tok, g2, fgain)


def _pair_tables(counts, n_rows):
    ne = counts.shape[0]
    sizes = counts.astype(I32)
    ends = jnp.cumsum(sizes)
    starts = ends - sizes
    first_blk = starts // ROW_BLOCK
    last_blk = (ends - 1) // ROW_BLOCK
    n_pairs = jnp.where(sizes > 0, last_blk - first_blk + 1, 0)
    pair_end = jnp.cumsum(n_pairs)
    pair_start = pair_end - n_pairs
    max_pairs = n_rows // ROW_BLOCK + ne
    p = jnp.arange(max_pairs, dtype=I32)
    ok = p < pair_end[-1]
    pc = jnp.minimum(p, pair_end[-1] - 1)
    pair_e = jnp.minimum(jnp.searchsorted(pair_end, pc, side='right'), ne - 1).astype(I32)
    pair_blk = (first_blk[pair_e] + pc - pair_start[pair_e]).astype(I32)
    return pair_e, pair_blk, ok.astype(I32), starts.astype(I32), ends.astype(I32)


def kernel(x, c, w_ada, b_ada, w_in, lb_logits, hgrn_norm, lam_re, lam_im, log_dt, b_re, b_im,
           c_re, c_im, d_skip, w_glu, b_glu, w_out, w_router, router_bias, w_gate, w_up, w_down,
           ws_gate, ws_up, ws_down, final_gain):
    bsz, seq, d = x.shape
    n = bsz * seq
    fdim = HGRN_HEADS * HGRN_KDIM
    n_chunks = seq // CHUNK
    lb = jnp.cumsum(jax.nn.softmax(lb_logits.astype(F32), axis=0), axis=0)[0].reshape(1, fdim)

    mod = _mod_call(c, w_ada[0], b_ada[0]).reshape(bsz, 6, d)
    u, out_a = _mix_front_call(x, mod, w_in[0].astype(BF16), lb, hgrn_norm[0].reshape(1, fdim))

    nb = u.shape[-1]
    groups = nb // S5_GROUP
    m_tab, p_tab, q_tab, a1, a2 = _s5_tables(lam_re[0], lam_im[0], log_dt[0], b_re[0], b_im[0],
                                             c_re[0], c_im[0])
    ut = u.reshape(bsz, n_chunks, CHUNK, groups, S5_GROUP).transpose(3, 1, 0, 2, 4)
    ut = ut.reshape(groups, n_chunks * bsz, CHUNK * S5_GROUP).astype(BF16)
    yt = _s5_call(ut, m_tab, p_tab, q_tab, a1, a2, n_chunks, bsz)
    ys = yt.reshape(groups, n_chunks, bsz, CHUNK, S5_GROUP).transpose(2, 1, 3, 0, 4).reshape(n, nb)

    wr_t = w_router[0].T
    wr_hi = wr_t.astype(BF16)
    wr_lo = (wr_t - wr_hi.astype(F32)).astype(BF16)
    x1s, hp, logits_t = _mix_back_call(
        x.reshape(n, d), out_a.reshape(n, fdim), ys, u.reshape(n, nb), mod,
        d_skip[0].reshape(1, nb), w_glu[0].astype(BF16), b_glu[0].reshape(1, nb),
        w_out[0].astype(BF16), wr_hi, wr_lo, ws_gate[0].astype(BF16), ws_up[0].astype(BF16),
        ws_down[0].astype(BF16), seq)

    top_idx, top_w, rank, counts = _route_call(logits_t, router_bias[0])
    pair_e, pair_blk, pair_ok, starts, ends = _pair_tables(counts[:, 0], n * TOP_K)
    pos = _pos_call(top_idx, rank, starts)

    pos_win = pos.reshape(TOP_K, n // SC_WINDOW, SC_WINDOW).transpose(1, 0, 2)
    row3 = (PACK_ROWS, LANES)
    xs = _dispatch_call(pos_win, hp.reshape((n,) + row3))
    y_sorted = _experts_call(pair_e, pair_blk, pair_ok, starts, ends,
                             xs.reshape(n * TOP_K * PACK_ROWS, LANES), w_gate[0], w_up[0], w_down[0])
    y_tok = _collect_call(pos_win, y_sorted.reshape((n * TOP_K,) + row3))
    out = _combine_call(y_tok.reshape(TOP_K, n * PACK_ROWS, LANES), x1s, top_w.T, mod[:, 5:6, :],
                        final_gain.reshape(1, d), seq)
    return out.reshape(bsz, seq, d)
```

```python
import functools

import jax
import jax.numpy as jnp
from jax import lax
from jax.experimental import pallas as pl
from jax.experimental.pallas import tpu as pltpu
from jax.experimental.pallas import tpu_sc as plsc

F32 = jnp.float32
BF16 = jnp.bfloat16
I32 = jnp.int32

EPS = 1e-6
CHUNK = 64
HGRN_HEADS = 4
HGRN_KDIM = 128
S5_GROUP = 16
S5_STATE = 64
N_EXPERT_GROUPS = 8
TOPK_GROUPS = 4
TOP_K = 8
ROUTE_SCALE = 2.5
LANES = 128
PACK_ROWS = 4
SC_CORES = 2
SC_SUBCORES = 16
SC_WINDOW = 64

SEQ_TILE = 512
TOK_TILE = 512
ROUTE_TILE = 256
POS_TILE = 1024
MOVE_TILE = 256
ROW_BLOCK = 512
VMEM_LIMIT = 56 * 1024 * 1024

_NT = (((1,), (1,)), ((), ()))
_TN = (((0,), (0,)), ((), ()))


def _sigmoid(v):
    return 1.0 / (1.0 + jnp.exp(-v))


def _silu(v):
    return v * _sigmoid(v)


def _bdot(a, b):
    return jnp.dot(a.astype(BF16), b.astype(BF16), preferred_element_type=F32)


def _store_packed(ref, val, n_rows):
    for j in range(PACK_ROWS):
        lo = val[:, 2 * j * LANES:(2 * j + 1) * LANES]
        hi = val[:, (2 * j + 1) * LANES:(2 * j + 2) * LANES]
        ref[pl.ds(j, n_rows, stride=PACK_ROWS), :] = pltpu.pack_elementwise([lo, hi], packed_dtype=BF16)


def _load_packed_f32(ref, n_rows):
    pieces = []
    for j in range(PACK_ROWS):
        w = ref[pl.ds(j, n_rows, stride=PACK_ROWS), :]
        for half in range(2):
            pieces.append(pltpu.unpack_elementwise(w, index=half, packed_dtype=BF16, unpacked_dtype=F32))
    return pieces


def _load_packed_bf16(ref, n_rows):
    pieces = []
    for j in range(PACK_ROWS):
        w = ref[pl.ds(j, n_rows, stride=PACK_ROWS), :]
        pieces.append(lax.bitcast_convert_type(w.astype(jnp.int16), BF16))
        pieces.append(lax.bitcast_convert_type(lax.shift_right_logical(w, 16).astype(jnp.int16), BF16))
    return jnp.concatenate(pieces, axis=1)


def _mod_kernel(c_ref, w_ref, b_ref, o_ref):
    o_ref[...] = _bdot(_silu(c_ref[...]), w_ref[...]) + b_ref[...]


def _mod_call(c, w_ada, b_ada):
    bsz, d = c.shape
    n_out = w_ada.shape[1]
    return pl.pallas_call(
        _mod_kernel,
        out_shape=jax.ShapeDtypeStruct((bsz, n_out), F32),
        grid=(n_out // d,),
        in_specs=[pl.BlockSpec((bsz, d), lambda j: (0, 0)),
                  pl.BlockSpec((d, d), lambda j: (0, j)),
                  pl.BlockSpec((1, d), lambda j: (0, j))],
        out_specs=pl.BlockSpec((bsz, d), lambda j: (0, j)),
        compiler_params=pltpu.CompilerParams(vmem_limit_bytes=VMEM_LIMIT),
        name="mod",
    )(c, w_ada, b_ada.reshape(1, n_out))


def _mix_front_kernel(x_ref, mod_ref, win_ref, lb_ref, gn_ref, ltri_ref,
                      u_ref, oa_ref, proj_ref, st_ref):
    fdim = HGRN_HEADS * HGRN_KDIM
    ts = x_ref.shape[1]

    @pl.when(pl.program_id(1) == 0)
    def _():
        st_ref[...] = jnp.zeros_like(st_ref)

    x = x_ref[0]
    ms = jnp.mean(x * x, axis=-1, keepdims=True)
    h = x * lax.rsqrt(ms + EPS) * (1.0 + mod_ref[0, 1:2, :]) + mod_ref[0, 0:1, :]
    proj_ref[...] = jnp.dot(h.astype(BF16), win_ref[...], preferred_element_type=F32)
    u_ref[0] = proj_ref[:, 4 * fdim:5 * fdim]

    lb = lb_ref[...]
    gn = gn_ref[...]
    ltri = ltri_ref[...]
    row = lax.broadcasted_iota(I32, (CHUNK, CHUNK), 0)
    col = lax.broadcasted_iota(I32, (CHUNK, CHUNK), 1)
    causal = row >= col

    def chunk_step(ci, carry):
        r0 = pl.multiple_of(ci * CHUNK, CHUNK)
        q = proj_ref[pl.ds(r0, CHUNK), 0:fdim]
        fl = proj_ref[pl.ds(r0, CHUNK), fdim:2 * fdim]
        iv = proj_ref[pl.ds(r0, CHUNK), 2 * fdim:3 * fdim]
        og = proj_ref[pl.ds(r0, CHUNK), 3 * fdim:4 * fdim]
        f = lb + (1.0 - lb) * _sigmoid(fl)
        b = jnp.dot(ltri, jnp.log(f), precision=lax.Precision.HIGHEST,
                    preferred_element_type=F32)
        b_ref = b[CHUNK // 2 - 1:CHUNK // 2, :]
        b_last = b[CHUNK - 1:CHUNK, :]
        qs = _silu(q)
        kk = 1.0 - f
        qe = (qs * jnp.exp(b - b_ref)).astype(BF16)
        ke = (kk * jnp.exp(b_ref - b)).astype(BF16)
        qb = (qs * jnp.exp(b)).astype(BF16)
        k2 = (kk * jnp.exp(b_last - b)).astype(BF16)
        dec = jnp.exp(b_last)
        ivb = iv.astype(BF16)
        outs = []
        for hh in range(HGRN_HEADS):
            sl = slice(hh * HGRN_KDIM, (hh + 1) * HGRN_KDIM)
            att = lax.dot_general(qe[:, sl], ke[:, sl], _NT, preferred_element_type=F32)
            att = jnp.where(causal, att, 0.0)
            st = st_ref[hh]
            o = jnp.dot(att.astype(BF16), ivb[:, sl], preferred_element_type=F32)
            o = o + lax.dot_general(qb[:, sl], st.astype(BF16), _NT, preferred_element_type=F32)
            st_ref[hh] = st * dec[:, sl] + lax.dot_general(
                ivb[:, sl], k2[:, sl], _TN, preferred_element_type=F32)
            outs.append(o * lax.rsqrt(jnp.mean(o * o, axis=-1, keepdims=True) + EPS))
        o = jnp.concatenate(outs, axis=1) * gn * _silu(og)
        oa_ref[0, pl.ds(r0, CHUNK), :] = o.astype(BF16)
        return carry

    lax.fori_loop(0, ts // CHUNK, chunk_step, 0, unroll=True)


def _mix_front_call(x, mod, w_in_b, lb, gn):
    bsz, seq, d = x.shape
    fdim = HGRN_HEADS * HGRN_KDIM
    ncols = w_in_b.shape[1]
    ltri = jnp.tril(jnp.ones((CHUNK, CHUNK), F32))
    ts = SEQ_TILE
    return pl.pallas_call(
        _mix_front_kernel,
        out_shape=(jax.ShapeDtypeStruct((bsz, seq, ncols - 4 * fdim), F32),
                   jax.ShapeDtypeStruct((bsz, seq, fdim), BF16)),
        grid=(bsz, seq // ts),
        in_specs=[pl.BlockSpec((1, ts, d), lambda b, j: (b, j, 0)),
                  pl.BlockSpec((1, 6, d), lambda b, j: (b, 0, 0)),
                  pl.BlockSpec((d, ncols), lambda b, j: (0, 0)),
                  pl.BlockSpec((1, fdim), lambda b, j: (0, 0)),
                  pl.BlockSpec((1, fdim), lambda b, j: (0, 0)),
                  pl.BlockSpec((CHUNK, CHUNK), lambda b, j: (0, 0))],
        out_specs=(pl.BlockSpec((1, ts, ncols - 4 * fdim), lambda b, j: (b, j, 0)),
                   pl.BlockSpec((1, ts, fdim), lambda b, j: (b, j, 0))),
        scratch_shapes=[pltpu.VMEM((ts, ncols), F32),
                        pltpu.VMEM((HGRN_HEADS, fdim // HGRN_HEADS, HGRN_KDIM), F32)],
        compiler_params=pltpu.CompilerParams(
            dimension_semantics=("arbitrary", "arbitrary"), vmem_limit_bytes=VMEM_LIMIT),
        name="mix_front",
    )(x, mod, w_in_b, lb, gn, ltri)


def _s5_tables(lam_re, lam_im, log_dt, b_re, b_im, c_re, c_im):
    t = CHUNK
    hp = lax.Precision.HIGHEST
    lam = lax.complex(jnp.minimum(lam_re, -1e-4), lam_im)
    lam_dt = lam * jnp.exp(log_dt)[:, None]
    lam_bar = jnp.exp(lam_dt)
    b_bar = ((lam_bar - 1.0) / lam)[..., None] * lax.complex(b_re, b_im)
    c_mat = lax.complex(c_re, c_im)
    taus = jnp.arange(t + 1, dtype=F32)
    lam_pow = jnp.exp(lam_dt[:, None, :] * taus[None, :, None])
    g, p = lam.shape
    c = b_re.shape[-1]
    cl = c_mat[:, None, :, :] * lam_pow[:, :t, None, :]
    kr = (jnp.einsum('gtcp,gpi->gtci', cl.real, b_bar.real, precision=hp)
          - jnp.einsum('gtcp,gpi->gtci', cl.imag, b_bar.imag, precision=hp))
    z = jnp.pad(kr.transpose(0, 3, 2, 1).astype(BF16), ((0, 0), (0, 0), (0, 0), (t - 1, 0)))
    kg = jnp.stack([z[..., t - 1 - s:2 * t - 1 - s] for s in range(t)], axis=1)
    m = kg.transpose(0, 1, 2, 4, 3).reshape(g, t * c, t * c)
    pc = lam_pow[:, t - 1::-1][:, :t, :, None] * b_bar[:, None, :, :]
    pc = pc.transpose(0, 1, 3, 2).reshape(g, t * c, p)
    p_tab = jnp.concatenate([pc.real, pc.imag], axis=-1)
    ql = c_mat[:, None, :, :] * lam_pow[:, 1:t + 1, None, :]
    ql = ql.transpose(0, 3, 1, 2).reshape(g, p, t * c)
    q_tab = jnp.concatenate([ql.real, -ql.imag], axis=1)
    lam_t = lam_pow[:, t]
    a1 = jnp.concatenate([lam_t.real, lam_t.real], axis=-1)[:, None, :]
    a2 = jnp.concatenate([-lam_t.imag, lam_t.imag], axis=-1)[:, None, :]
    return m.astype(BF16), p_tab.astype(BF16), q_tab.astype(BF16), a1, a2


def _s5_kernel(u_ref, m_ref, p_ref, q_ref, a1_ref, a2_ref, y_ref, v_ref, xs_ref, *, n_chunks, n_batch):
    u = u_ref[0]
    v_ref[...] = jnp.dot(u, p_ref[0], preferred_element_type=F32)
    a1 = a1_ref[0]
    a2 = a2_ref[0]
    half = xs_ref.shape[1] // 2

    def step(n, state):
        r0 = pl.multiple_of(n * n_batch, n_batch)
        xs_ref[pl.ds(r0, n_batch), :] = state
        return a1 * state + a2 * pltpu.roll(state, half, 1) + v_ref[pl.ds(r0, n_batch), :]

    lax.fori_loop(0, n_chunks, step, jnp.zeros((n_batch, xs_ref.shape[1]), F32))
    y = jnp.dot(u, m_ref[0], preferred_element_type=F32)
    y = y + jnp.dot(xs_ref[...].astype(BF16), q_ref[0], preferred_element_type=F32)
    y_ref[0] = y.astype(BF16)


def _s5_call(ut, m, p_tab, q_tab, a1, a2, n_chunks, n_batch):
    g, rows, width = ut.shape
    p2 = p_tab.shape[-1]
    return pl.pallas_call(
        functools.partial(_s5_kernel, n_chunks=n_chunks, n_batch=n_batch),
        out_shape=jax.ShapeDtypeStruct((g, rows, width), BF16),
        grid=(g,),
        in_specs=[pl.BlockSpec((1, rows, width), lambda i: (i, 0, 0)),
                  pl.BlockSpec((1, width, width), lambda i: (i, 0, 0)),
                  pl.BlockSpec((1, width, p2), lambda i: (i, 0, 0)),
                  pl.BlockSpec((1, p2, width), lambda i: (i, 0, 0)),
                  pl.BlockSpec((1, 1, p2), lambda i: (i, 0, 0)),
                  pl.BlockSpec((1, 1, p2), lambda i: (i, 0, 0))],
        out_specs=pl.BlockSpec((1, rows, width), lambda i: (i, 0, 0)),
        scratch_shapes=[pltpu.VMEM((rows, p2), F32), pltpu.VMEM((rows, p2), F32)],
        compiler_params=pltpu.CompilerParams(
            dimension_semantics=("arbitrary",), vmem_limit_bytes=VMEM_LIMIT),
        name="s5",
    )(ut, m, p_tab, q_tab, a1, a2)


def _mix_back_kernel(x_ref, oa_ref, ys_ref, u_ref, mod_ref, dskip_ref, wglu_ref, bglu_ref,
                     wout_ref, wrh_ref, wrl_ref, wsg_ref, wsu_ref, wsd_ref,
                     x1_ref, hp_ref, lt_ref):
    na = oa_ref.shape[1]
    y = ys_ref[...].astype(F32) + dskip_ref[...] * u_ref[...]
    y = jax.nn.gelu(y)
    ob = y * _sigmoid(_bdot(y, wglu_ref[...]) + bglu_ref[...])
    mixed = (jnp.dot(oa_ref[...], wout_ref[0:na, :], preferred_element_type=F32)
             + jnp.dot(ob.astype(BF16), wout_ref[na:, :], preferred_element_type=F32))
    x1 = x_ref[...] + mod_ref[0, 2:3, :] * mixed
    ms = jnp.mean(x1 * x1, axis=-1, keepdims=True)
    h2 = x1 * lax.rsqrt(ms + EPS) * (1.0 + mod_ref[0, 4:5, :]) + mod_ref[0, 3:4, :]
    _store_packed(hp_ref, h2, h2.shape[0])
    h_hi = h2.astype(BF16)
    h_lo = (h2 - h_hi.astype(F32)).astype(BF16)
    lt = lax.dot_general(wrh_ref[...], h_hi, _NT, preferred_element_type=F32)
    lt = lt + lax.dot_general(wrl_ref[...], h_hi, _NT, preferred_element_type=F32)
    lt = lt + lax.dot_general(wrh_ref[...], h_lo, _NT, preferred_element_type=F32)
    lt_ref[...] = lt
    hid = _silu(jnp.dot(h_hi, wsg_ref[...], preferred_element_type=F32)) * jnp.dot(
        h_hi, wsu_ref[...], preferred_element_type=F32)
    shared = jnp.dot(hid.astype(BF16), wsd_ref[...], preferred_element_type=F32)
    x1_ref[...] = x1 + mod_ref[0, 5:6, :] * shared


def _mix_back_call(x2d, oa, ys, u, mod, dskip, wglu, bglu, wout, wrh, wrl, wsg, wsu, wsd, seq):
    n, d = x2d.shape
    nb = oa.shape[1]
    ne = wrh.shape[0]
    dsh = wsg.shape[1]
    tm = TOK_TILE
    per_b = seq // tm
    const = lambda i: (0, 0)
    return pl.pallas_call(
        _mix_back_kernel,
        out_shape=(jax.ShapeDtypeStruct((n, d), F32),
                   jax.ShapeDtypeStruct((n * PACK_ROWS, LANES), I32),
                   jax.ShapeDtypeStruct((ne, n), F32)),
        grid=(n // tm,),
        in_specs=[pl.BlockSpec((tm, d), lambda i: (i, 0)),
                  pl.BlockSpec((tm, nb), lambda i: (i, 0)),
                  pl.BlockSpec((tm, nb), lambda i: (i, 0)),
                  pl.BlockSpec((tm, nb), lambda i: (i, 0)),
                  pl.BlockSpec((1, 6, d), lambda i: (i // per_b, 0, 0)),
                  pl.BlockSpec((1, nb), const),
                  pl.BlockSpec((nb, nb), const),
                  pl.BlockSpec((1, nb), const),
                  pl.BlockSpec((d, d), const),
                  pl.BlockSpec((ne, d), const),
                  pl.BlockSpec((ne, d), const),
                  pl.BlockSpec((d, dsh), const),
                  pl.BlockSpec((d, dsh), const),
                  pl.BlockSpec((dsh, d), const)],
        out_specs=(pl.BlockSpec((tm, d), lambda i: (i, 0)),
                   pl.BlockSpec((tm * PACK_ROWS, LANES), lambda i: (i, 0)),
                   pl.BlockSpec((ne, tm), lambda i: (0, i))),
        compiler_params=pltpu.CompilerParams(
            dimension_semantics=("arbitrary",), vmem_limit_bytes=VMEM_LIMIT),
        name="mix_back",
    )(x2d, oa, ys, u, mod, dskip, wglu, bglu, wout, wrh, wrl, wsg, wsu, wsd)


def _route_kernel(lt_ref, bias_ref, su_ref, idx_ref, w_ref, rank_ref, cnt_ref, run_ref):
    ne, tr = lt_ref.shape
    per_group = ne // N_EXPERT_GROUPS
    neg = -jnp.inf

    @pl.when(pl.program_id(0) == 0)
    def _():
        run_ref[...] = jnp.zeros_like(run_ref)

    s = _sigmoid(lt_ref[...])
    sel = s + bias_ref[...]
    gio = lax.broadcasted_iota(I32, (per_group, tr), 0)
    gscore = []
    for g in range(N_EXPERT_GROUPS):
        v = sel[g * per_group:(g + 1) * per_group, :]
        m1 = jnp.max(v, axis=0, keepdims=True)
        i1 = jnp.min(jnp.where(v == m1, gio, per_group), axis=0, keepdims=True)
        m2 = jnp.max(jnp.where(gio == i1, neg, v), axis=0, keepdims=True)
        gscore.append(m1 + m2)
    masked = []
    for g in range(N_EXPERT_GROUPS):
        ahead = jnp.zeros((1, tr), I32)
        for o in range(N_EXPERT_GROUPS):
            if o == g:
                continue
            wins = (gscore[o] >= gscore[g]) if o < g else (gscore[o] > gscore[g])
            ahead = ahead + wins.astype(I32)
        keep = ahead < TOPK_GROUPS
        masked.append(jnp.where(keep, sel[g * per_group:(g + 1) * per_group, :], neg))
    selm = jnp.concatenate(masked, axis=0)
    eio = lax.broadcasted_iota(I32, (ne, tr), 0)
    hits = jnp.zeros((ne, tr), F32)
    idxs, ws = [], []
    for k in range(TOP_K):
        m = jnp.max(selm, axis=0, keepdims=True)
        ik = jnp.min(jnp.where(selm == m, eio, ne), axis=0, keepdims=True)
        onehot = eio == ik
        ws.append(jnp.sum(jnp.where(onehot, s, 0.0), axis=0, keepdims=True))
        hits = jnp.where(onehot, 1.0, hits)
        selm = jnp.where(onehot, neg, selm)
        idxs.append(ik)
    wsum = ws[0]
    for k in range(1, TOP_K):
        wsum = wsum + ws[k]
    scale = ROUTE_SCALE / wsum
    ranks = jnp.dot(hits.astype(BF16), su_ref[...], preferred_element_type=F32) + run_ref[...]
    for k in range(TOP_K):
        idx_ref[k:k + 1, :] = idxs[k]
        w_ref[k:k + 1, :] = ws[k] * scale
        rk = jnp.sum(jnp.where(eio == idxs[k], ranks, 0.0), axis=0, keepdims=True)
        rank_ref[k:k + 1, :] = rk.astype(I32)
    run_ref[...] = run_ref[...] + jnp.sum(hits, axis=1, keepdims=True)
    cnt_ref[...] = run_ref[...]


def _route_call(lt, bias):
    ne, n = lt.shape
    tr = ROUTE_TILE
    su = jnp.triu(jnp.ones((tr, tr), F32), k=1).astype(BF16)
    return pl.pallas_call(
        _route_kernel,
        out_shape=(jax.ShapeDtypeStruct((TOP_K, n), I32),
                   jax.ShapeDtypeStruct((TOP_K, n), F32),
                   jax.ShapeDtypeStruct((TOP_K, n), I32),
                   jax.ShapeDtypeStruct((ne, 1), F32)),
        grid=(n // tr,),
        in_specs=[pl.BlockSpec((ne, tr), lambda i: (0, i)),
                  pl.BlockSpec((ne, 1), lambda i: (0, 0)),
                  pl.BlockSpec((tr, tr), lambda i: (0, 0))],
        out_specs=(pl.BlockSpec((TOP_K, tr), lambda i: (0, i)),
                   pl.BlockSpec((TOP_K, tr), lambda i: (0, i)),
                   pl.BlockSpec((TOP_K, tr), lambda i: (0, i)),
                   pl.BlockSpec((ne, 1), lambda i: (0, 0))),
        scratch_shapes=[pltpu.VMEM((ne, 1), F32)],
        compiler_params=pltpu.CompilerParams(
            dimension_semantics=("arbitrary",), vmem_limit_bytes=VMEM_LIMIT),
        name="route",
    )(lt, bias.reshape(ne, 1), su)


def _pos_kernel(idx_ref, rank_ref, st_ref, pos_ref):
    ne = st_ref.shape[0]
    tp = idx_ref.shape[1]
    eio = lax.broadcasted_iota(I32, (ne, tp), 0)
    st = st_ref[...]
    for k in range(TOP_K):
        base = jnp.sum(jnp.where(eio == idx_ref[k:k + 1, :], st, 0), axis=0, keepdims=True)
        pos_ref[k:k + 1, :] = base + rank_ref[k:k + 1, :]


def _pos_call(top_idx, rank, starts):
    kk, n = top_idx.shape
    ne = starts.shape[0]
    tp = POS_TILE
    return pl.pallas_call(
        _pos_kernel,
        out_shape=jax.ShapeDtypeStruct((kk, n), I32),
        grid=(n // tp,),
        in_specs=[pl.BlockSpec((kk, tp), lambda i: (0, i)),
                  pl.BlockSpec((kk, tp), lambda i: (0, i)),
                  pl.BlockSpec((ne, 1), lambda i: (0, 0))],
        out_specs=pl.BlockSpec((kk, tp), lambda i: (0, i)),
        compiler_params=pltpu.CompilerParams(
            dimension_semantics=("arbitrary",), vmem_limit_bytes=VMEM_LIMIT),
        name="pos",
    )(top_idx, rank, starts.reshape(ne, 1))


def _sc_mesh():
    return plsc.VectorSubcoreMesh(core_axis_name="c", subcore_axis_name="s",
                                  num_cores=SC_CORES, num_subcores=SC_SUBCORES)


def _sc_worker():
    return lax.axis_index("s") * SC_CORES + lax.axis_index("c")


def _dispatch_call(pos_win, hp):
    n = hp.shape[0]
    n_workers = SC_CORES * SC_SUBCORES
    wins_per_worker = n // SC_WINDOW // n_workers

    def body(hp_hbm, pos_hbm, xs_hbm, idx_v, rows_v, sem):
        first_win = _sc_worker() * wins_per_worker

        @pl.loop(0, wins_per_worker)
        def _(w):
            win = first_win + w
            pltpu.sync_copy(hp_hbm.at[pl.ds(win * SC_WINDOW, SC_WINDOW)], rows_v)
            pltpu.sync_copy(pos_hbm.at[win], idx_v)
            copies = [pltpu.async_copy(rows_v, xs_hbm.at[idx_v.at[k]], sem) for k in range(TOP_K)]
            for cp in copies:
                cp.wait()

    return pl.kernel(
        body,
        out_type=jax.ShapeDtypeStruct((n * TOP_K,) + hp.shape[1:], hp.dtype),
        mesh=_sc_mesh(),
        scratch_types=[pltpu.VMEM((TOP_K, SC_WINDOW), I32),
                       pltpu.VMEM((SC_WINDOW,) + hp.shape[1:], hp.dtype),
                       pltpu.SemaphoreType.DMA],
        name="dispatch",
    )(hp, pos_win)


def _collect_call(pos_win, y_sorted):
    n = pos_win.shape[0] * SC_WINDOW
    n_workers = SC_CORES * SC_SUBCORES
    wins_per_worker = n // SC_WINDOW // n_workers

    def body(ys_hbm, pos_hbm, out_hbm, idx_v, rows_a, rows_b, gather_sem, write_sem):
        first_win = _sc_worker() * wins_per_worker
        bufs = (rows_a, rows_b)

        @pl.loop(0, wins_per_worker)
        def _(w):
            win = first_win + w
            pltpu.sync_copy(pos_hbm.at[win], idx_v)

            def gather(k):
                return pltpu.async_copy(ys_hbm.at[idx_v.at[k]], bufs[k % 2], gather_sem)

            pending_gather = gather(0)
            pending_write = None
            for k in range(TOP_K):
                pending_gather.wait()
                if pending_write is not None:
                    pending_write.wait()
                if k + 1 < TOP_K:
                    pending_gather = gather(k + 1)
                pending_write = pltpu.async_copy(
                    bufs[k % 2], out_hbm.at[k, pl.ds(win * SC_WINDOW, SC_WINDOW)], write_sem)
            pending_write.wait()

    row_buf = pltpu.VMEM((SC_WINDOW,) + y_sorted.shape[1:], y_sorted.dtype)
    return pl.kernel(
        body,
        out_type=jax.ShapeDtypeStruct((TOP_K, n) + y_sorted.shape[1:], y_sorted.dtype),
        mesh=_sc_mesh(),
        scratch_types=[pltpu.VMEM((TOP_K, SC_WINDOW), I32), row_buf, row_buf,
                       pltpu.SemaphoreType.DMA, pltpu.SemaphoreType.DMA],
        name="collect",
    )(y_sorted, pos_win)


def _experts_kernel(pe_ref, pb_ref, pv_ref, st_ref, en_ref,
                    xs_ref, wg_ref, wu_ref, wd_ref, y_ref, wgb_ref, wub_ref, wdb_ref, acc_ref):
    p = pl.program_id(0)
    n_pairs = pl.num_programs(0)
    e = pe_ref[p]
    blk = pb_ref[p]
    rb = acc_ref.shape[0]
    prev = jnp.maximum(p - 1, 0)
    nxt = jnp.minimum(p + 1, n_pairs - 1)
    live = pv_ref[p] == 1
    first = jnp.logical_or(p == 0, pb_ref[prev] != blk)
    last = jnp.logical_or(p == n_pairs - 1,
                          jnp.logical_or(pb_ref[nxt] != blk, pv_ref[nxt] == 0))

    @pl.when(jnp.logical_or(p == 0, pe_ref[prev] != e))
    def _():
        wgb_ref[...] = wg_ref[0].astype(BF16)
        wub_ref[...] = wu_ref[0].astype(BF16)
        wdb_ref[...] = wd_ref[0].astype(BF16)

    @pl.when(live)
    def _():
        xb = _load_packed_bf16(xs_ref, rb)
        gate = jnp.dot(xb, wgb_ref[...], preferred_element_type=F32)
        up = jnp.dot(xb, wub_ref[...], preferred_element_type=F32)
        rows = blk * rb + lax.broadcasted_iota(I32, (rb, 1), 0)
        mine = jnp.logical_and(rows >= st_ref[e], rows < en_ref[e])
        hid = jnp.where(mine, _silu(gate) * up, 0.0).astype(BF16)
        yb = jnp.dot(hid, wdb_ref[...], preferred_element_type=F32)
        not_first = jnp.logical_not(first)
        not_last = jnp.logical_not(last)

        @pl.when(jnp.logical_and(first, last))
        def _():
            _store_packed(y_ref, yb, rb)

        @pl.when(jnp.logical_and(first, not_last))
        def _():
            acc_ref[...] = yb

        @pl.when(jnp.logical_and(not_first, not_last))
        def _():
            acc_ref[...] += yb

        @pl.when(jnp.logical_and(not_first, last))
        def _():
            _store_packed(y_ref, acc_ref[...] + yb, rb)


def _experts_call(pair_e, pair_blk, pair_ok, starts, ends, xs, w_gate, w_up, w_down):
    ne, d, de = w_gate.shape
    rb = ROW_BLOCK
    n_pairs = pair_e.shape[0]
    grid_spec = pltpu.PrefetchScalarGridSpec(
        num_scalar_prefetch=5,
        grid=(n_pairs,),
        in_specs=[pl.BlockSpec((rb * PACK_ROWS, LANES), lambda p, pe, pb, pv, st, en: (pb[p], 0)),
                  pl.BlockSpec((1, d, de), lambda p, pe, pb, pv, st, en: (pe[p], 0, 0)),
                  pl.BlockSpec((1, d, de), lambda p, pe, pb, pv, st, en: (pe[p], 0, 0)),
                  pl.BlockSpec((1, de, d), lambda p, pe, pb, pv, st, en: (pe[p], 0, 0))],
        out_specs=pl.BlockSpec((rb * PACK_ROWS, LANES), lambda p, pe, pb, pv, st, en: (pb[p], 0)),
        scratch_shapes=[pltpu.VMEM((d, de), BF16), pltpu.VMEM((d, de), BF16),
                        pltpu.VMEM((de, d), BF16), pltpu.VMEM((rb, d), F32)],
    )
    return pl.pallas_call(
        _experts_kernel,
        out_shape=jax.ShapeDtypeStruct(xs.shape, xs.dtype),
        grid_spec=grid_spec,
        compiler_params=pltpu.CompilerParams(
            dimension_semantics=("arbitrary",), vmem_limit_bytes=VMEM_LIMIT),
        name="experts",
    )(pair_e, pair_blk, pair_ok, starts, ends, xs, w_gate, w_up, w_down)


def _combine_kernel(yk_ref, x1_ref, w_ref, g2_ref, fg_ref, o_ref):
    tm = x1_ref.shape[0]
    w = w_ref[...]
    parts = None
    for k in range(TOP_K):
        wk = w[:, k:k + 1]
        scaled = [piece * wk for piece in _load_packed_f32(yk_ref.at[k], tm)]
        parts = scaled if parts is None else [a + b for a, b in zip(parts, scaled)]
    x2 = x1_ref[...] + g2_ref[0] * jnp.concatenate(parts, axis=1)
    ms = jnp.mean(x2 * x2, axis=-1, keepdims=True)
    o_ref[...] = x2 * lax.rsqrt(ms + EPS) * fg_ref[...]


def _combine_call(y_tok, x1s, w_tok, g2, fgain, seq):
    n, d = x1s.shape
    tm = MOVE_TILE
    per_b = seq // tm
    return pl.pallas_call(
        _combine_kernel,
        out_shape=jax.ShapeDtypeStruct((n, d), F32),
        grid=(n // tm,),
        in_specs=[pl.BlockSpec((TOP_K, tm * PACK_ROWS, LANES), lambda i: (0, i, 0)),
                  pl.BlockSpec((tm, d), lambda i: (i, 0)),
                  pl.BlockSpec((tm, TOP_K), lambda i: (i, 0)),
                  pl.BlockSpec((1, 1, d), lambda i: (i // per_b, 0, 0)),
                  pl.BlockSpec((1, d), lambda i: (0, 0))],
        out_specs=pl.BlockSpec((tm, d), lambda i: (i, 0)),
        compiler_params=pltpu.CompilerParams(
            dimension_semantics=("arbitrary",), vmem_limit_bytes=VMEM_LIMIT),
        name="combine",
    )(y_tok, x1s, w_tok, g2, fgain)


def _pair_tables(counts, n_rows):
    ne = counts.shape[0]
    sizes = counts.astype(I32)
    ends = jnp.cumsum(sizes)
    starts = ends - sizes
    first_blk = starts // ROW_BLOCK
    last_blk = (ends - 1) // ROW_BLOCK
    n_pairs = jnp.where(sizes > 0, last_blk - first_blk + 1, 0)
    pair_end = jnp.cumsum(n_pairs)
    pair_start = pair_end - n_pairs
    max_pairs = n_rows // ROW_BLOCK + ne
    p = jnp.arange(max_pairs, dtype=I32)
    ok = p < pair_end[-1]
    pc = jnp.minimum(p, pair_end[-1] - 1)
    pair_e = jnp.minimum(jnp.searchsorted(pair_end, pc, side='right'), ne - 1).astype(I32)
    pair_blk = (first_blk[pair_e] + pc - pair_start[pair_e]).astype(I32)
    return pair_e, pair_blk, ok.astype(I32), starts.astype(I32), ends.astype(I32)


def kernel(x, c, w_ada, b_ada, w_in, lb_logits, hgrn_norm, lam_re, lam_im, log_dt, b_re, b_im,
           c_re, c_im, d_skip, w_glu, b_glu, w_out, w_router, router_bias, w_gate, w_up, w_down,
           ws_gate, ws_up, ws_down, final_gain):
    bsz, seq, d = x.shape
    n = bsz * seq
    fdim = HGRN_HEADS * HGRN_KDIM
    n_chunks = seq // CHUNK
    lb = jnp.cumsum(jax.nn.softmax(lb_logits.astype(F32), axis=0), axis=0)[0].reshape(1, fdim)

    mod = _mod_call(c, w_ada[0], b_ada[0]).reshape(bsz, 6, d)
    u, out_a = _mix_front_call(x, mod, w_in[0].astype(BF16), lb, hgrn_norm[0].reshape(1, fdim))

    nb = u.shape[-1]
    groups = nb // S5_GROUP
    m_tab, p_tab, q_tab, a1, a2 = _s5_tables(lam_re[0], lam_im[0], log_dt[0], b_re[0], b_im[0],
                                             c_re[0], c_im[0])
    ut = u.reshape(bsz, n_chunks, CHUNK, groups, S5_GROUP).transpose(3, 1, 0, 2, 4)
    ut = ut.reshape(groups, n_chunks * bsz, CHUNK * S5_GROUP).astype(BF16)
    yt = _s5_call(ut, m_tab, p_tab, q_tab, a1, a2, n_chunks, bsz)
    ys = yt.reshape(groups, n_chunks, bsz, CHUNK, S5_GROUP).transpose(2, 1, 3, 0, 4).reshape(n, nb)

    wr_t = w_router[0].T
    wr_hi = wr_t.astype(BF16)
    wr_lo = (wr_t - wr_hi.astype(F32)).astype(BF16)
    x1s, hp, logits_t = _mix_back_call(
        x.reshape(n, d), out_a.reshape(n, fdim), ys, u.reshape(n, nb), mod,
        d_skip[0].reshape(1, nb), w_glu[0].astype(BF16), b_glu[0].reshape(1, nb),
        w_out[0].astype(BF16), wr_hi, wr_lo, ws_gate[0].astype(BF16), ws_up[0].astype(BF16),
        ws_down[0].astype(BF16), seq)

    top_idx, top_w, rank, counts = _route_call(logits_t, router_bias[0])
    pair_e, pair_blk, pair_ok, starts, ends = _pair_tables(counts[:, 0], n * TOP_K)
    pos = _pos_call(top_idx, rank, starts)

    pos_win = pos.reshape(TOP_K, n // SC_WINDOW, SC_WINDOW).transpose(1, 0, 2)
    row3 = (PACK_ROWS, LANES)
    xs = _dispatch_call(pos_win, hp.reshape((n,) + row3))
    y_sorted = _experts_call(pair_e, pair_blk, pair_ok, starts, ends,
                             xs.reshape(n * TOP_K * PACK_ROWS, LANES), w_gate[0], w_up[0], w_down[0])
    y_tok = _collect_call(pos_win, y_sorted.reshape((n * TOP_K,) + row3))
    out = _combine_call(y_tok.reshape(TOP_K, n * PACK_ROWS, LANES), x1s, top_w.T, mod[:, 5:6, :],
                        final_gain.reshape(1, d), seq)
    return out.reshape(bsz, seq, d)
```

```python
import functools

import jax
import jax.numpy as jnp
from jax import lax
from jax.experimental import pallas as pl
from jax.experimental.pallas import tpu as pltpu
from jax.experimental.pallas import tpu_sc as plsc

F32 = jnp.float32
BF16 = jnp.bfloat16
I32 = jnp.int32

EPS = 1e-6
CHUNK = 64
HGRN_HEADS = 4
HGRN_KDIM = 128
S5_GROUP = 16
S5_STATE = 64
N_EXPERT_GROUPS = 8
TOPK_GROUPS = 4
TOP_K = 8
ROUTE_SCALE = 2.5
LANES = 128
PACK_ROWS = 4
SC_CORES = 2
SC_SUBCORES = 16
SC_WINDOW = 64

SEQ_TILE = 512
TOK_TILE = 512
ROUTE_TILE = 256
POS_TILE = 1024
MOVE_TILE = 256
ROW_BLOCK = 512
VMEM_LIMIT = 56 * 1024 * 1024

_NT = (((1,), (1,)), ((), ()))
_TN = (((0,), (0,)), ((), ()))


def _sigmoid(v):
    return 1.0 / (1.0 + jnp.exp(-v))


def _silu(v):
    return v * _sigmoid(v)


def _bdot(a, b):
    return jnp.dot(a.astype(BF16), b.astype(BF16), preferred_element_type=F32)


def _store_packed(ref, val, n_rows):
    for j in range(PACK_ROWS):
        lo = val[:, 2 * j * LANES:(2 * j + 1) * LANES]
        hi = val[:, (2 * j + 1) * LANES:(2 * j + 2) * LANES]
        ref[pl.ds(j, n_rows, stride=PACK_ROWS), :] = pltpu.pack_elementwise([lo, hi], packed_dtype=BF16)


def _load_packed_f32(ref, n_rows):
    pieces = []
    for j in range(PACK_ROWS):
        w = ref[pl.ds(j, n_rows, stride=PACK_ROWS), :]
        for half in range(2):
            pieces.append(pltpu.unpack_elementwise(w, index=half, packed_dtype=BF16, unpacked_dtype=F32))
    return pieces


def _load_packed_bf16(ref, n_rows):
    pieces = []
    for j in range(PACK_ROWS):
        w = ref[pl.ds(j, n_rows, stride=PACK_ROWS), :]
        pieces.append(lax.bitcast_convert_type(w.astype(jnp.int16), BF16))
        pieces.append(lax.bitcast_convert_type(lax.shift_right_logical(w, 16).astype(jnp.int16), BF16))
    return jnp.concatenate(pieces, axis=1)


def _mod_kernel(c_ref, w_ref, b_ref, o_ref):
    o_ref[...] = _bdot(_silu(c_ref[...]), w_ref[...]) + b_ref[...]


def _mod_call(c, w_ada, b_ada):
    bsz, d = c.shape
    n_out = w_ada.shape[1]
    return pl.pallas_call(
        _mod_kernel,
        out_shape=jax.ShapeDtypeStruct((bsz, n_out), F32),
        grid=(n_out // d,),
        in_specs=[pl.BlockSpec((bsz, d), lambda j: (0, 0)),
                  pl.BlockSpec((d, d), lambda j: (0, j)),
                  pl.BlockSpec((1, d), lambda j: (0, j))],
        out_specs=pl.BlockSpec((bsz, d), lambda j: (0, j)),
        compiler_params=pltpu.CompilerParams(vmem_limit_bytes=VMEM_LIMIT),
        name="mod",
    )(c, w_ada, b_ada.reshape(1, n_out))


def _split_chunk_pairs(tile_even, tile_odd):
    low = lax.broadcasted_iota(I32, tile_even.shape, 1) < CHUNK
    first = jnp.where(low, tile_even, pltpu.roll(tile_odd, CHUNK, 1))
    second = jnp.where(low, pltpu.roll(tile_even, CHUNK, 1), tile_odd)
    return first, second


def _mix_front_kernel(x_ref, mod_ref, win_ref, wut_ref, lb_ref, gn_ref, ltri_ref,
                      ut_ref, oa_ref, proj_ref, st_ref, flat_ref):
    fdim = HGRN_HEADS * HGRN_KDIM
    ts = x_ref.shape[1]
    pairs = wut_ref.shape[0] // 2
    rows = ts // CHUNK

    @pl.when(pl.program_id(1) == 0)
    def _():
        st_ref[...] = jnp.zeros_like(st_ref)

    x = x_ref[0]
    ms = jnp.mean(x * x, axis=-1, keepdims=True)
    h = x * lax.rsqrt(ms + EPS) * (1.0 + mod_ref[0, 1:2, :]) + mod_ref[0, 0:1, :]
    hb = h.astype(BF16)
    proj_ref[...] = jnp.dot(hb, win_ref[...], preferred_element_type=F32)
    u_t = lax.dot_general(wut_ref[...], hb, _NT, preferred_element_type=F32)
    for m in range(ts // LANES):
        first, second = _split_chunk_pairs(u_t[:pairs, m * LANES:(m + 1) * LANES],
                                           u_t[pairs:, m * LANES:(m + 1) * LANES])
        flat_ref[2 * m * pairs:(2 * m + 1) * pairs, :] = first
        flat_ref[(2 * m + 1) * pairs:(2 * m + 2) * pairs, :] = second
    per_group = S5_GROUP // 2
    for q in range(pairs):
        ut_ref[q // per_group, :, (q % per_group) * LANES:(q % per_group + 1) * LANES] = (
            flat_ref[pl.ds(q, rows, stride=pairs), :])

    lb = lb_ref[...]
    gn = gn_ref[...]
    ltri = ltri_ref[...]
    row = lax.broadcasted_iota(I32, (CHUNK, CHUNK), 0)
    col = lax.broadcasted_iota(I32, (CHUNK, CHUNK), 1)
    causal = row >= col

    def chunk_step(ci, carry):
        r0 = pl.multiple_of(ci * CHUNK, CHUNK)
        q = proj_ref[pl.ds(r0, CHUNK), 0:fdim]
        fl = proj_ref[pl.ds(r0, CHUNK), fdim:2 * fdim]
        iv = proj_ref[pl.ds(r0, CHUNK), 2 * fdim:3 * fdim]
        og = proj_ref[pl.ds(r0, CHUNK), 3 * fdim:4 * fdim]
        f = lb + (1.0 - lb) * _sigmoid(fl)
        b = jnp.dot(ltri, jnp.log(f), precision=lax.Precision.HIGHEST,
                    preferred_element_type=F32)
        b_ref = b[CHUNK // 2 - 1:CHUNK // 2, :]
        b_last = b[CHUNK - 1:CHUNK, :]
        qs = _silu(q)
        kk = 1.0 - f
        qe = (qs * jnp.exp(b - b_ref)).astype(BF16)
        ke = (kk * jnp.exp(b_ref - b)).astype(BF16)
        qb = (qs * jnp.exp(b)).astype(BF16)
        k2 = (kk * jnp.exp(b_last - b)).astype(BF16)
        dec = jnp.exp(b_last)
        ivb = iv.astype(BF16)
        outs = []
        for hh in range(HGRN_HEADS):
            sl = slice(hh * HGRN_KDIM, (hh + 1) * HGRN_KDIM)
            att = lax.dot_general(qe[:, sl], ke[:, sl], _NT, preferred_element_type=F32)
            att = jnp.where(causal, att, 0.0)
            st = st_ref[hh]
            o = jnp.dot(att.astype(BF16), ivb[:, sl], preferred_element_type=F32)
            o = o + lax.dot_general(qb[:, sl], st.astype(BF16), _NT, preferred_element_type=F32)
            st_ref[hh] = st * dec[:, sl] + lax.dot_general(
                ivb[:, sl], k2[:, sl], _TN, preferred_element_type=F32)
            outs.append(o * lax.rsqrt(jnp.mean(o * o, axis=-1, keepdims=True) + EPS))
        o = jnp.concatenate(outs, axis=1) * gn * _silu(og)
        oa_ref[0, pl.ds(r0, CHUNK), :] = o.astype(BF16)
        return carry

    lax.fori_loop(0, ts // CHUNK, chunk_step, 0, unroll=True)


def _mix_front_call(x, mod, w_main, w_ut, lb, gn):
    bsz, seq, d = x.shape
    fdim = HGRN_HEADS * HGRN_KDIM
    ncols = w_main.shape[1]
    nb = w_ut.shape[0]
    groups = nb // S5_GROUP
    ltri = jnp.tril(jnp.ones((CHUNK, CHUNK), F32))
    ts = SEQ_TILE
    tiles = seq // ts
    rows = ts // CHUNK
    return pl.pallas_call(
        _mix_front_kernel,
        out_shape=(jax.ShapeDtypeStruct((groups, bsz * seq // CHUNK, S5_GROUP * CHUNK), F32),
                   jax.ShapeDtypeStruct((bsz, seq, fdim), BF16)),
        grid=(bsz, tiles),
        in_specs=[pl.BlockSpec((1, ts, d), lambda b, j: (b, j, 0)),
                  pl.BlockSpec((1, 6, d), lambda b, j: (b, 0, 0)),
                  pl.BlockSpec((d, ncols), lambda b, j: (0, 0)),
                  pl.BlockSpec((nb, d), lambda b, j: (0, 0)),
                  pl.BlockSpec((1, fdim), lambda b, j: (0, 0)),
                  pl.BlockSpec((1, fdim), lambda b, j: (0, 0)),
                  pl.BlockSpec((CHUNK, CHUNK), lambda b, j: (0, 0))],
        out_specs=(pl.BlockSpec((groups, rows, S5_GROUP * CHUNK), lambda b, j: (0, b * tiles + j, 0)),
                   pl.BlockSpec((1, ts, fdim), lambda b, j: (b, j, 0))),
        scratch_shapes=[pltpu.VMEM((ts, ncols), F32),
                        pltpu.VMEM((HGRN_HEADS, fdim // HGRN_HEADS, HGRN_KDIM), F32),
                        pltpu.VMEM((rows * nb // 2, LANES), F32)],
        compiler_params=pltpu.CompilerParams(
            dimension_semantics=("arbitrary", "arbitrary"), vmem_limit_bytes=VMEM_LIMIT),
        name="mix_front",
    )(x, mod, w_main, w_ut, lb, gn, ltri)


def _s5_tables(lam_re, lam_im, log_dt, b_re, b_im, c_re, c_im):
    t = CHUNK
    hp = lax.Precision.HIGHEST
    lam = lax.complex(jnp.minimum(lam_re, -1e-4), lam_im)
    lam_dt = lam * jnp.exp(log_dt)[:, None]
    lam_bar = jnp.exp(lam_dt)
    b_bar = ((lam_bar - 1.0) / lam)[..., None] * lax.complex(b_re, b_im)
    c_mat = lax.complex(c_re, c_im)
    taus = jnp.arange(t + 1, dtype=F32)
    lam_pow = jnp.exp(lam_dt[:, None, :] * taus[None, :, None])
    g, p = lam.shape
    c = b_re.shape[-1]
    cl = c_mat[:, None, :, :] * lam_pow[:, :t, None, :]
    kr = (jnp.einsum('gtcp,gpi->gtci', cl.real, b_bar.real, precision=hp)
          - jnp.einsum('gtcp,gpi->gtci', cl.imag, b_bar.imag, precision=hp))
    z = jnp.pad(kr.transpose(0, 3, 2, 1).astype(BF16), ((0, 0), (0, 0), (0, 0), (t - 1, 0)))
    kg = jnp.stack([z[..., t - 1 - s:2 * t - 1 - s] for s in range(t)], axis=2)
    m = kg.reshape(g, c * t, c * t)
    pc = lam_pow[:, t - 1::-1][:, :t, :, None] * b_bar[:, None, :, :]
    pc = pc.transpose(0, 3, 1, 2).reshape(g, c * t, p)
    p_tab = jnp.concatenate([pc.real, pc.imag], axis=-1)
    ql = c_mat[:, None, :, :] * lam_pow[:, 1:t + 1, None, :]
    ql = ql.transpose(0, 3, 2, 1).reshape(g, p, c * t)
    q_tab = jnp.concatenate([ql.real, -ql.imag], axis=1)
    lam_t = lam_pow[:, t]
    a1 = jnp.concatenate([lam_t.real, lam_t.real], axis=-1)[:, None, :]
    a2 = jnp.concatenate([-lam_t.imag, lam_t.imag], axis=-1)[:, None, :]
    return m.astype(BF16), p_tab.astype(BF16), q_tab.astype(BF16), a1, a2


def _s5_kernel(u_ref, m_ref, p_ref, q_ref, a1_ref, a2_ref, y_ref, v_ref, xs_ref, *, n_chunks, n_batch):
    u = u_ref[0].astype(BF16)
    v_ref[...] = jnp.dot(u, p_ref[0], preferred_element_type=F32)
    a1 = a1_ref[0]
    a2 = a2_ref[0]
    half = xs_ref.shape[1] // 2

    def step(n, state):
        xs_ref[pl.ds(n, n_batch, stride=n_chunks), :] = state
        return (a1 * state + a2 * pltpu.roll(state, half, 1)
                + v_ref[pl.ds(n, n_batch, stride=n_chunks), :])

    lax.fori_loop(0, n_chunks, step, jnp.zeros((n_batch, xs_ref.shape[1]), F32))
    y = jnp.dot(u, m_ref[0], preferred_element_type=F32)
    y_ref[0] = y + jnp.dot(xs_ref[...].astype(BF16), q_ref[0], preferred_element_type=F32)


def _s5_call(ut, m, p_tab, q_tab, a1, a2, n_chunks, n_batch):
    g, rows, width = ut.shape
    p2 = p_tab.shape[-1]
    return pl.pallas_call(
        functools.partial(_s5_kernel, n_chunks=n_chunks, n_batch=n_batch),
        out_shape=jax.ShapeDtypeStruct((g, rows, width), F32),
        grid=(g,),
        in_specs=[pl.BlockSpec((1, rows, width), lambda i: (i, 0, 0)),
                  pl.BlockSpec((1, width, width), lambda i: (i, 0, 0)),
                  pl.BlockSpec((1, width, p2), lambda i: (i, 0, 0)),
                  pl.BlockSpec((1, p2, width), lambda i: (i, 0, 0)),
                  pl.BlockSpec((1, 1, p2), lambda i: (i, 0, 0)),
                  pl.BlockSpec((1, 1, p2), lambda i: (i, 0, 0))],
        out_specs=pl.BlockSpec((1, rows, width), lambda i: (i, 0, 0)),
        scratch_shapes=[pltpu.VMEM((rows, p2), F32), pltpu.VMEM((rows, p2), F32)],
        compiler_params=pltpu.CompilerParams(
            dimension_semantics=("arbitrary",), vmem_limit_bytes=VMEM_LIMIT),
        name="s5",
    )(ut, m, p_tab, q_tab, a1, a2)


def _token_major(flat_ref, src_ref):
    groups, rows, _ = src_ref.shape
    per_group = S5_GROUP // 2
    pairs = groups * per_group
    for q in range(pairs):
        flat_ref[pl.ds(q, rows, stride=pairs), :] = (
            src_ref[q // per_group, :, (q % per_group) * LANES:(q % per_group + 1) * LANES])
    tiles = []
    for m in range(rows // 2):
        even, odd = _split_chunk_pairs(flat_ref[2 * m * pairs:(2 * m + 1) * pairs, :],
                                       flat_ref[(2 * m + 1) * pairs:(2 * m + 2) * pairs, :])
        tiles.append(jnp.concatenate([even, odd], axis=0))
    return jnp.concatenate(tiles, axis=1)


def _mix_back_kernel(x_ref, oa_ref, yt_ref, ut_ref, mod_ref, dskip_ref, wglu_ref, bglu_ref,
                     wout_ref, wrh_ref, wrl_ref, wsg_ref, wsu_ref, wsd_ref,
                     x1_ref, hp_ref, lt_ref, flat_ref):
    na = oa_ref.shape[1]
    y_t = _token_major(flat_ref, yt_ref)
    u_t = _token_major(flat_ref, ut_ref)
    z_t = jax.nn.gelu(y_t + dskip_ref[...] * u_t)
    gate_t = _sigmoid(jnp.dot(wglu_ref[...], z_t.astype(BF16), preferred_element_type=F32)
                      + bglu_ref[...])
    ob_t = (z_t * gate_t).astype(BF16)
    mixed = (jnp.dot(oa_ref[...], wout_ref[0:na, :], preferred_element_type=F32)
             + lax.dot_general(ob_t, wout_ref[na:, :], _TN, preferred_element_type=F32))
    x1 = x_ref[...] + mod_ref[0, 2:3, :] * mixed
    ms = jnp.mean(x1 * x1, axis=-1, keepdims=True)
    h2 = x1 * lax.rsqrt(ms + EPS) * (1.0 + mod_ref[0, 4:5, :]) + mod_ref[0, 3:4, :]
    _store_packed(hp_ref, h2, h2.shape[0])
    h_hi = h2.astype(BF16)
    h_lo = (h2 - h_hi.astype(F32)).astype(BF16)
    lt = lax.dot_general(wrh_ref[...], h_hi, _NT, preferred_element_type=F32)
    lt = lt + lax.dot_general(wrl_ref[...], h_hi, _NT, preferred_element_type=F32)
    lt = lt + lax.dot_general(wrh_ref[...], h_lo, _NT, preferred_element_type=F32)
    lt_ref[...] = lt
    hid = _silu(jnp.dot(h_hi, wsg_ref[...], preferred_element_type=F32)) * jnp.dot(
        h_hi, wsu_ref[...], preferred_element_type=F32)
    shared = jnp.dot(hid.astype(BF16), wsd_ref[...], preferred_element_type=F32)
    x1_ref[...] = x1 + mod_ref[0, 5:6, :] * shared


def _mix_back_call(x2d, oa, yt, ut, mod, dskip, wglu, bglu, wout, wrh, wrl, wsg, wsu, wsd, seq):
    n, d = x2d.shape
    nb = oa.shape[1]
    ne = wrh.shape[0]
    dsh = wsg.shape[1]
    tm = TOK_TILE
    per_b = seq // tm
    groups, _, width = yt.shape
    rows = tm // CHUNK
    const = lambda i: (0, 0)
    flat_block = pl.BlockSpec((groups, rows, width), lambda i: (0, i, 0))
    return pl.pallas_call(
        _mix_back_kernel,
        out_shape=(jax.ShapeDtypeStruct((n, d), F32),
                   jax.ShapeDtypeStruct((n * PACK_ROWS, LANES), I32),
                   jax.ShapeDtypeStruct((ne, n), F32)),
        grid=(n // tm,),
        in_specs=[pl.BlockSpec((tm, d), lambda i: (i, 0)),
                  pl.BlockSpec((tm, nb), lambda i: (i, 0)),
                  flat_block,
                  flat_block,
                  pl.BlockSpec((1, 6, d), lambda i: (i // per_b, 0, 0)),
                  pl.BlockSpec((nb, 1), const),
                  pl.BlockSpec((nb, nb), const),
                  pl.BlockSpec((nb, 1), const),
                  pl.BlockSpec((d, d), const),
                  pl.BlockSpec((ne, d), const),
                  pl.BlockSpec((ne, d), const),
                  pl.BlockSpec((d, dsh), const),
                  pl.BlockSpec((d, dsh), const),
                  pl.BlockSpec((dsh, d), const)],
        out_specs=(pl.BlockSpec((tm, d), lambda i: (i, 0)),
                   pl.BlockSpec((tm * PACK_ROWS, LANES), lambda i: (i, 0)),
                   pl.BlockSpec((ne, tm), lambda i: (0, i))),
        scratch_shapes=[pltpu.VMEM((rows * nb // 2, LANES), F32)],
        compiler_params=pltpu.CompilerParams(
            dimension_semantics=("arbitrary",), vmem_limit_bytes=VMEM_LIMIT),
        name="mix_back",
    )(x2d, oa, yt, ut, mod, dskip, wglu, bglu, wout, wrh, wrl, wsg, wsu, wsd)


def _route_kernel(lt_ref, bias_ref, su_ref, idx_ref, w_ref, rank_ref, cnt_ref, run_ref):
    ne, tr = lt_ref.shape
    per_group = ne // N_EXPERT_GROUPS
    neg = -jnp.inf

    @pl.when(pl.program_id(0) == 0)
    def _():
        run_ref[...] = jnp.zeros_like(run_ref)

    s = _sigmoid(lt_ref[...])
    sel = s + bias_ref[...]
    gio = lax.broadcasted_iota(I32, (per_group, tr), 0)
    gscore = []
    for g in range(N_EXPERT_GROUPS):
        v = sel[g * per_group:(g + 1) * per_group, :]
        m1 = jnp.max(v, axis=0, keepdims=True)
        i1 = jnp.min(jnp.where(v == m1, gio, per_group), axis=0, keepdims=True)
        m2 = jnp.max(jnp.where(gio == i1, neg, v), axis=0, keepdims=True)
        gscore.append(m1 + m2)
    masked = []
    for g in range(N_EXPERT_GROUPS):
        ahead = jnp.zeros((1, tr), I32)
        for o in range(N_EXPERT_GROUPS):
            if o == g:
                continue
            wins = (gscore[o] >= gscore[g]) if o < g else (gscore[o] > gscore[g])
            ahead = ahead + wins.astype(I32)
        keep = ahead < TOPK_GROUPS
        masked.append(jnp.where(keep, sel[g * per_group:(g + 1) * per_group, :], neg))
    selm = jnp.concatenate(masked, axis=0)
    eio = lax.broadcasted_iota(I32, (ne, tr), 0)
    hits = jnp.zeros((ne, tr), F32)
    idxs, ws = [], []
    for k in range(TOP_K):
        m = jnp.max(selm, axis=0, keepdims=True)
        ik = jnp.min(jnp.where(selm == m, eio, ne), axis=0, keepdims=True)
        onehot = eio == ik
        ws.append(jnp.sum(jnp.where(onehot, s, 0.0), axis=0, keepdims=True))
        hits = jnp.where(onehot, 1.0, hits)
        selm = jnp.where(onehot, neg, selm)
        idxs.append(ik)
    wsum = ws[0]
    for k in range(1, TOP_K):
        wsum = wsum + ws[k]
    scale = ROUTE_SCALE / wsum
    ranks = jnp.dot(hits.astype(BF16), su_ref[...], preferred_element_type=F32) + run_ref[...]
    for k in range(TOP_K):
        idx_ref[k:k + 1, :] = idxs[k]
        w_ref[k:k + 1, :] = ws[k] * scale
        rk = jnp.sum(jnp.where(eio == idxs[k], ranks, 0.0), axis=0, keepdims=True)
        rank_ref[k:k + 1, :] = rk.astype(I32)
    run_ref[...] = run_ref[...] + jnp.sum(hits, axis=1, keepdims=True)
    cnt_ref[...] = run_ref[...]


def _route_call(lt, bias):
    ne, n = lt.shape
    tr = ROUTE_TILE
    su = jnp.triu(jnp.ones((tr, tr), F32), k=1).astype(BF16)
    return pl.pallas_call(
        _route_kernel,
        out_shape=(jax.ShapeDtypeStruct((TOP_K, n), I32),
                   jax.ShapeDtypeStruct((TOP_K, n), F32),
                   jax.ShapeDtypeStruct((TOP_K, n), I32),
                   jax.ShapeDtypeStruct((ne, 1), F32)),
        grid=(n // tr,),
        in_specs=[pl.BlockSpec((ne, tr), lambda i: (0, i)),
                  pl.BlockSpec((ne, 1), lambda i: (0, 0)),
                  pl.BlockSpec((tr, tr), lambda i: (0, 0))],
        out_specs=(pl.BlockSpec((TOP_K, tr), lambda i: (0, i)),
                   pl.BlockSpec((TOP_K, tr), lambda i: (0, i)),
                   pl.BlockSpec((TOP_K, tr), lambda i: (0, i)),
                   pl.BlockSpec((ne, 1), lambda i: (0, 0))),
        scratch_shapes=[pltpu.VMEM((ne, 1), F32)],
        compiler_params=pltpu.CompilerParams(
            dimension_semantics=("arbitrary",), vmem_limit_bytes=VMEM_LIMIT),
        name="route",
    )(lt, bias.reshape(ne, 1), su)


def _pos_kernel(idx_ref, rank_ref, st_ref, pos_ref):
    ne = st_ref.shape[0]
    tp = idx_ref.shape[1]
    eio = lax.broadcasted_iota(I32, (ne, tp), 0)
    st = st_ref[...]
    for k in range(TOP_K):
        base = jnp.sum(jnp.where(eio == idx_ref[k:k + 1, :], st, 0), axis=0, keepdims=True)
        pos_ref[k:k + 1, :] = base + rank_ref[k:k + 1, :]


def _pos_call(top_idx, rank, starts):
    kk, n = top_idx.shape
    ne = starts.shape[0]
    tp = POS_TILE
    return pl.pallas_call(
        _pos_kernel,
        out_shape=jax.ShapeDtypeStruct((kk, n), I32),
        grid=(n // tp,),
        in_specs=[pl.BlockSpec((kk, tp), lambda i: (0, i)),
                  pl.BlockSpec((kk, tp), lambda i: (0, i)),
                  pl.BlockSpec((ne, 1), lambda i: (0, 0))],
        out_specs=pl.BlockSpec((kk, tp), lambda i: (0, i)),
        compiler_params=pltpu.CompilerParams(
            dimension_semantics=("arbitrary",), vmem_limit_bytes=VMEM_LIMIT),
        name="pos",
    )(top_idx, rank, starts.reshape(ne, 1))


def _sc_mesh():
    return plsc.VectorSubcoreMesh(core_axis_name="c", subcore_axis_name="s",
                                  num_cores=SC_CORES, num_subcores=SC_SUBCORES)


def _sc_worker():
    return lax.axis_index("s") * SC_CORES + lax.axis_index("c")


def _dispatch_call(pos_win, hp):
    n = hp.shape[0]
    n_workers = SC_CORES * SC_SUBCORES
    wins_per_worker = n // SC_WINDOW // n_workers

    def body(hp_hbm, pos_hbm, xs_hbm, idx_v, rows_v, sem):
        first_win = _sc_worker() * wins_per_worker

        @pl.loop(0, wins_per_worker)
        def _(w):
            win = first_win + w
            pltpu.sync_copy(hp_hbm.at[pl.ds(win * SC_WINDOW, SC_WINDOW)], rows_v)
            pltpu.sync_copy(pos_hbm.at[win], idx_v)
            copies = [pltpu.async_copy(rows_v, xs_hbm.at[idx_v.at[k]], sem) for k in range(TOP_K)]
            for cp in copies:
                cp.wait()

    return pl.kernel(
        body,
        out_type=jax.ShapeDtypeStruct((n * TOP_K,) + hp.shape[1:], hp.dtype),
        mesh=_sc_mesh(),
        scratch_types=[pltpu.VMEM((TOP_K, SC_WINDOW), I32),
                       pltpu.VMEM((SC_WINDOW,) + hp.shape[1:], hp.dtype),
                       pltpu.SemaphoreType.DMA],
        name="dispatch",
    )(hp, pos_win)


def _collect_call(pos_win, y_sorted):
    n = pos_win.shape[0] * SC_WINDOW
    n_workers = SC_CORES * SC_SUBCORES
    wins_per_worker = n // SC_WINDOW // n_workers

    def body(ys_hbm, pos_hbm, out_hbm, idx_v, rows_a, rows_b, gather_sem, write_sem):
        first_win = _sc_worker() * wins_per_worker
        bufs = (rows_a, rows_b)

        @pl.loop(0, wins_per_worker)
        def _(w):
            win = first_win + w
            pltpu.sync_copy(pos_hbm.at[win], idx_v)

            def gather(k):
                return pltpu.async_copy(ys_hbm.at[idx_v.at[k]], bufs[k % 2], gather_sem)

            pending_gather = gather(0)
            pending_write = None
            for k in range(TOP_K):
                pending_gather.wait()
                if pending_write is not None:
                    pending_write.wait()
                if k + 1 < TOP_K:
                    pending_gather = gather(k + 1)
                pending_write = pltpu.async_copy(
                    bufs[k % 2], out_hbm.at[k, pl.ds(win * SC_WINDOW, SC_WINDOW)], write_sem)
            pending_write.wait()

    row_buf = pltpu.VMEM((SC_WINDOW,) + y_sorted.shape[1:], y_sorted.dtype)
    return pl.kernel(
        body,
        out_type=jax.ShapeDtypeStruct((TOP_K, n) + y_sorted.shape[1:], y_sorted.dtype),
        mesh=_sc_mesh(),
        scratch_types=[pltpu.VMEM((TOP_K, SC_WINDOW), I32), row_buf, row_buf,
                       pltpu.SemaphoreType.DMA, pltpu.SemaphoreType.DMA],
        name="collect",
    )(y_sorted, pos_win)


def _experts_kernel(pe_ref, pb_ref, pv_ref, st_ref, en_ref,
                    xs_ref, wg_ref, wu_ref, wd_ref, y_ref, wgb_ref, wub_ref, wdb_ref, acc_ref):
    p = pl.program_id(0)
    n_pairs = pl.num_programs(0)
    e = pe_ref[p]
    blk = pb_ref[p]
    rb = acc_ref.shape[0]
    prev = jnp.maximum(p - 1, 0)
    nxt = jnp.minimum(p + 1, n_pairs - 1)
    live = pv_ref[p] == 1
    first = jnp.logical_or(p == 0, pb_ref[prev] != blk)
    last = jnp.logical_or(p == n_pairs - 1,
                          jnp.logical_or(pb_ref[nxt] != blk, pv_ref[nxt] == 0))

    @pl.when(jnp.logical_or(p == 0, pe_ref[prev] != e))
    def _():
        wgb_ref[...] = wg_ref[0].astype(BF16)
        wub_ref[...] = wu_ref[0].astype(BF16)
        wdb_ref[...] = wd_ref[0].astype(BF16)

    @pl.when(live)
    def _():
        xb = _load_packed_bf16(xs_ref, rb)
        gate = jnp.dot(xb, wgb_ref[...], preferred_element_type=F32)
        up = jnp.dot(xb, wub_ref[...], preferred_element_type=F32)
        rows = blk * rb + lax.broadcasted_iota(I32, (rb, 1), 0)
        mine = jnp.logical_and(rows >= st_ref[e], rows < en_ref[e])
        hid = jnp.where(mine, _silu(gate) * up, 0.0).astype(BF16)
        yb = jnp.dot(hid, wdb_ref[...], preferred_element_type=F32)
        not_first = jnp.logical_not(first)
        not_last = jnp.logical_not(last)

        @pl.when(jnp.logical_and(first, last))
        def _():
            _store_packed(y_ref, yb, rb)

        @pl.when(jnp.logical_and(first, not_last))
        def _():
            acc_ref[...] = yb

        @pl.when(jnp.logical_and(not_first, not_last))
        def _():
            acc_ref[...] += yb

        @pl.when(jnp.logical_and(not_first, last))
        def _():
            _store_packed(y_ref, acc_ref[...] + yb, rb)


def _experts_call(pair_e, pair_blk, pair_ok, starts, ends, xs, w_gate, w_up, w_down):
    ne, d, de = w_gate.shape
    rb = ROW_BLOCK
    n_pairs = pair_e.shape[0]
    grid_spec = pltpu.PrefetchScalarGridSpec(
        num_scalar_prefetch=5,
        grid=(n_pairs,),
        in_specs=[pl.BlockSpec((rb * PACK_ROWS, LANES), lambda p, pe, pb, pv, st, en: (pb[p], 0)),
                  pl.BlockSpec((1, d, de), lambda p, pe, pb, pv, st, en: (pe[p], 0, 0)),
                  pl.BlockSpec((1, d, de), lambda p, pe, pb, pv, st, en: (pe[p], 0, 0)),
                  pl.BlockSpec((1, de, d), lambda p, pe, pb, pv, st, en: (pe[p], 0, 0))],
        out_specs=pl.BlockSpec((rb * PACK_ROWS, LANES), lambda p, pe, pb, pv, st, en: (pb[p], 0)),
        scratch_shapes=[pltpu.VMEM((d, de), BF16), pltpu.VMEM((d, de), BF16),
                        pltpu.VMEM((de, d), BF16), pltpu.VMEM((rb, d), F32)],
    )
    return pl.pallas_call(
        _experts_kernel,
        out_shape=jax.ShapeDtypeStruct(xs.shape, xs.dtype),
        grid_spec=grid_spec,
        compiler_params=pltpu.CompilerParams(
            dimension_semantics=("arbitrary",), vmem_limit_bytes=VMEM_LIMIT),
        name="experts",
    )(pair_e, pair_blk, pair_ok, starts, ends, xs, w_gate, w_up, w_down)


def _combine_kernel(yk_ref, x1_ref, w_ref, g2_ref, fg_ref, o_ref):
    tm = x1_ref.shape[0]
    w = w_ref[...]
    parts = None
    for k in range(TOP_K):
        wk = w[:, k:k + 1]
        scaled = [piece * wk for piece in _load_packed_f32(yk_ref.at[k], tm)]
        parts = scaled if parts is None else [a + b for a, b in zip(parts, scaled)]
    x2 = x1_ref[...] + g2_ref[0] * jnp.concatenate(parts, axis=1)
    ms = jnp.mean(x2 * x2, axis=-1, keepdims=True)
    o_ref[...] = x2 * lax.rsqrt(ms + EPS) * fg_ref[...]


def _combine_call(y_tok, x1s, w_tok, g2, fgain, seq):
    n, d = x1s.shape
    tm = MOVE_TILE
    per_b = seq // tm
    return pl.pallas_call(
        _combine_kernel,
        out_shape=jax.ShapeDtypeStruct((n, d), F32),
        grid=(n // tm,),
        in_specs=[pl.BlockSpec((TOP_K, tm * PACK_ROWS, LANES), lambda i: (0, i, 0)),
                  pl.BlockSpec((tm, d), lambda i: (i, 0)),
                  pl.BlockSpec((tm, TOP_K), lambda i: (i, 0)),
                  pl.BlockSpec((1, 1, d), lambda i: (i // per_b, 0, 0)),
                  pl.BlockSpec((1, d), lambda i: (0, 0))],
        out_specs=pl.BlockSpec((tm, d), lambda i: (i, 0)),
        compiler_params=pltpu.CompilerParams(
            dimension_semantics=("arbitrary",), vmem_limit_bytes=VMEM_LIMIT),
        name="combine",
    )(y_tok, x1s, w_tok, g2, fgain)


def _pair_tables(counts, n_rows):
    ne = counts.shape[0]
    sizes = counts.astype(I32)
    ends = jnp.cumsum(sizes)
    starts = ends - sizes
    first_blk = starts // ROW_BLOCK
    last_blk = (ends - 1) // ROW_BLOCK
    n_pairs = jnp.where(sizes > 0, last_blk - first_blk + 1, 0)
    pair_end = jnp.cumsum(n_pairs)
    pair_start = pair_end - n_pairs
    max_pairs = n_rows // ROW_BLOCK + ne
    p = jnp.arange(max_pairs, dtype=I32)
    ok = p < pair_end[-1]
    pc = jnp.minimum(p, pair_end[-1] - 1)
    pair_e = jnp.minimum(jnp.searchsorted(pair_end, pc, side='right'), ne - 1).astype(I32)
    pair_blk = (first_blk[pair_e] + pc - pair_start[pair_e]).astype(I32)
    return pair_e, pair_blk, ok.astype(I32), starts.astype(I32), ends.astype(I32)


def kernel(x, c, w_ada, b_ada, w_in, lb_logits, hgrn_norm, lam_re, lam_im, log_dt, b_re, b_im,
           c_re, c_im, d_skip, w_glu, b_glu, w_out, w_router, router_bias, w_gate, w_up, w_down,
           ws_gate, ws_up, ws_down, final_gain):
    bsz, seq, d = x.shape
    n = bsz * seq
    fdim = HGRN_HEADS * HGRN_KDIM
    n_chunks = seq // CHUNK
    lb = jnp.cumsum(jax.nn.softmax(lb_logits.astype(F32), axis=0), axis=0)[0].reshape(1, fdim)

    mod = _mod_call(c, w_ada[0], b_ada[0]).reshape(bsz, 6, d)
    nb = w_in.shape[2] - 4 * fdim
    perm = jnp.concatenate([jnp.arange(0, nb, 2), jnp.arange(1, nb, 2)])
    w_in_b = w_in[0].astype(BF16)
    ut, out_a = _mix_front_call(x, mod, w_in_b[:, :4 * fdim], w_in_b[:, 4 * fdim:][:, perm].T, lb,
                                hgrn_norm[0].reshape(1, fdim))

    m_tab, p_tab, q_tab, a1, a2 = _s5_tables(lam_re[0], lam_im[0], log_dt[0], b_re[0], b_im[0],
                                             c_re[0], c_im[0])
    yt = _s5_call(ut, m_tab, p_tab, q_tab, a1, a2, n_chunks, bsz)

    wr_t = w_router[0].T
    wr_hi = wr_t.astype(BF16)
    wr_lo = (wr_t - wr_hi.astype(F32)).astype(BF16)
    w_out_b = w_out[0].astype(BF16)
    w_out_p = jnp.concatenate([w_out_b[:fdim], w_out_b[fdim:][perm]], axis=0)
    x1s, hp, logits_t = _mix_back_call(
        x.reshape(n, d), out_a.reshape(n, fdim), yt, ut, mod,
        d_skip[0][perm].reshape(nb, 1), w_glu[0][perm][:, perm].T.astype(BF16),
        b_glu[0][perm].reshape(nb, 1), w_out_p, wr_hi, wr_lo, ws_gate[0].astype(BF16),
        ws_up[0].astype(BF16), ws_down[0].astype(BF16), seq)

    top_idx, top_w, rank, counts = _route_call(logits_t, router_bias[0])
    pair_e, pair_blk, pair_ok, starts, ends = _pair_tables(counts[:, 0], n * TOP_K)
    pos = _pos_call(top_idx, rank, starts)

    pos_win = pos.reshape(TOP_K, n // SC_WINDOW, SC_WINDOW).transpose(1, 0, 2)
    row3 = (PACK_ROWS, LANES)
    xs = _dispatch_call(pos_win, hp.reshape((n,) + row3))
    y_sorted = _experts_call(pair_e, pair_blk, pair_ok, starts, ends,
                             xs.reshape(n * TOP_K * PACK_ROWS, LANES), w_gate[0], w_up[0], w_down[0])
    y_tok = _collect_call(pos_win, y_sorted.reshape((n * TOP_K,) + row3))
    out = _combine_call(y_tok.reshape(TOP_K, n * PACK_ROWS, LANES), x1s, top_w.T, mod[:, 5:6, :],
                        final_gain.reshape(1, d), seq)
    return out.reshape(bsz, seq, d)
```

```python
import functools

import jax
import jax.numpy as jnp
from jax import lax
from jax.experimental import pallas as pl
from jax.experimental.pallas import tpu as pltpu
from jax.experimental.pallas import tpu_sc as plsc

F32 = jnp.float32
BF16 = jnp.bfloat16
I32 = jnp.int32

EPS = 1e-6
CHUNK = 64
HGRN_HEADS = 4
HGRN_KDIM = 128
S5_GROUP = 16
S5_STATE = 64
N_EXPERT_GROUPS = 8
TOPK_GROUPS = 4
TOP_K = 8
ROUTE_SCALE = 2.5
LANES = 128
PACK_ROWS = 4
SC_CORES = 2
SC_SUBCORES = 16
SC_WINDOW = 64

SEQ_TILE = 512
TOK_TILE = 512
ROUTE_TILE = 256
POS_TILE = 1024
MOVE_TILE = 256
ROW_BLOCK = 512
VMEM_LIMIT = 56 * 1024 * 1024

_NT = (((1,), (1,)), ((), ()))
_TN = (((0,), (0,)), ((), ()))


def _sigmoid(v):
    return 1.0 / (1.0 + jnp.exp(-v))


def _silu(v):
    return v * _sigmoid(v)


def _bdot(a, b):
    return jnp.dot(a.astype(BF16), b.astype(BF16), preferred_element_type=F32)


def _store_packed(ref, val, n_rows):
    for j in range(PACK_ROWS):
        lo = val[:, 2 * j * LANES:(2 * j + 1) * LANES]
        hi = val[:, (2 * j + 1) * LANES:(2 * j + 2) * LANES]
        ref[pl.ds(j, n_rows, stride=PACK_ROWS), :] = pltpu.pack_elementwise([lo, hi], packed_dtype=BF16)


def _mod_kernel(c_ref, w_ref, b_ref, o_ref):
    o_ref[...] = _bdot(_silu(c_ref[...]), w_ref[...]) + b_ref[...]


def _mod_call(c, w_ada, b_ada):
    bsz, d = c.shape
    n_out = w_ada.shape[1]
    return pl.pallas_call(
        _mod_kernel,
        out_shape=jax.ShapeDtypeStruct((bsz, n_out), F32),
        grid=(n_out // d,),
        in_specs=[pl.BlockSpec((bsz, d), lambda j: (0, 0)),
                  pl.BlockSpec((d, d), lambda j: (0, j)),
                  pl.BlockSpec((1, d), lambda j: (0, j))],
        out_specs=pl.BlockSpec((bsz, d), lambda j: (0, j)),
        compiler_params=pltpu.CompilerParams(vmem_limit_bytes=VMEM_LIMIT),
        name="mod",
    )(c, w_ada, b_ada.reshape(1, n_out))


def _split_chunk_pairs(tile_even, tile_odd):
    low = lax.broadcasted_iota(I32, tile_even.shape, 1) < CHUNK
    first = jnp.where(low, tile_even, pltpu.roll(tile_odd, CHUNK, 1))
    second = jnp.where(low, pltpu.roll(tile_even, CHUNK, 1), tile_odd)
    return first, second


def _mix_front_kernel(x_ref, mod_ref, win_ref, wut_ref, lb_ref, gn_ref, ltri_ref,
                      ut_ref, oa_ref, proj_ref, st_ref, flat_ref):
    fdim = HGRN_HEADS * HGRN_KDIM
    ts = x_ref.shape[1]
    pairs = wut_ref.shape[0] // 2
    rows = ts // CHUNK

    @pl.when(pl.program_id(1) == 0)
    def _():
        st_ref[...] = jnp.zeros_like(st_ref)

    x = x_ref[0]
    ms = jnp.mean(x * x, axis=-1, keepdims=True)
    h = x * lax.rsqrt(ms + EPS) * (1.0 + mod_ref[0, 1:2, :]) + mod_ref[0, 0:1, :]
    hb = h.astype(BF16)
    proj_ref[...] = jnp.dot(hb, win_ref[...], preferred_element_type=F32)
    u_t = lax.dot_general(wut_ref[...], hb, _NT, preferred_element_type=F32)
    for m in range(ts // LANES):
        first, second = _split_chunk_pairs(u_t[:pairs, m * LANES:(m + 1) * LANES],
                                           u_t[pairs:, m * LANES:(m + 1) * LANES])
        flat_ref[2 * m * pairs:(2 * m + 1) * pairs, :] = first
        flat_ref[(2 * m + 1) * pairs:(2 * m + 2) * pairs, :] = second
    per_group = S5_GROUP // 2
    for q in range(pairs):
        ut_ref[q // per_group, :, (q % per_group) * LANES:(q % per_group + 1) * LANES] = (
            flat_ref[pl.ds(q, rows, stride=pairs), :])

    lb = lb_ref[...]
    gn = gn_ref[...]
    ltri = ltri_ref[...]
    row = lax.broadcasted_iota(I32, (CHUNK, CHUNK), 0)
    col = lax.broadcasted_iota(I32, (CHUNK, CHUNK), 1)
    causal = row >= col

    def chunk_step(ci, carry):
        r0 = pl.multiple_of(ci * CHUNK, CHUNK)
        q = proj_ref[pl.ds(r0, CHUNK), 0:fdim]
        fl = proj_ref[pl.ds(r0, CHUNK), fdim:2 * fdim]
        iv = proj_ref[pl.ds(r0, CHUNK), 2 * fdim:3 * fdim]
        og = proj_ref[pl.ds(r0, CHUNK), 3 * fdim:4 * fdim]
        f = lb + (1.0 - lb) * _sigmoid(fl)
        b = jnp.dot(ltri, jnp.log(f), precision=lax.Precision.HIGHEST,
                    preferred_element_type=F32)
        b_ref = b[CHUNK // 2 - 1:CHUNK // 2, :]
        b_last = b[CHUNK - 1:CHUNK, :]
        qs = _silu(q)
        kk = 1.0 - f
        qe = (qs * jnp.exp(b - b_ref)).astype(BF16)
        ke = (kk * jnp.exp(b_ref - b)).astype(BF16)
        qb = (qs * jnp.exp(b)).astype(BF16)
        k2 = (kk * jnp.exp(b_last - b)).astype(BF16)
        dec = jnp.exp(b_last)
        ivb = iv.astype(BF16)
        outs = []
        for hh in range(HGRN_HEADS):
            sl = slice(hh * HGRN_KDIM, (hh + 1) * HGRN_KDIM)
            att = lax.dot_general(qe[:, sl], ke[:, sl], _NT, preferred_element_type=F32)
            att = jnp.where(causal, att, 0.0)
            st = st_ref[hh]
            o = jnp.dot(att.astype(BF16), ivb[:, sl], preferred_element_type=F32)
            o = o + lax.dot_general(qb[:, sl], st.astype(BF16), _NT, preferred_element_type=F32)
            st_ref[hh] = st * dec[:, sl] + lax.dot_general(
                ivb[:, sl], k2[:, sl], _TN, preferred_element_type=F32)
            outs.append(o * lax.rsqrt(jnp.mean(o * o, axis=-1, keepdims=True) + EPS))
        o = jnp.concatenate(outs, axis=1) * gn * _silu(og)
        oa_ref[0, pl.ds(r0, CHUNK), :] = o.astype(BF16)
        return carry

    lax.fori_loop(0, ts // CHUNK, chunk_step, 0, unroll=True)


def _mix_front_call(x, mod, w_main, w_ut, lb, gn):
    bsz, seq, d = x.shape
    fdim = HGRN_HEADS * HGRN_KDIM
    ncols = w_main.shape[1]
    nb = w_ut.shape[0]
    groups = nb // S5_GROUP
    ltri = jnp.tril(jnp.ones((CHUNK, CHUNK), F32))
    ts = SEQ_TILE
    tiles = seq // ts
    rows = ts // CHUNK
    return pl.pallas_call(
        _mix_front_kernel,
        out_shape=(jax.ShapeDtypeStruct((groups, bsz * seq // CHUNK, S5_GROUP * CHUNK), F32),
                   jax.ShapeDtypeStruct((bsz, seq, fdim), BF16)),
        grid=(bsz, tiles),
        in_specs=[pl.BlockSpec((1, ts, d), lambda b, j: (b, j, 0)),
                  pl.BlockSpec((1, 6, d), lambda b, j: (b, 0, 0)),
                  pl.BlockSpec((d, ncols), lambda b, j: (0, 0)),
                  pl.BlockSpec((nb, d), lambda b, j: (0, 0)),
                  pl.BlockSpec((1, fdim), lambda b, j: (0, 0)),
                  pl.BlockSpec((1, fdim), lambda b, j: (0, 0)),
                  pl.BlockSpec((CHUNK, CHUNK), lambda b, j: (0, 0))],
        out_specs=(pl.BlockSpec((groups, rows, S5_GROUP * CHUNK), lambda b, j: (0, b * tiles + j, 0)),
                   pl.BlockSpec((1, ts, fdim), lambda b, j: (b, j, 0))),
        scratch_shapes=[pltpu.VMEM((ts, ncols), F32),
                        pltpu.VMEM((HGRN_HEADS, fdim // HGRN_HEADS, HGRN_KDIM), F32),
                        pltpu.VMEM((rows * nb // 2, LANES), F32)],
        compiler_params=pltpu.CompilerParams(
            dimension_semantics=("arbitrary", "arbitrary"), vmem_limit_bytes=VMEM_LIMIT),
        name="mix_front",
    )(x, mod, w_main, w_ut, lb, gn, ltri)


def _s5_tables(lam_re, lam_im, log_dt, b_re, b_im, c_re, c_im):
    t = CHUNK
    hp = lax.Precision.HIGHEST
    lam = lax.complex(jnp.minimum(lam_re, -1e-4), lam_im)
    lam_dt = lam * jnp.exp(log_dt)[:, None]
    lam_bar = jnp.exp(lam_dt)
    b_bar = ((lam_bar - 1.0) / lam)[..., None] * lax.complex(b_re, b_im)
    c_mat = lax.complex(c_re, c_im)
    taus = jnp.arange(t + 1, dtype=F32)
    lam_pow = jnp.exp(lam_dt[:, None, :] * taus[None, :, None])
    g, p = lam.shape
    c = b_re.shape[-1]
    cl = c_mat[:, None, :, :] * lam_pow[:, :t, None, :]
    kr = (jnp.einsum('gtcp,gpi->gtci', cl.real, b_bar.real, precision=hp)
          - jnp.einsum('gtcp,gpi->gtci', cl.imag, b_bar.imag, precision=hp))
    z = jnp.pad(kr.transpose(0, 3, 2, 1).astype(BF16), ((0, 0), (0, 0), (0, 0), (t - 1, 0)))
    kg = jnp.stack([z[..., t - 1 - s:2 * t - 1 - s] for s in range(t)], axis=2)
    m = kg.reshape(g, c * t, c * t)
    pc = lam_pow[:, t - 1::-1][:, :t, :, None] * b_bar[:, None, :, :]
    pc = pc.transpose(0, 3, 1, 2).reshape(g, c * t, p)
    p_tab = jnp.concatenate([pc.real, pc.imag], axis=-1)
    ql = c_mat[:, None, :, :] * lam_pow[:, 1:t + 1, None, :]
    ql = ql.transpose(0, 3, 2, 1).reshape(g, p, c * t)
    q_tab = jnp.concatenate([ql.real, -ql.imag], axis=1)
    lam_t = lam_pow[:, t]
    a1 = jnp.concatenate([lam_t.real, lam_t.real], axis=-1)[:, None, :]
    a2 = jnp.concatenate([-lam_t.imag, lam_t.imag], axis=-1)[:, None, :]
    return m.astype(BF16), p_tab.astype(BF16), q_tab.astype(BF16), a1, a2


def _s5_kernel(u_ref, m_ref, p_ref, q_ref, a1_ref, a2_ref, y_ref, v_ref, xs_ref, *, n_chunks, n_batch):
    u = u_ref[0].astype(BF16)
    v_ref[...] = jnp.dot(u, p_ref[0], preferred_element_type=F32)
    a1 = a1_ref[0]
    a2 = a2_ref[0]
    half = xs_ref.shape[1] // 2

    def step(n, state):
        xs_ref[pl.ds(n, n_batch, stride=n_chunks), :] = state
        return (a1 * state + a2 * pltpu.roll(state, half, 1)
                + v_ref[pl.ds(n, n_batch, stride=n_chunks), :])

    lax.fori_loop(0, n_chunks, step, jnp.zeros((n_batch, xs_ref.shape[1]), F32))
    y = jnp.dot(u, m_ref[0], preferred_element_type=F32)
    y_ref[0] = y + jnp.dot(xs_ref[...].astype(BF16), q_ref[0], preferred_element_type=F32)


def _s5_call(ut, m, p_tab, q_tab, a1, a2, n_chunks, n_batch):
    g, rows, width = ut.shape
    p2 = p_tab.shape[-1]
    return pl.pallas_call(
        functools.partial(_s5_kernel, n_chunks=n_chunks, n_batch=n_batch),
        out_shape=jax.ShapeDtypeStruct((g, rows, width), F32),
        grid=(g,),
        in_specs=[pl.BlockSpec((1, rows, width), lambda i: (i, 0, 0)),
                  pl.BlockSpec((1, width, width), lambda i: (i, 0, 0)),
                  pl.BlockSpec((1, width, p2), lambda i: (i, 0, 0)),
                  pl.BlockSpec((1, p2, width), lambda i: (i, 0, 0)),
                  pl.BlockSpec((1, 1, p2), lambda i: (i, 0, 0)),
                  pl.BlockSpec((1, 1, p2), lambda i: (i, 0, 0))],
        out_specs=pl.BlockSpec((1, rows, width), lambda i: (i, 0, 0)),
        scratch_shapes=[pltpu.VMEM((rows, p2), F32), pltpu.VMEM((rows, p2), F32)],
        compiler_params=pltpu.CompilerParams(
            dimension_semantics=("arbitrary",), vmem_limit_bytes=VMEM_LIMIT),
        name="s5",
    )(ut, m, p_tab, q_tab, a1, a2)


def _token_major(flat_ref, src_ref):
    groups, rows, _ = src_ref.shape
    per_group = S5_GROUP // 2
    pairs = groups * per_group
    for q in range(pairs):
        flat_ref[pl.ds(q, rows, stride=pairs), :] = (
            src_ref[q // per_group, :, (q % per_group) * LANES:(q % per_group + 1) * LANES])
    tiles = []
    for m in range(rows // 2):
        even, odd = _split_chunk_pairs(flat_ref[2 * m * pairs:(2 * m + 1) * pairs, :],
                                       flat_ref[(2 * m + 1) * pairs:(2 * m + 2) * pairs, :])
        tiles.append(jnp.concatenate([even, odd], axis=0))
    return jnp.concatenate(tiles, axis=1)


def _mix_back_kernel(x_ref, oa_ref, yt_ref, ut_ref, mod_ref, dskip_ref, wglu_ref, bglu_ref,
                     wout_ref, wrh_ref, wrl_ref, wsg_ref, wsu_ref, wsd_ref,
                     x1_ref, hp_ref, lt_ref, flat_ref):
    na = oa_ref.shape[1]
    y_t = _token_major(flat_ref, yt_ref)
    u_t = _token_major(flat_ref, ut_ref)
    z_t = jax.nn.gelu(y_t + dskip_ref[...] * u_t)
    gate_t = _sigmoid(jnp.dot(wglu_ref[...], z_t.astype(BF16), preferred_element_type=F32)
                      + bglu_ref[...])
    ob_t = (z_t * gate_t).astype(BF16)
    mixed = (jnp.dot(oa_ref[...], wout_ref[0:na, :], preferred_element_type=F32)
             + lax.dot_general(ob_t, wout_ref[na:, :], _TN, preferred_element_type=F32))
    x1 = x_ref[...] + mod_ref[0, 2:3, :] * mixed
    ms = jnp.mean(x1 * x1, axis=-1, keepdims=True)
    h2 = x1 * lax.rsqrt(ms + EPS) * (1.0 + mod_ref[0, 4:5, :]) + mod_ref[0, 3:4, :]
    _store_packed(hp_ref, h2, h2.shape[0])
    h_hi = h2.astype(BF16)
    h_lo = (h2 - h_hi.astype(F32)).astype(BF16)
    lt = lax.dot_general(wrh_ref[...], h_hi, _NT, preferred_element_type=F32)
    lt = lt + lax.dot_general(wrl_ref[...], h_hi, _NT, preferred_element_type=F32)
    lt = lt + lax.dot_general(wrh_ref[...], h_lo, _NT, preferred_element_type=F32)
    lt_ref[...] = lt
    hid = _silu(jnp.dot(h_hi, wsg_ref[...], preferred_element_type=F32)) * jnp.dot(
        h_hi, wsu_ref[...], preferred_element_type=F32)
    shared = jnp.dot(hid.astype(BF16), wsd_ref[...], preferred_element_type=F32)
    x1_ref[...] = x1 + mod_ref[0, 5:6, :] * shared


def _mix_back_call(x2d, oa, yt, ut, mod, dskip, wglu, bglu, wout, wrh, wrl, wsg, wsu, wsd, seq):
    n, d = x2d.shape
    nb = oa.shape[1]
    ne = wrh.shape[0]
    dsh = wsg.shape[1]
    tm = TOK_TILE
    per_b = seq // tm
    groups, _, width = yt.shape
    rows = tm // CHUNK
    const = lambda i: (0, 0)
    flat_block = pl.BlockSpec((groups, rows, width), lambda i: (0, i, 0))
    return pl.pallas_call(
        _mix_back_kernel,
        out_shape=(jax.ShapeDtypeStruct((n, d), F32),
                   jax.ShapeDtypeStruct((n * PACK_ROWS, LANES), I32),
                   jax.ShapeDtypeStruct((ne, n), F32)),
        grid=(n // tm,),
        in_specs=[pl.BlockSpec((tm, d), lambda i: (i, 0)),
                  pl.BlockSpec((tm, nb), lambda i: (i, 0)),
                  flat_block,
                  flat_block,
                  pl.BlockSpec((1, 6, d), lambda i: (i // per_b, 0, 0)),
                  pl.BlockSpec((nb, 1), const),
                  pl.BlockSpec((nb, nb), const),
                  pl.BlockSpec((nb, 1), const),
                  pl.BlockSpec((d, d), const),
                  pl.BlockSpec((ne, d), const),
                  pl.BlockSpec((ne, d), const),
                  pl.BlockSpec((d, dsh), const),
                  pl.BlockSpec((d, dsh), const),
                  pl.BlockSpec((dsh, d), const)],
        out_specs=(pl.BlockSpec((tm, d), lambda i: (i, 0)),
                   pl.BlockSpec((tm * PACK_ROWS, LANES), lambda i: (i, 0)),
                   pl.BlockSpec((ne, tm), lambda i: (0, i))),
        scratch_shapes=[pltpu.VMEM((rows * nb // 2, LANES), F32)],
        compiler_params=pltpu.CompilerParams(
            dimension_semantics=("arbitrary",), vmem_limit_bytes=VMEM_LIMIT),
        name="mix_back",
    )(x2d, oa, yt, ut, mod, dskip, wglu, bglu, wout, wrh, wrl, wsg, wsu, wsd)


def _route_kernel(lt_ref, bias_ref, su_ref, idx_ref, w_ref, rank_ref, cnt_ref, run_ref):
    ne, tr = lt_ref.shape
    per_group = ne // N_EXPERT_GROUPS
    neg = -jnp.inf

    @pl.when(pl.program_id(0) == 0)
    def _():
        run_ref[...] = jnp.zeros_like(run_ref)

    s = _sigmoid(lt_ref[...])
    sel = s + bias_ref[...]
    gio = lax.broadcasted_iota(I32, (per_group, tr), 0)
    gscore = []
    for g in range(N_EXPERT_GROUPS):
        v = sel[g * per_group:(g + 1) * per_group, :]
        m1 = jnp.max(v, axis=0, keepdims=True)
        i1 = jnp.min(jnp.where(v == m1, gio, per_group), axis=0, keepdims=True)
        m2 = jnp.max(jnp.where(gio == i1, neg, v), axis=0, keepdims=True)
        gscore.append(m1 + m2)
    masked = []
    for g in range(N_EXPERT_GROUPS):
        ahead = jnp.zeros((1, tr), I32)
        for o in range(N_EXPERT_GROUPS):
            if o == g:
                continue
            wins = (gscore[o] >= gscore[g]) if o < g else (gscore[o] > gscore[g])
            ahead = ahead + wins.astype(I32)
        keep = ahead < TOPK_GROUPS
        masked.append(jnp.where(keep, sel[g * per_group:(g + 1) * per_group, :], neg))
    selm = jnp.concatenate(masked, axis=0)
    eio = lax.broadcasted_iota(I32, (ne, tr), 0)
    hits = jnp.zeros((ne, tr), F32)
    idxs, ws = [], []
    for k in range(TOP_K):
        m = jnp.max(selm, axis=0, keepdims=True)
        ik = jnp.min(jnp.where(selm == m, eio, ne), axis=0, keepdims=True)
        onehot = eio == ik
        ws.append(jnp.sum(jnp.where(onehot, s, 0.0), axis=0, keepdims=True))
        hits = jnp.where(onehot, 1.0, hits)
        selm = jnp.where(onehot, neg, selm)
        idxs.append(ik)
    wsum = ws[0]
    for k in range(1, TOP_K):
        wsum = wsum + ws[k]
    scale = ROUTE_SCALE / wsum
    ranks = jnp.dot(hits.astype(BF16), su_ref[...], preferred_element_type=F32) + run_ref[...]
    for k in range(TOP_K):
        idx_ref[k:k + 1, :] = idxs[k]
        w_ref[k:k + 1, :] = ws[k] * scale
        rk = jnp.sum(jnp.where(eio == idxs[k], ranks, 0.0), axis=0, keepdims=True)
        rank_ref[k:k + 1, :] = rk.astype(I32)
    run_ref[...] = run_ref[...] + jnp.sum(hits, axis=1, keepdims=True)
    cnt_ref[...] = run_ref[...]


def _route_call(lt, bias):
    ne, n = lt.shape
    tr = ROUTE_TILE
    su = jnp.triu(jnp.ones((tr, tr), F32), k=1).astype(BF16)
    return pl.pallas_call(
        _route_kernel,
        out_shape=(jax.ShapeDtypeStruct((TOP_K, n), I32),
                   jax.ShapeDtypeStruct((TOP_K, n), F32),
                   jax.ShapeDtypeStruct((TOP_K, n), I32),
                   jax.ShapeDtypeStruct((ne, 1), F32)),
        grid=(n // tr,),
        in_specs=[pl.BlockSpec((ne, tr), lambda i: (0, i)),
                  pl.BlockSpec((ne, 1), lambda i: (0, 0)),
                  pl.BlockSpec((tr, tr), lambda i: (0, 0))],
        out_specs=(pl.BlockSpec((TOP_K, tr), lambda i: (0, i)),
                   pl.BlockSpec((TOP_K, tr), lambda i: (0, i)),
                   pl.BlockSpec((TOP_K, tr), lambda i: (0, i)),
                   pl.BlockSpec((ne, 1), lambda i: (0, 0))),
        scratch_shapes=[pltpu.VMEM((ne, 1), F32)],
        compiler_params=pltpu.CompilerParams(
            dimension_semantics=("arbitrary",), vmem_limit_bytes=VMEM_LIMIT),
        name="route",
    )(lt, bias.reshape(ne, 1), su)


def _pos_kernel(idx_ref, rank_ref, st_ref, pos_ref):
    ne = st_ref.shape[0]
    tp = idx_ref.shape[1]
    eio = lax.broadcasted_iota(I32, (ne, tp), 0)
    st = st_ref[...]
    for k in range(TOP_K):
        base = jnp.sum(jnp.where(eio == idx_ref[k:k + 1, :], st, 0), axis=0, keepdims=True)
        pos_ref[k:k + 1, :] = base + rank_ref[k:k + 1, :]


def _pos_call(top_idx, rank, starts):
    kk, n = top_idx.shape
    ne = starts.shape[0]
    tp = POS_TILE
    return pl.pallas_call(
        _pos_kernel,
        out_shape=jax.ShapeDtypeStruct((kk, n), I32),
        grid=(n // tp,),
        in_specs=[pl.BlockSpec((kk, tp), lambda i: (0, i)),
                  pl.BlockSpec((kk, tp), lambda i: (0, i)),
                  pl.BlockSpec((ne, 1), lambda i: (0, 0))],
        out_specs=pl.BlockSpec((kk, tp), lambda i: (0, i)),
        compiler_params=pltpu.CompilerParams(
            dimension_semantics=("arbitrary",), vmem_limit_bytes=VMEM_LIMIT),
        name="pos",
    )(top_idx, rank, starts.reshape(ne, 1))


def _sc_mesh():
    return plsc.VectorSubcoreMesh(core_axis_name="c", subcore_axis_name="s",
                                  num_cores=SC_CORES, num_subcores=SC_SUBCORES)


def _sc_worker():
    return lax.axis_index("s") * SC_CORES + lax.axis_index("c")


def _dispatch_call(pos_win, hp):
    n = hp.shape[0]
    n_workers = SC_CORES * SC_SUBCORES
    wins_per_worker = n // SC_WINDOW // n_workers

    def body(hp_hbm, pos_hbm, xs_hbm, idx_v, rows_v, sem):
        first_win = _sc_worker() * wins_per_worker

        @pl.loop(0, wins_per_worker)
        def _(w):
            win = first_win + w
            pltpu.sync_copy(hp_hbm.at[pl.ds(win * SC_WINDOW, SC_WINDOW)], rows_v)
            pltpu.sync_copy(pos_hbm.at[win], idx_v)
            copies = [pltpu.async_copy(rows_v, xs_hbm.at[idx_v.at[k]], sem) for k in range(TOP_K)]
            for cp in copies:
                cp.wait()

    return pl.kernel(
        body,
        out_type=jax.ShapeDtypeStruct((n * TOP_K,) + hp.shape[1:], hp.dtype),
        mesh=_sc_mesh(),
        scratch_types=[pltpu.VMEM((TOP_K, SC_WINDOW), I32),
                       pltpu.VMEM((SC_WINDOW,) + hp.shape[1:], hp.dtype),
                       pltpu.SemaphoreType.DMA],
        name="dispatch",
    )(hp, pos_win)


def _collect_call(pos_win, y_sorted):
    n = pos_win.shape[0] * SC_WINDOW
    n_workers = SC_CORES * SC_SUBCORES
    wins_per_worker = n // SC_WINDOW // n_workers

    def body(ys_hbm, pos_hbm, out_hbm, idx_v, rows_a, rows_b, gather_sem, write_sem):
        first_win = _sc_worker() * wins_per_worker
        bufs = (rows_a, rows_b)

        @pl.loop(0, wins_per_worker)
        def _(w):
            win = first_win + w
            pltpu.sync_copy(pos_hbm.at[win], idx_v)

            def gather(k):
                return pltpu.async_copy(ys_hbm.at[idx_v.at[k]], bufs[k % 2], gather_sem)

            pending_gather = gather(0)
            pending_write = None
            for k in range(TOP_K):
                pending_gather.wait()
                if pending_write is not None:
                    for cp in pending_write:
                        cp.wait()
                if k + 1 < TOP_K:
                    pending_gather = gather(k + 1)
                pending_write = [
                    pltpu.async_copy(bufs[k % 2].at[:, j],
                                     out_hbm.at[j, k, pl.ds(win * SC_WINDOW, SC_WINDOW)], write_sem)
                    for j in range(PACK_ROWS)]
            for cp in pending_write:
                cp.wait()

    row_buf = pltpu.VMEM((SC_WINDOW,) + y_sorted.shape[1:], y_sorted.dtype)
    return pl.kernel(
        body,
        out_type=jax.ShapeDtypeStruct((PACK_ROWS, TOP_K, n, LANES), y_sorted.dtype),
        mesh=_sc_mesh(),
        scratch_types=[pltpu.VMEM((TOP_K, SC_WINDOW), I32), row_buf, row_buf,
                       pltpu.SemaphoreType.DMA, pltpu.SemaphoreType.DMA],
        name="collect",
    )(y_sorted, pos_win)


def _experts_kernel(pe_ref, pb_ref, pv_ref, st_ref, en_ref,
                    xs_hbm, wg_ref, wu_ref, wd_ref, y_hbm,
                    wgb_ref, wub_ref, wdb_ref, acc_ref, xbuf_ref, stage_ref,
                    in_sems, out_sem, state_ref):
    p = pl.program_id(0)
    n_pairs = pl.num_programs(0)
    e = pe_ref[p]
    blk = pb_ref[p]
    rb = acc_ref.shape[0]
    prev = jnp.maximum(p - 1, 0)
    nxt = jnp.minimum(p + 1, n_pairs - 1)
    live = pv_ref[p] == 1
    first = jnp.logical_or(p == 0, pb_ref[prev] != blk)
    block_ends = jnp.logical_or(pb_ref[nxt] != blk, pv_ref[nxt] == 0)
    last = jnp.logical_or(p == n_pairs - 1, block_ends)

    def in_copy(j, block, slot):
        return pltpu.make_async_copy(xs_hbm.at[pl.ds(block * rb, rb), j], xbuf_ref.at[slot, j],
                                     in_sems.at[slot])

    def out_copy(j, block):
        return pltpu.make_async_copy(stage_ref.at[j], y_hbm.at[pl.ds(block * rb, rb), j], out_sem)

    def drain():
        @pl.when(state_ref[1] == 1)
        def _():
            for j in range(PACK_ROWS):
                out_copy(j, 0).wait()
            state_ref[1] = 0

    def emit(val):
        drain()
        for j in range(PACK_ROWS):
            lo = val[:, 2 * j * LANES:(2 * j + 1) * LANES]
            hi = val[:, (2 * j + 1) * LANES:(2 * j + 2) * LANES]
            stage_ref[j] = pltpu.pack_elementwise([lo, hi], packed_dtype=BF16)
        for j in range(PACK_ROWS):
            out_copy(j, blk).start()
        state_ref[1] = 1

    @pl.when(p == 0)
    def _():
        state_ref[0] = 0
        state_ref[1] = 0
        for j in range(PACK_ROWS):
            in_copy(j, blk, 0).start()

    slot = jnp.where(jnp.logical_and(first, p > 0), 1 - state_ref[0], state_ref[0])
    state_ref[0] = slot

    @pl.when(jnp.logical_and(p < n_pairs - 1,
                             jnp.logical_and(pb_ref[nxt] != blk, pv_ref[nxt] == 1)))
    def _():
        for j in range(PACK_ROWS):
            in_copy(j, pb_ref[nxt], 1 - slot).start()

    @pl.when(first)
    def _():
        for j in range(PACK_ROWS):
            in_copy(j, blk, slot).wait()

    @pl.when(jnp.logical_or(p == 0, pe_ref[prev] != e))
    def _():
        wgb_ref[...] = wg_ref[0].astype(BF16)
        wub_ref[...] = wu_ref[0].astype(BF16)
        wdb_ref[...] = wd_ref[0].astype(BF16)

    @pl.when(live)
    def _():
        pieces = []
        for j in range(PACK_ROWS):
            w = xbuf_ref[slot, j]
            pieces.append(lax.bitcast_convert_type(w.astype(jnp.int16), BF16))
            pieces.append(lax.bitcast_convert_type(
                lax.shift_right_logical(w, 16).astype(jnp.int16), BF16))
        xb = jnp.concatenate(pieces, axis=1)
        gate = jnp.dot(xb, wgb_ref[...], preferred_element_type=F32)
        up = jnp.dot(xb, wub_ref[...], preferred_element_type=F32)
        rows = blk * rb + lax.broadcasted_iota(I32, (rb, 1), 0)
        mine = jnp.logical_and(rows >= st_ref[e], rows < en_ref[e])
        hid = jnp.where(mine, _silu(gate) * up, 0.0).astype(BF16)
        yb = jnp.dot(hid, wdb_ref[...], preferred_element_type=F32)
        not_first = jnp.logical_not(first)
        not_last = jnp.logical_not(last)

        @pl.when(jnp.logical_and(first, last))
        def _():
            emit(yb)

        @pl.when(jnp.logical_and(first, not_last))
        def _():
            acc_ref[...] = yb

        @pl.when(jnp.logical_and(not_first, not_last))
        def _():
            acc_ref[...] += yb

        @pl.when(jnp.logical_and(not_first, last))
        def _():
            emit(acc_ref[...] + yb)

    @pl.when(p == n_pairs - 1)
    def _():
        drain()


def _experts_call(pair_e, pair_blk, pair_ok, starts, ends, xs, w_gate, w_up, w_down):
    ne, d, de = w_gate.shape
    rb = ROW_BLOCK
    n_pairs = pair_e.shape[0]
    grid_spec = pltpu.PrefetchScalarGridSpec(
        num_scalar_prefetch=5,
        grid=(n_pairs,),
        in_specs=[pl.BlockSpec(memory_space=pl.ANY),
                  pl.BlockSpec((1, d, de), lambda p, pe, pb, pv, st, en: (pe[p], 0, 0)),
                  pl.BlockSpec((1, d, de), lambda p, pe, pb, pv, st, en: (pe[p], 0, 0)),
                  pl.BlockSpec((1, de, d), lambda p, pe, pb, pv, st, en: (pe[p], 0, 0))],
        out_specs=pl.BlockSpec(memory_space=pl.ANY),
        scratch_shapes=[pltpu.VMEM((d, de), BF16), pltpu.VMEM((d, de), BF16),
                        pltpu.VMEM((de, d), BF16), pltpu.VMEM((rb, d), F32),
                        pltpu.VMEM((2, PACK_ROWS, rb, LANES), I32),
                        pltpu.VMEM((PACK_ROWS, rb, LANES), I32),
                        pltpu.SemaphoreType.DMA((2,)), pltpu.SemaphoreType.DMA,
                        pltpu.SMEM((2,), I32)],
    )
    return pl.pallas_call(
        _experts_kernel,
        out_shape=jax.ShapeDtypeStruct(xs.shape, xs.dtype),
        grid_spec=grid_spec,
        compiler_params=pltpu.CompilerParams(
            dimension_semantics=("arbitrary",), vmem_limit_bytes=VMEM_LIMIT),
        name="experts",
    )(pair_e, pair_blk, pair_ok, starts, ends, xs, w_gate, w_up, w_down)


def _combine_kernel(y0_ref, y1_ref, y2_ref, y3_ref, x1_ref, w_ref, g2_ref, fg_ref, o_ref):
    w = w_ref[...]
    parts = []
    for y_ref in (y0_ref, y1_ref, y2_ref, y3_ref):
        for half in range(2):
            acc = None
            for k in range(TOP_K):
                piece = pltpu.unpack_elementwise(y_ref[k], index=half, packed_dtype=BF16,
                                                 unpacked_dtype=F32) * w[:, k:k + 1]
                acc = piece if acc is None else acc + piece
            parts.append(acc)
    x2 = x1_ref[...] + g2_ref[0] * jnp.concatenate(parts, axis=1)
    ms = jnp.mean(x2 * x2, axis=-1, keepdims=True)
    o_ref[...] = x2 * lax.rsqrt(ms + EPS) * fg_ref[...]


def _combine_call(y_tok, x1s, w_tok, g2, fgain, seq):
    n, d = x1s.shape
    tm = MOVE_TILE
    per_b = seq // tm

    def piece_spec(j):
        return pl.BlockSpec((None, TOP_K, tm, LANES), lambda i: (j, 0, i, 0))

    return pl.pallas_call(
        _combine_kernel,
        out_shape=jax.ShapeDtypeStruct((n, d), F32),
        grid=(n // tm,),
        in_specs=[piece_spec(j) for j in range(PACK_ROWS)] + [
                  pl.BlockSpec((tm, d), lambda i: (i, 0)),
                  pl.BlockSpec((tm, TOP_K), lambda i: (i, 0)),
                  pl.BlockSpec((1, 1, d), lambda i: (i // per_b, 0, 0)),
                  pl.BlockSpec((1, d), lambda i: (0, 0))],
        out_specs=pl.BlockSpec((tm, d), lambda i: (i, 0)),
        compiler_params=pltpu.CompilerParams(
            dimension_semantics=("arbitrary",), vmem_limit_bytes=VMEM_LIMIT),
        name="combine",
    )(y_tok, y_tok, y_tok, y_tok, x1s, w_tok, g2, fgain)


def _pair_tables(counts, n_rows):
    ne = counts.shape[0]
    sizes = counts.astype(I32)
    ends = jnp.cumsum(sizes)
    starts = ends - sizes
    first_blk = starts // ROW_BLOCK
    last_blk = (ends - 1) // ROW_BLOCK
    n_pairs = jnp.where(sizes > 0, last_blk - first_blk + 1, 0)
    pair_end = jnp.cumsum(n_pairs)
    pair_start = pair_end - n_pairs
    max_pairs = n_rows // ROW_BLOCK + ne
    p = jnp.arange(max_pairs, dtype=I32)
    ok = p < pair_end[-1]
    pc = jnp.minimum(p, pair_end[-1] - 1)
    pair_e = jnp.minimum(jnp.searchsorted(pair_end, pc, side='right'), ne - 1).astype(I32)
    pair_blk = (first_blk[pair_e] + pc - pair_start[pair_e]).astype(I32)
    return pair_e, pair_blk, ok.astype(I32), starts.astype(I32), ends.astype(I32)


def kernel(x, c, w_ada, b_ada, w_in, lb_logits, hgrn_norm, lam_re, lam_im, log_dt, b_re, b_im,
           c_re, c_im, d_skip, w_glu, b_glu, w_out, w_router, router_bias, w_gate, w_up, w_down,
           ws_gate, ws_up, ws_down, final_gain):
    bsz, seq, d = x.shape
    n = bsz * seq
    fdim = HGRN_HEADS * HGRN_KDIM
    n_chunks = seq // CHUNK
    lb = jnp.cumsum(jax.nn.softmax(lb_logits.astype(F32), axis=0), axis=0)[0].reshape(1, fdim)

    mod = _mod_call(c, w_ada[0], b_ada[0]).reshape(bsz, 6, d)
    nb = w_in.shape[2] - 4 * fdim
    perm = jnp.concatenate([jnp.arange(0, nb, 2), jnp.arange(1, nb, 2)])
    w_in_b = w_in[0].astype(BF16)
    ut, out_a = _mix_front_call(x, mod, w_in_b[:, :4 * fdim], w_in_b[:, 4 * fdim:][:, perm].T, lb,
                                hgrn_norm[0].reshape(1, fdim))

    m_tab, p_tab, q_tab, a1, a2 = _s5_tables(lam_re[0], lam_im[0], log_dt[0], b_re[0], b_im[0],
                                             c_re[0], c_im[0])
    yt = _s5_call(ut, m_tab, p_tab, q_tab, a1, a2, n_chunks, bsz)

    wr_t = w_router[0].T
    wr_hi = wr_t.astype(BF16)
    wr_lo = (wr_t - wr_hi.astype(F32)).astype(BF16)
    w_out_b = w_out[0].astype(BF16)
    w_out_p = jnp.concatenate([w_out_b[:fdim], w_out_b[fdim:][perm]], axis=0)
    x1s, hp, logits_t = _mix_back_call(
        x.reshape(n, d), out_a.reshape(n, fdim), yt, ut, mod,
        d_skip[0][perm].reshape(nb, 1), w_glu[0][perm][:, perm].T.astype(BF16),
        b_glu[0][perm].reshape(nb, 1), w_out_p, wr_hi, wr_lo, ws_gate[0].astype(BF16),
        ws_up[0].astype(BF16), ws_down[0].astype(BF16), seq)

    top_idx, top_w, rank, counts = _route_call(logits_t, router_bias[0])
    pair_e, pair_blk, pair_ok, starts, ends = _pair_tables(counts[:, 0], n * TOP_K)
    pos = _pos_call(top_idx, rank, starts)

    pos_win = pos.reshape(TOP_K, n // SC_WINDOW, SC_WINDOW).transpose(1, 0, 2)
    xs = _dispatch_call(pos_win, hp.reshape(n, PACK_ROWS, LANES))
    y_sorted = _experts_call(pair_e, pair_blk, pair_ok, starts, ends, xs, w_gate[0], w_up[0], w_down[0])
    y_tok = _collect_call(pos_win, y_sorted)
    out = _combine_call(y_tok, x1s, top_w.T, mod[:, 5:6, :], final_gain.reshape(1, d), seq)
    return out.reshape(bsz, seq, d)
```

```python
import functools

import jax
import jax.numpy as jnp
from jax import lax
from jax.experimental import pallas as pl
from jax.experimental.pallas import tpu as pltpu
from jax.experimental.pallas import tpu_sc as plsc

F32 = jnp.float32
BF16 = jnp.bfloat16
I32 = jnp.int32

EPS = 1e-6
CHUNK = 64
HGRN_HEADS = 4
HGRN_KDIM = 128
S5_GROUP = 16
S5_STATE = 64
N_EXPERT_GROUPS = 8
TOPK_GROUPS = 4
TOP_K = 8
ROUTE_SCALE = 2.5
LANES = 128
PACK_ROWS = 4
SC_CORES = 2
SC_SUBCORES = 16
SC_WINDOW = 64

SEQ_TILE = 512
TOK_TILE = 512
ROUTE_TILE = 256
POS_TILE = 1024
MOVE_TILE = 256
ROW_BLOCK = 1024
VMEM_LIMIT = 56 * 1024 * 1024

_NT = (((1,), (1,)), ((), ()))
_TN = (((0,), (0,)), ((), ()))


def _sigmoid(v):
    return 1.0 / (1.0 + jnp.exp(-v))


def _silu(v):
    return v * _sigmoid(v)


def _bdot(a, b):
    return jnp.dot(a.astype(BF16), b.astype(BF16), preferred_element_type=F32)


def _store_packed(ref, val, n_rows):
    for j in range(PACK_ROWS):
        lo = val[:, 2 * j * LANES:(2 * j + 1) * LANES]
        hi = val[:, (2 * j + 1) * LANES:(2 * j + 2) * LANES]
        ref[pl.ds(j, n_rows, stride=PACK_ROWS), :] = pltpu.pack_elementwise([lo, hi], packed_dtype=BF16)


def _mod_kernel(c_ref, w_ref, b_ref, o_ref):
    o_ref[...] = _bdot(_silu(c_ref[...]), w_ref[...]) + b_ref[...]


def _mod_call(c, w_ada, b_ada):
    bsz, d = c.shape
    n_out = w_ada.shape[1]
    return pl.pallas_call(
        _mod_kernel,
        out_shape=jax.ShapeDtypeStruct((bsz, n_out), F32),
        grid=(n_out // d,),
        in_specs=[pl.BlockSpec((bsz, d), lambda j: (0, 0)),
                  pl.BlockSpec((d, d), lambda j: (0, j)),
                  pl.BlockSpec((1, d), lambda j: (0, j))],
        out_specs=pl.BlockSpec((bsz, d), lambda j: (0, j)),
        compiler_params=pltpu.CompilerParams(vmem_limit_bytes=VMEM_LIMIT),
        name="mod",
    )(c, w_ada, b_ada.reshape(1, n_out))


def _split_chunk_pairs(tile_even, tile_odd):
    low = lax.broadcasted_iota(I32, tile_even.shape, 1) < CHUNK
    first = jnp.where(low, tile_even, pltpu.roll(tile_odd, CHUNK, 1))
    second = jnp.where(low, pltpu.roll(tile_even, CHUNK, 1), tile_odd)
    return first, second


def _mix_front_kernel(x_ref, mod_ref, win_ref, wut_ref, lb_ref, gn_ref, ltri_ref,
                      ut_ref, oa_ref, proj_ref, st_ref, flat_ref):
    fdim = HGRN_HEADS * HGRN_KDIM
    ts = x_ref.shape[1]
    pairs = wut_ref.shape[0] // 2
    rows = ts // CHUNK

    @pl.when(pl.program_id(1) == 0)
    def _():
        st_ref[...] = jnp.zeros_like(st_ref)

    x = x_ref[0]
    ms = jnp.mean(x * x, axis=-1, keepdims=True)
    h = x * lax.rsqrt(ms + EPS) * (1.0 + mod_ref[0, 1:2, :]) + mod_ref[0, 0:1, :]
    hb = h.astype(BF16)
    proj_ref[...] = jnp.dot(hb, win_ref[...], preferred_element_type=F32)
    u_t = lax.dot_general(wut_ref[...], hb, _NT, preferred_element_type=F32)
    for m in range(ts // LANES):
        first, second = _split_chunk_pairs(u_t[:pairs, m * LANES:(m + 1) * LANES],
                                           u_t[pairs:, m * LANES:(m + 1) * LANES])
        flat_ref[2 * m * pairs:(2 * m + 1) * pairs, :] = first
        flat_ref[(2 * m + 1) * pairs:(2 * m + 2) * pairs, :] = second
    per_group = S5_GROUP // 2
    for q in range(pairs):
        ut_ref[q // per_group, :, (q % per_group) * LANES:(q % per_group + 1) * LANES] = (
            flat_ref[pl.ds(q, rows, stride=pairs), :])

    lb = lb_ref[...]
    gn = gn_ref[...]
    ltri = ltri_ref[...]
    row = lax.broadcasted_iota(I32, (CHUNK, CHUNK), 0)
    col = lax.broadcasted_iota(I32, (CHUNK, CHUNK), 1)
    causal = row >= col

    def chunk_step(ci, carry):
        r0 = pl.multiple_of(ci * CHUNK, CHUNK)
        q = proj_ref[pl.ds(r0, CHUNK), 0:fdim]
        fl = proj_ref[pl.ds(r0, CHUNK), fdim:2 * fdim]
        iv = proj_ref[pl.ds(r0, CHUNK), 2 * fdim:3 * fdim]
        og = proj_ref[pl.ds(r0, CHUNK), 3 * fdim:4 * fdim]
        f = lb + (1.0 - lb) * _sigmoid(fl)
        b = jnp.dot(ltri, jnp.log(f), precision=lax.Precision.HIGHEST,
                    preferred_element_type=F32)
        b_ref = b[CHUNK // 2 - 1:CHUNK // 2, :]
        b_last = b[CHUNK - 1:CHUNK, :]
        qs = _silu(q)
        kk = 1.0 - f
        qe = (qs * jnp.exp(b - b_ref)).astype(BF16)
        ke = (kk * jnp.exp(b_ref - b)).astype(BF16)
        qb = (qs * jnp.exp(b)).astype(BF16)
        k2 = (kk * jnp.exp(b_last - b)).astype(BF16)
        dec = jnp.exp(b_last)
        ivb = iv.astype(BF16)
        outs = []
        for hh in range(HGRN_HEADS):
            sl = slice(hh * HGRN_KDIM, (hh + 1) * HGRN_KDIM)
            att = lax.dot_general(qe[:, sl], ke[:, sl], _NT, preferred_element_type=F32)
            att = jnp.where(causal, att, 0.0)
            st = st_ref[hh]
            o = jnp.dot(att.astype(BF16), ivb[:, sl], preferred_element_type=F32)
            o = o + lax.dot_general(qb[:, sl], st.astype(BF16), _NT, preferred_element_type=F32)
            st_ref[hh] = st * dec[:, sl] + lax.dot_general(
                ivb[:, sl], k2[:, sl], _TN, preferred_element_type=F32)
            outs.append(o * lax.rsqrt(jnp.mean(o * o, axis=-1, keepdims=True) + EPS))
        o = jnp.concatenate(outs, axis=1) * gn * _silu(og)
        oa_ref[0, pl.ds(r0, CHUNK), :] = o.astype(BF16)
        return carry

    lax.fori_loop(0, ts // CHUNK, chunk_step, 0, unroll=True)


def _mix_front_call(x, mod, w_main, w_ut, lb, gn):
    bsz, seq, d = x.shape
    fdim = HGRN_HEADS * HGRN_KDIM
    ncols = w_main.shape[1]
    nb = w_ut.shape[0]
    groups = nb // S5_GROUP
    ltri = jnp.tril(jnp.ones((CHUNK, CHUNK), F32))
    ts = SEQ_TILE
    tiles = seq // ts
    rows = ts // CHUNK
    return pl.pallas_call(
        _mix_front_kernel,
        out_shape=(jax.ShapeDtypeStruct((groups, bsz * seq // CHUNK, S5_GROUP * CHUNK), F32),
                   jax.ShapeDtypeStruct((bsz, seq, fdim), BF16)),
        grid=(bsz, tiles),
        in_specs=[pl.BlockSpec((1, ts, d), lambda b, j: (b, j, 0)),
                  pl.BlockSpec((1, 6, d), lambda b, j: (b, 0, 0)),
                  pl.BlockSpec((d, ncols), lambda b, j: (0, 0)),
                  pl.BlockSpec((nb, d), lambda b, j: (0, 0)),
                  pl.BlockSpec((1, fdim), lambda b, j: (0, 0)),
                  pl.BlockSpec((1, fdim), lambda b, j: (0, 0)),
                  pl.BlockSpec((CHUNK, CHUNK), lambda b, j: (0, 0))],
        out_specs=(pl.BlockSpec((groups, rows, S5_GROUP * CHUNK), lambda b, j: (0, b * tiles + j, 0)),
                   pl.BlockSpec((1, ts, fdim), lambda b, j: (b, j, 0))),
        scratch_shapes=[pltpu.VMEM((ts, ncols), F32),
                        pltpu.VMEM((HGRN_HEADS, fdim // HGRN_HEADS, HGRN_KDIM), F32),
                        pltpu.VMEM((rows * nb // 2, LANES), F32)],
        compiler_params=pltpu.CompilerParams(
            dimension_semantics=("arbitrary", "arbitrary"), vmem_limit_bytes=VMEM_LIMIT),
        name="mix_front",
    )(x, mod, w_main, w_ut, lb, gn, ltri)


def _s5_tables(lam_re, lam_im, log_dt, b_re, b_im, c_re, c_im):
    t = CHUNK
    hp = lax.Precision.HIGHEST
    lam = lax.complex(jnp.minimum(lam_re, -1e-4), lam_im)
    lam_dt = lam * jnp.exp(log_dt)[:, None]
    lam_bar = jnp.exp(lam_dt)
    b_bar = ((lam_bar - 1.0) / lam)[..., None] * lax.complex(b_re, b_im)
    c_mat = lax.complex(c_re, c_im)
    taus = jnp.arange(t + 1, dtype=F32)
    lam_pow = jnp.exp(lam_dt[:, None, :] * taus[None, :, None])
    g, p = lam.shape
    c = b_re.shape[-1]
    cl = c_mat[:, None, :, :] * lam_pow[:, :t, None, :]
    kr = (jnp.einsum('gtcp,gpi->gtci', cl.real, b_bar.real, precision=hp)
          - jnp.einsum('gtcp,gpi->gtci', cl.imag, b_bar.imag, precision=hp))
    z = jnp.pad(kr.transpose(0, 3, 2, 1).astype(BF16), ((0, 0), (0, 0), (0, 0), (t - 1, 0)))
    kg = jnp.stack([z[..., t - 1 - s:2 * t - 1 - s] for s in range(t)], axis=2)
    m = kg.reshape(g, c * t, c * t)
    pc = lam_pow[:, t - 1::-1][:, :t, :, None] * b_bar[:, None, :, :]
    pc = pc.transpose(0, 3, 1, 2).reshape(g, c * t, p)
    p_tab = jnp.concatenate([pc.real, pc.imag], axis=-1)
    ql = c_mat[:, None, :, :] * lam_pow[:, 1:t + 1, None, :]
    ql = ql.transpose(0, 3, 2, 1).reshape(g, p, c * t)
    q_tab = jnp.concatenate([ql.real, -ql.imag], axis=1)
    lam_t = lam_pow[:, t]
    a1 = jnp.concatenate([lam_t.real, lam_t.real], axis=-1)[:, None, :]
    a2 = jnp.concatenate([-lam_t.imag, lam_t.imag], axis=-1)[:, None, :]
    return m.astype(BF16), p_tab.astype(BF16), q_tab.astype(BF16), a1, a2


def _s5_kernel(u_ref, m_ref, p_ref, q_ref, a1_ref, a2_ref, y_ref, v_ref, xs_ref, *, n_chunks, n_batch):
    u = u_ref[0].astype(BF16)
    v_ref[...] = jnp.dot(u, p_ref[0], preferred_element_type=F32)
    a1 = a1_ref[0]
    a2 = a2_ref[0]
    half = xs_ref.shape[1] // 2

    def step(n, state):
        xs_ref[pl.ds(n, n_batch, stride=n_chunks), :] = state
        return (a1 * state + a2 * pltpu.roll(state, half, 1)
                + v_ref[pl.ds(n, n_batch, stride=n_chunks), :])

    lax.fori_loop(0, n_chunks, step, jnp.zeros((n_batch, xs_ref.shape[1]), F32))
    y = jnp.dot(u, m_ref[0], preferred_element_type=F32)
    y_ref[0] = y + jnp.dot(xs_ref[...].astype(BF16), q_ref[0], preferred_element_type=F32)


def _s5_call(ut, m, p_tab, q_tab, a1, a2, n_chunks, n_batch):
    g, rows, width = ut.shape
    p2 = p_tab.shape[-1]
    return pl.pallas_call(
        functools.partial(_s5_kernel, n_chunks=n_chunks, n_batch=n_batch),
        out_shape=jax.ShapeDtypeStruct((g, rows, width), F32),
        grid=(g,),
        in_specs=[pl.BlockSpec((1, rows, width), lambda i: (i, 0, 0)),
                  pl.BlockSpec((1, width, width), lambda i: (i, 0, 0)),
                  pl.BlockSpec((1, width, p2), lambda i: (i, 0, 0)),
                  pl.BlockSpec((1, p2, width), lambda i: (i, 0, 0)),
                  pl.BlockSpec((1, 1, p2), lambda i: (i, 0, 0)),
                  pl.BlockSpec((1, 1, p2), lambda i: (i, 0, 0))],
        out_specs=pl.BlockSpec((1, rows, width), lambda i: (i, 0, 0)),
        scratch_shapes=[pltpu.VMEM((rows, p2), F32), pltpu.VMEM((rows, p2), F32)],
        compiler_params=pltpu.CompilerParams(
            dimension_semantics=("arbitrary",), vmem_limit_bytes=VMEM_LIMIT),
        name="s5",
    )(ut, m, p_tab, q_tab, a1, a2)


def _token_major(flat_ref, src_ref):
    groups, rows, _ = src_ref.shape
    per_group = S5_GROUP // 2
    pairs = groups * per_group
    for q in range(pairs):
        flat_ref[pl.ds(q, rows, stride=pairs), :] = (
            src_ref[q // per_group, :, (q % per_group) * LANES:(q % per_group + 1) * LANES])
    tiles = []
    for m in range(rows // 2):
        even, odd = _split_chunk_pairs(flat_ref[2 * m * pairs:(2 * m + 1) * pairs, :],
                                       flat_ref[(2 * m + 1) * pairs:(2 * m + 2) * pairs, :])
        tiles.append(jnp.concatenate([even, odd], axis=0))
    return jnp.concatenate(tiles, axis=1)


def _mix_back_kernel(x_ref, oa_ref, yt_ref, ut_ref, mod_ref, dskip_ref, wglu_ref, bglu_ref,
                     wout_ref, wrh_ref, wrl_ref, wsg_ref, wsu_ref, wsd_ref,
                     x1_ref, hp_ref, lt_ref, flat_ref):
    na = oa_ref.shape[1]
    y_t = _token_major(flat_ref, yt_ref)
    u_t = _token_major(flat_ref, ut_ref)
    z_t = jax.nn.gelu(y_t + dskip_ref[...] * u_t)
    gate_t = _sigmoid(jnp.dot(wglu_ref[...], z_t.astype(BF16), preferred_element_type=F32)
                      + bglu_ref[...])
    ob_t = (z_t * gate_t).astype(BF16)
    mixed = (jnp.dot(oa_ref[...], wout_ref[0:na, :], preferred_element_type=F32)
             + lax.dot_general(ob_t, wout_ref[na:, :], _TN, preferred_element_type=F32))
    x1 = x_ref[...] + mod_ref[0, 2:3, :] * mixed
    ms = jnp.mean(x1 * x1, axis=-1, keepdims=True)
    h2 = x1 * lax.rsqrt(ms + EPS) * (1.0 + mod_ref[0, 4:5, :]) + mod_ref[0, 3:4, :]
    _store_packed(hp_ref, h2, h2.shape[0])
    h_hi = h2.astype(BF16)
    h_lo = (h2 - h_hi.astype(F32)).astype(BF16)
    lt = lax.dot_general(wrh_ref[...], h_hi, _NT, preferred_element_type=F32)
    lt = lt + lax.dot_general(wrl_ref[...], h_hi, _NT, preferred_element_type=F32)
    lt = lt + lax.dot_general(wrh_ref[...], h_lo, _NT, preferred_element_type=F32)
    lt_ref[...] = lt
    hid = _silu(jnp.dot(h_hi, wsg_ref[...], preferred_element_type=F32)) * jnp.dot(
        h_hi, wsu_ref[...], preferred_element_type=F32)
    shared = jnp.dot(hid.astype(BF16), wsd_ref[...], preferred_element_type=F32)
    x1_ref[...] = x1 + mod_ref[0, 5:6, :] * shared


def _mix_back_call(x2d, oa, yt, ut, mod, dskip, wglu, bglu, wout, wrh, wrl, wsg, wsu, wsd, seq):
    n, d = x2d.shape
    nb = oa.shape[1]
    ne = wrh.shape[0]
    dsh = wsg.shape[1]
    tm = TOK_TILE
    per_b = seq // tm
    groups, _, width = yt.shape
    rows = tm // CHUNK
    const = lambda i: (0, 0)
    flat_block = pl.BlockSpec((groups, rows, width), lambda i: (0, i, 0))
    return pl.pallas_call(
        _mix_back_kernel,
        out_shape=(jax.ShapeDtypeStruct((n, d), F32),
                   jax.ShapeDtypeStruct((n * PACK_ROWS, LANES), I32),
                   jax.ShapeDtypeStruct((ne, n), F32)),
        grid=(n // tm,),
        in_specs=[pl.BlockSpec((tm, d), lambda i: (i, 0)),
                  pl.BlockSpec((tm, nb), lambda i: (i, 0)),
                  flat_block,
                  flat_block,
                  pl.BlockSpec((1, 6, d), lambda i: (i // per_b, 0, 0)),
                  pl.BlockSpec((nb, 1), const),
                  pl.BlockSpec((nb, nb), const),
                  pl.BlockSpec((nb, 1), const),
                  pl.BlockSpec((d, d), const),
                  pl.BlockSpec((ne, d), const),
                  pl.BlockSpec((ne, d), const),
                  pl.BlockSpec((d, dsh), const),
                  pl.BlockSpec((d, dsh), const),
                  pl.BlockSpec((dsh, d), const)],
        out_specs=(pl.BlockSpec((tm, d), lambda i: (i, 0)),
                   pl.BlockSpec((tm * PACK_ROWS, LANES), lambda i: (i, 0)),
                   pl.BlockSpec((ne, tm), lambda i: (0, i))),
        scratch_shapes=[pltpu.VMEM((rows * nb // 2, LANES), F32)],
        compiler_params=pltpu.CompilerParams(
            dimension_semantics=("arbitrary",), vmem_limit_bytes=VMEM_LIMIT),
        name="mix_back",
    )(x2d, oa, yt, ut, mod, dskip, wglu, bglu, wout, wrh, wrl, wsg, wsu, wsd)


def _route_kernel(lt_ref, bias_ref, su_ref, idx_ref, w_ref, rank_ref, cnt_ref, run_ref):
    ne, tr = lt_ref.shape
    per_group = ne // N_EXPERT_GROUPS
    neg = -jnp.inf

    @pl.when(pl.program_id(0) == 0)
    def _():
        run_ref[...] = jnp.zeros_like(run_ref)

    s = _sigmoid(lt_ref[...])
    sel = s + bias_ref[...]
    gio = lax.broadcasted_iota(I32, (per_group, tr), 0)
    gscore = []
    for g in range(N_EXPERT_GROUPS):
        v = sel[g * per_group:(g + 1) * per_group, :]
        m1 = jnp.max(v, axis=0, keepdims=True)
        i1 = jnp.min(jnp.where(v == m1, gio, per_group), axis=0, keepdims=True)
        m2 = jnp.max(jnp.where(gio == i1, neg, v), axis=0, keepdims=True)
        gscore.append(m1 + m2)
    masked = []
    for g in range(N_EXPERT_GROUPS):
        ahead = jnp.zeros((1, tr), I32)
        for o in range(N_EXPERT_GROUPS):
            if o == g:
                continue
            wins = (gscore[o] >= gscore[g]) if o < g else (gscore[o] > gscore[g])
            ahead = ahead + wins.astype(I32)
        keep = ahead < TOPK_GROUPS
        masked.append(jnp.where(keep, sel[g * per_group:(g + 1) * per_group, :], neg))
    selm = jnp.concatenate(masked, axis=0)
    eio = lax.broadcasted_iota(I32, (ne, tr), 0)
    hits = jnp.zeros((ne, tr), F32)
    idxs, ws = [], []
    for k in range(TOP_K):
        m = jnp.max(selm, axis=0, keepdims=True)
        ik = jnp.min(jnp.where(selm == m, eio, ne), axis=0, keepdims=True)
        onehot = eio == ik
        ws.append(jnp.sum(jnp.where(onehot, s, 0.0), axis=0, keepdims=True))
        hits = jnp.where(onehot, 1.0, hits)
        selm = jnp.where(onehot, neg, selm)
        idxs.append(ik)
    wsum = ws[0]
    for k in range(1, TOP_K):
        wsum = wsum + ws[k]
    scale = ROUTE_SCALE / wsum
    ranks = jnp.dot(hits.astype(BF16), su_ref[...], preferred_element_type=F32) + run_ref[...]
    for k in range(TOP_K):
        idx_ref[k:k + 1, :] = idxs[k]
        w_ref[k:k + 1, :] = ws[k] * scale
        rk = jnp.sum(jnp.where(eio == idxs[k], ranks, 0.0), axis=0, keepdims=True)
        rank_ref[k:k + 1, :] = rk.astype(I32)
    run_ref[...] = run_ref[...] + jnp.sum(hits, axis=1, keepdims=True)
    cnt_ref[...] = run_ref[...]


def _route_call(lt, bias):
    ne, n = lt.shape
    tr = ROUTE_TILE
    su = jnp.triu(jnp.ones((tr, tr), F32), k=1).astype(BF16)
    return pl.pallas_call(
        _route_kernel,
        out_shape=(jax.ShapeDtypeStruct((TOP_K, n), I32),
                   jax.ShapeDtypeStruct((TOP_K, n), F32),
                   jax.ShapeDtypeStruct((TOP_K, n), I32),
                   jax.ShapeDtypeStruct((ne, 1), F32)),
        grid=(n // tr,),
        in_specs=[pl.BlockSpec((ne, tr), lambda i: (0, i)),
                  pl.BlockSpec((ne, 1), lambda i: (0, 0)),
                  pl.BlockSpec((tr, tr), lambda i: (0, 0))],
        out_specs=(pl.BlockSpec((TOP_K, tr), lambda i: (0, i)),
                   pl.BlockSpec((TOP_K, tr), lambda i: (0, i)),
                   pl.BlockSpec((TOP_K, tr), lambda i: (0, i)),
                   pl.BlockSpec((ne, 1), lambda i: (0, 0))),
        scratch_shapes=[pltpu.VMEM((ne, 1), F32)],
        compiler_params=pltpu.CompilerParams(
            dimension_semantics=("arbitrary",), vmem_limit_bytes=VMEM_LIMIT),
        name="route",
    )(lt, bias.reshape(ne, 1), su)


def _pos_kernel(idx_ref, rank_ref, st_ref, pos_ref):
    ne = st_ref.shape[0]
    tp = idx_ref.shape[1]
    eio = lax.broadcasted_iota(I32, (ne, tp), 0)
    st = st_ref[...]
    for k in range(TOP_K):
        base = jnp.sum(jnp.where(eio == idx_ref[k:k + 1, :], st, 0), axis=0, keepdims=True)
        pos_ref[k:k + 1, :] = base + rank_ref[k:k + 1, :]


def _pos_call(top_idx, rank, starts):
    kk, n = top_idx.shape
    ne = starts.shape[0]
    tp = POS_TILE
    return pl.pallas_call(
        _pos_kernel,
        out_shape=jax.ShapeDtypeStruct((kk, n), I32),
        grid=(n // tp,),
        in_specs=[pl.BlockSpec((kk, tp), lambda i: (0, i)),
                  pl.BlockSpec((kk, tp), lambda i: (0, i)),
                  pl.BlockSpec((ne, 1), lambda i: (0, 0))],
        out_specs=pl.BlockSpec((kk, tp), lambda i: (0, i)),
        compiler_params=pltpu.CompilerParams(
            dimension_semantics=("arbitrary",), vmem_limit_bytes=VMEM_LIMIT),
        name="pos",
    )(top_idx, rank, starts.reshape(ne, 1))


def _sc_mesh():
    return plsc.VectorSubcoreMesh(core_axis_name="c", subcore_axis_name="s",
                                  num_cores=SC_CORES, num_subcores=SC_SUBCORES)


def _sc_worker():
    return lax.axis_index("s") * SC_CORES + lax.axis_index("c")


def _dispatch_call(pos_win, hp):
    n = hp.shape[0]
    n_workers = SC_CORES * SC_SUBCORES
    wins_per_worker = n // SC_WINDOW // n_workers

    def body(hp_hbm, pos_hbm, xs_hbm, idx_v, rows_v, sem):
        first_win = _sc_worker() * wins_per_worker

        @pl.loop(0, wins_per_worker)
        def _(w):
            win = first_win + w
            pltpu.sync_copy(hp_hbm.at[pl.ds(win * SC_WINDOW, SC_WINDOW)], rows_v)
            pltpu.sync_copy(pos_hbm.at[win], idx_v)
            copies = [pltpu.async_copy(rows_v, xs_hbm.at[idx_v.at[k]], sem) for k in range(TOP_K)]
            for cp in copies:
                cp.wait()

    return pl.kernel(
        body,
        out_type=jax.ShapeDtypeStruct((n * TOP_K,) + hp.shape[1:], hp.dtype),
        mesh=_sc_mesh(),
        scratch_types=[pltpu.VMEM((TOP_K, SC_WINDOW), I32),
                       pltpu.VMEM((SC_WINDOW,) + hp.shape[1:], hp.dtype),
                       pltpu.SemaphoreType.DMA],
        name="dispatch",
    )(hp, pos_win)


def _collect_call(pos_win, y_sorted):
    n = pos_win.shape[0] * SC_WINDOW
    n_workers = SC_CORES * SC_SUBCORES
    wins_per_worker = n // SC_WINDOW // n_workers

    def body(ys_hbm, pos_hbm, out_hbm, idx_v, rows_a, rows_b, gather_sem, write_sem):
        first_win = _sc_worker() * wins_per_worker
        bufs = (rows_a, rows_b)

        @pl.loop(0, wins_per_worker)
        def _(w):
            win = first_win + w
            pltpu.sync_copy(pos_hbm.at[win], idx_v)

            def gather(k):
                return pltpu.async_copy(ys_hbm.at[idx_v.at[k]], bufs[k % 2], gather_sem)

            pending_gather = gather(0)
            pending_write = None
            for k in range(TOP_K):
                pending_gather.wait()
                if pending_write is not None:
                    for cp in pending_write:
                        cp.wait()
                if k + 1 < TOP_K:
                    pending_gather = gather(k + 1)
                pending_write = [
                    pltpu.async_copy(bufs[k % 2].at[:, j],
                                     out_hbm.at[j, k, pl.ds(win * SC_WINDOW, SC_WINDOW)], write_sem)
                    for j in range(PACK_ROWS)]
            for cp in pending_write:
                cp.wait()

    row_buf = pltpu.VMEM((SC_WINDOW,) + y_sorted.shape[1:], y_sorted.dtype)
    return pl.kernel(
        body,
        out_type=jax.ShapeDtypeStruct((PACK_ROWS, TOP_K, n, LANES), y_sorted.dtype),
        mesh=_sc_mesh(),
        scratch_types=[pltpu.VMEM((TOP_K, SC_WINDOW), I32), row_buf, row_buf,
                       pltpu.SemaphoreType.DMA, pltpu.SemaphoreType.DMA],
        name="collect",
    )(y_sorted, pos_win)


def _experts_kernel(pe_ref, pb_ref, pv_ref, st_ref, en_ref,
                    xs_hbm, wg_ref, wu_ref, wd_ref, y_hbm,
                    wgb_ref, wub_ref, wdb_ref, acc_ref, xbuf_ref, stage_ref,
                    in_sems, out_sem, state_ref):
    p = pl.program_id(0)
    n_pairs = pl.num_programs(0)
    e = pe_ref[p]
    blk = pb_ref[p]
    rb = acc_ref.shape[0]
    prev = jnp.maximum(p - 1, 0)
    nxt = jnp.minimum(p + 1, n_pairs - 1)
    live = pv_ref[p] == 1
    first = jnp.logical_or(p == 0, pb_ref[prev] != blk)
    block_ends = jnp.logical_or(pb_ref[nxt] != blk, pv_ref[nxt] == 0)
    last = jnp.logical_or(p == n_pairs - 1, block_ends)

    def in_copy(j, block, slot):
        return pltpu.make_async_copy(xs_hbm.at[pl.ds(block * rb, rb), j], xbuf_ref.at[slot, j],
                                     in_sems.at[slot])

    def out_copy(j, block):
        return pltpu.make_async_copy(stage_ref.at[j], y_hbm.at[pl.ds(block * rb, rb), j], out_sem)

    def drain():
        @pl.when(state_ref[1] == 1)
        def _():
            for j in range(PACK_ROWS):
                out_copy(j, 0).wait()
            state_ref[1] = 0

    def emit(val):
        drain()
        for j in range(PACK_ROWS):
            lo = val[:, 2 * j * LANES:(2 * j + 1) * LANES]
            hi = val[:, (2 * j + 1) * LANES:(2 * j + 2) * LANES]
            stage_ref[j] = pltpu.pack_elementwise([lo, hi], packed_dtype=BF16)
        for j in range(PACK_ROWS):
            out_copy(j, blk).start()
        state_ref[1] = 1

    @pl.when(p == 0)
    def _():
        state_ref[0] = 0
        state_ref[1] = 0
        for j in range(PACK_ROWS):
            in_copy(j, blk, 0).start()

    slot = jnp.where(jnp.logical_and(first, p > 0), 1 - state_ref[0], state_ref[0])
    state_ref[0] = slot

    @pl.when(jnp.logical_and(p < n_pairs - 1,
                             jnp.logical_and(pb_ref[nxt] != blk, pv_ref[nxt] == 1)))
    def _():
        for j in range(PACK_ROWS):
            in_copy(j, pb_ref[nxt], 1 - slot).start()

    @pl.when(first)
    def _():
        for j in range(PACK_ROWS):
            in_copy(j, blk, slot).wait()

    @pl.when(jnp.logical_or(p == 0, pe_ref[prev] != e))
    def _():
        wgb_ref[...] = wg_ref[0].astype(BF16)
        wub_ref[...] = wu_ref[0].astype(BF16)
        wdb_ref[...] = wd_ref[0].astype(BF16)

    @pl.when(live)
    def _():
        pieces = []
        for j in range(PACK_ROWS):
            w = xbuf_ref[slot, j]
            pieces.append(lax.bitcast_convert_type(w.astype(jnp.int16), BF16))
            pieces.append(lax.bitcast_convert_type(
                lax.shift_right_logical(w, 16).astype(jnp.int16), BF16))
        xb = jnp.concatenate(pieces, axis=1)
        gate = jnp.dot(xb, wgb_ref[...], preferred_element_type=F32)
        up = jnp.dot(xb, wub_ref[...], preferred_element_type=F32)
        rows = blk * rb + lax.broadcasted_iota(I32, (rb, 1), 0)
        mine = jnp.logical_and(rows >= st_ref[e], rows < en_ref[e])
        hid = jnp.where(mine, _silu(gate) * up, 0.0).astype(BF16)
        yb = jnp.dot(hid, wdb_ref[...], preferred_element_type=F32)
        not_first = jnp.logical_not(first)
        not_last = jnp.logical_not(last)

        @pl.when(jnp.logical_and(first, last))
        def _():
            emit(yb)

        @pl.when(jnp.logical_and(first, not_last))
        def _():
            acc_ref[...] = yb

        @pl.when(jnp.logical_and(not_first, not_last))
        def _():
            acc_ref[...] += yb

        @pl.when(jnp.logical_and(not_first, last))
        def _():
            emit(acc_ref[...] + yb)

    @pl.when(p == n_pairs - 1)
    def _():
        drain()


def _experts_call(pair_e, pair_blk, pair_ok, starts, ends, xs, w_gate, w_up, w_down):
    ne, d, de = w_gate.shape
    rb = ROW_BLOCK
    n_pairs = pair_e.shape[0]
    grid_spec = pltpu.PrefetchScalarGridSpec(
        num_scalar_prefetch=5,
        grid=(n_pairs,),
        in_specs=[pl.BlockSpec(memory_space=pl.ANY),
                  pl.BlockSpec((1, d, de), lambda p, pe, pb, pv, st, en: (pe[p], 0, 0)),
                  pl.BlockSpec((1, d, de), lambda p, pe, pb, pv, st, en: (pe[p], 0, 0)),
                  pl.BlockSpec((1, de, d), lambda p, pe, pb, pv, st, en: (pe[p], 0, 0))],
        out_specs=pl.BlockSpec(memory_space=pl.ANY),
        scratch_shapes=[pltpu.VMEM((d, de), BF16), pltpu.VMEM((d, de), BF16),
                        pltpu.VMEM((de, d), BF16), pltpu.VMEM((rb, d), F32),
                        pltpu.VMEM((2, PACK_ROWS, rb, LANES), I32),
                        pltpu.VMEM((PACK_ROWS, rb, LANES), I32),
                        pltpu.SemaphoreType.DMA((2,)), pltpu.SemaphoreType.DMA,
                        pltpu.SMEM((2,), I32)],
    )
    return pl.pallas_call(
        _experts_kernel,
        out_shape=jax.ShapeDtypeStruct(xs.shape, xs.dtype),
        grid_spec=grid_spec,
        compiler_params=pltpu.CompilerParams(
            dimension_semantics=("arbitrary",), vmem_limit_bytes=VMEM_LIMIT),
        name="experts",
    )(pair_e, pair_blk, pair_ok, starts, ends, xs, w_gate, w_up, w_down)


def _combine_kernel(y0_ref, y1_ref, y2_ref, y3_ref, x1_ref, w_ref, g2_ref, fg_ref, o_ref):
    w = w_ref[...]
    parts = []
    for y_ref in (y0_ref, y1_ref, y2_ref, y3_ref):
        for half in range(2):
            acc = None
            for k in range(TOP_K):
                piece = pltpu.unpack_elementwise(y_ref[k], index=half, packed_dtype=BF16,
                                                 unpacked_dtype=F32) * w[:, k:k + 1]
                acc = piece if acc is None else acc + piece
            parts.append(acc)
    x2 = x1_ref[...] + g2_ref[0] * jnp.concatenate(parts, axis=1)
    ms = jnp.mean(x2 * x2, axis=-1, keepdims=True)
    o_ref[...] = x2 * lax.rsqrt(ms + EPS) * fg_ref[...]


def _combine_call(y_tok, x1s, w_tok, g2, fgain, seq):
    n, d = x1s.shape
    tm = MOVE_TILE
    per_b = seq // tm

    def piece_spec(j):
        return pl.BlockSpec((None, TOP_K, tm, LANES), lambda i: (j, 0, i, 0))

    return pl.pallas_call(
        _combine_kernel,
        out_shape=jax.ShapeDtypeStruct((n, d), F32),
        grid=(n // tm,),
        in_specs=[piece_spec(j) for j in range(PACK_ROWS)] + [
                  pl.BlockSpec((tm, d), lambda i: (i, 0)),
                  pl.BlockSpec((tm, TOP_K), lambda i: (i, 0)),
                  pl.BlockSpec((1, 1, d), lambda i: (i // per_b, 0, 0)),
                  pl.BlockSpec((1, d), lambda i: (0, 0))],
        out_specs=pl.BlockSpec((tm, d), lambda i: (i, 0)),
        compiler_params=pltpu.CompilerParams(
            dimension_semantics=("arbitrary",), vmem_limit_bytes=VMEM_LIMIT),
        name="combine",
    )(y_tok, y_tok, y_tok, y_tok, x1s, w_tok, g2, fgain)


def _pair_tables(counts, n_rows):
    ne = counts.shape[0]
    sizes = counts.astype(I32)
    ends = jnp.cumsum(sizes)
    starts = ends - sizes
    first_blk = starts // ROW_BLOCK
    last_blk = (ends - 1) // ROW_BLOCK
    n_pairs = jnp.where(sizes > 0, last_blk - first_blk + 1, 0)
    pair_end = jnp.cumsum(n_pairs)
    pair_start = pair_end - n_pairs
    max_pairs = n_rows // ROW_BLOCK + ne
    p = jnp.arange(max_pairs, dtype=I32)
    ok = p < pair_end[-1]
    pc = jnp.minimum(p, pair_end[-1] - 1)
    pair_e = jnp.minimum(jnp.searchsorted(pair_end, pc, side='right'), ne - 1).astype(I32)
    pair_blk = (first_blk[pair_e] + pc - pair_start[pair_e]).astype(I32)
    return pair_e, pair_blk, ok.astype(I32), starts.astype(I32), ends.astype(I32)


def kernel(x, c, w_ada, b_ada, w_in, lb_logits, hgrn_norm, lam_re, lam_im, log_dt, b_re, b_im,
           c_re, c_im, d_skip, w_glu, b_glu, w_out, w_router, router_bias, w_gate, w_up, w_down,
           ws_gate, ws_up, ws_down, final_gain):
    bsz, seq, d = x.shape
    n = bsz * seq
    fdim = HGRN_HEADS * HGRN_KDIM
    n_chunks = seq // CHUNK
    lb = jnp.cumsum(jax.nn.softmax(lb_logits.astype(F32), axis=0), axis=0)[0].reshape(1, fdim)

    mod = _mod_call(c, w_ada[0], b_ada[0]).reshape(bsz, 6, d)
    nb = w_in.shape[2] - 4 * fdim
    perm = jnp.concatenate([jnp.arange(0, nb, 2), jnp.arange(1, nb, 2)])
    w_in_b = w_in[0].astype(BF16)
    ut, out_a = _mix_front_call(x, mod, w_in_b[:, :4 * fdim], w_in_b[:, 4 * fdim:][:, perm].T, lb,
                                hgrn_norm[0].reshape(1, fdim))

    m_tab, p_tab, q_tab, a1, a2 = _s5_tables(lam_re[0], lam_im[0], log_dt[0], b_re[0], b_im[0],
                                             c_re[0], c_im[0])
    yt = _s5_call(ut, m_tab, p_tab, q_tab, a1, a2, n_chunks, bsz)

    wr_t = w_router[0].T
    wr_hi = wr_t.astype(BF16)
    wr_lo = (wr_t - wr_hi.astype(F32)).astype(BF16)
    w_out_b = w_out[0].astype(BF16)
    w_out_p = jnp.concatenate([w_out_b[:fdim], w_out_b[fdim:][perm]], axis=0)
    x1s, hp, logits_t = _mix_back_call(
        x.reshape(n, d), out_a.reshape(n, fdim), yt, ut, mod,
        d_skip[0][perm].reshape(nb, 1), w_glu[0][perm][:, perm].T.astype(BF16),
        b_glu[0][perm].reshape(nb, 1), w_out_p, wr_hi, wr_lo, ws_gate[0].astype(BF16),
        ws_up[0].astype(BF16), ws_down[0].astype(BF16), seq)

    top_idx, top_w, rank, counts = _route_call(logits_t, router_bias[0])
    pair_e, pair_blk, pair_ok, starts, ends = _pair_tables(counts[:, 0], n * TOP_K)
    pos = _pos_call(top_idx, rank, starts)

    pos_win = pos.reshape(TOP_K, n // SC_WINDOW, SC_WINDOW).transpose(1, 0, 2)
    xs = _dispatch_call(pos_win, hp.reshape(n, PACK_ROWS, LANES))
    y_sorted = _experts_call(pair_e, pair_blk, pair_ok, starts, ends, xs, w_gate[0], w_up[0], w_down[0])
    y_tok = _collect_call(pos_win, y_sorted)
    out = _combine_call(y_tok, x1s, top_w.T, mod[:, 5:6, :], final_gain.reshape(1, d), seq)
    return out.reshape(bsz, seq, d)
```

```python
import functools

import jax
import jax.numpy as jnp
from jax import lax
from jax.experimental import pallas as pl
from jax.experimental.pallas import tpu as pltpu
from jax.experimental.pallas import tpu_sc as plsc

F32 = jnp.float32
BF16 = jnp.bfloat16
I32 = jnp.int32

EPS = 1e-6
CHUNK = 64
HGRN_HEADS = 4
HGRN_KDIM = 128
S5_GROUP = 16
S5_STATE = 64
N_EXPERT_GROUPS = 8
TOPK_GROUPS = 4
TOP_K = 8
ROUTE_SCALE = 2.5
LANES = 128
PACK_ROWS = 4
SC_CORES = 2
SC_SUBCORES = 16
SC_WINDOW = 64

SEQ_TILE = 512
TOK_TILE = 512
ROUTE_TILE = 256
POS_TILE = 1024
MOVE_TILE = 256
ROW_BLOCK = 2048
SUB_BLOCK = 512
VMEM_LIMIT = 56 * 1024 * 1024

_NT = (((1,), (1,)), ((), ()))
_TN = (((0,), (0,)), ((), ()))


def _sigmoid(v):
    return 1.0 / (1.0 + jnp.exp(-v))


def _silu(v):
    return v * _sigmoid(v)


def _bdot(a, b):
    return jnp.dot(a.astype(BF16), b.astype(BF16), preferred_element_type=F32)


def _store_packed(ref, val, n_rows):
    for j in range(PACK_ROWS):
        lo = val[:, 2 * j * LANES:(2 * j + 1) * LANES]
        hi = val[:, (2 * j + 1) * LANES:(2 * j + 2) * LANES]
        ref[pl.ds(j, n_rows, stride=PACK_ROWS), :] = pltpu.pack_elementwise([lo, hi], packed_dtype=BF16)


def _mod_kernel(c_ref, w_ref, b_ref, o_ref):
    o_ref[...] = _bdot(_silu(c_ref[...]), w_ref[...]) + b_ref[...]


def _mod_call(c, w_ada, b_ada):
    bsz, d = c.shape
    n_out = w_ada.shape[1]
    return pl.pallas_call(
        _mod_kernel,
        out_shape=jax.ShapeDtypeStruct((bsz, n_out), F32),
        grid=(n_out // d,),
        in_specs=[pl.BlockSpec((bsz, d), lambda j: (0, 0)),
                  pl.BlockSpec((d, d), lambda j: (0, j)),
                  pl.BlockSpec((1, d), lambda j: (0, j))],
        out_specs=pl.BlockSpec((bsz, d), lambda j: (0, j)),
        compiler_params=pltpu.CompilerParams(vmem_limit_bytes=VMEM_LIMIT),
        name="mod",
    )(c, w_ada, b_ada.reshape(1, n_out))


def _split_chunk_pairs(tile_even, tile_odd):
    low = lax.broadcasted_iota(I32, tile_even.shape, 1) < CHUNK
    first = jnp.where(low, tile_even, pltpu.roll(tile_odd, CHUNK, 1))
    second = jnp.where(low, pltpu.roll(tile_even, CHUNK, 1), tile_odd)
    return first, second


def _mix_front_kernel(x_ref, mod_ref, win_ref, wut_ref, lb_ref, gn_ref, ltri_ref,
                      ut_ref, oa_ref, proj_ref, st_ref, flat_ref):
    fdim = HGRN_HEADS * HGRN_KDIM
    ts = x_ref.shape[1]
    pairs = wut_ref.shape[0] // 2
    rows = ts // CHUNK

    @pl.when(pl.program_id(1) == 0)
    def _():
        st_ref[...] = jnp.zeros_like(st_ref)

    x = x_ref[0]
    ms = jnp.mean(x * x, axis=-1, keepdims=True)
    h = x * lax.rsqrt(ms + EPS) * (1.0 + mod_ref[0, 1:2, :]) + mod_ref[0, 0:1, :]
    hb = h.astype(BF16)
    proj_ref[...] = jnp.dot(hb, win_ref[...], preferred_element_type=F32)
    u_t = lax.dot_general(wut_ref[...], hb, _NT, preferred_element_type=F32)
    for m in range(ts // LANES):
        first, second = _split_chunk_pairs(u_t[:pairs, m * LANES:(m + 1) * LANES],
                                           u_t[pairs:, m * LANES:(m + 1) * LANES])
        flat_ref[2 * m * pairs:(2 * m + 1) * pairs, :] = first
        flat_ref[(2 * m + 1) * pairs:(2 * m + 2) * pairs, :] = second
    per_group = S5_GROUP // 2
    for q in range(pairs):
        ut_ref[q // per_group, :, (q % per_group) * LANES:(q % per_group + 1) * LANES] = (
            flat_ref[pl.ds(q, rows, stride=pairs), :])

    lb = lb_ref[...]
    gn = gn_ref[...]
    ltri = ltri_ref[...]
    row = lax.broadcasted_iota(I32, (CHUNK, CHUNK), 0)
    col = lax.broadcasted_iota(I32, (CHUNK, CHUNK), 1)
    causal = row >= col

    def chunk_step(ci, carry):
        r0 = pl.multiple_of(ci * CHUNK, CHUNK)
        q = proj_ref[pl.ds(r0, CHUNK), 0:fdim]
        fl = proj_ref[pl.ds(r0, CHUNK), fdim:2 * fdim]
        iv = proj_ref[pl.ds(r0, CHUNK), 2 * fdim:3 * fdim]
        og = proj_ref[pl.ds(r0, CHUNK), 3 * fdim:4 * fdim]
        f = lb + (1.0 - lb) * _sigmoid(fl)
        b = jnp.dot(ltri, jnp.log(f), precision=lax.Precision.HIGHEST,
                    preferred_element_type=F32)
        b_ref = b[CHUNK // 2 - 1:CHUNK // 2, :]
        b_last = b[CHUNK - 1:CHUNK, :]
        qs = _silu(q)
        kk = 1.0 - f
        qe = (qs * jnp.exp(b - b_ref)).astype(BF16)
        ke = (kk * jnp.exp(b_ref - b)).astype(BF16)
        qb = (qs * jnp.exp(b)).astype(BF16)
        k2 = (kk * jnp.exp(b_last - b)).astype(BF16)
        dec = jnp.exp(b_last)
        ivb = iv.astype(BF16)
        outs = []
        for hh in range(HGRN_HEADS):
            sl = slice(hh * HGRN_KDIM, (hh + 1) * HGRN_KDIM)
            att = lax.dot_general(qe[:, sl], ke[:, sl], _NT, preferred_element_type=F32)
            att = jnp.where(causal, att, 0.0)
            st = st_ref[hh]
            o = jnp.dot(att.astype(BF16), ivb[:, sl], preferred_element_type=F32)
            o = o + lax.dot_general(qb[:, sl], st.astype(BF16), _NT, preferred_element_type=F32)
            st_ref[hh] = st * dec[:, sl] + lax.dot_general(
                ivb[:, sl], k2[:, sl], _TN, preferred_element_type=F32)
            outs.append(o * lax.rsqrt(jnp.mean(o * o, axis=-1, keepdims=True) + EPS))
        o = jnp.concatenate(outs, axis=1) * gn * _silu(og)
        oa_ref[0, pl.ds(r0, CHUNK), :] = o.astype(BF16)
        return carry

    lax.fori_loop(0, ts // CHUNK, chunk_step, 0, unroll=True)


def _mix_front_call(x, mod, w_main, w_ut, lb, gn):
    bsz, seq, d = x.shape
    fdim = HGRN_HEADS * HGRN_KDIM
    ncols = w_main.shape[1]
    nb = w_ut.shape[0]
    groups = nb // S5_GROUP
    ltri = jnp.tril(jnp.ones((CHUNK, CHUNK), F32))
    ts = SEQ_TILE
    tiles = seq // ts
    rows = ts // CHUNK
    return pl.pallas_call(
        _mix_front_kernel,
        out_shape=(jax.ShapeDtypeStruct((groups, bsz * seq // CHUNK, S5_GROUP * CHUNK), F32),
                   jax.ShapeDtypeStruct((bsz, seq, fdim), BF16)),
        grid=(bsz, tiles),
        in_specs=[pl.BlockSpec((1, ts, d), lambda b, j: (b, j, 0)),
                  pl.BlockSpec((1, 6, d), lambda b, j: (b, 0, 0)),
                  pl.BlockSpec((d, ncols), lambda b, j: (0, 0)),
                  pl.BlockSpec((nb, d), lambda b, j: (0, 0)),
                  pl.BlockSpec((1, fdim), lambda b, j: (0, 0)),
                  pl.BlockSpec((1, fdim), lambda b, j: (0, 0)),
                  pl.BlockSpec((CHUNK, CHUNK), lambda b, j: (0, 0))],
        out_specs=(pl.BlockSpec((groups, rows, S5_GROUP * CHUNK), lambda b, j: (0, b * tiles + j, 0)),
                   pl.BlockSpec((1, ts, fdim), lambda b, j: (b, j, 0))),
        scratch_shapes=[pltpu.VMEM((ts, ncols), F32),
                        pltpu.VMEM((HGRN_HEADS, fdim // HGRN_HEADS, HGRN_KDIM), F32),
                        pltpu.VMEM((rows * nb // 2, LANES), F32)],
        compiler_params=pltpu.CompilerParams(
            dimension_semantics=("arbitrary", "arbitrary"), vmem_limit_bytes=VMEM_LIMIT),
        name="mix_front",
    )(x, mod, w_main, w_ut, lb, gn, ltri)


def _s5_tables(lam_re, lam_im, log_dt, b_re, b_im, c_re, c_im):
    t = CHUNK
    hp = lax.Precision.HIGHEST
    lam = lax.complex(jnp.minimum(lam_re, -1e-4), lam_im)
    lam_dt = lam * jnp.exp(log_dt)[:, None]
    lam_bar = jnp.exp(lam_dt)
    b_bar = ((lam_bar - 1.0) / lam)[..., None] * lax.complex(b_re, b_im)
    c_mat = lax.complex(c_re, c_im)
    taus = jnp.arange(t + 1, dtype=F32)
    lam_pow = jnp.exp(lam_dt[:, None, :] * taus[None, :, None])
    g, p = lam.shape
    c = b_re.shape[-1]
    cl = c_mat[:, None, :, :] * lam_pow[:, :t, None, :]
    kr = (jnp.einsum('gtcp,gpi->gtci', cl.real, b_bar.real, precision=hp)
          - jnp.einsum('gtcp,gpi->gtci', cl.imag, b_bar.imag, precision=hp))
    z = jnp.pad(kr.transpose(0, 3, 2, 1).astype(BF16), ((0, 0), (0, 0), (0, 0), (t - 1, 0)))
    kg = jnp.stack([z[..., t - 1 - s:2 * t - 1 - s] for s in range(t)], axis=2)
    m = kg.reshape(g, c * t, c * t)
    pc = lam_pow[:, t - 1::-1][:, :t, :, None] * b_bar[:, None, :, :]
    pc = pc.transpose(0, 3, 1, 2).reshape(g, c * t, p)
    p_tab = jnp.concatenate([pc.real, pc.imag], axis=-1)
    ql = c_mat[:, None, :, :] * lam_pow[:, 1:t + 1, None, :]
    ql = ql.transpose(0, 3, 2, 1).reshape(g, p, c * t)
    q_tab = jnp.concatenate([ql.real, -ql.imag], axis=1)
    lam_t = lam_pow[:, t]
    a1 = jnp.concatenate([lam_t.real, lam_t.real], axis=-1)[:, None, :]
    a2 = jnp.concatenate([-lam_t.imag, lam_t.imag], axis=-1)[:, None, :]
    return m.astype(BF16), p_tab.astype(BF16), q_tab.astype(BF16), a1, a2


def _s5_kernel(u_ref, m_ref, p_ref, q_ref, a1_ref, a2_ref, y_ref, v_ref, xs_ref, *, n_chunks, n_batch):
    u = u_ref[0].astype(BF16)
    v_ref[...] = jnp.dot(u, p_ref[0], preferred_element_type=F32)
    a1 = a1_ref[0]
    a2 = a2_ref[0]
    half = xs_ref.shape[1] // 2

    def step(n, state):
        xs_ref[pl.ds(n, n_batch, stride=n_chunks), :] = state
        return (a1 * state + a2 * pltpu.roll(state, half, 1)
                + v_ref[pl.ds(n, n_batch, stride=n_chunks), :])

    lax.fori_loop(0, n_chunks, step, jnp.zeros((n_batch, xs_ref.shape[1]), F32))
    y = jnp.dot(u, m_ref[0], preferred_element_type=F32)
    y_ref[0] = y + jnp.dot(xs_ref[...].astype(BF16), q_ref[0], preferred_element_type=F32)


def _s5_call(ut, m, p_tab, q_tab, a1, a2, n_chunks, n_batch):
    g, rows, width = ut.shape
    p2 = p_tab.shape[-1]
    return pl.pallas_call(
        functools.partial(_s5_kernel, n_chunks=n_chunks, n_batch=n_batch),
        out_shape=jax.ShapeDtypeStruct((g, rows, width), F32),
        grid=(g,),
        in_specs=[pl.BlockSpec((1, rows, width), lambda i: (i, 0, 0)),
                  pl.BlockSpec((1, width, width), lambda i: (i, 0, 0)),
                  pl.BlockSpec((1, width, p2), lambda i: (i, 0, 0)),
                  pl.BlockSpec((1, p2, width), lambda i: (i, 0, 0)),
                  pl.BlockSpec((1, 1, p2), lambda i: (i, 0, 0)),
                  pl.BlockSpec((1, 1, p2), lambda i: (i, 0, 0))],
        out_specs=pl.BlockSpec((1, rows, width), lambda i: (i, 0, 0)),
        scratch_shapes=[pltpu.VMEM((rows, p2), F32), pltpu.VMEM((rows, p2), F32)],
        compiler_params=pltpu.CompilerParams(
            dimension_semantics=("arbitrary",), vmem_limit_bytes=VMEM_LIMIT),
        name="s5",
    )(ut, m, p_tab, q_tab, a1, a2)


def _token_major(flat_ref, src_ref):
    groups, rows, _ = src_ref.shape
    per_group = S5_GROUP // 2
    pairs = groups * per_group
    for q in range(pairs):
        flat_ref[pl.ds(q, rows, stride=pairs), :] = (
            src_ref[q // per_group, :, (q % per_group) * LANES:(q % per_group + 1) * LANES])
    tiles = []
    for m in range(rows // 2):
        even, odd = _split_chunk_pairs(flat_ref[2 * m * pairs:(2 * m + 1) * pairs, :],
                                       flat_ref[(2 * m + 1) * pairs:(2 * m + 2) * pairs, :])
        tiles.append(jnp.concatenate([even, odd], axis=0))
    return jnp.concatenate(tiles, axis=1)


def _mix_back_kernel(x_ref, oa_ref, yt_ref, ut_ref, mod_ref, dskip_ref, wglu_ref, bglu_ref,
                     wout_ref, wrh_ref, wrl_ref, wsg_ref, wsu_ref, wsd_ref,
                     x1_ref, hp_ref, lt_ref, flat_ref):
    na = oa_ref.shape[1]
    y_t = _token_major(flat_ref, yt_ref)
    u_t = _token_major(flat_ref, ut_ref)
    z_t = jax.nn.gelu(y_t + dskip_ref[...] * u_t)
    gate_t = _sigmoid(jnp.dot(wglu_ref[...], z_t.astype(BF16), preferred_element_type=F32)
                      + bglu_ref[...])
    ob_t = (z_t * gate_t).astype(BF16)
    mixed = (jnp.dot(oa_ref[...], wout_ref[0:na, :], preferred_element_type=F32)
             + lax.dot_general(ob_t, wout_ref[na:, :], _TN, preferred_element_type=F32))
    x1 = x_ref[...] + mod_ref[0, 2:3, :] * mixed
    ms = jnp.mean(x1 * x1, axis=-1, keepdims=True)
    h2 = x1 * lax.rsqrt(ms + EPS) * (1.0 + mod_ref[0, 4:5, :]) + mod_ref[0, 3:4, :]
    _store_packed(hp_ref, h2, h2.shape[0])
    h_hi = h2.astype(BF16)
    h_lo = (h2 - h_hi.astype(F32)).astype(BF16)
    lt = lax.dot_general(wrh_ref[...], h_hi, _NT, preferred_element_type=F32)
    lt = lt + lax.dot_general(wrl_ref[...], h_hi, _NT, preferred_element_type=F32)
    lt = lt + lax.dot_general(wrh_ref[...], h_lo, _NT, preferred_element_type=F32)
    lt_ref[...] = lt
    hid = _silu(jnp.dot(h_hi, wsg_ref[...], preferred_element_type=F32)) * jnp.dot(
        h_hi, wsu_ref[...], preferred_element_type=F32)
    shared = jnp.dot(hid.astype(BF16), wsd_ref[...], preferred_element_type=F32)
    x1_ref[...] = x1 + mod_ref[0, 5:6, :] * shared


def _mix_back_call(x2d, oa, yt, ut, mod, dskip, wglu, bglu, wout, wrh, wrl, wsg, wsu, wsd, seq):
    n, d = x2d.shape
    nb = oa.shape[1]
    ne = wrh.shape[0]
    dsh = wsg.shape[1]
    tm = TOK_TILE
    per_b = seq // tm
    groups, _, width = yt.shape
    rows = tm // CHUNK
    const = lambda i: (0, 0)
    flat_block = pl.BlockSpec((groups, rows, width), lambda i: (0, i, 0))
    return pl.pallas_call(
        _mix_back_kernel,
        out_shape=(jax.ShapeDtypeStruct((n, d), F32),
                   jax.ShapeDtypeStruct((n * PACK_ROWS, LANES), I32),
                   jax.ShapeDtypeStruct((ne, n), F32)),
        grid=(n // tm,),
        in_specs=[pl.BlockSpec((tm, d), lambda i: (i, 0)),
                  pl.BlockSpec((tm, nb), lambda i: (i, 0)),
                  flat_block,
                  flat_block,
                  pl.BlockSpec((1, 6, d), lambda i: (i // per_b, 0, 0)),
                  pl.BlockSpec((nb, 1), const),
                  pl.BlockSpec((nb, nb), const),
                  pl.BlockSpec((nb, 1), const),
                  pl.BlockSpec((d, d), const),
                  pl.BlockSpec((ne, d), const),
                  pl.BlockSpec((ne, d), const),
                  pl.BlockSpec((d, dsh), const),
                  pl.BlockSpec((d, dsh), const),
                  pl.BlockSpec((dsh, d), const)],
        out_specs=(pl.BlockSpec((tm, d), lambda i: (i, 0)),
                   pl.BlockSpec((tm * PACK_ROWS, LANES), lambda i: (i, 0)),
                   pl.BlockSpec((ne, tm), lambda i: (0, i))),
        scratch_shapes=[pltpu.VMEM((rows * nb // 2, LANES), F32)],
        compiler_params=pltpu.CompilerParams(
            dimension_semantics=("arbitrary",), vmem_limit_bytes=VMEM_LIMIT),
        name="mix_back",
    )(x2d, oa, yt, ut, mod, dskip, wglu, bglu, wout, wrh, wrl, wsg, wsu, wsd)


def _route_kernel(lt_ref, bias_ref, su_ref, idx_ref, w_ref, rank_ref, cnt_ref, run_ref):
    ne, tr = lt_ref.shape
    per_group = ne // N_EXPERT_GROUPS
    neg = -jnp.inf

    @pl.when(pl.program_id(0) == 0)
    def _():
        run_ref[...] = jnp.zeros_like(run_ref)

    s = _sigmoid(lt_ref[...])
    sel = s + bias_ref[...]
    gio = lax.broadcasted_iota(I32, (per_group, tr), 0)
    gscore = []
    for g in range(N_EXPERT_GROUPS):
        v = sel[g * per_group:(g + 1) * per_group, :]
        m1 = jnp.max(v, axis=0, keepdims=True)
        i1 = jnp.min(jnp.where(v == m1, gio, per_group), axis=0, keepdims=True)
        m2 = jnp.max(jnp.where(gio == i1, neg, v), axis=0, keepdims=True)
        gscore.append(m1 + m2)
    masked = []
    for g in range(N_EXPERT_GROUPS):
        ahead = jnp.zeros((1, tr), I32)
        for o in range(N_EXPERT_GROUPS):
            if o == g:
                continue
            wins = (gscore[o] >= gscore[g]) if o < g else (gscore[o] > gscore[g])
            ahead = ahead + wins.astype(I32)
        keep = ahead < TOPK_GROUPS
        masked.append(jnp.where(keep, sel[g * per_group:(g + 1) * per_group, :], neg))
    selm = jnp.concatenate(masked, axis=0)
    eio = lax.broadcasted_iota(I32, (ne, tr), 0)
    hits = jnp.zeros((ne, tr), F32)
    idxs, ws = [], []
    for k in range(TOP_K):
        m = jnp.max(selm, axis=0, keepdims=True)
        ik = jnp.min(jnp.where(selm == m, eio, ne), axis=0, keepdims=True)
        onehot = eio == ik
        ws.append(jnp.sum(jnp.where(onehot, s, 0.0), axis=0, keepdims=True))
        hits = jnp.where(onehot, 1.0, hits)
        selm = jnp.where(onehot, neg, selm)
        idxs.append(ik)
    wsum = ws[0]
    for k in range(1, TOP_K):
        wsum = wsum + ws[k]
    scale = ROUTE_SCALE / wsum
    ranks = jnp.dot(hits.astype(BF16), su_ref[...], preferred_element_type=F32) + run_ref[...]
    for k in range(TOP_K):
        idx_ref[k:k + 1, :] = idxs[k]
        w_ref[k:k + 1, :] = ws[k] * scale
        rk = jnp.sum(jnp.where(eio == idxs[k], ranks, 0.0), axis=0, keepdims=True)
        rank_ref[k:k + 1, :] = rk.astype(I32)
    run_ref[...] = run_ref[...] + jnp.sum(hits, axis=1, keepdims=True)
    cnt_ref[...] = run_ref[...]


def _route_call(lt, bias):
    ne, n = lt.shape
    tr = ROUTE_TILE
    su = jnp.triu(jnp.ones((tr, tr), F32), k=1).astype(BF16)
    return pl.pallas_call(
        _route_kernel,
        out_shape=(jax.ShapeDtypeStruct((TOP_K, n), I32),
                   jax.ShapeDtypeStruct((TOP_K, n), F32),
                   jax.ShapeDtypeStruct((TOP_K, n), I32),
                   jax.ShapeDtypeStruct((ne, 1), F32)),
        grid=(n // tr,),
        in_specs=[pl.BlockSpec((ne, tr), lambda i: (0, i)),
                  pl.BlockSpec((ne, 1), lambda i: (0, 0)),
                  pl.BlockSpec((tr, tr), lambda i: (0, 0))],
        out_specs=(pl.BlockSpec((TOP_K, tr), lambda i: (0, i)),
                   pl.BlockSpec((TOP_K, tr), lambda i: (0, i)),
                   pl.BlockSpec((TOP_K, tr), lambda i: (0, i)),
                   pl.BlockSpec((ne, 1), lambda i: (0, 0))),
        scratch_shapes=[pltpu.VMEM((ne, 1), F32)],
        compiler_params=pltpu.CompilerParams(
            dimension_semantics=("arbitrary",), vmem_limit_bytes=VMEM_LIMIT),
        name="route",
    )(lt, bias.reshape(ne, 1), su)


def _pos_kernel(idx_ref, rank_ref, st_ref, pos_ref):
    ne = st_ref.shape[0]
    tp = idx_ref.shape[1]
    eio = lax.broadcasted_iota(I32, (ne, tp), 0)
    st = st_ref[...]
    for k in range(TOP_K):
        base = jnp.sum(jnp.where(eio == idx_ref[k:k + 1, :], st, 0), axis=0, keepdims=True)
        pos_ref[k:k + 1, :] = base + rank_ref[k:k + 1, :]


def _pos_call(top_idx, rank, starts):
    kk, n = top_idx.shape
    ne = starts.shape[0]
    tp = POS_TILE
    return pl.pallas_call(
        _pos_kernel,
        out_shape=jax.ShapeDtypeStruct((kk, n), I32),
        grid=(n // tp,),
        in_specs=[pl.BlockSpec((kk, tp), lambda i: (0, i)),
                  pl.BlockSpec((kk, tp), lambda i: (0, i)),
                  pl.BlockSpec((ne, 1), lambda i: (0, 0))],
        out_specs=pl.BlockSpec((kk, tp), lambda i: (0, i)),
        compiler_params=pltpu.CompilerParams(
            dimension_semantics=("arbitrary",), vmem_limit_bytes=VMEM_LIMIT),
        name="pos",
    )(top_idx, rank, starts.reshape(ne, 1))


def _sc_mesh():
    return plsc.VectorSubcoreMesh(core_axis_name="c", subcore_axis_name="s",
                                  num_cores=SC_CORES, num_subcores=SC_SUBCORES)


def _sc_worker():
    return lax.axis_index("s") * SC_CORES + lax.axis_index("c")


def _dispatch_call(pos_win, hp):
    n = hp.shape[0]
    n_workers = SC_CORES * SC_SUBCORES
    wins_per_worker = n // SC_WINDOW // n_workers

    def body(hp_hbm, pos_hbm, xs_hbm, idx_v, rows_v, sem):
        first_win = _sc_worker() * wins_per_worker

        @pl.loop(0, wins_per_worker)
        def _(w):
            win = first_win + w
            pltpu.sync_copy(hp_hbm.at[pl.ds(win * SC_WINDOW, SC_WINDOW)], rows_v)
            pltpu.sync_copy(pos_hbm.at[win], idx_v)
            copies = [pltpu.async_copy(rows_v, xs_hbm.at[idx_v.at[k]], sem) for k in range(TOP_K)]
            for cp in copies:
                cp.wait()

    return pl.kernel(
        body,
        out_type=jax.ShapeDtypeStruct((n * TOP_K,) + hp.shape[1:], hp.dtype),
        mesh=_sc_mesh(),
        scratch_types=[pltpu.VMEM((TOP_K, SC_WINDOW), I32),
                       pltpu.VMEM((SC_WINDOW,) + hp.shape[1:], hp.dtype),
                       pltpu.SemaphoreType.DMA],
        name="dispatch",
    )(hp, pos_win)


def _collect_call(pos_win, y_sorted):
    n = pos_win.shape[0] * SC_WINDOW
    n_workers = SC_CORES * SC_SUBCORES
    wins_per_worker = n // SC_WINDOW // n_workers

    def body(ys_hbm, pos_hbm, out_hbm, idx_v, rows_a, rows_b, gather_sem, write_sem):
        first_win = _sc_worker() * wins_per_worker
        bufs = (rows_a, rows_b)

        @pl.loop(0, wins_per_worker)
        def _(w):
            win = first_win + w
            pltpu.sync_copy(pos_hbm.at[win], idx_v)

            def gather(k):
                return pltpu.async_copy(ys_hbm.at[idx_v.at[k]], bufs[k % 2], gather_sem)

            pending_gather = gather(0)
            pending_write = None
            for k in range(TOP_K):
                pending_gather.wait()
                if pending_write is not None:
                    for cp in pending_write:
                        cp.wait()
                if k + 1 < TOP_K:
                    pending_gather = gather(k + 1)
                pending_write = [
                    pltpu.async_copy(bufs[k % 2].at[:, j],
                                     out_hbm.at[j, k, pl.ds(win * SC_WINDOW, SC_WINDOW)], write_sem)
                    for j in range(PACK_ROWS)]
            for cp in pending_write:
                cp.wait()

    row_buf = pltpu.VMEM((SC_WINDOW,) + y_sorted.shape[1:], y_sorted.dtype)
    return pl.kernel(
        body,
        out_type=jax.ShapeDtypeStruct((PACK_ROWS, TOP_K, n, LANES), y_sorted.dtype),
        mesh=_sc_mesh(),
        scratch_types=[pltpu.VMEM((TOP_K, SC_WINDOW), I32), row_buf, row_buf,
                       pltpu.SemaphoreType.DMA, pltpu.SemaphoreType.DMA],
        name="collect",
    )(y_sorted, pos_win)


def _experts_kernel(pe_ref, pb_ref, pv_ref, st_ref, en_ref,
                    xs_hbm, wg_ref, wu_ref, wd_ref, y_hbm,
                    wgb_ref, wub_ref, wdb_ref, acc_ref, xbuf_ref, stage_ref,
                    in_sems, out_sems, state_ref):
    p = pl.program_id(0)
    n_pairs = pl.num_programs(0)
    e = pe_ref[p]
    blk = pb_ref[p]
    rb = acc_ref.shape[0]
    prev = jnp.maximum(p - 1, 0)
    nxt = jnp.minimum(p + 1, n_pairs - 1)
    live = pv_ref[p] == 1
    first = jnp.logical_or(p == 0, pb_ref[prev] != blk)
    block_ends = jnp.logical_or(pb_ref[nxt] != blk, pv_ref[nxt] == 0)
    last = jnp.logical_or(p == n_pairs - 1, block_ends)

    def in_copy(j, block, slot):
        return pltpu.make_async_copy(xs_hbm.at[pl.ds(block * rb, rb), j], xbuf_ref.at[slot, j],
                                     in_sems.at[slot])

    def out_copy(j, block, slot):
        return pltpu.make_async_copy(stage_ref.at[slot, j], y_hbm.at[pl.ds(block * rb, rb), j],
                                     out_sems.at[slot])

    def drain(slot):
        @pl.when(state_ref[1 + slot] == 1)
        def _():
            for j in range(PACK_ROWS):
                out_copy(j, 0, slot).wait()
            state_ref[1 + slot] = 0

    @pl.when(p == 0)
    def _():
        state_ref[0] = 0
        state_ref[1] = 0
        state_ref[2] = 0
        for j in range(PACK_ROWS):
            in_copy(j, blk, 0).start()

    slot = jnp.where(jnp.logical_and(first, p > 0), 1 - state_ref[0], state_ref[0])
    state_ref[0] = slot

    @pl.when(jnp.logical_and(p < n_pairs - 1,
                             jnp.logical_and(pb_ref[nxt] != blk, pv_ref[nxt] == 1)))
    def _():
        for j in range(PACK_ROWS):
            in_copy(j, pb_ref[nxt], 1 - slot).start()

    @pl.when(first)
    def _():
        for j in range(PACK_ROWS):
            in_copy(j, blk, slot).wait()
        drain(slot)

    @pl.when(jnp.logical_or(p == 0, pe_ref[prev] != e))
    def _():
        wgb_ref[...] = wg_ref[0].astype(BF16)
        wub_ref[...] = wu_ref[0].astype(BF16)
        wdb_ref[...] = wd_ref[0].astype(BF16)

    lo_row = st_ref[e]
    hi_row = en_ref[e]

    def stage_rows(s, val):
        for j in range(PACK_ROWS):
            lo = val[:, 2 * j * LANES:(2 * j + 1) * LANES]
            hi = val[:, (2 * j + 1) * LANES:(2 * j + 2) * LANES]
            stage_ref[slot, j, s * SUB_BLOCK:(s + 1) * SUB_BLOCK, :] = pltpu.pack_elementwise(
                [lo, hi], packed_dtype=BF16)

    for s in range(rb // SUB_BLOCK):
        row0 = blk * rb + s * SUB_BLOCK
        sub = pl.ds(s * SUB_BLOCK, SUB_BLOCK)
        touched = jnp.logical_and(live, jnp.logical_and(row0 < hi_row, row0 + SUB_BLOCK > lo_row))

        @pl.when(touched)
        def _():
            pieces = []
            for j in range(PACK_ROWS):
                w = xbuf_ref[slot, j, sub, :]
                pieces.append(lax.bitcast_convert_type(w.astype(jnp.int16), BF16))
                pieces.append(lax.bitcast_convert_type(
                    lax.shift_right_logical(w, 16).astype(jnp.int16), BF16))
            xb = jnp.concatenate(pieces, axis=1)
            gate = jnp.dot(xb, wgb_ref[...], preferred_element_type=F32)
            up = jnp.dot(xb, wub_ref[...], preferred_element_type=F32)
            rows = row0 + lax.broadcasted_iota(I32, (SUB_BLOCK, 1), 0)
            mine = jnp.logical_and(rows >= lo_row, rows < hi_row)
            hid = jnp.where(mine, _silu(gate) * up, 0.0).astype(BF16)
            yb = jnp.dot(hid, wdb_ref[...], preferred_element_type=F32)
            opens = lo_row <= row0
            closes = hi_row >= row0 + SUB_BLOCK

            @pl.when(jnp.logical_and(opens, closes))
            def _():
                stage_rows(s, yb)

            @pl.when(jnp.logical_and(opens, jnp.logical_not(closes)))
            def _():
                acc_ref[sub, :] = yb

            @pl.when(jnp.logical_and(jnp.logical_not(opens), jnp.logical_not(closes)))
            def _():
                acc_ref[sub, :] += yb

            @pl.when(jnp.logical_and(jnp.logical_not(opens), closes))
            def _():
                stage_rows(s, acc_ref[sub, :] + yb)

    @pl.when(jnp.logical_and(live, last))
    def _():
        for j in range(PACK_ROWS):
            out_copy(j, blk, slot).start()
        state_ref[1 + slot] = 1

    @pl.when(p == n_pairs - 1)
    def _():
        drain(0)
        drain(1)


def _experts_call(pair_e, pair_blk, pair_ok, starts, ends, xs, w_gate, w_up, w_down):
    ne, d, de = w_gate.shape
    rb = ROW_BLOCK
    n_pairs = pair_e.shape[0]
    grid_spec = pltpu.PrefetchScalarGridSpec(
        num_scalar_prefetch=5,
        grid=(n_pairs,),
        in_specs=[pl.BlockSpec(memory_space=pl.ANY),
                  pl.BlockSpec((1, d, de), lambda p, pe, pb, pv, st, en: (pe[p], 0, 0)),
                  pl.BlockSpec((1, d, de), lambda p, pe, pb, pv, st, en: (pe[p], 0, 0)),
                  pl.BlockSpec((1, de, d), lambda p, pe, pb, pv, st, en: (pe[p], 0, 0))],
        out_specs=pl.BlockSpec(memory_space=pl.ANY),
        scratch_shapes=[pltpu.VMEM((d, de), BF16), pltpu.VMEM((d, de), BF16),
                        pltpu.VMEM((de, d), BF16), pltpu.VMEM((rb, d), F32),
                        pltpu.VMEM((2, PACK_ROWS, rb, LANES), I32),
                        pltpu.VMEM((2, PACK_ROWS, rb, LANES), I32),
                        pltpu.SemaphoreType.DMA((2,)), pltpu.SemaphoreType.DMA((2,)),
                        pltpu.SMEM((3,), I32)],
    )
    return pl.pallas_call(
        _experts_kernel,
        out_shape=jax.ShapeDtypeStruct(xs.shape, xs.dtype),
        grid_spec=grid_spec,
        compiler_params=pltpu.CompilerParams(
            dimension_semantics=("arbitrary",), vmem_limit_bytes=VMEM_LIMIT),
        name="experts",
    )(pair_e, pair_blk, pair_ok, starts, ends, xs, w_gate, w_up, w_down)


def _combine_kernel(y0_ref, y1_ref, y2_ref, y3_ref, x1_ref, w_ref, g2_ref, fg_ref, o_ref):
    w = w_ref[...]
    parts = []
    for y_ref in (y0_ref, y1_ref, y2_ref, y3_ref):
        for half in range(2):
            acc = None
            for k in range(TOP_K):
                piece = pltpu.unpack_elementwise(y_ref[k], index=half, packed_dtype=BF16,
                                                 unpacked_dtype=F32) * w[:, k:k + 1]
                acc = piece if acc is None else acc + piece
            parts.append(acc)
    x2 = x1_ref[...] + g2_ref[0] * jnp.concatenate(parts, axis=1)
    ms = jnp.mean(x2 * x2, axis=-1, keepdims=True)
    o_ref[...] = x2 * lax.rsqrt(ms + EPS) * fg_ref[...]


def _combine_call(y_tok, x1s, w_tok, g2, fgain, seq):
    n, d = x1s.shape
    tm = MOVE_TILE
    per_b = seq // tm

    def piece_spec(j):
        return pl.BlockSpec((None, TOP_K, tm, LANES), lambda i: (j, 0, i, 0))

    return pl.pallas_call(
        _combine_kernel,
        out_shape=jax.ShapeDtypeStruct((n, d), F32),
        grid=(n // tm,),
        in_specs=[piece_spec(j) for j in range(PACK_ROWS)] + [
                  pl.BlockSpec((tm, d), lambda i: (i, 0)),
                  pl.BlockSpec((tm, TOP_K), lambda i: (i, 0)),
                  pl.BlockSpec((1, 1, d), lambda i: (i // per_b, 0, 0)),
                  pl.BlockSpec((1, d), lambda i: (0, 0))],
        out_specs=pl.BlockSpec((tm, d), lambda i: (i, 0)),
        compiler_params=pltpu.CompilerParams(
            dimension_semantics=("arbitrary",), vmem_limit_bytes=VMEM_LIMIT),
        name="combine",
    )(y_tok, y_tok, y_tok, y_tok, x1s, w_tok, g2, fgain)


def _pair_tables(counts, n_rows):
    ne = counts.shape[0]
    sizes = counts.astype(I32)
    ends = jnp.cumsum(sizes)
    starts = ends - sizes
    first_blk = starts // ROW_BLOCK
    last_blk = (ends - 1) // ROW_BLOCK
    n_pairs = jnp.where(sizes > 0, last_blk - first_blk + 1, 0)
    pair_end = jnp.cumsum(n_pairs)
    pair_start = pair_end - n_pairs
    max_pairs = n_rows // ROW_BLOCK + ne
    p = jnp.arange(max_pairs, dtype=I32)
    ok = p < pair_end[-1]
    pc = jnp.minimum(p, pair_end[-1] - 1)
    pair_e = jnp.minimum(jnp.searchsorted(pair_end, pc, side='right'), ne - 1).astype(I32)
    pair_blk = (first_blk[pair_e] + pc - pair_start[pair_e]).astype(I32)
    return pair_e, pair_blk, ok.astype(I32), starts.astype(I32), ends.astype(I32)


def kernel(x, c, w_ada, b_ada, w_in, lb_logits, hgrn_norm, lam_re, lam_im, log_dt, b_re, b_im,
           c_re, c_im, d_skip, w_glu, b_glu, w_out, w_router, router_bias, w_gate, w_up, w_down,
           ws_gate, ws_up, ws_down, final_gain):
    bsz, seq, d = x.shape
    n = bsz * seq
    fdim = HGRN_HEADS * HGRN_KDIM
    n_chunks = seq // CHUNK
    lb = jnp.cumsum(jax.nn.softmax(lb_logits.astype(F32), axis=0), axis=0)[0].reshape(1, fdim)

    mod = _mod_call(c, w_ada[0], b_ada[0]).reshape(bsz, 6, d)
    nb = w_in.shape[2] - 4 * fdim
    perm = jnp.concatenate([jnp.arange(0, nb, 2), jnp.arange(1, nb, 2)])
    w_in_b = w_in[0].astype(BF16)
    ut, out_a = _mix_front_call(x, mod, w_in_b[:, :4 * fdim], w_in_b[:, 4 * fdim:][:, perm].T, lb,
                                hgrn_norm[0].reshape(1, fdim))

    m_tab, p_tab, q_tab, a1, a2 = _s5_tables(lam_re[0], lam_im[0], log_dt[0], b_re[0], b_im[0],
                                             c_re[0], c_im[0])
    yt = _s5_call(ut, m_tab, p_tab, q_tab, a1, a2, n_chunks, bsz)

    wr_t = w_router[0].T
    wr_hi = wr_t.astype(BF16)
    wr_lo = (wr_t - wr_hi.astype(F32)).astype(BF16)
    w_out_b = w_out[0].astype(BF16)
    w_out_p = jnp.concatenate([w_out_b[:fdim], w_out_b[fdim:][perm]], axis=0)
    x1s, hp, logits_t = _mix_back_call(
        x.reshape(n, d), out_a.reshape(n, fdim), yt, ut, mod,
        d_skip[0][perm].reshape(nb, 1), w_glu[0][perm][:, perm].T.astype(BF16),
        b_glu[0][perm].reshape(nb, 1), w_out_p, wr_hi, wr_lo, ws_gate[0].astype(BF16),
        ws_up[0].astype(BF16), ws_down[0].astype(BF16), seq)

    top_idx, top_w, rank, counts = _route_call(logits_t, router_bias[0])
    pair_e, pair_blk, pair_ok, starts, ends = _pair_tables(counts[:, 0], n * TOP_K)
    pos = _pos_call(top_idx, rank, starts)

    pos_win = pos.reshape(TOP_K, n // SC_WINDOW, SC_WINDOW).transpose(1, 0, 2)
    xs = _dispatch_call(pos_win, hp.reshape(n, PACK_ROWS, LANES))
    y_sorted = _experts_call(pair_e, pair_blk, pair_ok, starts, ends, xs, w_gate[0], w_up[0], w_down[0])
    y_tok = _collect_call(pos_win, y_sorted)
    out = _combine_call(y_tok, x1s, top_w.T, mod[:, 5:6, :], final_gain.reshape(1, d), seq)
    return out.reshape(bsz, seq, d)
```

```python
import functools

import jax
import jax.numpy as jnp
from jax import lax
from jax.experimental import pallas as pl
from jax.experimental.pallas import tpu as pltpu
from jax.experimental.pallas import tpu_sc as plsc

F32 = jnp.float32
BF16 = jnp.bfloat16
I32 = jnp.int32

EPS = 1e-6
CHUNK = 64
HGRN_HEADS = 4
HGRN_KDIM = 128
S5_GROUP = 16
S5_STATE = 64
N_EXPERT_GROUPS = 8
TOPK_GROUPS = 4
TOP_K = 8
ROUTE_SCALE = 2.5
LANES = 128
PACK_ROWS = 4
SC_CORES = 2
SC_SUBCORES = 16
SC_WINDOW = 64

SEQ_TILE = 512
TOK_TILE = 512
ROUTE_TILE = 256
POS_TILE = 1024
MOVE_TILE = 256
ROW_BLOCK = 2048
SUB_BLOCK = 512
TAIL_PARTS = 4
VMEM_LIMIT = 56 * 1024 * 1024

_NT = (((1,), (1,)), ((), ()))
_TN = (((0,), (0,)), ((), ()))


def _sigmoid(v):
    return 1.0 / (1.0 + jnp.exp(-v))


def _silu(v):
    return v * _sigmoid(v)


def _bdot(a, b):
    return jnp.dot(a.astype(BF16), b.astype(BF16), preferred_element_type=F32)


def _store_packed(ref, val, n_rows):
    for j in range(PACK_ROWS):
        lo = val[:, 2 * j * LANES:(2 * j + 1) * LANES]
        hi = val[:, (2 * j + 1) * LANES:(2 * j + 2) * LANES]
        ref[pl.ds(j, n_rows, stride=PACK_ROWS), :] = pltpu.pack_elementwise([lo, hi], packed_dtype=BF16)


def _mod_kernel(c_ref, w_ref, b_ref, o_ref):
    o_ref[...] = _bdot(_silu(c_ref[...]), w_ref[...]) + b_ref[...]


def _mod_call(c, w_ada, b_ada):
    bsz, d = c.shape
    n_out = w_ada.shape[1]
    return pl.pallas_call(
        _mod_kernel,
        out_shape=jax.ShapeDtypeStruct((bsz, n_out), F32),
        grid=(n_out // d,),
        in_specs=[pl.BlockSpec((bsz, d), lambda j: (0, 0)),
                  pl.BlockSpec((d, d), lambda j: (0, j)),
                  pl.BlockSpec((1, d), lambda j: (0, j))],
        out_specs=pl.BlockSpec((bsz, d), lambda j: (0, j)),
        compiler_params=pltpu.CompilerParams(vmem_limit_bytes=VMEM_LIMIT),
        name="mod",
    )(c, w_ada, b_ada.reshape(1, n_out))


def _split_chunk_pairs(tile_even, tile_odd):
    low = lax.broadcasted_iota(I32, tile_even.shape, 1) < CHUNK
    first = jnp.where(low, tile_even, pltpu.roll(tile_odd, CHUNK, 1))
    second = jnp.where(low, pltpu.roll(tile_even, CHUNK, 1), tile_odd)
    return first, second


def _mix_front_kernel(x_ref, mod_ref, win_ref, wut_ref, lb_ref, gn_ref, ltri_ref,
                      ut_ref, oa_ref, proj_ref, st_ref, flat_ref):
    fdim = HGRN_HEADS * HGRN_KDIM
    ts = x_ref.shape[1]
    pairs = wut_ref.shape[0] // 2
    rows = ts // CHUNK

    @pl.when(pl.program_id(1) == 0)
    def _():
        st_ref[...] = jnp.zeros_like(st_ref)

    x = x_ref[0]
    ms = jnp.mean(x * x, axis=-1, keepdims=True)
    h = x * lax.rsqrt(ms + EPS) * (1.0 + mod_ref[0, 1:2, :]) + mod_ref[0, 0:1, :]
    hb = h.astype(BF16)
    proj_ref[...] = jnp.dot(hb, win_ref[...], preferred_element_type=F32)
    u_t = lax.dot_general(wut_ref[...], hb, _NT, preferred_element_type=F32)
    for m in range(ts // LANES):
        first, second = _split_chunk_pairs(u_t[:pairs, m * LANES:(m + 1) * LANES],
                                           u_t[pairs:, m * LANES:(m + 1) * LANES])
        flat_ref[2 * m * pairs:(2 * m + 1) * pairs, :] = first
        flat_ref[(2 * m + 1) * pairs:(2 * m + 2) * pairs, :] = second
    per_group = S5_GROUP // 2
    for q in range(pairs):
        ut_ref[q // per_group, :, (q % per_group) * LANES:(q % per_group + 1) * LANES] = (
            flat_ref[pl.ds(q, rows, stride=pairs), :])

    lb = lb_ref[...]
    gn = gn_ref[...]
    ltri = ltri_ref[...]
    row = lax.broadcasted_iota(I32, (CHUNK, CHUNK), 0)
    col = lax.broadcasted_iota(I32, (CHUNK, CHUNK), 1)
    causal = row >= col

    def chunk_step(ci, carry):
        r0 = pl.multiple_of(ci * CHUNK, CHUNK)
        q = proj_ref[pl.ds(r0, CHUNK), 0:fdim]
        fl = proj_ref[pl.ds(r0, CHUNK), fdim:2 * fdim]
        iv = proj_ref[pl.ds(r0, CHUNK), 2 * fdim:3 * fdim]
        og = proj_ref[pl.ds(r0, CHUNK), 3 * fdim:4 * fdim]
        f = lb + (1.0 - lb) * _sigmoid(fl)
        b = jnp.dot(ltri, jnp.log(f), precision=lax.Precision.HIGHEST,
                    preferred_element_type=F32)
        b_ref = b[CHUNK // 2 - 1:CHUNK // 2, :]
        b_last = b[CHUNK - 1:CHUNK, :]
        qs = _silu(q)
        kk = 1.0 - f
        qe = (qs * jnp.exp(b - b_ref)).astype(BF16)
        ke = (kk * jnp.exp(b_ref - b)).astype(BF16)
        qb = (qs * jnp.exp(b)).astype(BF16)
        k2 = (kk * jnp.exp(b_last - b)).astype(BF16)
        dec = jnp.exp(b_last)
        ivb = iv.astype(BF16)
        outs = []
        for hh in range(HGRN_HEADS):
            sl = slice(hh * HGRN_KDIM, (hh + 1) * HGRN_KDIM)
            att = lax.dot_general(qe[:, sl], ke[:, sl], _NT, preferred_element_type=F32)
            att = jnp.where(causal, att, 0.0)
            st = st_ref[hh]
            o = jnp.dot(att.astype(BF16), ivb[:, sl], preferred_element_type=F32)
            o = o + lax.dot_general(qb[:, sl], st.astype(BF16), _NT, preferred_element_type=F32)
            st_ref[hh] = st * dec[:, sl] + lax.dot_general(
                ivb[:, sl], k2[:, sl], _TN, preferred_element_type=F32)
            outs.append(o * lax.rsqrt(jnp.mean(o * o, axis=-1, keepdims=True) + EPS))
        o = jnp.concatenate(outs, axis=1) * gn * _silu(og)
        oa_ref[0, pl.ds(r0, CHUNK), :] = o.astype(BF16)
        return carry

    lax.fori_loop(0, ts // CHUNK, chunk_step, 0, unroll=True)


def _mix_front_call(x, mod, w_main, w_ut, lb, gn):
    bsz, seq, d = x.shape
    fdim = HGRN_HEADS * HGRN_KDIM
    ncols = w_main.shape[1]
    nb = w_ut.shape[0]
    groups = nb // S5_GROUP
    ltri = jnp.tril(jnp.ones((CHUNK, CHUNK), F32))
    ts = SEQ_TILE
    tiles = seq // ts
    rows = ts // CHUNK
    return pl.pallas_call(
        _mix_front_kernel,
        out_shape=(jax.ShapeDtypeStruct((groups, bsz * seq // CHUNK, S5_GROUP * CHUNK), F32),
                   jax.ShapeDtypeStruct((bsz, seq, fdim), BF16)),
        grid=(bsz, tiles),
        in_specs=[pl.BlockSpec((1, ts, d), lambda b, j: (b, j, 0)),
                  pl.BlockSpec((1, 6, d), lambda b, j: (b, 0, 0)),
                  pl.BlockSpec((d, ncols), lambda b, j: (0, 0)),
                  pl.BlockSpec((nb, d), lambda b, j: (0, 0)),
                  pl.BlockSpec((1, fdim), lambda b, j: (0, 0)),
                  pl.BlockSpec((1, fdim), lambda b, j: (0, 0)),
                  pl.BlockSpec((CHUNK, CHUNK), lambda b, j: (0, 0))],
        out_specs=(pl.BlockSpec((groups, rows, S5_GROUP * CHUNK), lambda b, j: (0, b * tiles + j, 0)),
                   pl.BlockSpec((1, ts, fdim), lambda b, j: (b, j, 0))),
        scratch_shapes=[pltpu.VMEM((ts, ncols), F32),
                        pltpu.VMEM((HGRN_HEADS, fdim // HGRN_HEADS, HGRN_KDIM), F32),
                        pltpu.VMEM((rows * nb // 2, LANES), F32)],
        compiler_params=pltpu.CompilerParams(
            dimension_semantics=("arbitrary", "arbitrary"), vmem_limit_bytes=VMEM_LIMIT),
        name="mix_front",
    )(x, mod, w_main, w_ut, lb, gn, ltri)


def _s5_tables(lam_re, lam_im, log_dt, b_re, b_im, c_re, c_im):
    t = CHUNK
    hp = lax.Precision.HIGHEST
    lam = lax.complex(jnp.minimum(lam_re, -1e-4), lam_im)
    lam_dt = lam * jnp.exp(log_dt)[:, None]
    lam_bar = jnp.exp(lam_dt)
    b_bar = ((lam_bar - 1.0) / lam)[..., None] * lax.complex(b_re, b_im)
    c_mat = lax.complex(c_re, c_im)
    taus = jnp.arange(t + 1, dtype=F32)
    lam_pow = jnp.exp(lam_dt[:, None, :] * taus[None, :, None])
    g, p = lam.shape
    c = b_re.shape[-1]
    cl = c_mat[:, None, :, :] * lam_pow[:, :t, None, :]
    kr = (jnp.einsum('gtcp,gpi->gtci', cl.real, b_bar.real, precision=hp)
          - jnp.einsum('gtcp,gpi->gtci', cl.imag, b_bar.imag, precision=hp))
    z = jnp.pad(kr.transpose(0, 3, 2, 1).astype(BF16), ((0, 0), (0, 0), (0, 0), (t - 1, 0)))
    kg = jnp.stack([z[..., t - 1 - s:2 * t - 1 - s] for s in range(t)], axis=2)
    m = kg.reshape(g, c * t, c * t)
    pc = lam_pow[:, t - 1::-1][:, :t, :, None] * b_bar[:, None, :, :]
    pc = pc.transpose(0, 3, 1, 2).reshape(g, c * t, p)
    p_tab = jnp.concatenate([pc.real, pc.imag], axis=-1)
    ql = c_mat[:, None, :, :] * lam_pow[:, 1:t + 1, None, :]
    ql = ql.transpose(0, 3, 2, 1).reshape(g, p, c * t)
    q_tab = jnp.concatenate([ql.real, -ql.imag], axis=1)
    lam_t = lam_pow[:, t]
    a1 = jnp.concatenate([lam_t.real, lam_t.real], axis=-1)[:, None, :]
    a2 = jnp.concatenate([-lam_t.imag, lam_t.imag], axis=-1)[:, None, :]
    return m.astype(BF16), p_tab.astype(BF16), q_tab.astype(BF16), a1, a2


def _s5_kernel(u_ref, m_ref, p_ref, q_ref, a1_ref, a2_ref, y_ref, v_ref, xs_ref, *, n_chunks, n_batch):
    u = u_ref[0].astype(BF16)
    v_ref[...] = jnp.dot(u, p_ref[0], preferred_element_type=F32)
    a1 = a1_ref[0]
    a2 = a2_ref[0]
    half = xs_ref.shape[1] // 2

    def step(n, state):
        xs_ref[pl.ds(n, n_batch, stride=n_chunks), :] = state
        return (a1 * state + a2 * pltpu.roll(state, half, 1)
                + v_ref[pl.ds(n, n_batch, stride=n_chunks), :])

    lax.fori_loop(0, n_chunks, step, jnp.zeros((n_batch, xs_ref.shape[1]), F32))
    y = jnp.dot(u, m_ref[0], preferred_element_type=F32)
    y_ref[0] = y + jnp.dot(xs_ref[...].astype(BF16), q_ref[0], preferred_element_type=F32)


def _s5_call(ut, m, p_tab, q_tab, a1, a2, n_chunks, n_batch):
    g, rows, width = ut.shape
    p2 = p_tab.shape[-1]
    return pl.pallas_call(
        functools.partial(_s5_kernel, n_chunks=n_chunks, n_batch=n_batch),
        out_shape=jax.ShapeDtypeStruct((g, rows, width), F32),
        grid=(g,),
        in_specs=[pl.BlockSpec((1, rows, width), lambda i: (i, 0, 0)),
                  pl.BlockSpec((1, width, width), lambda i: (i, 0, 0)),
                  pl.BlockSpec((1, width, p2), lambda i: (i, 0, 0)),
                  pl.BlockSpec((1, p2, width), lambda i: (i, 0, 0)),
                  pl.BlockSpec((1, 1, p2), lambda i: (i, 0, 0)),
                  pl.BlockSpec((1, 1, p2), lambda i: (i, 0, 0))],
        out_specs=pl.BlockSpec((1, rows, width), lambda i: (i, 0, 0)),
        scratch_shapes=[pltpu.VMEM((rows, p2), F32), pltpu.VMEM((rows, p2), F32)],
        compiler_params=pltpu.CompilerParams(
            dimension_semantics=("arbitrary",), vmem_limit_bytes=VMEM_LIMIT),
        name="s5",
    )(ut, m, p_tab, q_tab, a1, a2)


def _token_major(flat_ref, src_ref):
    groups, rows, _ = src_ref.shape
    per_group = S5_GROUP // 2
    pairs = groups * per_group
    for q in range(pairs):
        flat_ref[pl.ds(q, rows, stride=pairs), :] = (
            src_ref[q // per_group, :, (q % per_group) * LANES:(q % per_group + 1) * LANES])
    tiles = []
    for m in range(rows // 2):
        even, odd = _split_chunk_pairs(flat_ref[2 * m * pairs:(2 * m + 1) * pairs, :],
                                       flat_ref[(2 * m + 1) * pairs:(2 * m + 2) * pairs, :])
        tiles.append(jnp.concatenate([even, odd], axis=0))
    return jnp.concatenate(tiles, axis=1)


def _mix_back_kernel(x_ref, oa_ref, yt_ref, ut_ref, mod_ref, dskip_ref, wglu_ref, bglu_ref,
                     wout_ref, wrh_ref, wrl_ref, wsg_ref, wsu_ref, wsd_ref,
                     x1_ref, hp_ref, lt_ref, flat_ref):
    na = oa_ref.shape[1]
    y_t = _token_major(flat_ref, yt_ref)
    u_t = _token_major(flat_ref, ut_ref)
    z_t = jax.nn.gelu(y_t + dskip_ref[...] * u_t)
    gate_t = _sigmoid(jnp.dot(wglu_ref[...], z_t.astype(BF16), preferred_element_type=F32)
                      + bglu_ref[...])
    ob_t = (z_t * gate_t).astype(BF16)
    mixed = (jnp.dot(oa_ref[...], wout_ref[0:na, :], preferred_element_type=F32)
             + lax.dot_general(ob_t, wout_ref[na:, :], _TN, preferred_element_type=F32))
    x1 = x_ref[...] + mod_ref[0, 2:3, :] * mixed
    ms = jnp.mean(x1 * x1, axis=-1, keepdims=True)
    h2 = x1 * lax.rsqrt(ms + EPS) * (1.0 + mod_ref[0, 4:5, :]) + mod_ref[0, 3:4, :]
    _store_packed(hp_ref, h2, h2.shape[0])
    h_hi = h2.astype(BF16)
    h_lo = (h2 - h_hi.astype(F32)).astype(BF16)
    lt = lax.dot_general(wrh_ref[...], h_hi, _NT, preferred_element_type=F32)
    lt = lt + lax.dot_general(wrl_ref[...], h_hi, _NT, preferred_element_type=F32)
    lt = lt + lax.dot_general(wrh_ref[...], h_lo, _NT, preferred_element_type=F32)
    lt_ref[...] = lt
    hid = _silu(jnp.dot(h_hi, wsg_ref[...], preferred_element_type=F32)) * jnp.dot(
        h_hi, wsu_ref[...], preferred_element_type=F32)
    shared = jnp.dot(hid.astype(BF16), wsd_ref[...], preferred_element_type=F32)
    x1_ref[...] = x1 + mod_ref[0, 5:6, :] * shared


def _mix_back_call(x2d, oa, yt, ut, mod, dskip, wglu, bglu, wout, wrh, wrl, wsg, wsu, wsd, seq):
    n, d = x2d.shape
    nb = oa.shape[1]
    ne = wrh.shape[0]
    dsh = wsg.shape[1]
    tm = TOK_TILE
    per_b = seq // tm
    groups, _, width = yt.shape
    rows = tm // CHUNK
    const = lambda i: (0, 0)
    flat_block = pl.BlockSpec((groups, rows, width), lambda i: (0, i, 0))
    return pl.pallas_call(
        _mix_back_kernel,
        out_shape=(jax.ShapeDtypeStruct((n, d), F32),
                   jax.ShapeDtypeStruct((n * PACK_ROWS, LANES), I32),
                   jax.ShapeDtypeStruct((ne, n), F32)),
        grid=(n // tm,),
        in_specs=[pl.BlockSpec((tm, d), lambda i: (i, 0)),
                  pl.BlockSpec((tm, nb), lambda i: (i, 0)),
                  flat_block,
                  flat_block,
                  pl.BlockSpec((1, 6, d), lambda i: (i // per_b, 0, 0)),
                  pl.BlockSpec((nb, 1), const),
                  pl.BlockSpec((nb, nb), const),
                  pl.BlockSpec((nb, 1), const),
                  pl.BlockSpec((d, d), const),
                  pl.BlockSpec((ne, d), const),
                  pl.BlockSpec((ne, d), const),
                  pl.BlockSpec((d, dsh), const),
                  pl.BlockSpec((d, dsh), const),
                  pl.BlockSpec((dsh, d), const)],
        out_specs=(pl.BlockSpec((tm, d), lambda i: (i, 0)),
                   pl.BlockSpec((tm * PACK_ROWS, LANES), lambda i: (i, 0)),
                   pl.BlockSpec((ne, tm), lambda i: (0, i))),
        scratch_shapes=[pltpu.VMEM((rows * nb // 2, LANES), F32)],
        compiler_params=pltpu.CompilerParams(
            dimension_semantics=("arbitrary",), vmem_limit_bytes=VMEM_LIMIT),
        name="mix_back",
    )(x2d, oa, yt, ut, mod, dskip, wglu, bglu, wout, wrh, wrl, wsg, wsu, wsd)


def _route_kernel(lt_ref, bias_ref, su_ref, idx_ref, w_ref, rank_ref, cnt_ref, run_ref):
    ne, tr = lt_ref.shape
    per_group = ne // N_EXPERT_GROUPS
    neg = -jnp.inf

    @pl.when(pl.program_id(0) == 0)
    def _():
        run_ref[...] = jnp.zeros_like(run_ref)

    s = _sigmoid(lt_ref[...])
    sel = s + bias_ref[...]
    gio = lax.broadcasted_iota(I32, (per_group, tr), 0)
    gscore = []
    for g in range(N_EXPERT_GROUPS):
        v = sel[g * per_group:(g + 1) * per_group, :]
        m1 = jnp.max(v, axis=0, keepdims=True)
        i1 = jnp.min(jnp.where(v == m1, gio, per_group), axis=0, keepdims=True)
        m2 = jnp.max(jnp.where(gio == i1, neg, v), axis=0, keepdims=True)
        gscore.append(m1 + m2)
    masked = []
    for g in range(N_EXPERT_GROUPS):
        ahead = jnp.zeros((1, tr), I32)
        for o in range(N_EXPERT_GROUPS):
            if o == g:
                continue
            wins = (gscore[o] >= gscore[g]) if o < g else (gscore[o] > gscore[g])
            ahead = ahead + wins.astype(I32)
        keep = ahead < TOPK_GROUPS
        masked.append(jnp.where(keep, sel[g * per_group:(g + 1) * per_group, :], neg))
    selm = jnp.concatenate(masked, axis=0)
    eio = lax.broadcasted_iota(I32, (ne, tr), 0)
    hits = jnp.zeros((ne, tr), F32)
    idxs, ws = [], []
    for k in range(TOP_K):
        m = jnp.max(selm, axis=0, keepdims=True)
        ik = jnp.min(jnp.where(selm == m, eio, ne), axis=0, keepdims=True)
        onehot = eio == ik
        ws.append(jnp.sum(jnp.where(onehot, s, 0.0), axis=0, keepdims=True))
        hits = jnp.where(onehot, 1.0, hits)
        selm = jnp.where(onehot, neg, selm)
        idxs.append(ik)
    wsum = ws[0]
    for k in range(1, TOP_K):
        wsum = wsum + ws[k]
    scale = ROUTE_SCALE / wsum
    ranks = jnp.dot(hits.astype(BF16), su_ref[...], preferred_element_type=F32) + run_ref[...]
    for k in range(TOP_K):
        idx_ref[k:k + 1, :] = idxs[k]
        w_ref[k:k + 1, :] = ws[k] * scale
        rk = jnp.sum(jnp.where(eio == idxs[k], ranks, 0.0), axis=0, keepdims=True)
        rank_ref[k:k + 1, :] = rk.astype(I32)
    run_ref[...] = run_ref[...] + jnp.sum(hits, axis=1, keepdims=True)
    cnt_ref[...] = run_ref[...]


def _route_call(lt, bias):
    ne, n = lt.shape
    tr = ROUTE_TILE
    su = jnp.triu(jnp.ones((tr, tr), F32), k=1).astype(BF16)
    return pl.pallas_call(
        _route_kernel,
        out_shape=(jax.ShapeDtypeStruct((TOP_K, n), I32),
                   jax.ShapeDtypeStruct((TOP_K, n), F32),
                   jax.ShapeDtypeStruct((TOP_K, n), I32),
                   jax.ShapeDtypeStruct((ne, 1), F32)),
        grid=(n // tr,),
        in_specs=[pl.BlockSpec((ne, tr), lambda i: (0, i)),
                  pl.BlockSpec((ne, 1), lambda i: (0, 0)),
                  pl.BlockSpec((tr, tr), lambda i: (0, 0))],
        out_specs=(pl.BlockSpec((TOP_K, tr), lambda i: (0, i)),
                   pl.BlockSpec((TOP_K, tr), lambda i: (0, i)),
                   pl.BlockSpec((TOP_K, tr), lambda i: (0, i)),
                   pl.BlockSpec((ne, 1), lambda i: (0, 0))),
        scratch_shapes=[pltpu.VMEM((ne, 1), F32)],
        compiler_params=pltpu.CompilerParams(
            dimension_semantics=("arbitrary",), vmem_limit_bytes=VMEM_LIMIT),
        name="route",
    )(lt, bias.reshape(ne, 1), su)


def _pos_kernel(idx_ref, rank_ref, st_ref, pos_ref):
    ne = st_ref.shape[0]
    tp = idx_ref.shape[1]
    eio = lax.broadcasted_iota(I32, (ne, tp), 0)
    st = st_ref[...]
    for k in range(TOP_K):
        base = jnp.sum(jnp.where(eio == idx_ref[k:k + 1, :], st, 0), axis=0, keepdims=True)
        pos_ref[k:k + 1, :] = base + rank_ref[k:k + 1, :]


def _pos_call(top_idx, rank, starts):
    kk, n = top_idx.shape
    ne = starts.shape[0]
    tp = POS_TILE
    return pl.pallas_call(
        _pos_kernel,
        out_shape=jax.ShapeDtypeStruct((kk, n), I32),
        grid=(n // tp,),
        in_specs=[pl.BlockSpec((kk, tp), lambda i: (0, i)),
                  pl.BlockSpec((kk, tp), lambda i: (0, i)),
                  pl.BlockSpec((ne, 1), lambda i: (0, 0))],
        out_specs=pl.BlockSpec((kk, tp), lambda i: (0, i)),
        compiler_params=pltpu.CompilerParams(
            dimension_semantics=("arbitrary",), vmem_limit_bytes=VMEM_LIMIT),
        name="pos",
    )(top_idx, rank, starts.reshape(ne, 1))


def _sc_mesh():
    return plsc.VectorSubcoreMesh(core_axis_name="c", subcore_axis_name="s",
                                  num_cores=SC_CORES, num_subcores=SC_SUBCORES)


def _sc_worker():
    return lax.axis_index("s") * SC_CORES + lax.axis_index("c")


def _dispatch_call(pos_win, hp):
    n = hp.shape[0]
    n_workers = SC_CORES * SC_SUBCORES
    wins_per_worker = n // SC_WINDOW // n_workers

    def body(hp_hbm, pos_hbm, xs_hbm, idx_v, rows_v, sem):
        first_win = _sc_worker() * wins_per_worker

        @pl.loop(0, wins_per_worker)
        def _(w):
            win = first_win + w
            pltpu.sync_copy(hp_hbm.at[pl.ds(win * SC_WINDOW, SC_WINDOW)], rows_v)
            pltpu.sync_copy(pos_hbm.at[win], idx_v)
            copies = [pltpu.async_copy(rows_v, xs_hbm.at[idx_v.at[k]], sem) for k in range(TOP_K)]
            for cp in copies:
                cp.wait()

    return pl.kernel(
        body,
        out_type=jax.ShapeDtypeStruct((n * TOP_K,) + hp.shape[1:], hp.dtype),
        mesh=_sc_mesh(),
        scratch_types=[pltpu.VMEM((TOP_K, SC_WINDOW), I32),
                       pltpu.VMEM((SC_WINDOW,) + hp.shape[1:], hp.dtype),
                       pltpu.SemaphoreType.DMA],
        name="dispatch",
    )(hp, pos_win)


def _collect_call(pos_win, y_sorted):
    n = pos_win.shape[0] * SC_WINDOW
    n_workers = SC_CORES * SC_SUBCORES
    wins_per_worker = n // SC_WINDOW // n_workers

    def body(ys_hbm, pos_hbm, out_hbm, idx_v, rows_a, rows_b, gather_sem, write_sem):
        first_win = _sc_worker() * wins_per_worker
        bufs = (rows_a, rows_b)

        @pl.loop(0, wins_per_worker)
        def _(w):
            win = first_win + w
            pltpu.sync_copy(pos_hbm.at[win], idx_v)

            def gather(k):
                return pltpu.async_copy(ys_hbm.at[idx_v.at[k]], bufs[k % 2], gather_sem)

            pending_gather = gather(0)
            pending_write = None
            for k in range(TOP_K):
                pending_gather.wait()
                if pending_write is not None:
                    for cp in pending_write:
                        cp.wait()
                if k + 1 < TOP_K:
                    pending_gather = gather(k + 1)
                pending_write = [
                    pltpu.async_copy(bufs[k % 2].at[:, j],
                                     out_hbm.at[j, k, pl.ds(win * SC_WINDOW, SC_WINDOW)], write_sem)
                    for j in range(PACK_ROWS)]
            for cp in pending_write:
                cp.wait()

    row_buf = pltpu.VMEM((SC_WINDOW,) + y_sorted.shape[1:], y_sorted.dtype)
    return pl.kernel(
        body,
        out_type=jax.ShapeDtypeStruct((PACK_ROWS, TOP_K, n, LANES), y_sorted.dtype),
        mesh=_sc_mesh(),
        scratch_types=[pltpu.VMEM((TOP_K, SC_WINDOW), I32), row_buf, row_buf,
                       pltpu.SemaphoreType.DMA, pltpu.SemaphoreType.DMA],
        name="collect",
    )(y_sorted, pos_win)


def _experts_kernel(pe_ref, pb_ref, pv_ref, st_ref, en_ref,
                    xs_hbm, wg_ref, wu_ref, wd_ref, y_hbm,
                    wgb_ref, wub_ref, wdb_ref, acc_ref, xbuf_ref, stage_ref,
                    in_sems, out_sems, state_ref):
    p = pl.program_id(0)
    n_pairs = pl.num_programs(0)
    e = pe_ref[p]
    blk = pb_ref[p]
    rb = acc_ref.shape[0]
    prev = jnp.maximum(p - 1, 0)
    nxt = jnp.minimum(p + 1, n_pairs - 1)
    live = pv_ref[p] == 1
    first = jnp.logical_or(p == 0, pb_ref[prev] != blk)
    block_ends = jnp.logical_or(pb_ref[nxt] != blk, pv_ref[nxt] == 0)
    last = jnp.logical_or(p == n_pairs - 1, block_ends)

    def in_copy(j, block, slot):
        return pltpu.make_async_copy(xs_hbm.at[pl.ds(block * rb, rb), j], xbuf_ref.at[slot, j],
                                     in_sems.at[slot])

    def out_copy(j, block, slot):
        return pltpu.make_async_copy(stage_ref.at[slot, j], y_hbm.at[pl.ds(block * rb, rb), j],
                                     out_sems.at[slot])

    def drain(slot):
        @pl.when(state_ref[1 + slot] == 1)
        def _():
            for j in range(PACK_ROWS):
                out_copy(j, 0, slot).wait()
            state_ref[1 + slot] = 0

    @pl.when(p == 0)
    def _():
        state_ref[0] = 0
        state_ref[1] = 0
        state_ref[2] = 0
        for j in range(PACK_ROWS):
            in_copy(j, blk, 0).start()

    slot = jnp.where(jnp.logical_and(first, p > 0), 1 - state_ref[0], state_ref[0])
    state_ref[0] = slot

    @pl.when(jnp.logical_and(p < n_pairs - 1,
                             jnp.logical_and(pb_ref[nxt] != blk, pv_ref[nxt] == 1)))
    def _():
        for j in range(PACK_ROWS):
            in_copy(j, pb_ref[nxt], 1 - slot).start()

    @pl.when(first)
    def _():
        for j in range(PACK_ROWS):
            in_copy(j, blk, slot).wait()
        drain(slot)

    @pl.when(jnp.logical_or(p == 0, pe_ref[prev] != e))
    def _():
        wgb_ref[...] = wg_ref[0].astype(BF16)
        wub_ref[...] = wu_ref[0].astype(BF16)
        wdb_ref[...] = wd_ref[0].astype(BF16)

    lo_row = st_ref[e]
    hi_row = en_ref[e]

    def stage_rows(s, val):
        for j in range(PACK_ROWS):
            lo = val[:, 2 * j * LANES:(2 * j + 1) * LANES]
            hi = val[:, (2 * j + 1) * LANES:(2 * j + 2) * LANES]
            stage_ref[slot, j, s * SUB_BLOCK:(s + 1) * SUB_BLOCK, :] = pltpu.pack_elementwise(
                [lo, hi], packed_dtype=BF16)

    for s in range(rb // SUB_BLOCK):
        row0 = blk * rb + s * SUB_BLOCK
        sub = pl.ds(s * SUB_BLOCK, SUB_BLOCK)
        touched = jnp.logical_and(live, jnp.logical_and(row0 < hi_row, row0 + SUB_BLOCK > lo_row))

        @pl.when(touched)
        def _():
            pieces = []
            for j in range(PACK_ROWS):
                w = xbuf_ref[slot, j, sub, :]
                pieces.append(lax.bitcast_convert_type(w.astype(jnp.int16), BF16))
                pieces.append(lax.bitcast_convert_type(
                    lax.shift_right_logical(w, 16).astype(jnp.int16), BF16))
            xb = jnp.concatenate(pieces, axis=1)
            gate = jnp.dot(xb, wgb_ref[...], preferred_element_type=F32)
            up = jnp.dot(xb, wub_ref[...], preferred_element_type=F32)
            rows = row0 + lax.broadcasted_iota(I32, (SUB_BLOCK, 1), 0)
            mine = jnp.logical_and(rows >= lo_row, rows < hi_row)
            hid = jnp.where(mine, _silu(gate) * up, 0.0).astype(BF16)
            yb = jnp.dot(hid, wdb_ref[...], preferred_element_type=F32)
            opens = lo_row <= row0
            closes = hi_row >= row0 + SUB_BLOCK

            @pl.when(jnp.logical_and(opens, closes))
            def _():
                stage_rows(s, yb)

            @pl.when(jnp.logical_and(opens, jnp.logical_not(closes)))
            def _():
                acc_ref[sub, :] = yb

            @pl.when(jnp.logical_and(jnp.logical_not(opens), jnp.logical_not(closes)))
            def _():
                acc_ref[sub, :] += yb

            @pl.when(jnp.logical_and(jnp.logical_not(opens), closes))
            def _():
                stage_rows(s, acc_ref[sub, :] + yb)

    @pl.when(jnp.logical_and(live, last))
    def _():
        for j in range(PACK_ROWS):
            out_copy(j, blk, slot).start()
        state_ref[1 + slot] = 1

    @pl.when(p == n_pairs - 1)
    def _():
        drain(0)
        drain(1)


def _experts_call(pair_e, pair_blk, pair_ok, starts, ends, xs, w_gate, w_up, w_down):
    ne, d, de = w_gate.shape
    rb = ROW_BLOCK
    n_pairs = pair_e.shape[0]
    grid_spec = pltpu.PrefetchScalarGridSpec(
        num_scalar_prefetch=5,
        grid=(n_pairs,),
        in_specs=[pl.BlockSpec(memory_space=pl.ANY),
                  pl.BlockSpec((1, d, de), lambda p, pe, pb, pv, st, en: (pe[p], 0, 0)),
                  pl.BlockSpec((1, d, de), lambda p, pe, pb, pv, st, en: (pe[p], 0, 0)),
                  pl.BlockSpec((1, de, d), lambda p, pe, pb, pv, st, en: (pe[p], 0, 0))],
        out_specs=pl.BlockSpec(memory_space=pl.ANY),
        scratch_shapes=[pltpu.VMEM((d, de), BF16), pltpu.VMEM((d, de), BF16),
                        pltpu.VMEM((de, d), BF16), pltpu.VMEM((rb, d), F32),
                        pltpu.VMEM((2, PACK_ROWS, rb, LANES), I32),
                        pltpu.VMEM((2, PACK_ROWS, rb, LANES), I32),
                        pltpu.SemaphoreType.DMA((2,)), pltpu.SemaphoreType.DMA((2,)),
                        pltpu.SMEM((3,), I32)],
    )
    return pl.pallas_call(
        _experts_kernel,
        out_shape=jax.ShapeDtypeStruct(xs.shape, xs.dtype),
        grid_spec=grid_spec,
        compiler_params=pltpu.CompilerParams(
            dimension_semantics=("arbitrary",), vmem_limit_bytes=VMEM_LIMIT),
        name="experts",
    )(pair_e, pair_blk, pair_ok, starts, ends, xs, w_gate, w_up, w_down)


def _combine_kernel(y0_ref, y1_ref, y2_ref, y3_ref, x1_ref, w_ref, g2_ref, fg_ref, o_ref):
    w = w_ref[...]
    parts = []
    for y_ref in (y0_ref, y1_ref, y2_ref, y3_ref):
        for half in range(2):
            acc = None
            for k in range(TOP_K):
                piece = pltpu.unpack_elementwise(y_ref[k], index=half, packed_dtype=BF16,
                                                 unpacked_dtype=F32) * w[:, k:k + 1]
                acc = piece if acc is None else acc + piece
            parts.append(acc)
    x2 = x1_ref[...] + g2_ref[0] * jnp.concatenate(parts, axis=1)
    ms = jnp.mean(x2 * x2, axis=-1, keepdims=True)
    o_ref[...] = x2 * lax.rsqrt(ms + EPS) * fg_ref[...]


def _combine_kernel_into(y0_ref, y1_ref, y2_ref, y3_ref, x1_ref, w_ref, g2_ref, fg_ref, prev_ref, o_ref):
    del prev_ref
    _combine_kernel(y0_ref, y1_ref, y2_ref, y3_ref, x1_ref, w_ref, g2_ref, fg_ref, o_ref)


def _combine_call(y_tok, x1s, w_tok, g2, fgain, seq, part, prev_out):
    n, d = x1s.shape
    tm = MOVE_TILE
    per_b = seq // tm
    tiles = y_tok.shape[2] // tm
    t0 = part * tiles

    def piece_spec(j):
        return pl.BlockSpec((None, TOP_K, tm, LANES), lambda i: (j, 0, i, 0))

    in_specs = [piece_spec(j) for j in range(PACK_ROWS)] + [
        pl.BlockSpec((tm, d), lambda i: (t0 + i, 0)),
        pl.BlockSpec((tm, TOP_K), lambda i: (t0 + i, 0)),
        pl.BlockSpec((1, 1, d), lambda i: ((t0 + i) // per_b, 0, 0)),
        pl.BlockSpec((1, d), lambda i: (0, 0))]
    args = [y_tok, y_tok, y_tok, y_tok, x1s, w_tok, g2, fgain]
    aliases = {}
    body = _combine_kernel
    if prev_out is not None:
        in_specs.append(pl.BlockSpec(memory_space=pl.ANY))
        args.append(prev_out)
        aliases = {len(args) - 1: 0}
        body = _combine_kernel_into
    return pl.pallas_call(
        body,
        out_shape=jax.ShapeDtypeStruct((n, d), F32),
        grid=(tiles,),
        in_specs=in_specs,
        out_specs=pl.BlockSpec((tm, d), lambda i: (t0 + i, 0)),
        input_output_aliases=aliases,
        compiler_params=pltpu.CompilerParams(
            dimension_semantics=("arbitrary",), vmem_limit_bytes=VMEM_LIMIT),
        name="combine",
    )(*args)


def _pair_tables(counts, n_rows):
    ne = counts.shape[0]
    sizes = counts.astype(I32)
    ends = jnp.cumsum(sizes)
    starts = ends - sizes
    first_blk = starts // ROW_BLOCK
    last_blk = (ends - 1) // ROW_BLOCK
    n_pairs = jnp.where(sizes > 0, last_blk - first_blk + 1, 0)
    pair_end = jnp.cumsum(n_pairs)
    pair_start = pair_end - n_pairs
    max_pairs = n_rows // ROW_BLOCK + ne
    p = jnp.arange(max_pairs, dtype=I32)
    ok = p < pair_end[-1]
    pc = jnp.minimum(p, pair_end[-1] - 1)
    pair_e = jnp.minimum(jnp.searchsorted(pair_end, pc, side='right'), ne - 1).astype(I32)
    pair_blk = (first_blk[pair_e] + pc - pair_start[pair_e]).astype(I32)
    return pair_e, pair_blk, ok.astype(I32), starts.astype(I32), ends.astype(I32)


def kernel(x, c, w_ada, b_ada, w_in, lb_logits, hgrn_norm, lam_re, lam_im, log_dt, b_re, b_im,
           c_re, c_im, d_skip, w_glu, b_glu, w_out, w_router, router_bias, w_gate, w_up, w_down,
           ws_gate, ws_up, ws_down, final_gain):
    bsz, seq, d = x.shape
    n = bsz * seq
    fdim = HGRN_HEADS * HGRN_KDIM
    n_chunks = seq // CHUNK
    lb = jnp.cumsum(jax.nn.softmax(lb_logits.astype(F32), axis=0), axis=0)[0].reshape(1, fdim)

    mod = _mod_call(c, w_ada[0], b_ada[0]).reshape(bsz, 6, d)
    nb = w_in.shape[2] - 4 * fdim
    perm = jnp.concatenate([jnp.arange(0, nb, 2), jnp.arange(1, nb, 2)])
    w_in_b = w_in[0].astype(BF16)
    ut, out_a = _mix_front_call(x, mod, w_in_b[:, :4 * fdim], w_in_b[:, 4 * fdim:][:, perm].T, lb,
                                hgrn_norm[0].reshape(1, fdim))

    m_tab, p_tab, q_tab, a1, a2 = _s5_tables(lam_re[0], lam_im[0], log_dt[0], b_re[0], b_im[0],
                                             c_re[0], c_im[0])
    yt = _s5_call(ut, m_tab, p_tab, q_tab, a1, a2, n_chunks, bsz)

    wr_t = w_router[0].T
    wr_hi = wr_t.astype(BF16)
    wr_lo = (wr_t - wr_hi.astype(F32)).astype(BF16)
    w_out_b = w_out[0].astype(BF16)
    w_out_p = jnp.concatenate([w_out_b[:fdim], w_out_b[fdim:][perm]], axis=0)
    x1s, hp, logits_t = _mix_back_call(
        x.reshape(n, d), out_a.reshape(n, fdim), yt, ut, mod,
        d_skip[0][perm].reshape(nb, 1), w_glu[0][perm][:, perm].T.astype(BF16),
        b_glu[0][perm].reshape(nb, 1), w_out_p, wr_hi, wr_lo, ws_gate[0].astype(BF16),
        ws_up[0].astype(BF16), ws_down[0].astype(BF16), seq)

    top_idx, top_w, rank, counts = _route_call(logits_t, router_bias[0])
    pair_e, pair_blk, pair_ok, starts, ends = _pair_tables(counts[:, 0], n * TOP_K)
    pos = _pos_call(top_idx, rank, starts)

    pos_win = pos.reshape(TOP_K, n // SC_WINDOW, SC_WINDOW).transpose(1, 0, 2)
    xs = _dispatch_call(pos_win, hp.reshape(n, PACK_ROWS, LANES))
    y_sorted = _experts_call(pair_e, pair_blk, pair_ok, starts, ends, xs, w_gate[0], w_up[0], w_down[0])
    w_tok = top_w.T
    g2 = mod[:, 5:6, :]
    fgain = final_gain.reshape(1, d)
    wins = pos_win.shape[0] // TAIL_PARTS
    out = None
    for part in range(TAIL_PARTS):
        y_tok = _collect_call(pos_win[part * wins:(part + 1) * wins], y_sorted)
        out = _combine_call(y_tok, x1s, w_tok, g2, fgain, seq, part, out)
    return out.reshape(bsz, seq, d)
```

```python
import functools

import jax
import jax.numpy as jnp
from jax import lax
from jax.experimental import pallas as pl
from jax.experimental.pallas import tpu as pltpu
from jax.experimental.pallas import tpu_sc as plsc

F32 = jnp.float32
BF16 = jnp.bfloat16
I32 = jnp.int32

EPS = 1e-6
CHUNK = 64
HGRN_HEADS = 4
HGRN_KDIM = 128
S5_GROUP = 16
S5_STATE = 64
N_EXPERT_GROUPS = 8
TOPK_GROUPS = 4
TOP_K = 8
ROUTE_SCALE = 2.5
LANES = 128
PACK_ROWS = 4
SC_CORES = 2
SC_SUBCORES = 16
SC_WINDOW = 64

SEQ_TILE = 512
TOK_TILE = 512
ROUTE_TILE = 512
POS_TILE = 1024
MOVE_TILE = 256
ROW_BLOCK = 2048
SUB_BLOCK = 512
TAIL_PARTS = 4
VMEM_LIMIT = 56 * 1024 * 1024

_NT = (((1,), (1,)), ((), ()))
_TN = (((0,), (0,)), ((), ()))


def _sigmoid(v):
    return 1.0 / (1.0 + jnp.exp(-v))


def _silu(v):
    return v * _sigmoid(v)


def _bdot(a, b):
    return jnp.dot(a.astype(BF16), b.astype(BF16), preferred_element_type=F32)


def _store_packed(ref, val, n_rows):
    for j in range(PACK_ROWS):
        lo = val[:, 2 * j * LANES:(2 * j + 1) * LANES]
        hi = val[:, (2 * j + 1) * LANES:(2 * j + 2) * LANES]
        ref[pl.ds(j, n_rows, stride=PACK_ROWS), :] = pltpu.pack_elementwise([lo, hi], packed_dtype=BF16)


def _mod_kernel(c_ref, w_ref, b_ref, o_ref):
    o_ref[...] = _bdot(_silu(c_ref[...]), w_ref[...]) + b_ref[...]


def _mod_call(c, w_ada, b_ada):
    bsz, d = c.shape
    n_out = w_ada.shape[1]
    return pl.pallas_call(
        _mod_kernel,
        out_shape=jax.ShapeDtypeStruct((bsz, n_out), F32),
        grid=(n_out // d,),
        in_specs=[pl.BlockSpec((bsz, d), lambda j: (0, 0)),
                  pl.BlockSpec((d, d), lambda j: (0, j)),
                  pl.BlockSpec((1, d), lambda j: (0, j))],
        out_specs=pl.BlockSpec((bsz, d), lambda j: (0, j)),
        compiler_params=pltpu.CompilerParams(vmem_limit_bytes=VMEM_LIMIT),
        name="mod",
    )(c, w_ada, b_ada.reshape(1, n_out))


def _split_chunk_pairs(tile_even, tile_odd):
    low = lax.broadcasted_iota(I32, tile_even.shape, 1) < CHUNK
    first = jnp.where(low, tile_even, pltpu.roll(tile_odd, CHUNK, 1))
    second = jnp.where(low, pltpu.roll(tile_even, CHUNK, 1), tile_odd)
    return first, second


def _mix_front_kernel(x_ref, mod_ref, win_ref, wut_ref, lb_ref, gn_ref, ltri_ref,
                      ut_ref, oa_ref, proj_ref, st_ref, flat_ref):
    fdim = HGRN_HEADS * HGRN_KDIM
    ts = x_ref.shape[1]
    pairs = wut_ref.shape[0] // 2
    rows = ts // CHUNK

    @pl.when(pl.program_id(1) == 0)
    def _():
        st_ref[...] = jnp.zeros_like(st_ref)

    x = x_ref[0]
    ms = jnp.mean(x * x, axis=-1, keepdims=True)
    h = x * lax.rsqrt(ms + EPS) * (1.0 + mod_ref[0, 1:2, :]) + mod_ref[0, 0:1, :]
    hb = h.astype(BF16)
    proj_ref[...] = jnp.dot(hb, win_ref[...], preferred_element_type=F32)
    u_t = lax.dot_general(wut_ref[...], hb, _NT, preferred_element_type=F32)
    for m in range(ts // LANES):
        first, second = _split_chunk_pairs(u_t[:pairs, m * LANES:(m + 1) * LANES],
                                           u_t[pairs:, m * LANES:(m + 1) * LANES])
        flat_ref[2 * m * pairs:(2 * m + 1) * pairs, :] = first
        flat_ref[(2 * m + 1) * pairs:(2 * m + 2) * pairs, :] = second
    per_group = S5_GROUP // 2
    for q in range(pairs):
        ut_ref[q // per_group, :, (q % per_group) * LANES:(q % per_group + 1) * LANES] = (
            flat_ref[pl.ds(q, rows, stride=pairs), :])

    lb = lb_ref[...]
    gn = gn_ref[...]
    ltri = ltri_ref[...]
    row = lax.broadcasted_iota(I32, (CHUNK, CHUNK), 0)
    col = lax.broadcasted_iota(I32, (CHUNK, CHUNK), 1)
    causal = row >= col

    def chunk_step(ci, carry):
        r0 = pl.multiple_of(ci * CHUNK, CHUNK)
        q = proj_ref[pl.ds(r0, CHUNK), 0:fdim]
        fl = proj_ref[pl.ds(r0, CHUNK), fdim:2 * fdim]
        iv = proj_ref[pl.ds(r0, CHUNK), 2 * fdim:3 * fdim]
        og = proj_ref[pl.ds(r0, CHUNK), 3 * fdim:4 * fdim]
        f = lb + (1.0 - lb) * _sigmoid(fl)
        b = jnp.dot(ltri, jnp.log(f), precision=lax.Precision.HIGHEST,
                    preferred_element_type=F32)
        b_ref = b[CHUNK // 2 - 1:CHUNK // 2, :]
        b_last = b[CHUNK - 1:CHUNK, :]
        qs = _silu(q)
        kk = 1.0 - f
        qe = (qs * jnp.exp(b - b_ref)).astype(BF16)
        ke = (kk * jnp.exp(b_ref - b)).astype(BF16)
        qb = (qs * jnp.exp(b)).astype(BF16)
        k2 = (kk * jnp.exp(b_last - b)).astype(BF16)
        dec = jnp.exp(b_last)
        ivb = iv.astype(BF16)
        outs = []
        for hh in range(HGRN_HEADS):
            sl = slice(hh * HGRN_KDIM, (hh + 1) * HGRN_KDIM)
            att = lax.dot_general(qe[:, sl], ke[:, sl], _NT, preferred_element_type=F32)
            att = jnp.where(causal, att, 0.0)
            st = st_ref[hh]
            o = jnp.dot(att.astype(BF16), ivb[:, sl], preferred_element_type=F32)
            o = o + lax.dot_general(qb[:, sl], st.astype(BF16), _NT, preferred_element_type=F32)
            st_ref[hh] = st * dec[:, sl] + lax.dot_general(
                ivb[:, sl], k2[:, sl], _TN, preferred_element_type=F32)
            outs.append(o * lax.rsqrt(jnp.mean(o * o, axis=-1, keepdims=True) + EPS))
        o = jnp.concatenate(outs, axis=1) * gn * _silu(og)
        oa_ref[0, pl.ds(r0, CHUNK), :] = o.astype(BF16)
        return carry

    lax.fori_loop(0, ts // CHUNK, chunk_step, 0, unroll=True)


def _mix_front_call(x, mod, w_main, w_ut, lb, gn):
    bsz, seq, d = x.shape
    fdim = HGRN_HEADS * HGRN_KDIM
    ncols = w_main.shape[1]
    nb = w_ut.shape[0]
    groups = nb // S5_GROUP
    ltri = jnp.tril(jnp.ones((CHUNK, CHUNK), F32))
    ts = SEQ_TILE
    tiles = seq // ts
    rows = ts // CHUNK
    return pl.pallas_call(
        _mix_front_kernel,
        out_shape=(jax.ShapeDtypeStruct((groups, bsz * seq // CHUNK, S5_GROUP * CHUNK), F32),
                   jax.ShapeDtypeStruct((bsz, seq, fdim), BF16)),
        grid=(bsz, tiles),
        in_specs=[pl.BlockSpec((1, ts, d), lambda b, j: (b, j, 0)),
                  pl.BlockSpec((1, 6, d), lambda b, j: (b, 0, 0)),
                  pl.BlockSpec((d, ncols), lambda b, j: (0, 0)),
                  pl.BlockSpec((nb, d), lambda b, j: (0, 0)),
                  pl.BlockSpec((1, fdim), lambda b, j: (0, 0)),
                  pl.BlockSpec((1, fdim), lambda b, j: (0, 0)),
                  pl.BlockSpec((CHUNK, CHUNK), lambda b, j: (0, 0))],
        out_specs=(pl.BlockSpec((groups, rows, S5_GROUP * CHUNK), lambda b, j: (0, b * tiles + j, 0)),
                   pl.BlockSpec((1, ts, fdim), lambda b, j: (b, j, 0))),
        scratch_shapes=[pltpu.VMEM((ts, ncols), F32),
                        pltpu.VMEM((HGRN_HEADS, fdim // HGRN_HEADS, HGRN_KDIM), F32),
                        pltpu.VMEM((rows * nb // 2, LANES), F32)],
        compiler_params=pltpu.CompilerParams(
            dimension_semantics=("arbitrary", "arbitrary"), vmem_limit_bytes=VMEM_LIMIT),
        name="mix_front",
    )(x, mod, w_main, w_ut, lb, gn, ltri)


def _s5_tables(lam_re, lam_im, log_dt, b_re, b_im, c_re, c_im):
    t = CHUNK
    hp = lax.Precision.HIGHEST
    lam = lax.complex(jnp.minimum(lam_re, -1e-4), lam_im)
    lam_dt = lam * jnp.exp(log_dt)[:, None]
    lam_bar = jnp.exp(lam_dt)
    b_bar = ((lam_bar - 1.0) / lam)[..., None] * lax.complex(b_re, b_im)
    c_mat = lax.complex(c_re, c_im)
    taus = jnp.arange(t + 1, dtype=F32)
    lam_pow = jnp.exp(lam_dt[:, None, :] * taus[None, :, None])
    g, p = lam.shape
    c = b_re.shape[-1]
    cl = c_mat[:, None, :, :] * lam_pow[:, :t, None, :]
    kr = (jnp.einsum('gtcp,gpi->gtci', cl.real, b_bar.real, precision=hp)
          - jnp.einsum('gtcp,gpi->gtci', cl.imag, b_bar.imag, precision=hp))
    z = jnp.pad(kr.transpose(0, 3, 2, 1).astype(BF16), ((0, 0), (0, 0), (0, 0), (t - 1, 0)))
    kg = jnp.stack([z[..., t - 1 - s:2 * t - 1 - s] for s in range(t)], axis=2)
    m = kg.reshape(g, c * t, c * t)
    pc = lam_pow[:, t - 1::-1][:, :t, :, None] * b_bar[:, None, :, :]
    pc = pc.transpose(0, 3, 1, 2).reshape(g, c * t, p)
    p_tab = jnp.concatenate([pc.real, pc.imag], axis=-1)
    ql = c_mat[:, None, :, :] * lam_pow[:, 1:t + 1, None, :]
    ql = ql.transpose(0, 3, 2, 1).reshape(g, p, c * t)
    q_tab = jnp.concatenate([ql.real, -ql.imag], axis=1)
    lam_t = lam_pow[:, t]
    a1 = jnp.concatenate([lam_t.real, lam_t.real], axis=-1)[:, None, :]
    a2 = jnp.concatenate([-lam_t.imag, lam_t.imag], axis=-1)[:, None, :]
    return m.astype(BF16), p_tab.astype(BF16), q_tab.astype(BF16), a1, a2


def _s5_kernel(u_ref, m_ref, p_ref, q_ref, a1_ref, a2_ref, y_ref, v_ref, xs_ref, *, n_chunks, n_batch):
    u = u_ref[0].astype(BF16)
    v_ref[...] = jnp.dot(u, p_ref[0], preferred_element_type=F32)
    a1 = a1_ref[0]
    a2 = a2_ref[0]
    half = xs_ref.shape[1] // 2

    def step(n, state):
        xs_ref[pl.ds(n, n_batch, stride=n_chunks), :] = state
        return (a1 * state + a2 * pltpu.roll(state, half, 1)
                + v_ref[pl.ds(n, n_batch, stride=n_chunks), :])

    lax.fori_loop(0, n_chunks, step, jnp.zeros((n_batch, xs_ref.shape[1]), F32))
    y = jnp.dot(u, m_ref[0], preferred_element_type=F32)
    y_ref[0] = y + jnp.dot(xs_ref[...].astype(BF16), q_ref[0], preferred_element_type=F32)


def _s5_call(ut, m, p_tab, q_tab, a1, a2, n_chunks, n_batch):
    g, rows, width = ut.shape
    p2 = p_tab.shape[-1]
    return pl.pallas_call(
        functools.partial(_s5_kernel, n_chunks=n_chunks, n_batch=n_batch),
        out_shape=jax.ShapeDtypeStruct((g, rows, width), F32),
        grid=(g,),
        in_specs=[pl.BlockSpec((1, rows, width), lambda i: (i, 0, 0)),
                  pl.BlockSpec((1, width, width), lambda i: (i, 0, 0)),
                  pl.BlockSpec((1, width, p2), lambda i: (i, 0, 0)),
                  pl.BlockSpec((1, p2, width), lambda i: (i, 0, 0)),
                  pl.BlockSpec((1, 1, p2), lambda i: (i, 0, 0)),
                  pl.BlockSpec((1, 1, p2), lambda i: (i, 0, 0))],
        out_specs=pl.BlockSpec((1, rows, width), lambda i: (i, 0, 0)),
        scratch_shapes=[pltpu.VMEM((rows, p2), F32), pltpu.VMEM((rows, p2), F32)],
        compiler_params=pltpu.CompilerParams(
            dimension_semantics=("arbitrary",), vmem_limit_bytes=VMEM_LIMIT),
        name="s5",
    )(ut, m, p_tab, q_tab, a1, a2)


def _token_major(flat_ref, src_ref):
    groups, rows, _ = src_ref.shape
    per_group = S5_GROUP // 2
    pairs = groups * per_group
    for q in range(pairs):
        flat_ref[pl.ds(q, rows, stride=pairs), :] = (
            src_ref[q // per_group, :, (q % per_group) * LANES:(q % per_group + 1) * LANES])
    tiles = []
    for m in range(rows // 2):
        even, odd = _split_chunk_pairs(flat_ref[2 * m * pairs:(2 * m + 1) * pairs, :],
                                       flat_ref[(2 * m + 1) * pairs:(2 * m + 2) * pairs, :])
        tiles.append(jnp.concatenate([even, odd], axis=0))
    return jnp.concatenate(tiles, axis=1)


def _mix_back_kernel(x_ref, oa_ref, yt_ref, ut_ref, mod_ref, dskip_ref, wglu_ref, bglu_ref,
                     wout_ref, wrh_ref, wrl_ref, wsg_ref, wsu_ref, wsd_ref,
                     x1_ref, hp_ref, lt_ref, flat_ref):
    na = oa_ref.shape[1]
    y_t = _token_major(flat_ref, yt_ref)
    u_t = _token_major(flat_ref, ut_ref)
    z_t = jax.nn.gelu(y_t + dskip_ref[...] * u_t)
    gate_t = _sigmoid(jnp.dot(wglu_ref[...], z_t.astype(BF16), preferred_element_type=F32)
                      + bglu_ref[...])
    ob_t = (z_t * gate_t).astype(BF16)
    mixed = (jnp.dot(oa_ref[...], wout_ref[0:na, :], preferred_element_type=F32)
             + lax.dot_general(ob_t, wout_ref[na:, :], _TN, preferred_element_type=F32))
    x1 = x_ref[...] + mod_ref[0, 2:3, :] * mixed
    ms = jnp.mean(x1 * x1, axis=-1, keepdims=True)
    h2 = x1 * lax.rsqrt(ms + EPS) * (1.0 + mod_ref[0, 4:5, :]) + mod_ref[0, 3:4, :]
    _store_packed(hp_ref, h2, h2.shape[0])
    h_hi = h2.astype(BF16)
    h_lo = (h2 - h_hi.astype(F32)).astype(BF16)
    lt = lax.dot_general(wrh_ref[...], h_hi, _NT, preferred_element_type=F32)
    lt = lt + lax.dot_general(wrl_ref[...], h_hi, _NT, preferred_element_type=F32)
    lt = lt + lax.dot_general(wrh_ref[...], h_lo, _NT, preferred_element_type=F32)
    lt_ref[...] = lt
    hid = _silu(jnp.dot(h_hi, wsg_ref[...], preferred_element_type=F32)) * jnp.dot(
        h_hi, wsu_ref[...], preferred_element_type=F32)
    shared = jnp.dot(hid.astype(BF16), wsd_ref[...], preferred_element_type=F32)
    x1_ref[...] = x1 + mod_ref[0, 5:6, :] * shared


def _mix_back_call(x2d, oa, yt, ut, mod, dskip, wglu, bglu, wout, wrh, wrl, wsg, wsu, wsd, seq):
    n, d = x2d.shape
    nb = oa.shape[1]
    ne = wrh.shape[0]
    dsh = wsg.shape[1]
    tm = TOK_TILE
    per_b = seq // tm
    groups, _, width = yt.shape
    rows = tm // CHUNK
    const = lambda i: (0, 0)
    flat_block = pl.BlockSpec((groups, rows, width), lambda i: (0, i, 0))
    return pl.pallas_call(
        _mix_back_kernel,
        out_shape=(jax.ShapeDtypeStruct((n, d), F32),
                   jax.ShapeDtypeStruct((n * PACK_ROWS, LANES), I32),
                   jax.ShapeDtypeStruct((ne, n), F32)),
        grid=(n // tm,),
        in_specs=[pl.BlockSpec((tm, d), lambda i: (i, 0)),
                  pl.BlockSpec((tm, nb), lambda i: (i, 0)),
                  flat_block,
                  flat_block,
                  pl.BlockSpec((1, 6, d), lambda i: (i // per_b, 0, 0)),
                  pl.BlockSpec((nb, 1), const),
                  pl.BlockSpec((nb, nb), const),
                  pl.BlockSpec((nb, 1), const),
                  pl.BlockSpec((d, d), const),
                  pl.BlockSpec((ne, d), const),
                  pl.BlockSpec((ne, d), const),
                  pl.BlockSpec((d, dsh), const),
                  pl.BlockSpec((d, dsh), const),
                  pl.BlockSpec((dsh, d), const)],
        out_specs=(pl.BlockSpec((tm, d), lambda i: (i, 0)),
                   pl.BlockSpec((tm * PACK_ROWS, LANES), lambda i: (i, 0)),
                   pl.BlockSpec((ne, tm), lambda i: (0, i))),
        scratch_shapes=[pltpu.VMEM((rows * nb // 2, LANES), F32)],
        compiler_params=pltpu.CompilerParams(
            dimension_semantics=("arbitrary",), vmem_limit_bytes=VMEM_LIMIT),
        name="mix_back",
    )(x2d, oa, yt, ut, mod, dskip, wglu, bglu, wout, wrh, wrl, wsg, wsu, wsd)


def _route_kernel(lt_ref, bias_ref, su_ref, idx_ref, w_ref, rank_ref, cnt_ref, run_ref):
    ne, tr = lt_ref.shape
    per_group = ne // N_EXPERT_GROUPS
    neg = -jnp.inf

    @pl.when(pl.program_id(0) == 0)
    def _():
        run_ref[...] = jnp.zeros_like(run_ref)

    s = _sigmoid(lt_ref[...])
    sel = s + bias_ref[...]
    gio = lax.broadcasted_iota(I32, (per_group, tr), 0)
    gscore = []
    for g in range(N_EXPERT_GROUPS):
        v = sel[g * per_group:(g + 1) * per_group, :]
        m1 = jnp.max(v, axis=0, keepdims=True)
        i1 = jnp.min(jnp.where(v == m1, gio, per_group), axis=0, keepdims=True)
        m2 = jnp.max(jnp.where(gio == i1, neg, v), axis=0, keepdims=True)
        gscore.append(m1 + m2)
    masked = []
    for g in range(N_EXPERT_GROUPS):
        ahead = jnp.zeros((1, tr), I32)
        for o in range(N_EXPERT_GROUPS):
            if o == g:
                continue
            wins = (gscore[o] >= gscore[g]) if o < g else (gscore[o] > gscore[g])
            ahead = ahead + wins.astype(I32)
        keep = ahead < TOPK_GROUPS
        masked.append(jnp.where(keep, sel[g * per_group:(g + 1) * per_group, :], neg))
    selm = jnp.concatenate(masked, axis=0)
    eio = lax.broadcasted_iota(I32, (ne, tr), 0)
    candidate = selm > neg
    idxs, ws = [], []
    for k in range(TOP_K):
        m = jnp.max(selm, axis=0, keepdims=True)
        ik = jnp.min(jnp.where(selm == m, eio, ne), axis=0, keepdims=True)
        onehot = eio == ik
        ws.append(jnp.sum(jnp.where(onehot, s, 0.0), axis=0, keepdims=True))
        selm = jnp.where(onehot, neg, selm)
        idxs.append(ik)
    hits = jnp.where(jnp.logical_and(candidate, selm == neg), 1.0, 0.0)
    wsum = ws[0]
    for k in range(1, TOP_K):
        wsum = wsum + ws[k]
    scale = ROUTE_SCALE / wsum
    ranks = jnp.dot(hits.astype(BF16), su_ref[...], preferred_element_type=F32) + run_ref[...]
    for k in range(TOP_K):
        idx_ref[k:k + 1, :] = idxs[k]
        w_ref[k:k + 1, :] = ws[k] * scale
        rk = jnp.sum(jnp.where(eio == idxs[k], ranks, 0.0), axis=0, keepdims=True)
        rank_ref[k:k + 1, :] = rk.astype(I32)
    run_ref[...] = run_ref[...] + jnp.sum(hits, axis=1, keepdims=True)
    cnt_ref[...] = run_ref[...]


def _route_call(lt, bias):
    ne, n = lt.shape
    tr = ROUTE_TILE
    su = jnp.triu(jnp.ones((tr, tr), F32), k=1).astype(BF16)
    return pl.pallas_call(
        _route_kernel,
        out_shape=(jax.ShapeDtypeStruct((TOP_K, n), I32),
                   jax.ShapeDtypeStruct((TOP_K, n), F32),
                   jax.ShapeDtypeStruct((TOP_K, n), I32),
                   jax.ShapeDtypeStruct((ne, 1), F32)),
        grid=(n // tr,),
        in_specs=[pl.BlockSpec((ne, tr), lambda i: (0, i)),
                  pl.BlockSpec((ne, 1), lambda i: (0, 0)),
                  pl.BlockSpec((tr, tr), lambda i: (0, 0))],
        out_specs=(pl.BlockSpec((TOP_K, tr), lambda i: (0, i)),
                   pl.BlockSpec((TOP_K, tr), lambda i: (0, i)),
                   pl.BlockSpec((TOP_K, tr), lambda i: (0, i)),
                   pl.BlockSpec((ne, 1), lambda i: (0, 0))),
        scratch_shapes=[pltpu.VMEM((ne, 1), F32)],
        compiler_params=pltpu.CompilerParams(
            dimension_semantics=("arbitrary",), vmem_limit_bytes=VMEM_LIMIT),
        name="route",
    )(lt, bias.reshape(ne, 1), su)


def _pos_kernel(idx_ref, rank_ref, st_ref, pos_ref):
    ne = st_ref.shape[0]
    tp = idx_ref.shape[1]
    eio = lax.broadcasted_iota(I32, (ne, tp), 0)
    st = st_ref[...]
    for k in range(TOP_K):
        base = jnp.sum(jnp.where(eio == idx_ref[k:k + 1, :], st, 0), axis=0, keepdims=True)
        pos_ref[k:k + 1, :] = base + rank_ref[k:k + 1, :]


def _pos_call(top_idx, rank, starts):
    kk, n = top_idx.shape
    ne = starts.shape[0]
    tp = POS_TILE
    return pl.pallas_call(
        _pos_kernel,
        out_shape=jax.ShapeDtypeStruct((kk, n), I32),
        grid=(n // tp,),
        in_specs=[pl.BlockSpec((kk, tp), lambda i: (0, i)),
                  pl.BlockSpec((kk, tp), lambda i: (0, i)),
                  pl.BlockSpec((ne, 1), lambda i: (0, 0))],
        out_specs=pl.BlockSpec((kk, tp), lambda i: (0, i)),
        compiler_params=pltpu.CompilerParams(
            dimension_semantics=("arbitrary",), vmem_limit_bytes=VMEM_LIMIT),
        name="pos",
    )(top_idx, rank, starts.reshape(ne, 1))


def _sc_mesh():
    return plsc.VectorSubcoreMesh(core_axis_name="c", subcore_axis_name="s",
                                  num_cores=SC_CORES, num_subcores=SC_SUBCORES)


def _sc_worker():
    return lax.axis_index("s") * SC_CORES + lax.axis_index("c")


def _dispatch_call(pos_win, hp):
    n = hp.shape[0]
    n_workers = SC_CORES * SC_SUBCORES
    wins_per_worker = n // SC_WINDOW // n_workers

    def body(hp_hbm, pos_hbm, xs_hbm, idx_v, rows_v, sem):
        first_win = _sc_worker() * wins_per_worker

        @pl.loop(0, wins_per_worker)
        def _(w):
            win = first_win + w
            pltpu.sync_copy(hp_hbm.at[pl.ds(win * SC_WINDOW, SC_WINDOW)], rows_v)
            pltpu.sync_copy(pos_hbm.at[win], idx_v)
            copies = [pltpu.async_copy(rows_v, xs_hbm.at[idx_v.at[k]], sem) for k in range(TOP_K)]
            for cp in copies:
                cp.wait()

    return pl.kernel(
        body,
        out_type=jax.ShapeDtypeStruct((n * TOP_K,) + hp.shape[1:], hp.dtype),
        mesh=_sc_mesh(),
        scratch_types=[pltpu.VMEM((TOP_K, SC_WINDOW), I32),
                       pltpu.VMEM((SC_WINDOW,) + hp.shape[1:], hp.dtype),
                       pltpu.SemaphoreType.DMA],
        name="dispatch",
    )(hp, pos_win)


def _collect_call(pos_win, y_sorted):
    n = pos_win.shape[0] * SC_WINDOW
    n_workers = SC_CORES * SC_SUBCORES
    wins_per_worker = n // SC_WINDOW // n_workers

    def body(ys_hbm, pos_hbm, out_hbm, idx_v, rows_a, rows_b, gather_sem, write_sem):
        first_win = _sc_worker() * wins_per_worker
        bufs = (rows_a, rows_b)

        @pl.loop(0, wins_per_worker)
        def _(w):
            win = first_win + w
            pltpu.sync_copy(pos_hbm.at[win], idx_v)

            def gather(k):
                return pltpu.async_copy(ys_hbm.at[idx_v.at[k]], bufs[k % 2], gather_sem)

            pending_gather = gather(0)
            pending_write = None
            for k in range(TOP_K):
                pending_gather.wait()
                if pending_write is not None:
                    for cp in pending_write:
                        cp.wait()
                if k + 1 < TOP_K:
                    pending_gather = gather(k + 1)
                pending_write = [
                    pltpu.async_copy(bufs[k % 2].at[:, j],
                                     out_hbm.at[j, k, pl.ds(win * SC_WINDOW, SC_WINDOW)], write_sem)
                    for j in range(PACK_ROWS)]
            for cp in pending_write:
                cp.wait()

    row_buf = pltpu.VMEM((SC_WINDOW,) + y_sorted.shape[1:], y_sorted.dtype)
    return pl.kernel(
        body,
        out_type=jax.ShapeDtypeStruct((PACK_ROWS, TOP_K, n, LANES), y_sorted.dtype),
        mesh=_sc_mesh(),
        scratch_types=[pltpu.VMEM((TOP_K, SC_WINDOW), I32), row_buf, row_buf,
                       pltpu.SemaphoreType.DMA, pltpu.SemaphoreType.DMA],
        name="collect",
    )(y_sorted, pos_win)


def _experts_kernel(pe_ref, pb_ref, pv_ref, st_ref, en_ref,
                    xs_hbm, wg_ref, wu_ref, wd_ref, y_hbm,
                    wgb_ref, wub_ref, wdb_ref, xbuf_ref, stage_ref,
                    in_sems, out_sems, state_ref):
    p = pl.program_id(0)
    n_pairs = pl.num_programs(0)
    e = pe_ref[p]
    blk = pb_ref[p]
    rb = xbuf_ref.shape[2]
    prev = jnp.maximum(p - 1, 0)
    nxt = jnp.minimum(p + 1, n_pairs - 1)
    live = pv_ref[p] == 1
    first = jnp.logical_or(p == 0, pb_ref[prev] != blk)
    block_ends = jnp.logical_or(pb_ref[nxt] != blk, pv_ref[nxt] == 0)
    last = jnp.logical_or(p == n_pairs - 1, block_ends)

    def in_copy(j, block, slot):
        return pltpu.make_async_copy(xs_hbm.at[pl.ds(block * rb, rb), j], xbuf_ref.at[slot, j],
                                     in_sems.at[slot])

    def out_copy(j, block, slot):
        return pltpu.make_async_copy(stage_ref.at[slot, j], y_hbm.at[pl.ds(block * rb, rb), j],
                                     out_sems.at[slot])

    def drain(slot):
        @pl.when(state_ref[1 + slot] == 1)
        def _():
            for j in range(PACK_ROWS):
                out_copy(j, 0, slot).wait()
            state_ref[1 + slot] = 0

    @pl.when(p == 0)
    def _():
        state_ref[0] = 0
        state_ref[1] = 0
        state_ref[2] = 0
        stage_ref[...] = jnp.zeros_like(stage_ref)
        for j in range(PACK_ROWS):
            in_copy(j, blk, 0).start()

    slot = jnp.where(jnp.logical_and(first, p > 0), 1 - state_ref[0], state_ref[0])
    state_ref[0] = slot

    @pl.when(jnp.logical_and(p < n_pairs - 1,
                             jnp.logical_and(pb_ref[nxt] != blk, pv_ref[nxt] == 1)))
    def _():
        for j in range(PACK_ROWS):
            in_copy(j, pb_ref[nxt], 1 - slot).start()

    @pl.when(first)
    def _():
        for j in range(PACK_ROWS):
            in_copy(j, blk, slot).wait()
        drain(slot)

    @pl.when(jnp.logical_or(p == 0, pe_ref[prev] != e))
    def _():
        wgb_ref[...] = wg_ref[0].astype(BF16)
        wub_ref[...] = wu_ref[0].astype(BF16)
        wdb_ref[...] = wd_ref[0].astype(BF16)

    lo_row = st_ref[e]
    hi_row = en_ref[e]

    def sub_block(s, row0, shared):
        sub = pl.ds(s * SUB_BLOCK, SUB_BLOCK)
        pieces = []
        for j in range(PACK_ROWS):
            w = xbuf_ref[slot, j, sub, :]
            pieces.append(lax.bitcast_convert_type(w.astype(jnp.int16), BF16))
            pieces.append(lax.bitcast_convert_type(
                lax.shift_right_logical(w, 16).astype(jnp.int16), BF16))
        xb = jnp.concatenate(pieces, axis=1)
        gate = jnp.dot(xb, wgb_ref[...], preferred_element_type=F32)
        up = jnp.dot(xb, wub_ref[...], preferred_element_type=F32)
        yb = jnp.dot((_silu(gate) * up).astype(BF16), wdb_ref[...], preferred_element_type=F32)
        if shared:
            rows = row0 + lax.broadcasted_iota(I32, (SUB_BLOCK, 1), 0)
            mine = jnp.logical_and(rows >= lo_row, rows < hi_row)
        for j in range(PACK_ROWS):
            word = pltpu.pack_elementwise(
                [yb[:, 2 * j * LANES:(2 * j + 1) * LANES], yb[:, (2 * j + 1) * LANES:(2 * j + 2) * LANES]],
                packed_dtype=BF16)
            if shared:
                word = jnp.where(mine, word, stage_ref[slot, j, sub, :])
            stage_ref[slot, j, sub, :] = word

    for s in range(rb // SUB_BLOCK):
        row0 = blk * rb + s * SUB_BLOCK
        touched = jnp.logical_and(live, jnp.logical_and(row0 < hi_row, row0 + SUB_BLOCK > lo_row))
        whole = jnp.logical_and(lo_row <= row0, hi_row >= row0 + SUB_BLOCK)

        @pl.when(jnp.logical_and(touched, whole))
        def _():
            sub_block(s, row0, shared=False)

        @pl.when(jnp.logical_and(touched, jnp.logical_not(whole)))
        def _():
            sub_block(s, row0, shared=True)

    @pl.when(jnp.logical_and(live, last))
    def _():
        for j in range(PACK_ROWS):
            out_copy(j, blk, slot).start()
        state_ref[1 + slot] = 1

    @pl.when(p == n_pairs - 1)
    def _():
        drain(0)
        drain(1)


def _experts_call(pair_e, pair_blk, pair_ok, starts, ends, xs, w_gate, w_up, w_down):
    ne, d, de = w_gate.shape
    rb = ROW_BLOCK
    n_pairs = pair_e.shape[0]
    grid_spec = pltpu.PrefetchScalarGridSpec(
        num_scalar_prefetch=5,
        grid=(n_pairs,),
        in_specs=[pl.BlockSpec(memory_space=pl.ANY),
                  pl.BlockSpec((1, d, de), lambda p, pe, pb, pv, st, en: (pe[p], 0, 0)),
                  pl.BlockSpec((1, d, de), lambda p, pe, pb, pv, st, en: (pe[p], 0, 0)),
                  pl.BlockSpec((1, de, d), lambda p, pe, pb, pv, st, en: (pe[p], 0, 0))],
        out_specs=pl.BlockSpec(memory_space=pl.ANY),
        scratch_shapes=[pltpu.VMEM((d, de), BF16), pltpu.VMEM((d, de), BF16),
                        pltpu.VMEM((de, d), BF16),
                        pltpu.VMEM((2, PACK_ROWS, rb, LANES), I32),
                        pltpu.VMEM((2, PACK_ROWS, rb, LANES), I32),
                        pltpu.SemaphoreType.DMA((2,)), pltpu.SemaphoreType.DMA((2,)),
                        pltpu.SMEM((3,), I32)],
    )
    return pl.pallas_call(
        _experts_kernel,
        out_shape=jax.ShapeDtypeStruct(xs.shape, xs.dtype),
        grid_spec=grid_spec,
        compiler_params=pltpu.CompilerParams(
            dimension_semantics=("arbitrary",), vmem_limit_bytes=VMEM_LIMIT),
        name="experts",
    )(pair_e, pair_blk, pair_ok, starts, ends, xs, w_gate, w_up, w_down)


def _combine_kernel(y0_ref, y1_ref, y2_ref, y3_ref, x1_ref, w_ref, g2_ref, fg_ref, o_ref):
    w = w_ref[...]
    parts = []
    for y_ref in (y0_ref, y1_ref, y2_ref, y3_ref):
        for half in range(2):
            acc = None
            for k in range(TOP_K):
                piece = pltpu.unpack_elementwise(y_ref[k], index=half, packed_dtype=BF16,
                                                 unpacked_dtype=F32) * w[:, k:k + 1]
                acc = piece if acc is None else acc + piece
            parts.append(acc)
    x2 = x1_ref[...] + g2_ref[0] * jnp.concatenate(parts, axis=1)
    ms = jnp.mean(x2 * x2, axis=-1, keepdims=True)
    o_ref[...] = x2 * lax.rsqrt(ms + EPS) * fg_ref[...]


def _combine_kernel_into(y0_ref, y1_ref, y2_ref, y3_ref, x1_ref, w_ref, g2_ref, fg_ref, prev_ref, o_ref):
    del prev_ref
    _combine_kernel(y0_ref, y1_ref, y2_ref, y3_ref, x1_ref, w_ref, g2_ref, fg_ref, o_ref)


def _combine_call(y_tok, x1s, w_tok, g2, fgain, seq, part, prev_out):
    n, d = x1s.shape
    tm = MOVE_TILE
    per_b = seq // tm
    tiles = y_tok.shape[2] // tm
    t0 = part * tiles

    def piece_spec(j):
        return pl.BlockSpec((None, TOP_K, tm, LANES), lambda i: (j, 0, i, 0))

    in_specs = [piece_spec(j) for j in range(PACK_ROWS)] + [
        pl.BlockSpec((tm, d), lambda i: (t0 + i, 0)),
        pl.BlockSpec((tm, TOP_K), lambda i: (t0 + i, 0)),
        pl.BlockSpec((1, 1, d), lambda i: ((t0 + i) // per_b, 0, 0)),
        pl.BlockSpec((1, d), lambda i: (0, 0))]
    args = [y_tok, y_tok, y_tok, y_tok, x1s, w_tok, g2, fgain]
    aliases = {}
    body = _combine_kernel
    if prev_out is not None:
        in_specs.append(pl.BlockSpec(memory_space=pl.ANY))
        args.append(prev_out)
        aliases = {len(args) - 1: 0}
        body = _combine_kernel_into
    return pl.pallas_call(
        body,
        out_shape=jax.ShapeDtypeStruct((n, d), F32),
        grid=(tiles,),
        in_specs=in_specs,
        out_specs=pl.BlockSpec((tm, d), lambda i: (t0 + i, 0)),
        input_output_aliases=aliases,
        compiler_params=pltpu.CompilerParams(
            dimension_semantics=("arbitrary",), vmem_limit_bytes=VMEM_LIMIT),
        name="combine",
    )(*args)


def _pair_tables(counts, n_rows):
    ne = counts.shape[0]
    sizes = counts.astype(I32)
    ends = jnp.cumsum(sizes)
    starts = ends - sizes
    first_blk = starts // ROW_BLOCK
    last_blk = (ends - 1) // ROW_BLOCK
    n_pairs = jnp.where(sizes > 0, last_blk - first_blk + 1, 0)
    pair_end = jnp.cumsum(n_pairs)
    pair_start = pair_end - n_pairs
    max_pairs = n_rows // ROW_BLOCK + ne
    p = jnp.arange(max_pairs, dtype=I32)
    ok = p < pair_end[-1]
    pc = jnp.minimum(p, pair_end[-1] - 1)
    pair_e = jnp.minimum(jnp.searchsorted(pair_end, pc, side='right'), ne - 1).astype(I32)
    pair_blk = (first_blk[pair_e] + pc - pair_start[pair_e]).astype(I32)
    return pair_e, pair_blk, ok.astype(I32), starts.astype(I32), ends.astype(I32)


def kernel(x, c, w_ada, b_ada, w_in, lb_logits, hgrn_norm, lam_re, lam_im, log_dt, b_re, b_im,
           c_re, c_im, d_skip, w_glu, b_glu, w_out, w_router, router_bias, w_gate, w_up, w_down,
           ws_gate, ws_up, ws_down, final_gain):
    bsz, seq, d = x.shape
    n = bsz * seq
    fdim = HGRN_HEADS * HGRN_KDIM
    n_chunks = seq // CHUNK
    lb = jnp.cumsum(jax.nn.softmax(lb_logits.astype(F32), axis=0), axis=0)[0].reshape(1, fdim)

    mod = _mod_call(c, w_ada[0], b_ada[0]).reshape(bsz, 6, d)
    nb = w_in.shape[2] - 4 * fdim
    perm = jnp.concatenate([jnp.arange(0, nb, 2), jnp.arange(1, nb, 2)])
    w_in_b = w_in[0].astype(BF16)
    ut, out_a = _mix_front_call(x, mod, w_in_b[:, :4 * fdim], w_in_b[:, 4 * fdim:][:, perm].T, lb,
                                hgrn_norm[0].reshape(1, fdim))

    m_tab, p_tab, q_tab, a1, a2 = _s5_tables(lam_re[0], lam_im[0], log_dt[0], b_re[0], b_im[0],
                                             c_re[0], c_im[0])
    yt = _s5_call(ut, m_tab, p_tab, q_tab, a1, a2, n_chunks, bsz)

    wr_t = w_router[0].T
    wr_hi = wr_t.astype(BF16)
    wr_lo = (wr_t - wr_hi.astype(F32)).astype(BF16)
    w_out_b = w_out[0].astype(BF16)
    w_out_p = jnp.concatenate([w_out_b[:fdim], w_out_b[fdim:][perm]], axis=0)
    x1s, hp, logits_t = _mix_back_call(
        x.reshape(n, d), out_a.reshape(n, fdim), yt, ut, mod,
        d_skip[0][perm].reshape(nb, 1), w_glu[0][perm][:, perm].T.astype(BF16),
        b_glu[0][perm].reshape(nb, 1), w_out_p, wr_hi, wr_lo, ws_gate[0].astype(BF16),
        ws_up[0].astype(BF16), ws_down[0].astype(BF16), seq)

    top_idx, top_w, rank, counts = _route_call(logits_t, router_bias[0])
    pair_e, pair_blk, pair_ok, starts, ends = _pair_tables(counts[:, 0], n * TOP_K)
    pos = _pos_call(top_idx, rank, starts)

    pos_win = pos.reshape(TOP_K, n // SC_WINDOW, SC_WINDOW).transpose(1, 0, 2)
    xs = _dispatch_call(pos_win, hp.reshape(n, PACK_ROWS, LANES))
    y_sorted = _experts_call(pair_e, pair_blk, pair_ok, starts, ends, xs, w_gate[0], w_up[0], w_down[0])
    w_tok = top_w.T
    g2 = mod[:, 5:6, :]
    fgain = final_gain.reshape(1, d)
    wins = pos_win.shape[0] // TAIL_PARTS
    out = None
    for part in range(TAIL_PARTS):
        y_tok = _collect_call(pos_win[part * wins:(part + 1) * wins], y_sorted)
        out = _combine_call(y_tok, x1s, w_tok, g2, fgain, seq, part, out)
    return out.reshape(bsz, seq, d)
```

```python
import functools

import jax
import jax.numpy as jnp
from jax import lax
from jax.experimental import pallas as pl
from jax.experimental.pallas import tpu as pltpu
from jax.experimental.pallas import tpu_sc as plsc

F32 = jnp.float32
BF16 = jnp.bfloat16
I32 = jnp.int32

EPS = 1e-6
CHUNK = 64
HGRN_HEADS = 4
HGRN_KDIM = 128
S5_GROUP = 16
S5_STATE = 64
N_EXPERT_GROUPS = 8
TOPK_GROUPS = 4
TOP_K = 8
ROUTE_SCALE = 2.5
LANES = 128
PACK_ROWS = 4
SC_CORES = 2
SC_SUBCORES = 16
SC_WINDOW = 64

SEQ_TILE = 512
TOK_TILE = 512
ROUTE_TILE = 512
POS_TILE = 1024
MOVE_TILE = 256
ROW_BLOCK = 2048
SUB_BLOCK = 512
TAIL_PARTS = 4
VMEM_LIMIT = 56 * 1024 * 1024

_NT = (((1,), (1,)), ((), ()))
_TN = (((0,), (0,)), ((), ()))


def _sigmoid(v):
    return 1.0 / (1.0 + jnp.exp(-v))


def _silu(v):
    return v * _sigmoid(v)


def _bdot(a, b):
    return jnp.dot(a.astype(BF16), b.astype(BF16), preferred_element_type=F32)


def _store_packed(ref, val, n_rows):
    for j in range(PACK_ROWS):
        lo = val[:, 2 * j * LANES:(2 * j + 1) * LANES]
        hi = val[:, (2 * j + 1) * LANES:(2 * j + 2) * LANES]
        ref[pl.ds(j, n_rows, stride=PACK_ROWS), :] = pltpu.pack_elementwise([lo, hi], packed_dtype=BF16)


def _mod_kernel(c_ref, w_ref, b_ref, o_ref):
    o_ref[...] = _bdot(_silu(c_ref[...]), w_ref[...]) + b_ref[...]


def _mod_call(c, w_ada, b_ada):
    bsz, d = c.shape
    n_out = w_ada.shape[1]
    return pl.pallas_call(
        _mod_kernel,
        out_shape=jax.ShapeDtypeStruct((bsz, n_out), F32),
        grid=(n_out // d,),
        in_specs=[pl.BlockSpec((bsz, d), lambda j: (0, 0)),
                  pl.BlockSpec((d, d), lambda j: (0, j)),
                  pl.BlockSpec((1, d), lambda j: (0, j))],
        out_specs=pl.BlockSpec((bsz, d), lambda j: (0, j)),
        compiler_params=pltpu.CompilerParams(vmem_limit_bytes=VMEM_LIMIT),
        name="mod",
    )(c, w_ada, b_ada.reshape(1, n_out))


def _split_chunk_pairs(tile_even, tile_odd):
    low = lax.broadcasted_iota(I32, tile_even.shape, 1) < CHUNK
    first = jnp.where(low, tile_even, pltpu.roll(tile_odd, CHUNK, 1))
    second = jnp.where(low, pltpu.roll(tile_even, CHUNK, 1), tile_odd)
    return first, second


def _mix_front_kernel(x_ref, mod_ref, win_ref, wut_ref, lb_ref, gn_ref, ltri_ref,
                      ut_ref, oa_ref, proj_ref, st_ref, flat_ref):
    fdim = HGRN_HEADS * HGRN_KDIM
    ts = x_ref.shape[1]
    pairs = wut_ref.shape[0] // 2
    rows = ts // CHUNK

    @pl.when(pl.program_id(1) == 0)
    def _():
        st_ref[...] = jnp.zeros_like(st_ref)

    x = x_ref[0]
    ms = jnp.mean(x * x, axis=-1, keepdims=True)
    h = x * lax.rsqrt(ms + EPS) * (1.0 + mod_ref[0, 1:2, :]) + mod_ref[0, 0:1, :]
    hb = h.astype(BF16)
    proj_ref[...] = jnp.dot(hb, win_ref[...], preferred_element_type=F32)
    u_t = lax.dot_general(wut_ref[...], hb, _NT, preferred_element_type=F32)
    for m in range(ts // LANES):
        first, second = _split_chunk_pairs(u_t[:pairs, m * LANES:(m + 1) * LANES],
                                           u_t[pairs:, m * LANES:(m + 1) * LANES])
        flat_ref[2 * m * pairs:(2 * m + 1) * pairs, :] = first
        flat_ref[(2 * m + 1) * pairs:(2 * m + 2) * pairs, :] = second
    per_group = S5_GROUP // 2
    for q in range(pairs):
        ut_ref[q // per_group, :, (q % per_group) * LANES:(q % per_group + 1) * LANES] = (
            flat_ref[pl.ds(q, rows, stride=pairs), :])

    lb = lb_ref[...]
    gn = gn_ref[...]
    ltri = ltri_ref[...]
    row = lax.broadcasted_iota(I32, (CHUNK, CHUNK), 0)
    col = lax.broadcasted_iota(I32, (CHUNK, CHUNK), 1)
    causal = row >= col

    def chunk_step(ci, carry):
        r0 = pl.multiple_of(ci * CHUNK, CHUNK)
        q = proj_ref[pl.ds(r0, CHUNK), 0:fdim]
        fl = proj_ref[pl.ds(r0, CHUNK), fdim:2 * fdim]
        iv = proj_ref[pl.ds(r0, CHUNK), 2 * fdim:3 * fdim]
        og = proj_ref[pl.ds(r0, CHUNK), 3 * fdim:4 * fdim]
        f = lb + (1.0 - lb) * _sigmoid(fl)
        b = jnp.dot(ltri, jnp.log(f), precision=lax.Precision.HIGHEST,
                    preferred_element_type=F32)
        b_ref = b[CHUNK // 2 - 1:CHUNK // 2, :]
        b_last = b[CHUNK - 1:CHUNK, :]
        qs = _silu(q)
        kk = 1.0 - f
        qe = (qs * jnp.exp(b - b_ref)).astype(BF16)
        ke = (kk * jnp.exp(b_ref - b)).astype(BF16)
        qb = (qs * jnp.exp(b)).astype(BF16)
        k2 = (kk * jnp.exp(b_last - b)).astype(BF16)
        dec = jnp.exp(b_last)
        ivb = iv.astype(BF16)
        outs = []
        for hh in range(HGRN_HEADS):
            sl = slice(hh * HGRN_KDIM, (hh + 1) * HGRN_KDIM)
            att = lax.dot_general(qe[:, sl], ke[:, sl], _NT, preferred_element_type=F32)
            att = jnp.where(causal, att, 0.0)
            st = st_ref[hh]
            o = jnp.dot(att.astype(BF16), ivb[:, sl], preferred_element_type=F32)
            o = o + lax.dot_general(qb[:, sl], st.astype(BF16), _NT, preferred_element_type=F32)
            st_ref[hh] = st * dec[:, sl] + lax.dot_general(
                ivb[:, sl], k2[:, sl], _TN, preferred_element_type=F32)
            outs.append(o * lax.rsqrt(jnp.mean(o * o, axis=-1, keepdims=True) + EPS))
        o = jnp.concatenate(outs, axis=1) * gn * _silu(og)
        oa_ref[0, pl.ds(r0, CHUNK), :] = o.astype(BF16)
        return carry

    lax.fori_loop(0, ts // CHUNK, chunk_step, 0, unroll=True)


def _mix_front_call(x, mod, w_main, w_ut, lb, gn):
    bsz, seq, d = x.shape
    fdim = HGRN_HEADS * HGRN_KDIM
    ncols = w_main.shape[1]
    nb = w_ut.shape[0]
    groups = nb // S5_GROUP
    ltri = jnp.tril(jnp.ones((CHUNK, CHUNK), F32))
    ts = SEQ_TILE
    tiles = seq // ts
    rows = ts // CHUNK
    return pl.pallas_call(
        _mix_front_kernel,
        out_shape=(jax.ShapeDtypeStruct((groups, bsz * seq // CHUNK, S5_GROUP * CHUNK), F32),
                   jax.ShapeDtypeStruct((bsz, seq, fdim), BF16)),
        grid=(bsz, tiles),
        in_specs=[pl.BlockSpec((1, ts, d), lambda b, j: (b, j, 0)),
                  pl.BlockSpec((1, 6, d), lambda b, j: (b, 0, 0)),
                  pl.BlockSpec((d, ncols), lambda b, j: (0, 0)),
                  pl.BlockSpec((nb, d), lambda b, j: (0, 0)),
                  pl.BlockSpec((1, fdim), lambda b, j: (0, 0)),
                  pl.BlockSpec((1, fdim), lambda b, j: (0, 0)),
                  pl.BlockSpec((CHUNK, CHUNK), lambda b, j: (0, 0))],
        out_specs=(pl.BlockSpec((groups, rows, S5_GROUP * CHUNK), lambda b, j: (0, b * tiles + j, 0)),
                   pl.BlockSpec((1, ts, fdim), lambda b, j: (b, j, 0))),
        scratch_shapes=[pltpu.VMEM((ts, ncols), F32),
                        pltpu.VMEM((HGRN_HEADS, fdim // HGRN_HEADS, HGRN_KDIM), F32),
                        pltpu.VMEM((rows * nb // 2, LANES), F32)],
        compiler_params=pltpu.CompilerParams(
            dimension_semantics=("arbitrary", "arbitrary"), vmem_limit_bytes=VMEM_LIMIT),
        name="mix_front",
    )(x, mod, w_main, w_ut, lb, gn, ltri)


def _s5_tables(lam_re, lam_im, log_dt, b_re, b_im, c_re, c_im):
    t = CHUNK
    hp = lax.Precision.HIGHEST
    lam = lax.complex(jnp.minimum(lam_re, -1e-4), lam_im)
    lam_dt = lam * jnp.exp(log_dt)[:, None]
    lam_bar = jnp.exp(lam_dt)
    b_bar = ((lam_bar - 1.0) / lam)[..., None] * lax.complex(b_re, b_im)
    c_mat = lax.complex(c_re, c_im)
    taus = jnp.arange(t + 1, dtype=F32)
    lam_pow = jnp.exp(lam_dt[:, None, :] * taus[None, :, None])
    g, p = lam.shape
    c = b_re.shape[-1]
    cl = c_mat[:, None, :, :] * lam_pow[:, :t, None, :]
    kr = (jnp.einsum('gtcp,gpi->gtci', cl.real, b_bar.real, precision=hp)
          - jnp.einsum('gtcp,gpi->gtci', cl.imag, b_bar.imag, precision=hp))
    z = jnp.pad(kr.transpose(0, 3, 2, 1).astype(BF16), ((0, 0), (0, 0), (0, 0), (t - 1, 0)))
    kg = jnp.stack([z[..., t - 1 - s:2 * t - 1 - s] for s in range(t)], axis=2)
    m = kg.reshape(g, c * t, c * t)
    pc = lam_pow[:, t - 1::-1][:, :t, :, None] * b_bar[:, None, :, :]
    pc = pc.transpose(0, 3, 1, 2).reshape(g, c * t, p)
    p_tab = jnp.concatenate([pc.real, pc.imag], axis=-1)
    ql = c_mat[:, None, :, :] * lam_pow[:, 1:t + 1, None, :]
    ql = ql.transpose(0, 3, 2, 1).reshape(g, p, c * t)
    q_tab = jnp.concatenate([ql.real, -ql.imag], axis=1)
    lam_t = lam_pow[:, t]
    a1 = jnp.concatenate([lam_t.real, lam_t.real], axis=-1)[:, None, :]
    a2 = jnp.concatenate([-lam_t.imag, lam_t.imag], axis=-1)[:, None, :]
    return m.astype(BF16), p_tab.astype(BF16), q_tab.astype(BF16), a1, a2


def _s5_kernel(u_ref, m_ref, p_ref, q_ref, a1_ref, a2_ref, y_ref, v_ref, xs_ref, *, n_chunks, n_batch):
    u = u_ref[0].astype(BF16)
    v_ref[...] = jnp.dot(u, p_ref[0], preferred_element_type=F32)
    a1 = a1_ref[0]
    a2 = a2_ref[0]
    half = xs_ref.shape[1] // 2

    def step(n, state):
        xs_ref[pl.ds(n, n_batch, stride=n_chunks), :] = state
        return (a1 * state + a2 * pltpu.roll(state, half, 1)
                + v_ref[pl.ds(n, n_batch, stride=n_chunks), :])

    lax.fori_loop(0, n_chunks, step, jnp.zeros((n_batch, xs_ref.shape[1]), F32))
    y = jnp.dot(u, m_ref[0], preferred_element_type=F32)
    y_ref[0] = y + jnp.dot(xs_ref[...].astype(BF16), q_ref[0], preferred_element_type=F32)


def _s5_call(ut, m, p_tab, q_tab, a1, a2, n_chunks, n_batch):
    g, rows, width = ut.shape
    p2 = p_tab.shape[-1]
    return pl.pallas_call(
        functools.partial(_s5_kernel, n_chunks=n_chunks, n_batch=n_batch),
        out_shape=jax.ShapeDtypeStruct((g, rows, width), F32),
        grid=(g,),
        in_specs=[pl.BlockSpec((1, rows, width), lambda i: (i, 0, 0)),
                  pl.BlockSpec((1, width, width), lambda i: (i, 0, 0)),
                  pl.BlockSpec((1, width, p2), lambda i: (i, 0, 0)),
                  pl.BlockSpec((1, p2, width), lambda i: (i, 0, 0)),
                  pl.BlockSpec((1, 1, p2), lambda i: (i, 0, 0)),
                  pl.BlockSpec((1, 1, p2), lambda i: (i, 0, 0))],
        out_specs=pl.BlockSpec((1, rows, width), lambda i: (i, 0, 0)),
        scratch_shapes=[pltpu.VMEM((rows, p2), F32), pltpu.VMEM((rows, p2), F32)],
        compiler_params=pltpu.CompilerParams(
            dimension_semantics=("arbitrary",), vmem_limit_bytes=VMEM_LIMIT),
        name="s5",
    )(ut, m, p_tab, q_tab, a1, a2)


def _token_major(flat_ref, src_ref):
    groups, rows, _ = src_ref.shape
    per_group = S5_GROUP // 2
    pairs = groups * per_group
    for q in range(pairs):
        flat_ref[pl.ds(q, rows, stride=pairs), :] = (
            src_ref[q // per_group, :, (q % per_group) * LANES:(q % per_group + 1) * LANES])
    tiles = []
    for m in range(rows // 2):
        even, odd = _split_chunk_pairs(flat_ref[2 * m * pairs:(2 * m + 1) * pairs, :],
                                       flat_ref[(2 * m + 1) * pairs:(2 * m + 2) * pairs, :])
        tiles.append(jnp.concatenate([even, odd], axis=0))
    return jnp.concatenate(tiles, axis=1)


def _mix_back_kernel(x_ref, oa_ref, yt_ref, ut_ref, mod_ref, dskip_ref, wglu_ref, bglu_ref,
                     wout_ref, wrh_ref, wrl_ref,
                     x1_ref, hp_ref, lt_ref, flat_ref):
    na = oa_ref.shape[1]
    y_t = _token_major(flat_ref, yt_ref)
    u_t = _token_major(flat_ref, ut_ref)
    z_t = jax.nn.gelu(y_t + dskip_ref[...] * u_t)
    gate_t = _sigmoid(jnp.dot(wglu_ref[...], z_t.astype(BF16), preferred_element_type=F32)
                      + bglu_ref[...])
    ob_t = (z_t * gate_t).astype(BF16)
    mixed = (jnp.dot(oa_ref[...], wout_ref[0:na, :], preferred_element_type=F32)
             + lax.dot_general(ob_t, wout_ref[na:, :], _TN, preferred_element_type=F32))
    x1 = x_ref[...] + mod_ref[0, 2:3, :] * mixed
    ms = jnp.mean(x1 * x1, axis=-1, keepdims=True)
    h2 = x1 * lax.rsqrt(ms + EPS) * (1.0 + mod_ref[0, 4:5, :]) + mod_ref[0, 3:4, :]
    _store_packed(hp_ref, h2, h2.shape[0])
    h_hi = h2.astype(BF16)
    h_lo = (h2 - h_hi.astype(F32)).astype(BF16)
    lt = lax.dot_general(wrh_ref[...], h_hi, _NT, preferred_element_type=F32)
    lt = lt + lax.dot_general(wrl_ref[...], h_hi, _NT, preferred_element_type=F32)
    lt = lt + lax.dot_general(wrh_ref[...], h_lo, _NT, preferred_element_type=F32)
    lt_ref[...] = lt
    x1_ref[...] = x1


def _mix_back_call(x2d, oa, yt, ut, mod, dskip, wglu, bglu, wout, wrh, wrl, seq):
    n, d = x2d.shape
    nb = oa.shape[1]
    ne = wrh.shape[0]
    tm = TOK_TILE
    per_b = seq // tm
    groups, _, width = yt.shape
    rows = tm // CHUNK
    const = lambda i: (0, 0)
    flat_block = pl.BlockSpec((groups, rows, width), lambda i: (0, i, 0))
    return pl.pallas_call(
        _mix_back_kernel,
        out_shape=(jax.ShapeDtypeStruct((n, d), F32),
                   jax.ShapeDtypeStruct((n * PACK_ROWS, LANES), I32),
                   jax.ShapeDtypeStruct((ne, n), F32)),
        grid=(n // tm,),
        in_specs=[pl.BlockSpec((tm, d), lambda i: (i, 0)),
                  pl.BlockSpec((tm, nb), lambda i: (i, 0)),
                  flat_block,
                  flat_block,
                  pl.BlockSpec((1, 6, d), lambda i: (i // per_b, 0, 0)),
                  pl.BlockSpec((nb, 1), const),
                  pl.BlockSpec((nb, nb), const),
                  pl.BlockSpec((nb, 1), const),
                  pl.BlockSpec((d, d), const),
                  pl.BlockSpec((ne, d), const),
                  pl.BlockSpec((ne, d), const)],
        out_specs=(pl.BlockSpec((tm, d), lambda i: (i, 0)),
                   pl.BlockSpec((tm * PACK_ROWS, LANES), lambda i: (i, 0)),
                   pl.BlockSpec((ne, tm), lambda i: (0, i))),
        scratch_shapes=[pltpu.VMEM((rows * nb // 2, LANES), F32)],
        compiler_params=pltpu.CompilerParams(
            dimension_semantics=("arbitrary",), vmem_limit_bytes=VMEM_LIMIT),
        name="mix_back",
    )(x2d, oa, yt, ut, mod, dskip, wglu, bglu, wout, wrh, wrl)


def _route_kernel(lt_ref, bias_ref, su_ref, idx_ref, w_ref, rank_ref, cnt_ref, run_ref):
    ne, tr = lt_ref.shape
    per_group = ne // N_EXPERT_GROUPS
    neg = -jnp.inf

    @pl.when(pl.program_id(0) == 0)
    def _():
        run_ref[...] = jnp.zeros_like(run_ref)

    s = _sigmoid(lt_ref[...])
    sel = s + bias_ref[...]
    gio = lax.broadcasted_iota(I32, (per_group, tr), 0)
    gscore = []
    for g in range(N_EXPERT_GROUPS):
        v = sel[g * per_group:(g + 1) * per_group, :]
        m1 = jnp.max(v, axis=0, keepdims=True)
        i1 = jnp.min(jnp.where(v == m1, gio, per_group), axis=0, keepdims=True)
        m2 = jnp.max(jnp.where(gio == i1, neg, v), axis=0, keepdims=True)
        gscore.append(m1 + m2)
    masked = []
    for g in range(N_EXPERT_GROUPS):
        ahead = jnp.zeros((1, tr), I32)
        for o in range(N_EXPERT_GROUPS):
            if o == g:
                continue
            wins = (gscore[o] >= gscore[g]) if o < g else (gscore[o] > gscore[g])
            ahead = ahead + wins.astype(I32)
        keep = ahead < TOPK_GROUPS
        masked.append(jnp.where(keep, sel[g * per_group:(g + 1) * per_group, :], neg))
    selm = jnp.concatenate(masked, axis=0)
    eio = lax.broadcasted_iota(I32, (ne, tr), 0)
    candidate = selm > neg
    idxs, ws = [], []
    for k in range(TOP_K):
        m = jnp.max(selm, axis=0, keepdims=True)
        ik = jnp.min(jnp.where(selm == m, eio, ne), axis=0, keepdims=True)
        onehot = eio == ik
        ws.append(jnp.sum(jnp.where(onehot, s, 0.0), axis=0, keepdims=True))
        selm = jnp.where(onehot, neg, selm)
        idxs.append(ik)
    hits = jnp.where(jnp.logical_and(candidate, selm == neg), 1.0, 0.0)
    wsum = ws[0]
    for k in range(1, TOP_K):
        wsum = wsum + ws[k]
    scale = ROUTE_SCALE / wsum
    ranks = jnp.dot(hits.astype(BF16), su_ref[...], preferred_element_type=F32) + run_ref[...]
    for k in range(TOP_K):
        idx_ref[k:k + 1, :] = idxs[k]
        w_ref[k:k + 1, :] = ws[k] * scale
        rk = jnp.sum(jnp.where(eio == idxs[k], ranks, 0.0), axis=0, keepdims=True)
        rank_ref[k:k + 1, :] = rk.astype(I32)
    run_ref[...] = run_ref[...] + jnp.sum(hits, axis=1, keepdims=True)
    cnt_ref[...] = run_ref[...]


def _route_call(lt, bias):
    ne, n = lt.shape
    tr = ROUTE_TILE
    su = jnp.triu(jnp.ones((tr, tr), F32), k=1).astype(BF16)
    return pl.pallas_call(
        _route_kernel,
        out_shape=(jax.ShapeDtypeStruct((TOP_K, n), I32),
                   jax.ShapeDtypeStruct((TOP_K, n), F32),
                   jax.ShapeDtypeStruct((TOP_K, n), I32),
                   jax.ShapeDtypeStruct((ne, 1), F32)),
        grid=(n // tr,),
        in_specs=[pl.BlockSpec((ne, tr), lambda i: (0, i)),
                  pl.BlockSpec((ne, 1), lambda i: (0, 0)),
                  pl.BlockSpec((tr, tr), lambda i: (0, 0))],
        out_specs=(pl.BlockSpec((TOP_K, tr), lambda i: (0, i)),
                   pl.BlockSpec((TOP_K, tr), lambda i: (0, i)),
                   pl.BlockSpec((TOP_K, tr), lambda i: (0, i)),
                   pl.BlockSpec((ne, 1), lambda i: (0, 0))),
        scratch_shapes=[pltpu.VMEM((ne, 1), F32)],
        compiler_params=pltpu.CompilerParams(
            dimension_semantics=("arbitrary",), vmem_limit_bytes=VMEM_LIMIT),
        name="route",
    )(lt, bias.reshape(ne, 1), su)


def _pos_kernel(idx_ref, rank_ref, st_ref, pos_ref):
    ne = st_ref.shape[0]
    tp = idx_ref.shape[1]
    eio = lax.broadcasted_iota(I32, (ne, tp), 0)
    st = st_ref[...]
    for k in range(TOP_K):
        base = jnp.sum(jnp.where(eio == idx_ref[k:k + 1, :], st, 0), axis=0, keepdims=True)
        pos_ref[k:k + 1, :] = base + rank_ref[k:k + 1, :]


def _pos_call(top_idx, rank, starts):
    kk, n = top_idx.shape
    ne = starts.shape[0]
    tp = POS_TILE
    return pl.pallas_call(
        _pos_kernel,
        out_shape=jax.ShapeDtypeStruct((kk, n), I32),
        grid=(n // tp,),
        in_specs=[pl.BlockSpec((kk, tp), lambda i: (0, i)),
                  pl.BlockSpec((kk, tp), lambda i: (0, i)),
                  pl.BlockSpec((ne, 1), lambda i: (0, 0))],
        out_specs=pl.BlockSpec((kk, tp), lambda i: (0, i)),
        compiler_params=pltpu.CompilerParams(
            dimension_semantics=("arbitrary",), vmem_limit_bytes=VMEM_LIMIT),
        name="pos",
    )(top_idx, rank, starts.reshape(ne, 1))


def _sc_mesh():
    return plsc.VectorSubcoreMesh(core_axis_name="c", subcore_axis_name="s",
                                  num_cores=SC_CORES, num_subcores=SC_SUBCORES)


def _sc_worker():
    return lax.axis_index("s") * SC_CORES + lax.axis_index("c")


def _dispatch_call(pos_win, hp):
    n = hp.shape[0]
    n_workers = SC_CORES * SC_SUBCORES
    wins_per_worker = n // SC_WINDOW // n_workers

    def body(hp_hbm, pos_hbm, xs_hbm, idx_v, rows_v, sem):
        first_win = _sc_worker() * wins_per_worker

        @pl.loop(0, wins_per_worker)
        def _(w):
            win = first_win + w
            pltpu.sync_copy(hp_hbm.at[pl.ds(win * SC_WINDOW, SC_WINDOW)], rows_v)
            pltpu.sync_copy(pos_hbm.at[win], idx_v)
            copies = [pltpu.async_copy(rows_v, xs_hbm.at[idx_v.at[k]], sem) for k in range(TOP_K)]
            for cp in copies:
                cp.wait()

    return pl.kernel(
        body,
        out_type=jax.ShapeDtypeStruct((n * TOP_K,) + hp.shape[1:], hp.dtype),
        mesh=_sc_mesh(),
        scratch_types=[pltpu.VMEM((TOP_K, SC_WINDOW), I32),
                       pltpu.VMEM((SC_WINDOW,) + hp.shape[1:], hp.dtype),
                       pltpu.SemaphoreType.DMA],
        name="dispatch",
    )(hp, pos_win)


def _collect_call(pos_win, y_sorted):
    n = pos_win.shape[0] * SC_WINDOW
    n_workers = SC_CORES * SC_SUBCORES
    wins_per_worker = n // SC_WINDOW // n_workers

    def body(ys_hbm, pos_hbm, out_hbm, idx_v, rows_a, rows_b, gather_sem, write_sem):
        first_win = _sc_worker() * wins_per_worker
        bufs = (rows_a, rows_b)

        @pl.loop(0, wins_per_worker)
        def _(w):
            win = first_win + w
            pltpu.sync_copy(pos_hbm.at[win], idx_v)

            def gather(k):
                return pltpu.async_copy(ys_hbm.at[idx_v.at[k]], bufs[k % 2], gather_sem)

            pending_gather = gather(0)
            pending_write = None
            for k in range(TOP_K):
                pending_gather.wait()
                if pending_write is not None:
                    for cp in pending_write:
                        cp.wait()
                if k + 1 < TOP_K:
                    pending_gather = gather(k + 1)
                pending_write = [
                    pltpu.async_copy(bufs[k % 2].at[:, j],
                                     out_hbm.at[j, k, pl.ds(win * SC_WINDOW, SC_WINDOW)], write_sem)
                    for j in range(PACK_ROWS)]
            for cp in pending_write:
                cp.wait()

    row_buf = pltpu.VMEM((SC_WINDOW,) + y_sorted.shape[1:], y_sorted.dtype)
    return pl.kernel(
        body,
        out_type=jax.ShapeDtypeStruct((PACK_ROWS, TOP_K, n, LANES), y_sorted.dtype),
        mesh=_sc_mesh(),
        scratch_types=[pltpu.VMEM((TOP_K, SC_WINDOW), I32), row_buf, row_buf,
                       pltpu.SemaphoreType.DMA, pltpu.SemaphoreType.DMA],
        name="collect",
    )(y_sorted, pos_win)


def _experts_kernel(pe_ref, pb_ref, pv_ref, st_ref, en_ref,
                    xs_hbm, wg_ref, wu_ref, wd_ref, y_hbm,
                    wgb_ref, wub_ref, wdb_ref, xbuf_ref, stage_ref,
                    in_sems, out_sems, state_ref):
    p = pl.program_id(0)
    n_pairs = pl.num_programs(0)
    e = pe_ref[p]
    blk = pb_ref[p]
    rb = xbuf_ref.shape[2]
    prev = jnp.maximum(p - 1, 0)
    nxt = jnp.minimum(p + 1, n_pairs - 1)
    live = pv_ref[p] == 1
    first = jnp.logical_or(p == 0, pb_ref[prev] != blk)
    block_ends = jnp.logical_or(pb_ref[nxt] != blk, pv_ref[nxt] == 0)
    last = jnp.logical_or(p == n_pairs - 1, block_ends)

    def in_copy(j, block, slot):
        return pltpu.make_async_copy(xs_hbm.at[pl.ds(block * rb, rb), j], xbuf_ref.at[slot, j],
                                     in_sems.at[slot])

    def out_copy(j, block, slot):
        return pltpu.make_async_copy(stage_ref.at[slot, j], y_hbm.at[pl.ds(block * rb, rb), j],
                                     out_sems.at[slot])

    def drain(slot):
        @pl.when(state_ref[1 + slot] == 1)
        def _():
            for j in range(PACK_ROWS):
                out_copy(j, 0, slot).wait()
            state_ref[1 + slot] = 0

    @pl.when(p == 0)
    def _():
        state_ref[0] = 0
        state_ref[1] = 0
        state_ref[2] = 0
        stage_ref[...] = jnp.zeros_like(stage_ref)
        for j in range(PACK_ROWS):
            in_copy(j, blk, 0).start()

    slot = jnp.where(jnp.logical_and(first, p > 0), 1 - state_ref[0], state_ref[0])
    state_ref[0] = slot

    @pl.when(jnp.logical_and(p < n_pairs - 1,
                             jnp.logical_and(pb_ref[nxt] != blk, pv_ref[nxt] == 1)))
    def _():
        for j in range(PACK_ROWS):
            in_copy(j, pb_ref[nxt], 1 - slot).start()

    @pl.when(first)
    def _():
        for j in range(PACK_ROWS):
            in_copy(j, blk, slot).wait()
        drain(slot)

    @pl.when(jnp.logical_or(p == 0, pe_ref[prev] != e))
    def _():
        wgb_ref[...] = wg_ref[0].astype(BF16)
        wub_ref[...] = wu_ref[0].astype(BF16)
        wdb_ref[...] = wd_ref[0].astype(BF16)

    lo_row = st_ref[e]
    hi_row = en_ref[e]

    def sub_block(s, row0, shared):
        sub = pl.ds(s * SUB_BLOCK, SUB_BLOCK)
        pieces = []
        for j in range(PACK_ROWS):
            w = xbuf_ref[slot, j, sub, :]
            pieces.append(lax.bitcast_convert_type(w.astype(jnp.int16), BF16))
            pieces.append(lax.bitcast_convert_type(
                lax.shift_right_logical(w, 16).astype(jnp.int16), BF16))
        xb = jnp.concatenate(pieces, axis=1)
        gate = jnp.dot(xb, wgb_ref[...], preferred_element_type=F32)
        up = jnp.dot(xb, wub_ref[...], preferred_element_type=F32)
        yb = jnp.dot((_silu(gate) * up).astype(BF16), wdb_ref[...], preferred_element_type=F32)
        if shared:
            rows = row0 + lax.broadcasted_iota(I32, (SUB_BLOCK, 1), 0)
            mine = jnp.logical_and(rows >= lo_row, rows < hi_row)
        for j in range(PACK_ROWS):
            word = pltpu.pack_elementwise(
                [yb[:, 2 * j * LANES:(2 * j + 1) * LANES], yb[:, (2 * j + 1) * LANES:(2 * j + 2) * LANES]],
                packed_dtype=BF16)
            if shared:
                word = jnp.where(mine, word, stage_ref[slot, j, sub, :])
            stage_ref[slot, j, sub, :] = word

    for s in range(rb // SUB_BLOCK):
        row0 = blk * rb + s * SUB_BLOCK
        touched = jnp.logical_and(live, jnp.logical_and(row0 < hi_row, row0 + SUB_BLOCK > lo_row))
        whole = jnp.logical_and(lo_row <= row0, hi_row >= row0 + SUB_BLOCK)

        @pl.when(jnp.logical_and(touched, whole))
        def _():
            sub_block(s, row0, shared=False)

        @pl.when(jnp.logical_and(touched, jnp.logical_not(whole)))
        def _():
            sub_block(s, row0, shared=True)

    @pl.when(jnp.logical_and(live, last))
    def _():
        for j in range(PACK_ROWS):
            out_copy(j, blk, slot).start()
        state_ref[1 + slot] = 1

    @pl.when(p == n_pairs - 1)
    def _():
        drain(0)
        drain(1)


def _experts_call(pair_e, pair_blk, pair_ok, starts, ends, xs, w_gate, w_up, w_down):
    ne, d, de = w_gate.shape
    rb = ROW_BLOCK
    n_pairs = pair_e.shape[0]
    grid_spec = pltpu.PrefetchScalarGridSpec(
        num_scalar_prefetch=5,
        grid=(n_pairs,),
        in_specs=[pl.BlockSpec(memory_space=pl.ANY),
                  pl.BlockSpec((1, d, de), lambda p, pe, pb, pv, st, en: (pe[p], 0, 0)),
                  pl.BlockSpec((1, d, de), lambda p, pe, pb, pv, st, en: (pe[p], 0, 0)),
                  pl.BlockSpec((1, de, d), lambda p, pe, pb, pv, st, en: (pe[p], 0, 0))],
        out_specs=pl.BlockSpec(memory_space=pl.ANY),
        scratch_shapes=[pltpu.VMEM((d, de), BF16), pltpu.VMEM((d, de), BF16),
                        pltpu.VMEM((de, d), BF16),
                        pltpu.VMEM((2, PACK_ROWS, rb, LANES), I32),
                        pltpu.VMEM((2, PACK_ROWS, rb, LANES), I32),
                        pltpu.SemaphoreType.DMA((2,)), pltpu.SemaphoreType.DMA((2,)),
                        pltpu.SMEM((3,), I32)],
    )
    return pl.pallas_call(
        _experts_kernel,
        out_shape=jax.ShapeDtypeStruct(xs.shape, xs.dtype),
        grid_spec=grid_spec,
        compiler_params=pltpu.CompilerParams(
            dimension_semantics=("arbitrary",), vmem_limit_bytes=VMEM_LIMIT),
        name="experts",
    )(pair_e, pair_blk, pair_ok, starts, ends, xs, w_gate, w_up, w_down)


def _combine_kernel(y0_ref, y1_ref, y2_ref, y3_ref, x1_ref, hp_ref, w_ref, g2_ref, fg_ref,
                    wsg_ref, wsu_ref, wsd_ref, o_ref):
    tm = x1_ref.shape[0]
    w = w_ref[...]
    parts = []
    for y_ref in (y0_ref, y1_ref, y2_ref, y3_ref):
        for half in range(2):
            acc = None
            for k in range(TOP_K):
                piece = pltpu.unpack_elementwise(y_ref[k], index=half, packed_dtype=BF16,
                                                 unpacked_dtype=F32) * w[:, k:k + 1]
                acc = piece if acc is None else acc + piece
            parts.append(acc)
    routed = jnp.concatenate(parts, axis=1)
    pieces = []
    for j in range(PACK_ROWS):
        word = hp_ref[pl.ds(j, tm, stride=PACK_ROWS), :]
        pieces.append(lax.bitcast_convert_type(word.astype(jnp.int16), BF16))
        pieces.append(lax.bitcast_convert_type(lax.shift_right_logical(word, 16).astype(jnp.int16), BF16))
    h2 = jnp.concatenate(pieces, axis=1)
    hid = _silu(jnp.dot(h2, wsg_ref[...], preferred_element_type=F32)) * jnp.dot(
        h2, wsu_ref[...], preferred_element_type=F32)
    shared = jnp.dot(hid.astype(BF16), wsd_ref[...], preferred_element_type=F32)
    x2 = x1_ref[...] + g2_ref[0] * (routed + shared)
    ms = jnp.mean(x2 * x2, axis=-1, keepdims=True)
    o_ref[...] = x2 * lax.rsqrt(ms + EPS) * fg_ref[...]


def _combine_kernel_into(*refs):
    _combine_kernel(*refs[:-2], refs[-1])


def _combine_call(y_tok, x1, hp, w_tok, g2, fgain, wsg, wsu, wsd, seq, part, prev_out):
    n, d = x1.shape
    dsh = wsg.shape[1]
    tm = MOVE_TILE
    per_b = seq // tm
    tiles = y_tok.shape[2] // tm
    t0 = part * tiles
    const = lambda i: (0, 0)

    def piece_spec(j):
        return pl.BlockSpec((None, TOP_K, tm, LANES), lambda i: (j, 0, i, 0))

    in_specs = [piece_spec(j) for j in range(PACK_ROWS)] + [
        pl.BlockSpec((tm, d), lambda i: (t0 + i, 0)),
        pl.BlockSpec((tm * PACK_ROWS, LANES), lambda i: (t0 + i, 0)),
        pl.BlockSpec((tm, TOP_K), lambda i: (t0 + i, 0)),
        pl.BlockSpec((1, 1, d), lambda i: ((t0 + i) // per_b, 0, 0)),
        pl.BlockSpec((1, d), const),
        pl.BlockSpec((d, dsh), const),
        pl.BlockSpec((d, dsh), const),
        pl.BlockSpec((dsh, d), const)]
    args = [y_tok, y_tok, y_tok, y_tok, x1, hp, w_tok, g2, fgain, wsg, wsu, wsd]
    aliases = {}
    body = _combine_kernel
    if prev_out is not None:
        in_specs.append(pl.BlockSpec(memory_space=pl.ANY))
        args.append(prev_out)
        aliases = {len(args) - 1: 0}
        body = _combine_kernel_into
    return pl.pallas_call(
        body,
        out_shape=jax.ShapeDtypeStruct((n, d), F32),
        grid=(tiles,),
        in_specs=in_specs,
        out_specs=pl.BlockSpec((tm, d), lambda i: (t0 + i, 0)),
        input_output_aliases=aliases,
        compiler_params=pltpu.CompilerParams(
            dimension_semantics=("arbitrary",), vmem_limit_bytes=VMEM_LIMIT),
        name="combine",
    )(*args)


def _pair_tables(counts, n_rows):
    ne = counts.shape[0]
    sizes = counts.astype(I32)
    ends = jnp.cumsum(sizes)
    starts = ends - sizes
    first_blk = starts // ROW_BLOCK
    last_blk = (ends - 1) // ROW_BLOCK
    n_pairs = jnp.where(sizes > 0, last_blk - first_blk + 1, 0)
    pair_end = jnp.cumsum(n_pairs)
    pair_start = pair_end - n_pairs
    max_pairs = n_rows // ROW_BLOCK + ne
    p = jnp.arange(max_pairs, dtype=I32)
    ok = p < pair_end[-1]
    pc = jnp.minimum(p, pair_end[-1] - 1)
    pair_e = jnp.minimum(jnp.searchsorted(pair_end, pc, side='right'), ne - 1).astype(I32)
    pair_blk = (first_blk[pair_e] + pc - pair_start[pair_e]).astype(I32)
    return pair_e, pair_blk, ok.astype(I32), starts.astype(I32), ends.astype(I32)


def kernel(x, c, w_ada, b_ada, w_in, lb_logits, hgrn_norm, lam_re, lam_im, log_dt, b_re, b_im,
           c_re, c_im, d_skip, w_glu, b_glu, w_out, w_router, router_bias, w_gate, w_up, w_down,
           ws_gate, ws_up, ws_down, final_gain):
    bsz, seq, d = x.shape
    n = bsz * seq
    fdim = HGRN_HEADS * HGRN_KDIM
    n_chunks = seq // CHUNK
    lb = jnp.cumsum(jax.nn.softmax(lb_logits.astype(F32), axis=0), axis=0)[0].reshape(1, fdim)

    mod = _mod_call(c, w_ada[0], b_ada[0]).reshape(bsz, 6, d)
    nb = w_in.shape[2] - 4 * fdim
    perm = jnp.concatenate([jnp.arange(0, nb, 2), jnp.arange(1, nb, 2)])
    w_in_b = w_in[0].astype(BF16)
    ut, out_a = _mix_front_call(x, mod, w_in_b[:, :4 * fdim], w_in_b[:, 4 * fdim:][:, perm].T, lb,
                                hgrn_norm[0].reshape(1, fdim))

    m_tab, p_tab, q_tab, a1, a2 = _s5_tables(lam_re[0], lam_im[0], log_dt[0], b_re[0], b_im[0],
                                             c_re[0], c_im[0])
    yt = _s5_call(ut, m_tab, p_tab, q_tab, a1, a2, n_chunks, bsz)

    wr_t = w_router[0].T
    wr_hi = wr_t.astype(BF16)
    wr_lo = (wr_t - wr_hi.astype(F32)).astype(BF16)
    w_out_b = w_out[0].astype(BF16)
    w_out_p = jnp.concatenate([w_out_b[:fdim], w_out_b[fdim:][perm]], axis=0)
    x1, hp, logits_t = _mix_back_call(
        x.reshape(n, d), out_a.reshape(n, fdim), yt, ut, mod,
        d_skip[0][perm].reshape(nb, 1), w_glu[0][perm][:, perm].T.astype(BF16),
        b_glu[0][perm].reshape(nb, 1), w_out_p, wr_hi, wr_lo, seq)

    top_idx, top_w, rank, counts = _route_call(logits_t, router_bias[0])
    pair_e, pair_blk, pair_ok, starts, ends = _pair_tables(counts[:, 0], n * TOP_K)
    pos = _pos_call(top_idx, rank, starts)

    pos_win = pos.reshape(TOP_K, n // SC_WINDOW, SC_WINDOW).transpose(1, 0, 2)
    xs = _dispatch_call(pos_win, hp.reshape(n, PACK_ROWS, LANES))
    y_sorted = _experts_call(pair_e, pair_blk, pair_ok, starts, ends, xs, w_gate[0], w_up[0], w_down[0])
    w_tok = top_w.T
    g2 = mod[:, 5:6, :]
    fgain = final_gain.reshape(1, d)
    wsg, wsu, wsd = ws_gate[0].astype(BF16), ws_up[0].astype(BF16), ws_down[0].astype(BF16)
    wins = pos_win.shape[0] // TAIL_PARTS
    out = None
    for part in range(TAIL_PARTS):
        y_tok = _collect_call(pos_win[part * wins:(part + 1) * wins], y_sorted)
        out = _combine_call(y_tok, x1, hp, w_tok, g2, fgain, wsg, wsu, wsd, seq, part, out)
    return out.reshape(bsz, seq, d)
```

```python
import functools

import jax
import jax.numpy as jnp
from jax import lax
from jax.experimental import pallas as pl
from jax.experimental.pallas import tpu as pltpu
from jax.experimental.pallas import tpu_sc as plsc

F32 = jnp.float32
BF16 = jnp.bfloat16
I32 = jnp.int32

EPS = 1e-6
CHUNK = 64
HGRN_HEADS = 4
HGRN_KDIM = 128
S5_GROUP = 16
S5_STATE = 64
N_EXPERT_GROUPS = 8
TOPK_GROUPS = 4
TOP_K = 8
ROUTE_SCALE = 2.5
LANES = 128
PACK_ROWS = 4
SC_CORES = 2
SC_SUBCORES = 16
SC_WINDOW = 64

SEQ_TILE = 512
TOK_TILE = 512
ROUTE_TILE = 512
POS_TILE = 1024
MOVE_TILE = 256
ROW_BLOCK = 2048
SUB_BLOCK = 256
TAIL_PARTS = 4
VMEM_LIMIT = 56 * 1024 * 1024

_NT = (((1,), (1,)), ((), ()))
_TN = (((0,), (0,)), ((), ()))


def _sigmoid(v):
    return 0.5 * jnp.tanh(0.5 * v) + 0.5


def _silu(v):
    return v * _sigmoid(v)


def _bdot(a, b):
    return jnp.dot(a.astype(BF16), b.astype(BF16), preferred_element_type=F32)


def _store_packed(ref, val, n_rows):
    for j in range(PACK_ROWS):
        lo = val[:, 2 * j * LANES:(2 * j + 1) * LANES]
        hi = val[:, (2 * j + 1) * LANES:(2 * j + 2) * LANES]
        ref[pl.ds(j, n_rows, stride=PACK_ROWS), :] = pltpu.pack_elementwise([lo, hi], packed_dtype=BF16)


def _mod_kernel(c_ref, w_ref, b_ref, o_ref):
    o_ref[...] = _bdot(_silu(c_ref[...]), w_ref[...]) + b_ref[...]


def _mod_call(c, w_ada, b_ada):
    bsz, d = c.shape
    n_out = w_ada.shape[1]
    return pl.pallas_call(
        _mod_kernel,
        out_shape=jax.ShapeDtypeStruct((bsz, n_out), F32),
        grid=(n_out // d,),
        in_specs=[pl.BlockSpec((bsz, d), lambda j: (0, 0)),
                  pl.BlockSpec((d, d), lambda j: (0, j)),
                  pl.BlockSpec((1, d), lambda j: (0, j))],
        out_specs=pl.BlockSpec((bsz, d), lambda j: (0, j)),
        compiler_params=pltpu.CompilerParams(vmem_limit_bytes=VMEM_LIMIT),
        name="mod",
    )(c, w_ada, b_ada.reshape(1, n_out))


def _split_chunk_pairs(tile_even, tile_odd):
    low = lax.broadcasted_iota(I32, tile_even.shape, 1) < CHUNK
    first = jnp.where(low, tile_even, pltpu.roll(tile_odd, CHUNK, 1))
    second = jnp.where(low, pltpu.roll(tile_even, CHUNK, 1), tile_odd)
    return first, second


def _mix_front_kernel(x_ref, mod_ref, win_ref, wut_ref, lb_ref, gn_ref, ltri_ref,
                      ut_ref, oa_ref, proj_ref, st_ref, flat_ref):
    fdim = HGRN_HEADS * HGRN_KDIM
    ts = x_ref.shape[1]
    pairs = wut_ref.shape[0] // 2
    rows = ts // CHUNK

    @pl.when(pl.program_id(1) == 0)
    def _():
        st_ref[...] = jnp.zeros_like(st_ref)

    x = x_ref[0]
    ms = jnp.mean(x * x, axis=-1, keepdims=True)
    h = x * lax.rsqrt(ms + EPS) * (1.0 + mod_ref[0, 1:2, :]) + mod_ref[0, 0:1, :]
    hb = h.astype(BF16)
    proj_ref[...] = jnp.dot(hb, win_ref[...], preferred_element_type=F32)
    u_t = lax.dot_general(wut_ref[...], hb, _NT, preferred_element_type=F32)
    for m in range(ts // LANES):
        first, second = _split_chunk_pairs(u_t[:pairs, m * LANES:(m + 1) * LANES],
                                           u_t[pairs:, m * LANES:(m + 1) * LANES])
        flat_ref[2 * m * pairs:(2 * m + 1) * pairs, :] = first
        flat_ref[(2 * m + 1) * pairs:(2 * m + 2) * pairs, :] = second
    per_group = S5_GROUP // 2
    for q in range(pairs):
        ut_ref[q // per_group, :, (q % per_group) * LANES:(q % per_group + 1) * LANES] = (
            flat_ref[pl.ds(q, rows, stride=pairs), :])

    lb = lb_ref[...]
    gn = gn_ref[...]
    ltri = ltri_ref[...]
    row = lax.broadcasted_iota(I32, (CHUNK, CHUNK), 0)
    col = lax.broadcasted_iota(I32, (CHUNK, CHUNK), 1)
    causal = row >= col

    def chunk_step(ci, carry):
        r0 = pl.multiple_of(ci * CHUNK, CHUNK)
        q = proj_ref[pl.ds(r0, CHUNK), 0:fdim]
        fl = proj_ref[pl.ds(r0, CHUNK), fdim:2 * fdim]
        iv = proj_ref[pl.ds(r0, CHUNK), 2 * fdim:3 * fdim]
        og = proj_ref[pl.ds(r0, CHUNK), 3 * fdim:4 * fdim]
        f = lb + (1.0 - lb) * _sigmoid(fl)
        lf = jnp.log(f)
        lf_hi = lf.astype(BF16)
        rem = lf - lf_hi.astype(F32)
        lf_mid = rem.astype(BF16)
        lf_lo = (rem - lf_mid.astype(F32)).astype(BF16)
        b = (jnp.dot(ltri, lf_hi, preferred_element_type=F32)
             + jnp.dot(ltri, lf_mid, preferred_element_type=F32)
             + jnp.dot(ltri, lf_lo, preferred_element_type=F32))
        b_ref = b[CHUNK // 2 - 1:CHUNK // 2, :]
        b_last = b[CHUNK - 1:CHUNK, :]
        qs = _silu(q)
        kk = 1.0 - f
        qe = (qs * jnp.exp(b - b_ref)).astype(BF16)
        ke = (kk * jnp.exp(b_ref - b)).astype(BF16)
        qb = (qs * jnp.exp(b)).astype(BF16)
        k2 = (kk * jnp.exp(b_last - b)).astype(BF16)
        dec = jnp.exp(b_last)
        ivb = iv.astype(BF16)
        outs = []
        for hh in range(HGRN_HEADS):
            sl = slice(hh * HGRN_KDIM, (hh + 1) * HGRN_KDIM)
            att = lax.dot_general(qe[:, sl], ke[:, sl], _NT, preferred_element_type=F32)
            att = jnp.where(causal, att, 0.0)
            st = st_ref[hh]
            o = jnp.dot(att.astype(BF16), ivb[:, sl], preferred_element_type=F32)
            o = o + lax.dot_general(qb[:, sl], st.astype(BF16), _NT, preferred_element_type=F32)
            st_ref[hh] = st * dec[:, sl] + lax.dot_general(
                ivb[:, sl], k2[:, sl], _TN, preferred_element_type=F32)
            outs.append(o * lax.rsqrt(jnp.mean(o * o, axis=-1, keepdims=True) + EPS))
        o = jnp.concatenate(outs, axis=1) * gn * _silu(og)
        oa_ref[0, pl.ds(r0, CHUNK), :] = o.astype(BF16)
        return carry

    lax.fori_loop(0, ts // CHUNK, chunk_step, 0, unroll=True)


def _mix_front_call(x, mod, w_main, w_ut, lb, gn):
    bsz, seq, d = x.shape
    fdim = HGRN_HEADS * HGRN_KDIM
    ncols = w_main.shape[1]
    nb = w_ut.shape[0]
    groups = nb // S5_GROUP
    ltri = jnp.tril(jnp.ones((CHUNK, CHUNK), BF16))
    ts = SEQ_TILE
    tiles = seq // ts
    rows = ts // CHUNK
    return pl.pallas_call(
        _mix_front_kernel,
        out_shape=(jax.ShapeDtypeStruct((groups, bsz * seq // CHUNK, S5_GROUP * CHUNK), F32),
                   jax.ShapeDtypeStruct((bsz, seq, fdim), BF16)),
        grid=(bsz, tiles),
        in_specs=[pl.BlockSpec((1, ts, d), lambda b, j: (b, j, 0)),
                  pl.BlockSpec((1, 6, d), lambda b, j: (b, 0, 0)),
                  pl.BlockSpec((d, ncols), lambda b, j: (0, 0)),
                  pl.BlockSpec((nb, d), lambda b, j: (0, 0)),
                  pl.BlockSpec((1, fdim), lambda b, j: (0, 0)),
                  pl.BlockSpec((1, fdim), lambda b, j: (0, 0)),
                  pl.BlockSpec((CHUNK, CHUNK), lambda b, j: (0, 0))],
        out_specs=(pl.BlockSpec((groups, rows, S5_GROUP * CHUNK), lambda b, j: (0, b * tiles + j, 0)),
                   pl.BlockSpec((1, ts, fdim), lambda b, j: (b, j, 0))),
        scratch_shapes=[pltpu.VMEM((ts, ncols), F32),
                        pltpu.VMEM((HGRN_HEADS, fdim // HGRN_HEADS, HGRN_KDIM), F32),
                        pltpu.VMEM((rows * nb // 2, LANES), F32)],
        compiler_params=pltpu.CompilerParams(
            dimension_semantics=("arbitrary", "arbitrary"), vmem_limit_bytes=VMEM_LIMIT),
        name="mix_front",
    )(x, mod, w_main, w_ut, lb, gn, ltri)


def _s5_tables(lam_re, lam_im, log_dt, b_re, b_im, c_re, c_im):
    t = CHUNK
    hp = lax.Precision.HIGHEST
    lam = lax.complex(jnp.minimum(lam_re, -1e-4), lam_im)
    lam_dt = lam * jnp.exp(log_dt)[:, None]
    lam_bar = jnp.exp(lam_dt)
    b_bar = ((lam_bar - 1.0) / lam)[..., None] * lax.complex(b_re, b_im)
    c_mat = lax.complex(c_re, c_im)
    taus = jnp.arange(t + 1, dtype=F32)
    lam_pow = jnp.exp(lam_dt[:, None, :] * taus[None, :, None])
    g, p = lam.shape
    c = b_re.shape[-1]
    cl = c_mat[:, None, :, :] * lam_pow[:, :t, None, :]
    kr = (jnp.einsum('gtcp,gpi->gtci', cl.real, b_bar.real, precision=hp)
          - jnp.einsum('gtcp,gpi->gtci', cl.imag, b_bar.imag, precision=hp))
    z = jnp.pad(kr.transpose(0, 3, 2, 1).astype(BF16), ((0, 0), (0, 0), (0, 0), (t - 1, 0)))
    kg = jnp.stack([z[..., t - 1 - s:2 * t - 1 - s] for s in range(t)], axis=2)
    m = kg.reshape(g, c * t, c * t)
    pc = lam_pow[:, t - 1::-1][:, :t, :, None] * b_bar[:, None, :, :]
    pc = pc.transpose(0, 3, 1, 2).reshape(g, c * t, p)
    p_tab = jnp.concatenate([pc.real, pc.imag], axis=-1)
    ql = c_mat[:, None, :, :] * lam_pow[:, 1:t + 1, None, :]
    ql = ql.transpose(0, 3, 2, 1).reshape(g, p, c * t)
    q_tab = jnp.concatenate([ql.real, -ql.imag], axis=1)
    lam_t = lam_pow[:, t]
    a1 = jnp.concatenate([lam_t.real, lam_t.real], axis=-1)[:, None, :]
    a2 = jnp.concatenate([-lam_t.imag, lam_t.imag], axis=-1)[:, None, :]
    return m.astype(BF16), p_tab.astype(BF16), q_tab.astype(BF16), a1, a2


def _s5_kernel(u_ref, m_ref, p_ref, q_ref, a1_ref, a2_ref, y_ref, v_ref, xs_ref, *, n_chunks, n_batch):
    u = u_ref[0].astype(BF16)
    v_ref[...] = jnp.dot(u, p_ref[0], preferred_element_type=F32)
    a1 = a1_ref[0]
    a2 = a2_ref[0]
    half = xs_ref.shape[1] // 2

    def step(n, state):
        xs_ref[pl.ds(n, n_batch, stride=n_chunks), :] = state
        return (a1 * state + a2 * pltpu.roll(state, half, 1)
                + v_ref[pl.ds(n, n_batch, stride=n_chunks), :])

    lax.fori_loop(0, n_chunks, step, jnp.zeros((n_batch, xs_ref.shape[1]), F32))
    y = jnp.dot(u, m_ref[0], preferred_element_type=F32)
    y_ref[0] = y + jnp.dot(xs_ref[...].astype(BF16), q_ref[0], preferred_element_type=F32)


def _s5_call(ut, m, p_tab, q_tab, a1, a2, n_chunks, n_batch):
    g, rows, width = ut.shape
    p2 = p_tab.shape[-1]
    return pl.pallas_call(
        functools.partial(_s5_kernel, n_chunks=n_chunks, n_batch=n_batch),
        out_shape=jax.ShapeDtypeStruct((g, rows, width), F32),
        grid=(g,),
        in_specs=[pl.BlockSpec((1, rows, width), lambda i: (i, 0, 0)),
                  pl.BlockSpec((1, width, width), lambda i: (i, 0, 0)),
                  pl.BlockSpec((1, width, p2), lambda i: (i, 0, 0)),
                  pl.BlockSpec((1, p2, width), lambda i: (i, 0, 0)),
                  pl.BlockSpec((1, 1, p2), lambda i: (i, 0, 0)),
                  pl.BlockSpec((1, 1, p2), lambda i: (i, 0, 0))],
        out_specs=pl.BlockSpec((1, rows, width), lambda i: (i, 0, 0)),
        scratch_shapes=[pltpu.VMEM((rows, p2), F32), pltpu.VMEM((rows, p2), F32)],
        compiler_params=pltpu.CompilerParams(
            dimension_semantics=("arbitrary",), vmem_limit_bytes=VMEM_LIMIT),
        name="s5",
    )(ut, m, p_tab, q_tab, a1, a2)


def _token_major(flat_ref, src_ref):
    groups, rows, _ = src_ref.shape
    per_group = S5_GROUP // 2
    pairs = groups * per_group
    for q in range(pairs):
        flat_ref[pl.ds(q, rows, stride=pairs), :] = (
            src_ref[q // per_group, :, (q % per_group) * LANES:(q % per_group + 1) * LANES])
    tiles = []
    for m in range(rows // 2):
        even, odd = _split_chunk_pairs(flat_ref[2 * m * pairs:(2 * m + 1) * pairs, :],
                                       flat_ref[(2 * m + 1) * pairs:(2 * m + 2) * pairs, :])
        tiles.append(jnp.concatenate([even, odd], axis=0))
    return jnp.concatenate(tiles, axis=1)


def _mix_back_kernel(x_ref, oa_ref, yt_ref, ut_ref, mod_ref, dskip_ref, wglu_ref, bglu_ref,
                     wout_ref, wrh_ref, wrl_ref,
                     x1_ref, hp_ref, lt_ref, flat_ref):
    na = oa_ref.shape[1]
    y_t = _token_major(flat_ref, yt_ref)
    u_t = _token_major(flat_ref, ut_ref)
    z_t = jax.nn.gelu(y_t + dskip_ref[...] * u_t)
    gate_t = _sigmoid(jnp.dot(wglu_ref[...], z_t.astype(BF16), preferred_element_type=F32)
                      + bglu_ref[...])
    ob_t = (z_t * gate_t).astype(BF16)
    mixed = (jnp.dot(oa_ref[...], wout_ref[0:na, :], preferred_element_type=F32)
             + lax.dot_general(ob_t, wout_ref[na:, :], _TN, preferred_element_type=F32))
    x1 = x_ref[...] + mod_ref[0, 2:3, :] * mixed
    ms = jnp.mean(x1 * x1, axis=-1, keepdims=True)
    h2 = x1 * lax.rsqrt(ms + EPS) * (1.0 + mod_ref[0, 4:5, :]) + mod_ref[0, 3:4, :]
    _store_packed(hp_ref, h2, h2.shape[0])
    h_hi = h2.astype(BF16)
    h_lo = (h2 - h_hi.astype(F32)).astype(BF16)
    lt = lax.dot_general(wrh_ref[...], h_hi, _NT, preferred_element_type=F32)
    lt = lt + lax.dot_general(wrl_ref[...], h_hi, _NT, preferred_element_type=F32)
    lt = lt + lax.dot_general(wrh_ref[...], h_lo, _NT, preferred_element_type=F32)
    lt_ref[...] = lt
    x1_ref[...] = x1


def _mix_back_call(x2d, oa, yt, ut, mod, dskip, wglu, bglu, wout, wrh, wrl, seq):
    n, d = x2d.shape
    nb = oa.shape[1]
    ne = wrh.shape[0]
    tm = TOK_TILE
    per_b = seq // tm
    groups, _, width = yt.shape
    rows = tm // CHUNK
    const = lambda i: (0, 0)
    flat_block = pl.BlockSpec((groups, rows, width), lambda i: (0, i, 0))
    return pl.pallas_call(
        _mix_back_kernel,
        out_shape=(jax.ShapeDtypeStruct((n, d), F32),
                   jax.ShapeDtypeStruct((n * PACK_ROWS, LANES), I32),
                   jax.ShapeDtypeStruct((ne, n), F32)),
        grid=(n // tm,),
        in_specs=[pl.BlockSpec((tm, d), lambda i: (i, 0)),
                  pl.BlockSpec((tm, nb), lambda i: (i, 0)),
                  flat_block,
                  flat_block,
                  pl.BlockSpec((1, 6, d), lambda i: (i // per_b, 0, 0)),
                  pl.BlockSpec((nb, 1), const),
                  pl.BlockSpec((nb, nb), const),
                  pl.BlockSpec((nb, 1), const),
                  pl.BlockSpec((d, d), const),
                  pl.BlockSpec((ne, d), const),
                  pl.BlockSpec((ne, d), const)],
        out_specs=(pl.BlockSpec((tm, d), lambda i: (i, 0)),
                   pl.BlockSpec((tm * PACK_ROWS, LANES), lambda i: (i, 0)),
                   pl.BlockSpec((ne, tm), lambda i: (0, i))),
        scratch_shapes=[pltpu.VMEM((rows * nb // 2, LANES), F32)],
        compiler_params=pltpu.CompilerParams(
            dimension_semantics=("arbitrary",), vmem_limit_bytes=VMEM_LIMIT),
        name="mix_back",
    )(x2d, oa, yt, ut, mod, dskip, wglu, bglu, wout, wrh, wrl)


def _route_kernel(lt_ref, bias_ref, su_ref, idx_ref, w_ref, rank_ref, cnt_ref, run_ref):
    ne, tr = lt_ref.shape
    per_group = ne // N_EXPERT_GROUPS
    neg = -jnp.inf

    @pl.when(pl.program_id(0) == 0)
    def _():
        run_ref[...] = jnp.zeros_like(run_ref)

    s = _sigmoid(lt_ref[...])
    sel = s + bias_ref[...]
    gio = lax.broadcasted_iota(I32, (per_group, tr), 0)
    gscore = []
    for g in range(N_EXPERT_GROUPS):
        v = sel[g * per_group:(g + 1) * per_group, :]
        m1 = jnp.max(v, axis=0, keepdims=True)
        i1 = jnp.min(jnp.where(v == m1, gio, per_group), axis=0, keepdims=True)
        m2 = jnp.max(jnp.where(gio == i1, neg, v), axis=0, keepdims=True)
        gscore.append(m1 + m2)
    masked = []
    for g in range(N_EXPERT_GROUPS):
        ahead = jnp.zeros((1, tr), I32)
        for o in range(N_EXPERT_GROUPS):
            if o == g:
                continue
            wins = (gscore[o] >= gscore[g]) if o < g else (gscore[o] > gscore[g])
            ahead = ahead + wins.astype(I32)
        keep = ahead < TOPK_GROUPS
        masked.append(jnp.where(keep, sel[g * per_group:(g + 1) * per_group, :], neg))
    selm = jnp.concatenate(masked, axis=0)
    eio = lax.broadcasted_iota(I32, (ne, tr), 0)
    candidate = selm > neg
    idxs, ws = [], []
    for k in range(TOP_K):
        m = jnp.max(selm, axis=0, keepdims=True)
        ik = jnp.min(jnp.where(selm == m, eio, ne), axis=0, keepdims=True)
        onehot = eio == ik
        ws.append(jnp.sum(jnp.where(onehot, s, 0.0), axis=0, keepdims=True))
        selm = jnp.where(onehot, neg, selm)
        idxs.append(ik)
    hits = jnp.where(jnp.logical_and(candidate, selm == neg), 1.0, 0.0)
    wsum = ws[0]
    for k in range(1, TOP_K):
        wsum = wsum + ws[k]
    scale = ROUTE_SCALE / wsum
    ranks = jnp.dot(hits.astype(BF16), su_ref[...], preferred_element_type=F32) + run_ref[...]
    for k in range(TOP_K):
        idx_ref[k:k + 1, :] = idxs[k]
        w_ref[k:k + 1, :] = ws[k] * scale
        rk = jnp.sum(jnp.where(eio == idxs[k], ranks, 0.0), axis=0, keepdims=True)
        rank_ref[k:k + 1, :] = rk.astype(I32)
    run_ref[...] = run_ref[...] + jnp.sum(hits, axis=1, keepdims=True)
    cnt_ref[...] = run_ref[...]


def _route_call(lt, bias):
    ne, n = lt.shape
    tr = ROUTE_TILE
    su = jnp.triu(jnp.ones((tr, tr), F32), k=1).astype(BF16)
    return pl.pallas_call(
        _route_kernel,
        out_shape=(jax.ShapeDtypeStruct((TOP_K, n), I32),
                   jax.ShapeDtypeStruct((TOP_K, n), F32),
                   jax.ShapeDtypeStruct((TOP_K, n), I32),
                   jax.ShapeDtypeStruct((ne, 1), F32)),
        grid=(n // tr,),
        in_specs=[pl.BlockSpec((ne, tr), lambda i: (0, i)),
                  pl.BlockSpec((ne, 1), lambda i: (0, 0)),
                  pl.BlockSpec((tr, tr), lambda i: (0, 0))],
        out_specs=(pl.BlockSpec((TOP_K, tr), lambda i: (0, i)),
                   pl.BlockSpec((TOP_K, tr), lambda i: (0, i)),
                   pl.BlockSpec((TOP_K, tr), lambda i: (0, i)),
                   pl.BlockSpec((ne, 1), lambda i: (0, 0))),
        scratch_shapes=[pltpu.VMEM((ne, 1), F32)],
        compiler_params=pltpu.CompilerParams(
            dimension_semantics=("arbitrary",), vmem_limit_bytes=VMEM_LIMIT),
        name="route",
    )(lt, bias.reshape(ne, 1), su)


def _pos_kernel(idx_ref, rank_ref, st_ref, pos_ref):
    ne = st_ref.shape[0]
    tp = idx_ref.shape[1]
    eio = lax.broadcasted_iota(I32, (ne, tp), 0)
    st = st_ref[...]
    for k in range(TOP_K):
        base = jnp.sum(jnp.where(eio == idx_ref[k:k + 1, :], st, 0), axis=0, keepdims=True)
        pos_ref[k:k + 1, :] = base + rank_ref[k:k + 1, :]


def _pos_call(top_idx, rank, starts):
    kk, n = top_idx.shape
    ne = starts.shape[0]
    tp = POS_TILE
    return pl.pallas_call(
        _pos_kernel,
        out_shape=jax.ShapeDtypeStruct((kk, n), I32),
        grid=(n // tp,),
        in_specs=[pl.BlockSpec((kk, tp), lambda i: (0, i)),
                  pl.BlockSpec((kk, tp), lambda i: (0, i)),
                  pl.BlockSpec((ne, 1), lambda i: (0, 0))],
        out_specs=pl.BlockSpec((kk, tp), lambda i: (0, i)),
        compiler_params=pltpu.CompilerParams(
            dimension_semantics=("arbitrary",), vmem_limit_bytes=VMEM_LIMIT),
        name="pos",
    )(top_idx, rank, starts.reshape(ne, 1))


def _sc_mesh():
    return plsc.VectorSubcoreMesh(core_axis_name="c", subcore_axis_name="s",
                                  num_cores=SC_CORES, num_subcores=SC_SUBCORES)


def _sc_worker():
    return lax.axis_index("s") * SC_CORES + lax.axis_index("c")


def _dispatch_call(pos_win, hp):
    n = hp.shape[0]
    n_workers = SC_CORES * SC_SUBCORES
    wins_per_worker = n // SC_WINDOW // n_workers

    def body(hp_hbm, pos_hbm, xs_hbm, idx_a, idx_b, rows_a, rows_b, load_a, load_b, scatter_sem):
        first_win = _sc_worker() * wins_per_worker
        idx_v, rows_v, load_sem = (idx_a, idx_b), (rows_a, rows_b), (load_a, load_b)

        def loads(win, slot):
            return (pltpu.make_async_copy(hp_hbm.at[pl.ds(win * SC_WINDOW, SC_WINDOW)], rows_v[slot],
                                          load_sem[slot]),
                    pltpu.make_async_copy(pos_hbm.at[win], idx_v[slot], load_sem[slot]))

        for cp in loads(first_win, 0):
            cp.start()

        @pl.loop(0, wins_per_worker, step=2)
        def _(w):
            for slot in range(2):
                win = first_win + w + slot
                for cp in loads(win, slot):
                    cp.wait()

                @pl.when(w + slot + 1 < wins_per_worker)
                def _():
                    for cp in loads(win + 1, 1 - slot):
                        cp.start()

                copies = [pltpu.async_copy(rows_v[slot], xs_hbm.at[idx_v[slot].at[k]], scatter_sem)
                          for k in range(TOP_K)]
                for cp in copies:
                    cp.wait()

    idx_buf = pltpu.VMEM((TOP_K, SC_WINDOW), I32)
    row_buf = pltpu.VMEM((SC_WINDOW,) + hp.shape[1:], hp.dtype)
    return pl.kernel(
        body,
        out_type=jax.ShapeDtypeStruct((n * TOP_K,) + hp.shape[1:], hp.dtype),
        mesh=_sc_mesh(),
        scratch_types=[idx_buf, idx_buf, row_buf, row_buf,
                       pltpu.SemaphoreType.DMA, pltpu.SemaphoreType.DMA, pltpu.SemaphoreType.DMA],
        name="dispatch",
    )(hp, pos_win)


def _collect_call(pos_win, y_sorted):
    n = pos_win.shape[0] * SC_WINDOW
    n_workers = SC_CORES * SC_SUBCORES
    wins_per_worker = n // SC_WINDOW // n_workers

    def body(ys_hbm, pos_hbm, out_hbm, idx_v, rows_a, rows_b, gather_sem, write_sem):
        first_win = _sc_worker() * wins_per_worker
        bufs = (rows_a, rows_b)

        @pl.loop(0, wins_per_worker)
        def _(w):
            win = first_win + w
            pltpu.sync_copy(pos_hbm.at[win], idx_v)

            def gather(k):
                return pltpu.async_copy(ys_hbm.at[idx_v.at[k]], bufs[k % 2], gather_sem)

            pending_gather = gather(0)
            pending_write = None
            for k in range(TOP_K):
                pending_gather.wait()
                if pending_write is not None:
                    for cp in pending_write:
                        cp.wait()
                if k + 1 < TOP_K:
                    pending_gather = gather(k + 1)
                pending_write = [
                    pltpu.async_copy(bufs[k % 2].at[:, j],
                                     out_hbm.at[j, k, pl.ds(win * SC_WINDOW, SC_WINDOW)], write_sem)
                    for j in range(PACK_ROWS)]
            for cp in pending_write:
                cp.wait()

    row_buf = pltpu.VMEM((SC_WINDOW,) + y_sorted.shape[1:], y_sorted.dtype)
    return pl.kernel(
        body,
        out_type=jax.ShapeDtypeStruct((PACK_ROWS, TOP_K, n, LANES), y_sorted.dtype),
        mesh=_sc_mesh(),
        scratch_types=[pltpu.VMEM((TOP_K, SC_WINDOW), I32), row_buf, row_buf,
                       pltpu.SemaphoreType.DMA, pltpu.SemaphoreType.DMA],
        name="collect",
    )(y_sorted, pos_win)


def _experts_kernel(pe_ref, pb_ref, pv_ref, st_ref, en_ref,
                    xs_hbm, wg_ref, wu_ref, wd_ref, y_hbm,
                    wgb_ref, wub_ref, wdb_ref, xbuf_ref, stage_ref,
                    in_sems, out_sems, state_ref):
    p = pl.program_id(0)
    n_pairs = pl.num_programs(0)
    e = pe_ref[p]
    blk = pb_ref[p]
    rb = xbuf_ref.shape[2]
    prev = jnp.maximum(p - 1, 0)
    nxt = jnp.minimum(p + 1, n_pairs - 1)
    live = pv_ref[p] == 1
    first = jnp.logical_or(p == 0, pb_ref[prev] != blk)
    block_ends = jnp.logical_or(pb_ref[nxt] != blk, pv_ref[nxt] == 0)
    last = jnp.logical_or(p == n_pairs - 1, block_ends)

    def in_copy(j, block, slot):
        return pltpu.make_async_copy(xs_hbm.at[pl.ds(block * rb, rb), j], xbuf_ref.at[slot, j],
                                     in_sems.at[slot])

    def out_copy(j, block, slot):
        return pltpu.make_async_copy(stage_ref.at[slot, j], y_hbm.at[pl.ds(block * rb, rb), j],
                                     out_sems.at[slot])

    def drain(slot):
        @pl.when(state_ref[1 + slot] == 1)
        def _():
            for j in range(PACK_ROWS):
                out_copy(j, 0, slot).wait()
            state_ref[1 + slot] = 0

    @pl.when(p == 0)
    def _():
        state_ref[0] = 0
        state_ref[1] = 0
        state_ref[2] = 0
        stage_ref[...] = jnp.zeros_like(stage_ref)
        for j in range(PACK_ROWS):
            in_copy(j, blk, 0).start()

    slot = jnp.where(jnp.logical_and(first, p > 0), 1 - state_ref[0], state_ref[0])
    state_ref[0] = slot

    @pl.when(jnp.logical_and(p < n_pairs - 1,
                             jnp.logical_and(pb_ref[nxt] != blk, pv_ref[nxt] == 1)))
    def _():
        for j in range(PACK_ROWS):
            in_copy(j, pb_ref[nxt], 1 - slot).start()

    @pl.when(first)
    def _():
        for j in range(PACK_ROWS):
            in_copy(j, blk, slot).wait()
        drain(slot)

    @pl.when(jnp.logical_or(p == 0, pe_ref[prev] != e))
    def _():
        wgb_ref[...] = wg_ref[0].astype(BF16)
        wub_ref[...] = wu_ref[0].astype(BF16)
        wdb_ref[...] = wd_ref[0].astype(BF16)

    lo_row = st_ref[e]
    hi_row = en_ref[e]

    def sub_block(s, row0, shared):
        sub = pl.ds(s * SUB_BLOCK, SUB_BLOCK)
        pieces = []
        for j in range(PACK_ROWS):
            w = xbuf_ref[slot, j, sub, :]
            pieces.append(lax.bitcast_convert_type(w.astype(jnp.int16), BF16))
            pieces.append(lax.bitcast_convert_type(
                lax.shift_right_logical(w, 16).astype(jnp.int16), BF16))
        xb = jnp.concatenate(pieces, axis=1)
        gate = jnp.dot(xb, wgb_ref[...], preferred_element_type=F32)
        up = jnp.dot(xb, wub_ref[...], preferred_element_type=F32)
        yb = jnp.dot((_silu(gate) * up).astype(BF16), wdb_ref[...], preferred_element_type=F32)
        if shared:
            rows = row0 + lax.broadcasted_iota(I32, (SUB_BLOCK, 1), 0)
            mine = jnp.logical_and(rows >= lo_row, rows < hi_row)
        for j in range(PACK_ROWS):
            word = pltpu.pack_elementwise(
                [yb[:, 2 * j * LANES:(2 * j + 1) * LANES], yb[:, (2 * j + 1) * LANES:(2 * j + 2) * LANES]],
                packed_dtype=BF16)
            if shared:
                word = jnp.where(mine, word, stage_ref[slot, j, sub, :])
            stage_ref[slot, j, sub, :] = word

    for s in range(rb // SUB_BLOCK):
        row0 = blk * rb + s * SUB_BLOCK
        touched = jnp.logical_and(live, jnp.logical_and(row0 < hi_row, row0 + SUB_BLOCK > lo_row))
        whole = jnp.logical_and(lo_row <= row0, hi_row >= row0 + SUB_BLOCK)

        @pl.when(jnp.logical_and(touched, whole))
        def _():
            sub_block(s, row0, shared=False)

        @pl.when(jnp.logical_and(touched, jnp.logical_not(whole)))
        def _():
            sub_block(s, row0, shared=True)

    @pl.when(jnp.logical_and(live, last))
    def _():
        for j in range(PACK_ROWS):
            out_copy(j, blk, slot).start()
        state_ref[1 + slot] = 1

    @pl.when(p == n_pairs - 1)
    def _():
        drain(0)
        drain(1)


def _experts_call(pair_e, pair_blk, pair_ok, starts, ends, xs, w_gate, w_up, w_down):
    ne, d, de = w_gate.shape
    rb = ROW_BLOCK
    n_pairs = pair_e.shape[0]
    grid_spec = pltpu.PrefetchScalarGridSpec(
        num_scalar_prefetch=5,
        grid=(n_pairs,),
        in_specs=[pl.BlockSpec(memory_space=pl.ANY),
                  pl.BlockSpec((1, d, de), lambda p, pe, pb, pv, st, en: (pe[p], 0, 0)),
                  pl.BlockSpec((1, d, de), lambda p, pe, pb, pv, st, en: (pe[p], 0, 0)),
                  pl.BlockSpec((1, de, d), lambda p, pe, pb, pv, st, en: (pe[p], 0, 0))],
        out_specs=pl.BlockSpec(memory_space=pl.ANY),
        scratch_shapes=[pltpu.VMEM((d, de), BF16), pltpu.VMEM((d, de), BF16),
                        pltpu.VMEM((de, d), BF16),
                        pltpu.VMEM((2, PACK_ROWS, rb, LANES), I32),
                        pltpu.VMEM((2, PACK_ROWS, rb, LANES), I32),
                        pltpu.SemaphoreType.DMA((2,)), pltpu.SemaphoreType.DMA((2,)),
                        pltpu.SMEM((3,), I32)],
    )
    return pl.pallas_call(
        _experts_kernel,
        out_shape=jax.ShapeDtypeStruct(xs.shape, xs.dtype),
        grid_spec=grid_spec,
        compiler_params=pltpu.CompilerParams(
            dimension_semantics=("arbitrary",), vmem_limit_bytes=VMEM_LIMIT),
        name="experts",
    )(pair_e, pair_blk, pair_ok, starts, ends, xs, w_gate, w_up, w_down)


def _combine_kernel(y0_ref, y1_ref, y2_ref, y3_ref, x1_ref, hp_ref, w_ref, g2_ref, fg_ref,
                    wsg_ref, wsu_ref, wsd_ref, o_ref):
    tm = x1_ref.shape[0]
    w = w_ref[...]
    parts = []
    for y_ref in (y0_ref, y1_ref, y2_ref, y3_ref):
        for half in range(2):
            acc = None
            for k in range(TOP_K):
                piece = pltpu.unpack_elementwise(y_ref[k], index=half, packed_dtype=BF16,
                                                 unpacked_dtype=F32) * w[:, k:k + 1]
                acc = piece if acc is None else acc + piece
            parts.append(acc)
    routed = jnp.concatenate(parts, axis=1)
    pieces = []
    for j in range(PACK_ROWS):
        word = hp_ref[pl.ds(j, tm, stride=PACK_ROWS), :]
        pieces.append(lax.bitcast_convert_type(word.astype(jnp.int16), BF16))
        pieces.append(lax.bitcast_convert_type(lax.shift_right_logical(word, 16).astype(jnp.int16), BF16))
    h2 = jnp.concatenate(pieces, axis=1)
    hid = _silu(jnp.dot(h2, wsg_ref[...], preferred_element_type=F32)) * jnp.dot(
        h2, wsu_ref[...], preferred_element_type=F32)
    shared = jnp.dot(hid.astype(BF16), wsd_ref[...], preferred_element_type=F32)
    x2 = x1_ref[...] + g2_ref[0] * (routed + shared)
    ms = jnp.mean(x2 * x2, axis=-1, keepdims=True)
    o_ref[...] = x2 * lax.rsqrt(ms + EPS) * fg_ref[...]


def _combine_kernel_into(*refs):
    _combine_kernel(*refs[:-2], refs[-1])


def _combine_call(y_tok, x1, hp, w_tok, g2, fgain, wsg, wsu, wsd, seq, part, prev_out):
    n, d = x1.shape
    dsh = wsg.shape[1]
    tm = MOVE_TILE
    per_b = seq // tm
    tiles = y_tok.shape[2] // tm
    t0 = part * tiles
    const = lambda i: (0, 0)

    def piece_spec(j):
        return pl.BlockSpec((None, TOP_K, tm, LANES), lambda i: (j, 0, i, 0))

    in_specs = [piece_spec(j) for j in range(PACK_ROWS)] + [
        pl.BlockSpec((tm, d), lambda i: (t0 + i, 0)),
        pl.BlockSpec((tm * PACK_ROWS, LANES), lambda i: (t0 + i, 0)),
        pl.BlockSpec((tm, TOP_K), lambda i: (t0 + i, 0)),
        pl.BlockSpec((1, 1, d), lambda i: ((t0 + i) // per_b, 0, 0)),
        pl.BlockSpec((1, d), const),
        pl.BlockSpec((d, dsh), const),
        pl.BlockSpec((d, dsh), const),
        pl.BlockSpec((dsh, d), const)]
    args = [y_tok, y_tok, y_tok, y_tok, x1, hp, w_tok, g2, fgain, wsg, wsu, wsd]
    aliases = {}
    body = _combine_kernel
    if prev_out is not None:
        in_specs.append(pl.BlockSpec(memory_space=pl.ANY))
        args.append(prev_out)
        aliases = {len(args) - 1: 0}
        body = _combine_kernel_into
    return pl.pallas_call(
        body,
        out_shape=jax.ShapeDtypeStruct((n, d), F32),
        grid=(tiles,),
        in_specs=in_specs,
        out_specs=pl.BlockSpec((tm, d), lambda i: (t0 + i, 0)),
        input_output_aliases=aliases,
        compiler_params=pltpu.CompilerParams(
            dimension_semantics=("arbitrary",), vmem_limit_bytes=VMEM_LIMIT),
        name="combine",
    )(*args)


def _pair_tables(counts, n_rows):
    ne = counts.shape[0]
    sizes = counts.astype(I32)
    ends = jnp.cumsum(sizes)
    starts = ends - sizes
    first_blk = starts // ROW_BLOCK
    last_blk = (ends - 1) // ROW_BLOCK
    n_pairs = jnp.where(sizes > 0, last_blk - first_blk + 1, 0)
    pair_end = jnp.cumsum(n_pairs)
    pair_start = pair_end - n_pairs
    max_pairs = n_rows // ROW_BLOCK + ne
    p = jnp.arange(max_pairs, dtype=I32)
    ok = p < pair_end[-1]
    pc = jnp.minimum(p, pair_end[-1] - 1)
    pair_e = jnp.minimum(jnp.searchsorted(pair_end, pc, side='right'), ne - 1).astype(I32)
    pair_blk = (first_blk[pair_e] + pc - pair_start[pair_e]).astype(I32)
    return pair_e, pair_blk, ok.astype(I32), starts.astype(I32), ends.astype(I32)


def kernel(x, c, w_ada, b_ada, w_in, lb_logits, hgrn_norm, lam_re, lam_im, log_dt, b_re, b_im,
           c_re, c_im, d_skip, w_glu, b_glu, w_out, w_router, router_bias, w_gate, w_up, w_down,
           ws_gate, ws_up, ws_down, final_gain):
    bsz, seq, d = x.shape
    n = bsz * seq
    fdim = HGRN_HEADS * HGRN_KDIM
    n_chunks = seq // CHUNK
    lb = jnp.cumsum(jax.nn.softmax(lb_logits.astype(F32), axis=0), axis=0)[0].reshape(1, fdim)

    mod = _mod_call(c, w_ada[0], b_ada[0]).reshape(bsz, 6, d)
    nb = w_in.shape[2] - 4 * fdim
    perm = jnp.concatenate([jnp.arange(0, nb, 2), jnp.arange(1, nb, 2)])
    w_in_b = w_in[0].astype(BF16)
    ut, out_a = _mix_front_call(x, mod, w_in_b[:, :4 * fdim], w_in_b[:, 4 * fdim:][:, perm].T, lb,
                                hgrn_norm[0].reshape(1, fdim))

    m_tab, p_tab, q_tab, a1, a2 = _s5_tables(lam_re[0], lam_im[0], log_dt[0], b_re[0], b_im[0],
                                             c_re[0], c_im[0])
    yt = _s5_call(ut, m_tab, p_tab, q_tab, a1, a2, n_chunks, bsz)

    wr_t = w_router[0].T
    wr_hi = wr_t.astype(BF16)
    wr_lo = (wr_t - wr_hi.astype(F32)).astype(BF16)
    w_out_b = w_out[0].astype(BF16)
    w_out_p = jnp.concatenate([w_out_b[:fdim], w_out_b[fdim:][perm]], axis=0)
    x1, hp, logits_t = _mix_back_call(
        x.reshape(n, d), out_a.reshape(n, fdim), yt, ut, mod,
        d_skip[0][perm].reshape(nb, 1), w_glu[0][perm][:, perm].T.astype(BF16),
        b_glu[0][perm].reshape(nb, 1), w_out_p, wr_hi, wr_lo, seq)

    top_idx, top_w, rank, counts = _route_call(logits_t, router_bias[0])
    pair_e, pair_blk, pair_ok, starts, ends = _pair_tables(counts[:, 0], n * TOP_K)
    pos = _pos_call(top_idx, rank, starts)

    pos_win = pos.reshape(TOP_K, n // SC_WINDOW, SC_WINDOW).transpose(1, 0, 2)
    xs = _dispatch_call(pos_win, hp.reshape(n, PACK_ROWS, LANES))
    y_sorted = _experts_call(pair_e, pair_blk, pair_ok, starts, ends, xs, w_gate[0], w_up[0], w_down[0])
    w_tok = top_w.T
    g2 = mod[:, 5:6, :]
    fgain = final_gain.reshape(1, d)
    wsg, wsu, wsd = ws_gate[0].astype(BF16), ws_up[0].astype(BF16), ws_down[0].astype(BF16)
    wins = pos_win.shape[0] // TAIL_PARTS
    out = None
    for part in range(TAIL_PARTS):
        y_tok = _collect_call(pos_win[part * wins:(part + 1) * wins], y_sorted)
        out = _combine_call(y_tok, x1, hp, w_tok, g2, fgain, wsg, wsu, wsd, seq, part, out)
    return out.reshape(bsz, seq, d)
```

```python
import functools

import jax
import jax.numpy as jnp
from jax import lax
from jax.experimental import pallas as pl
from jax.experimental.pallas import tpu as pltpu
from jax.experimental.pallas import tpu_sc as plsc

F32 = jnp.float32
BF16 = jnp.bfloat16
I32 = jnp.int32

EPS = 1e-6
CHUNK = 64
HGRN_HEADS = 4
HGRN_KDIM = 128
S5_GROUP = 16
S5_STATE = 64
N_EXPERT_GROUPS = 8
TOPK_GROUPS = 4
TOP_K = 8
ROUTE_SCALE = 2.5
LANES = 128
PACK_ROWS = 4
SC_CORES = 2
SC_SUBCORES = 16
SC_WINDOW = 64

SEQ_TILE = 512
TOK_TILE = 512
ROUTE_TILE = 512
POS_TILE = 1024
MOVE_TILE = 512
ROW_BLOCK = 2048
SUB_BLOCK = 512
TAIL_PARTS = 4
VMEM_LIMIT = 56 * 1024 * 1024

_NT = (((1,), (1,)), ((), ()))
_TN = (((0,), (0,)), ((), ()))


def _sigmoid(v):
    return 0.5 * jnp.tanh(0.5 * v) + 0.5


def _silu(v):
    return v * _sigmoid(v)


def _bdot(a, b):
    return jnp.dot(a.astype(BF16), b.astype(BF16), preferred_element_type=F32)


def _store_packed(ref, val, n_rows):
    for j in range(PACK_ROWS):
        lo = val[:, 2 * j * LANES:(2 * j + 1) * LANES]
        hi = val[:, (2 * j + 1) * LANES:(2 * j + 2) * LANES]
        ref[pl.ds(j, n_rows, stride=PACK_ROWS), :] = pltpu.pack_elementwise([lo, hi], packed_dtype=BF16)


def _mod_kernel(c_ref, w_ref, b_ref, o_ref):
    o_ref[...] = _bdot(_silu(c_ref[...]), w_ref[...]) + b_ref[...]


def _mod_call(c, w_ada, b_ada):
    bsz, d = c.shape
    n_out = w_ada.shape[1]
    return pl.pallas_call(
        _mod_kernel,
        out_shape=jax.ShapeDtypeStruct((bsz, n_out), F32),
        grid=(n_out // d,),
        in_specs=[pl.BlockSpec((bsz, d), lambda j: (0, 0)),
                  pl.BlockSpec((d, d), lambda j: (0, j)),
                  pl.BlockSpec((1, d), lambda j: (0, j))],
        out_specs=pl.BlockSpec((bsz, d), lambda j: (0, j)),
        compiler_params=pltpu.CompilerParams(vmem_limit_bytes=VMEM_LIMIT),
        name="mod",
    )(c, w_ada, b_ada.reshape(1, n_out))


def _split_chunk_pairs(tile_even, tile_odd):
    low = lax.broadcasted_iota(I32, tile_even.shape, 1) < CHUNK
    first = jnp.where(low, tile_even, pltpu.roll(tile_odd, CHUNK, 1))
    second = jnp.where(low, pltpu.roll(tile_even, CHUNK, 1), tile_odd)
    return first, second


def _mix_front_kernel(x_ref, mod_ref, win_ref, wut_ref, lb_ref, gn_ref, ltri_ref,
                      ut_ref, oa_ref, proj_ref, st_ref, flat_ref):
    fdim = HGRN_HEADS * HGRN_KDIM
    ts = x_ref.shape[1]
    pairs = wut_ref.shape[0] // 2
    rows = ts // CHUNK

    @pl.when(pl.program_id(1) == 0)
    def _():
        st_ref[...] = jnp.zeros_like(st_ref)

    x = x_ref[0]
    ms = jnp.mean(x * x, axis=-1, keepdims=True)
    h = x * lax.rsqrt(ms + EPS) * (1.0 + mod_ref[0, 1:2, :]) + mod_ref[0, 0:1, :]
    hb = h.astype(BF16)
    proj_ref[...] = jnp.dot(hb, win_ref[...], preferred_element_type=F32)
    u_t = lax.dot_general(wut_ref[...], hb, _NT, preferred_element_type=F32)
    for m in range(ts // LANES):
        first, second = _split_chunk_pairs(u_t[:pairs, m * LANES:(m + 1) * LANES],
                                           u_t[pairs:, m * LANES:(m + 1) * LANES])
        flat_ref[2 * m * pairs:(2 * m + 1) * pairs, :] = first
        flat_ref[(2 * m + 1) * pairs:(2 * m + 2) * pairs, :] = second
    per_group = S5_GROUP // 2
    for q in range(pairs):
        ut_ref[q // per_group, :, (q % per_group) * LANES:(q % per_group + 1) * LANES] = (
            flat_ref[pl.ds(q, rows, stride=pairs), :])

    lb = lb_ref[...]
    gn = gn_ref[...]
    ltri = ltri_ref[...]
    row = lax.broadcasted_iota(I32, (CHUNK, CHUNK), 0)
    col = lax.broadcasted_iota(I32, (CHUNK, CHUNK), 1)
    causal = row >= col

    def chunk_step(ci, carry):
        r0 = pl.multiple_of(ci * CHUNK, CHUNK)
        q = proj_ref[pl.ds(r0, CHUNK), 0:fdim]
        fl = proj_ref[pl.ds(r0, CHUNK), fdim:2 * fdim]
        iv = proj_ref[pl.ds(r0, CHUNK), 2 * fdim:3 * fdim]
        og = proj_ref[pl.ds(r0, CHUNK), 3 * fdim:4 * fdim]
        f = lb + (1.0 - lb) * _sigmoid(fl)
        lf = jnp.log(f)
        lf_hi = lf.astype(BF16)
        rem = lf - lf_hi.astype(F32)
        lf_mid = rem.astype(BF16)
        lf_lo = (rem - lf_mid.astype(F32)).astype(BF16)
        b = (jnp.dot(ltri, lf_hi, preferred_element_type=F32)
             + jnp.dot(ltri, lf_mid, preferred_element_type=F32)
             + jnp.dot(ltri, lf_lo, preferred_element_type=F32))
        b_ref = b[CHUNK // 2 - 1:CHUNK // 2, :]
        b_last = b[CHUNK - 1:CHUNK, :]
        qs = _silu(q)
        kk = 1.0 - f
        qe = (qs * jnp.exp(b - b_ref)).astype(BF16)
        ke = (kk * jnp.exp(b_ref - b)).astype(BF16)
        qb = (qs * jnp.exp(b)).astype(BF16)
        k2 = (kk * jnp.exp(b_last - b)).astype(BF16)
        dec = jnp.exp(b_last)
        ivb = iv.astype(BF16)
        outs = []
        for hh in range(HGRN_HEADS):
            sl = slice(hh * HGRN_KDIM, (hh + 1) * HGRN_KDIM)
            att = lax.dot_general(qe[:, sl], ke[:, sl], _NT, preferred_element_type=F32)
            att = jnp.where(causal, att, 0.0)
            st = st_ref[hh]
            o = jnp.dot(att.astype(BF16), ivb[:, sl], preferred_element_type=F32)
            o = o + lax.dot_general(qb[:, sl], st.astype(BF16), _NT, preferred_element_type=F32)
            st_ref[hh] = st * dec[:, sl] + lax.dot_general(
                ivb[:, sl], k2[:, sl], _TN, preferred_element_type=F32)
            outs.append(o * lax.rsqrt(jnp.mean(o * o, axis=-1, keepdims=True) + EPS))
        o = jnp.concatenate(outs, axis=1) * gn * _silu(og)
        oa_ref[0, pl.ds(r0, CHUNK), :] = o.astype(BF16)
        return carry

    lax.fori_loop(0, ts // CHUNK, chunk_step, 0, unroll=True)


def _mix_front_call(x, mod, w_main, w_ut, lb, gn):
    bsz, seq, d = x.shape
    fdim = HGRN_HEADS * HGRN_KDIM
    ncols = w_main.shape[1]
    nb = w_ut.shape[0]
    groups = nb // S5_GROUP
    ltri = jnp.tril(jnp.ones((CHUNK, CHUNK), BF16))
    ts = SEQ_TILE
    tiles = seq // ts
    rows = ts // CHUNK
    return pl.pallas_call(
        _mix_front_kernel,
        out_shape=(jax.ShapeDtypeStruct((groups, bsz * seq // CHUNK, S5_GROUP * CHUNK), F32),
                   jax.ShapeDtypeStruct((bsz, seq, fdim), BF16)),
        grid=(bsz, tiles),
        in_specs=[pl.BlockSpec((1, ts, d), lambda b, j: (b, j, 0)),
                  pl.BlockSpec((1, 6, d), lambda b, j: (b, 0, 0)),
                  pl.BlockSpec((d, ncols), lambda b, j: (0, 0)),
                  pl.BlockSpec((nb, d), lambda b, j: (0, 0)),
                  pl.BlockSpec((1, fdim), lambda b, j: (0, 0)),
                  pl.BlockSpec((1, fdim), lambda b, j: (0, 0)),
                  pl.BlockSpec((CHUNK, CHUNK), lambda b, j: (0, 0))],
        out_specs=(pl.BlockSpec((groups, rows, S5_GROUP * CHUNK), lambda b, j: (0, b * tiles + j, 0)),
                   pl.BlockSpec((1, ts, fdim), lambda b, j: (b, j, 0))),
        scratch_shapes=[pltpu.VMEM((ts, ncols), F32),
                        pltpu.VMEM((HGRN_HEADS, fdim // HGRN_HEADS, HGRN_KDIM), F32),
                        pltpu.VMEM((rows * nb // 2, LANES), F32)],
        compiler_params=pltpu.CompilerParams(
            dimension_semantics=("arbitrary", "arbitrary"), vmem_limit_bytes=VMEM_LIMIT),
        name="mix_front",
    )(x, mod, w_main, w_ut, lb, gn, ltri)


def _s5_tables(lam_re, lam_im, log_dt, b_re, b_im, c_re, c_im):
    t = CHUNK
    hp = lax.Precision.HIGHEST
    lam = lax.complex(jnp.minimum(lam_re, -1e-4), lam_im)
    lam_dt = lam * jnp.exp(log_dt)[:, None]
    lam_bar = jnp.exp(lam_dt)
    b_bar = ((lam_bar - 1.0) / lam)[..., None] * lax.complex(b_re, b_im)
    c_mat = lax.complex(c_re, c_im)
    taus = jnp.arange(t + 1, dtype=F32)
    lam_pow = jnp.exp(lam_dt[:, None, :] * taus[None, :, None])
    g, p = lam.shape
    c = b_re.shape[-1]
    cl = c_mat[:, None, :, :] * lam_pow[:, :t, None, :]
    kr = (jnp.einsum('gtcp,gpi->gtci', cl.real, b_bar.real, precision=hp)
          - jnp.einsum('gtcp,gpi->gtci', cl.imag, b_bar.imag, precision=hp))
    z = jnp.pad(kr.transpose(0, 3, 2, 1).astype(BF16), ((0, 0), (0, 0), (0, 0), (t - 1, 0)))
    kg = jnp.stack([z[..., t - 1 - s:2 * t - 1 - s] for s in range(t)], axis=2)
    m = kg.reshape(g, c * t, c * t)
    pc = lam_pow[:, t - 1::-1][:, :t, :, None] * b_bar[:, None, :, :]
    pc = pc.transpose(0, 3, 1, 2).reshape(g, c * t, p)
    p_tab = jnp.concatenate([pc.real, pc.imag], axis=-1)
    ql = c_mat[:, None, :, :] * lam_pow[:, 1:t + 1, None, :]
    ql = ql.transpose(0, 3, 2, 1).reshape(g, p, c * t)
    q_tab = jnp.concatenate([ql.real, -ql.imag], axis=1)
    lam_t = lam_pow[:, t]
    a1 = jnp.concatenate([lam_t.real, lam_t.real], axis=-1)[:, None, :]
    a2 = jnp.concatenate([-lam_t.imag, lam_t.imag], axis=-1)[:, None, :]
    return m.astype(BF16), p_tab.astype(BF16), q_tab.astype(BF16), a1, a2


def _s5_kernel(u_ref, m_ref, p_ref, q_ref, a1_ref, a2_ref, y_ref, v_ref, xs_ref, *, n_chunks, n_batch):
    u = u_ref[0].astype(BF16)
    v_ref[...] = jnp.dot(u, p_ref[0], preferred_element_type=F32)
    a1 = a1_ref[0]
    a2 = a2_ref[0]
    half = xs_ref.shape[1] // 2

    def step(n, state):
        xs_ref[pl.ds(n, n_batch, stride=n_chunks), :] = state
        return (a1 * state + a2 * pltpu.roll(state, half, 1)
                + v_ref[pl.ds(n, n_batch, stride=n_chunks), :])

    lax.fori_loop(0, n_chunks, step, jnp.zeros((n_batch, xs_ref.shape[1]), F32))
    y = jnp.dot(u, m_ref[0], preferred_element_type=F32)
    y_ref[0] = y + jnp.dot(xs_ref[...].astype(BF16), q_ref[0], preferred_element_type=F32)


def _s5_call(ut, m, p_tab, q_tab, a1, a2, n_chunks, n_batch):
    g, rows, width = ut.shape
    p2 = p_tab.shape[-1]
    return pl.pallas_call(
        functools.partial(_s5_kernel, n_chunks=n_chunks, n_batch=n_batch),
        out_shape=jax.ShapeDtypeStruct((g, rows, width), F32),
        grid=(g,),
        in_specs=[pl.BlockSpec((1, rows, width), lambda i: (i, 0, 0)),
                  pl.BlockSpec((1, width, width), lambda i: (i, 0, 0)),
                  pl.BlockSpec((1, width, p2), lambda i: (i, 0, 0)),
                  pl.BlockSpec((1, p2, width), lambda i: (i, 0, 0)),
                  pl.BlockSpec((1, 1, p2), lambda i: (i, 0, 0)),
                  pl.BlockSpec((1, 1, p2), lambda i: (i, 0, 0))],
        out_specs=pl.BlockSpec((1, rows, width), lambda i: (i, 0, 0)),
        scratch_shapes=[pltpu.VMEM((rows, p2), F32), pltpu.VMEM((rows, p2), F32)],
        compiler_params=pltpu.CompilerParams(
            dimension_semantics=("arbitrary",), vmem_limit_bytes=VMEM_LIMIT),
        name="s5",
    )(ut, m, p_tab, q_tab, a1, a2)


def _token_major(flat_ref, src_ref):
    groups, rows, _ = src_ref.shape
    per_group = S5_GROUP // 2
    pairs = groups * per_group
    for q in range(pairs):
        flat_ref[pl.ds(q, rows, stride=pairs), :] = (
            src_ref[q // per_group, :, (q % per_group) * LANES:(q % per_group + 1) * LANES])
    tiles = []
    for m in range(rows // 2):
        even, odd = _split_chunk_pairs(flat_ref[2 * m * pairs:(2 * m + 1) * pairs, :],
                                       flat_ref[(2 * m + 1) * pairs:(2 * m + 2) * pairs, :])
        tiles.append(jnp.concatenate([even, odd], axis=0))
    return jnp.concatenate(tiles, axis=1)


def _mix_back_kernel(x_ref, oa_ref, yt_ref, ut_ref, mod_ref, dskip_ref, wglu_ref, bglu_ref,
                     wout_ref, wrh_ref, wrl_ref,
                     x1_ref, hp_ref, lt_ref, flat_ref):
    na = oa_ref.shape[1]
    y_t = _token_major(flat_ref, yt_ref)
    u_t = _token_major(flat_ref, ut_ref)
    z_t = jax.nn.gelu(y_t + dskip_ref[...] * u_t)
    gate_t = _sigmoid(jnp.dot(wglu_ref[...], z_t.astype(BF16), preferred_element_type=F32)
                      + bglu_ref[...])
    ob_t = (z_t * gate_t).astype(BF16)
    mixed = (jnp.dot(oa_ref[...], wout_ref[0:na, :], preferred_element_type=F32)
             + lax.dot_general(ob_t, wout_ref[na:, :], _TN, preferred_element_type=F32))
    x1 = x_ref[...] + mod_ref[0, 2:3, :] * mixed
    ms = jnp.mean(x1 * x1, axis=-1, keepdims=True)
    h2 = x1 * lax.rsqrt(ms + EPS) * (1.0 + mod_ref[0, 4:5, :]) + mod_ref[0, 3:4, :]
    _store_packed(hp_ref, h2, h2.shape[0])
    h_hi = h2.astype(BF16)
    h_lo = (h2 - h_hi.astype(F32)).astype(BF16)
    lt = lax.dot_general(wrh_ref[...], h_hi, _NT, preferred_element_type=F32)
    lt = lt + lax.dot_general(wrl_ref[...], h_hi, _NT, preferred_element_type=F32)
    lt = lt + lax.dot_general(wrh_ref[...], h_lo, _NT, preferred_element_type=F32)
    lt_ref[...] = lt
    x1_ref[...] = x1


def _mix_back_call(x2d, oa, yt, ut, mod, dskip, wglu, bglu, wout, wrh, wrl, seq):
    n, d = x2d.shape
    nb = oa.shape[1]
    ne = wrh.shape[0]
    tm = TOK_TILE
    per_b = seq // tm
    groups, _, width = yt.shape
    rows = tm // CHUNK
    const = lambda i: (0, 0)
    flat_block = pl.BlockSpec((groups, rows, width), lambda i: (0, i, 0))
    return pl.pallas_call(
        _mix_back_kernel,
        out_shape=(jax.ShapeDtypeStruct((n, d), F32),
                   jax.ShapeDtypeStruct((n * PACK_ROWS, LANES), I32),
                   jax.ShapeDtypeStruct((ne, n), F32)),
        grid=(n // tm,),
        in_specs=[pl.BlockSpec((tm, d), lambda i: (i, 0)),
                  pl.BlockSpec((tm, nb), lambda i: (i, 0)),
                  flat_block,
                  flat_block,
                  pl.BlockSpec((1, 6, d), lambda i: (i // per_b, 0, 0)),
                  pl.BlockSpec((nb, 1), const),
                  pl.BlockSpec((nb, nb), const),
                  pl.BlockSpec((nb, 1), const),
                  pl.BlockSpec((d, d), const),
                  pl.BlockSpec((ne, d), const),
                  pl.BlockSpec((ne, d), const)],
        out_specs=(pl.BlockSpec((tm, d), lambda i: (i, 0)),
                   pl.BlockSpec((tm * PACK_ROWS, LANES), lambda i: (i, 0)),
                   pl.BlockSpec((ne, tm), lambda i: (0, i))),
        scratch_shapes=[pltpu.VMEM((rows * nb // 2, LANES), F32)],
        compiler_params=pltpu.CompilerParams(
            dimension_semantics=("arbitrary",), vmem_limit_bytes=VMEM_LIMIT),
        name="mix_back",
    )(x2d, oa, yt, ut, mod, dskip, wglu, bglu, wout, wrh, wrl)


def _route_kernel(lt_ref, bias_ref, su_ref, idx_ref, w_ref, rank_ref, cnt_ref, run_ref):
    ne, tr = lt_ref.shape
    per_group = ne // N_EXPERT_GROUPS
    neg = -jnp.inf

    @pl.when(pl.program_id(0) == 0)
    def _():
        run_ref[...] = jnp.zeros_like(run_ref)

    s = _sigmoid(lt_ref[...])
    sel = s + bias_ref[...]
    gio = lax.broadcasted_iota(I32, (per_group, tr), 0)
    gscore = []
    for g in range(N_EXPERT_GROUPS):
        v = sel[g * per_group:(g + 1) * per_group, :]
        m1 = jnp.max(v, axis=0, keepdims=True)
        i1 = jnp.min(jnp.where(v == m1, gio, per_group), axis=0, keepdims=True)
        m2 = jnp.max(jnp.where(gio == i1, neg, v), axis=0, keepdims=True)
        gscore.append(m1 + m2)
    masked = []
    for g in range(N_EXPERT_GROUPS):
        ahead = jnp.zeros((1, tr), I32)
        for o in range(N_EXPERT_GROUPS):
            if o == g:
                continue
            wins = (gscore[o] >= gscore[g]) if o < g else (gscore[o] > gscore[g])
            ahead = ahead + wins.astype(I32)
        keep = ahead < TOPK_GROUPS
        masked.append(jnp.where(keep, sel[g * per_group:(g + 1) * per_group, :], neg))
    selm = jnp.concatenate(masked, axis=0)
    eio = lax.broadcasted_iota(I32, (ne, tr), 0)
    candidate = selm > neg
    idxs, ws = [], []
    for k in range(TOP_K):
        m = jnp.max(selm, axis=0, keepdims=True)
        ik = jnp.min(jnp.where(selm == m, eio, ne), axis=0, keepdims=True)
        onehot = eio == ik
        ws.append(jnp.sum(jnp.where(onehot, s, 0.0), axis=0, keepdims=True))
        selm = jnp.where(onehot, neg, selm)
        idxs.append(ik)
    hits = jnp.where(jnp.logical_and(candidate, selm == neg), 1.0, 0.0)
    wsum = ws[0]
    for k in range(1, TOP_K):
        wsum = wsum + ws[k]
    scale = ROUTE_SCALE / wsum
    ranks = jnp.dot(hits.astype(BF16), su_ref[...], preferred_element_type=F32) + run_ref[...]
    for k in range(TOP_K):
        idx_ref[k:k + 1, :] = idxs[k]
        w_ref[k:k + 1, :] = ws[k] * scale
        rk = jnp.sum(jnp.where(eio == idxs[k], ranks, 0.0), axis=0, keepdims=True)
        rank_ref[k:k + 1, :] = rk.astype(I32)
    run_ref[...] = run_ref[...] + jnp.sum(hits, axis=1, keepdims=True)
    cnt_ref[...] = run_ref[...]


def _route_call(lt, bias):
    ne, n = lt.shape
    tr = ROUTE_TILE
    su = jnp.triu(jnp.ones((tr, tr), F32), k=1).astype(BF16)
    return pl.pallas_call(
        _route_kernel,
        out_shape=(jax.ShapeDtypeStruct((TOP_K, n), I32),
                   jax.ShapeDtypeStruct((TOP_K, n), F32),
                   jax.ShapeDtypeStruct((TOP_K, n), I32),
                   jax.ShapeDtypeStruct((ne, 1), F32)),
        grid=(n // tr,),
        in_specs=[pl.BlockSpec((ne, tr), lambda i: (0, i)),
                  pl.BlockSpec((ne, 1), lambda i: (0, 0)),
                  pl.BlockSpec((tr, tr), lambda i: (0, 0))],
        out_specs=(pl.BlockSpec((TOP_K, tr), lambda i: (0, i)),
                   pl.BlockSpec((TOP_K, tr), lambda i: (0, i)),
                   pl.BlockSpec((TOP_K, tr), lambda i: (0, i)),
                   pl.BlockSpec((ne, 1), lambda i: (0, 0))),
        scratch_shapes=[pltpu.VMEM((ne, 1), F32)],
        compiler_params=pltpu.CompilerParams(
            dimension_semantics=("arbitrary",), vmem_limit_bytes=VMEM_LIMIT),
        name="route",
    )(lt, bias.reshape(ne, 1), su)


def _pos_kernel(idx_ref, rank_ref, st_ref, pos_ref):
    ne = st_ref.shape[0]
    tp = idx_ref.shape[1]
    eio = lax.broadcasted_iota(I32, (ne, tp), 0)
    st = st_ref[...]
    for k in range(TOP_K):
        base = jnp.sum(jnp.where(eio == idx_ref[k:k + 1, :], st, 0), axis=0, keepdims=True)
        pos_ref[k:k + 1, :] = base + rank_ref[k:k + 1, :]


def _pos_call(top_idx, rank, starts):
    kk, n = top_idx.shape
    ne = starts.shape[0]
    tp = POS_TILE
    return pl.pallas_call(
        _pos_kernel,
        out_shape=jax.ShapeDtypeStruct((kk, n), I32),
        grid=(n // tp,),
        in_specs=[pl.BlockSpec((kk, tp), lambda i: (0, i)),
                  pl.BlockSpec((kk, tp), lambda i: (0, i)),
                  pl.BlockSpec((ne, 1), lambda i: (0, 0))],
        out_specs=pl.BlockSpec((kk, tp), lambda i: (0, i)),
        compiler_params=pltpu.CompilerParams(
            dimension_semantics=("arbitrary",), vmem_limit_bytes=VMEM_LIMIT),
        name="pos",
    )(top_idx, rank, starts.reshape(ne, 1))


def _sc_mesh():
    return plsc.VectorSubcoreMesh(core_axis_name="c", subcore_axis_name="s",
                                  num_cores=SC_CORES, num_subcores=SC_SUBCORES)


def _sc_worker():
    return lax.axis_index("s") * SC_CORES + lax.axis_index("c")


def _dispatch_call(pos_win, hp):
    n = hp.shape[0]
    n_workers = SC_CORES * SC_SUBCORES
    wins_per_worker = n // SC_WINDOW // n_workers

    def body(hp_hbm, pos_hbm, xs_hbm, idx_a, idx_b, rows_a, rows_b, load_a, load_b, scatter_sem):
        first_win = _sc_worker() * wins_per_worker
        idx_v, rows_v, load_sem = (idx_a, idx_b), (rows_a, rows_b), (load_a, load_b)

        def loads(win, slot):
            return (pltpu.make_async_copy(hp_hbm.at[pl.ds(win * SC_WINDOW, SC_WINDOW)], rows_v[slot],
                                          load_sem[slot]),
                    pltpu.make_async_copy(pos_hbm.at[win], idx_v[slot], load_sem[slot]))

        for cp in loads(first_win, 0):
            cp.start()

        @pl.loop(0, wins_per_worker, step=2)
        def _(w):
            for slot in range(2):
                win = first_win + w + slot
                for cp in loads(win, slot):
                    cp.wait()

                @pl.when(w + slot + 1 < wins_per_worker)
                def _():
                    for cp in loads(win + 1, 1 - slot):
                        cp.start()

                copies = [pltpu.async_copy(rows_v[slot], xs_hbm.at[idx_v[slot].at[k]], scatter_sem)
                          for k in range(TOP_K)]
                for cp in copies:
                    cp.wait()

    idx_buf = pltpu.VMEM((TOP_K, SC_WINDOW), I32)
    row_buf = pltpu.VMEM((SC_WINDOW,) + hp.shape[1:], hp.dtype)
    return pl.kernel(
        body,
        out_type=jax.ShapeDtypeStruct((n * TOP_K,) + hp.shape[1:], hp.dtype),
        mesh=_sc_mesh(),
        scratch_types=[idx_buf, idx_buf, row_buf, row_buf,
                       pltpu.SemaphoreType.DMA, pltpu.SemaphoreType.DMA, pltpu.SemaphoreType.DMA],
        name="dispatch",
    )(hp, pos_win)


def _collect_call(pos_win, y_sorted):
    n = pos_win.shape[0] * SC_WINDOW
    n_workers = SC_CORES * SC_SUBCORES
    wins_per_worker = n // SC_WINDOW // n_workers

    def body(ys_hbm, pos_hbm, out_hbm, idx_v, rows_a, rows_b, gather_sem, write_sem):
        first_win = _sc_worker() * wins_per_worker
        bufs = (rows_a, rows_b)

        @pl.loop(0, wins_per_worker)
        def _(w):
            win = first_win + w
            pltpu.sync_copy(pos_hbm.at[win], idx_v)

            def gather(k):
                return pltpu.async_copy(ys_hbm.at[idx_v.at[k]], bufs[k % 2], gather_sem)

            pending_gather = gather(0)
            pending_write = None
            for k in range(TOP_K):
                pending_gather.wait()
                if pending_write is not None:
                    for cp in pending_write:
                        cp.wait()
                if k + 1 < TOP_K:
                    pending_gather = gather(k + 1)
                pending_write = [
                    pltpu.async_copy(bufs[k % 2].at[:, j],
                                     out_hbm.at[j, k, pl.ds(win * SC_WINDOW, SC_WINDOW)], write_sem)
                    for j in range(PACK_ROWS)]
            for cp in pending_write:
                cp.wait()

    row_buf = pltpu.VMEM((SC_WINDOW,) + y_sorted.shape[1:], y_sorted.dtype)
    return pl.kernel(
        body,
        out_type=jax.ShapeDtypeStruct((PACK_ROWS, TOP_K, n, LANES), y_sorted.dtype),
        mesh=_sc_mesh(),
        scratch_types=[pltpu.VMEM((TOP_K, SC_WINDOW), I32), row_buf, row_buf,
                       pltpu.SemaphoreType.DMA, pltpu.SemaphoreType.DMA],
        name="collect",
    )(y_sorted, pos_win)


def _experts_kernel(pe_ref, pb_ref, pv_ref, st_ref, en_ref,
                    xs_hbm, wg_ref, wu_ref, wd_ref, y_hbm,
                    wgb_ref, wub_ref, wdb_ref, xbuf_ref, stage_ref,
                    in_sems, out_sems, state_ref):
    p = pl.program_id(0)
    n_pairs = pl.num_programs(0)
    e = pe_ref[p]
    blk = pb_ref[p]
    rb = xbuf_ref.shape[2]
    prev = jnp.maximum(p - 1, 0)
    nxt = jnp.minimum(p + 1, n_pairs - 1)
    live = pv_ref[p] == 1
    first = jnp.logical_or(p == 0, pb_ref[prev] != blk)
    block_ends = jnp.logical_or(pb_ref[nxt] != blk, pv_ref[nxt] == 0)
    last = jnp.logical_or(p == n_pairs - 1, block_ends)

    def in_copy(j, block, slot):
        return pltpu.make_async_copy(xs_hbm.at[pl.ds(block * rb, rb), j], xbuf_ref.at[slot, j],
                                     in_sems.at[slot])

    def out_copy(j, block, slot):
        return pltpu.make_async_copy(stage_ref.at[slot, j], y_hbm.at[pl.ds(block * rb, rb), j],
                                     out_sems.at[slot])

    def drain(slot):
        @pl.when(state_ref[1 + slot] == 1)
        def _():
            for j in range(PACK_ROWS):
                out_copy(j, 0, slot).wait()
            state_ref[1 + slot] = 0

    @pl.when(p == 0)
    def _():
        state_ref[0] = 0
        state_ref[1] = 0
        state_ref[2] = 0
        stage_ref[...] = jnp.zeros_like(stage_ref)
        for j in range(PACK_ROWS):
            in_copy(j, blk, 0).start()

    slot = jnp.where(jnp.logical_and(first, p > 0), 1 - state_ref[0], state_ref[0])
    state_ref[0] = slot

    @pl.when(jnp.logical_and(p < n_pairs - 1,
                             jnp.logical_and(pb_ref[nxt] != blk, pv_ref[nxt] == 1)))
    def _():
        for j in range(PACK_ROWS):
            in_copy(j, pb_ref[nxt], 1 - slot).start()

    @pl.when(first)
    def _():
        for j in range(PACK_ROWS):
            in_copy(j, blk, slot).wait()
        drain(slot)

    @pl.when(jnp.logical_or(p == 0, pe_ref[prev] != e))
    def _():
        wgb_ref[...] = wg_ref[0].astype(BF16)
        wub_ref[...] = wu_ref[0].astype(BF16)
        wdb_ref[...] = wd_ref[0].astype(BF16)

    lo_row = st_ref[e]
    hi_row = en_ref[e]

    def sub_block(s, row0, shared):
        sub = pl.ds(s * SUB_BLOCK, SUB_BLOCK)
        pieces = []
        for j in range(PACK_ROWS):
            w = xbuf_ref[slot, j, sub, :]
            pieces.append(lax.bitcast_convert_type(w.astype(jnp.int16), BF16))
            pieces.append(lax.bitcast_convert_type(
                lax.shift_right_logical(w, 16).astype(jnp.int16), BF16))
        xb = jnp.concatenate(pieces, axis=1)
        gate = jnp.dot(xb, wgb_ref[...], preferred_element_type=F32)
        up = jnp.dot(xb, wub_ref[...], preferred_element_type=F32)
        yb = jnp.dot((_silu(gate) * up).astype(BF16), wdb_ref[...], preferred_element_type=F32)
        if shared:
            rows = row0 + lax.broadcasted_iota(I32, (SUB_BLOCK, 1), 0)
            mine = jnp.logical_and(rows >= lo_row, rows < hi_row)
        for j in range(PACK_ROWS):
            word = pltpu.pack_elementwise(
                [yb[:, 2 * j * LANES:(2 * j + 1) * LANES], yb[:, (2 * j + 1) * LANES:(2 * j + 2) * LANES]],
                packed_dtype=BF16)
            if shared:
                word = jnp.where(mine, word, stage_ref[slot, j, sub, :])
            stage_ref[slot, j, sub, :] = word

    for s in range(rb // SUB_BLOCK):
        row0 = blk * rb + s * SUB_BLOCK
        touched = jnp.logical_and(live, jnp.logical_and(row0 < hi_row, row0 + SUB_BLOCK > lo_row))
        whole = jnp.logical_and(lo_row <= row0, hi_row >= row0 + SUB_BLOCK)

        @pl.when(jnp.logical_and(touched, whole))
        def _():
            sub_block(s, row0, shared=False)

        @pl.when(jnp.logical_and(touched, jnp.logical_not(whole)))
        def _():
            sub_block(s, row0, shared=True)

    @pl.when(jnp.logical_and(live, last))
    def _():
        for j in range(PACK_ROWS):
            out_copy(j, blk, slot).start()
        state_ref[1 + slot] = 1

    @pl.when(p == n_pairs - 1)
    def _():
        drain(0)
        drain(1)


def _experts_call(pair_e, pair_blk, pair_ok, starts, ends, xs, w_gate, w_up, w_down):
    ne, d, de = w_gate.shape
    rb = ROW_BLOCK
    n_pairs = pair_e.shape[0]
    grid_spec = pltpu.PrefetchScalarGridSpec(
        num_scalar_prefetch=5,
        grid=(n_pairs,),
        in_specs=[pl.BlockSpec(memory_space=pl.ANY),
                  pl.BlockSpec((1, d, de), lambda p, pe, pb, pv, st, en: (pe[p], 0, 0)),
                  pl.BlockSpec((1, d, de), lambda p, pe, pb, pv, st, en: (pe[p], 0, 0)),
                  pl.BlockSpec((1, de, d), lambda p, pe, pb, pv, st, en: (pe[p], 0, 0))],
        out_specs=pl.BlockSpec(memory_space=pl.ANY),
        scratch_shapes=[pltpu.VMEM((d, de), BF16), pltpu.VMEM((d, de), BF16),
                        pltpu.VMEM((de, d), BF16),
                        pltpu.VMEM((2, PACK_ROWS, rb, LANES), I32),
                        pltpu.VMEM((2, PACK_ROWS, rb, LANES), I32),
                        pltpu.SemaphoreType.DMA((2,)), pltpu.SemaphoreType.DMA((2,)),
                        pltpu.SMEM((3,), I32)],
    )
    return pl.pallas_call(
        _experts_kernel,
        out_shape=jax.ShapeDtypeStruct(xs.shape, xs.dtype),
        grid_spec=grid_spec,
        compiler_params=pltpu.CompilerParams(
            dimension_semantics=("arbitrary",), vmem_limit_bytes=VMEM_LIMIT),
        name="experts",
    )(pair_e, pair_blk, pair_ok, starts, ends, xs, w_gate, w_up, w_down)


def _combine_kernel(y0_ref, y1_ref, y2_ref, y3_ref, x1_ref, hp_ref, w_ref, g2_ref, fg_ref,
                    wsg_ref, wsu_ref, wsd_ref, o_ref):
    tm = x1_ref.shape[0]
    w = w_ref[...]
    parts = []
    for y_ref in (y0_ref, y1_ref, y2_ref, y3_ref):
        for half in range(2):
            acc = None
            for k in range(TOP_K):
                piece = pltpu.unpack_elementwise(y_ref[k], index=half, packed_dtype=BF16,
                                                 unpacked_dtype=F32) * w[:, k:k + 1]
                acc = piece if acc is None else acc + piece
            parts.append(acc)
    routed = jnp.concatenate(parts, axis=1)
    pieces = []
    for j in range(PACK_ROWS):
        word = hp_ref[pl.ds(j, tm, stride=PACK_ROWS), :]
        pieces.append(lax.bitcast_convert_type(word.astype(jnp.int16), BF16))
        pieces.append(lax.bitcast_convert_type(lax.shift_right_logical(word, 16).astype(jnp.int16), BF16))
    h2 = jnp.concatenate(pieces, axis=1)
    hid = _silu(jnp.dot(h2, wsg_ref[...], preferred_element_type=F32)) * jnp.dot(
        h2, wsu_ref[...], preferred_element_type=F32)
    shared = jnp.dot(hid.astype(BF16), wsd_ref[...], preferred_element_type=F32)
    x2 = x1_ref[...] + g2_ref[0] * (routed + shared)
    ms = jnp.mean(x2 * x2, axis=-1, keepdims=True)
    o_ref[...] = x2 * lax.rsqrt(ms + EPS) * fg_ref[...]


def _combine_kernel_into(*refs):
    _combine_kernel(*refs[:-2], refs[-1])


def _combine_call(y_tok, x1, hp, w_tok, g2, fgain, wsg, wsu, wsd, seq, part, prev_out):
    n, d = x1.shape
    dsh = wsg.shape[1]
    tm = MOVE_TILE
    per_b = seq // tm
    tiles = y_tok.shape[2] // tm
    t0 = part * tiles
    const = lambda i: (0, 0)

    def piece_spec(j):
        return pl.BlockSpec((None, TOP_K, tm, LANES), lambda i: (j, 0, i, 0))

    in_specs = [piece_spec(j) for j in range(PACK_ROWS)] + [
        pl.BlockSpec((tm, d), lambda i: (t0 + i, 0)),
        pl.BlockSpec((tm * PACK_ROWS, LANES), lambda i: (t0 + i, 0)),
        pl.BlockSpec((tm, TOP_K), lambda i: (t0 + i, 0)),
        pl.BlockSpec((1, 1, d), lambda i: ((t0 + i) // per_b, 0, 0)),
        pl.BlockSpec((1, d), const),
        pl.BlockSpec((d, dsh), const),
        pl.BlockSpec((d, dsh), const),
        pl.BlockSpec((dsh, d), const)]
    args = [y_tok, y_tok, y_tok, y_tok, x1, hp, w_tok, g2, fgain, wsg, wsu, wsd]
    aliases = {}
    body = _combine_kernel
    if prev_out is not None:
        in_specs.append(pl.BlockSpec(memory_space=pl.ANY))
        args.append(prev_out)
        aliases = {len(args) - 1: 0}
        body = _combine_kernel_into
    return pl.pallas_call(
        body,
        out_shape=jax.ShapeDtypeStruct((n, d), F32),
        grid=(tiles,),
        in_specs=in_specs,
        out_specs=pl.BlockSpec((tm, d), lambda i: (t0 + i, 0)),
        input_output_aliases=aliases,
        compiler_params=pltpu.CompilerParams(
            dimension_semantics=("arbitrary",), vmem_limit_bytes=VMEM_LIMIT),
        name="combine",
    )(*args)


def _pair_tables(counts, n_rows):
    ne = counts.shape[0]
    sizes = counts.astype(I32)
    ends = jnp.cumsum(sizes)
    starts = ends - sizes
    first_blk = starts // ROW_BLOCK
    last_blk = (ends - 1) // ROW_BLOCK
    n_pairs = jnp.where(sizes > 0, last_blk - first_blk + 1, 0)
    pair_end = jnp.cumsum(n_pairs)
    pair_start = pair_end - n_pairs
    max_pairs = n_rows // ROW_BLOCK + ne
    p = jnp.arange(max_pairs, dtype=I32)
    ok = p < pair_end[-1]
    pc = jnp.minimum(p, pair_end[-1] - 1)
    pair_e = jnp.minimum(jnp.searchsorted(pair_end, pc, side='right'), ne - 1).astype(I32)
    pair_blk = (first_blk[pair_e] + pc - pair_start[pair_e]).astype(I32)
    return pair_e, pair_blk, ok.astype(I32), starts.astype(I32), ends.astype(I32)


def kernel(x, c, w_ada, b_ada, w_in, lb_logits, hgrn_norm, lam_re, lam_im, log_dt, b_re, b_im,
           c_re, c_im, d_skip, w_glu, b_glu, w_out, w_router, router_bias, w_gate, w_up, w_down,
           ws_gate, ws_up, ws_down, final_gain):
    bsz, seq, d = x.shape
    n = bsz * seq
    fdim = HGRN_HEADS * HGRN_KDIM
    n_chunks = seq // CHUNK
    lb = jnp.cumsum(jax.nn.softmax(lb_logits.astype(F32), axis=0), axis=0)[0].reshape(1, fdim)

    mod = _mod_call(c, w_ada[0], b_ada[0]).reshape(bsz, 6, d)
    nb = w_in.shape[2] - 4 * fdim
    perm = jnp.concatenate([jnp.arange(0, nb, 2), jnp.arange(1, nb, 2)])
    w_in_b = w_in[0].astype(BF16)
    ut, out_a = _mix_front_call(x, mod, w_in_b[:, :4 * fdim], w_in_b[:, 4 * fdim:][:, perm].T, lb,
                                hgrn_norm[0].reshape(1, fdim))

    m_tab, p_tab, q_tab, a1, a2 = _s5_tables(lam_re[0], lam_im[0], log_dt[0], b_re[0], b_im[0],
                                             c_re[0], c_im[0])
    yt = _s5_call(ut, m_tab, p_tab, q_tab, a1, a2, n_chunks, bsz)

    wr_t = w_router[0].T
    wr_hi = wr_t.astype(BF16)
    wr_lo = (wr_t - wr_hi.astype(F32)).astype(BF16)
    w_out_b = w_out[0].astype(BF16)
    w_out_p = jnp.concatenate([w_out_b[:fdim], w_out_b[fdim:][perm]], axis=0)
    x1, hp, logits_t = _mix_back_call(
        x.reshape(n, d), out_a.reshape(n, fdim), yt, ut, mod,
        d_skip[0][perm].reshape(nb, 1), w_glu[0][perm][:, perm].T.astype(BF16),
        b_glu[0][perm].reshape(nb, 1), w_out_p, wr_hi, wr_lo, seq)

    top_idx, top_w, rank, counts = _route_call(logits_t, router_bias[0])
    pair_e, pair_blk, pair_ok, starts, ends = _pair_tables(counts[:, 0], n * TOP_K)
    pos = _pos_call(top_idx, rank, starts)

    pos_win = pos.reshape(TOP_K, n // SC_WINDOW, SC_WINDOW).transpose(1, 0, 2)
    xs = _dispatch_call(pos_win, hp.reshape(n, PACK_ROWS, LANES))
    y_sorted = _experts_call(pair_e, pair_blk, pair_ok, starts, ends, xs, w_gate[0], w_up[0], w_down[0])
    w_tok = top_w.T
    g2 = mod[:, 5:6, :]
    fgain = final_gain.reshape(1, d)
    wsg, wsu, wsd = ws_gate[0].astype(BF16), ws_up[0].astype(BF16), ws_down[0].astype(BF16)
    wins = pos_win.shape[0] // TAIL_PARTS
    out = None
    for part in range(TAIL_PARTS):
        y_tok = _collect_call(pos_win[part * wins:(part + 1) * wins], y_sorted)
        out = _combine_call(y_tok, x1, hp, w_tok, g2, fgain, wsg, wsu, wsd, seq, part, out)
    return out.reshape(bsz, seq, d)
```

```python
import functools

import jax
import jax.numpy as jnp
from jax import lax
from jax.experimental import pallas as pl
from jax.experimental.pallas import tpu as pltpu
from jax.experimental.pallas import tpu_sc as plsc

F32 = jnp.float32
BF16 = jnp.bfloat16
I32 = jnp.int32

EPS = 1e-6
CHUNK = 64
HGRN_HEADS = 4
HGRN_KDIM = 128
S5_GROUP = 16
S5_STATE = 64
N_EXPERT_GROUPS = 8
TOPK_GROUPS = 4
TOP_K = 8
ROUTE_SCALE = 2.5
LANES = 128
PACK_ROWS = 4
SC_CORES = 2
SC_SUBCORES = 16
SC_WINDOW = 64

SEQ_TILE = 512
TOK_TILE = 512
ROUTE_TILE = 512
POS_TILE = 1024
MOVE_TILE = 512
ROW_BLOCK = 2048
SUB_BLOCK = 512
TAIL_PARTS = 4
VMEM_LIMIT = 56 * 1024 * 1024

_NT = (((1,), (1,)), ((), ()))
_TN = (((0,), (0,)), ((), ()))


def _sigmoid(v):
    return 0.5 * jnp.tanh(0.5 * v) + 0.5


def _silu(v):
    return v * _sigmoid(v)


def _bdot(a, b):
    return jnp.dot(a.astype(BF16), b.astype(BF16), preferred_element_type=F32)


def _store_packed(ref, val, n_rows):
    for j in range(PACK_ROWS):
        lo = val[:, 2 * j * LANES:(2 * j + 1) * LANES]
        hi = val[:, (2 * j + 1) * LANES:(2 * j + 2) * LANES]
        ref[pl.ds(j, n_rows, stride=PACK_ROWS), :] = pltpu.pack_elementwise([lo, hi], packed_dtype=BF16)


def _mod_kernel(c_ref, w_ref, b_ref, o_ref):
    o_ref[...] = _bdot(_silu(c_ref[...]), w_ref[...]) + b_ref[...]


def _mod_call(c, w_ada, b_ada):
    bsz, d = c.shape
    n_out = w_ada.shape[1]
    return pl.pallas_call(
        _mod_kernel,
        out_shape=jax.ShapeDtypeStruct((bsz, n_out), F32),
        grid=(n_out // d,),
        in_specs=[pl.BlockSpec((bsz, d), lambda j: (0, 0)),
                  pl.BlockSpec((d, d), lambda j: (0, j)),
                  pl.BlockSpec((1, d), lambda j: (0, j))],
        out_specs=pl.BlockSpec((bsz, d), lambda j: (0, j)),
        compiler_params=pltpu.CompilerParams(vmem_limit_bytes=VMEM_LIMIT),
        name="mod",
    )(c, w_ada, b_ada.reshape(1, n_out))


def _split_chunk_pairs(tile_even, tile_odd):
    low = lax.broadcasted_iota(I32, tile_even.shape, 1) < CHUNK
    first = jnp.where(low, tile_even, pltpu.roll(tile_odd, CHUNK, 1))
    second = jnp.where(low, pltpu.roll(tile_even, CHUNK, 1), tile_odd)
    return first, second


def _mix_front_kernel(x_ref, mod_ref, win_ref, wut_ref, lb_ref, gn_ref, ltri_ref,
                      ut_ref, oa_ref, ucm_ref, proj_ref, st_ref, flat_ref):
    fdim = HGRN_HEADS * HGRN_KDIM
    ts = x_ref.shape[1]
    pairs = wut_ref.shape[0] // 2
    rows = ts // CHUNK

    @pl.when(pl.program_id(1) == 0)
    def _():
        st_ref[...] = jnp.zeros_like(st_ref)

    x = x_ref[0]
    ms = jnp.mean(x * x, axis=-1, keepdims=True)
    h = x * lax.rsqrt(ms + EPS) * (1.0 + mod_ref[0, 1:2, :]) + mod_ref[0, 0:1, :]
    hb = h.astype(BF16)
    proj_ref[...] = jnp.dot(hb, win_ref[...], preferred_element_type=F32)
    u_t = lax.dot_general(wut_ref[...], hb, _NT, preferred_element_type=F32)
    ucm_ref[...] = u_t.astype(BF16)
    for m in range(ts // LANES):
        first, second = _split_chunk_pairs(u_t[:pairs, m * LANES:(m + 1) * LANES],
                                           u_t[pairs:, m * LANES:(m + 1) * LANES])
        flat_ref[2 * m * pairs:(2 * m + 1) * pairs, :] = first
        flat_ref[(2 * m + 1) * pairs:(2 * m + 2) * pairs, :] = second
    per_group = S5_GROUP // 2
    for q in range(pairs):
        ut_ref[q // per_group, :, (q % per_group) * LANES:(q % per_group + 1) * LANES] = (
            flat_ref[pl.ds(q, rows, stride=pairs), :])

    lb = lb_ref[...]
    gn = gn_ref[...]
    ltri = ltri_ref[...]
    row = lax.broadcasted_iota(I32, (CHUNK, CHUNK), 0)
    col = lax.broadcasted_iota(I32, (CHUNK, CHUNK), 1)
    causal = row >= col

    def chunk_step(ci, carry):
        r0 = pl.multiple_of(ci * CHUNK, CHUNK)
        q = proj_ref[pl.ds(r0, CHUNK), 0:fdim]
        fl = proj_ref[pl.ds(r0, CHUNK), fdim:2 * fdim]
        iv = proj_ref[pl.ds(r0, CHUNK), 2 * fdim:3 * fdim]
        og = proj_ref[pl.ds(r0, CHUNK), 3 * fdim:4 * fdim]
        f = lb + (1.0 - lb) * _sigmoid(fl)
        lf = jnp.log(f)
        lf_hi = lf.astype(BF16)
        rem = lf - lf_hi.astype(F32)
        lf_mid = rem.astype(BF16)
        lf_lo = (rem - lf_mid.astype(F32)).astype(BF16)
        b = (jnp.dot(ltri, lf_hi, preferred_element_type=F32)
             + jnp.dot(ltri, lf_mid, preferred_element_type=F32)
             + jnp.dot(ltri, lf_lo, preferred_element_type=F32))
        b_ref = b[CHUNK // 2 - 1:CHUNK // 2, :]
        b_last = b[CHUNK - 1:CHUNK, :]
        qs = _silu(q)
        kk = 1.0 - f
        qe = (qs * jnp.exp(b - b_ref)).astype(BF16)
        ke = (kk * jnp.exp(b_ref - b)).astype(BF16)
        qb = (qs * jnp.exp(b)).astype(BF16)
        k2 = (kk * jnp.exp(b_last - b)).astype(BF16)
        dec = jnp.exp(b_last)
        ivb = iv.astype(BF16)
        outs = []
        for hh in range(HGRN_HEADS):
            sl = slice(hh * HGRN_KDIM, (hh + 1) * HGRN_KDIM)
            att = lax.dot_general(qe[:, sl], ke[:, sl], _NT, preferred_element_type=F32)
            att = jnp.where(causal, att, 0.0)
            st = st_ref[hh]
            o = jnp.dot(att.astype(BF16), ivb[:, sl], preferred_element_type=F32)
            o = o + lax.dot_general(qb[:, sl], st.astype(BF16), _NT, preferred_element_type=F32)
            st_ref[hh] = st * dec[:, sl] + lax.dot_general(
                ivb[:, sl], k2[:, sl], _TN, preferred_element_type=F32)
            outs.append(o * lax.rsqrt(jnp.mean(o * o, axis=-1, keepdims=True) + EPS))
        o = jnp.concatenate(outs, axis=1) * gn * _silu(og)
        oa_ref[0, pl.ds(r0, CHUNK), :] = o.astype(BF16)
        return carry

    lax.fori_loop(0, ts // CHUNK, chunk_step, 0, unroll=True)


def _mix_front_call(x, mod, w_main, w_ut, lb, gn):
    bsz, seq, d = x.shape
    fdim = HGRN_HEADS * HGRN_KDIM
    ncols = w_main.shape[1]
    nb = w_ut.shape[0]
    groups = nb // S5_GROUP
    ltri = jnp.tril(jnp.ones((CHUNK, CHUNK), BF16))
    ts = SEQ_TILE
    tiles = seq // ts
    rows = ts // CHUNK
    return pl.pallas_call(
        _mix_front_kernel,
        out_shape=(jax.ShapeDtypeStruct((groups, bsz * seq // CHUNK, S5_GROUP * CHUNK), F32),
                   jax.ShapeDtypeStruct((bsz, seq, fdim), BF16),
                   jax.ShapeDtypeStruct((nb, bsz * seq), BF16)),
        grid=(bsz, tiles),
        in_specs=[pl.BlockSpec((1, ts, d), lambda b, j: (b, j, 0)),
                  pl.BlockSpec((1, 6, d), lambda b, j: (b, 0, 0)),
                  pl.BlockSpec((d, ncols), lambda b, j: (0, 0)),
                  pl.BlockSpec((nb, d), lambda b, j: (0, 0)),
                  pl.BlockSpec((1, fdim), lambda b, j: (0, 0)),
                  pl.BlockSpec((1, fdim), lambda b, j: (0, 0)),
                  pl.BlockSpec((CHUNK, CHUNK), lambda b, j: (0, 0))],
        out_specs=(pl.BlockSpec((groups, rows, S5_GROUP * CHUNK), lambda b, j: (0, b * tiles + j, 0)),
                   pl.BlockSpec((1, ts, fdim), lambda b, j: (b, j, 0)),
                   pl.BlockSpec((nb, ts), lambda b, j: (0, b * tiles + j))),
        scratch_shapes=[pltpu.VMEM((ts, ncols), F32),
                        pltpu.VMEM((HGRN_HEADS, fdim // HGRN_HEADS, HGRN_KDIM), F32),
                        pltpu.VMEM((rows * nb // 2, LANES), F32)],
        compiler_params=pltpu.CompilerParams(
            dimension_semantics=("arbitrary", "arbitrary"), vmem_limit_bytes=VMEM_LIMIT),
        name="mix_front",
    )(x, mod, w_main, w_ut, lb, gn, ltri)


def _s5_tables(lam_re, lam_im, log_dt, b_re, b_im, c_re, c_im):
    t = CHUNK
    hp = lax.Precision.HIGHEST
    lam = lax.complex(jnp.minimum(lam_re, -1e-4), lam_im)
    lam_dt = lam * jnp.exp(log_dt)[:, None]
    lam_bar = jnp.exp(lam_dt)
    b_bar = ((lam_bar - 1.0) / lam)[..., None] * lax.complex(b_re, b_im)
    c_mat = lax.complex(c_re, c_im)
    taus = jnp.arange(t + 1, dtype=F32)
    lam_pow = jnp.exp(lam_dt[:, None, :] * taus[None, :, None])
    g, p = lam.shape
    c = b_re.shape[-1]
    cl = c_mat[:, None, :, :] * lam_pow[:, :t, None, :]
    kr = (jnp.einsum('gtcp,gpi->gtci', cl.real, b_bar.real, precision=hp)
          - jnp.einsum('gtcp,gpi->gtci', cl.imag, b_bar.imag, precision=hp))
    z = jnp.pad(kr.transpose(0, 3, 2, 1).astype(BF16), ((0, 0), (0, 0), (0, 0), (t - 1, 0)))
    kg = jnp.stack([z[..., t - 1 - s:2 * t - 1 - s] for s in range(t)], axis=2)
    m = kg.reshape(g, c * t, c * t)
    pc = lam_pow[:, t - 1::-1][:, :t, :, None] * b_bar[:, None, :, :]
    pc = pc.transpose(0, 3, 1, 2).reshape(g, c * t, p)
    p_tab = jnp.concatenate([pc.real, pc.imag], axis=-1)
    ql = c_mat[:, None, :, :] * lam_pow[:, 1:t + 1, None, :]
    ql = ql.transpose(0, 3, 2, 1).reshape(g, p, c * t)
    q_tab = jnp.concatenate([ql.real, -ql.imag], axis=1)
    lam_t = lam_pow[:, t]
    a1 = jnp.concatenate([lam_t.real, lam_t.real], axis=-1)[:, None, :]
    a2 = jnp.concatenate([-lam_t.imag, lam_t.imag], axis=-1)[:, None, :]
    return m.astype(BF16), p_tab.astype(BF16), q_tab.astype(BF16), a1, a2


def _s5_kernel(u_ref, m_ref, p_ref, q_ref, a1_ref, a2_ref, y_ref, v_ref, xs_ref, *, n_chunks, n_batch):
    u = u_ref[0].astype(BF16)
    v_ref[...] = jnp.dot(u, p_ref[0], preferred_element_type=F32)
    a1 = a1_ref[0]
    a2 = a2_ref[0]
    half = xs_ref.shape[1] // 2

    def step(n, state):
        xs_ref[pl.ds(n, n_batch, stride=n_chunks), :] = state
        return (a1 * state + a2 * pltpu.roll(state, half, 1)
                + v_ref[pl.ds(n, n_batch, stride=n_chunks), :])

    lax.fori_loop(0, n_chunks, step, jnp.zeros((n_batch, xs_ref.shape[1]), F32))
    y = jnp.dot(u, m_ref[0], preferred_element_type=F32)
    y_ref[0] = y + jnp.dot(xs_ref[...].astype(BF16), q_ref[0], preferred_element_type=F32)


def _s5_call(ut, m, p_tab, q_tab, a1, a2, n_chunks, n_batch):
    g, rows, width = ut.shape
    p2 = p_tab.shape[-1]
    return pl.pallas_call(
        functools.partial(_s5_kernel, n_chunks=n_chunks, n_batch=n_batch),
        out_shape=jax.ShapeDtypeStruct((g, rows, width), F32),
        grid=(g,),
        in_specs=[pl.BlockSpec((1, rows, width), lambda i: (i, 0, 0)),
                  pl.BlockSpec((1, width, width), lambda i: (i, 0, 0)),
                  pl.BlockSpec((1, width, p2), lambda i: (i, 0, 0)),
                  pl.BlockSpec((1, p2, width), lambda i: (i, 0, 0)),
                  pl.BlockSpec((1, 1, p2), lambda i: (i, 0, 0)),
                  pl.BlockSpec((1, 1, p2), lambda i: (i, 0, 0))],
        out_specs=pl.BlockSpec((1, rows, width), lambda i: (i, 0, 0)),
        scratch_shapes=[pltpu.VMEM((rows, p2), F32), pltpu.VMEM((rows, p2), F32)],
        compiler_params=pltpu.CompilerParams(
            dimension_semantics=("arbitrary",), vmem_limit_bytes=VMEM_LIMIT),
        name="s5",
    )(ut, m, p_tab, q_tab, a1, a2)


def _token_major(flat_ref, src_ref):
    groups, rows, _ = src_ref.shape
    per_group = S5_GROUP // 2
    pairs = groups * per_group
    for q in range(pairs):
        flat_ref[pl.ds(q, rows, stride=pairs), :] = (
            src_ref[q // per_group, :, (q % per_group) * LANES:(q % per_group + 1) * LANES])
    tiles = []
    for m in range(rows // 2):
        even, odd = _split_chunk_pairs(flat_ref[2 * m * pairs:(2 * m + 1) * pairs, :],
                                       flat_ref[(2 * m + 1) * pairs:(2 * m + 2) * pairs, :])
        tiles.append(jnp.concatenate([even, odd], axis=0))
    return jnp.concatenate(tiles, axis=1)


def _mix_back_kernel(x_ref, oa_ref, yt_ref, ucm_ref, mod_ref, dskip_ref, wglu_ref, bglu_ref,
                     wout_ref, wrh_ref, wrl_ref,
                     x1_ref, hp_ref, lt_ref, flat_ref):
    na = oa_ref.shape[1]
    y_t = _token_major(flat_ref, yt_ref)
    u_t = ucm_ref[...].astype(F32)
    z_t = jax.nn.gelu(y_t + dskip_ref[...] * u_t)
    gate_t = _sigmoid(jnp.dot(wglu_ref[...], z_t.astype(BF16), preferred_element_type=F32)
                      + bglu_ref[...])
    ob_t = (z_t * gate_t).astype(BF16)
    mixed = (jnp.dot(oa_ref[...], wout_ref[0:na, :], preferred_element_type=F32)
             + lax.dot_general(ob_t, wout_ref[na:, :], _TN, preferred_element_type=F32))
    x1 = x_ref[...] + mod_ref[0, 2:3, :] * mixed
    ms = jnp.mean(x1 * x1, axis=-1, keepdims=True)
    h2 = x1 * lax.rsqrt(ms + EPS) * (1.0 + mod_ref[0, 4:5, :]) + mod_ref[0, 3:4, :]
    _store_packed(hp_ref, h2, h2.shape[0])
    h_hi = h2.astype(BF16)
    h_lo = (h2 - h_hi.astype(F32)).astype(BF16)
    lt = lax.dot_general(wrh_ref[...], h_hi, _NT, preferred_element_type=F32)
    lt = lt + lax.dot_general(wrl_ref[...], h_hi, _NT, preferred_element_type=F32)
    lt = lt + lax.dot_general(wrh_ref[...], h_lo, _NT, preferred_element_type=F32)
    lt_ref[...] = lt
    x1_ref[...] = x1


def _mix_back_call(x2d, oa, yt, ucm, mod, dskip, wglu, bglu, wout, wrh, wrl, seq):
    n, d = x2d.shape
    nb = oa.shape[1]
    ne = wrh.shape[0]
    tm = TOK_TILE
    per_b = seq // tm
    groups, _, width = yt.shape
    rows = tm // CHUNK
    const = lambda i: (0, 0)
    flat_block = pl.BlockSpec((groups, rows, width), lambda i: (0, i, 0))
    return pl.pallas_call(
        _mix_back_kernel,
        out_shape=(jax.ShapeDtypeStruct((n, d), F32),
                   jax.ShapeDtypeStruct((n * PACK_ROWS, LANES), I32),
                   jax.ShapeDtypeStruct((ne, n), F32)),
        grid=(n // tm,),
        in_specs=[pl.BlockSpec((tm, d), lambda i: (i, 0)),
                  pl.BlockSpec((tm, nb), lambda i: (i, 0)),
                  flat_block,
                  pl.BlockSpec((nb, tm), lambda i: (0, i)),
                  pl.BlockSpec((1, 6, d), lambda i: (i // per_b, 0, 0)),
                  pl.BlockSpec((nb, 1), const),
                  pl.BlockSpec((nb, nb), const),
                  pl.BlockSpec((nb, 1), const),
                  pl.BlockSpec((d, d), const),
                  pl.BlockSpec((ne, d), const),
                  pl.BlockSpec((ne, d), const)],
        out_specs=(pl.BlockSpec((tm, d), lambda i: (i, 0)),
                   pl.BlockSpec((tm * PACK_ROWS, LANES), lambda i: (i, 0)),
                   pl.BlockSpec((ne, tm), lambda i: (0, i))),
        scratch_shapes=[pltpu.VMEM((rows * nb // 2, LANES), F32)],
        compiler_params=pltpu.CompilerParams(
            dimension_semantics=("arbitrary",), vmem_limit_bytes=VMEM_LIMIT),
        name="mix_back",
    )(x2d, oa, yt, ucm, mod, dskip, wglu, bglu, wout, wrh, wrl)


def _route_kernel(lt_ref, bias_ref, su_ref, idx_ref, w_ref, rank_ref, cnt_ref, run_ref):
    ne, tr = lt_ref.shape
    per_group = ne // N_EXPERT_GROUPS
    neg = -jnp.inf

    @pl.when(pl.program_id(0) == 0)
    def _():
        run_ref[...] = jnp.zeros_like(run_ref)

    s = _sigmoid(lt_ref[...])
    sel = s + bias_ref[...]
    gio = lax.broadcasted_iota(I32, (per_group, tr), 0)
    gscore = []
    for g in range(N_EXPERT_GROUPS):
        v = sel[g * per_group:(g + 1) * per_group, :]
        m1 = jnp.max(v, axis=0, keepdims=True)
        i1 = jnp.min(jnp.where(v == m1, gio, per_group), axis=0, keepdims=True)
        m2 = jnp.max(jnp.where(gio == i1, neg, v), axis=0, keepdims=True)
        gscore.append(m1 + m2)
    masked = []
    for g in range(N_EXPERT_GROUPS):
        ahead = jnp.zeros((1, tr), I32)
        for o in range(N_EXPERT_GROUPS):
            if o == g:
                continue
            wins = (gscore[o] >= gscore[g]) if o < g else (gscore[o] > gscore[g])
            ahead = ahead + wins.astype(I32)
        keep = ahead < TOPK_GROUPS
        masked.append(jnp.where(keep, sel[g * per_group:(g + 1) * per_group, :], neg))
    selm = jnp.concatenate(masked, axis=0)
    eio = lax.broadcasted_iota(I32, (ne, tr), 0)
    candidate = selm > neg
    idxs, ws = [], []
    for k in range(TOP_K):
        m = jnp.max(selm, axis=0, keepdims=True)
        ik = jnp.min(jnp.where(selm == m, eio, ne), axis=0, keepdims=True)
        onehot = eio == ik
        ws.append(jnp.sum(jnp.where(onehot, s, 0.0), axis=0, keepdims=True))
        selm = jnp.where(onehot, neg, selm)
        idxs.append(ik)
    hits = jnp.where(jnp.logical_and(candidate, selm == neg), 1.0, 0.0)
    wsum = ws[0]
    for k in range(1, TOP_K):
        wsum = wsum + ws[k]
    scale = ROUTE_SCALE / wsum
    ranks = jnp.dot(hits.astype(BF16), su_ref[...], preferred_element_type=F32) + run_ref[...]
    for k in range(TOP_K):
        idx_ref[k:k + 1, :] = idxs[k]
        w_ref[k:k + 1, :] = ws[k] * scale
        rk = jnp.sum(jnp.where(eio == idxs[k], ranks, 0.0), axis=0, keepdims=True)
        rank_ref[k:k + 1, :] = rk.astype(I32)
    run_ref[...] = run_ref[...] + jnp.sum(hits, axis=1, keepdims=True)
    cnt_ref[...] = run_ref[...]


def _route_call(lt, bias):
    ne, n = lt.shape
    tr = ROUTE_TILE
    su = jnp.triu(jnp.ones((tr, tr), F32), k=1).astype(BF16)
    return pl.pallas_call(
        _route_kernel,
        out_shape=(jax.ShapeDtypeStruct((TOP_K, n), I32),
                   jax.ShapeDtypeStruct((TOP_K, n), F32),
                   jax.ShapeDtypeStruct((TOP_K, n), I32),
                   jax.ShapeDtypeStruct((ne, 1), F32)),
        grid=(n // tr,),
        in_specs=[pl.BlockSpec((ne, tr), lambda i: (0, i)),
                  pl.BlockSpec((ne, 1), lambda i: (0, 0)),
                  pl.BlockSpec((tr, tr), lambda i: (0, 0))],
        out_specs=(pl.BlockSpec((TOP_K, tr), lambda i: (0, i)),
                   pl.BlockSpec((TOP_K, tr), lambda i: (0, i)),
                   pl.BlockSpec((TOP_K, tr), lambda i: (0, i)),
                   pl.BlockSpec((ne, 1), lambda i: (0, 0))),
        scratch_shapes=[pltpu.VMEM((ne, 1), F32)],
        compiler_params=pltpu.CompilerParams(
            dimension_semantics=("arbitrary",), vmem_limit_bytes=VMEM_LIMIT),
        name="route",
    )(lt, bias.reshape(ne, 1), su)


def _pos_kernel(idx_ref, rank_ref, st_ref, pos_ref):
    ne = st_ref.shape[0]
    tp = idx_ref.shape[1]
    eio = lax.broadcasted_iota(I32, (ne, tp), 0)
    st = st_ref[...]
    for k in range(TOP_K):
        base = jnp.sum(jnp.where(eio == idx_ref[k:k + 1, :], st, 0), axis=0, keepdims=True)
        pos_ref[k:k + 1, :] = base + rank_ref[k:k + 1, :]


def _pos_call(top_idx, rank, starts):
    kk, n = top_idx.shape
    ne = starts.shape[0]
    tp = POS_TILE
    return pl.pallas_call(
        _pos_kernel,
        out_shape=jax.ShapeDtypeStruct((kk, n), I32),
        grid=(n // tp,),
        in_specs=[pl.BlockSpec((kk, tp), lambda i: (0, i)),
                  pl.BlockSpec((kk, tp), lambda i: (0, i)),
                  pl.BlockSpec((ne, 1), lambda i: (0, 0))],
        out_specs=pl.BlockSpec((kk, tp), lambda i: (0, i)),
        compiler_params=pltpu.CompilerParams(
            dimension_semantics=("arbitrary",), vmem_limit_bytes=VMEM_LIMIT),
        name="pos",
    )(top_idx, rank, starts.reshape(ne, 1))


def _sc_mesh():
    return plsc.VectorSubcoreMesh(core_axis_name="c", subcore_axis_name="s",
                                  num_cores=SC_CORES, num_subcores=SC_SUBCORES)


def _sc_worker():
    return lax.axis_index("s") * SC_CORES + lax.axis_index("c")


def _dispatch_call(pos_win, hp):
    n = hp.shape[0]
    n_workers = SC_CORES * SC_SUBCORES
    wins_per_worker = n // SC_WINDOW // n_workers

    def body(hp_hbm, pos_hbm, xs_hbm, idx_a, idx_b, rows_a, rows_b, load_a, load_b, scatter_sem):
        first_win = _sc_worker() * wins_per_worker
        idx_v, rows_v, load_sem = (idx_a, idx_b), (rows_a, rows_b), (load_a, load_b)

        def loads(win, slot):
            return (pltpu.make_async_copy(hp_hbm.at[pl.ds(win * SC_WINDOW, SC_WINDOW)], rows_v[slot],
                                          load_sem[slot]),
                    pltpu.make_async_copy(pos_hbm.at[win], idx_v[slot], load_sem[slot]))

        for cp in loads(first_win, 0):
            cp.start()

        @pl.loop(0, wins_per_worker, step=2)
        def _(w):
            for slot in range(2):
                win = first_win + w + slot
                for cp in loads(win, slot):
                    cp.wait()

                @pl.when(w + slot + 1 < wins_per_worker)
                def _():
                    for cp in loads(win + 1, 1 - slot):
                        cp.start()

                copies = [pltpu.async_copy(rows_v[slot], xs_hbm.at[idx_v[slot].at[k]], scatter_sem)
                          for k in range(TOP_K)]
                for cp in copies:
                    cp.wait()

    idx_buf = pltpu.VMEM((TOP_K, SC_WINDOW), I32)
    row_buf = pltpu.VMEM((SC_WINDOW,) + hp.shape[1:], hp.dtype)
    return pl.kernel(
        body,
        out_type=jax.ShapeDtypeStruct((n * TOP_K,) + hp.shape[1:], hp.dtype),
        mesh=_sc_mesh(),
        scratch_types=[idx_buf, idx_buf, row_buf, row_buf,
                       pltpu.SemaphoreType.DMA, pltpu.SemaphoreType.DMA, pltpu.SemaphoreType.DMA],
        name="dispatch",
    )(hp, pos_win)


def _collect_call(pos_win, y_sorted):
    n = pos_win.shape[0] * SC_WINDOW
    n_workers = SC_CORES * SC_SUBCORES
    wins_per_worker = n // SC_WINDOW // n_workers

    def body(ys_hbm, pos_hbm, out_hbm, idx_v, rows_a, rows_b, gather_sem, write_sem):
        first_win = _sc_worker() * wins_per_worker
        bufs = (rows_a, rows_b)

        @pl.loop(0, wins_per_worker)
        def _(w):
            win = first_win + w
            pltpu.sync_copy(pos_hbm.at[win], idx_v)

            def gather(k):
                return pltpu.async_copy(ys_hbm.at[idx_v.at[k]], bufs[k % 2], gather_sem)

            pending_gather = gather(0)
            pending_write = None
            for k in range(TOP_K):
                pending_gather.wait()
                if pending_write is not None:
                    for cp in pending_write:
                        cp.wait()
                if k + 1 < TOP_K:
                    pending_gather = gather(k + 1)
                pending_write = [
                    pltpu.async_copy(bufs[k % 2].at[:, j],
                                     out_hbm.at[j, k, pl.ds(win * SC_WINDOW, SC_WINDOW)], write_sem)
                    for j in range(PACK_ROWS)]
            for cp in pending_write:
                cp.wait()

    row_buf = pltpu.VMEM((SC_WINDOW,) + y_sorted.shape[1:], y_sorted.dtype)
    return pl.kernel(
        body,
        out_type=jax.ShapeDtypeStruct((PACK_ROWS, TOP_K, n, LANES), y_sorted.dtype),
        mesh=_sc_mesh(),
        scratch_types=[pltpu.VMEM((TOP_K, SC_WINDOW), I32), row_buf, row_buf,
                       pltpu.SemaphoreType.DMA, pltpu.SemaphoreType.DMA],
        name="collect",
    )(y_sorted, pos_win)


def _experts_kernel(pe_ref, pb_ref, pv_ref, st_ref, en_ref,
                    xs_hbm, wg_ref, wu_ref, wd_ref, y_hbm,
                    wgb_ref, wub_ref, wdb_ref, xbuf_ref, stage_ref,
                    in_sems, out_sems, state_ref):
    p = pl.program_id(0)
    n_pairs = pl.num_programs(0)
    e = pe_ref[p]
    blk = pb_ref[p]
    rb = xbuf_ref.shape[2]
    prev = jnp.maximum(p - 1, 0)
    nxt = jnp.minimum(p + 1, n_pairs - 1)
    live = pv_ref[p] == 1
    first = jnp.logical_or(p == 0, pb_ref[prev] != blk)
    block_ends = jnp.logical_or(pb_ref[nxt] != blk, pv_ref[nxt] == 0)
    last = jnp.logical_or(p == n_pairs - 1, block_ends)

    def in_copy(j, block, slot):
        return pltpu.make_async_copy(xs_hbm.at[pl.ds(block * rb, rb), j], xbuf_ref.at[slot, j],
                                     in_sems.at[slot])

    def out_copy(j, block, slot):
        return pltpu.make_async_copy(stage_ref.at[slot, j], y_hbm.at[pl.ds(block * rb, rb), j],
                                     out_sems.at[slot])

    def drain(slot):
        @pl.when(state_ref[1 + slot] == 1)
        def _():
            for j in range(PACK_ROWS):
                out_copy(j, 0, slot).wait()
            state_ref[1 + slot] = 0

    @pl.when(p == 0)
    def _():
        state_ref[0] = 0
        state_ref[1] = 0
        state_ref[2] = 0
        stage_ref[...] = jnp.zeros_like(stage_ref)
        for j in range(PACK_ROWS):
            in_copy(j, blk, 0).start()

    slot = jnp.where(jnp.logical_and(first, p > 0), 1 - state_ref[0], state_ref[0])
    state_ref[0] = slot

    @pl.when(jnp.logical_and(p < n_pairs - 1,
                             jnp.logical_and(pb_ref[nxt] != blk, pv_ref[nxt] == 1)))
    def _():
        for j in range(PACK_ROWS):
            in_copy(j, pb_ref[nxt], 1 - slot).start()

    @pl.when(first)
    def _():
        for j in range(PACK_ROWS):
            in_copy(j, blk, slot).wait()
        drain(slot)

    @pl.when(jnp.logical_or(p == 0, pe_ref[prev] != e))
    def _():
        wgb_ref[...] = wg_ref[0].astype(BF16)
        wub_ref[...] = wu_ref[0].astype(BF16)
        wdb_ref[...] = wd_ref[0].astype(BF16)

    lo_row = st_ref[e]
    hi_row = en_ref[e]

    def sub_block(s, row0, shared):
        sub = pl.ds(s * SUB_BLOCK, SUB_BLOCK)
        pieces = []
        for j in range(PACK_ROWS):
            w = xbuf_ref[slot, j, sub, :]
            pieces.append(lax.bitcast_convert_type(w.astype(jnp.int16), BF16))
            pieces.append(lax.bitcast_convert_type(
                lax.shift_right_logical(w, 16).astype(jnp.int16), BF16))
        xb = jnp.concatenate(pieces, axis=1)
        gate = jnp.dot(xb, wgb_ref[...], preferred_element_type=F32)
        up = jnp.dot(xb, wub_ref[...], preferred_element_type=F32)
        yb = jnp.dot((_silu(gate) * up).astype(BF16), wdb_ref[...], preferred_element_type=F32)
        if shared:
            rows = row0 + lax.broadcasted_iota(I32, (SUB_BLOCK, 1), 0)
            mine = jnp.logical_and(rows >= lo_row, rows < hi_row)
        for j in range(PACK_ROWS):
            word = pltpu.pack_elementwise(
                [yb[:, 2 * j * LANES:(2 * j + 1) * LANES], yb[:, (2 * j + 1) * LANES:(2 * j + 2) * LANES]],
                packed_dtype=BF16)
            if shared:
                word = jnp.where(mine, word, stage_ref[slot, j, sub, :])
            stage_ref[slot, j, sub, :] = word

    for s in range(rb // SUB_BLOCK):
        row0 = blk * rb + s * SUB_BLOCK
        touched = jnp.logical_and(live, jnp.logical_and(row0 < hi_row, row0 + SUB_BLOCK > lo_row))
        whole = jnp.logical_and(lo_row <= row0, hi_row >= row0 + SUB_BLOCK)

        @pl.when(jnp.logical_and(touched, whole))
        def _():
            sub_block(s, row0, shared=False)

        @pl.when(jnp.logical_and(touched, jnp.logical_not(whole)))
        def _():
            sub_block(s, row0, shared=True)

    @pl.when(jnp.logical_and(live, last))
    def _():
        for j in range(PACK_ROWS):
            out_copy(j, blk, slot).start()
        state_ref[1 + slot] = 1

    @pl.when(p == n_pairs - 1)
    def _():
        drain(0)
        drain(1)


def _experts_call(pair_e, pair_blk, pair_ok, starts, ends, xs, w_gate, w_up, w_down):
    ne, d, de = w_gate.shape
    rb = ROW_BLOCK
    n_pairs = pair_e.shape[0]
    grid_spec = pltpu.PrefetchScalarGridSpec(
        num_scalar_prefetch=5,
        grid=(n_pairs,),
        in_specs=[pl.BlockSpec(memory_space=pl.ANY),
                  pl.BlockSpec((1, d, de), lambda p, pe, pb, pv, st, en: (pe[p], 0, 0)),
                  pl.BlockSpec((1, d, de), lambda p, pe, pb, pv, st, en: (pe[p], 0, 0)),
                  pl.BlockSpec((1, de, d), lambda p, pe, pb, pv, st, en: (pe[p], 0, 0))],
        out_specs=pl.BlockSpec(memory_space=pl.ANY),
        scratch_shapes=[pltpu.VMEM((d, de), BF16), pltpu.VMEM((d, de), BF16),
                        pltpu.VMEM((de, d), BF16),
                        pltpu.VMEM((2, PACK_ROWS, rb, LANES), I32),
                        pltpu.VMEM((2, PACK_ROWS, rb, LANES), I32),
                        pltpu.SemaphoreType.DMA((2,)), pltpu.SemaphoreType.DMA((2,)),
                        pltpu.SMEM((3,), I32)],
    )
    return pl.pallas_call(
        _experts_kernel,
        out_shape=jax.ShapeDtypeStruct(xs.shape, xs.dtype),
        grid_spec=grid_spec,
        compiler_params=pltpu.CompilerParams(
            dimension_semantics=("arbitrary",), vmem_limit_bytes=VMEM_LIMIT),
        name="experts",
    )(pair_e, pair_blk, pair_ok, starts, ends, xs, w_gate, w_up, w_down)


def _combine_kernel(y0_ref, y1_ref, y2_ref, y3_ref, x1_ref, hp_ref, w_ref, g2_ref, fg_ref,
                    wsg_ref, wsu_ref, wsd_ref, o_ref):
    tm = x1_ref.shape[0]
    w = w_ref[...]
    parts = []
    for y_ref in (y0_ref, y1_ref, y2_ref, y3_ref):
        for half in range(2):
            acc = None
            for k in range(TOP_K):
                piece = pltpu.unpack_elementwise(y_ref[k], index=half, packed_dtype=BF16,
                                                 unpacked_dtype=F32) * w[:, k:k + 1]
                acc = piece if acc is None else acc + piece
            parts.append(acc)
    routed = jnp.concatenate(parts, axis=1)
    pieces = []
    for j in range(PACK_ROWS):
        word = hp_ref[pl.ds(j, tm, stride=PACK_ROWS), :]
        pieces.append(lax.bitcast_convert_type(word.astype(jnp.int16), BF16))
        pieces.append(lax.bitcast_convert_type(lax.shift_right_logical(word, 16).astype(jnp.int16), BF16))
    h2 = jnp.concatenate(pieces, axis=1)
    hid = _silu(jnp.dot(h2, wsg_ref[...], preferred_element_type=F32)) * jnp.dot(
        h2, wsu_ref[...], preferred_element_type=F32)
    shared = jnp.dot(hid.astype(BF16), wsd_ref[...], preferred_element_type=F32)
    x2 = x1_ref[...] + g2_ref[0] * (routed + shared)
    ms = jnp.mean(x2 * x2, axis=-1, keepdims=True)
    o_ref[...] = x2 * lax.rsqrt(ms + EPS) * fg_ref[...]


def _combine_kernel_into(*refs):
    _combine_kernel(*refs[:-2], refs[-1])


def _combine_call(y_tok, x1, hp, w_tok, g2, fgain, wsg, wsu, wsd, seq, part, prev_out):
    n, d = x1.shape
    dsh = wsg.shape[1]
    tm = MOVE_TILE
    per_b = seq // tm
    tiles = y_tok.shape[2] // tm
    t0 = part * tiles
    const = lambda i: (0, 0)

    def piece_spec(j):
        return pl.BlockSpec((None, TOP_K, tm, LANES), lambda i: (j, 0, i, 0))

    in_specs = [piece_spec(j) for j in range(PACK_ROWS)] + [
        pl.BlockSpec((tm, d), lambda i: (t0 + i, 0)),
        pl.BlockSpec((tm * PACK_ROWS, LANES), lambda i: (t0 + i, 0)),
        pl.BlockSpec((tm, TOP_K), lambda i: (t0 + i, 0)),
        pl.BlockSpec((1, 1, d), lambda i: ((t0 + i) // per_b, 0, 0)),
        pl.BlockSpec((1, d), const),
        pl.BlockSpec((d, dsh), const),
        pl.BlockSpec((d, dsh), const),
        pl.BlockSpec((dsh, d), const)]
    args = [y_tok, y_tok, y_tok, y_tok, x1, hp, w_tok, g2, fgain, wsg, wsu, wsd]
    aliases = {}
    body = _combine_kernel
    if prev_out is not None:
        in_specs.append(pl.BlockSpec(memory_space=pl.ANY))
        args.append(prev_out)
        aliases = {len(args) - 1: 0}
        body = _combine_kernel_into
    return pl.pallas_call(
        body,
        out_shape=jax.ShapeDtypeStruct((n, d), F32),
        grid=(tiles,),
        in_specs=in_specs,
        out_specs=pl.BlockSpec((tm, d), lambda i: (t0 + i, 0)),
        input_output_aliases=aliases,
        compiler_params=pltpu.CompilerParams(
            dimension_semantics=("arbitrary",), vmem_limit_bytes=VMEM_LIMIT),
        name="combine",
    )(*args)


def _pair_tables(counts, n_rows):
    ne = counts.shape[0]
    sizes = counts.astype(I32)
    ends = jnp.cumsum(sizes)
    starts = ends - sizes
    first_blk = starts // ROW_BLOCK
    last_blk = (ends - 1) // ROW_BLOCK
    n_pairs = jnp.where(sizes > 0, last_blk - first_blk + 1, 0)
    pair_end = jnp.cumsum(n_pairs)
    pair_start = pair_end - n_pairs
    max_pairs = n_rows // ROW_BLOCK + ne
    p = jnp.arange(max_pairs, dtype=I32)
    ok = p < pair_end[-1]
    pc = jnp.minimum(p, pair_end[-1] - 1)
    pair_e = jnp.minimum(jnp.searchsorted(pair_end, pc, side='right'), ne - 1).astype(I32)
    pair_blk = (first_blk[pair_e] + pc - pair_start[pair_e]).astype(I32)
    return pair_e, pair_blk, ok.astype(I32), starts.astype(I32), ends.astype(I32)


def kernel(x, c, w_ada, b_ada, w_in, lb_logits, hgrn_norm, lam_re, lam_im, log_dt, b_re, b_im,
           c_re, c_im, d_skip, w_glu, b_glu, w_out, w_router, router_bias, w_gate, w_up, w_down,
           ws_gate, ws_up, ws_down, final_gain):
    bsz, seq, d = x.shape
    n = bsz * seq
    fdim = HGRN_HEADS * HGRN_KDIM
    n_chunks = seq // CHUNK
    lb = jnp.cumsum(jax.nn.softmax(lb_logits.astype(F32), axis=0), axis=0)[0].reshape(1, fdim)

    mod = _mod_call(c, w_ada[0], b_ada[0]).reshape(bsz, 6, d)
    nb = w_in.shape[2] - 4 * fdim
    perm = jnp.concatenate([jnp.arange(0, nb, 2), jnp.arange(1, nb, 2)])
    w_in_b = w_in[0].astype(BF16)
    ut, out_a, ucm = _mix_front_call(x, mod, w_in_b[:, :4 * fdim], w_in_b[:, 4 * fdim:][:, perm].T, lb,
                                hgrn_norm[0].reshape(1, fdim))

    m_tab, p_tab, q_tab, a1, a2 = _s5_tables(lam_re[0], lam_im[0], log_dt[0], b_re[0], b_im[0],
                                             c_re[0], c_im[0])
    yt = _s5_call(ut, m_tab, p_tab, q_tab, a1, a2, n_chunks, bsz)

    wr_t = w_router[0].T
    wr_hi = wr_t.astype(BF16)
    wr_lo = (wr_t - wr_hi.astype(F32)).astype(BF16)
    w_out_b = w_out[0].astype(BF16)
    w_out_p = jnp.concatenate([w_out_b[:fdim], w_out_b[fdim:][perm]], axis=0)
    x1, hp, logits_t = _mix_back_call(
        x.reshape(n, d), out_a.reshape(n, fdim), yt, ucm, mod,
        d_skip[0][perm].reshape(nb, 1), w_glu[0][perm][:, perm].T.astype(BF16),
        b_glu[0][perm].reshape(nb, 1), w_out_p, wr_hi, wr_lo, seq)

    top_idx, top_w, rank, counts = _route_call(logits_t, router_bias[0])
    pair_e, pair_blk, pair_ok, starts, ends = _pair_tables(counts[:, 0], n * TOP_K)
    pos = _pos_call(top_idx, rank, starts)

    pos_win = pos.reshape(TOP_K, n // SC_WINDOW, SC_WINDOW).transpose(1, 0, 2)
    xs = _dispatch_call(pos_win, hp.reshape(n, PACK_ROWS, LANES))
    y_sorted = _experts_call(pair_e, pair_blk, pair_ok, starts, ends, xs, w_gate[0], w_up[0], w_down[0])
    w_tok = top_w.T
    g2 = mod[:, 5:6, :]
    fgain = final_gain.reshape(1, d)
    wsg, wsu, wsd = ws_gate[0].astype(BF16), ws_up[0].astype(BF16), ws_down[0].astype(BF16)
    wins = pos_win.shape[0] // TAIL_PARTS
    out = None
    for part in range(TAIL_PARTS):
        y_tok = _collect_call(pos_win[part * wins:(part + 1) * wins], y_sorted)
        out = _combine_call(y_tok, x1, hp, w_tok, g2, fgain, wsg, wsu, wsd, seq, part, out)
    return out.reshape(bsz, seq, d)
```

```python
import functools

import jax
import jax.numpy as jnp
from jax import lax
from jax.experimental import pallas as pl
from jax.experimental.pallas import tpu as pltpu
from jax.experimental.pallas import tpu_sc as plsc

F32 = jnp.float32
BF16 = jnp.bfloat16
I32 = jnp.int32

EPS = 1e-6
CHUNK = 64
HGRN_HEADS = 4
HGRN_KDIM = 128
S5_GROUP = 16
S5_STATE = 64
N_EXPERT_GROUPS = 8
TOPK_GROUPS = 4
TOP_K = 8
ROUTE_SCALE = 2.5
LANES = 128
PACK_ROWS = 4
SC_CORES = 2
SC_SUBCORES = 16
SC_WINDOW = 64

SEQ_TILE = 512
TOK_TILE = 512
ROUTE_TILE = 512
POS_TILE = 1024
MOVE_TILE = 512
ROW_BLOCK = 2048
SUB_BLOCK = 512
TAIL_PARTS = 4
VMEM_LIMIT = 56 * 1024 * 1024

_NT = (((1,), (1,)), ((), ()))
_TN = (((0,), (0,)), ((), ()))


def _sigmoid(v):
    return 0.5 * jnp.tanh(0.5 * v) + 0.5


def _silu(v):
    return v * _sigmoid(v)


def _bdot(a, b):
    return jnp.dot(a.astype(BF16), b.astype(BF16), preferred_element_type=F32)


def _store_packed(ref, val, n_rows):
    for j in range(PACK_ROWS):
        lo = val[:, 2 * j * LANES:(2 * j + 1) * LANES]
        hi = val[:, (2 * j + 1) * LANES:(2 * j + 2) * LANES]
        ref[pl.ds(j, n_rows, stride=PACK_ROWS), :] = pltpu.pack_elementwise([lo, hi], packed_dtype=BF16)


def _mod_kernel(c_ref, w_ref, b_ref, o_ref):
    o_ref[...] = _bdot(_silu(c_ref[...]), w_ref[...]) + b_ref[...]


def _mod_call(c, w_ada, b_ada):
    bsz, d = c.shape
    n_out = w_ada.shape[1]
    return pl.pallas_call(
        _mod_kernel,
        out_shape=jax.ShapeDtypeStruct((bsz, n_out), F32),
        grid=(n_out // d,),
        in_specs=[pl.BlockSpec((bsz, d), lambda j: (0, 0)),
                  pl.BlockSpec((d, d), lambda j: (0, j)),
                  pl.BlockSpec((1, d), lambda j: (0, j))],
        out_specs=pl.BlockSpec((bsz, d), lambda j: (0, j)),
        compiler_params=pltpu.CompilerParams(vmem_limit_bytes=VMEM_LIMIT),
        name="mod",
    )(c, w_ada, b_ada.reshape(1, n_out))


def _split_chunk_pairs(tile_even, tile_odd):
    low = lax.broadcasted_iota(I32, tile_even.shape, 1) < CHUNK
    first = jnp.where(low, tile_even, pltpu.roll(tile_odd, CHUNK, 1))
    second = jnp.where(low, pltpu.roll(tile_even, CHUNK, 1), tile_odd)
    return first, second


def _mix_front_kernel(x_ref, mod_ref, win_ref, wut_ref, lb_ref, gn_ref, ltri_ref,
                      ut_ref, oa_ref, ucm_ref, proj_ref, st_ref, flat_ref):
    fdim = HGRN_HEADS * HGRN_KDIM
    ts = x_ref.shape[1]
    pairs = wut_ref.shape[0] // 2
    rows = ts // CHUNK

    @pl.when(pl.program_id(1) == 0)
    def _():
        st_ref[...] = jnp.zeros_like(st_ref)

    x = x_ref[0]
    ms = jnp.mean(x * x, axis=-1, keepdims=True)
    h = x * lax.rsqrt(ms + EPS) * (1.0 + mod_ref[0, 1:2, :]) + mod_ref[0, 0:1, :]
    hb = h.astype(BF16)
    proj_ref[...] = jnp.dot(hb, win_ref[...], preferred_element_type=F32)
    u_t = lax.dot_general(wut_ref[...], hb, _NT, preferred_element_type=F32)
    ucm_ref[...] = u_t.astype(BF16)
    for m in range(ts // LANES):
        first, second = _split_chunk_pairs(u_t[:pairs, m * LANES:(m + 1) * LANES],
                                           u_t[pairs:, m * LANES:(m + 1) * LANES])
        flat_ref[2 * m * pairs:(2 * m + 1) * pairs, :] = first
        flat_ref[(2 * m + 1) * pairs:(2 * m + 2) * pairs, :] = second
    per_group = S5_GROUP // 2
    for q in range(pairs):
        ut_ref[q // per_group, :, (q % per_group) * LANES:(q % per_group + 1) * LANES] = (
            flat_ref[pl.ds(q, rows, stride=pairs), :])

    lb = lb_ref[...]
    gn = gn_ref[...]
    ltri = ltri_ref[...]
    row = lax.broadcasted_iota(I32, (CHUNK, CHUNK), 0)
    col = lax.broadcasted_iota(I32, (CHUNK, CHUNK), 1)
    causal = row >= col

    def chunk_step(ci, carry):
        r0 = pl.multiple_of(ci * CHUNK, CHUNK)
        q = proj_ref[pl.ds(r0, CHUNK), 0:fdim]
        fl = proj_ref[pl.ds(r0, CHUNK), fdim:2 * fdim]
        iv = proj_ref[pl.ds(r0, CHUNK), 2 * fdim:3 * fdim]
        og = proj_ref[pl.ds(r0, CHUNK), 3 * fdim:4 * fdim]
        f = lb + (1.0 - lb) * _sigmoid(fl)
        lf = jnp.log(f)
        lf_hi = lf.astype(BF16)
        rem = lf - lf_hi.astype(F32)
        lf_mid = rem.astype(BF16)
        lf_lo = (rem - lf_mid.astype(F32)).astype(BF16)
        b = (jnp.dot(ltri, lf_hi, preferred_element_type=F32)
             + jnp.dot(ltri, lf_mid, preferred_element_type=F32)
             + jnp.dot(ltri, lf_lo, preferred_element_type=F32))
        b_ref = b[CHUNK // 2 - 1:CHUNK // 2, :]
        b_last = b[CHUNK - 1:CHUNK, :]
        qs = _silu(q)
        kk = 1.0 - f
        qe = (qs * jnp.exp(b - b_ref)).astype(BF16)
        ke = (kk * jnp.exp(b_ref - b)).astype(BF16)
        qb = (qs * jnp.exp(b)).astype(BF16)
        k2 = (kk * jnp.exp(b_last - b)).astype(BF16)
        dec = jnp.exp(b_last)
        ivb = iv.astype(BF16)
        outs = []
        for hh in range(HGRN_HEADS):
            sl = slice(hh * HGRN_KDIM, (hh + 1) * HGRN_KDIM)
            att = lax.dot_general(qe[:, sl], ke[:, sl], _NT, preferred_element_type=F32)
            att = jnp.where(causal, att, 0.0)
            st = st_ref[hh]
            o = jnp.dot(att.astype(BF16), ivb[:, sl], preferred_element_type=F32)
            o = o + lax.dot_general(qb[:, sl], st.astype(BF16), _NT, preferred_element_type=F32)
            st_ref[hh] = st * dec[:, sl] + lax.dot_general(
                ivb[:, sl], k2[:, sl], _TN, preferred_element_type=F32)
            outs.append(o * lax.rsqrt(jnp.mean(o * o, axis=-1, keepdims=True) + EPS))
        o = jnp.concatenate(outs, axis=1) * gn * _silu(og)
        oa_ref[0, pl.ds(r0, CHUNK), :] = o.astype(BF16)
        return carry

    lax.fori_loop(0, ts // CHUNK, chunk_step, 0, unroll=True)


def _mix_front_call(x, mod, w_main, w_ut, lb, gn):
    bsz, seq, d = x.shape
    fdim = HGRN_HEADS * HGRN_KDIM
    ncols = w_main.shape[1]
    nb = w_ut.shape[0]
    groups = nb // S5_GROUP
    ltri = jnp.tril(jnp.ones((CHUNK, CHUNK), BF16))
    ts = SEQ_TILE
    tiles = seq // ts
    rows = ts // CHUNK
    return pl.pallas_call(
        _mix_front_kernel,
        out_shape=(jax.ShapeDtypeStruct((groups, bsz * seq // CHUNK, S5_GROUP * CHUNK), F32),
                   jax.ShapeDtypeStruct((bsz, seq, fdim), BF16),
                   jax.ShapeDtypeStruct((nb, bsz * seq), BF16)),
        grid=(bsz, tiles),
        in_specs=[pl.BlockSpec((1, ts, d), lambda b, j: (b, j, 0)),
                  pl.BlockSpec((1, 6, d), lambda b, j: (b, 0, 0)),
                  pl.BlockSpec((d, ncols), lambda b, j: (0, 0)),
                  pl.BlockSpec((nb, d), lambda b, j: (0, 0)),
                  pl.BlockSpec((1, fdim), lambda b, j: (0, 0)),
                  pl.BlockSpec((1, fdim), lambda b, j: (0, 0)),
                  pl.BlockSpec((CHUNK, CHUNK), lambda b, j: (0, 0))],
        out_specs=(pl.BlockSpec((groups, rows, S5_GROUP * CHUNK), lambda b, j: (0, b * tiles + j, 0)),
                   pl.BlockSpec((1, ts, fdim), lambda b, j: (b, j, 0)),
                   pl.BlockSpec((nb, ts), lambda b, j: (0, b * tiles + j))),
        scratch_shapes=[pltpu.VMEM((ts, ncols), F32),
                        pltpu.VMEM((HGRN_HEADS, fdim // HGRN_HEADS, HGRN_KDIM), F32),
                        pltpu.VMEM((rows * nb // 2, LANES), F32)],
        compiler_params=pltpu.CompilerParams(
            dimension_semantics=("arbitrary", "arbitrary"), vmem_limit_bytes=VMEM_LIMIT),
        name="mix_front",
    )(x, mod, w_main, w_ut, lb, gn, ltri)


def _s5_tables(lam_re, lam_im, log_dt, b_re, b_im, c_re, c_im):
    t = CHUNK
    hp = lax.Precision.HIGHEST
    lam = lax.complex(jnp.minimum(lam_re, -1e-4), lam_im)
    lam_dt = lam * jnp.exp(log_dt)[:, None]
    lam_bar = jnp.exp(lam_dt)
    b_bar = ((lam_bar - 1.0) / lam)[..., None] * lax.complex(b_re, b_im)
    c_mat = lax.complex(c_re, c_im)
    taus = jnp.arange(t + 1, dtype=F32)
    lam_pow = jnp.exp(lam_dt[:, None, :] * taus[None, :, None])
    g, p = lam.shape
    c = b_re.shape[-1]
    cl = c_mat[:, None, :, :] * lam_pow[:, :t, None, :]
    kr = (jnp.einsum('gtcp,gpi->gtci', cl.real, b_bar.real, precision=hp)
          - jnp.einsum('gtcp,gpi->gtci', cl.imag, b_bar.imag, precision=hp))
    k_tab = kr.transpose(0, 3, 2, 1).reshape(g, c, c // 2, 2 * t)
    pc = lam_pow[:, t - 1::-1][:, :t, :, None] * b_bar[:, None, :, :]
    pc = pc.transpose(0, 3, 1, 2).reshape(g, c * t, p)
    p_tab = jnp.concatenate([pc.real, pc.imag], axis=-1)
    ql = c_mat[:, None, :, :] * lam_pow[:, 1:t + 1, None, :]
    ql = ql.transpose(0, 3, 2, 1).reshape(g, p, c * t)
    q_tab = jnp.concatenate([ql.real, -ql.imag], axis=1)
    lam_t = lam_pow[:, t]
    a1 = jnp.concatenate([lam_t.real, lam_t.real], axis=-1)[:, None, :]
    a2 = jnp.concatenate([-lam_t.imag, lam_t.imag], axis=-1)[:, None, :]
    return k_tab, p_tab.astype(BF16), q_tab.astype(BF16), a1, a2


def _s5_kernel(u_ref, k_ref, p_ref, q_ref, a1_ref, a2_ref, y_ref, v_ref, xs_ref, m_ref,
               *, n_chunks, n_batch):
    n_in, n_pairs, _ = k_ref.shape[1:]
    lane = lax.broadcasted_iota(I32, (CHUNK, LANES), 1)
    causal = (lane & (CHUNK - 1)) >= lax.broadcasted_iota(I32, (CHUNK, LANES), 0)
    for ci in range(n_in):
        for a in range(n_pairs):
            lags = jnp.broadcast_to(k_ref[0, ci, a:a + 1, :], (CHUNK, LANES))
            tile = pltpu.roll(lags, 0, 1, stride=1, stride_axis=0)
            m_ref[ci * CHUNK:(ci + 1) * CHUNK, a * LANES:(a + 1) * LANES] = jnp.where(
                causal, tile, 0.0).astype(BF16)
    u = u_ref[0].astype(BF16)
    v_ref[...] = jnp.dot(u, p_ref[0], preferred_element_type=F32)
    a1 = a1_ref[0]
    a2 = a2_ref[0]
    half = xs_ref.shape[1] // 2

    def step(n, state):
        xs_ref[pl.ds(n, n_batch, stride=n_chunks), :] = state
        return (a1 * state + a2 * pltpu.roll(state, half, 1)
                + v_ref[pl.ds(n, n_batch, stride=n_chunks), :])

    lax.fori_loop(0, n_chunks, step, jnp.zeros((n_batch, xs_ref.shape[1]), F32))
    y = jnp.dot(u, m_ref[...], preferred_element_type=F32)
    y_ref[0] = y + jnp.dot(xs_ref[...].astype(BF16), q_ref[0], preferred_element_type=F32)


def _s5_call(ut, k_tab, p_tab, q_tab, a1, a2, n_chunks, n_batch):
    g, rows, width = ut.shape
    p2 = p_tab.shape[-1]
    return pl.pallas_call(
        functools.partial(_s5_kernel, n_chunks=n_chunks, n_batch=n_batch),
        out_shape=jax.ShapeDtypeStruct((g, rows, width), F32),
        grid=(g,),
        in_specs=[pl.BlockSpec((1, rows, width), lambda i: (i, 0, 0)),
                  pl.BlockSpec((1,) + k_tab.shape[1:], lambda i: (i, 0, 0, 0)),
                  pl.BlockSpec((1, width, p2), lambda i: (i, 0, 0)),
                  pl.BlockSpec((1, p2, width), lambda i: (i, 0, 0)),
                  pl.BlockSpec((1, 1, p2), lambda i: (i, 0, 0)),
                  pl.BlockSpec((1, 1, p2), lambda i: (i, 0, 0))],
        out_specs=pl.BlockSpec((1, rows, width), lambda i: (i, 0, 0)),
        scratch_shapes=[pltpu.VMEM((rows, p2), F32), pltpu.VMEM((rows, p2), F32),
                        pltpu.VMEM((width, width), BF16)],
        compiler_params=pltpu.CompilerParams(
            dimension_semantics=("arbitrary",), vmem_limit_bytes=VMEM_LIMIT),
        name="s5",
    )(ut, k_tab, p_tab, q_tab, a1, a2)


def _token_major(flat_ref, src_ref):
    groups, rows, _ = src_ref.shape
    per_group = S5_GROUP // 2
    pairs = groups * per_group
    for q in range(pairs):
        flat_ref[pl.ds(q, rows, stride=pairs), :] = (
            src_ref[q // per_group, :, (q % per_group) * LANES:(q % per_group + 1) * LANES])
    tiles = []
    for m in range(rows // 2):
        even, odd = _split_chunk_pairs(flat_ref[2 * m * pairs:(2 * m + 1) * pairs, :],
                                       flat_ref[(2 * m + 1) * pairs:(2 * m + 2) * pairs, :])
        tiles.append(jnp.concatenate([even, odd], axis=0))
    return jnp.concatenate(tiles, axis=1)


def _mix_back_kernel(x_ref, oa_ref, yt_ref, ucm_ref, mod_ref, dskip_ref, wglu_ref, bglu_ref,
                     wout_ref, wrh_ref, wrl_ref,
                     x1_ref, hp_ref, lt_ref, flat_ref):
    na = oa_ref.shape[1]
    y_t = _token_major(flat_ref, yt_ref)
    u_t = ucm_ref[...].astype(F32)
    z_t = jax.nn.gelu(y_t + dskip_ref[...] * u_t)
    gate_t = _sigmoid(jnp.dot(wglu_ref[...], z_t.astype(BF16), preferred_element_type=F32)
                      + bglu_ref[...])
    ob_t = (z_t * gate_t).astype(BF16)
    mixed = (jnp.dot(oa_ref[...], wout_ref[0:na, :], preferred_element_type=F32)
             + lax.dot_general(ob_t, wout_ref[na:, :], _TN, preferred_element_type=F32))
    x1 = x_ref[...] + mod_ref[0, 2:3, :] * mixed
    ms = jnp.mean(x1 * x1, axis=-1, keepdims=True)
    h2 = x1 * lax.rsqrt(ms + EPS) * (1.0 + mod_ref[0, 4:5, :]) + mod_ref[0, 3:4, :]
    _store_packed(hp_ref, h2, h2.shape[0])
    h_hi = h2.astype(BF16)
    h_lo = (h2 - h_hi.astype(F32)).astype(BF16)
    lt = lax.dot_general(wrh_ref[...], h_hi, _NT, preferred_element_type=F32)
    lt = lt + lax.dot_general(wrl_ref[...], h_hi, _NT, preferred_element_type=F32)
    lt = lt + lax.dot_general(wrh_ref[...], h_lo, _NT, preferred_element_type=F32)
    lt_ref[...] = lt
    x1_ref[...] = x1


def _mix_back_call(x2d, oa, yt, ucm, mod, dskip, wglu, bglu, wout, wrh, wrl, seq):
    n, d = x2d.shape
    nb = oa.shape[1]
    ne = wrh.shape[0]
    tm = TOK_TILE
    per_b = seq // tm
    groups, _, width = yt.shape
    rows = tm // CHUNK
    const = lambda i: (0, 0)
    flat_block = pl.BlockSpec((groups, rows, width), lambda i: (0, i, 0))
    return pl.pallas_call(
        _mix_back_kernel,
        out_shape=(jax.ShapeDtypeStruct((n, d), F32),
                   jax.ShapeDtypeStruct((n * PACK_ROWS, LANES), I32),
                   jax.ShapeDtypeStruct((ne, n), F32)),
        grid=(n // tm,),
        in_specs=[pl.BlockSpec((tm, d), lambda i: (i, 0)),
                  pl.BlockSpec((tm, nb), lambda i: (i, 0)),
                  flat_block,
                  pl.BlockSpec((nb, tm), lambda i: (0, i)),
                  pl.BlockSpec((1, 6, d), lambda i: (i // per_b, 0, 0)),
                  pl.BlockSpec((nb, 1), const),
                  pl.BlockSpec((nb, nb), const),
                  pl.BlockSpec((nb, 1), const),
                  pl.BlockSpec((d, d), const),
                  pl.BlockSpec((ne, d), const),
                  pl.BlockSpec((ne, d), const)],
        out_specs=(pl.BlockSpec((tm, d), lambda i: (i, 0)),
                   pl.BlockSpec((tm * PACK_ROWS, LANES), lambda i: (i, 0)),
                   pl.BlockSpec((ne, tm), lambda i: (0, i))),
        scratch_shapes=[pltpu.VMEM((rows * nb // 2, LANES), F32)],
        compiler_params=pltpu.CompilerParams(
            dimension_semantics=("arbitrary",), vmem_limit_bytes=VMEM_LIMIT),
        name="mix_back",
    )(x2d, oa, yt, ucm, mod, dskip, wglu, bglu, wout, wrh, wrl)


def _route_kernel(lt_ref, bias_ref, su_ref, idx_ref, w_ref, rank_ref, cnt_ref, run_ref):
    ne, tr = lt_ref.shape
    per_group = ne // N_EXPERT_GROUPS
    neg = -jnp.inf

    @pl.when(pl.program_id(0) == 0)
    def _():
        run_ref[...] = jnp.zeros_like(run_ref)

    s = _sigmoid(lt_ref[...])
    sel = s + bias_ref[...]
    gio = lax.broadcasted_iota(I32, (per_group, tr), 0)
    gscore = []
    for g in range(N_EXPERT_GROUPS):
        v = sel[g * per_group:(g + 1) * per_group, :]
        m1 = jnp.max(v, axis=0, keepdims=True)
        i1 = jnp.min(jnp.where(v == m1, gio, per_group), axis=0, keepdims=True)
        m2 = jnp.max(jnp.where(gio == i1, neg, v), axis=0, keepdims=True)
        gscore.append(m1 + m2)
    masked = []
    for g in range(N_EXPERT_GROUPS):
        ahead = jnp.zeros((1, tr), I32)
        for o in range(N_EXPERT_GROUPS):
            if o == g:
                continue
            wins = (gscore[o] >= gscore[g]) if o < g else (gscore[o] > gscore[g])
            ahead = ahead + wins.astype(I32)
        keep = ahead < TOPK_GROUPS
        masked.append(jnp.where(keep, sel[g * per_group:(g + 1) * per_group, :], neg))
    selm = jnp.concatenate(masked, axis=0)
    eio = lax.broadcasted_iota(I32, (ne, tr), 0)
    candidate = selm > neg
    idxs, ws = [], []
    for k in range(TOP_K):
        m = jnp.max(selm, axis=0, keepdims=True)
        ik = jnp.min(jnp.where(selm == m, eio, ne), axis=0, keepdims=True)
        onehot = eio == ik
        ws.append(jnp.sum(jnp.where(onehot, s, 0.0), axis=0, keepdims=True))
        selm = jnp.where(onehot, neg, selm)
        idxs.append(ik)
    hits = jnp.where(jnp.logical_and(candidate, selm == neg), 1.0, 0.0)
    wsum = ws[0]
    for k in range(1, TOP_K):
        wsum = wsum + ws[k]
    scale = ROUTE_SCALE / wsum
    ranks = jnp.dot(hits.astype(BF16), su_ref[...], preferred_element_type=F32) + run_ref[...]
    for k in range(TOP_K):
        idx_ref[k:k + 1, :] = idxs[k]
        w_ref[k:k + 1, :] = ws[k] * scale
        rk = jnp.sum(jnp.where(eio == idxs[k], ranks, 0.0), axis=0, keepdims=True)
        rank_ref[k:k + 1, :] = rk.astype(I32)
    run_ref[...] = run_ref[...] + jnp.sum(hits, axis=1, keepdims=True)
    cnt_ref[...] = run_ref[...]


def _route_call(lt, bias):
    ne, n = lt.shape
    tr = ROUTE_TILE
    su = jnp.triu(jnp.ones((tr, tr), F32), k=1).astype(BF16)
    return pl.pallas_call(
        _route_kernel,
        out_shape=(jax.ShapeDtypeStruct((TOP_K, n), I32),
                   jax.ShapeDtypeStruct((TOP_K, n), F32),
                   jax.ShapeDtypeStruct((TOP_K, n), I32),
                   jax.ShapeDtypeStruct((ne, 1), F32)),
        grid=(n // tr,),
        in_specs=[pl.BlockSpec((ne, tr), lambda i: (0, i)),
                  pl.BlockSpec((ne, 1), lambda i: (0, 0)),
                  pl.BlockSpec((tr, tr), lambda i: (0, 0))],
        out_specs=(pl.BlockSpec((TOP_K, tr), lambda i: (0, i)),
                   pl.BlockSpec((TOP_K, tr), lambda i: (0, i)),
                   pl.BlockSpec((TOP_K, tr), lambda i: (0, i)),
                   pl.BlockSpec((ne, 1), lambda i: (0, 0))),
        scratch_shapes=[pltpu.VMEM((ne, 1), F32)],
        compiler_params=pltpu.CompilerParams(
            dimension_semantics=("arbitrary",), vmem_limit_bytes=VMEM_LIMIT),
        name="route",
    )(lt, bias.reshape(ne, 1), su)


def _pos_kernel(idx_ref, rank_ref, st_ref, pos_ref):
    ne = st_ref.shape[0]
    tp = idx_ref.shape[1]
    eio = lax.broadcasted_iota(I32, (ne, tp), 0)
    st = st_ref[...]
    for k in range(TOP_K):
        base = jnp.sum(jnp.where(eio == idx_ref[k:k + 1, :], st, 0), axis=0, keepdims=True)
        pos_ref[k:k + 1, :] = base + rank_ref[k:k + 1, :]


def _pos_call(top_idx, rank, starts):
    kk, n = top_idx.shape
    ne = starts.shape[0]
    tp = POS_TILE
    return pl.pallas_call(
        _pos_kernel,
        out_shape=jax.ShapeDtypeStruct((kk, n), I32),
        grid=(n // tp,),
        in_specs=[pl.BlockSpec((kk, tp), lambda i: (0, i)),
                  pl.BlockSpec((kk, tp), lambda i: (0, i)),
                  pl.BlockSpec((ne, 1), lambda i: (0, 0))],
        out_specs=pl.BlockSpec((kk, tp), lambda i: (0, i)),
        compiler_params=pltpu.CompilerParams(
            dimension_semantics=("arbitrary",), vmem_limit_bytes=VMEM_LIMIT),
        name="pos",
    )(top_idx, rank, starts.reshape(ne, 1))


def _sc_mesh():
    return plsc.VectorSubcoreMesh(core_axis_name="c", subcore_axis_name="s",
                                  num_cores=SC_CORES, num_subcores=SC_SUBCORES)


def _sc_worker():
    return lax.axis_index("s") * SC_CORES + lax.axis_index("c")


def _dispatch_call(pos_win, hp):
    n = hp.shape[0]
    n_workers = SC_CORES * SC_SUBCORES
    wins_per_worker = n // SC_WINDOW // n_workers

    def body(hp_hbm, pos_hbm, xs_hbm, idx_a, idx_b, rows_a, rows_b, load_a, load_b, scatter_sem):
        first_win = _sc_worker() * wins_per_worker
        idx_v, rows_v, load_sem = (idx_a, idx_b), (rows_a, rows_b), (load_a, load_b)

        def loads(win, slot):
            return (pltpu.make_async_copy(hp_hbm.at[pl.ds(win * SC_WINDOW, SC_WINDOW)], rows_v[slot],
                                          load_sem[slot]),
                    pltpu.make_async_copy(pos_hbm.at[win], idx_v[slot], load_sem[slot]))

        for cp in loads(first_win, 0):
            cp.start()

        @pl.loop(0, wins_per_worker, step=2)
        def _(w):
            for slot in range(2):
                win = first_win + w + slot
                for cp in loads(win, slot):
                    cp.wait()

                @pl.when(w + slot + 1 < wins_per_worker)
                def _():
                    for cp in loads(win + 1, 1 - slot):
                        cp.start()

                copies = [pltpu.async_copy(rows_v[slot], xs_hbm.at[idx_v[slot].at[k]], scatter_sem)
                          for k in range(TOP_K)]
                for cp in copies:
                    cp.wait()

    idx_buf = pltpu.VMEM((TOP_K, SC_WINDOW), I32)
    row_buf = pltpu.VMEM((SC_WINDOW,) + hp.shape[1:], hp.dtype)
    return pl.kernel(
        body,
        out_type=jax.ShapeDtypeStruct((n * TOP_K,) + hp.shape[1:], hp.dtype),
        mesh=_sc_mesh(),
        scratch_types=[idx_buf, idx_buf, row_buf, row_buf,
                       pltpu.SemaphoreType.DMA, pltpu.SemaphoreType.DMA, pltpu.SemaphoreType.DMA],
        name="dispatch",
    )(hp, pos_win)


def _collect_call(pos_win, y_sorted):
    n = pos_win.shape[0] * SC_WINDOW
    n_workers = SC_CORES * SC_SUBCORES
    wins_per_worker = n // SC_WINDOW // n_workers

    def body(ys_hbm, pos_hbm, out_hbm, idx_v, rows_a, rows_b, gather_sem, write_sem):
        first_win = _sc_worker() * wins_per_worker
        bufs = (rows_a, rows_b)

        @pl.loop(0, wins_per_worker)
        def _(w):
            win = first_win + w
            pltpu.sync_copy(pos_hbm.at[win], idx_v)

            def gather(k):
                return pltpu.async_copy(ys_hbm.at[idx_v.at[k]], bufs[k % 2], gather_sem)

            pending_gather = gather(0)
            pending_write = None
            for k in range(TOP_K):
                pending_gather.wait()
                if pending_write is not None:
                    for cp in pending_write:
                        cp.wait()
                if k + 1 < TOP_K:
                    pending_gather = gather(k + 1)
                pending_write = [
                    pltpu.async_copy(bufs[k % 2].at[:, j],
                                     out_hbm.at[j, k, pl.ds(win * SC_WINDOW, SC_WINDOW)], write_sem)
                    for j in range(PACK_ROWS)]
            for cp in pending_write:
                cp.wait()

    row_buf = pltpu.VMEM((SC_WINDOW,) + y_sorted.shape[1:], y_sorted.dtype)
    return pl.kernel(
        body,
        out_type=jax.ShapeDtypeStruct((PACK_ROWS, TOP_K, n, LANES), y_sorted.dtype),
        mesh=_sc_mesh(),
        scratch_types=[pltpu.VMEM((TOP_K, SC_WINDOW), I32), row_buf, row_buf,
                       pltpu.SemaphoreType.DMA, pltpu.SemaphoreType.DMA],
        name="collect",
    )(y_sorted, pos_win)


def _experts_kernel(pe_ref, pb_ref, pv_ref, st_ref, en_ref,
                    xs_hbm, wg_ref, wu_ref, wd_ref, y_hbm,
                    wgb_ref, wub_ref, wdb_ref, xbuf_ref, stage_ref,
                    in_sems, out_sems, state_ref):
    p = pl.program_id(0)
    n_pairs = pl.num_programs(0)
    e = pe_ref[p]
    blk = pb_ref[p]
    rb = xbuf_ref.shape[2]
    prev = jnp.maximum(p - 1, 0)
    nxt = jnp.minimum(p + 1, n_pairs - 1)
    live = pv_ref[p] == 1
    first = jnp.logical_or(p == 0, pb_ref[prev] != blk)
    block_ends = jnp.logical_or(pb_ref[nxt] != blk, pv_ref[nxt] == 0)
    last = jnp.logical_or(p == n_pairs - 1, block_ends)

    def in_copy(j, block, slot):
        return pltpu.make_async_copy(xs_hbm.at[pl.ds(block * rb, rb), j], xbuf_ref.at[slot, j],
                                     in_sems.at[slot])

    def out_copy(j, block, slot):
        return pltpu.make_async_copy(stage_ref.at[slot, j], y_hbm.at[pl.ds(block * rb, rb), j],
                                     out_sems.at[slot])

    def drain(slot):
        @pl.when(state_ref[1 + slot] == 1)
        def _():
            for j in range(PACK_ROWS):
                out_copy(j, 0, slot).wait()
            state_ref[1 + slot] = 0

    @pl.when(p == 0)
    def _():
        state_ref[0] = 0
        state_ref[1] = 0
        state_ref[2] = 0
        stage_ref[...] = jnp.zeros_like(stage_ref)
        for j in range(PACK_ROWS):
            in_copy(j, blk, 0).start()

    slot = jnp.where(jnp.logical_and(first, p > 0), 1 - state_ref[0], state_ref[0])
    state_ref[0] = slot

    @pl.when(jnp.logical_and(p < n_pairs - 1,
                             jnp.logical_and(pb_ref[nxt] != blk, pv_ref[nxt] == 1)))
    def _():
        for j in range(PACK_ROWS):
            in_copy(j, pb_ref[nxt], 1 - slot).start()

    @pl.when(first)
    def _():
        for j in range(PACK_ROWS):
            in_copy(j, blk, slot).wait()
        drain(slot)

    @pl.when(jnp.logical_or(p == 0, pe_ref[prev] != e))
    def _():
        wgb_ref[...] = wg_ref[0].astype(BF16)
        wub_ref[...] = wu_ref[0].astype(BF16)
        wdb_ref[...] = wd_ref[0].astype(BF16)

    lo_row = st_ref[e]
    hi_row = en_ref[e]

    def sub_block(s, row0, shared):
        sub = pl.ds(s * SUB_BLOCK, SUB_BLOCK)
        pieces = []
        for j in range(PACK_ROWS):
            w = xbuf_ref[slot, j, sub, :]
            pieces.append(lax.bitcast_convert_type(w.astype(jnp.int16), BF16))
            pieces.append(lax.bitcast_convert_type(
                lax.shift_right_logical(w, 16).astype(jnp.int16), BF16))
        xb = jnp.concatenate(pieces, axis=1)
        gate = jnp.dot(xb, wgb_ref[...], preferred_element_type=F32)
        up = jnp.dot(xb, wub_ref[...], preferred_element_type=F32)
        yb = jnp.dot((_silu(gate) * up).astype(BF16), wdb_ref[...], preferred_element_type=F32)
        if shared:
            rows = row0 + lax.broadcasted_iota(I32, (SUB_BLOCK, 1), 0)
            mine = jnp.logical_and(rows >= lo_row, rows < hi_row)
        for j in range(PACK_ROWS):
            word = pltpu.pack_elementwise(
                [yb[:, 2 * j * LANES:(2 * j + 1) * LANES], yb[:, (2 * j + 1) * LANES:(2 * j + 2) * LANES]],
                packed_dtype=BF16)
            if shared:
                word = jnp.where(mine, word, stage_ref[slot, j, sub, :])
            stage_ref[slot, j, sub, :] = word

    for s in range(rb // SUB_BLOCK):
        row0 = blk * rb + s * SUB_BLOCK
        touched = jnp.logical_and(live, jnp.logical_and(row0 < hi_row, row0 + SUB_BLOCK > lo_row))
        whole = jnp.logical_and(lo_row <= row0, hi_row >= row0 + SUB_BLOCK)

        @pl.when(jnp.logical_and(touched, whole))
        def _():
            sub_block(s, row0, shared=False)

        @pl.when(jnp.logical_and(touched, jnp.logical_not(whole)))
        def _():
            sub_block(s, row0, shared=True)

    @pl.when(jnp.logical_and(live, last))
    def _():
        for j in range(PACK_ROWS):
            out_copy(j, blk, slot).start()
        state_ref[1 + slot] = 1

    @pl.when(p == n_pairs - 1)
    def _():
        drain(0)
        drain(1)


def _experts_call(pair_e, pair_blk, pair_ok, starts, ends, xs, w_gate, w_up, w_down):
    ne, d, de = w_gate.shape
    rb = ROW_BLOCK
    n_pairs = pair_e.shape[0]
    grid_spec = pltpu.PrefetchScalarGridSpec(
        num_scalar_prefetch=5,
        grid=(n_pairs,),
        in_specs=[pl.BlockSpec(memory_space=pl.ANY),
                  pl.BlockSpec((1, d, de), lambda p, pe, pb, pv, st, en: (pe[p], 0, 0)),
                  pl.BlockSpec((1, d, de), lambda p, pe, pb, pv, st, en: (pe[p], 0, 0)),
                  pl.BlockSpec((1, de, d), lambda p, pe, pb, pv, st, en: (pe[p], 0, 0))],
        out_specs=pl.BlockSpec(memory_space=pl.ANY),
        scratch_shapes=[pltpu.VMEM((d, de), BF16), pltpu.VMEM((d, de), BF16),
                        pltpu.VMEM((de, d), BF16),
                        pltpu.VMEM((2, PACK_ROWS, rb, LANES), I32),
                        pltpu.VMEM((2, PACK_ROWS, rb, LANES), I32),
                        pltpu.SemaphoreType.DMA((2,)), pltpu.SemaphoreType.DMA((2,)),
                        pltpu.SMEM((3,), I32)],
    )
    return pl.pallas_call(
        _experts_kernel,
        out_shape=jax.ShapeDtypeStruct(xs.shape, xs.dtype),
        grid_spec=grid_spec,
        compiler_params=pltpu.CompilerParams(
            dimension_semantics=("arbitrary",), vmem_limit_bytes=VMEM_LIMIT),
        name="experts",
    )(pair_e, pair_blk, pair_ok, starts, ends, xs, w_gate, w_up, w_down)


def _combine_kernel(y0_ref, y1_ref, y2_ref, y3_ref, x1_ref, hp_ref, w_ref, g2_ref, fg_ref,
                    wsg_ref, wsu_ref, wsd_ref, o_ref):
    tm = x1_ref.shape[0]
    w = w_ref[...]
    parts = []
    for y_ref in (y0_ref, y1_ref, y2_ref, y3_ref):
        for half in range(2):
            acc = None
            for k in range(TOP_K):
                piece = pltpu.unpack_elementwise(y_ref[k], index=half, packed_dtype=BF16,
                                                 unpacked_dtype=F32) * w[:, k:k + 1]
                acc = piece if acc is None else acc + piece
            parts.append(acc)
    routed = jnp.concatenate(parts, axis=1)
    pieces = []
    for j in range(PACK_ROWS):
        word = hp_ref[pl.ds(j, tm, stride=PACK_ROWS), :]
        pieces.append(lax.bitcast_convert_type(word.astype(jnp.int16), BF16))
        pieces.append(lax.bitcast_convert_type(lax.shift_right_logical(word, 16).astype(jnp.int16), BF16))
    h2 = jnp.concatenate(pieces, axis=1)
    hid = _silu(jnp.dot(h2, wsg_ref[...], preferred_element_type=F32)) * jnp.dot(
        h2, wsu_ref[...], preferred_element_type=F32)
    shared = jnp.dot(hid.astype(BF16), wsd_ref[...], preferred_element_type=F32)
    x2 = x1_ref[...] + g2_ref[0] * (routed + shared)
    ms = jnp.mean(x2 * x2, axis=-1, keepdims=True)
    o_ref[...] = x2 * lax.rsqrt(ms + EPS) * fg_ref[...]


def _combine_kernel_into(*refs):
    _combine_kernel(*refs[:-2], refs[-1])


def _combine_call(y_tok, x1, hp, w_tok, g2, fgain, wsg, wsu, wsd, seq, part, prev_out):
    n, d = x1.shape
    dsh = wsg.shape[1]
    tm = MOVE_TILE
    per_b = seq // tm
    tiles = y_tok.shape[2] // tm
    t0 = part * tiles
    const = lambda i: (0, 0)

    def piece_spec(j):
        return pl.BlockSpec((None, TOP_K, tm, LANES), lambda i: (j, 0, i, 0))

    in_specs = [piece_spec(j) for j in range(PACK_ROWS)] + [
        pl.BlockSpec((tm, d), lambda i: (t0 + i, 0)),
        pl.BlockSpec((tm * PACK_ROWS, LANES), lambda i: (t0 + i, 0)),
        pl.BlockSpec((tm, TOP_K), lambda i: (t0 + i, 0)),
        pl.BlockSpec((1, 1, d), lambda i: ((t0 + i) // per_b, 0, 0)),
        pl.BlockSpec((1, d), const),
        pl.BlockSpec((d, dsh), const),
        pl.BlockSpec((d, dsh), const),
        pl.BlockSpec((dsh, d), const)]
    args = [y_tok, y_tok, y_tok, y_tok, x1, hp, w_tok, g2, fgain, wsg, wsu, wsd]
    aliases = {}
    body = _combine_kernel
    if prev_out is not None:
        in_specs.append(pl.BlockSpec(memory_space=pl.ANY))
        args.append(prev_out)
        aliases = {len(args) - 1: 0}
        body = _combine_kernel_into
    return pl.pallas_call(
        body,
        out_shape=jax.ShapeDtypeStruct((n, d), F32),
        grid=(tiles,),
        in_specs=in_specs,
        out_specs=pl.BlockSpec((tm, d), lambda i: (t0 + i, 0)),
        input_output_aliases=aliases,
        compiler_params=pltpu.CompilerParams(
            dimension_semantics=("arbitrary",), vmem_limit_bytes=VMEM_LIMIT),
        name="combine",
    )(*args)


def _pair_tables(counts, n_rows):
    ne = counts.shape[0]
    sizes = counts.astype(I32)
    ends = jnp.cumsum(sizes)
    starts = ends - sizes
    first_blk = starts // ROW_BLOCK
    last_blk = (ends - 1) // ROW_BLOCK
    n_pairs = jnp.where(sizes > 0, last_blk - first_blk + 1, 0)
    pair_end = jnp.cumsum(n_pairs)
    pair_start = pair_end - n_pairs
    max_pairs = n_rows // ROW_BLOCK + ne
    p = jnp.arange(max_pairs, dtype=I32)
    ok = p < pair_end[-1]
    pc = jnp.minimum(p, pair_end[-1] - 1)
    pair_e = jnp.minimum(jnp.searchsorted(pair_end, pc, side='right'), ne - 1).astype(I32)
    pair_blk = (first_blk[pair_e] + pc - pair_start[pair_e]).astype(I32)
    return pair_e, pair_blk, ok.astype(I32), starts.astype(I32), ends.astype(I32)


def kernel(x, c, w_ada, b_ada, w_in, lb_logits, hgrn_norm, lam_re, lam_im, log_dt, b_re, b_im,
           c_re, c_im, d_skip, w_glu, b_glu, w_out, w_router, router_bias, w_gate, w_up, w_down,
           ws_gate, ws_up, ws_down, final_gain):
    bsz, seq, d = x.shape
    n = bsz * seq
    fdim = HGRN_HEADS * HGRN_KDIM
    n_chunks = seq // CHUNK
    lb = jnp.cumsum(jax.nn.softmax(lb_logits.astype(F32), axis=0), axis=0)[0].reshape(1, fdim)

    mod = _mod_call(c, w_ada[0], b_ada[0]).reshape(bsz, 6, d)
    nb = w_in.shape[2] - 4 * fdim
    perm = jnp.concatenate([jnp.arange(0, nb, 2), jnp.arange(1, nb, 2)])
    w_in_b = w_in[0].astype(BF16)
    ut, out_a, ucm = _mix_front_call(x, mod, w_in_b[:, :4 * fdim], w_in_b[:, 4 * fdim:][:, perm].T, lb,
                                hgrn_norm[0].reshape(1, fdim))

    k_tab, p_tab, q_tab, a1, a2 = _s5_tables(lam_re[0], lam_im[0], log_dt[0], b_re[0], b_im[0],
                                             c_re[0], c_im[0])
    yt = _s5_call(ut, k_tab, p_tab, q_tab, a1, a2, n_chunks, bsz)

    wr_t = w_router[0].T
    wr_hi = wr_t.astype(BF16)
    wr_lo = (wr_t - wr_hi.astype(F32)).astype(BF16)
    w_out_b = w_out[0].astype(BF16)
    w_out_p = jnp.concatenate([w_out_b[:fdim], w_out_b[fdim:][perm]], axis=0)
    x1, hp, logits_t = _mix_back_call(
        x.reshape(n, d), out_a.reshape(n, fdim), yt, ucm, mod,
        d_skip[0][perm].reshape(nb, 1), w_glu[0][perm][:, perm].T.astype(BF16),
        b_glu[0][perm].reshape(nb, 1), w_out_p, wr_hi, wr_lo, seq)

    top_idx, top_w, rank, counts = _route_call(logits_t, router_bias[0])
    pair_e, pair_blk, pair_ok, starts, ends = _pair_tables(counts[:, 0], n * TOP_K)
    pos = _pos_call(top_idx, rank, starts)

    pos_win = pos.reshape(TOP_K, n // SC_WINDOW, SC_WINDOW).transpose(1, 0, 2)
    xs = _dispatch_call(pos_win, hp.reshape(n, PACK_ROWS, LANES))
    y_sorted = _experts_call(pair_e, pair_blk, pair_ok, starts, ends, xs, w_gate[0], w_up[0], w_down[0])
    w_tok = top_w.T
    g2 = mod[:, 5:6, :]
    fgain = final_gain.reshape(1, d)
    wsg, wsu, wsd = ws_gate[0].astype(BF16), ws_up[0].astype(BF16), ws_down[0].astype(BF16)
    wins = pos_win.shape[0] // TAIL_PARTS
    out = None
    for part in range(TAIL_PARTS):
        y_tok = _collect_call(pos_win[part * wins:(part + 1) * wins], y_sorted)
        out = _combine_call(y_tok, x1, hp, w_tok, g2, fgain, wsg, wsu, wsd, seq, part, out)
    return out.reshape(bsz, seq, d)
```

```python
import functools

import jax
import jax.numpy as jnp
from jax import lax
from jax.experimental import pallas as pl
from jax.experimental.pallas import tpu as pltpu
from jax.experimental.pallas import tpu_sc as plsc

F32 = jnp.float32
BF16 = jnp.bfloat16
I32 = jnp.int32

EPS = 1e-6
CHUNK = 64
HGRN_HEADS = 4
HGRN_KDIM = 128
S5_GROUP = 16
S5_STATE = 64
N_EXPERT_GROUPS = 8
TOPK_GROUPS = 4
TOP_K = 8
ROUTE_SCALE = 2.5
LANES = 128
PACK_ROWS = 4
SC_CORES = 2
SC_SUBCORES = 16
SC_WINDOW = 64

SEQ_TILE = 512
TOK_TILE = 512
ROUTE_TILE = 512
POS_TILE = 1024
MOVE_TILE = 512
ROW_BLOCK = 2048
SUB_BLOCK = 512
TAIL_PARTS = 4
VMEM_LIMIT = 56 * 1024 * 1024

_NT = (((1,), (1,)), ((), ()))
_TN = (((0,), (0,)), ((), ()))


def _sigmoid(v):
    return 0.5 * jnp.tanh(0.5 * v) + 0.5


def _silu(v):
    return v * _sigmoid(v)


def _bdot(a, b):
    return jnp.dot(a.astype(BF16), b.astype(BF16), preferred_element_type=F32)


def _store_packed(ref, val, n_rows):
    for j in range(PACK_ROWS):
        lo = val[:, 2 * j * LANES:(2 * j + 1) * LANES]
        hi = val[:, (2 * j + 1) * LANES:(2 * j + 2) * LANES]
        ref[pl.ds(j, n_rows, stride=PACK_ROWS), :] = pltpu.pack_elementwise([lo, hi], packed_dtype=BF16)


def _mod_kernel(c_ref, w_ref, b_ref, o_ref):
    o_ref[...] = _bdot(_silu(c_ref[...]), w_ref[...]) + b_ref[...]


def _mod_call(c, w_ada, b_ada):
    bsz, d = c.shape
    n_out = w_ada.shape[1]
    return pl.pallas_call(
        _mod_kernel,
        out_shape=jax.ShapeDtypeStruct((bsz, n_out), F32),
        grid=(n_out // d,),
        in_specs=[pl.BlockSpec((bsz, d), lambda j: (0, 0)),
                  pl.BlockSpec((d, d), lambda j: (0, j)),
                  pl.BlockSpec((1, d), lambda j: (0, j))],
        out_specs=pl.BlockSpec((bsz, d), lambda j: (0, j)),
        compiler_params=pltpu.CompilerParams(vmem_limit_bytes=VMEM_LIMIT),
        name="mod",
    )(c, w_ada, b_ada.reshape(1, n_out))


def _split_chunk_pairs(tile_even, tile_odd):
    low = lax.broadcasted_iota(I32, tile_even.shape, 1) < CHUNK
    first = jnp.where(low, tile_even, pltpu.roll(tile_odd, CHUNK, 1))
    second = jnp.where(low, pltpu.roll(tile_even, CHUNK, 1), tile_odd)
    return first, second


def _mix_front_kernel(x_ref, mod_ref, win_ref, wut_ref, lb_ref, gn_ref, ltri_ref,
                      ut_ref, oa_ref, ucm_ref, proj_ref, st_ref, flat_ref):
    fdim = HGRN_HEADS * HGRN_KDIM
    ts = x_ref.shape[1]
    pairs = wut_ref.shape[0] // 2
    rows = ts // CHUNK

    @pl.when(pl.program_id(1) == 0)
    def _():
        st_ref[...] = jnp.zeros_like(st_ref)

    x = x_ref[0]
    ms = jnp.mean(x * x, axis=-1, keepdims=True)
    h = x * lax.rsqrt(ms + EPS) * (1.0 + mod_ref[0, 1:2, :]) + mod_ref[0, 0:1, :]
    hb = h.astype(BF16)
    proj_ref[...] = jnp.dot(hb, win_ref[...], preferred_element_type=F32)
    u_t = lax.dot_general(wut_ref[...], hb, _NT, preferred_element_type=F32)
    ucm_ref[...] = u_t.astype(BF16)
    for m in range(ts // LANES):
        first, second = _split_chunk_pairs(u_t[:pairs, m * LANES:(m + 1) * LANES],
                                           u_t[pairs:, m * LANES:(m + 1) * LANES])
        flat_ref[2 * m * pairs:(2 * m + 1) * pairs, :] = first
        flat_ref[(2 * m + 1) * pairs:(2 * m + 2) * pairs, :] = second
    per_group = S5_GROUP // 2
    for q in range(pairs):
        ut_ref[q // per_group, :, (q % per_group) * LANES:(q % per_group + 1) * LANES] = (
            flat_ref[pl.ds(q, rows, stride=pairs), :])

    lb = lb_ref[...]
    gn = gn_ref[...]
    ltri = ltri_ref[...]
    row = lax.broadcasted_iota(I32, (CHUNK, CHUNK), 0)
    col = lax.broadcasted_iota(I32, (CHUNK, CHUNK), 1)
    causal = row >= col

    def chunk_step(ci, carry):
        r0 = pl.multiple_of(ci * CHUNK, CHUNK)
        q = proj_ref[pl.ds(r0, CHUNK), 0:fdim]
        fl = proj_ref[pl.ds(r0, CHUNK), fdim:2 * fdim]
        iv = proj_ref[pl.ds(r0, CHUNK), 2 * fdim:3 * fdim]
        og = proj_ref[pl.ds(r0, CHUNK), 3 * fdim:4 * fdim]
        f = lb + (1.0 - lb) * _sigmoid(fl)
        lf = jnp.log(f)
        lf_hi = lf.astype(BF16)
        rem = lf - lf_hi.astype(F32)
        lf_mid = rem.astype(BF16)
        lf_lo = (rem - lf_mid.astype(F32)).astype(BF16)
        b = (jnp.dot(ltri, lf_hi, preferred_element_type=F32)
             + jnp.dot(ltri, lf_mid, preferred_element_type=F32)
             + jnp.dot(ltri, lf_lo, preferred_element_type=F32))
        b_ref = b[CHUNK // 2 - 1:CHUNK // 2, :]
        b_last = b[CHUNK - 1:CHUNK, :]
        qs = _silu(q)
        kk = 1.0 - f
        qe = (qs * jnp.exp(b - b_ref)).astype(BF16)
        ke = (kk * jnp.exp(b_ref - b)).astype(BF16)
        qb = (qs * jnp.exp(b)).astype(BF16)
        k2 = (kk * jnp.exp(b_last - b)).astype(BF16)
        dec = jnp.exp(b_last)
        ivb = iv.astype(BF16)
        outs = []
        for hh in range(HGRN_HEADS):
            sl = slice(hh * HGRN_KDIM, (hh + 1) * HGRN_KDIM)
            att = lax.dot_general(qe[:, sl], ke[:, sl], _NT, preferred_element_type=F32)
            att = jnp.where(causal, att, 0.0)
            st = st_ref[hh]
            o = jnp.dot(att.astype(BF16), ivb[:, sl], preferred_element_type=F32)
            o = o + lax.dot_general(qb[:, sl], st.astype(BF16), _NT, preferred_element_type=F32)
            st_ref[hh] = st * dec[:, sl] + lax.dot_general(
                ivb[:, sl], k2[:, sl], _TN, preferred_element_type=F32)
            outs.append(o * lax.rsqrt(jnp.mean(o * o, axis=-1, keepdims=True) + EPS))
        o = jnp.concatenate(outs, axis=1) * gn * _silu(og)
        oa_ref[0, pl.ds(r0, CHUNK), :] = o.astype(BF16)
        return carry

    lax.fori_loop(0, ts // CHUNK, chunk_step, 0, unroll=True)


def _mix_front_call(x, mod, w_main, w_ut, lb, gn):
    bsz, seq, d = x.shape
    fdim = HGRN_HEADS * HGRN_KDIM
    ncols = w_main.shape[1]
    nb = w_ut.shape[0]
    groups = nb // S5_GROUP
    ltri = jnp.tril(jnp.ones((CHUNK, CHUNK), BF16))
    ts = SEQ_TILE
    tiles = seq // ts
    rows = ts // CHUNK
    return pl.pallas_call(
        _mix_front_kernel,
        out_shape=(jax.ShapeDtypeStruct((groups, bsz * seq // CHUNK, S5_GROUP * CHUNK), F32),
                   jax.ShapeDtypeStruct((bsz, seq, fdim), BF16),
                   jax.ShapeDtypeStruct((nb, bsz * seq), BF16)),
        grid=(bsz, tiles),
        in_specs=[pl.BlockSpec((1, ts, d), lambda b, j: (b, j, 0)),
                  pl.BlockSpec((1, 6, d), lambda b, j: (b, 0, 0)),
                  pl.BlockSpec((d, ncols), lambda b, j: (0, 0)),
                  pl.BlockSpec((nb, d), lambda b, j: (0, 0)),
                  pl.BlockSpec((1, fdim), lambda b, j: (0, 0)),
                  pl.BlockSpec((1, fdim), lambda b, j: (0, 0)),
                  pl.BlockSpec((CHUNK, CHUNK), lambda b, j: (0, 0))],
        out_specs=(pl.BlockSpec((groups, rows, S5_GROUP * CHUNK), lambda b, j: (0, b * tiles + j, 0)),
                   pl.BlockSpec((1, ts, fdim), lambda b, j: (b, j, 0)),
                   pl.BlockSpec((nb, ts), lambda b, j: (0, b * tiles + j))),
        scratch_shapes=[pltpu.VMEM((ts, ncols), F32),
                        pltpu.VMEM((HGRN_HEADS, fdim // HGRN_HEADS, HGRN_KDIM), F32),
                        pltpu.VMEM((rows * nb // 2, LANES), F32)],
        compiler_params=pltpu.CompilerParams(
            dimension_semantics=("arbitrary", "arbitrary"), vmem_limit_bytes=VMEM_LIMIT),
        name="mix_front",
    )(x, mod, w_main, w_ut, lb, gn, ltri)


def _s5_tables(lam_re, lam_im, log_dt, b_re, b_im, c_re, c_im):
    t = CHUNK
    hp = lax.Precision.HIGHEST
    lam = lax.complex(jnp.minimum(lam_re, -1e-4), lam_im)
    lam_dt = lam * jnp.exp(log_dt)[:, None]
    lam_bar = jnp.exp(lam_dt)
    b_bar = ((lam_bar - 1.0) / lam)[..., None] * lax.complex(b_re, b_im)
    c_mat = lax.complex(c_re, c_im)
    taus = jnp.arange(t + 1, dtype=F32)
    lam_pow = jnp.exp(lam_dt[:, None, :] * taus[None, :, None])
    g, p = lam.shape
    c = b_re.shape[-1]
    cl = c_mat[:, None, :, :] * lam_pow[:, :t, None, :]
    kr = (jnp.einsum('gtcp,gpi->gtci', cl.real, b_bar.real, precision=hp)
          - jnp.einsum('gtcp,gpi->gtci', cl.imag, b_bar.imag, precision=hp))
    k_tab = kr.transpose(0, 3, 2, 1).reshape(g, c, c // 2, 2 * t)
    pc = lam_pow[:, t - 1::-1][:, :t, :, None] * b_bar[:, None, :, :]
    pc = pc.transpose(0, 3, 1, 2).reshape(g, c * t, p)
    p_tab = jnp.concatenate([pc.real, pc.imag], axis=-1)
    ql = c_mat[:, None, :, :] * lam_pow[:, 1:t + 1, None, :]
    ql = ql.transpose(0, 3, 2, 1).reshape(g, p, c * t)
    q_tab = jnp.concatenate([ql.real, -ql.imag], axis=1)
    lam_t = lam_pow[:, t]
    a1 = jnp.concatenate([lam_t.real, lam_t.real], axis=-1)[:, None, :]
    a2 = jnp.concatenate([-lam_t.imag, lam_t.imag], axis=-1)[:, None, :]
    return k_tab, p_tab.astype(BF16), q_tab.astype(BF16), a1, a2


def _s5_kernel(u_ref, k_ref, p_ref, q_ref, a1_ref, a2_ref, y_ref, v_ref, xs_ref, m_ref,
               *, n_chunks, n_batch):
    n_in, n_pairs, _ = k_ref.shape[1:]
    lane = lax.broadcasted_iota(I32, (CHUNK, LANES), 1)
    causal = (lane & (CHUNK - 1)) >= lax.broadcasted_iota(I32, (CHUNK, LANES), 0)
    for ci in range(n_in):
        for a in range(n_pairs):
            lags = jnp.broadcast_to(k_ref[0, ci, a:a + 1, :], (CHUNK, LANES))
            tile = pltpu.roll(lags, 0, 1, stride=1, stride_axis=0)
            m_ref[ci * CHUNK:(ci + 1) * CHUNK, a * LANES:(a + 1) * LANES] = jnp.where(
                causal, tile, 0.0).astype(BF16)
    u = u_ref[0].astype(BF16)
    v_ref[...] = jnp.dot(u, p_ref[0], preferred_element_type=F32)
    a1 = a1_ref[0]
    a2 = a2_ref[0]
    half = xs_ref.shape[1] // 2

    def step(n, state):
        xs_ref[pl.ds(n, n_batch, stride=n_chunks), :] = state
        return (a1 * state + a2 * pltpu.roll(state, half, 1)
                + v_ref[pl.ds(n, n_batch, stride=n_chunks), :])

    lax.fori_loop(0, n_chunks, step, jnp.zeros((n_batch, xs_ref.shape[1]), F32))
    y = jnp.dot(u, m_ref[...], preferred_element_type=F32)
    y_ref[0] = y + jnp.dot(xs_ref[...].astype(BF16), q_ref[0], preferred_element_type=F32)


def _s5_call(ut, k_tab, p_tab, q_tab, a1, a2, n_chunks, n_batch):
    g, rows, width = ut.shape
    p2 = p_tab.shape[-1]
    return pl.pallas_call(
        functools.partial(_s5_kernel, n_chunks=n_chunks, n_batch=n_batch),
        out_shape=jax.ShapeDtypeStruct((g, rows, width), F32),
        grid=(g,),
        in_specs=[pl.BlockSpec((1, rows, width), lambda i: (i, 0, 0)),
                  pl.BlockSpec((1,) + k_tab.shape[1:], lambda i: (i, 0, 0, 0)),
                  pl.BlockSpec((1, width, p2), lambda i: (i, 0, 0)),
                  pl.BlockSpec((1, p2, width), lambda i: (i, 0, 0)),
                  pl.BlockSpec((1, 1, p2), lambda i: (i, 0, 0)),
                  pl.BlockSpec((1, 1, p2), lambda i: (i, 0, 0))],
        out_specs=pl.BlockSpec((1, rows, width), lambda i: (i, 0, 0)),
        scratch_shapes=[pltpu.VMEM((rows, p2), F32), pltpu.VMEM((rows, p2), F32),
                        pltpu.VMEM((width, width), BF16)],
        compiler_params=pltpu.CompilerParams(
            dimension_semantics=("arbitrary",), vmem_limit_bytes=VMEM_LIMIT),
        name="s5",
    )(ut, k_tab, p_tab, q_tab, a1, a2)


def _token_major(flat_ref, src_ref):
    groups, rows, _ = src_ref.shape
    per_group = S5_GROUP // 2
    pairs = groups * per_group
    for q in range(pairs):
        flat_ref[pl.ds(q, rows, stride=pairs), :] = (
            src_ref[q // per_group, :, (q % per_group) * LANES:(q % per_group + 1) * LANES])
    tiles = []
    for m in range(rows // 2):
        even, odd = _split_chunk_pairs(flat_ref[2 * m * pairs:(2 * m + 1) * pairs, :],
                                       flat_ref[(2 * m + 1) * pairs:(2 * m + 2) * pairs, :])
        tiles.append(jnp.concatenate([even, odd], axis=0))
    return jnp.concatenate(tiles, axis=1)


def _mix_back_kernel(x_ref, oa_ref, yt_ref, ucm_ref, mod_ref, dskip_ref, wglu_ref, bglu_ref,
                     wout_ref, wrh_ref, wrl_ref,
                     x1_ref, hp_ref, lt_ref, flat_ref):
    na = oa_ref.shape[1]
    y_t = _token_major(flat_ref, yt_ref)
    u_t = ucm_ref[...].astype(F32)
    z_t = jax.nn.gelu(y_t + dskip_ref[...] * u_t)
    gate_t = _sigmoid(jnp.dot(wglu_ref[...], z_t.astype(BF16), preferred_element_type=F32)
                      + bglu_ref[...])
    ob_t = (z_t * gate_t).astype(BF16)
    mixed = (jnp.dot(oa_ref[...], wout_ref[0:na, :], preferred_element_type=F32)
             + lax.dot_general(ob_t, wout_ref[na:, :], _TN, preferred_element_type=F32))
    x1 = x_ref[...] + mod_ref[0, 2:3, :] * mixed
    ms = jnp.mean(x1 * x1, axis=-1, keepdims=True)
    h2 = x1 * lax.rsqrt(ms + EPS) * (1.0 + mod_ref[0, 4:5, :]) + mod_ref[0, 3:4, :]
    _store_packed(hp_ref, h2, h2.shape[0])
    h_hi = h2.astype(BF16)
    h_lo = (h2 - h_hi.astype(F32)).astype(BF16)
    lt = lax.dot_general(wrh_ref[...], h_hi, _NT, preferred_element_type=F32)
    lt = lt + lax.dot_general(wrl_ref[...], h_hi, _NT, preferred_element_type=F32)
    lt = lt + lax.dot_general(wrh_ref[...], h_lo, _NT, preferred_element_type=F32)
    lt_ref[...] = lt
    x1_ref[...] = x1


def _mix_back_call(x2d, oa, yt, ucm, mod, dskip, wglu, bglu, wout, wrh, wrl, seq):
    n, d = x2d.shape
    nb = oa.shape[1]
    ne = wrh.shape[0]
    tm = TOK_TILE
    per_b = seq // tm
    groups, _, width = yt.shape
    rows = tm // CHUNK
    const = lambda i: (0, 0)
    flat_block = pl.BlockSpec((groups, rows, width), lambda i: (0, i, 0))
    return pl.pallas_call(
        _mix_back_kernel,
        out_shape=(jax.ShapeDtypeStruct((n, d), F32),
                   jax.ShapeDtypeStruct((n * PACK_ROWS, LANES), I32),
                   jax.ShapeDtypeStruct((ne, n), F32)),
        grid=(n // tm,),
        in_specs=[pl.BlockSpec((tm, d), lambda i: (i, 0)),
                  pl.BlockSpec((tm, nb), lambda i: (i, 0)),
                  flat_block,
                  pl.BlockSpec((nb, tm), lambda i: (0, i)),
                  pl.BlockSpec((1, 6, d), lambda i: (i // per_b, 0, 0)),
                  pl.BlockSpec((nb, 1), const),
                  pl.BlockSpec((nb, nb), const),
                  pl.BlockSpec((nb, 1), const),
                  pl.BlockSpec((d, d), const),
                  pl.BlockSpec((ne, d), const),
                  pl.BlockSpec((ne, d), const)],
        out_specs=(pl.BlockSpec((tm, d), lambda i: (i, 0)),
                   pl.BlockSpec((tm * PACK_ROWS, LANES), lambda i: (i, 0)),
                   pl.BlockSpec((ne, tm), lambda i: (0, i))),
        scratch_shapes=[pltpu.VMEM((rows * nb // 2, LANES), F32)],
        compiler_params=pltpu.CompilerParams(
            dimension_semantics=("arbitrary",), vmem_limit_bytes=VMEM_LIMIT),
        name="mix_back",
    )(x2d, oa, yt, ucm, mod, dskip, wglu, bglu, wout, wrh, wrl)


def _route_kernel(lt_ref, bias_ref, su_ref, idx_ref, w_ref, rank_ref, cnt_ref, run_ref):
    ne, tr = lt_ref.shape
    per_group = ne // N_EXPERT_GROUPS
    neg = -jnp.inf

    @pl.when(pl.program_id(0) == 0)
    def _():
        run_ref[...] = jnp.zeros_like(run_ref)

    s = _sigmoid(lt_ref[...])
    sel = s + bias_ref[...]
    gio = lax.broadcasted_iota(I32, (per_group, tr), 0)
    gscore = []
    for g in range(N_EXPERT_GROUPS):
        v = sel[g * per_group:(g + 1) * per_group, :]
        m1 = jnp.max(v, axis=0, keepdims=True)
        i1 = jnp.min(jnp.where(v == m1, gio, per_group), axis=0, keepdims=True)
        m2 = jnp.max(jnp.where(gio == i1, neg, v), axis=0, keepdims=True)
        gscore.append(m1 + m2)
    masked = []
    for g in range(N_EXPERT_GROUPS):
        ahead = jnp.zeros((1, tr), I32)
        for o in range(N_EXPERT_GROUPS):
            if o == g:
                continue
            wins = (gscore[o] >= gscore[g]) if o < g else (gscore[o] > gscore[g])
            ahead = ahead + wins.astype(I32)
        keep = ahead < TOPK_GROUPS
        masked.append(jnp.where(keep, sel[g * per_group:(g + 1) * per_group, :], neg))
    selm = jnp.concatenate(masked, axis=0)
    eio = lax.broadcasted_iota(I32, (ne, tr), 0)
    candidate = selm > neg
    idxs, ws = [], []
    for k in range(TOP_K):
        m = jnp.max(selm, axis=0, keepdims=True)
        ik = jnp.min(jnp.where(selm == m, eio, ne), axis=0, keepdims=True)
        onehot = eio == ik
        ws.append(jnp.sum(jnp.where(onehot, s, 0.0), axis=0, keepdims=True))
        selm = jnp.where(onehot, neg, selm)
        idxs.append(ik)
    hits = jnp.where(jnp.logical_and(candidate, selm == neg), 1.0, 0.0)
    wsum = ws[0]
    for k in range(1, TOP_K):
        wsum = wsum + ws[k]
    scale = ROUTE_SCALE / wsum
    ranks = jnp.dot(hits.astype(BF16), su_ref[...], preferred_element_type=F32) + run_ref[...]
    for k in range(TOP_K):
        idx_ref[k:k + 1, :] = idxs[k]
        w_ref[k:k + 1, :] = ws[k] * scale
        rk = jnp.sum(jnp.where(eio == idxs[k], ranks, 0.0), axis=0, keepdims=True)
        rank_ref[k:k + 1, :] = rk.astype(I32)
    run_ref[...] = run_ref[...] + jnp.sum(hits, axis=1, keepdims=True)
    cnt_ref[...] = run_ref[...]


def _route_call(lt, bias):
    ne, n = lt.shape
    tr = ROUTE_TILE
    su = jnp.triu(jnp.ones((tr, tr), F32), k=1).astype(BF16)
    return pl.pallas_call(
        _route_kernel,
        out_shape=(jax.ShapeDtypeStruct((TOP_K, n), I32),
                   jax.ShapeDtypeStruct((TOP_K, n), F32),
                   jax.ShapeDtypeStruct((TOP_K, n), I32),
                   jax.ShapeDtypeStruct((ne, 1), F32)),
        grid=(n // tr,),
        in_specs=[pl.BlockSpec((ne, tr), lambda i: (0, i)),
                  pl.BlockSpec((ne, 1), lambda i: (0, 0)),
                  pl.BlockSpec((tr, tr), lambda i: (0, 0))],
        out_specs=(pl.BlockSpec((TOP_K, tr), lambda i: (0, i)),
                   pl.BlockSpec((TOP_K, tr), lambda i: (0, i)),
                   pl.BlockSpec((TOP_K, tr), lambda i: (0, i)),
                   pl.BlockSpec((ne, 1), lambda i: (0, 0))),
        scratch_shapes=[pltpu.VMEM((ne, 1), F32)],
        compiler_params=pltpu.CompilerParams(
            dimension_semantics=("arbitrary",), vmem_limit_bytes=VMEM_LIMIT),
        name="route",
    )(lt, bias.reshape(ne, 1), su)


def _pos_kernel(idx_ref, rank_ref, st_ref, pos_ref):
    ne = st_ref.shape[0]
    tp = idx_ref.shape[1]
    eio = lax.broadcasted_iota(I32, (ne, tp), 0)
    st = st_ref[...]
    for k in range(TOP_K):
        base = jnp.sum(jnp.where(eio == idx_ref[k:k + 1, :], st, 0), axis=0, keepdims=True)
        pos_ref[k:k + 1, :] = base + rank_ref[k:k + 1, :]


def _pos_call(top_idx, rank, starts):
    kk, n = top_idx.shape
    ne = starts.shape[0]
    tp = POS_TILE
    return pl.pallas_call(
        _pos_kernel,
        out_shape=jax.ShapeDtypeStruct((kk, n), I32),
        grid=(n // tp,),
        in_specs=[pl.BlockSpec((kk, tp), lambda i: (0, i)),
                  pl.BlockSpec((kk, tp), lambda i: (0, i)),
                  pl.BlockSpec((ne, 1), lambda i: (0, 0))],
        out_specs=pl.BlockSpec((kk, tp), lambda i: (0, i)),
        compiler_params=pltpu.CompilerParams(
            dimension_semantics=("arbitrary",), vmem_limit_bytes=VMEM_LIMIT),
        name="pos",
    )(top_idx, rank, starts.reshape(ne, 1))


def _sc_mesh():
    return plsc.VectorSubcoreMesh(core_axis_name="c", subcore_axis_name="s",
                                  num_cores=SC_CORES, num_subcores=SC_SUBCORES)


def _sc_worker():
    return lax.axis_index("s") * SC_CORES + lax.axis_index("c")


def _dispatch_call(pos_win, hp):
    n = hp.shape[0]
    n_workers = SC_CORES * SC_SUBCORES
    wins_per_worker = n // SC_WINDOW // n_workers

    def body(hp_hbm, pos_hbm, xs_hbm, idx_a, idx_b, rows_a, rows_b, load_a, load_b, scatter_sem):
        first_win = _sc_worker() * wins_per_worker
        idx_v, rows_v, load_sem = (idx_a, idx_b), (rows_a, rows_b), (load_a, load_b)

        def loads(win, slot):
            return (pltpu.make_async_copy(hp_hbm.at[pl.ds(win * SC_WINDOW, SC_WINDOW)], rows_v[slot],
                                          load_sem[slot]),
                    pltpu.make_async_copy(pos_hbm.at[win], idx_v[slot], load_sem[slot]))

        for cp in loads(first_win, 0):
            cp.start()

        @pl.loop(0, wins_per_worker, step=2)
        def _(w):
            for slot in range(2):
                win = first_win + w + slot
                for cp in loads(win, slot):
                    cp.wait()

                @pl.when(w + slot + 1 < wins_per_worker)
                def _():
                    for cp in loads(win + 1, 1 - slot):
                        cp.start()

                copies = [pltpu.async_copy(rows_v[slot], xs_hbm.at[idx_v[slot].at[k]], scatter_sem)
                          for k in range(TOP_K)]
                for cp in copies:
                    cp.wait()

    idx_buf = pltpu.VMEM((TOP_K, SC_WINDOW), I32)
    row_buf = pltpu.VMEM((SC_WINDOW,) + hp.shape[1:], hp.dtype)
    return pl.kernel(
        body,
        out_type=jax.ShapeDtypeStruct((n * TOP_K,) + hp.shape[1:], hp.dtype),
        mesh=_sc_mesh(),
        scratch_types=[idx_buf, idx_buf, row_buf, row_buf,
                       pltpu.SemaphoreType.DMA, pltpu.SemaphoreType.DMA, pltpu.SemaphoreType.DMA],
        name="dispatch",
    )(hp, pos_win)


def _collect_call(pos_win, y_sorted):
    n = pos_win.shape[0] * SC_WINDOW
    n_workers = SC_CORES * SC_SUBCORES
    wins_per_worker = n // SC_WINDOW // n_workers

    def body(ys_hbm, pos_hbm, out_hbm, idx_v, rows_a, rows_b, gather_sem, write_sem):
        first_win = _sc_worker() * wins_per_worker
        bufs = (rows_a, rows_b)

        @pl.loop(0, wins_per_worker)
        def _(w):
            win = first_win + w
            pltpu.sync_copy(pos_hbm.at[win], idx_v)

            def gather(k):
                return pltpu.async_copy(ys_hbm.at[idx_v.at[k]], bufs[k % 2], gather_sem)

            pending_gather = gather(0)
            pending_write = None
            for k in range(TOP_K):
                pending_gather.wait()
                if pending_write is not None:
                    for cp in pending_write:
                        cp.wait()
                if k + 1 < TOP_K:
                    pending_gather = gather(k + 1)
                pending_write = [
                    pltpu.async_copy(bufs[k % 2].at[:, j],
                                     out_hbm.at[j, k, pl.ds(win * SC_WINDOW, SC_WINDOW)], write_sem)
                    for j in range(PACK_ROWS)]
            for cp in pending_write:
                cp.wait()

    row_buf = pltpu.VMEM((SC_WINDOW,) + y_sorted.shape[1:], y_sorted.dtype)
    return pl.kernel(
        body,
        out_type=jax.ShapeDtypeStruct((PACK_ROWS, TOP_K, n, LANES), y_sorted.dtype),
        mesh=_sc_mesh(),
        scratch_types=[pltpu.VMEM((TOP_K, SC_WINDOW), I32), row_buf, row_buf,
                       pltpu.SemaphoreType.DMA, pltpu.SemaphoreType.DMA],
        name="collect",
    )(y_sorted, pos_win)


def _experts_kernel(pe_ref, pb_ref, pv_ref, st_ref, en_ref, nx_ref,
                    xs_hbm, wg_hbm, wu_hbm, wd_hbm, y_hbm,
                    wgb_ref, wub_ref, wdb_ref, wgf_ref, wuf_ref, wdf_ref, xbuf_ref, stage_ref,
                    in_sems, out_sems, w_sems, state_ref):
    p = pl.program_id(0)
    n_pairs = pl.num_programs(0)
    e = pe_ref[p]
    blk = pb_ref[p]
    rb = xbuf_ref.shape[2]
    prev = jnp.maximum(p - 1, 0)
    nxt = jnp.minimum(p + 1, n_pairs - 1)
    live = pv_ref[p] == 1
    first = jnp.logical_or(p == 0, pb_ref[prev] != blk)
    block_ends = jnp.logical_or(pb_ref[nxt] != blk, pv_ref[nxt] == 0)
    last = jnp.logical_or(p == n_pairs - 1, block_ends)

    def in_copy(j, block, slot):
        return pltpu.make_async_copy(xs_hbm.at[pl.ds(block * rb, rb), j], xbuf_ref.at[slot, j],
                                     in_sems.at[slot])

    def out_copy(j, block, slot):
        return pltpu.make_async_copy(stage_ref.at[slot, j], y_hbm.at[pl.ds(block * rb, rb), j],
                                     out_sems.at[slot])

    def drain(slot):
        @pl.when(state_ref[1 + slot] == 1)
        def _():
            for j in range(PACK_ROWS):
                out_copy(j, 0, slot).wait()
            state_ref[1 + slot] = 0

    @pl.when(p == 0)
    def _():
        state_ref[0] = 0
        state_ref[1] = 0
        state_ref[2] = 0
        stage_ref[...] = jnp.zeros_like(stage_ref)
        for j in range(PACK_ROWS):
            in_copy(j, blk, 0).start()

    slot = jnp.where(jnp.logical_and(first, p > 0), 1 - state_ref[0], state_ref[0])
    state_ref[0] = slot

    @pl.when(jnp.logical_and(p < n_pairs - 1,
                             jnp.logical_and(pb_ref[nxt] != blk, pv_ref[nxt] == 1)))
    def _():
        for j in range(PACK_ROWS):
            in_copy(j, pb_ref[nxt], 1 - slot).start()

    @pl.when(first)
    def _():
        for j in range(PACK_ROWS):
            in_copy(j, blk, slot).wait()
        drain(slot)

    def w_copies(expert, wslot):
        return (pltpu.make_async_copy(wg_hbm.at[expert], wgf_ref.at[wslot], w_sems.at[wslot]),
                pltpu.make_async_copy(wu_hbm.at[expert], wuf_ref.at[wslot], w_sems.at[wslot]),
                pltpu.make_async_copy(wd_hbm.at[expert], wdf_ref.at[wslot], w_sems.at[wslot]))

    @pl.when(p == 0)
    def _():
        state_ref[3] = 0
        for cp in w_copies(e, 0):
            cp.start()

    new_expert = jnp.logical_or(p == 0, pe_ref[prev] != e)
    wslot = jnp.where(jnp.logical_and(new_expert, p > 0), 1 - state_ref[3], state_ref[3])
    state_ref[3] = wslot

    @pl.when(new_expert)
    def _():
        for cp in w_copies(e, wslot):
            cp.wait()
        wgb_ref[...] = wgf_ref[wslot].astype(BF16)
        wub_ref[...] = wuf_ref[wslot].astype(BF16)
        wdb_ref[...] = wdf_ref[wslot].astype(BF16)

        @pl.when(nx_ref[e] != e)
        def _():
            for cp in w_copies(nx_ref[e], 1 - wslot):
                cp.start()

    lo_row = st_ref[e]
    hi_row = en_ref[e]

    def sub_block(s, row0, shared):
        sub = pl.ds(s * SUB_BLOCK, SUB_BLOCK)
        pieces = []
        for j in range(PACK_ROWS):
            w = xbuf_ref[slot, j, sub, :]
            pieces.append(lax.bitcast_convert_type(w.astype(jnp.int16), BF16))
            pieces.append(lax.bitcast_convert_type(
                lax.shift_right_logical(w, 16).astype(jnp.int16), BF16))
        xb = jnp.concatenate(pieces, axis=1)
        gate = jnp.dot(xb, wgb_ref[...], preferred_element_type=F32)
        up = jnp.dot(xb, wub_ref[...], preferred_element_type=F32)
        yb = jnp.dot((_silu(gate) * up).astype(BF16), wdb_ref[...], preferred_element_type=F32)
        if shared:
            rows = row0 + lax.broadcasted_iota(I32, (SUB_BLOCK, 1), 0)
            mine = jnp.logical_and(rows >= lo_row, rows < hi_row)
        for j in range(PACK_ROWS):
            word = pltpu.pack_elementwise(
                [yb[:, 2 * j * LANES:(2 * j + 1) * LANES], yb[:, (2 * j + 1) * LANES:(2 * j + 2) * LANES]],
                packed_dtype=BF16)
            if shared:
                word = jnp.where(mine, word, stage_ref[slot, j, sub, :])
            stage_ref[slot, j, sub, :] = word

    for s in range(rb // SUB_BLOCK):
        row0 = blk * rb + s * SUB_BLOCK
        touched = jnp.logical_and(live, jnp.logical_and(row0 < hi_row, row0 + SUB_BLOCK > lo_row))
        whole = jnp.logical_and(lo_row <= row0, hi_row >= row0 + SUB_BLOCK)

        @pl.when(jnp.logical_and(touched, whole))
        def _():
            sub_block(s, row0, shared=False)

        @pl.when(jnp.logical_and(touched, jnp.logical_not(whole)))
        def _():
            sub_block(s, row0, shared=True)

    @pl.when(jnp.logical_and(live, last))
    def _():
        for j in range(PACK_ROWS):
            out_copy(j, blk, slot).start()
        state_ref[1 + slot] = 1

    @pl.when(p == n_pairs - 1)
    def _():
        drain(0)
        drain(1)


def _experts_call(pair_e, pair_blk, pair_ok, starts, ends, next_e, xs, w_gate, w_up, w_down):
    ne, d, de = w_gate.shape
    rb = ROW_BLOCK
    n_pairs = pair_e.shape[0]
    anywhere = pl.BlockSpec(memory_space=pl.ANY)
    grid_spec = pltpu.PrefetchScalarGridSpec(
        num_scalar_prefetch=6,
        grid=(n_pairs,),
        in_specs=[anywhere, anywhere, anywhere, anywhere],
        out_specs=anywhere,
        scratch_shapes=[pltpu.VMEM((d, de), BF16), pltpu.VMEM((d, de), BF16),
                        pltpu.VMEM((de, d), BF16),
                        pltpu.VMEM((2, d, de), F32), pltpu.VMEM((2, d, de), F32),
                        pltpu.VMEM((2, de, d), F32),
                        pltpu.VMEM((2, PACK_ROWS, rb, LANES), I32),
                        pltpu.VMEM((2, PACK_ROWS, rb, LANES), I32),
                        pltpu.SemaphoreType.DMA((2,)), pltpu.SemaphoreType.DMA((2,)),
                        pltpu.SemaphoreType.DMA((2,)),
                        pltpu.SMEM((4,), I32)],
    )
    return pl.pallas_call(
        _experts_kernel,
        out_shape=jax.ShapeDtypeStruct(xs.shape, xs.dtype),
        grid_spec=grid_spec,
        compiler_params=pltpu.CompilerParams(
            dimension_semantics=("arbitrary",), vmem_limit_bytes=VMEM_LIMIT),
        name="experts",
    )(pair_e, pair_blk, pair_ok, starts, ends, next_e, xs, w_gate, w_up, w_down)


def _combine_kernel(y0_ref, y1_ref, y2_ref, y3_ref, x1_ref, hp_ref, w_ref, g2_ref, fg_ref,
                    wsg_ref, wsu_ref, wsd_ref, o_ref):
    tm = x1_ref.shape[0]
    w = w_ref[...]
    parts = []
    for y_ref in (y0_ref, y1_ref, y2_ref, y3_ref):
        for half in range(2):
            acc = None
            for k in range(TOP_K):
                piece = pltpu.unpack_elementwise(y_ref[k], index=half, packed_dtype=BF16,
                                                 unpacked_dtype=F32) * w[:, k:k + 1]
                acc = piece if acc is None else acc + piece
            parts.append(acc)
    routed = jnp.concatenate(parts, axis=1)
    pieces = []
    for j in range(PACK_ROWS):
        word = hp_ref[pl.ds(j, tm, stride=PACK_ROWS), :]
        pieces.append(lax.bitcast_convert_type(word.astype(jnp.int16), BF16))
        pieces.append(lax.bitcast_convert_type(lax.shift_right_logical(word, 16).astype(jnp.int16), BF16))
    h2 = jnp.concatenate(pieces, axis=1)
    hid = _silu(jnp.dot(h2, wsg_ref[...], preferred_element_type=F32)) * jnp.dot(
        h2, wsu_ref[...], preferred_element_type=F32)
    shared = jnp.dot(hid.astype(BF16), wsd_ref[...], preferred_element_type=F32)
    x2 = x1_ref[...] + g2_ref[0] * (routed + shared)
    ms = jnp.mean(x2 * x2, axis=-1, keepdims=True)
    o_ref[...] = x2 * lax.rsqrt(ms + EPS) * fg_ref[...]


def _combine_kernel_into(*refs):
    _combine_kernel(*refs[:-2], refs[-1])


def _combine_call(y_tok, x1, hp, w_tok, g2, fgain, wsg, wsu, wsd, seq, part, prev_out):
    n, d = x1.shape
    dsh = wsg.shape[1]
    tm = MOVE_TILE
    per_b = seq // tm
    tiles = y_tok.shape[2] // tm
    t0 = part * tiles
    const = lambda i: (0, 0)

    def piece_spec(j):
        return pl.BlockSpec((None, TOP_K, tm, LANES), lambda i: (j, 0, i, 0))

    in_specs = [piece_spec(j) for j in range(PACK_ROWS)] + [
        pl.BlockSpec((tm, d), lambda i: (t0 + i, 0)),
        pl.BlockSpec((tm * PACK_ROWS, LANES), lambda i: (t0 + i, 0)),
        pl.BlockSpec((tm, TOP_K), lambda i: (t0 + i, 0)),
        pl.BlockSpec((1, 1, d), lambda i: ((t0 + i) // per_b, 0, 0)),
        pl.BlockSpec((1, d), const),
        pl.BlockSpec((d, dsh), const),
        pl.BlockSpec((d, dsh), const),
        pl.BlockSpec((dsh, d), const)]
    args = [y_tok, y_tok, y_tok, y_tok, x1, hp, w_tok, g2, fgain, wsg, wsu, wsd]
    aliases = {}
    body = _combine_kernel
    if prev_out is not None:
        in_specs.append(pl.BlockSpec(memory_space=pl.ANY))
        args.append(prev_out)
        aliases = {len(args) - 1: 0}
        body = _combine_kernel_into
    return pl.pallas_call(
        body,
        out_shape=jax.ShapeDtypeStruct((n, d), F32),
        grid=(tiles,),
        in_specs=in_specs,
        out_specs=pl.BlockSpec((tm, d), lambda i: (t0 + i, 0)),
        input_output_aliases=aliases,
        compiler_params=pltpu.CompilerParams(
            dimension_semantics=("arbitrary",), vmem_limit_bytes=VMEM_LIMIT),
        name="combine",
    )(*args)


def _pair_tables(counts, n_rows):
    ne = counts.shape[0]
    sizes = counts.astype(I32)
    ends = jnp.cumsum(sizes)
    starts = ends - sizes
    first_blk = starts // ROW_BLOCK
    last_blk = (ends - 1) // ROW_BLOCK
    n_pairs = jnp.where(sizes > 0, last_blk - first_blk + 1, 0)
    pair_end = jnp.cumsum(n_pairs)
    pair_start = pair_end - n_pairs
    max_pairs = n_rows // ROW_BLOCK + ne
    p = jnp.arange(max_pairs, dtype=I32)
    ok = p < pair_end[-1]
    pc = jnp.minimum(p, pair_end[-1] - 1)
    pair_e = jnp.minimum(jnp.searchsorted(pair_end, pc, side='right'), ne - 1).astype(I32)
    pair_blk = (first_blk[pair_e] + pc - pair_start[pair_e]).astype(I32)
    eid = jnp.arange(ne, dtype=I32)
    later = lax.cummin(jnp.where(sizes > 0, eid, ne), reverse=True)
    next_e = jnp.concatenate([later[1:], jnp.full((1,), ne, I32)])
    next_e = jnp.where(next_e < ne, next_e, eid)
    return pair_e, pair_blk, ok.astype(I32), starts.astype(I32), ends.astype(I32), next_e


def kernel(x, c, w_ada, b_ada, w_in, lb_logits, hgrn_norm, lam_re, lam_im, log_dt, b_re, b_im,
           c_re, c_im, d_skip, w_glu, b_glu, w_out, w_router, router_bias, w_gate, w_up, w_down,
           ws_gate, ws_up, ws_down, final_gain):
    bsz, seq, d = x.shape
    n = bsz * seq
    fdim = HGRN_HEADS * HGRN_KDIM
    n_chunks = seq // CHUNK
    lb = jnp.cumsum(jax.nn.softmax(lb_logits.astype(F32), axis=0), axis=0)[0].reshape(1, fdim)

    mod = _mod_call(c, w_ada[0], b_ada[0]).reshape(bsz, 6, d)
    nb = w_in.shape[2] - 4 * fdim
    perm = jnp.concatenate([jnp.arange(0, nb, 2), jnp.arange(1, nb, 2)])
    w_in_b = w_in[0].astype(BF16)
    ut, out_a, ucm = _mix_front_call(x, mod, w_in_b[:, :4 * fdim], w_in_b[:, 4 * fdim:][:, perm].T, lb,
                                hgrn_norm[0].reshape(1, fdim))

    k_tab, p_tab, q_tab, a1, a2 = _s5_tables(lam_re[0], lam_im[0], log_dt[0], b_re[0], b_im[0],
                                             c_re[0], c_im[0])
    yt = _s5_call(ut, k_tab, p_tab, q_tab, a1, a2, n_chunks, bsz)

    wr_t = w_router[0].T
    wr_hi = wr_t.astype(BF16)
    wr_lo = (wr_t - wr_hi.astype(F32)).astype(BF16)
    w_out_b = w_out[0].astype(BF16)
    w_out_p = jnp.concatenate([w_out_b[:fdim], w_out_b[fdim:][perm]], axis=0)
    x1, hp, logits_t = _mix_back_call(
        x.reshape(n, d), out_a.reshape(n, fdim), yt, ucm, mod,
        d_skip[0][perm].reshape(nb, 1), w_glu[0][perm][:, perm].T.astype(BF16),
        b_glu[0][perm].reshape(nb, 1), w_out_p, wr_hi, wr_lo, seq)

    top_idx, top_w, rank, counts = _route_call(logits_t, router_bias[0])
    pair_e, pair_blk, pair_ok, starts, ends, next_e = _pair_tables(counts[:, 0], n * TOP_K)
    pos = _pos_call(top_idx, rank, starts)

    pos_win = pos.reshape(TOP_K, n // SC_WINDOW, SC_WINDOW).transpose(1, 0, 2)
    xs = _dispatch_call(pos_win, hp.reshape(n, PACK_ROWS, LANES))
    y_sorted = _experts_call(pair_e, pair_blk, pair_ok, starts, ends, next_e, xs,
                             w_gate[0], w_up[0], w_down[0])
    w_tok = top_w.T
    g2 = mod[:, 5:6, :]
    fgain = final_gain.reshape(1, d)
    wsg, wsu, wsd = ws_gate[0].astype(BF16), ws_up[0].astype(BF16), ws_down[0].astype(BF16)
    wins = pos_win.shape[0] // TAIL_PARTS
    out = None
    for part in range(TAIL_PARTS):
        y_tok = _collect_call(pos_win[part * wins:(part + 1) * wins], y_sorted)
        out = _combine_call(y_tok, x1, hp, w_tok, g2, fgain, wsg, wsu, wsd, seq, part, out)
    return out.reshape(bsz, seq, d)
```

```python
import functools

import jax
import jax.numpy as jnp
from jax import lax
from jax.experimental import pallas as pl
from jax.experimental.pallas import tpu as pltpu
from jax.experimental.pallas import tpu_sc as plsc

F32 = jnp.float32
BF16 = jnp.bfloat16
I32 = jnp.int32

EPS = 1e-6
CHUNK = 64
HGRN_HEADS = 4
HGRN_KDIM = 128
S5_GROUP = 16
S5_STATE = 64
N_EXPERT_GROUPS = 8
TOPK_GROUPS = 4
TOP_K = 8
ROUTE_SCALE = 2.5
LANES = 128
PACK_ROWS = 4
SC_CORES = 2
SC_SUBCORES = 16
SC_WINDOW = 64

SEQ_TILE = 512
TOK_TILE = 512
ROUTE_TILE = 512
POS_TILE = 1024
MOVE_TILE = 512
ROW_BLOCK = 2048
SUB_BLOCK = 512
TAIL_PARTS = 4
VMEM_LIMIT = 56 * 1024 * 1024

_NT = (((1,), (1,)), ((), ()))
_TN = (((0,), (0,)), ((), ()))


def _sigmoid(v):
    return 0.5 * jnp.tanh(0.5 * v) + 0.5


def _silu(v):
    return v * _sigmoid(v)


def _bdot(a, b):
    return jnp.dot(a.astype(BF16), b.astype(BF16), preferred_element_type=F32)


def _store_packed(ref, val, n_rows):
    for j in range(PACK_ROWS):
        lo = val[:, 2 * j * LANES:(2 * j + 1) * LANES]
        hi = val[:, (2 * j + 1) * LANES:(2 * j + 2) * LANES]
        ref[pl.ds(j, n_rows, stride=PACK_ROWS), :] = pltpu.pack_elementwise([lo, hi], packed_dtype=BF16)


def _mod_kernel(c_ref, w_ref, b_ref, o_ref):
    o_ref[...] = _bdot(_silu(c_ref[...]), w_ref[...]) + b_ref[...]


def _mod_call(c, w_ada, b_ada):
    bsz, d = c.shape
    n_out = w_ada.shape[1]
    return pl.pallas_call(
        _mod_kernel,
        out_shape=jax.ShapeDtypeStruct((bsz, n_out), F32),
        grid=(n_out // d,),
        in_specs=[pl.BlockSpec((bsz, d), lambda j: (0, 0)),
                  pl.BlockSpec((d, d), lambda j: (0, j)),
                  pl.BlockSpec((1, d), lambda j: (0, j))],
        out_specs=pl.BlockSpec((bsz, d), lambda j: (0, j)),
        compiler_params=pltpu.CompilerParams(vmem_limit_bytes=VMEM_LIMIT),
        name="mod",
    )(c, w_ada, b_ada.reshape(1, n_out))


def _split_chunk_pairs(tile_even, tile_odd):
    low = lax.broadcasted_iota(I32, tile_even.shape, 1) < CHUNK
    first = jnp.where(low, tile_even, pltpu.roll(tile_odd, CHUNK, 1))
    second = jnp.where(low, pltpu.roll(tile_even, CHUNK, 1), tile_odd)
    return first, second


def _mix_front_kernel(x_ref, mod_ref, win_ref, wut_ref, lb_ref, gn_ref, ltri_ref,
                      ut_ref, oa_ref, ucm_ref, proj_ref, st_ref, flat_ref):
    fdim = HGRN_HEADS * HGRN_KDIM
    ts = x_ref.shape[1]
    pairs = wut_ref.shape[0] // 2
    rows = ts // CHUNK

    @pl.when(pl.program_id(1) == 0)
    def _():
        st_ref[...] = jnp.zeros_like(st_ref)

    x = x_ref[0]
    ms = jnp.mean(x * x, axis=-1, keepdims=True)
    h = x * lax.rsqrt(ms + EPS) * (1.0 + mod_ref[0, 1:2, :]) + mod_ref[0, 0:1, :]
    hb = h.astype(BF16)
    proj_ref[...] = jnp.dot(hb, win_ref[...], preferred_element_type=F32)
    u_t = lax.dot_general(wut_ref[...], hb, _NT, preferred_element_type=F32)
    ucm_ref[...] = u_t.astype(BF16)
    for m in range(ts // LANES):
        first, second = _split_chunk_pairs(u_t[:pairs, m * LANES:(m + 1) * LANES],
                                           u_t[pairs:, m * LANES:(m + 1) * LANES])
        flat_ref[2 * m * pairs:(2 * m + 1) * pairs, :] = first
        flat_ref[(2 * m + 1) * pairs:(2 * m + 2) * pairs, :] = second
    per_group = S5_GROUP // 2
    for q in range(pairs):
        ut_ref[q // per_group, :, (q % per_group) * LANES:(q % per_group + 1) * LANES] = (
            flat_ref[pl.ds(q, rows, stride=pairs), :])

    lb = lb_ref[...]
    gn = gn_ref[...]
    ltri = ltri_ref[...]
    row = lax.broadcasted_iota(I32, (CHUNK, CHUNK), 0)
    col = lax.broadcasted_iota(I32, (CHUNK, CHUNK), 1)
    causal = row >= col

    def chunk_step(ci, carry):
        r0 = pl.multiple_of(ci * CHUNK, CHUNK)
        q = proj_ref[pl.ds(r0, CHUNK), 0:fdim]
        fl = proj_ref[pl.ds(r0, CHUNK), fdim:2 * fdim]
        iv = proj_ref[pl.ds(r0, CHUNK), 2 * fdim:3 * fdim]
        og = proj_ref[pl.ds(r0, CHUNK), 3 * fdim:4 * fdim]
        f = lb + (1.0 - lb) * _sigmoid(fl)
        lf = jnp.log(f)
        lf_hi = lf.astype(BF16)
        rem = lf - lf_hi.astype(F32)
        lf_mid = rem.astype(BF16)
        lf_lo = (rem - lf_mid.astype(F32)).astype(BF16)
        b = (jnp.dot(ltri, lf_hi, preferred_element_type=F32)
             + jnp.dot(ltri, lf_mid, preferred_element_type=F32)
             + jnp.dot(ltri, lf_lo, preferred_element_type=F32))
        b_ref = b[CHUNK // 2 - 1:CHUNK // 2, :]
        b_last = b[CHUNK - 1:CHUNK, :]
        qs = _silu(q)
        kk = 1.0 - f
        qe = (qs * jnp.exp(b - b_ref)).astype(BF16)
        ke = (kk * jnp.exp(b_ref - b)).astype(BF16)
        qb = (qs * jnp.exp(b)).astype(BF16)
        k2 = (kk * jnp.exp(b_last - b)).astype(BF16)
        dec = jnp.exp(b_last)
        ivb = iv.astype(BF16)
        outs = []
        for hh in range(HGRN_HEADS):
            sl = slice(hh * HGRN_KDIM, (hh + 1) * HGRN_KDIM)
            att = lax.dot_general(qe[:, sl], ke[:, sl], _NT, preferred_element_type=F32)
            att = jnp.where(causal, att, 0.0)
            st = st_ref[hh]
            o = jnp.dot(att.astype(BF16), ivb[:, sl], preferred_element_type=F32)
            o = o + lax.dot_general(qb[:, sl], st.astype(BF16), _NT, preferred_element_type=F32)
            st_ref[hh] = st * dec[:, sl] + lax.dot_general(
                ivb[:, sl], k2[:, sl], _TN, preferred_element_type=F32)
            outs.append(o * lax.rsqrt(jnp.mean(o * o, axis=-1, keepdims=True) + EPS))
        o = jnp.concatenate(outs, axis=1) * gn * _silu(og)
        oa_ref[0, pl.ds(r0, CHUNK), :] = o.astype(BF16)
        return carry

    lax.fori_loop(0, ts // CHUNK, chunk_step, 0, unroll=True)


def _mix_front_call(x, mod, w_main, w_ut, lb, gn):
    bsz, seq, d = x.shape
    fdim = HGRN_HEADS * HGRN_KDIM
    ncols = w_main.shape[1]
    nb = w_ut.shape[0]
    groups = nb // S5_GROUP
    ltri = jnp.tril(jnp.ones((CHUNK, CHUNK), BF16))
    ts = SEQ_TILE
    tiles = seq // ts
    rows = ts // CHUNK
    return pl.pallas_call(
        _mix_front_kernel,
        out_shape=(jax.ShapeDtypeStruct((groups, bsz * seq // CHUNK, S5_GROUP * CHUNK), F32),
                   jax.ShapeDtypeStruct((bsz, seq, fdim), BF16),
                   jax.ShapeDtypeStruct((nb, bsz * seq), BF16)),
        grid=(bsz, tiles),
        in_specs=[pl.BlockSpec((1, ts, d), lambda b, j: (b, j, 0)),
                  pl.BlockSpec((1, 6, d), lambda b, j: (b, 0, 0)),
                  pl.BlockSpec((d, ncols), lambda b, j: (0, 0)),
                  pl.BlockSpec((nb, d), lambda b, j: (0, 0)),
                  pl.BlockSpec((1, fdim), lambda b, j: (0, 0)),
                  pl.BlockSpec((1, fdim), lambda b, j: (0, 0)),
                  pl.BlockSpec((CHUNK, CHUNK), lambda b, j: (0, 0))],
        out_specs=(pl.BlockSpec((groups, rows, S5_GROUP * CHUNK), lambda b, j: (0, b * tiles + j, 0)),
                   pl.BlockSpec((1, ts, fdim), lambda b, j: (b, j, 0)),
                   pl.BlockSpec((nb, ts), lambda b, j: (0, b * tiles + j))),
        scratch_shapes=[pltpu.VMEM((ts, ncols), F32),
                        pltpu.VMEM((HGRN_HEADS, fdim // HGRN_HEADS, HGRN_KDIM), F32),
                        pltpu.VMEM((rows * nb // 2, LANES), F32)],
        compiler_params=pltpu.CompilerParams(
            dimension_semantics=("arbitrary", "arbitrary"), vmem_limit_bytes=VMEM_LIMIT),
        name="mix_front",
    )(x, mod, w_main, w_ut, lb, gn, ltri)


def _s5_tables(lam_re, lam_im, log_dt, b_re, b_im, c_re, c_im):
    t = CHUNK
    hp = lax.Precision.HIGHEST
    lam = lax.complex(jnp.minimum(lam_re, -1e-4), lam_im)
    lam_dt = lam * jnp.exp(log_dt)[:, None]
    lam_bar = jnp.exp(lam_dt)
    b_bar = ((lam_bar - 1.0) / lam)[..., None] * lax.complex(b_re, b_im)
    c_mat = lax.complex(c_re, c_im)
    taus = jnp.arange(t + 1, dtype=F32)
    lam_pow = jnp.exp(lam_dt[:, None, :] * taus[None, :, None])
    g, p = lam.shape
    c = b_re.shape[-1]
    cl = c_mat[:, None, :, :] * lam_pow[:, :t, None, :]
    kr = (jnp.einsum('gtcp,gpi->gtci', cl.real, b_bar.real, precision=hp)
          - jnp.einsum('gtcp,gpi->gtci', cl.imag, b_bar.imag, precision=hp))
    k_tab = kr.transpose(0, 3, 2, 1).reshape(g, c, c // 2, 2 * t)
    pc = lam_pow[:, t - 1::-1][:, :t, :, None] * b_bar[:, None, :, :]
    pc = pc.transpose(0, 3, 1, 2).reshape(g, c * t, p)
    p_tab = jnp.concatenate([pc.real, pc.imag], axis=-1)
    ql = c_mat[:, None, :, :] * lam_pow[:, 1:t + 1, None, :]
    ql = ql.transpose(0, 3, 2, 1).reshape(g, p, c * t)
    q_tab = jnp.concatenate([ql.real, -ql.imag], axis=1)
    lam_t = lam_pow[:, t]
    a1 = jnp.concatenate([lam_t.real, lam_t.real], axis=-1)[:, None, :]
    a2 = jnp.concatenate([-lam_t.imag, lam_t.imag], axis=-1)[:, None, :]
    return k_tab, p_tab.astype(BF16), q_tab.astype(BF16), a1, a2


def _s5_kernel(u_ref, k_ref, p_ref, q_ref, a1_ref, a2_ref, y_ref, v_ref, xs_ref, m_ref,
               *, n_chunks, n_batch):
    n_in, n_pairs, _ = k_ref.shape[1:]
    lane = lax.broadcasted_iota(I32, (CHUNK, LANES), 1)
    causal = (lane & (CHUNK - 1)) >= lax.broadcasted_iota(I32, (CHUNK, LANES), 0)
    for ci in range(n_in):
        for a in range(n_pairs):
            lags = jnp.broadcast_to(k_ref[0, ci, a:a + 1, :], (CHUNK, LANES))
            tile = pltpu.roll(lags, 0, 1, stride=1, stride_axis=0)
            m_ref[ci * CHUNK:(ci + 1) * CHUNK, a * LANES:(a + 1) * LANES] = jnp.where(
                causal, tile, 0.0).astype(BF16)
    u = u_ref[0].astype(BF16)
    v_ref[...] = jnp.dot(u, p_ref[0], preferred_element_type=F32)
    a1 = a1_ref[0]
    a2 = a2_ref[0]
    half = xs_ref.shape[1] // 2

    def step(n, state):
        xs_ref[pl.ds(n, n_batch, stride=n_chunks), :] = state
        return (a1 * state + a2 * pltpu.roll(state, half, 1)
                + v_ref[pl.ds(n, n_batch, stride=n_chunks), :])

    lax.fori_loop(0, n_chunks, step, jnp.zeros((n_batch, xs_ref.shape[1]), F32))
    y = jnp.dot(u, m_ref[...], preferred_element_type=F32)
    y_ref[0] = y + jnp.dot(xs_ref[...].astype(BF16), q_ref[0], preferred_element_type=F32)


def _s5_call(ut, k_tab, p_tab, q_tab, a1, a2, n_chunks, n_batch):
    g, rows, width = ut.shape
    p2 = p_tab.shape[-1]
    return pl.pallas_call(
        functools.partial(_s5_kernel, n_chunks=n_chunks, n_batch=n_batch),
        out_shape=jax.ShapeDtypeStruct((g, rows, width), F32),
        grid=(g,),
        in_specs=[pl.BlockSpec((1, rows, width), lambda i: (i, 0, 0)),
                  pl.BlockSpec((1,) + k_tab.shape[1:], lambda i: (i, 0, 0, 0)),
                  pl.BlockSpec((1, width, p2), lambda i: (i, 0, 0)),
                  pl.BlockSpec((1, p2, width), lambda i: (i, 0, 0)),
                  pl.BlockSpec((1, 1, p2), lambda i: (i, 0, 0)),
                  pl.BlockSpec((1, 1, p2), lambda i: (i, 0, 0))],
        out_specs=pl.BlockSpec((1, rows, width), lambda i: (i, 0, 0)),
        scratch_shapes=[pltpu.VMEM((rows, p2), F32), pltpu.VMEM((rows, p2), F32),
                        pltpu.VMEM((width, width), BF16)],
        compiler_params=pltpu.CompilerParams(
            dimension_semantics=("arbitrary",), vmem_limit_bytes=VMEM_LIMIT),
        name="s5",
    )(ut, k_tab, p_tab, q_tab, a1, a2)


def _token_major(flat_ref, src_ref):
    groups, rows, _ = src_ref.shape
    per_group = S5_GROUP // 2
    pairs = groups * per_group
    for q in range(pairs):
        flat_ref[pl.ds(q, rows, stride=pairs), :] = (
            src_ref[q // per_group, :, (q % per_group) * LANES:(q % per_group + 1) * LANES])
    tiles = []
    for m in range(rows // 2):
        even, odd = _split_chunk_pairs(flat_ref[2 * m * pairs:(2 * m + 1) * pairs, :],
                                       flat_ref[(2 * m + 1) * pairs:(2 * m + 2) * pairs, :])
        tiles.append(jnp.concatenate([even, odd], axis=0))
    return jnp.concatenate(tiles, axis=1)


def _mix_back_kernel(x_ref, oa_ref, yt_ref, ucm_ref, mod_ref, dskip_ref, wglu_ref, bglu_ref,
                     wout_ref, wrh_ref, wrl_ref,
                     x1_ref, hp_ref, lt_ref, flat_ref):
    na = oa_ref.shape[1]
    y_t = _token_major(flat_ref, yt_ref)
    u_t = ucm_ref[...].astype(F32)
    z_t = jax.nn.gelu(y_t + dskip_ref[...] * u_t)
    gate_t = _sigmoid(jnp.dot(wglu_ref[...], z_t.astype(BF16), preferred_element_type=F32)
                      + bglu_ref[...])
    ob_t = (z_t * gate_t).astype(BF16)
    mixed = (jnp.dot(oa_ref[...], wout_ref[0:na, :], preferred_element_type=F32)
             + lax.dot_general(ob_t, wout_ref[na:, :], _TN, preferred_element_type=F32))
    x1 = x_ref[...] + mod_ref[0, 2:3, :] * mixed
    ms = jnp.mean(x1 * x1, axis=-1, keepdims=True)
    h2 = x1 * lax.rsqrt(ms + EPS) * (1.0 + mod_ref[0, 4:5, :]) + mod_ref[0, 3:4, :]
    _store_packed(hp_ref, h2, h2.shape[0])
    h_hi = h2.astype(BF16)
    h_lo = (h2 - h_hi.astype(F32)).astype(BF16)
    lt = lax.dot_general(wrh_ref[...], h_hi, _NT, preferred_element_type=F32)
    lt = lt + lax.dot_general(wrl_ref[...], h_hi, _NT, preferred_element_type=F32)
    lt = lt + lax.dot_general(wrh_ref[...], h_lo, _NT, preferred_element_type=F32)
    lt_ref[...] = lt
    x1_ref[...] = x1


def _mix_back_call(x2d, oa, yt, ucm, mod, dskip, wglu, bglu, wout, wrh, wrl, seq):
    n, d = x2d.shape
    nb = oa.shape[1]
    ne = wrh.shape[0]
    tm = TOK_TILE
    per_b = seq // tm
    groups, _, width = yt.shape
    rows = tm // CHUNK
    const = lambda i: (0, 0)
    flat_block = pl.BlockSpec((groups, rows, width), lambda i: (0, i, 0))
    return pl.pallas_call(
        _mix_back_kernel,
        out_shape=(jax.ShapeDtypeStruct((n, d), F32),
                   jax.ShapeDtypeStruct((n * PACK_ROWS, LANES), I32),
                   jax.ShapeDtypeStruct((ne, n), F32)),
        grid=(n // tm,),
        in_specs=[pl.BlockSpec((tm, d), lambda i: (i, 0)),
                  pl.BlockSpec((tm, nb), lambda i: (i, 0)),
                  flat_block,
                  pl.BlockSpec((nb, tm), lambda i: (0, i)),
                  pl.BlockSpec((1, 6, d), lambda i: (i // per_b, 0, 0)),
                  pl.BlockSpec((nb, 1), const),
                  pl.BlockSpec((nb, nb), const),
                  pl.BlockSpec((nb, 1), const),
                  pl.BlockSpec((d, d), const),
                  pl.BlockSpec((ne, d), const),
                  pl.BlockSpec((ne, d), const)],
        out_specs=(pl.BlockSpec((tm, d), lambda i: (i, 0)),
                   pl.BlockSpec((tm * PACK_ROWS, LANES), lambda i: (i, 0)),
                   pl.BlockSpec((ne, tm), lambda i: (0, i))),
        scratch_shapes=[pltpu.VMEM((rows * nb // 2, LANES), F32)],
        compiler_params=pltpu.CompilerParams(
            dimension_semantics=("arbitrary",), vmem_limit_bytes=VMEM_LIMIT),
        name="mix_back",
    )(x2d, oa, yt, ucm, mod, dskip, wglu, bglu, wout, wrh, wrl)


def _route_kernel(lt_ref, bias_ref, su_ref, idx_ref, w_ref, rank_ref, cnt_ref, run_ref):
    ne, tr = lt_ref.shape
    per_group = ne // N_EXPERT_GROUPS
    neg = -jnp.inf

    @pl.when(pl.program_id(0) == 0)
    def _():
        run_ref[...] = jnp.zeros_like(run_ref)

    s = _sigmoid(lt_ref[...])
    sel = s + bias_ref[...]
    gio = lax.broadcasted_iota(I32, (per_group, tr), 0)
    gscore = []
    for g in range(N_EXPERT_GROUPS):
        v = sel[g * per_group:(g + 1) * per_group, :]
        m1 = jnp.max(v, axis=0, keepdims=True)
        i1 = jnp.min(jnp.where(v == m1, gio, per_group), axis=0, keepdims=True)
        m2 = jnp.max(jnp.where(gio == i1, neg, v), axis=0, keepdims=True)
        gscore.append(m1 + m2)
    masked = []
    for g in range(N_EXPERT_GROUPS):
        ahead = jnp.zeros((1, tr), I32)
        for o in range(N_EXPERT_GROUPS):
            if o == g:
                continue
            wins = (gscore[o] >= gscore[g]) if o < g else (gscore[o] > gscore[g])
            ahead = ahead + wins.astype(I32)
        keep = ahead < TOPK_GROUPS
        masked.append(jnp.where(keep, sel[g * per_group:(g + 1) * per_group, :], neg))
    selm = jnp.concatenate(masked, axis=0)
    eio = lax.broadcasted_iota(I32, (ne, tr), 0)
    candidate = selm > neg
    idxs, ws = [], []
    for k in range(TOP_K):
        m = jnp.max(selm, axis=0, keepdims=True)
        ik = jnp.min(jnp.where(selm == m, eio, ne), axis=0, keepdims=True)
        onehot = eio == ik
        ws.append(jnp.sum(jnp.where(onehot, s, 0.0), axis=0, keepdims=True))
        selm = jnp.where(onehot, neg, selm)
        idxs.append(ik)
    hits = jnp.where(jnp.logical_and(candidate, selm == neg), 1.0, 0.0)
    wsum = ws[0]
    for k in range(1, TOP_K):
        wsum = wsum + ws[k]
    scale = ROUTE_SCALE / wsum
    ranks = jnp.dot(hits.astype(BF16), su_ref[...], preferred_element_type=F32) + run_ref[...]
    for k in range(TOP_K):
        idx_ref[k:k + 1, :] = idxs[k]
        w_ref[k:k + 1, :] = ws[k] * scale
        rk = jnp.sum(jnp.where(eio == idxs[k], ranks, 0.0), axis=0, keepdims=True)
        rank_ref[k:k + 1, :] = rk.astype(I32)
    run_ref[...] = run_ref[...] + jnp.sum(hits, axis=1, keepdims=True)
    cnt_ref[...] = run_ref[...]


def _route_call(lt, bias):
    ne, n = lt.shape
    tr = ROUTE_TILE
    su = jnp.triu(jnp.ones((tr, tr), F32), k=1).astype(BF16)
    return pl.pallas_call(
        _route_kernel,
        out_shape=(jax.ShapeDtypeStruct((TOP_K, n), I32),
                   jax.ShapeDtypeStruct((TOP_K, n), F32),
                   jax.ShapeDtypeStruct((TOP_K, n), I32),
                   jax.ShapeDtypeStruct((ne, 1), F32)),
        grid=(n // tr,),
        in_specs=[pl.BlockSpec((ne, tr), lambda i: (0, i)),
                  pl.BlockSpec((ne, 1), lambda i: (0, 0)),
                  pl.BlockSpec((tr, tr), lambda i: (0, 0))],
        out_specs=(pl.BlockSpec((TOP_K, tr), lambda i: (0, i)),
                   pl.BlockSpec((TOP_K, tr), lambda i: (0, i)),
                   pl.BlockSpec((TOP_K, tr), lambda i: (0, i)),
                   pl.BlockSpec((ne, 1), lambda i: (0, 0))),
        scratch_shapes=[pltpu.VMEM((ne, 1), F32)],
        compiler_params=pltpu.CompilerParams(
            dimension_semantics=("arbitrary",), vmem_limit_bytes=VMEM_LIMIT),
        name="route",
    )(lt, bias.reshape(ne, 1), su)


def _pos_kernel(idx_ref, rank_ref, st_ref, pos_ref):
    ne = st_ref.shape[0]
    tp = idx_ref.shape[1]
    eio = lax.broadcasted_iota(I32, (ne, tp), 0)
    st = st_ref[...]
    for k in range(TOP_K):
        base = jnp.sum(jnp.where(eio == idx_ref[k:k + 1, :], st, 0), axis=0, keepdims=True)
        pos_ref[k:k + 1, :] = base + rank_ref[k:k + 1, :]


def _pos_call(top_idx, rank, starts):
    kk, n = top_idx.shape
    ne = starts.shape[0]
    tp = POS_TILE
    return pl.pallas_call(
        _pos_kernel,
        out_shape=jax.ShapeDtypeStruct((kk, n), I32),
        grid=(n // tp,),
        in_specs=[pl.BlockSpec((kk, tp), lambda i: (0, i)),
                  pl.BlockSpec((kk, tp), lambda i: (0, i)),
                  pl.BlockSpec((ne, 1), lambda i: (0, 0))],
        out_specs=pl.BlockSpec((kk, tp), lambda i: (0, i)),
        compiler_params=pltpu.CompilerParams(
            dimension_semantics=("arbitrary",), vmem_limit_bytes=VMEM_LIMIT),
        name="pos",
    )(top_idx, rank, starts.reshape(ne, 1))


def _sc_mesh():
    return plsc.VectorSubcoreMesh(core_axis_name="c", subcore_axis_name="s",
                                  num_cores=SC_CORES, num_subcores=SC_SUBCORES)


def _sc_worker():
    return lax.axis_index("s") * SC_CORES + lax.axis_index("c")


def _dispatch_call(pos_win, hp):
    n = hp.shape[0]
    n_workers = SC_CORES * SC_SUBCORES
    wins_per_worker = n // SC_WINDOW // n_workers

    def body(hp_hbm, pos_hbm, xs_hbm, idx_a, idx_b, rows_a, rows_b, load_a, load_b, scatter_sem):
        first_win = _sc_worker() * wins_per_worker
        idx_v, rows_v, load_sem = (idx_a, idx_b), (rows_a, rows_b), (load_a, load_b)

        def loads(win, slot):
            return (pltpu.make_async_copy(hp_hbm.at[pl.ds(win * SC_WINDOW, SC_WINDOW)], rows_v[slot],
                                          load_sem[slot]),
                    pltpu.make_async_copy(pos_hbm.at[win], idx_v[slot], load_sem[slot]))

        for cp in loads(first_win, 0):
            cp.start()

        @pl.loop(0, wins_per_worker, step=2)
        def _(w):
            for slot in range(2):
                win = first_win + w + slot
                for cp in loads(win, slot):
                    cp.wait()

                @pl.when(w + slot + 1 < wins_per_worker)
                def _():
                    for cp in loads(win + 1, 1 - slot):
                        cp.start()

                copies = [pltpu.async_copy(rows_v[slot], xs_hbm.at[idx_v[slot].at[k]], scatter_sem)
                          for k in range(TOP_K)]
                for cp in copies:
                    cp.wait()

    idx_buf = pltpu.VMEM((TOP_K, SC_WINDOW), I32)
    row_buf = pltpu.VMEM((SC_WINDOW,) + hp.shape[1:], hp.dtype)
    return pl.kernel(
        body,
        out_type=jax.ShapeDtypeStruct((n * TOP_K,) + hp.shape[1:], hp.dtype),
        mesh=_sc_mesh(),
        scratch_types=[idx_buf, idx_buf, row_buf, row_buf,
                       pltpu.SemaphoreType.DMA, pltpu.SemaphoreType.DMA, pltpu.SemaphoreType.DMA],
        name="dispatch",
    )(hp, pos_win)


def _collect_call(pos_win, y_sorted):
    n = pos_win.shape[0] * SC_WINDOW
    n_workers = SC_CORES * SC_SUBCORES
    wins_per_worker = n // SC_WINDOW // n_workers

    def body(ys_hbm, pos_hbm, out_hbm, idx_v, rows_a, rows_b, gather_sem, write_sem):
        first_win = _sc_worker() * wins_per_worker
        bufs = (rows_a, rows_b)

        @pl.loop(0, wins_per_worker)
        def _(w):
            win = first_win + w
            pltpu.sync_copy(pos_hbm.at[win], idx_v)

            def gather(k):
                return pltpu.async_copy(ys_hbm.at[idx_v.at[k]], bufs[k % 2], gather_sem)

            pending_gather = gather(0)
            pending_write = None
            for k in range(TOP_K):
                pending_gather.wait()
                if pending_write is not None:
                    for cp in pending_write:
                        cp.wait()
                if k + 1 < TOP_K:
                    pending_gather = gather(k + 1)
                pending_write = [
                    pltpu.async_copy(bufs[k % 2].at[:, j],
                                     out_hbm.at[j, k, pl.ds(win * SC_WINDOW, SC_WINDOW)], write_sem)
                    for j in range(PACK_ROWS)]
            for cp in pending_write:
                cp.wait()

    row_buf = pltpu.VMEM((SC_WINDOW,) + y_sorted.shape[1:], y_sorted.dtype)
    return pl.kernel(
        body,
        out_type=jax.ShapeDtypeStruct((PACK_ROWS, TOP_K, n, LANES), y_sorted.dtype),
        mesh=_sc_mesh(),
        scratch_types=[pltpu.VMEM((TOP_K, SC_WINDOW), I32), row_buf, row_buf,
                       pltpu.SemaphoreType.DMA, pltpu.SemaphoreType.DMA],
        name="collect",
    )(y_sorted, pos_win)


def _experts_kernel(pe_ref, pb_ref, pv_ref, st_ref, en_ref, nx_ref,
                    xs_hbm, wg_hbm, wu_hbm, wd_hbm, y_hbm,
                    wgb_ref, wub_ref, wdb_ref, wgf_ref, wuf_ref, wdf_ref, xbuf_ref, stage_ref,
                    in_sems, out_sems, w_sems, state_ref):
    p = pl.program_id(0)
    n_pairs = pl.num_programs(0)
    e = pe_ref[p]
    blk = pb_ref[p]
    rb = xbuf_ref.shape[2]
    prev = jnp.maximum(p - 1, 0)
    nxt = jnp.minimum(p + 1, n_pairs - 1)
    live = pv_ref[p] == 1
    first = jnp.logical_or(p == 0, pb_ref[prev] != blk)
    block_ends = jnp.logical_or(pb_ref[nxt] != blk, pv_ref[nxt] == 0)
    last = jnp.logical_or(p == n_pairs - 1, block_ends)

    def in_copy(j, block, slot):
        return pltpu.make_async_copy(xs_hbm.at[pl.ds(block * rb, rb), j], xbuf_ref.at[slot, j],
                                     in_sems.at[slot])

    def out_copy(j, block, slot):
        return pltpu.make_async_copy(stage_ref.at[slot, j], y_hbm.at[pl.ds(block * rb, rb), j],
                                     out_sems.at[slot])

    def drain(slot):
        @pl.when(state_ref[1 + slot] == 1)
        def _():
            for j in range(PACK_ROWS):
                out_copy(j, 0, slot).wait()
            state_ref[1 + slot] = 0

    @pl.when(p == 0)
    def _():
        state_ref[0] = 0
        state_ref[1] = 0
        state_ref[2] = 0
        stage_ref[...] = jnp.zeros_like(stage_ref)
        for j in range(PACK_ROWS):
            in_copy(j, blk, 0).start()

    slot = jnp.where(jnp.logical_and(first, p > 0), 1 - state_ref[0], state_ref[0])
    state_ref[0] = slot

    @pl.when(jnp.logical_and(first, blk + 1 < y_hbm.shape[0] // rb))
    def _():
        for j in range(PACK_ROWS):
            in_copy(j, blk + 1, 1 - slot).start()

    @pl.when(first)
    def _():
        for j in range(PACK_ROWS):
            in_copy(j, blk, slot).wait()
        drain(slot)

    def w_copies(expert, wslot):
        return (pltpu.make_async_copy(wg_hbm.at[expert], wgf_ref.at[wslot], w_sems.at[wslot]),
                pltpu.make_async_copy(wu_hbm.at[expert], wuf_ref.at[wslot], w_sems.at[wslot]),
                pltpu.make_async_copy(wd_hbm.at[expert], wdf_ref.at[wslot], w_sems.at[wslot]))

    @pl.when(p == 0)
    def _():
        state_ref[3] = 0
        for cp in w_copies(e, 0):
            cp.start()

    new_expert = jnp.logical_or(p == 0, pe_ref[prev] != e)
    wslot = jnp.where(jnp.logical_and(new_expert, p > 0), 1 - state_ref[3], state_ref[3])
    state_ref[3] = wslot

    @pl.when(new_expert)
    def _():
        for cp in w_copies(e, wslot):
            cp.wait()
        wgb_ref[...] = wgf_ref[wslot].astype(BF16)
        wub_ref[...] = wuf_ref[wslot].astype(BF16)
        wdb_ref[...] = wdf_ref[wslot].astype(BF16)

        @pl.when(nx_ref[e] != e)
        def _():
            for cp in w_copies(nx_ref[e], 1 - wslot):
                cp.start()

    lo_row = st_ref[e]
    hi_row = en_ref[e]

    def sub_block(s, row0, shared):
        sub = pl.ds(s * SUB_BLOCK, SUB_BLOCK)
        pieces = []
        for j in range(PACK_ROWS):
            w = xbuf_ref[slot, j, sub, :]
            pieces.append(lax.bitcast_convert_type(w.astype(jnp.int16), BF16))
            pieces.append(lax.bitcast_convert_type(
                lax.shift_right_logical(w, 16).astype(jnp.int16), BF16))
        xb = jnp.concatenate(pieces, axis=1)
        gate = jnp.dot(xb, wgb_ref[...], preferred_element_type=F32)
        up = jnp.dot(xb, wub_ref[...], preferred_element_type=F32)
        yb = jnp.dot((_silu(gate) * up).astype(BF16), wdb_ref[...], preferred_element_type=F32)
        if shared:
            rows = row0 + lax.broadcasted_iota(I32, (SUB_BLOCK, 1), 0)
            mine = jnp.logical_and(rows >= lo_row, rows < hi_row)
        for j in range(PACK_ROWS):
            word = pltpu.pack_elementwise(
                [yb[:, 2 * j * LANES:(2 * j + 1) * LANES], yb[:, (2 * j + 1) * LANES:(2 * j + 2) * LANES]],
                packed_dtype=BF16)
            if shared:
                word = jnp.where(mine, word, stage_ref[slot, j, sub, :])
            stage_ref[slot, j, sub, :] = word

    for s in range(rb // SUB_BLOCK):
        row0 = blk * rb + s * SUB_BLOCK
        touched = jnp.logical_and(live, jnp.logical_and(row0 < hi_row, row0 + SUB_BLOCK > lo_row))
        whole = jnp.logical_and(lo_row <= row0, hi_row >= row0 + SUB_BLOCK)

        @pl.when(jnp.logical_and(touched, whole))
        def _():
            sub_block(s, row0, shared=False)

        @pl.when(jnp.logical_and(touched, jnp.logical_not(whole)))
        def _():
            sub_block(s, row0, shared=True)

    @pl.when(jnp.logical_and(live, last))
    def _():
        for j in range(PACK_ROWS):
            out_copy(j, blk, slot).start()
        state_ref[1 + slot] = 1

    @pl.when(p == n_pairs - 1)
    def _():
        drain(0)
        drain(1)


def _experts_call(pair_e, pair_blk, pair_ok, starts, ends, next_e, xs, w_gate, w_up, w_down):
    ne, d, de = w_gate.shape
    rb = ROW_BLOCK
    n_pairs = pair_e.shape[0]
    anywhere = pl.BlockSpec(memory_space=pl.ANY)
    grid_spec = pltpu.PrefetchScalarGridSpec(
        num_scalar_prefetch=6,
        grid=(n_pairs,),
        in_specs=[anywhere, anywhere, anywhere, anywhere],
        out_specs=anywhere,
        scratch_shapes=[pltpu.VMEM((d, de), BF16), pltpu.VMEM((d, de), BF16),
                        pltpu.VMEM((de, d), BF16),
                        pltpu.VMEM((2, d, de), F32), pltpu.VMEM((2, d, de), F32),
                        pltpu.VMEM((2, de, d), F32),
                        pltpu.VMEM((2, PACK_ROWS, rb, LANES), I32),
                        pltpu.VMEM((2, PACK_ROWS, rb, LANES), I32),
                        pltpu.SemaphoreType.DMA((2,)), pltpu.SemaphoreType.DMA((2,)),
                        pltpu.SemaphoreType.DMA((2,)),
                        pltpu.SMEM((4,), I32)],
    )
    return pl.pallas_call(
        _experts_kernel,
        out_shape=jax.ShapeDtypeStruct(xs.shape, xs.dtype),
        grid_spec=grid_spec,
        compiler_params=pltpu.CompilerParams(
            dimension_semantics=("arbitrary",), vmem_limit_bytes=VMEM_LIMIT),
        name="experts",
    )(pair_e, pair_blk, pair_ok, starts, ends, next_e, xs, w_gate, w_up, w_down)


def _combine_kernel(y0_ref, y1_ref, y2_ref, y3_ref, x1_ref, hp_ref, w_ref, g2_ref, fg_ref,
                    wsg_ref, wsu_ref, wsd_ref, o_ref):
    tm = x1_ref.shape[0]
    w = w_ref[...]
    parts = []
    for y_ref in (y0_ref, y1_ref, y2_ref, y3_ref):
        for half in range(2):
            acc = None
            for k in range(TOP_K):
                piece = pltpu.unpack_elementwise(y_ref[k], index=half, packed_dtype=BF16,
                                                 unpacked_dtype=F32) * w[:, k:k + 1]
                acc = piece if acc is None else acc + piece
            parts.append(acc)
    routed = jnp.concatenate(parts, axis=1)
    pieces = []
    for j in range(PACK_ROWS):
        word = hp_ref[pl.ds(j, tm, stride=PACK_ROWS), :]
        pieces.append(lax.bitcast_convert_type(word.astype(jnp.int16), BF16))
        pieces.append(lax.bitcast_convert_type(lax.shift_right_logical(word, 16).astype(jnp.int16), BF16))
    h2 = jnp.concatenate(pieces, axis=1)
    hid = _silu(jnp.dot(h2, wsg_ref[...], preferred_element_type=F32)) * jnp.dot(
        h2, wsu_ref[...], preferred_element_type=F32)
    shared = jnp.dot(hid.astype(BF16), wsd_ref[...], preferred_element_type=F32)
    x2 = x1_ref[...] + g2_ref[0] * (routed + shared)
    ms = jnp.mean(x2 * x2, axis=-1, keepdims=True)
    o_ref[...] = x2 * lax.rsqrt(ms + EPS) * fg_ref[...]


def _combine_kernel_into(*refs):
    _combine_kernel(*refs[:-2], refs[-1])


def _combine_call(y_tok, x1, hp, w_tok, g2, fgain, wsg, wsu, wsd, seq, part, prev_out):
    n, d = x1.shape
    dsh = wsg.shape[1]
    tm = MOVE_TILE
    per_b = seq // tm
    tiles = y_tok.shape[2] // tm
    t0 = part * tiles
    const = lambda i: (0, 0)

    def piece_spec(j):
        return pl.BlockSpec((None, TOP_K, tm, LANES), lambda i: (j, 0, i, 0))

    in_specs = [piece_spec(j) for j in range(PACK_ROWS)] + [
        pl.BlockSpec((tm, d), lambda i: (t0 + i, 0)),
        pl.BlockSpec((tm * PACK_ROWS, LANES), lambda i: (t0 + i, 0)),
        pl.BlockSpec((tm, TOP_K), lambda i: (t0 + i, 0)),
        pl.BlockSpec((1, 1, d), lambda i: ((t0 + i) // per_b, 0, 0)),
        pl.BlockSpec((1, d), const),
        pl.BlockSpec((d, dsh), const),
        pl.BlockSpec((d, dsh), const),
        pl.BlockSpec((dsh, d), const)]
    args = [y_tok, y_tok, y_tok, y_tok, x1, hp, w_tok, g2, fgain, wsg, wsu, wsd]
    aliases = {}
    body = _combine_kernel
    if prev_out is not None:
        in_specs.append(pl.BlockSpec(memory_space=pl.ANY))
        args.append(prev_out)
        aliases = {len(args) - 1: 0}
        body = _combine_kernel_into
    return pl.pallas_call(
        body,
        out_shape=jax.ShapeDtypeStruct((n, d), F32),
        grid=(tiles,),
        in_specs=in_specs,
        out_specs=pl.BlockSpec((tm, d), lambda i: (t0 + i, 0)),
        input_output_aliases=aliases,
        compiler_params=pltpu.CompilerParams(
            dimension_semantics=("arbitrary",), vmem_limit_bytes=VMEM_LIMIT),
        name="combine",
    )(*args)


def _pair_tables(counts, n_rows):
    ne = counts.shape[0]
    sizes = counts.astype(I32)
    ends = jnp.cumsum(sizes)
    starts = ends - sizes
    first_blk = starts // ROW_BLOCK
    last_blk = (ends - 1) // ROW_BLOCK
    n_pairs = jnp.where(sizes > 0, last_blk - first_blk + 1, 0)
    pair_end = jnp.cumsum(n_pairs)
    pair_start = pair_end - n_pairs
    max_pairs = n_rows // ROW_BLOCK + ne
    p = jnp.arange(max_pairs, dtype=I32)
    ok = p < pair_end[-1]
    pc = jnp.minimum(p, pair_end[-1] - 1)
    pair_e = jnp.minimum(jnp.searchsorted(pair_end, pc, side='right'), ne - 1).astype(I32)
    pair_blk = (first_blk[pair_e] + pc - pair_start[pair_e]).astype(I32)
    eid = jnp.arange(ne, dtype=I32)
    later = lax.cummin(jnp.where(sizes > 0, eid, ne), reverse=True)
    next_e = jnp.concatenate([later[1:], jnp.full((1,), ne, I32)])
    next_e = jnp.where(next_e < ne, next_e, eid)
    return pair_e, pair_blk, ok.astype(I32), starts.astype(I32), ends.astype(I32), next_e


def kernel(x, c, w_ada, b_ada, w_in, lb_logits, hgrn_norm, lam_re, lam_im, log_dt, b_re, b_im,
           c_re, c_im, d_skip, w_glu, b_glu, w_out, w_router, router_bias, w_gate, w_up, w_down,
           ws_gate, ws_up, ws_down, final_gain):
    bsz, seq, d = x.shape
    n = bsz * seq
    fdim = HGRN_HEADS * HGRN_KDIM
    n_chunks = seq // CHUNK
    lb = jnp.cumsum(jax.nn.softmax(lb_logits.astype(F32), axis=0), axis=0)[0].reshape(1, fdim)

    mod = _mod_call(c, w_ada[0], b_ada[0]).reshape(bsz, 6, d)
    nb = w_in.shape[2] - 4 * fdim
    perm = jnp.concatenate([jnp.arange(0, nb, 2), jnp.arange(1, nb, 2)])
    w_in_b = w_in[0].astype(BF16)
    ut, out_a, ucm = _mix_front_call(x, mod, w_in_b[:, :4 * fdim], w_in_b[:, 4 * fdim:][:, perm].T, lb,
                                hgrn_norm[0].reshape(1, fdim))

    k_tab, p_tab, q_tab, a1, a2 = _s5_tables(lam_re[0], lam_im[0], log_dt[0], b_re[0], b_im[0],
                                             c_re[0], c_im[0])
    yt = _s5_call(ut, k_tab, p_tab, q_tab, a1, a2, n_chunks, bsz)

    wr_t = w_router[0].T
    wr_hi = wr_t.astype(BF16)
    wr_lo = (wr_t - wr_hi.astype(F32)).astype(BF16)
    w_out_b = w_out[0].astype(BF16)
    w_out_p = jnp.concatenate([w_out_b[:fdim], w_out_b[fdim:][perm]], axis=0)
    x1, hp, logits_t = _mix_back_call(
        x.reshape(n, d), out_a.reshape(n, fdim), yt, ucm, mod,
        d_skip[0][perm].reshape(nb, 1), w_glu[0][perm][:, perm].T.astype(BF16),
        b_glu[0][perm].reshape(nb, 1), w_out_p, wr_hi, wr_lo, seq)

    top_idx, top_w, rank, counts = _route_call(logits_t, router_bias[0])
    pair_e, pair_blk, pair_ok, starts, ends, next_e = _pair_tables(counts[:, 0], n * TOP_K)
    pos = _pos_call(top_idx, rank, starts)

    pos_win = pos.reshape(TOP_K, n // SC_WINDOW, SC_WINDOW).transpose(1, 0, 2)
    xs = _dispatch_call(pos_win, hp.reshape(n, PACK_ROWS, LANES))
    y_sorted = _experts_call(pair_e, pair_blk, pair_ok, starts, ends, next_e, xs,
                             w_gate[0], w_up[0], w_down[0])
    w_tok = top_w.T
    g2 = mod[:, 5:6, :]
    fgain = final_gain.reshape(1, d)
    wsg, wsu, wsd = ws_gate[0].astype(BF16), ws_up[0].astype(BF16), ws_down[0].astype(BF16)
    wins = pos_win.shape[0] // TAIL_PARTS
    out = None
    for part in range(TAIL_PARTS):
        y_tok = _collect_call(pos_win[part * wins:(part + 1) * wins], y_sorted)
        out = _combine_call(y_tok, x1, hp, w_tok, g2, fgain, wsg, wsu, wsd, seq, part, out)
    return out.reshape(bsz, seq, d)
```

```python
import functools

import jax
import jax.numpy as jnp
from jax import lax
from jax.experimental import pallas as pl
from jax.experimental.pallas import tpu as pltpu
from jax.experimental.pallas import tpu_sc as plsc

F32 = jnp.float32
BF16 = jnp.bfloat16
I32 = jnp.int32

EPS = 1e-6
CHUNK = 64
HGRN_HEADS = 4
HGRN_KDIM = 128
S5_GROUP = 16
S5_STATE = 64
N_EXPERT_GROUPS = 8
TOPK_GROUPS = 4
TOP_K = 8
ROUTE_SCALE = 2.5
LANES = 128
PACK_ROWS = 4
SC_CORES = 2
SC_SUBCORES = 16
SC_WINDOW = 64

SEQ_TILE = 512
TOK_TILE = 512
ROUTE_TILE = 512
POS_TILE = 1024
MOVE_TILE = 512
ROW_BLOCK = 2048
SUB_BLOCK = 512
TAIL_PARTS = 4
VMEM_LIMIT = 56 * 1024 * 1024

_NT = (((1,), (1,)), ((), ()))
_TN = (((0,), (0,)), ((), ()))


def _sigmoid(v):
    return 0.5 * jnp.tanh(0.5 * v) + 0.5


def _silu(v):
    return v * _sigmoid(v)


def _bdot(a, b):
    return jnp.dot(a.astype(BF16), b.astype(BF16), preferred_element_type=F32)


def _store_packed(ref, val, n_rows):
    for j in range(PACK_ROWS):
        lo = val[:, 2 * j * LANES:(2 * j + 1) * LANES]
        hi = val[:, (2 * j + 1) * LANES:(2 * j + 2) * LANES]
        ref[pl.ds(j, n_rows, stride=PACK_ROWS), :] = pltpu.pack_elementwise([lo, hi], packed_dtype=BF16)


def _mod_kernel(c_ref, w_ref, b_ref, o_ref):
    o_ref[...] = _bdot(_silu(c_ref[...]), w_ref[...]) + b_ref[...]


def _mod_call(c, w_ada, b_ada):
    bsz, d = c.shape
    n_out = w_ada.shape[1]
    return pl.pallas_call(
        _mod_kernel,
        out_shape=jax.ShapeDtypeStruct((bsz, n_out), F32),
        grid=(n_out // d,),
        in_specs=[pl.BlockSpec((bsz, d), lambda j: (0, 0)),
                  pl.BlockSpec((d, d), lambda j: (0, j)),
                  pl.BlockSpec((1, d), lambda j: (0, j))],
        out_specs=pl.BlockSpec((bsz, d), lambda j: (0, j)),
        compiler_params=pltpu.CompilerParams(vmem_limit_bytes=VMEM_LIMIT),
        name="mod",
    )(c, w_ada, b_ada.reshape(1, n_out))


def _split_chunk_pairs(tile_even, tile_odd):
    low = lax.broadcasted_iota(I32, tile_even.shape, 1) < CHUNK
    first = jnp.where(low, tile_even, pltpu.roll(tile_odd, CHUNK, 1))
    second = jnp.where(low, pltpu.roll(tile_even, CHUNK, 1), tile_odd)
    return first, second


def _mix_front_kernel(x_ref, mod_ref, win_ref, wut_ref, lb_ref, gn_ref, ltri_ref,
                      ut_ref, oa_ref, ucm_ref, proj_ref, st_ref, flat_ref):
    fdim = HGRN_HEADS * HGRN_KDIM
    ts = x_ref.shape[1]
    pairs = wut_ref.shape[0] // 2
    rows = ts // CHUNK

    @pl.when(pl.program_id(1) == 0)
    def _():
        st_ref[...] = jnp.zeros_like(st_ref)

    x = x_ref[0]
    ms = jnp.mean(x * x, axis=-1, keepdims=True)
    h = x * lax.rsqrt(ms + EPS) * (1.0 + mod_ref[0, 1:2, :]) + mod_ref[0, 0:1, :]
    hb = h.astype(BF16)
    proj_ref[...] = jnp.dot(hb, win_ref[...], preferred_element_type=F32)
    u_t = lax.dot_general(wut_ref[...], hb, _NT, preferred_element_type=F32)
    ucm_ref[...] = u_t.astype(BF16)
    for m in range(ts // LANES):
        first, second = _split_chunk_pairs(u_t[:pairs, m * LANES:(m + 1) * LANES],
                                           u_t[pairs:, m * LANES:(m + 1) * LANES])
        flat_ref[2 * m * pairs:(2 * m + 1) * pairs, :] = first
        flat_ref[(2 * m + 1) * pairs:(2 * m + 2) * pairs, :] = second
    per_group = S5_GROUP // 2
    for q in range(pairs):
        ut_ref[q // per_group, :, (q % per_group) * LANES:(q % per_group + 1) * LANES] = (
            flat_ref[pl.ds(q, rows, stride=pairs), :])

    lb = lb_ref[...]
    gn = gn_ref[...]
    ltri = ltri_ref[...]
    row = lax.broadcasted_iota(I32, (CHUNK, CHUNK), 0)
    col = lax.broadcasted_iota(I32, (CHUNK, CHUNK), 1)
    causal = row >= col

    def chunk_step(ci, carry):
        r0 = pl.multiple_of(ci * CHUNK, CHUNK)
        q = proj_ref[pl.ds(r0, CHUNK), 0:fdim]
        fl = proj_ref[pl.ds(r0, CHUNK), fdim:2 * fdim]
        iv = proj_ref[pl.ds(r0, CHUNK), 2 * fdim:3 * fdim]
        og = proj_ref[pl.ds(r0, CHUNK), 3 * fdim:4 * fdim]
        f = lb + (1.0 - lb) * _sigmoid(fl)
        lf = jnp.log(f)
        lf_hi = lf.astype(BF16)
        rem = lf - lf_hi.astype(F32)
        lf_mid = rem.astype(BF16)
        lf_lo = (rem - lf_mid.astype(F32)).astype(BF16)
        b = (jnp.dot(ltri, lf_hi, preferred_element_type=F32)
             + jnp.dot(ltri, lf_mid, preferred_element_type=F32)
             + jnp.dot(ltri, lf_lo, preferred_element_type=F32))
        b_ref = b[CHUNK // 2 - 1:CHUNK // 2, :]
        b_last = b[CHUNK - 1:CHUNK, :]
        qs = _silu(q)
        kk = 1.0 - f
        qe = (qs * jnp.exp(b - b_ref)).astype(BF16)
        ke = (kk * jnp.exp(b_ref - b)).astype(BF16)
        qb = (qs * jnp.exp(b)).astype(BF16)
        k2 = (kk * jnp.exp(b_last - b)).astype(BF16)
        dec = jnp.exp(b_last)
        ivb = iv.astype(BF16)
        outs = []
        for hh in range(HGRN_HEADS):
            sl = slice(hh * HGRN_KDIM, (hh + 1) * HGRN_KDIM)
            att = lax.dot_general(qe[:, sl], ke[:, sl], _NT, preferred_element_type=F32)
            att = jnp.where(causal, att, 0.0)
            st = st_ref[hh]
            o = jnp.dot(att.astype(BF16), ivb[:, sl], preferred_element_type=F32)
            o = o + lax.dot_general(qb[:, sl], st.astype(BF16), _NT, preferred_element_type=F32)
            st_ref[hh] = st * dec[:, sl] + lax.dot_general(
                ivb[:, sl], k2[:, sl], _TN, preferred_element_type=F32)
            outs.append(o * lax.rsqrt(jnp.mean(o * o, axis=-1, keepdims=True) + EPS))
        o = jnp.concatenate(outs, axis=1) * gn * _silu(og)
        oa_ref[0, pl.ds(r0, CHUNK), :] = o.astype(BF16)
        return carry

    lax.fori_loop(0, ts // CHUNK, chunk_step, 0, unroll=True)


def _mix_front_call(x, mod, w_main, w_ut, lb, gn):
    bsz, seq, d = x.shape
    fdim = HGRN_HEADS * HGRN_KDIM
    ncols = w_main.shape[1]
    nb = w_ut.shape[0]
    groups = nb // S5_GROUP
    ltri = jnp.tril(jnp.ones((CHUNK, CHUNK), BF16))
    ts = SEQ_TILE
    tiles = seq // ts
    rows = ts // CHUNK
    return pl.pallas_call(
        _mix_front_kernel,
        out_shape=(jax.ShapeDtypeStruct((groups, bsz * seq // CHUNK, S5_GROUP * CHUNK), F32),
                   jax.ShapeDtypeStruct((bsz, seq, fdim), BF16),
                   jax.ShapeDtypeStruct((nb, bsz * seq), BF16)),
        grid=(bsz, tiles),
        in_specs=[pl.BlockSpec((1, ts, d), lambda b, j: (b, j, 0)),
                  pl.BlockSpec((1, 6, d), lambda b, j: (b, 0, 0)),
                  pl.BlockSpec((d, ncols), lambda b, j: (0, 0)),
                  pl.BlockSpec((nb, d), lambda b, j: (0, 0)),
                  pl.BlockSpec((1, fdim), lambda b, j: (0, 0)),
                  pl.BlockSpec((1, fdim), lambda b, j: (0, 0)),
                  pl.BlockSpec((CHUNK, CHUNK), lambda b, j: (0, 0))],
        out_specs=(pl.BlockSpec((groups, rows, S5_GROUP * CHUNK), lambda b, j: (0, b * tiles + j, 0)),
                   pl.BlockSpec((1, ts, fdim), lambda b, j: (b, j, 0)),
                   pl.BlockSpec((nb, ts), lambda b, j: (0, b * tiles + j))),
        scratch_shapes=[pltpu.VMEM((ts, ncols), F32),
                        pltpu.VMEM((HGRN_HEADS, fdim // HGRN_HEADS, HGRN_KDIM), F32),
                        pltpu.VMEM((rows * nb // 2, LANES), F32)],
        compiler_params=pltpu.CompilerParams(
            dimension_semantics=("arbitrary", "arbitrary"), vmem_limit_bytes=VMEM_LIMIT),
        name="mix_front",
    )(x, mod, w_main, w_ut, lb, gn, ltri)


def _s5_tables(lam_re, lam_im, log_dt, b_re, b_im, c_re, c_im):
    t = CHUNK
    hp = lax.Precision.HIGHEST
    lam = lax.complex(jnp.minimum(lam_re, -1e-4), lam_im)
    lam_dt = lam * jnp.exp(log_dt)[:, None]
    lam_bar = jnp.exp(lam_dt)
    b_bar = ((lam_bar - 1.0) / lam)[..., None] * lax.complex(b_re, b_im)
    c_mat = lax.complex(c_re, c_im)
    taus = jnp.arange(t + 1, dtype=F32)
    lam_pow = jnp.exp(lam_dt[:, None, :] * taus[None, :, None])
    g, p = lam.shape
    c = b_re.shape[-1]
    cl = c_mat[:, None, :, :] * lam_pow[:, :t, None, :]
    kr = (jnp.einsum('gtcp,gpi->gtci', cl.real, b_bar.real, precision=hp)
          - jnp.einsum('gtcp,gpi->gtci', cl.imag, b_bar.imag, precision=hp))
    k_tab = kr.transpose(0, 3, 2, 1).reshape(g, c, c // 2, 2 * t)
    pc = lam_pow[:, t - 1::-1][:, :t, :, None] * b_bar[:, None, :, :]
    pc = pc.transpose(0, 3, 1, 2).reshape(g, c * t, p)
    p_tab = jnp.concatenate([pc.real, pc.imag], axis=-1)
    ql = c_mat[:, None, :, :] * lam_pow[:, 1:t + 1, None, :]
    ql = ql.transpose(0, 3, 2, 1).reshape(g, p, c * t)
    q_tab = jnp.concatenate([ql.real, -ql.imag], axis=1)
    lam_t = lam_pow[:, t]
    a1 = jnp.concatenate([lam_t.real, lam_t.real], axis=-1)[:, None, :]
    a2 = jnp.concatenate([-lam_t.imag, lam_t.imag], axis=-1)[:, None, :]
    return k_tab, p_tab.astype(BF16), q_tab.astype(BF16), a1, a2


def _s5_kernel(u_ref, k_ref, p_ref, q_ref, a1_ref, a2_ref, y_ref, v_ref, xs_ref, m_ref,
               *, n_chunks, n_batch):
    n_in, n_pairs, _ = k_ref.shape[1:]
    lane = lax.broadcasted_iota(I32, (CHUNK, LANES), 1)
    causal = (lane & (CHUNK - 1)) >= lax.broadcasted_iota(I32, (CHUNK, LANES), 0)
    for ci in range(n_in):
        for a in range(n_pairs):
            lags = jnp.broadcast_to(k_ref[0, ci, a:a + 1, :], (CHUNK, LANES))
            tile = pltpu.roll(lags, 0, 1, stride=1, stride_axis=0)
            m_ref[ci * CHUNK:(ci + 1) * CHUNK, a * LANES:(a + 1) * LANES] = jnp.where(
                causal, tile, 0.0).astype(BF16)
    u = u_ref[0].astype(BF16)
    v_ref[...] = jnp.dot(u, p_ref[0], preferred_element_type=F32)
    a1 = a1_ref[0]
    a2 = a2_ref[0]
    half = xs_ref.shape[1] // 2

    def step(n, state):
        xs_ref[pl.ds(n, n_batch, stride=n_chunks), :] = state
        return (a1 * state + a2 * pltpu.roll(state, half, 1)
                + v_ref[pl.ds(n, n_batch, stride=n_chunks), :])

    lax.fori_loop(0, n_chunks, step, jnp.zeros((n_batch, xs_ref.shape[1]), F32))
    y = jnp.dot(u, m_ref[...], preferred_element_type=F32)
    y_ref[0] = y + jnp.dot(xs_ref[...].astype(BF16), q_ref[0], preferred_element_type=F32)


def _s5_call(ut, k_tab, p_tab, q_tab, a1, a2, n_chunks, n_batch):
    g, rows, width = ut.shape
    p2 = p_tab.shape[-1]
    return pl.pallas_call(
        functools.partial(_s5_kernel, n_chunks=n_chunks, n_batch=n_batch),
        out_shape=jax.ShapeDtypeStruct((g, rows, width), F32),
        grid=(g,),
        in_specs=[pl.BlockSpec((1, rows, width), lambda i: (i, 0, 0)),
                  pl.BlockSpec((1,) + k_tab.shape[1:], lambda i: (i, 0, 0, 0)),
                  pl.BlockSpec((1, width, p2), lambda i: (i, 0, 0)),
                  pl.BlockSpec((1, p2, width), lambda i: (i, 0, 0)),
                  pl.BlockSpec((1, 1, p2), lambda i: (i, 0, 0)),
                  pl.BlockSpec((1, 1, p2), lambda i: (i, 0, 0))],
        out_specs=pl.BlockSpec((1, rows, width), lambda i: (i, 0, 0)),
        scratch_shapes=[pltpu.VMEM((rows, p2), F32), pltpu.VMEM((rows, p2), F32),
                        pltpu.VMEM((width, width), BF16)],
        compiler_params=pltpu.CompilerParams(
            dimension_semantics=("arbitrary",), vmem_limit_bytes=VMEM_LIMIT),
        name="s5",
    )(ut, k_tab, p_tab, q_tab, a1, a2)


def _token_major(flat_ref, src_ref):
    groups, rows, _ = src_ref.shape
    per_group = S5_GROUP // 2
    pairs = groups * per_group
    for q in range(pairs):
        flat_ref[pl.ds(q, rows, stride=pairs), :] = (
            src_ref[q // per_group, :, (q % per_group) * LANES:(q % per_group + 1) * LANES])
    tiles = []
    for m in range(rows // 2):
        even, odd = _split_chunk_pairs(flat_ref[2 * m * pairs:(2 * m + 1) * pairs, :],
                                       flat_ref[(2 * m + 1) * pairs:(2 * m + 2) * pairs, :])
        tiles.append(jnp.concatenate([even, odd], axis=0))
    return jnp.concatenate(tiles, axis=1)


def _mix_back_kernel(x_ref, oa_ref, yt_ref, ucm_ref, mod_ref, dskip_ref, wglu_ref, bglu_ref,
                     wout_ref, wrh_ref, wrl_ref,
                     x1_ref, hp_ref, lt_ref, flat_ref):
    na = oa_ref.shape[1]
    y_t = _token_major(flat_ref, yt_ref)
    u_t = ucm_ref[...].astype(F32)
    z_t = jax.nn.gelu(y_t + dskip_ref[...] * u_t)
    gate_t = _sigmoid(jnp.dot(wglu_ref[...], z_t.astype(BF16), preferred_element_type=F32)
                      + bglu_ref[...])
    ob_t = (z_t * gate_t).astype(BF16)
    mixed = (jnp.dot(oa_ref[...], wout_ref[0:na, :], preferred_element_type=F32)
             + lax.dot_general(ob_t, wout_ref[na:, :], _TN, preferred_element_type=F32))
    x1 = x_ref[...] + mod_ref[0, 2:3, :] * mixed
    ms = jnp.mean(x1 * x1, axis=-1, keepdims=True)
    h2 = x1 * lax.rsqrt(ms + EPS) * (1.0 + mod_ref[0, 4:5, :]) + mod_ref[0, 3:4, :]
    _store_packed(hp_ref, h2, h2.shape[0])
    h_hi = h2.astype(BF16)
    h_lo = (h2 - h_hi.astype(F32)).astype(BF16)
    lt = lax.dot_general(wrh_ref[...], h_hi, _NT, preferred_element_type=F32)
    lt = lt + lax.dot_general(wrl_ref[...], h_hi, _NT, preferred_element_type=F32)
    lt = lt + lax.dot_general(wrh_ref[...], h_lo, _NT, preferred_element_type=F32)
    lt_ref[...] = lt
    x1_ref[...] = x1


def _mix_back_call(x2d, oa, yt, ucm, mod, dskip, wglu, bglu, wout, wrh, wrl, seq):
    n, d = x2d.shape
    nb = oa.shape[1]
    ne = wrh.shape[0]
    tm = TOK_TILE
    per_b = seq // tm
    groups, _, width = yt.shape
    rows = tm // CHUNK
    const = lambda i: (0, 0)
    flat_block = pl.BlockSpec((groups, rows, width), lambda i: (0, i, 0))
    return pl.pallas_call(
        _mix_back_kernel,
        out_shape=(jax.ShapeDtypeStruct((n, d), F32),
                   jax.ShapeDtypeStruct((n * PACK_ROWS, LANES), I32),
                   jax.ShapeDtypeStruct((ne, n), F32)),
        grid=(n // tm,),
        in_specs=[pl.BlockSpec((tm, d), lambda i: (i, 0)),
                  pl.BlockSpec((tm, nb), lambda i: (i, 0)),
                  flat_block,
                  pl.BlockSpec((nb, tm), lambda i: (0, i)),
                  pl.BlockSpec((1, 6, d), lambda i: (i // per_b, 0, 0)),
                  pl.BlockSpec((nb, 1), const),
                  pl.BlockSpec((nb, nb), const),
                  pl.BlockSpec((nb, 1), const),
                  pl.BlockSpec((d, d), const),
                  pl.BlockSpec((ne, d), const),
                  pl.BlockSpec((ne, d), const)],
        out_specs=(pl.BlockSpec((tm, d), lambda i: (i, 0)),
                   pl.BlockSpec((tm * PACK_ROWS, LANES), lambda i: (i, 0)),
                   pl.BlockSpec((ne, tm), lambda i: (0, i))),
        scratch_shapes=[pltpu.VMEM((rows * nb // 2, LANES), F32)],
        compiler_params=pltpu.CompilerParams(
            dimension_semantics=("arbitrary",), vmem_limit_bytes=VMEM_LIMIT),
        name="mix_back",
    )(x2d, oa, yt, ucm, mod, dskip, wglu, bglu, wout, wrh, wrl)


def _route_kernel(lt_ref, bias_ref, su_ref, idx_ref, w_ref, rank_ref, cnt_ref, run_ref):
    ne, tr = lt_ref.shape
    per_group = ne // N_EXPERT_GROUPS
    neg = -jnp.inf

    @pl.when(pl.program_id(0) == 0)
    def _():
        run_ref[...] = jnp.zeros_like(run_ref)

    s = _sigmoid(lt_ref[...])
    sel = s + bias_ref[...]
    gio = lax.broadcasted_iota(I32, (per_group, tr), 0)
    gscore = []
    for g in range(N_EXPERT_GROUPS):
        v = sel[g * per_group:(g + 1) * per_group, :]
        m1 = jnp.max(v, axis=0, keepdims=True)
        i1 = jnp.min(jnp.where(v == m1, gio, per_group), axis=0, keepdims=True)
        m2 = jnp.max(jnp.where(gio == i1, neg, v), axis=0, keepdims=True)
        gscore.append(m1 + m2)
    masked = []
    for g in range(N_EXPERT_GROUPS):
        ahead = jnp.zeros((1, tr), I32)
        for o in range(N_EXPERT_GROUPS):
            if o == g:
                continue
            wins = (gscore[o] >= gscore[g]) if o < g else (gscore[o] > gscore[g])
            ahead = ahead + wins.astype(I32)
        keep = ahead < TOPK_GROUPS
        masked.append(jnp.where(keep, sel[g * per_group:(g + 1) * per_group, :], neg))
    selm = jnp.concatenate(masked, axis=0)
    eio = lax.broadcasted_iota(I32, (ne, tr), 0)
    candidate = selm > neg
    idxs, ws = [], []
    for k in range(TOP_K):
        m = jnp.max(selm, axis=0, keepdims=True)
        ik = jnp.min(jnp.where(selm == m, eio, ne), axis=0, keepdims=True)
        onehot = eio == ik
        ws.append(jnp.sum(jnp.where(onehot, s, 0.0), axis=0, keepdims=True))
        selm = jnp.where(onehot, neg, selm)
        idxs.append(ik)
    hits = jnp.where(jnp.logical_and(candidate, selm == neg), 1.0, 0.0)
    wsum = ws[0]
    for k in range(1, TOP_K):
        wsum = wsum + ws[k]
    scale = ROUTE_SCALE / wsum
    ranks = jnp.dot(hits.astype(BF16), su_ref[...], preferred_element_type=F32) + run_ref[...]
    for k in range(TOP_K):
        idx_ref[k:k + 1, :] = idxs[k]
        w_ref[k:k + 1, :] = ws[k] * scale
        rk = jnp.sum(jnp.where(eio == idxs[k], ranks, 0.0), axis=0, keepdims=True)
        rank_ref[k:k + 1, :] = rk.astype(I32)
    run_ref[...] = run_ref[...] + jnp.sum(hits, axis=1, keepdims=True)
    cnt_ref[...] = run_ref[...]


def _route_call(lt, bias):
    ne, n = lt.shape
    tr = ROUTE_TILE
    su = jnp.triu(jnp.ones((tr, tr), F32), k=1).astype(BF16)
    return pl.pallas_call(
        _route_kernel,
        out_shape=(jax.ShapeDtypeStruct((TOP_K, n), I32),
                   jax.ShapeDtypeStruct((TOP_K, n), F32),
                   jax.ShapeDtypeStruct((TOP_K, n), I32),
                   jax.ShapeDtypeStruct((ne, 1), F32)),
        grid=(n // tr,),
        in_specs=[pl.BlockSpec((ne, tr), lambda i: (0, i)),
                  pl.BlockSpec((ne, 1), lambda i: (0, 0)),
                  pl.BlockSpec((tr, tr), lambda i: (0, 0))],
        out_specs=(pl.BlockSpec((TOP_K, tr), lambda i: (0, i)),
                   pl.BlockSpec((TOP_K, tr), lambda i: (0, i)),
                   pl.BlockSpec((TOP_K, tr), lambda i: (0, i)),
                   pl.BlockSpec((ne, 1), lambda i: (0, 0))),
        scratch_shapes=[pltpu.VMEM((ne, 1), F32)],
        compiler_params=pltpu.CompilerParams(
            dimension_semantics=("arbitrary",), vmem_limit_bytes=VMEM_LIMIT),
        name="route",
    )(lt, bias.reshape(ne, 1), su)


def _pos_kernel(idx_ref, rank_ref, st_ref, pos_ref):
    ne = st_ref.shape[0]
    tp = idx_ref.shape[1]
    eio = lax.broadcasted_iota(I32, (ne, tp), 0)
    st = st_ref[...]
    for k in range(TOP_K):
        base = jnp.sum(jnp.where(eio == idx_ref[k:k + 1, :], st, 0), axis=0, keepdims=True)
        pos_ref[k:k + 1, :] = base + rank_ref[k:k + 1, :]


def _pos_call(top_idx, rank, starts):
    kk, n = top_idx.shape
    ne = starts.shape[0]
    tp = POS_TILE
    return pl.pallas_call(
        _pos_kernel,
        out_shape=jax.ShapeDtypeStruct((kk, n), I32),
        grid=(n // tp,),
        in_specs=[pl.BlockSpec((kk, tp), lambda i: (0, i)),
                  pl.BlockSpec((kk, tp), lambda i: (0, i)),
                  pl.BlockSpec((ne, 1), lambda i: (0, 0))],
        out_specs=pl.BlockSpec((kk, tp), lambda i: (0, i)),
        compiler_params=pltpu.CompilerParams(
            dimension_semantics=("arbitrary",), vmem_limit_bytes=VMEM_LIMIT),
        name="pos",
    )(top_idx, rank, starts.reshape(ne, 1))


def _sc_mesh():
    return plsc.VectorSubcoreMesh(core_axis_name="c", subcore_axis_name="s",
                                  num_cores=SC_CORES, num_subcores=SC_SUBCORES)


def _sc_worker():
    return lax.axis_index("s") * SC_CORES + lax.axis_index("c")


def _dispatch_call(pos_win, hp):
    n = hp.shape[0]
    n_workers = SC_CORES * SC_SUBCORES
    wins_per_worker = n // SC_WINDOW // n_workers

    def body(hp_hbm, pos_hbm, xs_hbm, idx_a, idx_b, rows_a, rows_b, load_a, load_b, scatter_sem):
        first_win = _sc_worker() * wins_per_worker
        idx_v, rows_v, load_sem = (idx_a, idx_b), (rows_a, rows_b), (load_a, load_b)

        def loads(win, slot):
            return (pltpu.make_async_copy(hp_hbm.at[pl.ds(win * SC_WINDOW, SC_WINDOW)], rows_v[slot],
                                          load_sem[slot]),
                    pltpu.make_async_copy(pos_hbm.at[win], idx_v[slot], load_sem[slot]))

        for cp in loads(first_win, 0):
            cp.start()

        @pl.loop(0, wins_per_worker, step=2)
        def _(w):
            for slot in range(2):
                win = first_win + w + slot
                for cp in loads(win, slot):
                    cp.wait()

                @pl.when(w + slot + 1 < wins_per_worker)
                def _():
                    for cp in loads(win + 1, 1 - slot):
                        cp.start()

                copies = [pltpu.async_copy(rows_v[slot], xs_hbm.at[idx_v[slot].at[k]], scatter_sem)
                          for k in range(TOP_K)]
                for cp in copies:
                    cp.wait()

    idx_buf = pltpu.VMEM((TOP_K, SC_WINDOW), I32)
    row_buf = pltpu.VMEM((SC_WINDOW,) + hp.shape[1:], hp.dtype)
    return pl.kernel(
        body,
        out_type=jax.ShapeDtypeStruct((n * TOP_K,) + hp.shape[1:], hp.dtype),
        mesh=_sc_mesh(),
        scratch_types=[idx_buf, idx_buf, row_buf, row_buf,
                       pltpu.SemaphoreType.DMA, pltpu.SemaphoreType.DMA, pltpu.SemaphoreType.DMA],
        name="dispatch",
    )(hp, pos_win)


def _collect_call(pos_win, y_sorted):
    n = pos_win.shape[0] * SC_WINDOW
    n_workers = SC_CORES * SC_SUBCORES
    wins_per_worker = n // SC_WINDOW // n_workers

    def body(ys_hbm, pos_hbm, out_hbm, idx_v, rows_a, rows_b, gather_sem, write_sem):
        first_win = _sc_worker() * wins_per_worker
        bufs = (rows_a, rows_b)

        @pl.loop(0, wins_per_worker)
        def _(w):
            win = first_win + w
            pltpu.sync_copy(pos_hbm.at[win], idx_v)

            def gather(k):
                return pltpu.async_copy(ys_hbm.at[idx_v.at[k]], bufs[k % 2], gather_sem)

            pending_gather = gather(0)
            pending_write = None
            for k in range(TOP_K):
                pending_gather.wait()
                if pending_write is not None:
                    for cp in pending_write:
                        cp.wait()
                if k + 1 < TOP_K:
                    pending_gather = gather(k + 1)
                pending_write = [
                    pltpu.async_copy(bufs[k % 2].at[:, j],
                                     out_hbm.at[j, k, pl.ds(win * SC_WINDOW, SC_WINDOW)], write_sem)
                    for j in range(PACK_ROWS)]
            for cp in pending_write:
                cp.wait()

    row_buf = pltpu.VMEM((SC_WINDOW,) + y_sorted.shape[1:], y_sorted.dtype)
    return pl.kernel(
        body,
        out_type=jax.ShapeDtypeStruct((PACK_ROWS, TOP_K, n, LANES), y_sorted.dtype),
        mesh=_sc_mesh(),
        scratch_types=[pltpu.VMEM((TOP_K, SC_WINDOW), I32), row_buf, row_buf,
                       pltpu.SemaphoreType.DMA, pltpu.SemaphoreType.DMA],
        name="collect",
    )(y_sorted, pos_win)


def _experts_kernel(pe_ref, pb_ref, pv_ref, st_ref, en_ref, nx_ref,
                    xs_hbm, wg_hbm, wu_hbm, wd_hbm, y_hbm,
                    wgb_ref, wub_ref, wdb_ref, wgf_ref, wuf_ref, wdf_ref, xbuf_ref, stage_ref,
                    in_sems, out_sems, w_sems, state_ref):
    p = pl.program_id(0)
    n_pairs = pl.num_programs(0)
    e = pe_ref[p]
    blk = pb_ref[p]
    rb = xbuf_ref.shape[2]
    prev = jnp.maximum(p - 1, 0)
    nxt = jnp.minimum(p + 1, n_pairs - 1)
    live = pv_ref[p] == 1
    first = jnp.logical_or(p == 0, pb_ref[prev] != blk)
    block_ends = jnp.logical_or(pb_ref[nxt] != blk, pv_ref[nxt] == 0)
    last = jnp.logical_or(p == n_pairs - 1, block_ends)

    def in_copy(j, block, slot):
        return pltpu.make_async_copy(xs_hbm.at[pl.ds(block * rb, rb), j], xbuf_ref.at[slot, j],
                                     in_sems.at[slot])

    def out_copy(j, block, slot):
        return pltpu.make_async_copy(stage_ref.at[slot, j], y_hbm.at[pl.ds(block * rb, rb), j],
                                     out_sems.at[slot])

    def drain(slot):
        @pl.when(state_ref[1 + slot] == 1)
        def _():
            for j in range(PACK_ROWS):
                out_copy(j, 0, slot).wait()
            state_ref[1 + slot] = 0

    @pl.when(p == 0)
    def _():
        state_ref[0] = 0
        state_ref[1] = 0
        state_ref[2] = 0
        stage_ref[...] = jnp.zeros_like(stage_ref)
        for j in range(PACK_ROWS):
            in_copy(j, blk, 0).start()

    slot = jnp.where(jnp.logical_and(first, p > 0), 1 - state_ref[0], state_ref[0])
    state_ref[0] = slot

    @pl.when(jnp.logical_and(first, blk + 1 < y_hbm.shape[0] // rb))
    def _():
        for j in range(PACK_ROWS):
            in_copy(j, blk + 1, 1 - slot).start()

    @pl.when(first)
    def _():
        for j in range(PACK_ROWS):
            in_copy(j, blk, slot).wait()
        drain(slot)

    def w_copies(expert, wslot):
        return (pltpu.make_async_copy(wg_hbm.at[expert], wgf_ref.at[wslot], w_sems.at[wslot]),
                pltpu.make_async_copy(wu_hbm.at[expert], wuf_ref.at[wslot], w_sems.at[wslot]),
                pltpu.make_async_copy(wd_hbm.at[expert], wdf_ref.at[wslot], w_sems.at[wslot]))

    @pl.when(p == 0)
    def _():
        state_ref[3] = 0
        for cp in w_copies(e, 0):
            cp.start()

    new_expert = jnp.logical_or(p == 0, pe_ref[prev] != e)
    wslot = jnp.where(jnp.logical_and(new_expert, p > 0), 1 - state_ref[3], state_ref[3])
    state_ref[3] = wslot

    @pl.when(new_expert)
    def _():
        for cp in w_copies(e, wslot):
            cp.wait()
        wgb_ref[...] = wgf_ref[wslot].astype(BF16)
        wub_ref[...] = wuf_ref[wslot].astype(BF16)
        wdb_ref[...] = wdf_ref[wslot].astype(BF16)

        @pl.when(nx_ref[e] != e)
        def _():
            for cp in w_copies(nx_ref[e], 1 - wslot):
                cp.start()

    lo_row = st_ref[e]
    hi_row = en_ref[e]

    def sub_block(s, row0, shared):
        sub = pl.ds(s * SUB_BLOCK, SUB_BLOCK)
        pieces = []
        for j in range(PACK_ROWS):
            w = xbuf_ref[slot, j, sub, :]
            pieces.append(lax.bitcast_convert_type(w.astype(jnp.int16), BF16))
            pieces.append(lax.bitcast_convert_type(
                lax.shift_right_logical(w, 16).astype(jnp.int16), BF16))
        xb = jnp.concatenate(pieces, axis=1)
        gate = jnp.dot(xb, wgb_ref[...], preferred_element_type=F32)
        up = jnp.dot(xb, wub_ref[...], preferred_element_type=F32)
        yb = jnp.dot((_silu(gate) * up).astype(BF16), wdb_ref[...], preferred_element_type=F32)
        if shared:
            rows = row0 + lax.broadcasted_iota(I32, (SUB_BLOCK, 1), 0)
            mine = jnp.logical_and(rows >= lo_row, rows < hi_row)
        for j in range(PACK_ROWS):
            word = pltpu.pack_elementwise(
                [yb[:, 2 * j * LANES:(2 * j + 1) * LANES], yb[:, (2 * j + 1) * LANES:(2 * j + 2) * LANES]],
                packed_dtype=BF16)
            if shared:
                word = jnp.where(mine, word, stage_ref[slot, j, sub, :])
            stage_ref[slot, j, sub, :] = word

    def one(s):
        row0 = blk * rb + s * SUB_BLOCK
        touched = jnp.logical_and(live, jnp.logical_and(row0 < hi_row, row0 + SUB_BLOCK > lo_row))
        whole = jnp.logical_and(lo_row <= row0, hi_row >= row0 + SUB_BLOCK)

        @pl.when(jnp.logical_and(touched, whole))
        def _():
            sub_block(s, row0, shared=False)

        @pl.when(jnp.logical_and(touched, jnp.logical_not(whole)))
        def _():
            sub_block(s, row0, shared=True)

    for s in range(0, rb // SUB_BLOCK, 2):
        row0 = blk * rb + s * SUB_BLOCK
        both = jnp.logical_and(live, jnp.logical_and(lo_row <= row0, hi_row >= row0 + 2 * SUB_BLOCK))

        @pl.when(both)
        def _():
            sub_block(s, row0, shared=False)
            sub_block(s + 1, row0 + SUB_BLOCK, shared=False)

        @pl.when(jnp.logical_not(both))
        def _():
            one(s)
            one(s + 1)

    @pl.when(jnp.logical_and(live, last))
    def _():
        for j in range(PACK_ROWS):
            out_copy(j, blk, slot).start()
        state_ref[1 + slot] = 1

    @pl.when(p == n_pairs - 1)
    def _():
        drain(0)
        drain(1)


def _experts_call(pair_e, pair_blk, pair_ok, starts, ends, next_e, xs, w_gate, w_up, w_down):
    ne, d, de = w_gate.shape
    rb = ROW_BLOCK
    n_pairs = pair_e.shape[0]
    anywhere = pl.BlockSpec(memory_space=pl.ANY)
    grid_spec = pltpu.PrefetchScalarGridSpec(
        num_scalar_prefetch=6,
        grid=(n_pairs,),
        in_specs=[anywhere, anywhere, anywhere, anywhere],
        out_specs=anywhere,
        scratch_shapes=[pltpu.VMEM((d, de), BF16), pltpu.VMEM((d, de), BF16),
                        pltpu.VMEM((de, d), BF16),
                        pltpu.VMEM((2, d, de), F32), pltpu.VMEM((2, d, de), F32),
                        pltpu.VMEM((2, de, d), F32),
                        pltpu.VMEM((2, PACK_ROWS, rb, LANES), I32),
                        pltpu.VMEM((2, PACK_ROWS, rb, LANES), I32),
                        pltpu.SemaphoreType.DMA((2,)), pltpu.SemaphoreType.DMA((2,)),
                        pltpu.SemaphoreType.DMA((2,)),
                        pltpu.SMEM((4,), I32)],
    )
    return pl.pallas_call(
        _experts_kernel,
        out_shape=jax.ShapeDtypeStruct(xs.shape, xs.dtype),
        grid_spec=grid_spec,
        compiler_params=pltpu.CompilerParams(
            dimension_semantics=("arbitrary",), vmem_limit_bytes=VMEM_LIMIT),
        name="experts",
    )(pair_e, pair_blk, pair_ok, starts, ends, next_e, xs, w_gate, w_up, w_down)


def _combine_kernel(y0_ref, y1_ref, y2_ref, y3_ref, x1_ref, hp_ref, w_ref, g2_ref, fg_ref,
                    wsg_ref, wsu_ref, wsd_ref, o_ref):
    tm = x1_ref.shape[0]
    w = w_ref[...]
    parts = []
    for y_ref in (y0_ref, y1_ref, y2_ref, y3_ref):
        for half in range(2):
            acc = None
            for k in range(TOP_K):
                piece = pltpu.unpack_elementwise(y_ref[k], index=half, packed_dtype=BF16,
                                                 unpacked_dtype=F32) * w[:, k:k + 1]
                acc = piece if acc is None else acc + piece
            parts.append(acc)
    routed = jnp.concatenate(parts, axis=1)
    pieces = []
    for j in range(PACK_ROWS):
        word = hp_ref[pl.ds(j, tm, stride=PACK_ROWS), :]
        pieces.append(lax.bitcast_convert_type(word.astype(jnp.int16), BF16))
        pieces.append(lax.bitcast_convert_type(lax.shift_right_logical(word, 16).astype(jnp.int16), BF16))
    h2 = jnp.concatenate(pieces, axis=1)
    hid = _silu(jnp.dot(h2, wsg_ref[...], preferred_element_type=F32)) * jnp.dot(
        h2, wsu_ref[...], preferred_element_type=F32)
    shared = jnp.dot(hid.astype(BF16), wsd_ref[...], preferred_element_type=F32)
    x2 = x1_ref[...] + g2_ref[0] * (routed + shared)
    ms = jnp.mean(x2 * x2, axis=-1, keepdims=True)
    o_ref[...] = x2 * lax.rsqrt(ms + EPS) * fg_ref[...]


def _combine_kernel_into(*refs):
    _combine_kernel(*refs[:-2], refs[-1])


def _combine_call(y_tok, x1, hp, w_tok, g2, fgain, wsg, wsu, wsd, seq, part, prev_out):
    n, d = x1.shape
    dsh = wsg.shape[1]
    tm = MOVE_TILE
    per_b = seq // tm
    tiles = y_tok.shape[2] // tm
    t0 = part * tiles
    const = lambda i: (0, 0)

    def piece_spec(j):
        return pl.BlockSpec((None, TOP_K, tm, LANES), lambda i: (j, 0, i, 0))

    in_specs = [piece_spec(j) for j in range(PACK_ROWS)] + [
        pl.BlockSpec((tm, d), lambda i: (t0 + i, 0)),
        pl.BlockSpec((tm * PACK_ROWS, LANES), lambda i: (t0 + i, 0)),
        pl.BlockSpec((tm, TOP_K), lambda i: (t0 + i, 0)),
        pl.BlockSpec((1, 1, d), lambda i: ((t0 + i) // per_b, 0, 0)),
        pl.BlockSpec((1, d), const),
        pl.BlockSpec((d, dsh), const),
        pl.BlockSpec((d, dsh), const),
        pl.BlockSpec((dsh, d), const)]
    args = [y_tok, y_tok, y_tok, y_tok, x1, hp, w_tok, g2, fgain, wsg, wsu, wsd]
    aliases = {}
    body = _combine_kernel
    if prev_out is not None:
        in_specs.append(pl.BlockSpec(memory_space=pl.ANY))
        args.append(prev_out)
        aliases = {len(args) - 1: 0}
        body = _combine_kernel_into
    return pl.pallas_call(
        body,
        out_shape=jax.ShapeDtypeStruct((n, d), F32),
        grid=(tiles,),
        in_specs=in_specs,
        out_specs=pl.BlockSpec((tm, d), lambda i: (t0 + i, 0)),
        input_output_aliases=aliases,
        compiler_params=pltpu.CompilerParams(
            dimension_semantics=("arbitrary",), vmem_limit_bytes=VMEM_LIMIT),
        name="combine",
    )(*args)


def _pair_tables(counts, n_rows):
    ne = counts.shape[0]
    sizes = counts.astype(I32)
    ends = jnp.cumsum(sizes)
    starts = ends - sizes
    first_blk = starts // ROW_BLOCK
    last_blk = (ends - 1) // ROW_BLOCK
    n_pairs = jnp.where(sizes > 0, last_blk - first_blk + 1, 0)
    pair_end = jnp.cumsum(n_pairs)
    pair_start = pair_end - n_pairs
    max_pairs = n_rows // ROW_BLOCK + ne
    p = jnp.arange(max_pairs, dtype=I32)
    ok = p < pair_end[-1]
    pc = jnp.minimum(p, pair_end[-1] - 1)
    pair_e = jnp.minimum(jnp.searchsorted(pair_end, pc, side='right'), ne - 1).astype(I32)
    pair_blk = (first_blk[pair_e] + pc - pair_start[pair_e]).astype(I32)
    eid = jnp.arange(ne, dtype=I32)
    later = lax.cummin(jnp.where(sizes > 0, eid, ne), reverse=True)
    next_e = jnp.concatenate([later[1:], jnp.full((1,), ne, I32)])
    next_e = jnp.where(next_e < ne, next_e, eid)
    return pair_e, pair_blk, ok.astype(I32), starts.astype(I32), ends.astype(I32), next_e


def kernel(x, c, w_ada, b_ada, w_in, lb_logits, hgrn_norm, lam_re, lam_im, log_dt, b_re, b_im,
           c_re, c_im, d_skip, w_glu, b_glu, w_out, w_router, router_bias, w_gate, w_up, w_down,
           ws_gate, ws_up, ws_down, final_gain):
    bsz, seq, d = x.shape
    n = bsz * seq
    fdim = HGRN_HEADS * HGRN_KDIM
    n_chunks = seq // CHUNK
    lb = jnp.cumsum(jax.nn.softmax(lb_logits.astype(F32), axis=0), axis=0)[0].reshape(1, fdim)

    mod = _mod_call(c, w_ada[0], b_ada[0]).reshape(bsz, 6, d)
    nb = w_in.shape[2] - 4 * fdim
    perm = jnp.concatenate([jnp.arange(0, nb, 2), jnp.arange(1, nb, 2)])
    w_in_b = w_in[0].astype(BF16)
    ut, out_a, ucm = _mix_front_call(x, mod, w_in_b[:, :4 * fdim], w_in_b[:, 4 * fdim:][:, perm].T, lb,
                                hgrn_norm[0].reshape(1, fdim))

    k_tab, p_tab, q_tab, a1, a2 = _s5_tables(lam_re[0], lam_im[0], log_dt[0], b_re[0], b_im[0],
                                             c_re[0], c_im[0])
    yt = _s5_call(ut, k_tab, p_tab, q_tab, a1, a2, n_chunks, bsz)

    wr_t = w_router[0].T
    wr_hi = wr_t.astype(BF16)
    wr_lo = (wr_t - wr_hi.astype(F32)).astype(BF16)
    w_out_b = w_out[0].astype(BF16)
    w_out_p = jnp.concatenate([w_out_b[:fdim], w_out_b[fdim:][perm]], axis=0)
    x1, hp, logits_t = _mix_back_call(
        x.reshape(n, d), out_a.reshape(n, fdim), yt, ucm, mod,
        d_skip[0][perm].reshape(nb, 1), w_glu[0][perm][:, perm].T.astype(BF16),
        b_glu[0][perm].reshape(nb, 1), w_out_p, wr_hi, wr_lo, seq)

    top_idx, top_w, rank, counts = _route_call(logits_t, router_bias[0])
    pair_e, pair_blk, pair_ok, starts, ends, next_e = _pair_tables(counts[:, 0], n * TOP_K)
    pos = _pos_call(top_idx, rank, starts)

    pos_win = pos.reshape(TOP_K, n // SC_WINDOW, SC_WINDOW).transpose(1, 0, 2)
    xs = _dispatch_call(pos_win, hp.reshape(n, PACK_ROWS, LANES))
    y_sorted = _experts_call(pair_e, pair_blk, pair_ok, starts, ends, next_e, xs,
                             w_gate[0], w_up[0], w_down[0])
    w_tok = top_w.T
    g2 = mod[:, 5:6, :]
    fgain = final_gain.reshape(1, d)
    wsg, wsu, wsd = ws_gate[0].astype(BF16), ws_up[0].astype(BF16), ws_down[0].astype(BF16)
    wins = pos_win.shape[0] // TAIL_PARTS
    out = None
    for part in range(TAIL_PARTS):
        y_tok = _collect_call(pos_win[part * wins:(part + 1) * wins], y_sorted)
        out = _combine_call(y_tok, x1, hp, w_tok, g2, fgain, wsg, wsu, wsd, seq, part, out)
    return out.reshape(bsz, seq, d)
```

```python
import functools

import jax
import jax.numpy as jnp
from jax import lax
from jax.experimental import pallas as pl
from jax.experimental.pallas import tpu as pltpu
from jax.experimental.pallas import tpu_sc as plsc

F32 = jnp.float32
BF16 = jnp.bfloat16
I32 = jnp.int32

EPS = 1e-6
CHUNK = 64
HGRN_HEADS = 4
HGRN_KDIM = 128
S5_GROUP = 16
S5_STATE = 64
N_EXPERT_GROUPS = 8
TOPK_GROUPS = 4
TOP_K = 8
ROUTE_SCALE = 2.5
LANES = 128
PACK_ROWS = 4
SC_CORES = 2
SC_SUBCORES = 16
SC_WINDOW = 64

SEQ_TILE = 512
TOK_TILE = 512
ROUTE_TILE = 512
POS_TILE = 1024
MOVE_TILE = 512
ROW_BLOCK = 2048
SUB_BLOCK = 512
TAIL_PARTS = 8
VMEM_LIMIT = 56 * 1024 * 1024

_NT = (((1,), (1,)), ((), ()))
_TN = (((0,), (0,)), ((), ()))


def _sigmoid(v):
    return 0.5 * jnp.tanh(0.5 * v) + 0.5


def _silu(v):
    return v * _sigmoid(v)


def _bdot(a, b):
    return jnp.dot(a.astype(BF16), b.astype(BF16), preferred_element_type=F32)


def _store_packed(ref, val, n_rows):
    for j in range(PACK_ROWS):
        lo = val[:, 2 * j * LANES:(2 * j + 1) * LANES]
        hi = val[:, (2 * j + 1) * LANES:(2 * j + 2) * LANES]
        ref[pl.ds(j, n_rows, stride=PACK_ROWS), :] = pltpu.pack_elementwise([lo, hi], packed_dtype=BF16)


def _mod_kernel(c_ref, w_ref, b_ref, o_ref):
    o_ref[...] = _bdot(_silu(c_ref[...]), w_ref[...]) + b_ref[...]


def _mod_call(c, w_ada, b_ada):
    bsz, d = c.shape
    n_out = w_ada.shape[1]
    return pl.pallas_call(
        _mod_kernel,
        out_shape=jax.ShapeDtypeStruct((bsz, n_out), F32),
        grid=(n_out // d,),
        in_specs=[pl.BlockSpec((bsz, d), lambda j: (0, 0)),
                  pl.BlockSpec((d, d), lambda j: (0, j)),
                  pl.BlockSpec((1, d), lambda j: (0, j))],
        out_specs=pl.BlockSpec((bsz, d), lambda j: (0, j)),
        compiler_params=pltpu.CompilerParams(vmem_limit_bytes=VMEM_LIMIT),
        name="mod",
    )(c, w_ada, b_ada.reshape(1, n_out))


def _split_chunk_pairs(tile_even, tile_odd):
    low = lax.broadcasted_iota(I32, tile_even.shape, 1) < CHUNK
    first = jnp.where(low, tile_even, pltpu.roll(tile_odd, CHUNK, 1))
    second = jnp.where(low, pltpu.roll(tile_even, CHUNK, 1), tile_odd)
    return first, second


def _mix_front_kernel(x_ref, mod_ref, win_ref, wut_ref, lb_ref, gn_ref, ltri_ref,
                      ut_ref, oa_ref, ucm_ref, proj_ref, st_ref, flat_ref):
    fdim = HGRN_HEADS * HGRN_KDIM
    ts = x_ref.shape[1]
    pairs = wut_ref.shape[0] // 2
    rows = ts // CHUNK

    @pl.when(pl.program_id(1) == 0)
    def _():
        st_ref[...] = jnp.zeros_like(st_ref)

    x = x_ref[0]
    ms = jnp.mean(x * x, axis=-1, keepdims=True)
    h = x * lax.rsqrt(ms + EPS) * (1.0 + mod_ref[0, 1:2, :]) + mod_ref[0, 0:1, :]
    hb = h.astype(BF16)
    proj_ref[...] = jnp.dot(hb, win_ref[...], preferred_element_type=F32)
    u_t = lax.dot_general(wut_ref[...], hb, _NT, preferred_element_type=F32)
    ucm_ref[...] = u_t.astype(BF16)
    for m in range(ts // LANES):
        first, second = _split_chunk_pairs(u_t[:pairs, m * LANES:(m + 1) * LANES],
                                           u_t[pairs:, m * LANES:(m + 1) * LANES])
        flat_ref[2 * m * pairs:(2 * m + 1) * pairs, :] = first
        flat_ref[(2 * m + 1) * pairs:(2 * m + 2) * pairs, :] = second
    per_group = S5_GROUP // 2
    for q in range(pairs):
        ut_ref[q // per_group, :, (q % per_group) * LANES:(q % per_group + 1) * LANES] = (
            flat_ref[pl.ds(q, rows, stride=pairs), :])

    lb = lb_ref[...]
    gn = gn_ref[...]
    ltri = ltri_ref[...]
    row = lax.broadcasted_iota(I32, (CHUNK, CHUNK), 0)
    col = lax.broadcasted_iota(I32, (CHUNK, CHUNK), 1)
    causal = row >= col

    def chunk_step(ci, carry):
        r0 = pl.multiple_of(ci * CHUNK, CHUNK)
        q = proj_ref[pl.ds(r0, CHUNK), 0:fdim]
        fl = proj_ref[pl.ds(r0, CHUNK), fdim:2 * fdim]
        iv = proj_ref[pl.ds(r0, CHUNK), 2 * fdim:3 * fdim]
        og = proj_ref[pl.ds(r0, CHUNK), 3 * fdim:4 * fdim]
        f = lb + (1.0 - lb) * _sigmoid(fl)
        lf = jnp.log(f)
        lf_hi = lf.astype(BF16)
        rem = lf - lf_hi.astype(F32)
        lf_mid = rem.astype(BF16)
        lf_lo = (rem - lf_mid.astype(F32)).astype(BF16)
        b = (jnp.dot(ltri, lf_hi, preferred_element_type=F32)
             + jnp.dot(ltri, lf_mid, preferred_element_type=F32)
             + jnp.dot(ltri, lf_lo, preferred_element_type=F32))
        b_ref = b[CHUNK // 2 - 1:CHUNK // 2, :]
        b_last = b[CHUNK - 1:CHUNK, :]
        qs = _silu(q)
        kk = 1.0 - f
        qe = (qs * jnp.exp(b - b_ref)).astype(BF16)
        ke = (kk * jnp.exp(b_ref - b)).astype(BF16)
        qb = (qs * jnp.exp(b)).astype(BF16)
        k2 = (kk * jnp.exp(b_last - b)).astype(BF16)
        dec = jnp.exp(b_last)
        ivb = iv.astype(BF16)
        outs = []
        for hh in range(HGRN_HEADS):
            sl = slice(hh * HGRN_KDIM, (hh + 1) * HGRN_KDIM)
            att = lax.dot_general(qe[:, sl], ke[:, sl], _NT, preferred_element_type=F32)
            att = jnp.where(causal, att, 0.0)
            st = st_ref[hh]
            o = jnp.dot(att.astype(BF16), ivb[:, sl], preferred_element_type=F32)
            o = o + lax.dot_general(qb[:, sl], st.astype(BF16), _NT, preferred_element_type=F32)
            st_ref[hh] = st * dec[:, sl] + lax.dot_general(
                ivb[:, sl], k2[:, sl], _TN, preferred_element_type=F32)
            outs.append(o * lax.rsqrt(jnp.mean(o * o, axis=-1, keepdims=True) + EPS))
        o = jnp.concatenate(outs, axis=1) * gn * _silu(og)
        oa_ref[0, pl.ds(r0, CHUNK), :] = o.astype(BF16)
        return carry

    lax.fori_loop(0, ts // CHUNK, chunk_step, 0, unroll=True)


def _mix_front_call(x, mod, w_main, w_ut, lb, gn):
    bsz, seq, d = x.shape
    fdim = HGRN_HEADS * HGRN_KDIM
    ncols = w_main.shape[1]
    nb = w_ut.shape[0]
    groups = nb // S5_GROUP
    ltri = jnp.tril(jnp.ones((CHUNK, CHUNK), BF16))
    ts = SEQ_TILE
    tiles = seq // ts
    rows = ts // CHUNK
    return pl.pallas_call(
        _mix_front_kernel,
        out_shape=(jax.ShapeDtypeStruct((groups, bsz * seq // CHUNK, S5_GROUP * CHUNK), F32),
                   jax.ShapeDtypeStruct((bsz, seq, fdim), BF16),
                   jax.ShapeDtypeStruct((nb, bsz * seq), BF16)),
        grid=(bsz, tiles),
        in_specs=[pl.BlockSpec((1, ts, d), lambda b, j: (b, j, 0)),
                  pl.BlockSpec((1, 6, d), lambda b, j: (b, 0, 0)),
                  pl.BlockSpec((d, ncols), lambda b, j: (0, 0)),
                  pl.BlockSpec((nb, d), lambda b, j: (0, 0)),
                  pl.BlockSpec((1, fdim), lambda b, j: (0, 0)),
                  pl.BlockSpec((1, fdim), lambda b, j: (0, 0)),
                  pl.BlockSpec((CHUNK, CHUNK), lambda b, j: (0, 0))],
        out_specs=(pl.BlockSpec((groups, rows, S5_GROUP * CHUNK), lambda b, j: (0, b * tiles + j, 0)),
                   pl.BlockSpec((1, ts, fdim), lambda b, j: (b, j, 0)),
                   pl.BlockSpec((nb, ts), lambda b, j: (0, b * tiles + j))),
        scratch_shapes=[pltpu.VMEM((ts, ncols), F32),
                        pltpu.VMEM((HGRN_HEADS, fdim // HGRN_HEADS, HGRN_KDIM), F32),
                        pltpu.VMEM((rows * nb // 2, LANES), F32)],
        compiler_params=pltpu.CompilerParams(
            dimension_semantics=("arbitrary", "arbitrary"), vmem_limit_bytes=VMEM_LIMIT),
        name="mix_front",
    )(x, mod, w_main, w_ut, lb, gn, ltri)


def _s5_tables(lam_re, lam_im, log_dt, b_re, b_im, c_re, c_im):
    t = CHUNK
    hp = lax.Precision.HIGHEST
    lam = lax.complex(jnp.minimum(lam_re, -1e-4), lam_im)
    lam_dt = lam * jnp.exp(log_dt)[:, None]
    lam_bar = jnp.exp(lam_dt)
    b_bar = ((lam_bar - 1.0) / lam)[..., None] * lax.complex(b_re, b_im)
    c_mat = lax.complex(c_re, c_im)
    taus = jnp.arange(t + 1, dtype=F32)
    lam_pow = jnp.exp(lam_dt[:, None, :] * taus[None, :, None])
    g, p = lam.shape
    c = b_re.shape[-1]
    cl = c_mat[:, None, :, :] * lam_pow[:, :t, None, :]
    kr = (jnp.einsum('gtcp,gpi->gtci', cl.real, b_bar.real, precision=hp)
          - jnp.einsum('gtcp,gpi->gtci', cl.imag, b_bar.imag, precision=hp))
    k_tab = kr.transpose(0, 3, 2, 1).reshape(g, c, c // 2, 2 * t)
    pc = lam_pow[:, t - 1::-1][:, :t, :, None] * b_bar[:, None, :, :]
    pc = pc.transpose(0, 3, 1, 2).reshape(g, c * t, p)
    p_tab = jnp.concatenate([pc.real, pc.imag], axis=-1)
    ql = c_mat[:, None, :, :] * lam_pow[:, 1:t + 1, None, :]
    ql = ql.transpose(0, 3, 2, 1).reshape(g, p, c * t)
    q_tab = jnp.concatenate([ql.real, -ql.imag], axis=1)
    lam_t = lam_pow[:, t]
    a1 = jnp.concatenate([lam_t.real, lam_t.real], axis=-1)[:, None, :]
    a2 = jnp.concatenate([-lam_t.imag, lam_t.imag], axis=-1)[:, None, :]
    return k_tab, p_tab.astype(BF16), q_tab.astype(BF16), a1, a2


def _s5_kernel(u_ref, k_ref, p_ref, q_ref, a1_ref, a2_ref, y_ref, v_ref, xs_ref, m_ref,
               *, n_chunks, n_batch):
    n_in, n_pairs, _ = k_ref.shape[1:]
    lane = lax.broadcasted_iota(I32, (CHUNK, LANES), 1)
    causal = (lane & (CHUNK - 1)) >= lax.broadcasted_iota(I32, (CHUNK, LANES), 0)
    for ci in range(n_in):
        for a in range(n_pairs):
            lags = jnp.broadcast_to(k_ref[0, ci, a:a + 1, :], (CHUNK, LANES))
            tile = pltpu.roll(lags, 0, 1, stride=1, stride_axis=0)
            m_ref[ci * CHUNK:(ci + 1) * CHUNK, a * LANES:(a + 1) * LANES] = jnp.where(
                causal, tile, 0.0).astype(BF16)
    u = u_ref[0].astype(BF16)
    v_ref[...] = jnp.dot(u, p_ref[0], preferred_element_type=F32)
    a1 = a1_ref[0]
    a2 = a2_ref[0]
    half = xs_ref.shape[1] // 2

    def step(n, state):
        xs_ref[pl.ds(n, n_batch, stride=n_chunks), :] = state
        return (a1 * state + a2 * pltpu.roll(state, half, 1)
                + v_ref[pl.ds(n, n_batch, stride=n_chunks), :])

    lax.fori_loop(0, n_chunks, step, jnp.zeros((n_batch, xs_ref.shape[1]), F32))
    y = jnp.dot(u, m_ref[...], preferred_element_type=F32)
    y_ref[0] = y + jnp.dot(xs_ref[...].astype(BF16), q_ref[0], preferred_element_type=F32)


def _s5_call(ut, k_tab, p_tab, q_tab, a1, a2, n_chunks, n_batch):
    g, rows, width = ut.shape
    p2 = p_tab.shape[-1]
    return pl.pallas_call(
        functools.partial(_s5_kernel, n_chunks=n_chunks, n_batch=n_batch),
        out_shape=jax.ShapeDtypeStruct((g, rows, width), F32),
        grid=(g,),
        in_specs=[pl.BlockSpec((1, rows, width), lambda i: (i, 0, 0)),
                  pl.BlockSpec((1,) + k_tab.shape[1:], lambda i: (i, 0, 0, 0)),
                  pl.BlockSpec((1, width, p2), lambda i: (i, 0, 0)),
                  pl.BlockSpec((1, p2, width), lambda i: (i, 0, 0)),
                  pl.BlockSpec((1, 1, p2), lambda i: (i, 0, 0)),
                  pl.BlockSpec((1, 1, p2), lambda i: (i, 0, 0))],
        out_specs=pl.BlockSpec((1, rows, width), lambda i: (i, 0, 0)),
        scratch_shapes=[pltpu.VMEM((rows, p2), F32), pltpu.VMEM((rows, p2), F32),
                        pltpu.VMEM((width, width), BF16)],
        compiler_params=pltpu.CompilerParams(
            dimension_semantics=("arbitrary",), vmem_limit_bytes=VMEM_LIMIT),
        name="s5",
    )(ut, k_tab, p_tab, q_tab, a1, a2)


def _token_major(flat_ref, src_ref):
    groups, rows, _ = src_ref.shape
    per_group = S5_GROUP // 2
    pairs = groups * per_group
    for q in range(pairs):
        flat_ref[pl.ds(q, rows, stride=pairs), :] = (
            src_ref[q // per_group, :, (q % per_group) * LANES:(q % per_group + 1) * LANES])
    tiles = []
    for m in range(rows // 2):
        even, odd = _split_chunk_pairs(flat_ref[2 * m * pairs:(2 * m + 1) * pairs, :],
                                       flat_ref[(2 * m + 1) * pairs:(2 * m + 2) * pairs, :])
        tiles.append(jnp.concatenate([even, odd], axis=0))
    return jnp.concatenate(tiles, axis=1)


def _mix_back_kernel(x_ref, oa_ref, yt_ref, ucm_ref, mod_ref, dskip_ref, wglu_ref, bglu_ref,
                     wout_ref, wrh_ref, wrl_ref,
                     x1_ref, hp_ref, lt_ref, flat_ref):
    na = oa_ref.shape[1]
    y_t = _token_major(flat_ref, yt_ref)
    u_t = ucm_ref[...].astype(F32)
    z_t = jax.nn.gelu(y_t + dskip_ref[...] * u_t)
    gate_t = _sigmoid(jnp.dot(wglu_ref[...], z_t.astype(BF16), preferred_element_type=F32)
                      + bglu_ref[...])
    ob_t = (z_t * gate_t).astype(BF16)
    mixed = (jnp.dot(oa_ref[...], wout_ref[0:na, :], preferred_element_type=F32)
             + lax.dot_general(ob_t, wout_ref[na:, :], _TN, preferred_element_type=F32))
    x1 = x_ref[...] + mod_ref[0, 2:3, :] * mixed
    ms = jnp.mean(x1 * x1, axis=-1, keepdims=True)
    h2 = x1 * lax.rsqrt(ms + EPS) * (1.0 + mod_ref[0, 4:5, :]) + mod_ref[0, 3:4, :]
    _store_packed(hp_ref, h2, h2.shape[0])
    h_hi = h2.astype(BF16)
    h_lo = (h2 - h_hi.astype(F32)).astype(BF16)
    lt = lax.dot_general(wrh_ref[...], h_hi, _NT, preferred_element_type=F32)
    lt = lt + lax.dot_general(wrl_ref[...], h_hi, _NT, preferred_element_type=F32)
    lt = lt + lax.dot_general(wrh_ref[...], h_lo, _NT, preferred_element_type=F32)
    lt_ref[...] = lt
    x1_ref[...] = x1


def _mix_back_call(x2d, oa, yt, ucm, mod, dskip, wglu, bglu, wout, wrh, wrl, seq):
    n, d = x2d.shape
    nb = oa.shape[1]
    ne = wrh.shape[0]
    tm = TOK_TILE
    per_b = seq // tm
    groups, _, width = yt.shape
    rows = tm // CHUNK
    const = lambda i: (0, 0)
    flat_block = pl.BlockSpec((groups, rows, width), lambda i: (0, i, 0))
    return pl.pallas_call(
        _mix_back_kernel,
        out_shape=(jax.ShapeDtypeStruct((n, d), F32),
                   jax.ShapeDtypeStruct((n * PACK_ROWS, LANES), I32),
                   jax.ShapeDtypeStruct((ne, n), F32)),
        grid=(n // tm,),
        in_specs=[pl.BlockSpec((tm, d), lambda i: (i, 0)),
                  pl.BlockSpec((tm, nb), lambda i: (i, 0)),
                  flat_block,
                  pl.BlockSpec((nb, tm), lambda i: (0, i)),
                  pl.BlockSpec((1, 6, d), lambda i: (i // per_b, 0, 0)),
                  pl.BlockSpec((nb, 1), const),
                  pl.BlockSpec((nb, nb), const),
                  pl.BlockSpec((nb, 1), const),
                  pl.BlockSpec((d, d), const),
                  pl.BlockSpec((ne, d), const),
                  pl.BlockSpec((ne, d), const)],
        out_specs=(pl.BlockSpec((tm, d), lambda i: (i, 0)),
                   pl.BlockSpec((tm * PACK_ROWS, LANES), lambda i: (i, 0)),
                   pl.BlockSpec((ne, tm), lambda i: (0, i))),
        scratch_shapes=[pltpu.VMEM((rows * nb // 2, LANES), F32)],
        compiler_params=pltpu.CompilerParams(
            dimension_semantics=("arbitrary",), vmem_limit_bytes=VMEM_LIMIT),
        name="mix_back",
    )(x2d, oa, yt, ucm, mod, dskip, wglu, bglu, wout, wrh, wrl)


def _route_kernel(lt_ref, bias_ref, su_ref, idx_ref, w_ref, rank_ref, cnt_ref, run_ref):
    ne, tr = lt_ref.shape
    per_group = ne // N_EXPERT_GROUPS
    neg = -jnp.inf

    @pl.when(pl.program_id(0) == 0)
    def _():
        run_ref[...] = jnp.zeros_like(run_ref)

    s = _sigmoid(lt_ref[...])
    sel = s + bias_ref[...]
    gio = lax.broadcasted_iota(I32, (per_group, tr), 0)
    gscore = []
    for g in range(N_EXPERT_GROUPS):
        v = sel[g * per_group:(g + 1) * per_group, :]
        m1 = jnp.max(v, axis=0, keepdims=True)
        i1 = jnp.min(jnp.where(v == m1, gio, per_group), axis=0, keepdims=True)
        m2 = jnp.max(jnp.where(gio == i1, neg, v), axis=0, keepdims=True)
        gscore.append(m1 + m2)
    masked = []
    for g in range(N_EXPERT_GROUPS):
        ahead = jnp.zeros((1, tr), I32)
        for o in range(N_EXPERT_GROUPS):
            if o == g:
                continue
            wins = (gscore[o] >= gscore[g]) if o < g else (gscore[o] > gscore[g])
            ahead = ahead + wins.astype(I32)
        keep = ahead < TOPK_GROUPS
        masked.append(jnp.where(keep, sel[g * per_group:(g + 1) * per_group, :], neg))
    selm = jnp.concatenate(masked, axis=0)
    eio = lax.broadcasted_iota(I32, (ne, tr), 0)
    candidate = selm > neg
    idxs, ws = [], []
    for k in range(TOP_K):
        m = jnp.max(selm, axis=0, keepdims=True)
        ik = jnp.min(jnp.where(selm == m, eio, ne), axis=0, keepdims=True)
        onehot = eio == ik
        ws.append(jnp.sum(jnp.where(onehot, s, 0.0), axis=0, keepdims=True))
        selm = jnp.where(onehot, neg, selm)
        idxs.append(ik)
    hits = jnp.where(jnp.logical_and(candidate, selm == neg), 1.0, 0.0)
    wsum = ws[0]
    for k in range(1, TOP_K):
        wsum = wsum + ws[k]
    scale = ROUTE_SCALE / wsum
    ranks = jnp.dot(hits.astype(BF16), su_ref[...], preferred_element_type=F32) + run_ref[...]
    for k in range(TOP_K):
        idx_ref[k:k + 1, :] = idxs[k]
        w_ref[k:k + 1, :] = ws[k] * scale
        rk = jnp.sum(jnp.where(eio == idxs[k], ranks, 0.0), axis=0, keepdims=True)
        rank_ref[k:k + 1, :] = rk.astype(I32)
    run_ref[...] = run_ref[...] + jnp.sum(hits, axis=1, keepdims=True)
    cnt_ref[...] = run_ref[...]


def _route_call(lt, bias):
    ne, n = lt.shape
    tr = ROUTE_TILE
    su = jnp.triu(jnp.ones((tr, tr), F32), k=1).astype(BF16)
    return pl.pallas_call(
        _route_kernel,
        out_shape=(jax.ShapeDtypeStruct((TOP_K, n), I32),
                   jax.ShapeDtypeStruct((TOP_K, n), F32),
                   jax.ShapeDtypeStruct((TOP_K, n), I32),
                   jax.ShapeDtypeStruct((ne, 1), F32)),
        grid=(n // tr,),
        in_specs=[pl.BlockSpec((ne, tr), lambda i: (0, i)),
                  pl.BlockSpec((ne, 1), lambda i: (0, 0)),
                  pl.BlockSpec((tr, tr), lambda i: (0, 0))],
        out_specs=(pl.BlockSpec((TOP_K, tr), lambda i: (0, i)),
                   pl.BlockSpec((TOP_K, tr), lambda i: (0, i)),
                   pl.BlockSpec((TOP_K, tr), lambda i: (0, i)),
                   pl.BlockSpec((ne, 1), lambda i: (0, 0))),
        scratch_shapes=[pltpu.VMEM((ne, 1), F32)],
        compiler_params=pltpu.CompilerParams(
            dimension_semantics=("arbitrary",), vmem_limit_bytes=VMEM_LIMIT),
        name="route",
    )(lt, bias.reshape(ne, 1), su)


def _pos_kernel(idx_ref, rank_ref, st_ref, pos_ref):
    ne = st_ref.shape[0]
    tp = idx_ref.shape[1]
    eio = lax.broadcasted_iota(I32, (ne, tp), 0)
    st = st_ref[...]
    for k in range(TOP_K):
        base = jnp.sum(jnp.where(eio == idx_ref[k:k + 1, :], st, 0), axis=0, keepdims=True)
        pos_ref[k:k + 1, :] = base + rank_ref[k:k + 1, :]


def _pos_call(top_idx, rank, starts):
    kk, n = top_idx.shape
    ne = starts.shape[0]
    tp = POS_TILE
    return pl.pallas_call(
        _pos_kernel,
        out_shape=jax.ShapeDtypeStruct((kk, n), I32),
        grid=(n // tp,),
        in_specs=[pl.BlockSpec((kk, tp), lambda i: (0, i)),
                  pl.BlockSpec((kk, tp), lambda i: (0, i)),
                  pl.BlockSpec((ne, 1), lambda i: (0, 0))],
        out_specs=pl.BlockSpec((kk, tp), lambda i: (0, i)),
        compiler_params=pltpu.CompilerParams(
            dimension_semantics=("arbitrary",), vmem_limit_bytes=VMEM_LIMIT),
        name="pos",
    )(top_idx, rank, starts.reshape(ne, 1))


def _sc_mesh():
    return plsc.VectorSubcoreMesh(core_axis_name="c", subcore_axis_name="s",
                                  num_cores=SC_CORES, num_subcores=SC_SUBCORES)


def _sc_worker():
    return lax.axis_index("s") * SC_CORES + lax.axis_index("c")


def _dispatch_call(pos_win, hp):
    n = hp.shape[0]
    n_workers = SC_CORES * SC_SUBCORES
    wins_per_worker = n // SC_WINDOW // n_workers

    def body(hp_hbm, pos_hbm, xs_hbm, idx_a, idx_b, rows_a, rows_b, load_a, load_b, scatter_sem):
        first_win = _sc_worker() * wins_per_worker
        idx_v, rows_v, load_sem = (idx_a, idx_b), (rows_a, rows_b), (load_a, load_b)

        def loads(win, slot):
            return (pltpu.make_async_copy(hp_hbm.at[pl.ds(win * SC_WINDOW, SC_WINDOW)], rows_v[slot],
                                          load_sem[slot]),
                    pltpu.make_async_copy(pos_hbm.at[win], idx_v[slot], load_sem[slot]))

        for cp in loads(first_win, 0):
            cp.start()

        @pl.loop(0, wins_per_worker, step=2)
        def _(w):
            for slot in range(2):
                win = first_win + w + slot
                for cp in loads(win, slot):
                    cp.wait()

                @pl.when(w + slot + 1 < wins_per_worker)
                def _():
                    for cp in loads(win + 1, 1 - slot):
                        cp.start()

                copies = [pltpu.async_copy(rows_v[slot], xs_hbm.at[idx_v[slot].at[k]], scatter_sem)
                          for k in range(TOP_K)]
                for cp in copies:
                    cp.wait()

    idx_buf = pltpu.VMEM((TOP_K, SC_WINDOW), I32)
    row_buf = pltpu.VMEM((SC_WINDOW,) + hp.shape[1:], hp.dtype)
    return pl.kernel(
        body,
        out_type=jax.ShapeDtypeStruct((n * TOP_K,) + hp.shape[1:], hp.dtype),
        mesh=_sc_mesh(),
        scratch_types=[idx_buf, idx_buf, row_buf, row_buf,
                       pltpu.SemaphoreType.DMA, pltpu.SemaphoreType.DMA, pltpu.SemaphoreType.DMA],
        name="dispatch",
    )(hp, pos_win)


def _collect_call(pos_win, y_sorted):
    n = pos_win.shape[0] * SC_WINDOW
    n_workers = SC_CORES * SC_SUBCORES
    wins_per_worker = n // SC_WINDOW // n_workers

    def body(ys_hbm, pos_hbm, out_hbm, idx_v, rows_a, rows_b, gather_sem, write_sem):
        first_win = _sc_worker() * wins_per_worker
        bufs = (rows_a, rows_b)

        @pl.loop(0, wins_per_worker)
        def _(w):
            win = first_win + w
            pltpu.sync_copy(pos_hbm.at[win], idx_v)

            def gather(k):
                return pltpu.async_copy(ys_hbm.at[idx_v.at[k]], bufs[k % 2], gather_sem)

            pending_gather = gather(0)
            pending_write = None
            for k in range(TOP_K):
                pending_gather.wait()
                if pending_write is not None:
                    for cp in pending_write:
                        cp.wait()
                if k + 1 < TOP_K:
                    pending_gather = gather(k + 1)
                pending_write = [
                    pltpu.async_copy(bufs[k % 2].at[:, j],
                                     out_hbm.at[j, k, pl.ds(win * SC_WINDOW, SC_WINDOW)], write_sem)
                    for j in range(PACK_ROWS)]
            for cp in pending_write:
                cp.wait()

    row_buf = pltpu.VMEM((SC_WINDOW,) + y_sorted.shape[1:], y_sorted.dtype)
    return pl.kernel(
        body,
        out_type=jax.ShapeDtypeStruct((PACK_ROWS, TOP_K, n, LANES), y_sorted.dtype),
        mesh=_sc_mesh(),
        scratch_types=[pltpu.VMEM((TOP_K, SC_WINDOW), I32), row_buf, row_buf,
                       pltpu.SemaphoreType.DMA, pltpu.SemaphoreType.DMA],
        name="collect",
    )(y_sorted, pos_win)


def _experts_kernel(pe_ref, pb_ref, pv_ref, st_ref, en_ref, nx_ref,
                    xs_hbm, wg_hbm, wu_hbm, wd_hbm, y_hbm,
                    wgb_ref, wub_ref, wdb_ref, wgf_ref, wuf_ref, wdf_ref, xbuf_ref, stage_ref,
                    in_sems, out_sems, w_sems, state_ref):
    p = pl.program_id(0)
    n_pairs = pl.num_programs(0)
    e = pe_ref[p]
    blk = pb_ref[p]
    rb = xbuf_ref.shape[2]
    prev = jnp.maximum(p - 1, 0)
    nxt = jnp.minimum(p + 1, n_pairs - 1)
    live = pv_ref[p] == 1
    first = jnp.logical_or(p == 0, pb_ref[prev] != blk)
    block_ends = jnp.logical_or(pb_ref[nxt] != blk, pv_ref[nxt] == 0)
    last = jnp.logical_or(p == n_pairs - 1, block_ends)

    def in_copy(j, block, slot):
        return pltpu.make_async_copy(xs_hbm.at[pl.ds(block * rb, rb), j], xbuf_ref.at[slot, j],
                                     in_sems.at[slot])

    def out_copy(j, block, slot):
        return pltpu.make_async_copy(stage_ref.at[slot, j], y_hbm.at[pl.ds(block * rb, rb), j],
                                     out_sems.at[slot])

    def drain(slot):
        @pl.when(state_ref[1 + slot] == 1)
        def _():
            for j in range(PACK_ROWS):
                out_copy(j, 0, slot).wait()
            state_ref[1 + slot] = 0

    @pl.when(p == 0)
    def _():
        state_ref[0] = 0
        state_ref[1] = 0
        state_ref[2] = 0
        stage_ref[...] = jnp.zeros_like(stage_ref)
        for j in range(PACK_ROWS):
            in_copy(j, blk, 0).start()

    slot = jnp.where(jnp.logical_and(first, p > 0), 1 - state_ref[0], state_ref[0])
    state_ref[0] = slot

    @pl.when(jnp.logical_and(first, blk + 1 < y_hbm.shape[0] // rb))
    def _():
        for j in range(PACK_ROWS):
            in_copy(j, blk + 1, 1 - slot).start()

    @pl.when(first)
    def _():
        for j in range(PACK_ROWS):
            in_copy(j, blk, slot).wait()
        drain(slot)

    def w_copies(expert, wslot):
        return (pltpu.make_async_copy(wg_hbm.at[expert], wgf_ref.at[wslot], w_sems.at[wslot]),
                pltpu.make_async_copy(wu_hbm.at[expert], wuf_ref.at[wslot], w_sems.at[wslot]),
                pltpu.make_async_copy(wd_hbm.at[expert], wdf_ref.at[wslot], w_sems.at[wslot]))

    @pl.when(p == 0)
    def _():
        state_ref[3] = 0
        for cp in w_copies(e, 0):
            cp.start()

    new_expert = jnp.logical_or(p == 0, pe_ref[prev] != e)
    wslot = jnp.where(jnp.logical_and(new_expert, p > 0), 1 - state_ref[3], state_ref[3])
    state_ref[3] = wslot

    @pl.when(new_expert)
    def _():
        for cp in w_copies(e, wslot):
            cp.wait()
        wgb_ref[...] = wgf_ref[wslot].astype(BF16)
        wub_ref[...] = wuf_ref[wslot].astype(BF16)
        wdb_ref[...] = wdf_ref[wslot].astype(BF16)

        @pl.when(nx_ref[e] != e)
        def _():
            for cp in w_copies(nx_ref[e], 1 - wslot):
                cp.start()

    lo_row = st_ref[e]
    hi_row = en_ref[e]

    def sub_block(s, row0, shared):
        sub = pl.ds(s * SUB_BLOCK, SUB_BLOCK)
        pieces = []
        for j in range(PACK_ROWS):
            w = xbuf_ref[slot, j, sub, :]
            pieces.append(lax.bitcast_convert_type(w.astype(jnp.int16), BF16))
            pieces.append(lax.bitcast_convert_type(
                lax.shift_right_logical(w, 16).astype(jnp.int16), BF16))
        xb = jnp.concatenate(pieces, axis=1)
        gate = jnp.dot(xb, wgb_ref[...], preferred_element_type=F32)
        up = jnp.dot(xb, wub_ref[...], preferred_element_type=F32)
        yb = jnp.dot((_silu(gate) * up).astype(BF16), wdb_ref[...], preferred_element_type=F32)
        if shared:
            rows = row0 + lax.broadcasted_iota(I32, (SUB_BLOCK, 1), 0)
            mine = jnp.logical_and(rows >= lo_row, rows < hi_row)
        for j in range(PACK_ROWS):
            word = pltpu.pack_elementwise(
                [yb[:, 2 * j * LANES:(2 * j + 1) * LANES], yb[:, (2 * j + 1) * LANES:(2 * j + 2) * LANES]],
                packed_dtype=BF16)
            if shared:
                word = jnp.where(mine, word, stage_ref[slot, j, sub, :])
            stage_ref[slot, j, sub, :] = word

    def one(s):
        row0 = blk * rb + s * SUB_BLOCK
        touched = jnp.logical_and(live, jnp.logical_and(row0 < hi_row, row0 + SUB_BLOCK > lo_row))
        whole = jnp.logical_and(lo_row <= row0, hi_row >= row0 + SUB_BLOCK)

        @pl.when(jnp.logical_and(touched, whole))
        def _():
            sub_block(s, row0, shared=False)

        @pl.when(jnp.logical_and(touched, jnp.logical_not(whole)))
        def _():
            sub_block(s, row0, shared=True)

    def pairs():
        for s in range(0, rb // SUB_BLOCK, 2):
            row0 = blk * rb + s * SUB_BLOCK
            both = jnp.logical_and(live, jnp.logical_and(lo_row <= row0, hi_row >= row0 + 2 * SUB_BLOCK))

            @pl.when(both)
            def _():
                sub_block(s, row0, shared=False)
                sub_block(s + 1, row0 + SUB_BLOCK, shared=False)

            @pl.when(jnp.logical_not(both))
            def _():
                one(s)
                one(s + 1)

    entire = jnp.logical_and(live, jnp.logical_and(lo_row <= blk * rb, hi_row >= (blk + 1) * rb))

    @pl.when(entire)
    def _():
        for s in range(rb // SUB_BLOCK):
            sub_block(s, blk * rb + s * SUB_BLOCK, shared=False)

    @pl.when(jnp.logical_not(entire))
    def _():
        pairs()

    @pl.when(jnp.logical_and(live, last))
    def _():
        for j in range(PACK_ROWS):
            out_copy(j, blk, slot).start()
        state_ref[1 + slot] = 1

    @pl.when(p == n_pairs - 1)
    def _():
        drain(0)
        drain(1)


def _experts_call(pair_e, pair_blk, pair_ok, starts, ends, next_e, xs, w_gate, w_up, w_down):
    ne, d, de = w_gate.shape
    rb = ROW_BLOCK
    n_pairs = pair_e.shape[0]
    anywhere = pl.BlockSpec(memory_space=pl.ANY)
    grid_spec = pltpu.PrefetchScalarGridSpec(
        num_scalar_prefetch=6,
        grid=(n_pairs,),
        in_specs=[anywhere, anywhere, anywhere, anywhere],
        out_specs=anywhere,
        scratch_shapes=[pltpu.VMEM((d, de), BF16), pltpu.VMEM((d, de), BF16),
                        pltpu.VMEM((de, d), BF16),
                        pltpu.VMEM((2, d, de), F32), pltpu.VMEM((2, d, de), F32),
                        pltpu.VMEM((2, de, d), F32),
                        pltpu.VMEM((2, PACK_ROWS, rb, LANES), I32),
                        pltpu.VMEM((2, PACK_ROWS, rb, LANES), I32),
                        pltpu.SemaphoreType.DMA((2,)), pltpu.SemaphoreType.DMA((2,)),
                        pltpu.SemaphoreType.DMA((2,)),
                        pltpu.SMEM((4,), I32)],
    )
    return pl.pallas_call(
        _experts_kernel,
        out_shape=jax.ShapeDtypeStruct(xs.shape, xs.dtype),
        grid_spec=grid_spec,
        compiler_params=pltpu.CompilerParams(
            dimension_semantics=("arbitrary",), vmem_limit_bytes=VMEM_LIMIT),
        name="experts",
    )(pair_e, pair_blk, pair_ok, starts, ends, next_e, xs, w_gate, w_up, w_down)


def _combine_kernel(y0_ref, y1_ref, y2_ref, y3_ref, x1_ref, hp_ref, w_ref, g2_ref, fg_ref,
                    wsg_ref, wsu_ref, wsd_ref, o_ref):
    tm = x1_ref.shape[0]
    w = w_ref[...]
    parts = []
    for y_ref in (y0_ref, y1_ref, y2_ref, y3_ref):
        for half in range(2):
            acc = None
            for k in range(TOP_K):
                piece = pltpu.unpack_elementwise(y_ref[k], index=half, packed_dtype=BF16,
                                                 unpacked_dtype=F32) * w[:, k:k + 1]
                acc = piece if acc is None else acc + piece
            parts.append(acc)
    routed = jnp.concatenate(parts, axis=1)
    pieces = []
    for j in range(PACK_ROWS):
        word = hp_ref[pl.ds(j, tm, stride=PACK_ROWS), :]
        pieces.append(lax.bitcast_convert_type(word.astype(jnp.int16), BF16))
        pieces.append(lax.bitcast_convert_type(lax.shift_right_logical(word, 16).astype(jnp.int16), BF16))
    h2 = jnp.concatenate(pieces, axis=1)
    hid = _silu(jnp.dot(h2, wsg_ref[...], preferred_element_type=F32)) * jnp.dot(
        h2, wsu_ref[...], preferred_element_type=F32)
    shared = jnp.dot(hid.astype(BF16), wsd_ref[...], preferred_element_type=F32)
    x2 = x1_ref[...] + g2_ref[0] * (routed + shared)
    ms = jnp.mean(x2 * x2, axis=-1, keepdims=True)
    o_ref[...] = x2 * lax.rsqrt(ms + EPS) * fg_ref[...]


def _combine_kernel_into(*refs):
    _combine_kernel(*refs[:-2], refs[-1])


def _combine_call(y_tok, x1, hp, w_tok, g2, fgain, wsg, wsu, wsd, seq, part, prev_out):
    n, d = x1.shape
    dsh = wsg.shape[1]
    tm = MOVE_TILE
    per_b = seq // tm
    tiles = y_tok.shape[2] // tm
    t0 = part * tiles
    const = lambda i: (0, 0)

    def piece_spec(j):
        return pl.BlockSpec((None, TOP_K, tm, LANES), lambda i: (j, 0, i, 0))

    in_specs = [piece_spec(j) for j in range(PACK_ROWS)] + [
        pl.BlockSpec((tm, d), lambda i: (t0 + i, 0)),
        pl.BlockSpec((tm * PACK_ROWS, LANES), lambda i: (t0 + i, 0)),
        pl.BlockSpec((tm, TOP_K), lambda i: (t0 + i, 0)),
        pl.BlockSpec((1, 1, d), lambda i: ((t0 + i) // per_b, 0, 0)),
        pl.BlockSpec((1, d), const),
        pl.BlockSpec((d, dsh), const),
        pl.BlockSpec((d, dsh), const),
        pl.BlockSpec((dsh, d), const)]
    args = [y_tok, y_tok, y_tok, y_tok, x1, hp, w_tok, g2, fgain, wsg, wsu, wsd]
    aliases = {}
    body = _combine_kernel
    if prev_out is not None:
        in_specs.append(pl.BlockSpec(memory_space=pl.ANY))
        args.append(prev_out)
        aliases = {len(args) - 1: 0}
        body = _combine_kernel_into
    return pl.pallas_call(
        body,
        out_shape=jax.ShapeDtypeStruct((n, d), F32),
        grid=(tiles,),
        in_specs=in_specs,
        out_specs=pl.BlockSpec((tm, d), lambda i: (t0 + i, 0)),
        input_output_aliases=aliases,
        compiler_params=pltpu.CompilerParams(
            dimension_semantics=("arbitrary",), vmem_limit_bytes=VMEM_LIMIT),
        name="combine",
    )(*args)


def _pair_tables(counts, n_rows):
    ne = counts.shape[0]
    sizes = counts.astype(I32)
    ends = jnp.cumsum(sizes)
    starts = ends - sizes
    first_blk = starts // ROW_BLOCK
    last_blk = (ends - 1) // ROW_BLOCK
    n_pairs = jnp.where(sizes > 0, last_blk - first_blk + 1, 0)
    pair_end = jnp.cumsum(n_pairs)
    pair_start = pair_end - n_pairs
    max_pairs = n_rows // ROW_BLOCK + ne
    p = jnp.arange(max_pairs, dtype=I32)
    ok = p < pair_end[-1]
    pc = jnp.minimum(p, pair_end[-1] - 1)
    pair_e = jnp.minimum(jnp.searchsorted(pair_end, pc, side='right'), ne - 1).astype(I32)
    pair_blk = (first_blk[pair_e] + pc - pair_start[pair_e]).astype(I32)
    eid = jnp.arange(ne, dtype=I32)
    later = lax.cummin(jnp.where(sizes > 0, eid, ne), reverse=True)
    next_e = jnp.concatenate([later[1:], jnp.full((1,), ne, I32)])
    next_e = jnp.where(next_e < ne, next_e, eid)
    return pair_e, pair_blk, ok.astype(I32), starts.astype(I32), ends.astype(I32), next_e


def kernel(x, c, w_ada, b_ada, w_in, lb_logits, hgrn_norm, lam_re, lam_im, log_dt, b_re, b_im,
           c_re, c_im, d_skip, w_glu, b_glu, w_out, w_router, router_bias, w_gate, w_up, w_down,
           ws_gate, ws_up, ws_down, final_gain):
    bsz, seq, d = x.shape
    n = bsz * seq
    fdim = HGRN_HEADS * HGRN_KDIM
    n_chunks = seq // CHUNK
    lb = jnp.cumsum(jax.nn.softmax(lb_logits.astype(F32), axis=0), axis=0)[0].reshape(1, fdim)

    mod = _mod_call(c, w_ada[0], b_ada[0]).reshape(bsz, 6, d)
    nb = w_in.shape[2] - 4 * fdim
    perm = jnp.concatenate([jnp.arange(0, nb, 2), jnp.arange(1, nb, 2)])
    w_in_b = w_in[0].astype(BF16)
    ut, out_a, ucm = _mix_front_call(x, mod, w_in_b[:, :4 * fdim], w_in_b[:, 4 * fdim:][:, perm].T, lb,
                                hgrn_norm[0].reshape(1, fdim))

    k_tab, p_tab, q_tab, a1, a2 = _s5_tables(lam_re[0], lam_im[0], log_dt[0], b_re[0], b_im[0],
                                             c_re[0], c_im[0])
    yt = _s5_call(ut, k_tab, p_tab, q_tab, a1, a2, n_chunks, bsz)

    wr_t = w_router[0].T
    wr_hi = wr_t.astype(BF16)
    wr_lo = (wr_t - wr_hi.astype(F32)).astype(BF16)
    w_out_b = w_out[0].astype(BF16)
    w_out_p = jnp.concatenate([w_out_b[:fdim], w_out_b[fdim:][perm]], axis=0)
    x1, hp, logits_t = _mix_back_call(
        x.reshape(n, d), out_a.reshape(n, fdim), yt, ucm, mod,
        d_skip[0][perm].reshape(nb, 1), w_glu[0][perm][:, perm].T.astype(BF16),
        b_glu[0][perm].reshape(nb, 1), w_out_p, wr_hi, wr_lo, seq)

    top_idx, top_w, rank, counts = _route_call(logits_t, router_bias[0])
    pair_e, pair_blk, pair_ok, starts, ends, next_e = _pair_tables(counts[:, 0], n * TOP_K)
    pos = _pos_call(top_idx, rank, starts)

    pos_win = pos.reshape(TOP_K, n // SC_WINDOW, SC_WINDOW).transpose(1, 0, 2)
    xs = _dispatch_call(pos_win, hp.reshape(n, PACK_ROWS, LANES))
    y_sorted = _experts_call(pair_e, pair_blk, pair_ok, starts, ends, next_e, xs,
                             w_gate[0], w_up[0], w_down[0])
    w_tok = top_w.T
    g2 = mod[:, 5:6, :]
    fgain = final_gain.reshape(1, d)
    wsg, wsu, wsd = ws_gate[0].astype(BF16), ws_up[0].astype(BF16), ws_down[0].astype(BF16)
    wins = pos_win.shape[0] // TAIL_PARTS
    out = None
    for part in range(TAIL_PARTS):
        y_tok = _collect_call(pos_win[part * wins:(part + 1) * wins], y_sorted)
        out = _combine_call(y_tok, x1, hp, w_tok, g2, fgain, wsg, wsu, wsd, seq, part, out)
    return out.reshape(bsz, seq, d)
```

```python
import functools

import jax
import jax.numpy as jnp
from jax import lax
from jax.experimental import pallas as pl
from jax.experimental.pallas import tpu as pltpu
from jax.experimental.pallas import tpu_sc as plsc

F32 = jnp.float32
BF16 = jnp.bfloat16
I32 = jnp.int32

EPS = 1e-6
CHUNK = 64
HGRN_HEADS = 4
HGRN_KDIM = 128
S5_GROUP = 16
S5_STATE = 64
N_EXPERT_GROUPS = 8
TOPK_GROUPS = 4
TOP_K = 8
ROUTE_SCALE = 2.5
LANES = 128
PACK_ROWS = 4
SC_CORES = 2
SC_SUBCORES = 16
SC_WINDOW = 64

SEQ_TILE = 512
TOK_TILE = 512
ROUTE_TILE = 512
POS_TILE = 1024
MOVE_TILE = 512
ROW_BLOCK = 4096
SUB_BLOCK = 512
TAIL_PARTS = 4
VMEM_LIMIT = 56 * 1024 * 1024

_NT = (((1,), (1,)), ((), ()))
_TN = (((0,), (0,)), ((), ()))


def _sigmoid(v):
    return 0.5 * jnp.tanh(0.5 * v) + 0.5


def _silu(v):
    return v * _sigmoid(v)


def _bdot(a, b):
    return jnp.dot(a.astype(BF16), b.astype(BF16), preferred_element_type=F32)


def _store_packed(ref, val, n_rows):
    for j in range(PACK_ROWS):
        lo = val[:, 2 * j * LANES:(2 * j + 1) * LANES]
        hi = val[:, (2 * j + 1) * LANES:(2 * j + 2) * LANES]
        ref[pl.ds(j, n_rows, stride=PACK_ROWS), :] = pltpu.pack_elementwise([lo, hi], packed_dtype=BF16)


def _mod_kernel(c_ref, w_ref, b_ref, o_ref):
    o_ref[...] = _bdot(_silu(c_ref[...]), w_ref[...]) + b_ref[...]


def _mod_call(c, w_ada, b_ada):
    bsz, d = c.shape
    n_out = w_ada.shape[1]
    return pl.pallas_call(
        _mod_kernel,
        out_shape=jax.ShapeDtypeStruct((bsz, n_out), F32),
        grid=(n_out // d,),
        in_specs=[pl.BlockSpec((bsz, d), lambda j: (0, 0)),
                  pl.BlockSpec((d, d), lambda j: (0, j)),
                  pl.BlockSpec((1, d), lambda j: (0, j))],
        out_specs=pl.BlockSpec((bsz, d), lambda j: (0, j)),
        compiler_params=pltpu.CompilerParams(vmem_limit_bytes=VMEM_LIMIT),
        name="mod",
    )(c, w_ada, b_ada.reshape(1, n_out))


def _split_chunk_pairs(tile_even, tile_odd):
    low = lax.broadcasted_iota(I32, tile_even.shape, 1) < CHUNK
    first = jnp.where(low, tile_even, pltpu.roll(tile_odd, CHUNK, 1))
    second = jnp.where(low, pltpu.roll(tile_even, CHUNK, 1), tile_odd)
    return first, second


def _mix_front_kernel(x_ref, mod_ref, win_ref, wut_ref, lb_ref, gn_ref, ltri_ref,
                      ut_ref, oa_ref, ucm_ref, proj_ref, st_ref, flat_ref):
    fdim = HGRN_HEADS * HGRN_KDIM
    ts = x_ref.shape[1]
    pairs = wut_ref.shape[0] // 2
    rows = ts // CHUNK

    @pl.when(pl.program_id(1) == 0)
    def _():
        st_ref[...] = jnp.zeros_like(st_ref)

    x = x_ref[0]
    ms = jnp.mean(x * x, axis=-1, keepdims=True)
    h = x * lax.rsqrt(ms + EPS) * (1.0 + mod_ref[0, 1:2, :]) + mod_ref[0, 0:1, :]
    hb = h.astype(BF16)
    proj_ref[...] = jnp.dot(hb, win_ref[...], preferred_element_type=F32)
    u_t = lax.dot_general(wut_ref[...], hb, _NT, preferred_element_type=F32)
    ucm_ref[...] = u_t.astype(BF16)
    for m in range(ts // LANES):
        first, second = _split_chunk_pairs(u_t[:pairs, m * LANES:(m + 1) * LANES],
                                           u_t[pairs:, m * LANES:(m + 1) * LANES])
        flat_ref[2 * m * pairs:(2 * m + 1) * pairs, :] = first
        flat_ref[(2 * m + 1) * pairs:(2 * m + 2) * pairs, :] = second
    per_group = S5_GROUP // 2
    for q in range(pairs):
        ut_ref[q // per_group, :, (q % per_group) * LANES:(q % per_group + 1) * LANES] = (
            flat_ref[pl.ds(q, rows, stride=pairs), :])

    lb = lb_ref[...]
    gn = gn_ref[...]
    ltri = ltri_ref[...]
    row = lax.broadcasted_iota(I32, (CHUNK, CHUNK), 0)
    col = lax.broadcasted_iota(I32, (CHUNK, CHUNK), 1)
    causal = row >= col

    def chunk_step(ci, carry):
        r0 = pl.multiple_of(ci * CHUNK, CHUNK)
        q = proj_ref[pl.ds(r0, CHUNK), 0:fdim]
        fl = proj_ref[pl.ds(r0, CHUNK), fdim:2 * fdim]
        iv = proj_ref[pl.ds(r0, CHUNK), 2 * fdim:3 * fdim]
        og = proj_ref[pl.ds(r0, CHUNK), 3 * fdim:4 * fdim]
        f = lb + (1.0 - lb) * _sigmoid(fl)
        lf = jnp.log(f)
        lf_hi = lf.astype(BF16)
        rem = lf - lf_hi.astype(F32)
        lf_mid = rem.astype(BF16)
        lf_lo = (rem - lf_mid.astype(F32)).astype(BF16)
        b = (jnp.dot(ltri, lf_hi, preferred_element_type=F32)
             + jnp.dot(ltri, lf_mid, preferred_element_type=F32)
             + jnp.dot(ltri, lf_lo, preferred_element_type=F32))
        b_ref = b[CHUNK // 2 - 1:CHUNK // 2, :]
        b_last = b[CHUNK - 1:CHUNK, :]
        qs = _silu(q)
        kk = 1.0 - f
        qe_f = qs * jnp.exp(b - b_ref)
        ke_f = kk * jnp.exp(b_ref - b)
        qe = qe_f.astype(BF16)
        ke = ke_f.astype(BF16)
        qb = (qe_f * jnp.exp(b_ref)).astype(BF16)
        k2 = (ke_f * jnp.exp(b_last - b_ref)).astype(BF16)
        dec = jnp.exp(b_last)
        ivb = iv.astype(BF16)
        outs = []
        for hh in range(HGRN_HEADS):
            sl = slice(hh * HGRN_KDIM, (hh + 1) * HGRN_KDIM)
            att = lax.dot_general(qe[:, sl], ke[:, sl], _NT, preferred_element_type=F32)
            att = jnp.where(causal, att, 0.0)
            st = st_ref[hh]
            o = jnp.dot(att.astype(BF16), ivb[:, sl], preferred_element_type=F32)
            o = o + lax.dot_general(qb[:, sl], st.astype(BF16), _NT, preferred_element_type=F32)
            st_ref[hh] = st * dec[:, sl] + lax.dot_general(
                ivb[:, sl], k2[:, sl], _TN, preferred_element_type=F32)
            outs.append(o * lax.rsqrt(jnp.mean(o * o, axis=-1, keepdims=True) + EPS))
        o = jnp.concatenate(outs, axis=1) * gn * _silu(og)
        oa_ref[0, pl.ds(r0, CHUNK), :] = o.astype(BF16)
        return carry

    lax.fori_loop(0, ts // CHUNK, chunk_step, 0, unroll=True)


def _mix_front_call(x, mod, w_main, w_ut, lb, gn):
    bsz, seq, d = x.shape
    fdim = HGRN_HEADS * HGRN_KDIM
    ncols = w_main.shape[1]
    nb = w_ut.shape[0]
    groups = nb // S5_GROUP
    ltri = jnp.tril(jnp.ones((CHUNK, CHUNK), BF16))
    ts = SEQ_TILE
    tiles = seq // ts
    rows = ts // CHUNK
    return pl.pallas_call(
        _mix_front_kernel,
        out_shape=(jax.ShapeDtypeStruct((groups, bsz * seq // CHUNK, S5_GROUP * CHUNK), F32),
                   jax.ShapeDtypeStruct((bsz, seq, fdim), BF16),
                   jax.ShapeDtypeStruct((nb, bsz * seq), BF16)),
        grid=(bsz, tiles),
        in_specs=[pl.BlockSpec((1, ts, d), lambda b, j: (b, j, 0)),
                  pl.BlockSpec((1, 6, d), lambda b, j: (b, 0, 0)),
                  pl.BlockSpec((d, ncols), lambda b, j: (0, 0)),
                  pl.BlockSpec((nb, d), lambda b, j: (0, 0)),
                  pl.BlockSpec((1, fdim), lambda b, j: (0, 0)),
                  pl.BlockSpec((1, fdim), lambda b, j: (0, 0)),
                  pl.BlockSpec((CHUNK, CHUNK), lambda b, j: (0, 0))],
        out_specs=(pl.BlockSpec((groups, rows, S5_GROUP * CHUNK), lambda b, j: (0, b * tiles + j, 0)),
                   pl.BlockSpec((1, ts, fdim), lambda b, j: (b, j, 0)),
                   pl.BlockSpec((nb, ts), lambda b, j: (0, b * tiles + j))),
        scratch_shapes=[pltpu.VMEM((ts, ncols), F32),
                        pltpu.VMEM((HGRN_HEADS, fdim // HGRN_HEADS, HGRN_KDIM), F32),
                        pltpu.VMEM((rows * nb // 2, LANES), F32)],
        compiler_params=pltpu.CompilerParams(
            dimension_semantics=("arbitrary", "arbitrary"), vmem_limit_bytes=VMEM_LIMIT),
        name="mix_front",
    )(x, mod, w_main, w_ut, lb, gn, ltri)


def _s5_tables(lam_re, lam_im, log_dt, b_re, b_im, c_re, c_im):
    t = CHUNK
    lam = lax.complex(jnp.minimum(lam_re, -1e-4), lam_im)
    lam_dt = lam * jnp.exp(log_dt)[:, None]
    lam_bar = jnp.exp(lam_dt)
    b_bar = ((lam_bar - 1.0) / lam)[..., None] * lax.complex(b_re, b_im)
    c_mat = lax.complex(c_re, c_im)
    taus = jnp.arange(t + 1, dtype=F32)
    lam_pow = jnp.exp(lam_dt[:, None, :] * taus[None, :, None])
    g, p = lam.shape
    c = b_re.shape[-1]
    cb = c_mat[:, None, :, :] * b_bar.transpose(0, 2, 1)[:, :, None, :]
    cb = jnp.concatenate([cb.real, -cb.imag], axis=-1)
    cb = cb.reshape(g, c, c // 2, 2 * 2 * p).reshape(g, c * c // 2, 4 * p)
    lp = jnp.concatenate([lam_pow[:, :t].real, lam_pow[:, :t].imag], axis=-1).transpose(0, 2, 1)
    zero = jnp.zeros_like(lp)
    lp2 = jnp.concatenate([jnp.concatenate([lp, zero], axis=2),
                           jnp.concatenate([zero, lp], axis=2)], axis=1)
    pc = lam_pow[:, t - 1::-1][:, :t, :, None] * b_bar[:, None, :, :]
    pc = pc.transpose(0, 3, 1, 2).reshape(g, c * t, p)
    p_tab = jnp.concatenate([pc.real, pc.imag], axis=-1)
    ql = c_mat[:, None, :, :] * lam_pow[:, 1:t + 1, None, :]
    ql = ql.transpose(0, 3, 2, 1).reshape(g, p, c * t)
    q_tab = jnp.concatenate([ql.real, -ql.imag], axis=1)
    lam_t = lam_pow[:, t]
    a1 = jnp.concatenate([lam_t.real, lam_t.real], axis=-1)[:, None, :]
    a2 = jnp.concatenate([-lam_t.imag, lam_t.imag], axis=-1)[:, None, :]
    return cb, lp2, p_tab.astype(BF16), q_tab.astype(BF16), a1, a2


def _split3(v):
    hi = v.astype(BF16)
    rem = v - hi.astype(F32)
    mid = rem.astype(BF16)
    return hi, mid, (rem - mid.astype(F32)).astype(BF16)


def _s5_kernel(u_ref, cb_ref, lp_ref, p_ref, q_ref, a1_ref, a2_ref, y_ref, v_ref, xs_ref, m_ref, k_ref,
               *, n_chunks, n_batch):
    c_hi, c_mid, c_lo = _split3(cb_ref[0])
    l_hi, l_mid, l_lo = _split3(lp_ref[0])
    k_ref[...] = (jnp.dot(c_hi, l_hi, preferred_element_type=F32)
                  + jnp.dot(c_hi, l_mid, preferred_element_type=F32)
                  + jnp.dot(c_mid, l_hi, preferred_element_type=F32)
                  + jnp.dot(c_hi, l_lo, preferred_element_type=F32)
                  + jnp.dot(c_mid, l_mid, preferred_element_type=F32)
                  + jnp.dot(c_lo, l_hi, preferred_element_type=F32))
    n_pairs = m_ref.shape[1] // LANES
    n_in = k_ref.shape[0] // n_pairs
    lane = lax.broadcasted_iota(I32, (CHUNK, LANES), 1)
    causal = (lane & (CHUNK - 1)) >= lax.broadcasted_iota(I32, (CHUNK, LANES), 0)
    for ci in range(n_in):
        for a in range(n_pairs):
            lags = jnp.broadcast_to(k_ref[ci * n_pairs + a:ci * n_pairs + a + 1, :], (CHUNK, LANES))
            tile = pltpu.roll(lags, 0, 1, stride=1, stride_axis=0)
            m_ref[ci * CHUNK:(ci + 1) * CHUNK, a * LANES:(a + 1) * LANES] = jnp.where(
                causal, tile, 0.0).astype(BF16)
    u = u_ref[0].astype(BF16)
    v_ref[...] = jnp.dot(u, p_ref[0], preferred_element_type=F32)
    a1 = a1_ref[0]
    a2 = a2_ref[0]
    half = xs_ref.shape[1] // 2

    def step(n, state):
        xs_ref[pl.ds(n, n_batch, stride=n_chunks), :] = state
        return (a1 * state + a2 * pltpu.roll(state, half, 1)
                + v_ref[pl.ds(n, n_batch, stride=n_chunks), :])

    lax.fori_loop(0, n_chunks, step, jnp.zeros((n_batch, xs_ref.shape[1]), F32))
    y = jnp.dot(u, m_ref[...], preferred_element_type=F32)
    y_ref[0] = y + jnp.dot(xs_ref[...].astype(BF16), q_ref[0], preferred_element_type=F32)


def _s5_call(ut, cb, lp2, p_tab, q_tab, a1, a2, n_chunks, n_batch):
    g, rows, width = ut.shape
    p2 = p_tab.shape[-1]
    return pl.pallas_call(
        functools.partial(_s5_kernel, n_chunks=n_chunks, n_batch=n_batch),
        out_shape=jax.ShapeDtypeStruct((g, rows, width), F32),
        grid=(g,),
        in_specs=[pl.BlockSpec((1, rows, width), lambda i: (i, 0, 0)),
                  pl.BlockSpec((1,) + cb.shape[1:], lambda i: (i, 0, 0)),
                  pl.BlockSpec((1,) + lp2.shape[1:], lambda i: (i, 0, 0)),
                  pl.BlockSpec((1, width, p2), lambda i: (i, 0, 0)),
                  pl.BlockSpec((1, p2, width), lambda i: (i, 0, 0)),
                  pl.BlockSpec((1, 1, p2), lambda i: (i, 0, 0)),
                  pl.BlockSpec((1, 1, p2), lambda i: (i, 0, 0))],
        out_specs=pl.BlockSpec((1, rows, width), lambda i: (i, 0, 0)),
        scratch_shapes=[pltpu.VMEM((rows, p2), F32), pltpu.VMEM((rows, p2), F32),
                        pltpu.VMEM((width, width), BF16),
                        pltpu.VMEM((cb.shape[1], lp2.shape[2]), F32)],
        compiler_params=pltpu.CompilerParams(
            dimension_semantics=("arbitrary",), vmem_limit_bytes=VMEM_LIMIT),
        name="s5",
    )(ut, cb, lp2, p_tab, q_tab, a1, a2)


def _token_major(flat_ref, src_ref):
    groups, rows, _ = src_ref.shape
    per_group = S5_GROUP // 2
    pairs = groups * per_group
    for q in range(pairs):
        flat_ref[pl.ds(q, rows, stride=pairs), :] = (
            src_ref[q // per_group, :, (q % per_group) * LANES:(q % per_group + 1) * LANES])
    tiles = []
    for m in range(rows // 2):
        even, odd = _split_chunk_pairs(flat_ref[2 * m * pairs:(2 * m + 1) * pairs, :],
                                       flat_ref[(2 * m + 1) * pairs:(2 * m + 2) * pairs, :])
        tiles.append(jnp.concatenate([even, odd], axis=0))
    return jnp.concatenate(tiles, axis=1)


def _mix_back_kernel(x_ref, oa_ref, yt_ref, ucm_ref, mod_ref, dskip_ref, wglu_ref, bglu_ref,
                     wout_ref, wrh_ref, wrl_ref,
                     x1_ref, hp_ref, lt_ref, flat_ref):
    na = oa_ref.shape[1]
    y_t = _token_major(flat_ref, yt_ref)
    u_t = ucm_ref[...].astype(F32)
    z_t = jax.nn.gelu(y_t + dskip_ref[...] * u_t)
    gate_t = _sigmoid(jnp.dot(wglu_ref[...], z_t.astype(BF16), preferred_element_type=F32)
                      + bglu_ref[...])
    ob_t = (z_t * gate_t).astype(BF16)
    mixed = (jnp.dot(oa_ref[...], wout_ref[0:na, :], preferred_element_type=F32)
             + lax.dot_general(ob_t, wout_ref[na:, :], _TN, preferred_element_type=F32))
    x1 = x_ref[...] + mod_ref[0, 2:3, :] * mixed
    ms = jnp.mean(x1 * x1, axis=-1, keepdims=True)
    h2 = x1 * lax.rsqrt(ms + EPS) * (1.0 + mod_ref[0, 4:5, :]) + mod_ref[0, 3:4, :]
    _store_packed(hp_ref, h2, h2.shape[0])
    h_hi = h2.astype(BF16)
    h_lo = (h2 - h_hi.astype(F32)).astype(BF16)
    lt = lax.dot_general(wrh_ref[...], h_hi, _NT, preferred_element_type=F32)
    lt = lt + lax.dot_general(wrl_ref[...], h_hi, _NT, preferred_element_type=F32)
    lt = lt + lax.dot_general(wrh_ref[...], h_lo, _NT, preferred_element_type=F32)
    lt_ref[...] = lt
    x1_ref[...] = x1


def _mix_back_call(x2d, oa, yt, ucm, mod, dskip, wglu, bglu, wout, wrh, wrl, seq):
    n, d = x2d.shape
    nb = oa.shape[1]
    ne = wrh.shape[0]
    tm = TOK_TILE
    per_b = seq // tm
    groups, _, width = yt.shape
    rows = tm // CHUNK
    const = lambda i: (0, 0)
    flat_block = pl.BlockSpec((groups, rows, width), lambda i: (0, i, 0))
    return pl.pallas_call(
        _mix_back_kernel,
        out_shape=(jax.ShapeDtypeStruct((n, d), F32),
                   jax.ShapeDtypeStruct((n * PACK_ROWS, LANES), I32),
                   jax.ShapeDtypeStruct((ne, n), F32)),
        grid=(n // tm,),
        in_specs=[pl.BlockSpec((tm, d), lambda i: (i, 0)),
                  pl.BlockSpec((tm, nb), lambda i: (i, 0)),
                  flat_block,
                  pl.BlockSpec((nb, tm), lambda i: (0, i)),
                  pl.BlockSpec((1, 6, d), lambda i: (i // per_b, 0, 0)),
                  pl.BlockSpec((nb, 1), const),
                  pl.BlockSpec((nb, nb), const),
                  pl.BlockSpec((nb, 1), const),
                  pl.BlockSpec((d, d), const),
                  pl.BlockSpec((ne, d), const),
                  pl.BlockSpec((ne, d), const)],
        out_specs=(pl.BlockSpec((tm, d), lambda i: (i, 0)),
                   pl.BlockSpec((tm * PACK_ROWS, LANES), lambda i: (i, 0)),
                   pl.BlockSpec((ne, tm), lambda i: (0, i))),
        scratch_shapes=[pltpu.VMEM((rows * nb // 2, LANES), F32)],
        compiler_params=pltpu.CompilerParams(
            dimension_semantics=("arbitrary",), vmem_limit_bytes=VMEM_LIMIT),
        name="mix_back",
    )(x2d, oa, yt, ucm, mod, dskip, wglu, bglu, wout, wrh, wrl)


def _route_kernel(lt_ref, bias_ref, su_ref, idx_ref, w_ref, rank_ref, cnt_ref, run_ref):
    ne, tr = lt_ref.shape
    per_group = ne // N_EXPERT_GROUPS
    neg = -jnp.inf

    @pl.when(pl.program_id(0) == 0)
    def _():
        run_ref[...] = jnp.zeros_like(run_ref)

    s = _sigmoid(lt_ref[...])
    sel = s + bias_ref[...]
    gio = lax.broadcasted_iota(I32, (per_group, tr), 0)
    gscore = []
    for g in range(N_EXPERT_GROUPS):
        v = sel[g * per_group:(g + 1) * per_group, :]
        m1 = jnp.max(v, axis=0, keepdims=True)
        i1 = jnp.min(jnp.where(v == m1, gio, per_group), axis=0, keepdims=True)
        m2 = jnp.max(jnp.where(gio == i1, neg, v), axis=0, keepdims=True)
        gscore.append(m1 + m2)
    masked = []
    for g in range(N_EXPERT_GROUPS):
        ahead = jnp.zeros((1, tr), I32)
        for o in range(N_EXPERT_GROUPS):
            if o == g:
                continue
            wins = (gscore[o] >= gscore[g]) if o < g else (gscore[o] > gscore[g])
            ahead = ahead + wins.astype(I32)
        keep = ahead < TOPK_GROUPS
        masked.append(jnp.where(keep, sel[g * per_group:(g + 1) * per_group, :], neg))
    selm = jnp.concatenate(masked, axis=0)
    eio = lax.broadcasted_iota(I32, (ne, tr), 0)
    candidate = selm > neg
    idxs, ws = [], []
    for k in range(TOP_K):
        m = jnp.max(selm, axis=0, keepdims=True)
        ik = jnp.min(jnp.where(selm == m, eio, ne), axis=0, keepdims=True)
        onehot = eio == ik
        ws.append(jnp.sum(jnp.where(onehot, s, 0.0), axis=0, keepdims=True))
        selm = jnp.where(onehot, neg, selm)
        idxs.append(ik)
    hits = jnp.where(jnp.logical_and(candidate, selm == neg), 1.0, 0.0)
    wsum = ws[0]
    for k in range(1, TOP_K):
        wsum = wsum + ws[k]
    scale = ROUTE_SCALE / wsum
    ranks = jnp.dot(hits.astype(BF16), su_ref[...], preferred_element_type=F32) + run_ref[...]
    for k in range(TOP_K):
        idx_ref[k:k + 1, :] = idxs[k]
        w_ref[k:k + 1, :] = ws[k] * scale
        rk = jnp.sum(jnp.where(eio == idxs[k], ranks, 0.0), axis=0, keepdims=True)
        rank_ref[k:k + 1, :] = rk.astype(I32)
    run_ref[...] = run_ref[...] + jnp.sum(hits, axis=1, keepdims=True)
    cnt_ref[...] = run_ref[...]


def _route_call(lt, bias):
    ne, n = lt.shape
    tr = ROUTE_TILE
    su = jnp.triu(jnp.ones((tr, tr), F32), k=1).astype(BF16)
    return pl.pallas_call(
        _route_kernel,
        out_shape=(jax.ShapeDtypeStruct((TOP_K, n), I32),
                   jax.ShapeDtypeStruct((TOP_K, n), F32),
                   jax.ShapeDtypeStruct((TOP_K, n), I32),
                   jax.ShapeDtypeStruct((ne, 1), F32)),
        grid=(n // tr,),
        in_specs=[pl.BlockSpec((ne, tr), lambda i: (0, i)),
                  pl.BlockSpec((ne, 1), lambda i: (0, 0)),
                  pl.BlockSpec((tr, tr), lambda i: (0, 0))],
        out_specs=(pl.BlockSpec((TOP_K, tr), lambda i: (0, i)),
                   pl.BlockSpec((TOP_K, tr), lambda i: (0, i)),
                   pl.BlockSpec((TOP_K, tr), lambda i: (0, i)),
                   pl.BlockSpec((ne, 1), lambda i: (0, 0))),
        scratch_shapes=[pltpu.VMEM((ne, 1), F32)],
        compiler_params=pltpu.CompilerParams(
            dimension_semantics=("arbitrary",), vmem_limit_bytes=VMEM_LIMIT),
        name="route",
    )(lt, bias.reshape(ne, 1), su)


def _pos_kernel(idx_ref, rank_ref, st_ref, pos_ref):
    ne = st_ref.shape[0]
    tp = idx_ref.shape[1]
    eio = lax.broadcasted_iota(I32, (ne, tp), 0)
    st = st_ref[...]
    for k in range(TOP_K):
        base = jnp.sum(jnp.where(eio == idx_ref[k:k + 1, :], st, 0), axis=0, keepdims=True)
        pos_ref[k:k + 1, :] = base + rank_ref[k:k + 1, :]


def _pos_call(top_idx, rank, starts):
    kk, n = top_idx.shape
    ne = starts.shape[0]
    tp = POS_TILE
    return pl.pallas_call(
        _pos_kernel,
        out_shape=jax.ShapeDtypeStruct((kk, n), I32),
        grid=(n // tp,),
        in_specs=[pl.BlockSpec((kk, tp), lambda i: (0, i)),
                  pl.BlockSpec((kk, tp), lambda i: (0, i)),
                  pl.BlockSpec((ne, 1), lambda i: (0, 0))],
        out_specs=pl.BlockSpec((kk, tp), lambda i: (0, i)),
        compiler_params=pltpu.CompilerParams(
            dimension_semantics=("arbitrary",), vmem_limit_bytes=VMEM_LIMIT),
        name="pos",
    )(top_idx, rank, starts.reshape(ne, 1))


def _sc_mesh():
    return plsc.VectorSubcoreMesh(core_axis_name="c", subcore_axis_name="s",
                                  num_cores=SC_CORES, num_subcores=SC_SUBCORES)


def _sc_worker():
    return lax.axis_index("s") * SC_CORES + lax.axis_index("c")


def _dispatch_call(pos_win, hp):
    n = hp.shape[0]
    n_workers = SC_CORES * SC_SUBCORES
    wins_per_worker = n // SC_WINDOW // n_workers

    def body(hp_hbm, pos_hbm, xs_hbm, idx_a, idx_b, rows_a, rows_b, load_a, load_b, scatter_sem):
        first_win = _sc_worker() * wins_per_worker
        idx_v, rows_v, load_sem = (idx_a, idx_b), (rows_a, rows_b), (load_a, load_b)

        def loads(win, slot):
            return (pltpu.make_async_copy(hp_hbm.at[pl.ds(win * SC_WINDOW, SC_WINDOW)], rows_v[slot],
                                          load_sem[slot]),
                    pltpu.make_async_copy(pos_hbm.at[win], idx_v[slot], load_sem[slot]))

        for cp in loads(first_win, 0):
            cp.start()

        @pl.loop(0, wins_per_worker, step=2)
        def _(w):
            for slot in range(2):
                win = first_win + w + slot
                for cp in loads(win, slot):
                    cp.wait()

                @pl.when(w + slot + 1 < wins_per_worker)
                def _():
                    for cp in loads(win + 1, 1 - slot):
                        cp.start()

                copies = [pltpu.async_copy(rows_v[slot], xs_hbm.at[idx_v[slot].at[k]], scatter_sem)
                          for k in range(TOP_K)]
                for cp in copies:
                    cp.wait()

    idx_buf = pltpu.VMEM((TOP_K, SC_WINDOW), I32)
    row_buf = pltpu.VMEM((SC_WINDOW,) + hp.shape[1:], hp.dtype)
    return pl.kernel(
        body,
        out_type=jax.ShapeDtypeStruct((n * TOP_K,) + hp.shape[1:], hp.dtype),
        mesh=_sc_mesh(),
        scratch_types=[idx_buf, idx_buf, row_buf, row_buf,
                       pltpu.SemaphoreType.DMA, pltpu.SemaphoreType.DMA, pltpu.SemaphoreType.DMA],
        name="dispatch",
    )(hp, pos_win)


def _collect_call(pos_win, y_sorted):
    n = pos_win.shape[0] * SC_WINDOW
    n_workers = SC_CORES * SC_SUBCORES
    wins_per_worker = n // SC_WINDOW // n_workers

    def body(ys_hbm, pos_hbm, out_hbm, idx_v, rows_a, rows_b, gather_sem, write_sem):
        first_win = _sc_worker() * wins_per_worker
        bufs = (rows_a, rows_b)

        @pl.loop(0, wins_per_worker)
        def _(w):
            win = first_win + w
            pltpu.sync_copy(pos_hbm.at[win], idx_v)

            def gather(k):
                return pltpu.async_copy(ys_hbm.at[idx_v.at[k]], bufs[k % 2], gather_sem)

            pending_gather = gather(0)
            pending_write = None
            for k in range(TOP_K):
                pending_gather.wait()
                if pending_write is not None:
                    for cp in pending_write:
                        cp.wait()
                if k + 1 < TOP_K:
                    pending_gather = gather(k + 1)
                pending_write = [
                    pltpu.async_copy(bufs[k % 2].at[:, j],
                                     out_hbm.at[j, k, pl.ds(win * SC_WINDOW, SC_WINDOW)], write_sem)
                    for j in range(PACK_ROWS)]
            for cp in pending_write:
                cp.wait()

    row_buf = pltpu.VMEM((SC_WINDOW,) + y_sorted.shape[1:], y_sorted.dtype)
    return pl.kernel(
        body,
        out_type=jax.ShapeDtypeStruct((PACK_ROWS, TOP_K, n, LANES), y_sorted.dtype),
        mesh=_sc_mesh(),
        scratch_types=[pltpu.VMEM((TOP_K, SC_WINDOW), I32), row_buf, row_buf,
                       pltpu.SemaphoreType.DMA, pltpu.SemaphoreType.DMA],
        name="collect",
    )(y_sorted, pos_win)


def _experts_kernel(pe_ref, pb_ref, pv_ref, st_ref, en_ref, nx_ref,
                    xs_hbm, wg_hbm, wu_hbm, wd_hbm, y_hbm,
                    wgb_ref, wub_ref, wdb_ref, wgf_ref, wuf_ref, wdf_ref, xbuf_ref, stage_ref,
                    in_sems, out_sems, w_sems, state_ref):
    p = pl.program_id(0)
    n_pairs = pl.num_programs(0)
    e = pe_ref[p]
    blk = pb_ref[p]
    rb = xbuf_ref.shape[2]
    prev = jnp.maximum(p - 1, 0)
    nxt = jnp.minimum(p + 1, n_pairs - 1)
    live = pv_ref[p] == 1
    first = jnp.logical_or(p == 0, pb_ref[prev] != blk)
    block_ends = jnp.logical_or(pb_ref[nxt] != blk, pv_ref[nxt] == 0)
    last = jnp.logical_or(p == n_pairs - 1, block_ends)

    def in_copy(j, block, slot):
        return pltpu.make_async_copy(xs_hbm.at[pl.ds(block * rb, rb), j], xbuf_ref.at[slot, j],
                                     in_sems.at[slot])

    def out_copy(j, block, slot):
        return pltpu.make_async_copy(stage_ref.at[slot, j], y_hbm.at[pl.ds(block * rb, rb), j],
                                     out_sems.at[slot])

    def drain(slot):
        @pl.when(state_ref[1 + slot] == 1)
        def _():
            for j in range(PACK_ROWS):
                out_copy(j, 0, slot).wait()
            state_ref[1 + slot] = 0

    @pl.when(p == 0)
    def _():
        state_ref[0] = 0
        state_ref[1] = 0
        state_ref[2] = 0
        stage_ref[...] = jnp.zeros_like(stage_ref)
        for j in range(PACK_ROWS):
            in_copy(j, blk, 0).start()

    slot = jnp.where(jnp.logical_and(first, p > 0), 1 - state_ref[0], state_ref[0])
    state_ref[0] = slot

    @pl.when(jnp.logical_and(first, blk + 1 < y_hbm.shape[0] // rb))
    def _():
        for j in range(PACK_ROWS):
            in_copy(j, blk + 1, 1 - slot).start()

    @pl.when(first)
    def _():
        for j in range(PACK_ROWS):
            in_copy(j, blk, slot).wait()
        drain(slot)

    def w_copies(expert, wslot):
        return (pltpu.make_async_copy(wg_hbm.at[expert], wgf_ref.at[wslot], w_sems.at[wslot]),
                pltpu.make_async_copy(wu_hbm.at[expert], wuf_ref.at[wslot], w_sems.at[wslot]),
                pltpu.make_async_copy(wd_hbm.at[expert], wdf_ref.at[wslot], w_sems.at[wslot]))

    @pl.when(p == 0)
    def _():
        state_ref[3] = 0
        for cp in w_copies(e, 0):
            cp.start()

    new_expert = jnp.logical_or(p == 0, pe_ref[prev] != e)
    wslot = jnp.where(jnp.logical_and(new_expert, p > 0), 1 - state_ref[3], state_ref[3])
    state_ref[3] = wslot

    @pl.when(new_expert)
    def _():
        for cp in w_copies(e, wslot):
            cp.wait()
        wgb_ref[...] = wgf_ref[wslot].astype(BF16)
        wub_ref[...] = wuf_ref[wslot].astype(BF16)
        wdb_ref[...] = wdf_ref[wslot].astype(BF16)

        @pl.when(nx_ref[e] != e)
        def _():
            for cp in w_copies(nx_ref[e], 1 - wslot):
                cp.start()

    lo_row = st_ref[e]
    hi_row = en_ref[e]

    def sub_block(s, row0, shared):
        sub = pl.ds(s * SUB_BLOCK, SUB_BLOCK)
        pieces = []
        for j in range(PACK_ROWS):
            w = xbuf_ref[slot, j, sub, :]
            pieces.append(lax.bitcast_convert_type(w.astype(jnp.int16), BF16))
            pieces.append(lax.bitcast_convert_type(
                lax.shift_right_logical(w, 16).astype(jnp.int16), BF16))
        xb = jnp.concatenate(pieces, axis=1)
        gate = jnp.dot(xb, wgb_ref[...], preferred_element_type=F32)
        up = jnp.dot(xb, wub_ref[...], preferred_element_type=F32)
        yb = jnp.dot((_silu(gate) * up).astype(BF16), wdb_ref[...], preferred_element_type=F32)
        if shared:
            rows = row0 + lax.broadcasted_iota(I32, (SUB_BLOCK, 1), 0)
            mine = jnp.logical_and(rows >= lo_row, rows < hi_row)
        for j in range(PACK_ROWS):
            word = pltpu.pack_elementwise(
                [yb[:, 2 * j * LANES:(2 * j + 1) * LANES], yb[:, (2 * j + 1) * LANES:(2 * j + 2) * LANES]],
                packed_dtype=BF16)
            if shared:
                word = jnp.where(mine, word, stage_ref[slot, j, sub, :])
            stage_ref[slot, j, sub, :] = word

    def one(s):
        row0 = blk * rb + s * SUB_BLOCK
        touched = jnp.logical_and(live, jnp.logical_and(row0 < hi_row, row0 + SUB_BLOCK > lo_row))
        whole = jnp.logical_and(lo_row <= row0, hi_row >= row0 + SUB_BLOCK)

        @pl.when(jnp.logical_and(touched, whole))
        def _():
            sub_block(s, row0, shared=False)

        @pl.when(jnp.logical_and(touched, jnp.logical_not(whole)))
        def _():
            sub_block(s, row0, shared=True)

    for s in range(0, rb // SUB_BLOCK, 2):
        row0 = blk * rb + s * SUB_BLOCK
        both = jnp.logical_and(live, jnp.logical_and(lo_row <= row0, hi_row >= row0 + 2 * SUB_BLOCK))

        @pl.when(both)
        def _():
            sub_block(s, row0, shared=False)
            sub_block(s + 1, row0 + SUB_BLOCK, shared=False)

        @pl.when(jnp.logical_not(both))
        def _():
            one(s)
            one(s + 1)

    @pl.when(jnp.logical_and(live, last))
    def _():
        for j in range(PACK_ROWS):
            out_copy(j, blk, slot).start()
        state_ref[1 + slot] = 1

    @pl.when(p == n_pairs - 1)
    def _():
        drain(0)
        drain(1)


def _experts_call(pair_e, pair_blk, pair_ok, starts, ends, next_e, xs, w_gate, w_up, w_down):
    ne, d, de = w_gate.shape
    rb = ROW_BLOCK
    n_pairs = pair_e.shape[0]
    anywhere = pl.BlockSpec(memory_space=pl.ANY)
    grid_spec = pltpu.PrefetchScalarGridSpec(
        num_scalar_prefetch=6,
        grid=(n_pairs,),
        in_specs=[anywhere, anywhere, anywhere, anywhere],
        out_specs=anywhere,
        scratch_shapes=[pltpu.VMEM((d, de), BF16), pltpu.VMEM((d, de), BF16),
                        pltpu.VMEM((de, d), BF16),
                        pltpu.VMEM((2, d, de), F32), pltpu.VMEM((2, d, de), F32),
                        pltpu.VMEM((2, de, d), F32),
                        pltpu.VMEM((2, PACK_ROWS, rb, LANES), I32),
                        pltpu.VMEM((2, PACK_ROWS, rb, LANES), I32),
                        pltpu.SemaphoreType.DMA((2,)), pltpu.SemaphoreType.DMA((2,)),
                        pltpu.SemaphoreType.DMA((2,)),
                        pltpu.SMEM((4,), I32)],
    )
    return pl.pallas_call(
        _experts_kernel,
        out_shape=jax.ShapeDtypeStruct(xs.shape, xs.dtype),
        grid_spec=grid_spec,
        compiler_params=pltpu.CompilerParams(
            dimension_semantics=("arbitrary",), vmem_limit_bytes=VMEM_LIMIT),
        name="experts",
    )(pair_e, pair_blk, pair_ok, starts, ends, next_e, xs, w_gate, w_up, w_down)


def _combine_kernel(y0_ref, y1_ref, y2_ref, y3_ref, x1_ref, hp_ref, w_ref, g2_ref, fg_ref,
                    wsg_ref, wsu_ref, wsd_ref, o_ref):
    tm = x1_ref.shape[0]
    w = w_ref[...].T
    parts = []
    for y_ref in (y0_ref, y1_ref, y2_ref, y3_ref):
        for half in range(2):
            acc = None
            for k in range(TOP_K):
                piece = pltpu.unpack_elementwise(y_ref[k], index=half, packed_dtype=BF16,
                                                 unpacked_dtype=F32) * w[:, k:k + 1]
                acc = piece if acc is None else acc + piece
            parts.append(acc)
    routed = jnp.concatenate(parts, axis=1)
    pieces = []
    for j in range(PACK_ROWS):
        word = hp_ref[pl.ds(j, tm, stride=PACK_ROWS), :]
        pieces.append(lax.bitcast_convert_type(word.astype(jnp.int16), BF16))
        pieces.append(lax.bitcast_convert_type(lax.shift_right_logical(word, 16).astype(jnp.int16), BF16))
    h2 = jnp.concatenate(pieces, axis=1)
    hid = _silu(jnp.dot(h2, wsg_ref[...], preferred_element_type=F32)) * jnp.dot(
        h2, wsu_ref[...], preferred_element_type=F32)
    shared = jnp.dot(hid.astype(BF16), wsd_ref[...], preferred_element_type=F32)
    x2 = x1_ref[...] + g2_ref[0] * (routed + shared)
    ms = jnp.mean(x2 * x2, axis=-1, keepdims=True)
    o_ref[...] = x2 * lax.rsqrt(ms + EPS) * fg_ref[...]


def _combine_kernel_into(*refs):
    _combine_kernel(*refs[:-2], refs[-1])


def _combine_call(y_tok, x1, hp, w_tok, g2, fgain, wsg, wsu, wsd, seq, part, prev_out):
    n, d = x1.shape
    dsh = wsg.shape[1]
    tm = MOVE_TILE
    per_b = seq // tm
    tiles = y_tok.shape[2] // tm
    t0 = part * tiles
    const = lambda i: (0, 0)

    def piece_spec(j):
        return pl.BlockSpec((None, TOP_K, tm, LANES), lambda i: (j, 0, i, 0))

    in_specs = [piece_spec(j) for j in range(PACK_ROWS)] + [
        pl.BlockSpec((tm, d), lambda i: (t0 + i, 0)),
        pl.BlockSpec((tm * PACK_ROWS, LANES), lambda i: (t0 + i, 0)),
        pl.BlockSpec((TOP_K, tm), lambda i: (0, t0 + i)),
        pl.BlockSpec((1, 1, d), lambda i: ((t0 + i) // per_b, 0, 0)),
        pl.BlockSpec((1, d), const),
        pl.BlockSpec((d, dsh), const),
        pl.BlockSpec((d, dsh), const),
        pl.BlockSpec((dsh, d), const)]
    args = [y_tok, y_tok, y_tok, y_tok, x1, hp, w_tok, g2, fgain, wsg, wsu, wsd]
    aliases = {}
    body = _combine_kernel
    if prev_out is not None:
        in_specs.append(pl.BlockSpec(memory_space=pl.ANY))
        args.append(prev_out)
        aliases = {len(args) - 1: 0}
        body = _combine_kernel_into
    return pl.pallas_call(
        body,
        out_shape=jax.ShapeDtypeStruct((n, d), F32),
        grid=(tiles,),
        in_specs=in_specs,
        out_specs=pl.BlockSpec((tm, d), lambda i: (t0 + i, 0)),
        input_output_aliases=aliases,
        compiler_params=pltpu.CompilerParams(
            dimension_semantics=("arbitrary",), vmem_limit_bytes=VMEM_LIMIT),
        name="combine",
    )(*args)


def _pair_tables(counts, n_rows):
    ne = counts.shape[0]
    sizes = counts.astype(I32)
    ends = jnp.cumsum(sizes)
    starts = ends - sizes
    first_blk = starts // ROW_BLOCK
    last_blk = (ends - 1) // ROW_BLOCK
    n_pairs = jnp.where(sizes > 0, last_blk - first_blk + 1, 0)
    pair_end = jnp.cumsum(n_pairs)
    pair_start = pair_end - n_pairs
    max_pairs = n_rows // ROW_BLOCK + ne
    p = jnp.arange(max_pairs, dtype=I32)
    ok = p < pair_end[-1]
    pc = jnp.minimum(p, pair_end[-1] - 1)
    pair_e = jnp.minimum(jnp.sum((pair_end[None, :] <= pc[:, None]).astype(I32), axis=1), ne - 1)
    pair_blk = (first_blk[pair_e] + pc - pair_start[pair_e]).astype(I32)
    eid = jnp.arange(ne, dtype=I32)
    later = lax.cummin(jnp.where(sizes > 0, eid, ne), reverse=True)
    next_e = jnp.concatenate([later[1:], jnp.full((1,), ne, I32)])
    next_e = jnp.where(next_e < ne, next_e, eid)
    return pair_e, pair_blk, ok.astype(I32), starts.astype(I32), ends.astype(I32), next_e


def kernel(x, c, w_ada, b_ada, w_in, lb_logits, hgrn_norm, lam_re, lam_im, log_dt, b_re, b_im,
           c_re, c_im, d_skip, w_glu, b_glu, w_out, w_router, router_bias, w_gate, w_up, w_down,
           ws_gate, ws_up, ws_down, final_gain):
    bsz, seq, d = x.shape
    n = bsz * seq
    fdim = HGRN_HEADS * HGRN_KDIM
    n_chunks = seq // CHUNK
    lb = jnp.cumsum(jax.nn.softmax(lb_logits.astype(F32), axis=0), axis=0)[0].reshape(1, fdim)

    mod = _mod_call(c, w_ada[0], b_ada[0]).reshape(bsz, 6, d)
    nb = w_in.shape[2] - 4 * fdim
    perm = jnp.concatenate([jnp.arange(0, nb, 2), jnp.arange(1, nb, 2)])
    w_in_b = w_in[0].astype(BF16)
    ut, out_a, ucm = _mix_front_call(x, mod, w_in_b[:, :4 * fdim], w_in_b[:, 4 * fdim:][:, perm].T, lb,
                                hgrn_norm[0].reshape(1, fdim))

    cb, lp2, p_tab, q_tab, a1, a2 = _s5_tables(lam_re[0], lam_im[0], log_dt[0], b_re[0], b_im[0],
                                             c_re[0], c_im[0])
    yt = _s5_call(ut, cb, lp2, p_tab, q_tab, a1, a2, n_chunks, bsz)

    wr_t = w_router[0].T
    wr_hi = wr_t.astype(BF16)
    wr_lo = (wr_t - wr_hi.astype(F32)).astype(BF16)
    w_out_b = w_out[0].astype(BF16)
    w_out_p = jnp.concatenate([w_out_b[:fdim], w_out_b[fdim:][perm]], axis=0)
    x1, hp, logits_t = _mix_back_call(
        x.reshape(n, d), out_a.reshape(n, fdim), yt, ucm, mod,
        d_skip[0][perm].reshape(nb, 1), w_glu[0][perm][:, perm].T.astype(BF16),
        b_glu[0][perm].reshape(nb, 1), w_out_p, wr_hi, wr_lo, seq)

    top_idx, top_w, rank, counts = _route_call(logits_t, router_bias[0])
    pair_e, pair_blk, pair_ok, starts, ends, next_e = _pair_tables(counts[:, 0], n * TOP_K)
    pos = _pos_call(top_idx, rank, starts)

    pos_win = pos.reshape(TOP_K, n // SC_WINDOW, SC_WINDOW).transpose(1, 0, 2)
    xs = _dispatch_call(pos_win, hp.reshape(n, PACK_ROWS, LANES))
    y_sorted = _experts_call(pair_e, pair_blk, pair_ok, starts, ends, next_e, xs,
                             w_gate[0], w_up[0], w_down[0])
    w_tok = top_w
    g2 = mod[:, 5:6, :]
    fgain = final_gain.reshape(1, d)
    wsg, wsu, wsd = ws_gate[0].astype(BF16), ws_up[0].astype(BF16), ws_down[0].astype(BF16)
    wins = pos_win.shape[0] // TAIL_PARTS
    out = None
    for part in range(TAIL_PARTS):
        y_tok = _collect_call(pos_win[part * wins:(part + 1) * wins], y_sorted)
        out = _combine_call(y_tok, x1, hp, w_tok, g2, fgain, wsg, wsu, wsd, seq, part, out)
    return out.reshape(bsz, seq, d)
```

```python
import functools

import jax
import jax.numpy as jnp
from jax import lax
from jax.experimental import pallas as pl
from jax.experimental.pallas import tpu as pltpu
from jax.experimental.pallas import tpu_sc as plsc

F32 = jnp.float32
BF16 = jnp.bfloat16
I32 = jnp.int32

EPS = 1e-6
CHUNK = 64
HGRN_HEADS = 4
HGRN_KDIM = 128
S5_GROUP = 16
S5_STATE = 64
N_EXPERT_GROUPS = 8
TOPK_GROUPS = 4
TOP_K = 8
ROUTE_SCALE = 2.5
LANES = 128
PACK_ROWS = 4
SC_CORES = 2
SC_SUBCORES = 16
SC_LANES = 16
SC_WINDOW = 64

SEQ_TILE = 512
TOK_TILE = 512
ROUTE_TILE = 512
MOVE_TILE = 512
ROW_BLOCK = 2048
SUB_BLOCK = 512
TAIL_PARTS = 4
VMEM_LIMIT = 56 * 1024 * 1024

_NT = (((1,), (1,)), ((), ()))
_TN = (((0,), (0,)), ((), ()))


def _sigmoid(v):
    return 0.5 * jnp.tanh(0.5 * v) + 0.5


def _silu(v):
    return v * _sigmoid(v)


def _bdot(a, b):
    return jnp.dot(a.astype(BF16), b.astype(BF16), preferred_element_type=F32)


def _store_packed(ref, val, n_rows):
    for j in range(PACK_ROWS):
        lo = val[:, 2 * j * LANES:(2 * j + 1) * LANES]
        hi = val[:, (2 * j + 1) * LANES:(2 * j + 2) * LANES]
        ref[pl.ds(j, n_rows, stride=PACK_ROWS), :] = pltpu.pack_elementwise([lo, hi], packed_dtype=BF16)


def _mod_kernel(c_ref, w_ref, b_ref, o_ref):
    o_ref[...] = _bdot(_silu(c_ref[...]), w_ref[...]) + b_ref[...]


def _mod_call(c, w_ada, b_ada):
    bsz, d = c.shape
    n_out = w_ada.shape[1]
    return pl.pallas_call(
        _mod_kernel,
        out_shape=jax.ShapeDtypeStruct((bsz, n_out), F32),
        grid=(n_out // d,),
        in_specs=[pl.BlockSpec((bsz, d), lambda j: (0, 0)),
                  pl.BlockSpec((d, d), lambda j: (0, j)),
                  pl.BlockSpec((1, d), lambda j: (0, j))],
        out_specs=pl.BlockSpec((bsz, d), lambda j: (0, j)),
        compiler_params=pltpu.CompilerParams(vmem_limit_bytes=VMEM_LIMIT),
        name="mod",
    )(c, w_ada, b_ada.reshape(1, n_out))


def _split_chunk_pairs(tile_even, tile_odd):
    low = lax.broadcasted_iota(I32, tile_even.shape, 1) < CHUNK
    first = jnp.where(low, tile_even, pltpu.roll(tile_odd, CHUNK, 1))
    second = jnp.where(low, pltpu.roll(tile_even, CHUNK, 1), tile_odd)
    return first, second


def _mix_front_kernel(x_ref, mod_ref, win_ref, wut_ref, lb_ref, gn_ref, ltri_ref,
                      ut_ref, oa_ref, ucm_ref, proj_ref, st_ref, flat_ref):
    fdim = HGRN_HEADS * HGRN_KDIM
    ts = x_ref.shape[1]
    pairs = wut_ref.shape[0] // 2
    rows = ts // CHUNK

    @pl.when(pl.program_id(1) == 0)
    def _():
        st_ref[...] = jnp.zeros_like(st_ref)

    x = x_ref[0]
    ms = jnp.mean(x * x, axis=-1, keepdims=True)
    h = x * lax.rsqrt(ms + EPS) * (1.0 + mod_ref[0, 1:2, :]) + mod_ref[0, 0:1, :]
    hb = h.astype(BF16)
    proj_ref[...] = jnp.dot(hb, win_ref[...], preferred_element_type=F32)
    u_t = lax.dot_general(wut_ref[...], hb, _NT, preferred_element_type=F32)
    ucm_ref[...] = u_t.astype(BF16)
    for m in range(ts // LANES):
        first, second = _split_chunk_pairs(u_t[:pairs, m * LANES:(m + 1) * LANES],
                                           u_t[pairs:, m * LANES:(m + 1) * LANES])
        flat_ref[2 * m * pairs:(2 * m + 1) * pairs, :] = first
        flat_ref[(2 * m + 1) * pairs:(2 * m + 2) * pairs, :] = second
    per_group = S5_GROUP // 2
    for q in range(pairs):
        ut_ref[q // per_group, :, (q % per_group) * LANES:(q % per_group + 1) * LANES] = (
            flat_ref[pl.ds(q, rows, stride=pairs), :])

    lb = lb_ref[...]
    gn = gn_ref[...]
    ltri = ltri_ref[...]
    row = lax.broadcasted_iota(I32, (CHUNK, CHUNK), 0)
    col = lax.broadcasted_iota(I32, (CHUNK, CHUNK), 1)
    causal = row >= col

    def chunk_step(ci, carry):
        r0 = pl.multiple_of(ci * CHUNK, CHUNK)
        q = proj_ref[pl.ds(r0, CHUNK), 0:fdim]
        fl = proj_ref[pl.ds(r0, CHUNK), fdim:2 * fdim]
        iv = proj_ref[pl.ds(r0, CHUNK), 2 * fdim:3 * fdim]
        og = proj_ref[pl.ds(r0, CHUNK), 3 * fdim:4 * fdim]
        f = lb + (1.0 - lb) * _sigmoid(fl)
        lf = jnp.log(f)
        lf_hi = lf.astype(BF16)
        rem = lf - lf_hi.astype(F32)
        lf_mid = rem.astype(BF16)
        lf_lo = (rem - lf_mid.astype(F32)).astype(BF16)
        b = (jnp.dot(ltri, lf_hi, preferred_element_type=F32)
             + jnp.dot(ltri, lf_mid, preferred_element_type=F32)
             + jnp.dot(ltri, lf_lo, preferred_element_type=F32))
        b_ref = b[CHUNK // 2 - 1:CHUNK // 2, :]
        b_last = b[CHUNK - 1:CHUNK, :]
        qs = _silu(q)
        kk = 1.0 - f
        qe = (qs * jnp.exp(b - b_ref)).astype(BF16)
        ke = (kk * jnp.exp(b_ref - b)).astype(BF16)
        qb = (qs * jnp.exp(b)).astype(BF16)
        k2 = (kk * jnp.exp(b_last - b)).astype(BF16)
        dec = jnp.exp(b_last)
        ivb = iv.astype(BF16)
        outs = []
        for hh in range(HGRN_HEADS):
            sl = slice(hh * HGRN_KDIM, (hh + 1) * HGRN_KDIM)
            att = lax.dot_general(qe[:, sl], ke[:, sl], _NT, preferred_element_type=F32)
            att = jnp.where(causal, att, 0.0)
            st = st_ref[hh]
            o = jnp.dot(att.astype(BF16), ivb[:, sl], preferred_element_type=F32)
            o = o + lax.dot_general(qb[:, sl], st.astype(BF16), _NT, preferred_element_type=F32)
            st_ref[hh] = st * dec[:, sl] + lax.dot_general(
                ivb[:, sl], k2[:, sl], _TN, preferred_element_type=F32)
            outs.append(o * lax.rsqrt(jnp.mean(o * o, axis=-1, keepdims=True) + EPS))
        o = jnp.concatenate(outs, axis=1) * gn * _silu(og)
        oa_ref[0, pl.ds(r0, CHUNK), :] = o.astype(BF16)
        return carry

    lax.fori_loop(0, ts // CHUNK, chunk_step, 0, unroll=True)


def _mix_front_call(x, mod, w_main, w_ut, lb, gn):
    bsz, seq, d = x.shape
    fdim = HGRN_HEADS * HGRN_KDIM
    ncols = w_main.shape[1]
    nb = w_ut.shape[0]
    groups = nb // S5_GROUP
    ltri = jnp.tril(jnp.ones((CHUNK, CHUNK), BF16))
    ts = SEQ_TILE
    tiles = seq // ts
    rows = ts // CHUNK
    return pl.pallas_call(
        _mix_front_kernel,
        out_shape=(jax.ShapeDtypeStruct((groups, bsz * seq // CHUNK, S5_GROUP * CHUNK), F32),
                   jax.ShapeDtypeStruct((bsz, seq, fdim), BF16),
                   jax.ShapeDtypeStruct((nb, bsz * seq), BF16)),
        grid=(bsz, tiles),
        in_specs=[pl.BlockSpec((1, ts, d), lambda b, j: (b, j, 0)),
                  pl.BlockSpec((1, 6, d), lambda b, j: (b, 0, 0)),
                  pl.BlockSpec((d, ncols), lambda b, j: (0, 0)),
                  pl.BlockSpec((nb, d), lambda b, j: (0, 0)),
                  pl.BlockSpec((1, fdim), lambda b, j: (0, 0)),
                  pl.BlockSpec((1, fdim), lambda b, j: (0, 0)),
                  pl.BlockSpec((CHUNK, CHUNK), lambda b, j: (0, 0))],
        out_specs=(pl.BlockSpec((groups, rows, S5_GROUP * CHUNK), lambda b, j: (0, b * tiles + j, 0)),
                   pl.BlockSpec((1, ts, fdim), lambda b, j: (b, j, 0)),
                   pl.BlockSpec((nb, ts), lambda b, j: (0, b * tiles + j))),
        scratch_shapes=[pltpu.VMEM((ts, ncols), F32),
                        pltpu.VMEM((HGRN_HEADS, fdim // HGRN_HEADS, HGRN_KDIM), F32),
                        pltpu.VMEM((rows * nb // 2, LANES), F32)],
        compiler_params=pltpu.CompilerParams(
            dimension_semantics=("arbitrary", "arbitrary"), vmem_limit_bytes=VMEM_LIMIT),
        name="mix_front",
    )(x, mod, w_main, w_ut, lb, gn, ltri)


def _s5_tables(lam_re, lam_im, log_dt, b_re, b_im, c_re, c_im):
    t = CHUNK
    lam = lax.complex(jnp.minimum(lam_re, -1e-4), lam_im)
    lam_dt = lam * jnp.exp(log_dt)[:, None]
    lam_bar = jnp.exp(lam_dt)
    b_bar = ((lam_bar - 1.0) / lam)[..., None] * lax.complex(b_re, b_im)
    c_mat = lax.complex(c_re, c_im)
    taus = jnp.arange(t + 1, dtype=F32)
    lam_pow = jnp.exp(lam_dt[:, None, :] * taus[None, :, None])
    g, p = lam.shape
    c = b_re.shape[-1]
    cb = c_mat[:, None, :, :] * b_bar.transpose(0, 2, 1)[:, :, None, :]
    cb = jnp.concatenate([cb.real, -cb.imag], axis=-1)
    cb = cb.reshape(g, c, c // 2, 2 * 2 * p).reshape(g, c * c // 2, 4 * p)
    lp = jnp.concatenate([lam_pow[:, :t].real, lam_pow[:, :t].imag], axis=-1).transpose(0, 2, 1)
    zero = jnp.zeros_like(lp)
    lp2 = jnp.concatenate([jnp.concatenate([lp, zero], axis=2),
                           jnp.concatenate([zero, lp], axis=2)], axis=1)
    pc = lam_pow[:, t - 1::-1][:, :t, :, None] * b_bar[:, None, :, :]
    pc = pc.transpose(0, 3, 1, 2).reshape(g, c * t, p)
    p_tab = jnp.concatenate([pc.real, pc.imag], axis=-1)
    ql = c_mat[:, None, :, :] * lam_pow[:, 1:t + 1, None, :]
    ql = ql.transpose(0, 3, 2, 1).reshape(g, p, c * t)
    q_tab = jnp.concatenate([ql.real, -ql.imag], axis=1)
    lam_t = lam_pow[:, t]
    a1 = jnp.concatenate([lam_t.real, lam_t.real], axis=-1)[:, None, :]
    a2 = jnp.concatenate([-lam_t.imag, lam_t.imag], axis=-1)[:, None, :]
    return cb, lp2, p_tab.astype(BF16), q_tab.astype(BF16), a1, a2


def _split3(v):
    hi = v.astype(BF16)
    rem = v - hi.astype(F32)
    mid = rem.astype(BF16)
    return hi, mid, (rem - mid.astype(F32)).astype(BF16)


def _s5_kernel(u_ref, cb_ref, lp_ref, p_ref, q_ref, a1_ref, a2_ref, y_ref, v_ref, xs_ref, m_ref, k_ref,
               *, n_chunks, n_batch):
    c_hi, c_mid, c_lo = _split3(cb_ref[0])
    l_hi, l_mid, l_lo = _split3(lp_ref[0])
    k_ref[...] = (jnp.dot(c_hi, l_hi, preferred_element_type=F32)
                  + jnp.dot(c_hi, l_mid, preferred_element_type=F32)
                  + jnp.dot(c_mid, l_hi, preferred_element_type=F32)
                  + jnp.dot(c_hi, l_lo, preferred_element_type=F32)
                  + jnp.dot(c_mid, l_mid, preferred_element_type=F32)
                  + jnp.dot(c_lo, l_hi, preferred_element_type=F32))
    n_pairs = m_ref.shape[1] // LANES
    n_in = k_ref.shape[0] // n_pairs
    lane = lax.broadcasted_iota(I32, (CHUNK, LANES), 1)
    causal = (lane & (CHUNK - 1)) >= lax.broadcasted_iota(I32, (CHUNK, LANES), 0)
    for ci in range(n_in):
        for a in range(n_pairs):
            lags = jnp.broadcast_to(k_ref[ci * n_pairs + a:ci * n_pairs + a + 1, :], (CHUNK, LANES))
            tile = pltpu.roll(lags, 0, 1, stride=1, stride_axis=0)
            m_ref[ci * CHUNK:(ci + 1) * CHUNK, a * LANES:(a + 1) * LANES] = jnp.where(
                causal, tile, 0.0).astype(BF16)
    u = u_ref[0].astype(BF16)
    v_ref[...] = jnp.dot(u, p_ref[0], preferred_element_type=F32)
    a1 = a1_ref[0]
    a2 = a2_ref[0]
    half = xs_ref.shape[1] // 2

    def step(n, state):
        xs_ref[pl.ds(n, n_batch, stride=n_chunks), :] = state
        return (a1 * state + a2 * pltpu.roll(state, half, 1)
                + v_ref[pl.ds(n, n_batch, stride=n_chunks), :])

    lax.fori_loop(0, n_chunks, step, jnp.zeros((n_batch, xs_ref.shape[1]), F32))
    y = jnp.dot(u, m_ref[...], preferred_element_type=F32)
    y_ref[0] = y + jnp.dot(xs_ref[...].astype(BF16), q_ref[0], preferred_element_type=F32)


def _s5_call(ut, cb, lp2, p_tab, q_tab, a1, a2, n_chunks, n_batch):
    g, rows, width = ut.shape
    p2 = p_tab.shape[-1]
    return pl.pallas_call(
        functools.partial(_s5_kernel, n_chunks=n_chunks, n_batch=n_batch),
        out_shape=jax.ShapeDtypeStruct((g, rows, width), F32),
        grid=(g,),
        in_specs=[pl.BlockSpec((1, rows, width), lambda i: (i, 0, 0)),
                  pl.BlockSpec((1,) + cb.shape[1:], lambda i: (i, 0, 0)),
                  pl.BlockSpec((1,) + lp2.shape[1:], lambda i: (i, 0, 0)),
                  pl.BlockSpec((1, width, p2), lambda i: (i, 0, 0)),
                  pl.BlockSpec((1, p2, width), lambda i: (i, 0, 0)),
                  pl.BlockSpec((1, 1, p2), lambda i: (i, 0, 0)),
                  pl.BlockSpec((1, 1, p2), lambda i: (i, 0, 0))],
        out_specs=pl.BlockSpec((1, rows, width), lambda i: (i, 0, 0)),
        scratch_shapes=[pltpu.VMEM((rows, p2), F32), pltpu.VMEM((rows, p2), F32),
                        pltpu.VMEM((width, width), BF16),
                        pltpu.VMEM((cb.shape[1], lp2.shape[2]), F32)],
        compiler_params=pltpu.CompilerParams(
            dimension_semantics=("arbitrary",), vmem_limit_bytes=VMEM_LIMIT),
        name="s5",
    )(ut, cb, lp2, p_tab, q_tab, a1, a2)


def _token_major(flat_ref, src_ref):
    groups, rows, _ = src_ref.shape
    per_group = S5_GROUP // 2
    pairs = groups * per_group
    for q in range(pairs):
        flat_ref[pl.ds(q, rows, stride=pairs), :] = (
            src_ref[q // per_group, :, (q % per_group) * LANES:(q % per_group + 1) * LANES])
    tiles = []
    for m in range(rows // 2):
        even, odd = _split_chunk_pairs(flat_ref[2 * m * pairs:(2 * m + 1) * pairs, :],
                                       flat_ref[(2 * m + 1) * pairs:(2 * m + 2) * pairs, :])
        tiles.append(jnp.concatenate([even, odd], axis=0))
    return jnp.concatenate(tiles, axis=1)


def _mix_back_kernel(x_ref, oa_ref, yt_ref, ucm_ref, mod_ref, dskip_ref, wglu_ref, bglu_ref,
                     wout_ref, wrh_ref, wrl_ref,
                     x1_ref, hp_ref, lt_ref, flat_ref):
    na = oa_ref.shape[1]
    y_t = _token_major(flat_ref, yt_ref)
    u_t = ucm_ref[...].astype(F32)
    z_t = jax.nn.gelu(y_t + dskip_ref[...] * u_t)
    gate_t = _sigmoid(jnp.dot(wglu_ref[...], z_t.astype(BF16), preferred_element_type=F32)
                      + bglu_ref[...])
    ob_t = (z_t * gate_t).astype(BF16)
    mixed = (jnp.dot(oa_ref[...], wout_ref[0:na, :], preferred_element_type=F32)
             + lax.dot_general(ob_t, wout_ref[na:, :], _TN, preferred_element_type=F32))
    x1 = x_ref[...] + mod_ref[0, 2:3, :] * mixed
    ms = jnp.mean(x1 * x1, axis=-1, keepdims=True)
    h2 = x1 * lax.rsqrt(ms + EPS) * (1.0 + mod_ref[0, 4:5, :]) + mod_ref[0, 3:4, :]
    _store_packed(hp_ref, h2, h2.shape[0])
    h_hi = h2.astype(BF16)
    h_lo = (h2 - h_hi.astype(F32)).astype(BF16)
    lt = lax.dot_general(wrh_ref[...], h_hi, _NT, preferred_element_type=F32)
    lt = lt + lax.dot_general(wrl_ref[...], h_hi, _NT, preferred_element_type=F32)
    lt = lt + lax.dot_general(wrh_ref[...], h_lo, _NT, preferred_element_type=F32)
    lt_ref[...] = lt
    x1_ref[...] = x1


def _mix_back_call(x2d, oa, yt, ucm, mod, dskip, wglu, bglu, wout, wrh, wrl, seq):
    n, d = x2d.shape
    nb = oa.shape[1]
    ne = wrh.shape[0]
    tm = TOK_TILE
    per_b = seq // tm
    groups, _, width = yt.shape
    rows = tm // CHUNK
    const = lambda i: (0, 0)
    flat_block = pl.BlockSpec((groups, rows, width), lambda i: (0, i, 0))
    return pl.pallas_call(
        _mix_back_kernel,
        out_shape=(jax.ShapeDtypeStruct((n, d), F32),
                   jax.ShapeDtypeStruct((n * PACK_ROWS, LANES), I32),
                   jax.ShapeDtypeStruct((ne, n), F32)),
        grid=(n // tm,),
        in_specs=[pl.BlockSpec((tm, d), lambda i: (i, 0)),
                  pl.BlockSpec((tm, nb), lambda i: (i, 0)),
                  flat_block,
                  pl.BlockSpec((nb, tm), lambda i: (0, i)),
                  pl.BlockSpec((1, 6, d), lambda i: (i // per_b, 0, 0)),
                  pl.BlockSpec((nb, 1), const),
                  pl.BlockSpec((nb, nb), const),
                  pl.BlockSpec((nb, 1), const),
                  pl.BlockSpec((d, d), const),
                  pl.BlockSpec((ne, d), const),
                  pl.BlockSpec((ne, d), const)],
        out_specs=(pl.BlockSpec((tm, d), lambda i: (i, 0)),
                   pl.BlockSpec((tm * PACK_ROWS, LANES), lambda i: (i, 0)),
                   pl.BlockSpec((ne, tm), lambda i: (0, i))),
        scratch_shapes=[pltpu.VMEM((rows * nb // 2, LANES), F32)],
        compiler_params=pltpu.CompilerParams(
            dimension_semantics=("arbitrary",), vmem_limit_bytes=VMEM_LIMIT),
        name="mix_back",
    )(x2d, oa, yt, ucm, mod, dskip, wglu, bglu, wout, wrh, wrl)


def _route_kernel(lt_ref, bias_ref, su_ref, idx_ref, w_ref, rank_ref, cnt_ref, run_ref):
    ne, tr = lt_ref.shape
    per_group = ne // N_EXPERT_GROUPS
    neg = -jnp.inf

    @pl.when(pl.program_id(0) == 0)
    def _():
        run_ref[...] = jnp.zeros_like(run_ref)

    s = _sigmoid(lt_ref[...])
    sel = s + bias_ref[...]
    gio = lax.broadcasted_iota(I32, (per_group, tr), 0)
    gscore = []
    for g in range(N_EXPERT_GROUPS):
        v = sel[g * per_group:(g + 1) * per_group, :]
        m1 = jnp.max(v, axis=0, keepdims=True)
        i1 = jnp.min(jnp.where(v == m1, gio, per_group), axis=0, keepdims=True)
        m2 = jnp.max(jnp.where(gio == i1, neg, v), axis=0, keepdims=True)
        gscore.append(m1 + m2)
    masked = []
    for g in range(N_EXPERT_GROUPS):
        ahead = jnp.zeros((1, tr), I32)
        for o in range(N_EXPERT_GROUPS):
            if o == g:
                continue
            wins = (gscore[o] >= gscore[g]) if o < g else (gscore[o] > gscore[g])
            ahead = ahead + wins.astype(I32)
        keep = ahead < TOPK_GROUPS
        masked.append(jnp.where(keep, sel[g * per_group:(g + 1) * per_group, :], neg))
    selm = jnp.concatenate(masked, axis=0)
    eio = lax.broadcasted_iota(I32, (ne, tr), 0)
    candidate = selm > neg
    idxs, ws = [], []
    for k in range(TOP_K):
        m = jnp.max(selm, axis=0, keepdims=True)
        ik = jnp.min(jnp.where(selm == m, eio, ne), axis=0, keepdims=True)
        onehot = eio == ik
        ws.append(jnp.sum(jnp.where(onehot, s, 0.0), axis=0, keepdims=True))
        selm = jnp.where(onehot, neg, selm)
        idxs.append(ik)
    hits = jnp.where(jnp.logical_and(candidate, selm == neg), 1.0, 0.0)
    wsum = ws[0]
    for k in range(1, TOP_K):
        wsum = wsum + ws[k]
    scale = ROUTE_SCALE / wsum
    ranks = jnp.dot(hits.astype(BF16), su_ref[...], preferred_element_type=F32) + run_ref[...]
    for k in range(TOP_K):
        idx_ref[k:k + 1, :] = idxs[k]
        w_ref[k:k + 1, :] = ws[k] * scale
        rk = jnp.sum(jnp.where(eio == idxs[k], ranks, 0.0), axis=0, keepdims=True)
        rank_ref[k:k + 1, :] = rk.astype(I32)
    run_ref[...] = run_ref[...] + jnp.sum(hits, axis=1, keepdims=True)
    cnt_ref[...] = run_ref[...]


def _route_call(lt, bias):
    ne, n = lt.shape
    tr = ROUTE_TILE
    su = jnp.triu(jnp.ones((tr, tr), F32), k=1).astype(BF16)
    return pl.pallas_call(
        _route_kernel,
        out_shape=(jax.ShapeDtypeStruct((TOP_K, n), I32),
                   jax.ShapeDtypeStruct((TOP_K, n), F32),
                   jax.ShapeDtypeStruct((TOP_K, n), I32),
                   jax.ShapeDtypeStruct((ne, 1), F32)),
        grid=(n // tr,),
        in_specs=[pl.BlockSpec((ne, tr), lambda i: (0, i)),
                  pl.BlockSpec((ne, 1), lambda i: (0, 0)),
                  pl.BlockSpec((tr, tr), lambda i: (0, 0))],
        out_specs=(pl.BlockSpec((TOP_K, tr), lambda i: (0, i)),
                   pl.BlockSpec((TOP_K, tr), lambda i: (0, i)),
                   pl.BlockSpec((TOP_K, tr), lambda i: (0, i)),
                   pl.BlockSpec((ne, 1), lambda i: (0, 0))),
        scratch_shapes=[pltpu.VMEM((ne, 1), F32)],
        compiler_params=pltpu.CompilerParams(
            dimension_semantics=("arbitrary",), vmem_limit_bytes=VMEM_LIMIT),
        name="route",
    )(lt, bias.reshape(ne, 1), su)


def _sc_mesh():
    return plsc.VectorSubcoreMesh(core_axis_name="c", subcore_axis_name="s",
                                  num_cores=SC_CORES, num_subcores=SC_SUBCORES)


def _sc_worker():
    return lax.axis_index("s") * SC_CORES + lax.axis_index("c")


def _dispatch_call(idx_win, rank_win, starts, hp):
    n = hp.shape[0]
    n_workers = SC_CORES * SC_SUBCORES
    wins_per_worker = n // SC_WINDOW // n_workers
    lanes = SC_LANES

    def body(hp_hbm, idx_hbm, rank_hbm, st_hbm, xs_hbm, pos_hbm,
             idx_a, idx_b, rank_a, rank_b, pos_a, pos_b, rows_a, rows_b, st_v,
             load_a, load_b, scatter_sem, pos_sem):
        first_win = _sc_worker() * wins_per_worker
        idx_v, rank_v, pos_v = (idx_a, idx_b), (rank_a, rank_b), (pos_a, pos_b)
        rows_v, load_sem = (rows_a, rows_b), (load_a, load_b)
        pltpu.sync_copy(st_hbm, st_v)

        def loads(win, slot):
            return (pltpu.make_async_copy(hp_hbm.at[pl.ds(win * SC_WINDOW, SC_WINDOW)], rows_v[slot],
                                          load_sem[slot]),
                    pltpu.make_async_copy(idx_hbm.at[win], idx_v[slot], load_sem[slot]),
                    pltpu.make_async_copy(rank_hbm.at[win], rank_v[slot], load_sem[slot]))

        for cp in loads(first_win, 0):
            cp.start()

        @pl.loop(0, wins_per_worker, step=2)
        def _(w):
            for slot in range(2):
                win = first_win + w + slot
                for cp in loads(win, slot):
                    cp.wait()

                @pl.when(w + slot + 1 < wins_per_worker)
                def _():
                    for cp in loads(win + 1, 1 - slot):
                        cp.start()

                for k in range(TOP_K):
                    for c in range(SC_WINDOW // lanes):
                        seg = pl.ds(c * lanes, lanes)
                        base = plsc.load_gather(st_v, [idx_v[slot][k, seg]])
                        pos_v[slot][k, seg] = base + rank_v[slot][k, seg]
                pos_out = pltpu.async_copy(pos_v[slot], pos_hbm.at[win], pos_sem)
                copies = [pltpu.async_copy(rows_v[slot], xs_hbm.at[pos_v[slot].at[k]], scatter_sem)
                          for k in range(TOP_K)]
                for cp in copies:
                    cp.wait()
                pos_out.wait()

    idx_buf = pltpu.VMEM((TOP_K, SC_WINDOW), I32)
    row_buf = pltpu.VMEM((SC_WINDOW,) + hp.shape[1:], hp.dtype)
    return pl.kernel(
        body,
        out_type=(jax.ShapeDtypeStruct((n * TOP_K,) + hp.shape[1:], hp.dtype),
                  jax.ShapeDtypeStruct(idx_win.shape, I32)),
        mesh=_sc_mesh(),
        scratch_types=[idx_buf, idx_buf, idx_buf, idx_buf, idx_buf, idx_buf, row_buf, row_buf,
                       pltpu.VMEM(starts.shape, I32),
                       pltpu.SemaphoreType.DMA, pltpu.SemaphoreType.DMA, pltpu.SemaphoreType.DMA,
                       pltpu.SemaphoreType.DMA],
        compiler_params=pltpu.CompilerParams(needs_layout_passes=False),
        name="dispatch",
    )(hp, idx_win, rank_win, starts)


def _collect_call(pos_win, y_sorted):
    n = pos_win.shape[0] * SC_WINDOW
    n_workers = SC_CORES * SC_SUBCORES
    wins_per_worker = n // SC_WINDOW // n_workers

    def body(ys_hbm, pos_hbm, out_hbm, idx_v, rows_a, rows_b, gather_sem, write_sem):
        first_win = _sc_worker() * wins_per_worker
        bufs = (rows_a, rows_b)

        @pl.loop(0, wins_per_worker)
        def _(w):
            win = first_win + w
            pltpu.sync_copy(pos_hbm.at[win], idx_v)

            def gather(k):
                return pltpu.async_copy(ys_hbm.at[idx_v.at[k]], bufs[k % 2], gather_sem)

            pending_gather = gather(0)
            pending_write = None
            for k in range(TOP_K):
                pending_gather.wait()
                if pending_write is not None:
                    for cp in pending_write:
                        cp.wait()
                if k + 1 < TOP_K:
                    pending_gather = gather(k + 1)
                pending_write = [
                    pltpu.async_copy(bufs[k % 2].at[:, j],
                                     out_hbm.at[j, k, pl.ds(win * SC_WINDOW, SC_WINDOW)], write_sem)
                    for j in range(PACK_ROWS)]
            for cp in pending_write:
                cp.wait()

    row_buf = pltpu.VMEM((SC_WINDOW,) + y_sorted.shape[1:], y_sorted.dtype)
    return pl.kernel(
        body,
        out_type=jax.ShapeDtypeStruct((PACK_ROWS, TOP_K, n, LANES), y_sorted.dtype),
        mesh=_sc_mesh(),
        scratch_types=[pltpu.VMEM((TOP_K, SC_WINDOW), I32), row_buf, row_buf,
                       pltpu.SemaphoreType.DMA, pltpu.SemaphoreType.DMA],
        name="collect",
    )(y_sorted, pos_win)


def _experts_kernel(pe_ref, pb_ref, pv_ref, st_ref, en_ref, nx_ref,
                    xs_hbm, wg_hbm, wu_hbm, wd_hbm, y_hbm,
                    wgb_ref, wub_ref, wdb_ref, wgf_ref, wuf_ref, wdf_ref, xbuf_ref, stage_ref,
                    in_sems, out_sems, w_sems, state_ref):
    p = pl.program_id(0)
    n_pairs = pl.num_programs(0)
    e = pe_ref[p]
    blk = pb_ref[p]
    rb = xbuf_ref.shape[2]
    prev = jnp.maximum(p - 1, 0)
    nxt = jnp.minimum(p + 1, n_pairs - 1)
    live = pv_ref[p] == 1
    first = jnp.logical_or(p == 0, pb_ref[prev] != blk)
    block_ends = jnp.logical_or(pb_ref[nxt] != blk, pv_ref[nxt] == 0)
    last = jnp.logical_or(p == n_pairs - 1, block_ends)

    def in_copy(j, block, slot):
        return pltpu.make_async_copy(xs_hbm.at[pl.ds(block * rb, rb), j], xbuf_ref.at[slot, j],
                                     in_sems.at[slot])

    def out_copy(j, block, slot):
        return pltpu.make_async_copy(stage_ref.at[slot, j], y_hbm.at[pl.ds(block * rb, rb), j],
                                     out_sems.at[slot])

    def drain(slot):
        @pl.when(state_ref[1 + slot] == 1)
        def _():
            for j in range(PACK_ROWS):
                out_copy(j, 0, slot).wait()
            state_ref[1 + slot] = 0

    @pl.when(p == 0)
    def _():
        state_ref[0] = 0
        state_ref[1] = 0
        state_ref[2] = 0
        stage_ref[...] = jnp.zeros_like(stage_ref)
        for j in range(PACK_ROWS):
            in_copy(j, blk, 0).start()

    slot = jnp.where(jnp.logical_and(first, p > 0), 1 - state_ref[0], state_ref[0])
    state_ref[0] = slot

    @pl.when(jnp.logical_and(first, blk + 1 < y_hbm.shape[0] // rb))
    def _():
        for j in range(PACK_ROWS):
            in_copy(j, blk + 1, 1 - slot).start()

    @pl.when(first)
    def _():
        for j in range(PACK_ROWS):
            in_copy(j, blk, slot).wait()
        drain(slot)

    def w_copies(expert, wslot):
        return (pltpu.make_async_copy(wg_hbm.at[expert], wgf_ref.at[wslot], w_sems.at[wslot]),
                pltpu.make_async_copy(wu_hbm.at[expert], wuf_ref.at[wslot], w_sems.at[wslot]),
                pltpu.make_async_copy(wd_hbm.at[expert], wdf_ref.at[wslot], w_sems.at[wslot]))

    @pl.when(p == 0)
    def _():
        state_ref[3] = 0
        for cp in w_copies(e, 0):
            cp.start()

    new_expert = jnp.logical_or(p == 0, pe_ref[prev] != e)
    wslot = jnp.where(jnp.logical_and(new_expert, p > 0), 1 - state_ref[3], state_ref[3])
    state_ref[3] = wslot

    @pl.when(new_expert)
    def _():
        for cp in w_copies(e, wslot):
            cp.wait()
        wgb_ref[...] = wgf_ref[wslot].astype(BF16)
        wub_ref[...] = wuf_ref[wslot].astype(BF16)
        wdb_ref[...] = wdf_ref[wslot].astype(BF16)

        @pl.when(nx_ref[e] != e)
        def _():
            for cp in w_copies(nx_ref[e], 1 - wslot):
                cp.start()

    lo_row = st_ref[e]
    hi_row = en_ref[e]

    def sub_block(s, row0, shared):
        sub = pl.ds(s * SUB_BLOCK, SUB_BLOCK)
        pieces = []
        for j in range(PACK_ROWS):
            w = xbuf_ref[slot, j, sub, :]
            pieces.append(lax.bitcast_convert_type(w.astype(jnp.int16), BF16))
            pieces.append(lax.bitcast_convert_type(
                lax.shift_right_logical(w, 16).astype(jnp.int16), BF16))
        xb = jnp.concatenate(pieces, axis=1)
        gate = jnp.dot(xb, wgb_ref[...], preferred_element_type=F32)
        up = jnp.dot(xb, wub_ref[...], preferred_element_type=F32)
        yb = jnp.dot((_silu(gate) * up).astype(BF16), wdb_ref[...], preferred_element_type=F32)
        if shared:
            rows = row0 + lax.broadcasted_iota(I32, (SUB_BLOCK, 1), 0)
            mine = jnp.logical_and(rows >= lo_row, rows < hi_row)
        for j in range(PACK_ROWS):
            word = pltpu.pack_elementwise(
                [yb[:, 2 * j * LANES:(2 * j + 1) * LANES], yb[:, (2 * j + 1) * LANES:(2 * j + 2) * LANES]],
                packed_dtype=BF16)
            if shared:
                word = jnp.where(mine, word, stage_ref[slot, j, sub, :])
            stage_ref[slot, j, sub, :] = word

    def one(s):
        row0 = blk * rb + s * SUB_BLOCK
        touched = jnp.logical_and(live, jnp.logical_and(row0 < hi_row, row0 + SUB_BLOCK > lo_row))
        whole = jnp.logical_and(lo_row <= row0, hi_row >= row0 + SUB_BLOCK)

        @pl.when(jnp.logical_and(touched, whole))
        def _():
            sub_block(s, row0, shared=False)

        @pl.when(jnp.logical_and(touched, jnp.logical_not(whole)))
        def _():
            sub_block(s, row0, shared=True)

    for s in range(0, rb // SUB_BLOCK, 2):
        row0 = blk * rb + s * SUB_BLOCK
        both = jnp.logical_and(live, jnp.logical_and(lo_row <= row0, hi_row >= row0 + 2 * SUB_BLOCK))

        @pl.when(both)
        def _():
            sub_block(s, row0, shared=False)
            sub_block(s + 1, row0 + SUB_BLOCK, shared=False)

        @pl.when(jnp.logical_not(both))
        def _():
            one(s)
            one(s + 1)

    @pl.when(jnp.logical_and(live, last))
    def _():
        for j in range(PACK_ROWS):
            out_copy(j, blk, slot).start()
        state_ref[1 + slot] = 1

    @pl.when(p == n_pairs - 1)
    def _():
        drain(0)
        drain(1)


def _experts_call(pair_e, pair_blk, pair_ok, starts, ends, next_e, xs, w_gate, w_up, w_down):
    ne, d, de = w_gate.shape
    rb = ROW_BLOCK
    n_pairs = pair_e.shape[0]
    anywhere = pl.BlockSpec(memory_space=pl.ANY)
    grid_spec = pltpu.PrefetchScalarGridSpec(
        num_scalar_prefetch=6,
        grid=(n_pairs,),
        in_specs=[anywhere, anywhere, anywhere, anywhere],
        out_specs=anywhere,
        scratch_shapes=[pltpu.VMEM((d, de), BF16), pltpu.VMEM((d, de), BF16),
                        pltpu.VMEM((de, d), BF16),
                        pltpu.VMEM((2, d, de), F32), pltpu.VMEM((2, d, de), F32),
                        pltpu.VMEM((2, de, d), F32),
                        pltpu.VMEM((2, PACK_ROWS, rb, LANES), I32),
                        pltpu.VMEM((2, PACK_ROWS, rb, LANES), I32),
                        pltpu.SemaphoreType.DMA((2,)), pltpu.SemaphoreType.DMA((2,)),
                        pltpu.SemaphoreType.DMA((2,)),
                        pltpu.SMEM((4,), I32)],
    )
    return pl.pallas_call(
        _experts_kernel,
        out_shape=jax.ShapeDtypeStruct(xs.shape, xs.dtype),
        grid_spec=grid_spec,
        compiler_params=pltpu.CompilerParams(
            dimension_semantics=("arbitrary",), vmem_limit_bytes=VMEM_LIMIT),
        name="experts",
    )(pair_e, pair_blk, pair_ok, starts, ends, next_e, xs, w_gate, w_up, w_down)


def _combine_kernel(y0_ref, y1_ref, y2_ref, y3_ref, x1_ref, hp_ref, w_ref, g2_ref, fg_ref,
                    wsg_ref, wsu_ref, wsd_ref, o_ref):
    tm = x1_ref.shape[0]
    w = w_ref[...].T
    parts = []
    for y_ref in (y0_ref, y1_ref, y2_ref, y3_ref):
        for half in range(2):
            acc = None
            for k in range(TOP_K):
                piece = pltpu.unpack_elementwise(y_ref[k], index=half, packed_dtype=BF16,
                                                 unpacked_dtype=F32) * w[:, k:k + 1]
                acc = piece if acc is None else acc + piece
            parts.append(acc)
    routed = jnp.concatenate(parts, axis=1)
    pieces = []
    for j in range(PACK_ROWS):
        word = hp_ref[pl.ds(j, tm, stride=PACK_ROWS), :]
        pieces.append(lax.bitcast_convert_type(word.astype(jnp.int16), BF16))
        pieces.append(lax.bitcast_convert_type(lax.shift_right_logical(word, 16).astype(jnp.int16), BF16))
    h2 = jnp.concatenate(pieces, axis=1)
    hid = _silu(jnp.dot(h2, wsg_ref[...], preferred_element_type=F32)) * jnp.dot(
        h2, wsu_ref[...], preferred_element_type=F32)
    shared = jnp.dot(hid.astype(BF16), wsd_ref[...], preferred_element_type=F32)
    x2 = x1_ref[...] + g2_ref[0] * (routed + shared)
    ms = jnp.mean(x2 * x2, axis=-1, keepdims=True)
    o_ref[...] = x2 * lax.rsqrt(ms + EPS) * fg_ref[...]


def _combine_kernel_into(*refs):
    _combine_kernel(*refs[:-2], refs[-1])


def _combine_call(y_tok, x1, hp, w_tok, g2, fgain, wsg, wsu, wsd, seq, part, prev_out):
    n, d = x1.shape
    dsh = wsg.shape[1]
    tm = MOVE_TILE
    per_b = seq // tm
    tiles = y_tok.shape[2] // tm
    t0 = part * tiles
    const = lambda i: (0, 0)

    def piece_spec(j):
        return pl.BlockSpec((None, TOP_K, tm, LANES), lambda i: (j, 0, i, 0))

    in_specs = [piece_spec(j) for j in range(PACK_ROWS)] + [
        pl.BlockSpec((tm, d), lambda i: (t0 + i, 0)),
        pl.BlockSpec((tm * PACK_ROWS, LANES), lambda i: (t0 + i, 0)),
        pl.BlockSpec((TOP_K, tm), lambda i: (0, t0 + i)),
        pl.BlockSpec((1, 1, d), lambda i: ((t0 + i) // per_b, 0, 0)),
        pl.BlockSpec((1, d), const),
        pl.BlockSpec((d, dsh), const),
        pl.BlockSpec((d, dsh), const),
        pl.BlockSpec((dsh, d), const)]
    args = [y_tok, y_tok, y_tok, y_tok, x1, hp, w_tok, g2, fgain, wsg, wsu, wsd]
    aliases = {}
    body = _combine_kernel
    if prev_out is not None:
        in_specs.append(pl.BlockSpec(memory_space=pl.ANY))
        args.append(prev_out)
        aliases = {len(args) - 1: 0}
        body = _combine_kernel_into
    return pl.pallas_call(
        body,
        out_shape=jax.ShapeDtypeStruct((n, d), F32),
        grid=(tiles,),
        in_specs=in_specs,
        out_specs=pl.BlockSpec((tm, d), lambda i: (t0 + i, 0)),
        input_output_aliases=aliases,
        compiler_params=pltpu.CompilerParams(
            dimension_semantics=("arbitrary",), vmem_limit_bytes=VMEM_LIMIT),
        name="combine",
    )(*args)


def _pair_tables(counts, n_rows):
    ne = counts.shape[0]
    sizes = counts.astype(I32)
    ends = jnp.cumsum(sizes)
    starts = ends - sizes
    first_blk = starts // ROW_BLOCK
    last_blk = (ends - 1) // ROW_BLOCK
    n_pairs = jnp.where(sizes > 0, last_blk - first_blk + 1, 0)
    pair_end = jnp.cumsum(n_pairs)
    pair_start = pair_end - n_pairs
    max_pairs = n_rows // ROW_BLOCK + ne
    p = jnp.arange(max_pairs, dtype=I32)
    ok = p < pair_end[-1]
    pc = jnp.minimum(p, pair_end[-1] - 1)
    pair_e = jnp.minimum(jnp.sum((pair_end[None, :] <= pc[:, None]).astype(I32), axis=1), ne - 1)
    pair_blk = (first_blk[pair_e] + pc - pair_start[pair_e]).astype(I32)
    eid = jnp.arange(ne, dtype=I32)
    later = lax.cummin(jnp.where(sizes > 0, eid, ne), reverse=True)
    next_e = jnp.concatenate([later[1:], jnp.full((1,), ne, I32)])
    next_e = jnp.where(next_e < ne, next_e, eid)
    return pair_e, pair_blk, ok.astype(I32), starts.astype(I32), ends.astype(I32), next_e


def kernel(x, c, w_ada, b_ada, w_in, lb_logits, hgrn_norm, lam_re, lam_im, log_dt, b_re, b_im,
           c_re, c_im, d_skip, w_glu, b_glu, w_out, w_router, router_bias, w_gate, w_up, w_down,
           ws_gate, ws_up, ws_down, final_gain):
    bsz, seq, d = x.shape
    n = bsz * seq
    fdim = HGRN_HEADS * HGRN_KDIM
    n_chunks = seq // CHUNK
    lb = jnp.cumsum(jax.nn.softmax(lb_logits.astype(F32), axis=0), axis=0)[0].reshape(1, fdim)

    mod = _mod_call(c, w_ada[0], b_ada[0]).reshape(bsz, 6, d)
    nb = w_in.shape[2] - 4 * fdim
    perm = jnp.concatenate([jnp.arange(0, nb, 2), jnp.arange(1, nb, 2)])
    w_in_b = w_in[0].astype(BF16)
    ut, out_a, ucm = _mix_front_call(x, mod, w_in_b[:, :4 * fdim], w_in_b[:, 4 * fdim:][:, perm].T, lb,
                                hgrn_norm[0].reshape(1, fdim))

    cb, lp2, p_tab, q_tab, a1, a2 = _s5_tables(lam_re[0], lam_im[0], log_dt[0], b_re[0], b_im[0],
                                             c_re[0], c_im[0])
    yt = _s5_call(ut, cb, lp2, p_tab, q_tab, a1, a2, n_chunks, bsz)

    wr_t = w_router[0].T
    wr_hi = wr_t.astype(BF16)
    wr_lo = (wr_t - wr_hi.astype(F32)).astype(BF16)
    w_out_b = w_out[0].astype(BF16)
    w_out_p = jnp.concatenate([w_out_b[:fdim], w_out_b[fdim:][perm]], axis=0)
    x1, hp, logits_t = _mix_back_call(
        x.reshape(n, d), out_a.reshape(n, fdim), yt, ucm, mod,
        d_skip[0][perm].reshape(nb, 1), w_glu[0][perm][:, perm].T.astype(BF16),
        b_glu[0][perm].reshape(nb, 1), w_out_p, wr_hi, wr_lo, seq)

    top_idx, top_w, rank, counts = _route_call(logits_t, router_bias[0])
    pair_e, pair_blk, pair_ok, starts, ends, next_e = _pair_tables(counts[:, 0], n * TOP_K)

    def windowed(a):
        return a.reshape(TOP_K, n // SC_WINDOW, SC_WINDOW).transpose(1, 0, 2)

    xs, pos_win = _dispatch_call(windowed(top_idx), windowed(rank), starts,
                                 hp.reshape(n, PACK_ROWS, LANES))
    y_sorted = _experts_call(pair_e, pair_blk, pair_ok, starts, ends, next_e, xs,
                             w_gate[0], w_up[0], w_down[0])
    w_tok = top_w
    g2 = mod[:, 5:6, :]
    fgain = final_gain.reshape(1, d)
    wsg, wsu, wsd = ws_gate[0].astype(BF16), ws_up[0].astype(BF16), ws_down[0].astype(BF16)
    wins = pos_win.shape[0] // TAIL_PARTS
    out = None
    for part in range(TAIL_PARTS):
        y_tok = _collect_call(pos_win[part * wins:(part + 1) * wins], y_sorted)
        out = _combine_call(y_tok, x1, hp, w_tok, g2, fgain, wsg, wsu, wsd, seq, part, out)
    return out.reshape(bsz, seq, d)
```

```python
import functools

import jax
import jax.numpy as jnp
from jax import lax
from jax.experimental import pallas as pl
from jax.experimental.pallas import tpu as pltpu
from jax.experimental.pallas import tpu_sc as plsc

F32 = jnp.float32
BF16 = jnp.bfloat16
I32 = jnp.int32

EPS = 1e-6
CHUNK = 64
HGRN_HEADS = 4
HGRN_KDIM = 128
S5_GROUP = 16
S5_STATE = 64
N_EXPERT_GROUPS = 8
TOPK_GROUPS = 4
TOP_K = 8
ROUTE_SCALE = 2.5
LANES = 128
PACK_ROWS = 4
SC_CORES = 2
SC_SUBCORES = 16
SC_LANES = 16
SC_WINDOW = 64

SEQ_TILE = 512
TOK_TILE = 512
ROUTE_TILE = 512
MOVE_TILE = 512
ROW_BLOCK = 2048
SUB_BLOCK = 512
TAIL_PARTS = 4
VMEM_LIMIT = 56 * 1024 * 1024

_NT = (((1,), (1,)), ((), ()))
_TN = (((0,), (0,)), ((), ()))


def _sigmoid(v):
    return 0.5 * jnp.tanh(0.5 * v) + 0.5


def _silu(v):
    return v * _sigmoid(v)


def _bdot(a, b):
    return jnp.dot(a.astype(BF16), b.astype(BF16), preferred_element_type=F32)


def _store_packed(ref, val, n_rows):
    for j in range(PACK_ROWS):
        lo = val[:, 2 * j * LANES:(2 * j + 1) * LANES]
        hi = val[:, (2 * j + 1) * LANES:(2 * j + 2) * LANES]
        ref[pl.ds(j, n_rows, stride=PACK_ROWS), :] = pltpu.pack_elementwise([lo, hi], packed_dtype=BF16)


def _mod_kernel(c_ref, w_ref, b_ref, o_ref):
    o_ref[...] = _bdot(_silu(c_ref[...]), w_ref[...]) + b_ref[...]


def _mod_call(c, w_ada, b_ada):
    bsz, d = c.shape
    n_out = w_ada.shape[1]
    return pl.pallas_call(
        _mod_kernel,
        out_shape=jax.ShapeDtypeStruct((bsz, n_out), F32),
        grid=(n_out // d,),
        in_specs=[pl.BlockSpec((bsz, d), lambda j: (0, 0)),
                  pl.BlockSpec((d, d), lambda j: (0, j)),
                  pl.BlockSpec((1, d), lambda j: (0, j))],
        out_specs=pl.BlockSpec((bsz, d), lambda j: (0, j)),
        compiler_params=pltpu.CompilerParams(vmem_limit_bytes=VMEM_LIMIT),
        name="mod",
    )(c, w_ada, b_ada.reshape(1, n_out))


def _split_chunk_pairs(tile_even, tile_odd):
    low = lax.broadcasted_iota(I32, tile_even.shape, 1) < CHUNK
    first = jnp.where(low, tile_even, pltpu.roll(tile_odd, CHUNK, 1))
    second = jnp.where(low, pltpu.roll(tile_even, CHUNK, 1), tile_odd)
    return first, second


def _mix_front_kernel(x_ref, mod_ref, win_ref, wut_ref, lb_ref, gn_ref, ltri_ref,
                      ut_ref, oa_ref, ucm_ref, proj_a, proj_b, st_ref, flat_ref, hb_ref, *, tiles_per_seq):
    fdim = HGRN_HEADS * HGRN_KDIM
    ts = x_ref.shape[1]
    pairs = wut_ref.shape[0] // 2
    rows = ts // CHUNK
    i = pl.program_id(0)

    @pl.when(i == 0)
    def _():
        proj_b[...] = jnp.zeros_like(proj_b)

    @pl.when(lax.rem(jnp.maximum(i - 1, 0), tiles_per_seq) == 0)
    def _():
        st_ref[...] = jnp.zeros_like(st_ref)

    def normalise():
        x = x_ref[0]
        ms = jnp.mean(x * x, axis=-1, keepdims=True)
        h = x * lax.rsqrt(ms + EPS) * (1.0 + mod_ref[0, 1:2, :]) + mod_ref[0, 0:1, :]
        hb_ref[...] = h.astype(BF16)

    def project_slab(proj_ref, n):
        width = proj_ref.shape[1] // rows
        cols = slice(n * width, (n + 1) * width)
        proj_ref[:, cols] = jnp.dot(hb_ref[...], win_ref[:, cols], preferred_element_type=F32)

    def s5_input():
        u_t = lax.dot_general(wut_ref[...], hb_ref[...], _NT, preferred_element_type=F32)
        ucm_ref[...] = u_t.astype(BF16)
        for m in range(ts // LANES):
            first, second = _split_chunk_pairs(u_t[:pairs, m * LANES:(m + 1) * LANES],
                                               u_t[pairs:, m * LANES:(m + 1) * LANES])
            flat_ref[2 * m * pairs:(2 * m + 1) * pairs, :] = first
            flat_ref[(2 * m + 1) * pairs:(2 * m + 2) * pairs, :] = second
        per_group = S5_GROUP // 2
        for q in range(pairs):
            ut_ref[q // per_group, :, (q % per_group) * LANES:(q % per_group + 1) * LANES] = (
                flat_ref[pl.ds(q, rows, stride=pairs), :])

    lb = lb_ref[...]
    gn = gn_ref[...]
    ltri = ltri_ref[...]
    row = lax.broadcasted_iota(I32, (CHUNK, CHUNK), 0)
    col = lax.broadcasted_iota(I32, (CHUNK, CHUNK), 1)
    causal = row >= col

    def step(proj_ref, next_ref):
        normalise()
        for ci in range(rows):
            project_slab(next_ref, ci)
            r0 = ci * CHUNK
            q = proj_ref[r0:r0 + CHUNK, 0:fdim]
            fl = proj_ref[r0:r0 + CHUNK, fdim:2 * fdim]
            iv = proj_ref[r0:r0 + CHUNK, 2 * fdim:3 * fdim]
            og = proj_ref[r0:r0 + CHUNK, 3 * fdim:4 * fdim]
            f = lb + (1.0 - lb) * _sigmoid(fl)
            lf_hi, lf_mid, lf_lo = _split3(jnp.log(f))
            b = (jnp.dot(ltri, lf_hi, preferred_element_type=F32)
                 + jnp.dot(ltri, lf_mid, preferred_element_type=F32)
                 + jnp.dot(ltri, lf_lo, preferred_element_type=F32))
            b_ref = b[CHUNK // 2 - 1:CHUNK // 2, :]
            b_last = b[CHUNK - 1:CHUNK, :]
            qs = _silu(q)
            kk = 1.0 - f
            qe = (qs * jnp.exp(b - b_ref)).astype(BF16)
            ke = (kk * jnp.exp(b_ref - b)).astype(BF16)
            qb = (qs * jnp.exp(b)).astype(BF16)
            k2 = (kk * jnp.exp(b_last - b)).astype(BF16)
            dec = jnp.exp(b_last)
            ivb = iv.astype(BF16)
            outs = []
            for hh in range(HGRN_HEADS):
                sl = slice(hh * HGRN_KDIM, (hh + 1) * HGRN_KDIM)
                att = lax.dot_general(qe[:, sl], ke[:, sl], _NT, preferred_element_type=F32)
                att = jnp.where(causal, att, 0.0)
                st = st_ref[hh]
                o = jnp.dot(att.astype(BF16), ivb[:, sl], preferred_element_type=F32)
                o = o + lax.dot_general(qb[:, sl], st.astype(BF16), _NT, preferred_element_type=F32)
                st_ref[hh] = st * dec[:, sl] + lax.dot_general(
                    ivb[:, sl], k2[:, sl], _TN, preferred_element_type=F32)
                outs.append(o * lax.rsqrt(jnp.mean(o * o, axis=-1, keepdims=True) + EPS))
            o = jnp.concatenate(outs, axis=1) * gn * _silu(og)
            oa_ref[0, r0:r0 + CHUNK, :] = o.astype(BF16)
        s5_input()

    @pl.when(lax.rem(i, 2) == 0)
    def _():
        step(proj_b, proj_a)

    @pl.when(lax.rem(i, 2) == 1)
    def _():
        step(proj_a, proj_b)


def _mix_front_call(x, mod, w_main, w_ut, lb, gn):
    bsz, seq, d = x.shape
    fdim = HGRN_HEADS * HGRN_KDIM
    ncols = w_main.shape[1]
    nb = w_ut.shape[0]
    groups = nb // S5_GROUP
    ltri = jnp.tril(jnp.ones((CHUNK, CHUNK), BF16))
    ts = SEQ_TILE
    tiles = seq // ts
    n_tiles = bsz * tiles
    rows = ts // CHUNK

    def cur(i):
        return jnp.minimum(i, n_tiles - 1)

    def prev(i):
        return jnp.maximum(i - 1, 0)

    return pl.pallas_call(
        functools.partial(_mix_front_kernel, tiles_per_seq=tiles),
        out_shape=(jax.ShapeDtypeStruct((groups, bsz * seq // CHUNK, S5_GROUP * CHUNK), F32),
                   jax.ShapeDtypeStruct((bsz, seq, fdim), BF16),
                   jax.ShapeDtypeStruct((nb, bsz * seq), BF16)),
        grid=(n_tiles + 1,),
        in_specs=[pl.BlockSpec((1, ts, d), lambda i: (cur(i) // tiles, cur(i) % tiles, 0)),
                  pl.BlockSpec((1, 6, d), lambda i: (cur(i) // tiles, 0, 0)),
                  pl.BlockSpec((d, ncols), lambda i: (0, 0)),
                  pl.BlockSpec((nb, d), lambda i: (0, 0)),
                  pl.BlockSpec((1, fdim), lambda i: (0, 0)),
                  pl.BlockSpec((1, fdim), lambda i: (0, 0)),
                  pl.BlockSpec((CHUNK, CHUNK), lambda i: (0, 0))],
        out_specs=(pl.BlockSpec((groups, rows, S5_GROUP * CHUNK), lambda i: (0, cur(i), 0)),
                   pl.BlockSpec((1, ts, fdim), lambda i: (prev(i) // tiles, prev(i) % tiles, 0)),
                   pl.BlockSpec((nb, ts), lambda i: (0, cur(i)))),
        scratch_shapes=[pltpu.VMEM((ts, ncols), F32), pltpu.VMEM((ts, ncols), F32),
                        pltpu.VMEM((HGRN_HEADS, fdim // HGRN_HEADS, HGRN_KDIM), F32),
                        pltpu.VMEM((rows * nb // 2, LANES), F32),
                        pltpu.VMEM((ts, d), BF16)],
        compiler_params=pltpu.CompilerParams(
            dimension_semantics=("arbitrary",), vmem_limit_bytes=VMEM_LIMIT),
        name="mix_front",
    )(x, mod, w_main, w_ut, lb, gn, ltri)


def _s5_tables(lam_re, lam_im, log_dt, b_re, b_im, c_re, c_im):
    t = CHUNK
    lam = lax.complex(jnp.minimum(lam_re, -1e-4), lam_im)
    lam_dt = lam * jnp.exp(log_dt)[:, None]
    lam_bar = jnp.exp(lam_dt)
    b_bar = ((lam_bar - 1.0) / lam)[..., None] * lax.complex(b_re, b_im)
    c_mat = lax.complex(c_re, c_im)
    taus = jnp.arange(t + 1, dtype=F32)
    lam_pow = jnp.exp(lam_dt[:, None, :] * taus[None, :, None])
    g, p = lam.shape
    c = b_re.shape[-1]
    cb = c_mat[:, None, :, :] * b_bar.transpose(0, 2, 1)[:, :, None, :]
    cb = jnp.concatenate([cb.real, -cb.imag], axis=-1)
    cb = cb.reshape(g, c, c // 2, 2 * 2 * p).reshape(g, c * c // 2, 4 * p)
    lp = jnp.concatenate([lam_pow[:, :t].real, lam_pow[:, :t].imag], axis=-1).transpose(0, 2, 1)
    zero = jnp.zeros_like(lp)
    lp2 = jnp.concatenate([jnp.concatenate([lp, zero], axis=2),
                           jnp.concatenate([zero, lp], axis=2)], axis=1)
    pc = lam_pow[:, t - 1::-1][:, :t, :, None] * b_bar[:, None, :, :]
    pc = pc.transpose(0, 3, 1, 2).reshape(g, c * t, p)
    p_tab = jnp.concatenate([pc.real, pc.imag], axis=-1)
    ql = c_mat[:, None, :, :] * lam_pow[:, 1:t + 1, None, :]
    ql = ql.transpose(0, 3, 2, 1).reshape(g, p, c * t)
    q_tab = jnp.concatenate([ql.real, -ql.imag], axis=1)
    lam_t = lam_pow[:, t]
    a1 = jnp.concatenate([lam_t.real, lam_t.real], axis=-1)[:, None, :]
    a2 = jnp.concatenate([-lam_t.imag, lam_t.imag], axis=-1)[:, None, :]
    return cb, lp2, p_tab.astype(BF16), q_tab.astype(BF16), a1, a2


def _split3(v):
    hi = v.astype(BF16)
    rem = v - hi.astype(F32)
    mid = rem.astype(BF16)
    return hi, mid, (rem - mid.astype(F32)).astype(BF16)


def _s5_kernel(u_ref, cb_ref, lp_ref, p_ref, q_ref, a1_ref, a2_ref, y_ref, v_ref, xs_ref, m_ref, k_ref,
               *, n_chunks, n_batch):
    c_hi, c_mid, c_lo = _split3(cb_ref[0])
    l_hi, l_mid, l_lo = _split3(lp_ref[0])
    k_ref[...] = (jnp.dot(c_hi, l_hi, preferred_element_type=F32)
                  + jnp.dot(c_hi, l_mid, preferred_element_type=F32)
                  + jnp.dot(c_mid, l_hi, preferred_element_type=F32)
                  + jnp.dot(c_hi, l_lo, preferred_element_type=F32)
                  + jnp.dot(c_mid, l_mid, preferred_element_type=F32)
                  + jnp.dot(c_lo, l_hi, preferred_element_type=F32))
    n_pairs = m_ref.shape[1] // LANES
    n_in = k_ref.shape[0] // n_pairs
    lane = lax.broadcasted_iota(I32, (CHUNK, LANES), 1)
    causal = (lane & (CHUNK - 1)) >= lax.broadcasted_iota(I32, (CHUNK, LANES), 0)
    for ci in range(n_in):
        for a in range(n_pairs):
            lags = jnp.broadcast_to(k_ref[ci * n_pairs + a:ci * n_pairs + a + 1, :], (CHUNK, LANES))
            tile = pltpu.roll(lags, 0, 1, stride=1, stride_axis=0)
            m_ref[ci * CHUNK:(ci + 1) * CHUNK, a * LANES:(a + 1) * LANES] = jnp.where(
                causal, tile, 0.0).astype(BF16)
    u = u_ref[0].astype(BF16)
    v_ref[...] = jnp.dot(u, p_ref[0], preferred_element_type=F32)
    a1 = a1_ref[0]
    a2 = a2_ref[0]
    half = xs_ref.shape[1] // 2

    def step(n, state):
        xs_ref[pl.ds(n, n_batch, stride=n_chunks), :] = state
        return (a1 * state + a2 * pltpu.roll(state, half, 1)
                + v_ref[pl.ds(n, n_batch, stride=n_chunks), :])

    lax.fori_loop(0, n_chunks, step, jnp.zeros((n_batch, xs_ref.shape[1]), F32))
    y = jnp.dot(u, m_ref[...], preferred_element_type=F32)
    y_ref[0] = y + jnp.dot(xs_ref[...].astype(BF16), q_ref[0], preferred_element_type=F32)


def _s5_call(ut, cb, lp2, p_tab, q_tab, a1, a2, n_chunks, n_batch):
    g, rows, width = ut.shape
    p2 = p_tab.shape[-1]
    return pl.pallas_call(
        functools.partial(_s5_kernel, n_chunks=n_chunks, n_batch=n_batch),
        out_shape=jax.ShapeDtypeStruct((g, rows, width), F32),
        grid=(g,),
        in_specs=[pl.BlockSpec((1, rows, width), lambda i: (i, 0, 0)),
                  pl.BlockSpec((1,) + cb.shape[1:], lambda i: (i, 0, 0)),
                  pl.BlockSpec((1,) + lp2.shape[1:], lambda i: (i, 0, 0)),
                  pl.BlockSpec((1, width, p2), lambda i: (i, 0, 0)),
                  pl.BlockSpec((1, p2, width), lambda i: (i, 0, 0)),
                  pl.BlockSpec((1, 1, p2), lambda i: (i, 0, 0)),
                  pl.BlockSpec((1, 1, p2), lambda i: (i, 0, 0))],
        out_specs=pl.BlockSpec((1, rows, width), lambda i: (i, 0, 0)),
        scratch_shapes=[pltpu.VMEM((rows, p2), F32), pltpu.VMEM((rows, p2), F32),
                        pltpu.VMEM((width, width), BF16),
                        pltpu.VMEM((cb.shape[1], lp2.shape[2]), F32)],
        compiler_params=pltpu.CompilerParams(
            dimension_semantics=("arbitrary",), vmem_limit_bytes=VMEM_LIMIT),
        name="s5",
    )(ut, cb, lp2, p_tab, q_tab, a1, a2)


def _token_major(flat_ref, src_ref):
    groups, rows, _ = src_ref.shape
    per_group = S5_GROUP // 2
    pairs = groups * per_group
    for q in range(pairs):
        flat_ref[pl.ds(q, rows, stride=pairs), :] = (
            src_ref[q // per_group, :, (q % per_group) * LANES:(q % per_group + 1) * LANES])
    tiles = []
    for m in range(rows // 2):
        even, odd = _split_chunk_pairs(flat_ref[2 * m * pairs:(2 * m + 1) * pairs, :],
                                       flat_ref[(2 * m + 1) * pairs:(2 * m + 2) * pairs, :])
        tiles.append(jnp.concatenate([even, odd], axis=0))
    return jnp.concatenate(tiles, axis=1)


def _mix_back_kernel(x_ref, oa_ref, yt_ref, ucm_ref, mod_ref, dskip_ref, wglu_ref, bglu_ref,
                     wout_ref, wrh_ref, wrl_ref,
                     x1_ref, hp_ref, lt_ref, flat_ref):
    na = oa_ref.shape[1]
    y_t = _token_major(flat_ref, yt_ref)
    u_t = ucm_ref[...].astype(F32)
    z_t = jax.nn.gelu(y_t + dskip_ref[...] * u_t)
    gate_t = _sigmoid(jnp.dot(wglu_ref[...], z_t.astype(BF16), preferred_element_type=F32)
                      + bglu_ref[...])
    ob_t = (z_t * gate_t).astype(BF16)
    mixed = (jnp.dot(oa_ref[...], wout_ref[0:na, :], preferred_element_type=F32)
             + lax.dot_general(ob_t, wout_ref[na:, :], _TN, preferred_element_type=F32))
    x1 = x_ref[...] + mod_ref[0, 2:3, :] * mixed
    ms = jnp.mean(x1 * x1, axis=-1, keepdims=True)
    h2 = x1 * lax.rsqrt(ms + EPS) * (1.0 + mod_ref[0, 4:5, :]) + mod_ref[0, 3:4, :]
    _store_packed(hp_ref, h2, h2.shape[0])
    h_hi = h2.astype(BF16)
    h_lo = (h2 - h_hi.astype(F32)).astype(BF16)
    lt = lax.dot_general(wrh_ref[...], h_hi, _NT, preferred_element_type=F32)
    lt = lt + lax.dot_general(wrl_ref[...], h_hi, _NT, preferred_element_type=F32)
    lt = lt + lax.dot_general(wrh_ref[...], h_lo, _NT, preferred_element_type=F32)
    lt_ref[...] = lt
    x1_ref[...] = x1


def _mix_back_call(x2d, oa, yt, ucm, mod, dskip, wglu, bglu, wout, wrh, wrl, seq):
    n, d = x2d.shape
    nb = oa.shape[1]
    ne = wrh.shape[0]
    tm = TOK_TILE
    per_b = seq // tm
    groups, _, width = yt.shape
    rows = tm // CHUNK
    const = lambda i: (0, 0)
    flat_block = pl.BlockSpec((groups, rows, width), lambda i: (0, i, 0))
    return pl.pallas_call(
        _mix_back_kernel,
        out_shape=(jax.ShapeDtypeStruct((n, d), F32),
                   jax.ShapeDtypeStruct((n * PACK_ROWS, LANES), I32),
                   jax.ShapeDtypeStruct((ne, n), F32)),
        grid=(n // tm,),
        in_specs=[pl.BlockSpec((tm, d), lambda i: (i, 0)),
                  pl.BlockSpec((tm, nb), lambda i: (i, 0)),
                  flat_block,
                  pl.BlockSpec((nb, tm), lambda i: (0, i)),
                  pl.BlockSpec((1, 6, d), lambda i: (i // per_b, 0, 0)),
                  pl.BlockSpec((nb, 1), const),
                  pl.BlockSpec((nb, nb), const),
                  pl.BlockSpec((nb, 1), const),
                  pl.BlockSpec((d, d), const),
                  pl.BlockSpec((ne, d), const),
                  pl.BlockSpec((ne, d), const)],
        out_specs=(pl.BlockSpec((tm, d), lambda i: (i, 0)),
                   pl.BlockSpec((tm * PACK_ROWS, LANES), lambda i: (i, 0)),
                   pl.BlockSpec((ne, tm), lambda i: (0, i))),
        scratch_shapes=[pltpu.VMEM((rows * nb // 2, LANES), F32)],
        compiler_params=pltpu.CompilerParams(
            dimension_semantics=("arbitrary",), vmem_limit_bytes=VMEM_LIMIT),
        name="mix_back",
    )(x2d, oa, yt, ucm, mod, dskip, wglu, bglu, wout, wrh, wrl)


def _route_kernel(lt_ref, bias_ref, su_ref, idx_ref, w_ref, rank_ref, cnt_ref, run_ref):
    ne, tr = lt_ref.shape
    per_group = ne // N_EXPERT_GROUPS
    neg = -jnp.inf

    @pl.when(pl.program_id(0) == 0)
    def _():
        run_ref[...] = jnp.zeros_like(run_ref)

    s = _sigmoid(lt_ref[...])
    sel = s + bias_ref[...]
    gio = lax.broadcasted_iota(I32, (per_group, tr), 0)
    gscore = []
    for g in range(N_EXPERT_GROUPS):
        v = sel[g * per_group:(g + 1) * per_group, :]
        m1 = jnp.max(v, axis=0, keepdims=True)
        i1 = jnp.min(jnp.where(v == m1, gio, per_group), axis=0, keepdims=True)
        m2 = jnp.max(jnp.where(gio == i1, neg, v), axis=0, keepdims=True)
        gscore.append(m1 + m2)
    masked = []
    for g in range(N_EXPERT_GROUPS):
        ahead = jnp.zeros((1, tr), I32)
        for o in range(N_EXPERT_GROUPS):
            if o == g:
                continue
            wins = (gscore[o] >= gscore[g]) if o < g else (gscore[o] > gscore[g])
            ahead = ahead + wins.astype(I32)
        keep = ahead < TOPK_GROUPS
        masked.append(jnp.where(keep, sel[g * per_group:(g + 1) * per_group, :], neg))
    selm = jnp.concatenate(masked, axis=0)
    eio = lax.broadcasted_iota(I32, (ne, tr), 0)
    candidate = selm > neg
    idxs, ws = [], []
    for k in range(TOP_K):
        m = jnp.max(selm, axis=0, keepdims=True)
        ik = jnp.min(jnp.where(selm == m, eio, ne), axis=0, keepdims=True)
        onehot = eio == ik
        ws.append(jnp.sum(jnp.where(onehot, s, 0.0), axis=0, keepdims=True))
        selm = jnp.where(onehot, neg, selm)
        idxs.append(ik)
    hits = jnp.where(jnp.logical_and(candidate, selm == neg), 1.0, 0.0)
    wsum = ws[0]
    for k in range(1, TOP_K):
        wsum = wsum + ws[k]
    scale = ROUTE_SCALE / wsum
    ranks = jnp.dot(hits.astype(BF16), su_ref[...], preferred_element_type=F32) + run_ref[...]
    for k in range(TOP_K):
        idx_ref[k:k + 1, :] = idxs[k]
        w_ref[k:k + 1, :] = ws[k] * scale
        rk = jnp.sum(jnp.where(eio == idxs[k], ranks, 0.0), axis=0, keepdims=True)
        rank_ref[k:k + 1, :] = rk.astype(I32)
    run_ref[...] = run_ref[...] + jnp.sum(hits, axis=1, keepdims=True)
    cnt_ref[...] = run_ref[...]


def _route_call(lt, bias):
    ne, n = lt.shape
    tr = ROUTE_TILE
    su = jnp.triu(jnp.ones((tr, tr), F32), k=1).astype(BF16)
    return pl.pallas_call(
        _route_kernel,
        out_shape=(jax.ShapeDtypeStruct((TOP_K, n), I32),
                   jax.ShapeDtypeStruct((TOP_K, n), F32),
                   jax.ShapeDtypeStruct((TOP_K, n), I32),
                   jax.ShapeDtypeStruct((ne, 1), F32)),
        grid=(n // tr,),
        in_specs=[pl.BlockSpec((ne, tr), lambda i: (0, i)),
                  pl.BlockSpec((ne, 1), lambda i: (0, 0)),
                  pl.BlockSpec((tr, tr), lambda i: (0, 0))],
        out_specs=(pl.BlockSpec((TOP_K, tr), lambda i: (0, i)),
                   pl.BlockSpec((TOP_K, tr), lambda i: (0, i)),
                   pl.BlockSpec((TOP_K, tr), lambda i: (0, i)),
                   pl.BlockSpec((ne, 1), lambda i: (0, 0))),
        scratch_shapes=[pltpu.VMEM((ne, 1), F32)],
        compiler_params=pltpu.CompilerParams(
            dimension_semantics=("arbitrary",), vmem_limit_bytes=VMEM_LIMIT),
        name="route",
    )(lt, bias.reshape(ne, 1), su)


def _sc_mesh():
    return plsc.VectorSubcoreMesh(core_axis_name="c", subcore_axis_name="s",
                                  num_cores=SC_CORES, num_subcores=SC_SUBCORES)


def _sc_worker():
    return lax.axis_index("s") * SC_CORES + lax.axis_index("c")


def _dispatch_call(idx_win, rank_win, starts, hp):
    n = hp.shape[0]
    n_workers = SC_CORES * SC_SUBCORES
    wins_per_worker = n // SC_WINDOW // n_workers
    lanes = SC_LANES

    def body(hp_hbm, idx_hbm, rank_hbm, st_hbm, xs_hbm, pos_hbm,
             idx_a, idx_b, rank_a, rank_b, pos_a, pos_b, rows_a, rows_b, st_v,
             load_a, load_b, scatter_sem, pos_sem):
        first_win = _sc_worker() * wins_per_worker
        idx_v, rank_v, pos_v = (idx_a, idx_b), (rank_a, rank_b), (pos_a, pos_b)
        rows_v, load_sem = (rows_a, rows_b), (load_a, load_b)
        pltpu.sync_copy(st_hbm, st_v)

        def loads(win, slot):
            return (pltpu.make_async_copy(hp_hbm.at[pl.ds(win * SC_WINDOW, SC_WINDOW)], rows_v[slot],
                                          load_sem[slot]),
                    pltpu.make_async_copy(idx_hbm.at[win], idx_v[slot], load_sem[slot]),
                    pltpu.make_async_copy(rank_hbm.at[win], rank_v[slot], load_sem[slot]))

        for cp in loads(first_win, 0):
            cp.start()

        @pl.loop(0, wins_per_worker, step=2)
        def _(w):
            for slot in range(2):
                win = first_win + w + slot
                for cp in loads(win, slot):
                    cp.wait()

                @pl.when(w + slot + 1 < wins_per_worker)
                def _():
                    for cp in loads(win + 1, 1 - slot):
                        cp.start()

                for k in range(TOP_K):
                    for c in range(SC_WINDOW // lanes):
                        seg = pl.ds(c * lanes, lanes)
                        base = plsc.load_gather(st_v, [idx_v[slot][k, seg]])
                        pos_v[slot][k, seg] = base + rank_v[slot][k, seg]
                pos_out = pltpu.async_copy(pos_v[slot], pos_hbm.at[win], pos_sem)
                copies = [pltpu.async_copy(rows_v[slot], xs_hbm.at[pos_v[slot].at[k]], scatter_sem)
                          for k in range(TOP_K)]
                for cp in copies:
                    cp.wait()
                pos_out.wait()

    idx_buf = pltpu.VMEM((TOP_K, SC_WINDOW), I32)
    row_buf = pltpu.VMEM((SC_WINDOW,) + hp.shape[1:], hp.dtype)
    return pl.kernel(
        body,
        out_type=(jax.ShapeDtypeStruct((n * TOP_K,) + hp.shape[1:], hp.dtype),
                  jax.ShapeDtypeStruct(idx_win.shape, I32)),
        mesh=_sc_mesh(),
        scratch_types=[idx_buf, idx_buf, idx_buf, idx_buf, idx_buf, idx_buf, row_buf, row_buf,
                       pltpu.VMEM(starts.shape, I32),
                       pltpu.SemaphoreType.DMA, pltpu.SemaphoreType.DMA, pltpu.SemaphoreType.DMA,
                       pltpu.SemaphoreType.DMA],
        compiler_params=pltpu.CompilerParams(needs_layout_passes=False),
        name="dispatch",
    )(hp, idx_win, rank_win, starts)


def _collect_call(pos_win, y_sorted):
    n = pos_win.shape[0] * SC_WINDOW
    n_workers = SC_CORES * SC_SUBCORES
    wins_per_worker = n // SC_WINDOW // n_workers

    def body(ys_hbm, pos_hbm, out_hbm, idx_v, rows_a, rows_b, gather_sem, write_sem):
        first_win = _sc_worker() * wins_per_worker
        bufs = (rows_a, rows_b)

        @pl.loop(0, wins_per_worker)
        def _(w):
            win = first_win + w
            pltpu.sync_copy(pos_hbm.at[win], idx_v)

            def gather(k):
                return pltpu.async_copy(ys_hbm.at[idx_v.at[k]], bufs[k % 2], gather_sem)

            pending_gather = gather(0)
            pending_write = None
            for k in range(TOP_K):
                pending_gather.wait()
                if pending_write is not None:
                    for cp in pending_write:
                        cp.wait()
                if k + 1 < TOP_K:
                    pending_gather = gather(k + 1)
                pending_write = [
                    pltpu.async_copy(bufs[k % 2].at[:, j],
                                     out_hbm.at[j, k, pl.ds(win * SC_WINDOW, SC_WINDOW)], write_sem)
                    for j in range(PACK_ROWS)]
            for cp in pending_write:
                cp.wait()

    row_buf = pltpu.VMEM((SC_WINDOW,) + y_sorted.shape[1:], y_sorted.dtype)
    return pl.kernel(
        body,
        out_type=jax.ShapeDtypeStruct((PACK_ROWS, TOP_K, n, LANES), y_sorted.dtype),
        mesh=_sc_mesh(),
        scratch_types=[pltpu.VMEM((TOP_K, SC_WINDOW), I32), row_buf, row_buf,
                       pltpu.SemaphoreType.DMA, pltpu.SemaphoreType.DMA],
        name="collect",
    )(y_sorted, pos_win)


def _experts_kernel(pe_ref, pb_ref, pv_ref, st_ref, en_ref, nx_ref,
                    xs_hbm, wg_hbm, wu_hbm, wd_hbm, y_hbm,
                    wgb_ref, wub_ref, wdb_ref, wgf_ref, wuf_ref, wdf_ref, xbuf_ref, stage_ref,
                    in_sems, out_sems, w_sems, state_ref):
    p = pl.program_id(0)
    n_pairs = pl.num_programs(0)
    e = pe_ref[p]
    blk = pb_ref[p]
    rb = xbuf_ref.shape[2]
    prev = jnp.maximum(p - 1, 0)
    nxt = jnp.minimum(p + 1, n_pairs - 1)
    live = pv_ref[p] == 1
    first = jnp.logical_or(p == 0, pb_ref[prev] != blk)
    block_ends = jnp.logical_or(pb_ref[nxt] != blk, pv_ref[nxt] == 0)
    last = jnp.logical_or(p == n_pairs - 1, block_ends)

    def in_copy(j, block, slot):
        return pltpu.make_async_copy(xs_hbm.at[pl.ds(block * rb, rb), j], xbuf_ref.at[slot, j],
                                     in_sems.at[slot])

    def out_copy(j, block, slot):
        return pltpu.make_async_copy(stage_ref.at[slot, j], y_hbm.at[pl.ds(block * rb, rb), j],
                                     out_sems.at[slot])

    def drain(slot):
        @pl.when(state_ref[1 + slot] == 1)
        def _():
            for j in range(PACK_ROWS):
                out_copy(j, 0, slot).wait()
            state_ref[1 + slot] = 0

    @pl.when(p == 0)
    def _():
        state_ref[0] = 0
        state_ref[1] = 0
        state_ref[2] = 0
        stage_ref[...] = jnp.zeros_like(stage_ref)
        for j in range(PACK_ROWS):
            in_copy(j, blk, 0).start()

    slot = jnp.where(jnp.logical_and(first, p > 0), 1 - state_ref[0], state_ref[0])
    state_ref[0] = slot

    @pl.when(jnp.logical_and(first, blk + 1 < y_hbm.shape[0] // rb))
    def _():
        for j in range(PACK_ROWS):
            in_copy(j, blk + 1, 1 - slot).start()

    @pl.when(first)
    def _():
        for j in range(PACK_ROWS):
            in_copy(j, blk, slot).wait()
        drain(slot)

    def w_copies(expert, wslot):
        return (pltpu.make_async_copy(wg_hbm.at[expert], wgf_ref.at[wslot], w_sems.at[wslot]),
                pltpu.make_async_copy(wu_hbm.at[expert], wuf_ref.at[wslot], w_sems.at[wslot]),
                pltpu.make_async_copy(wd_hbm.at[expert], wdf_ref.at[wslot], w_sems.at[wslot]))

    @pl.when(p == 0)
    def _():
        state_ref[3] = 0
        for cp in w_copies(e, 0):
            cp.start()

    new_expert = jnp.logical_or(p == 0, pe_ref[prev] != e)
    wslot = jnp.where(jnp.logical_and(new_expert, p > 0), 1 - state_ref[3], state_ref[3])
    state_ref[3] = wslot

    @pl.when(new_expert)
    def _():
        for cp in w_copies(e, wslot):
            cp.wait()
        wgb_ref[...] = wgf_ref[wslot].astype(BF16)
        wub_ref[...] = wuf_ref[wslot].astype(BF16)
        wdb_ref[...] = wdf_ref[wslot].astype(BF16)

        @pl.when(nx_ref[e] != e)
        def _():
            for cp in w_copies(nx_ref[e], 1 - wslot):
                cp.start()

    lo_row = st_ref[e]
    hi_row = en_ref[e]

    def sub_block(s, row0, shared):
        sub = pl.ds(s * SUB_BLOCK, SUB_BLOCK)
        pieces = []
        for j in range(PACK_ROWS):
            w = xbuf_ref[slot, j, sub, :]
            pieces.append(lax.bitcast_convert_type(w.astype(jnp.int16), BF16))
            pieces.append(lax.bitcast_convert_type(
                lax.shift_right_logical(w, 16).astype(jnp.int16), BF16))
        xb = jnp.concatenate(pieces, axis=1)
        gate = jnp.dot(xb, wgb_ref[...], preferred_element_type=F32)
        up = jnp.dot(xb, wub_ref[...], preferred_element_type=F32)
        yb = jnp.dot((_silu(gate) * up).astype(BF16), wdb_ref[...], preferred_element_type=F32)
        if shared:
            rows = row0 + lax.broadcasted_iota(I32, (SUB_BLOCK, 1), 0)
            mine = jnp.logical_and(rows >= lo_row, rows < hi_row)
        for j in range(PACK_ROWS):
            word = pltpu.pack_elementwise(
                [yb[:, 2 * j * LANES:(2 * j + 1) * LANES], yb[:, (2 * j + 1) * LANES:(2 * j + 2) * LANES]],
                packed_dtype=BF16)
            if shared:
                word = jnp.where(mine, word, stage_ref[slot, j, sub, :])
            stage_ref[slot, j, sub, :] = word

    def one(s):
        row0 = blk * rb + s * SUB_BLOCK
        touched = jnp.logical_and(live, jnp.logical_and(row0 < hi_row, row0 + SUB_BLOCK > lo_row))
        whole = jnp.logical_and(lo_row <= row0, hi_row >= row0 + SUB_BLOCK)

        @pl.when(jnp.logical_and(touched, whole))
        def _():
            sub_block(s, row0, shared=False)

        @pl.when(jnp.logical_and(touched, jnp.logical_not(whole)))
        def _():
            sub_block(s, row0, shared=True)

    for s in range(0, rb // SUB_BLOCK, 2):
        row0 = blk * rb + s * SUB_BLOCK
        both = jnp.logical_and(live, jnp.logical_and(lo_row <= row0, hi_row >= row0 + 2 * SUB_BLOCK))

        @pl.when(both)
        def _():
            sub_block(s, row0, shared=False)
            sub_block(s + 1, row0 + SUB_BLOCK, shared=False)

        @pl.when(jnp.logical_not(both))
        def _():
            one(s)
            one(s + 1)

    @pl.when(jnp.logical_and(live, last))
    def _():
        for j in range(PACK_ROWS):
            out_copy(j, blk, slot).start()
        state_ref[1 + slot] = 1

    @pl.when(p == n_pairs - 1)
    def _():
        drain(0)
        drain(1)


def _experts_call(pair_e, pair_blk, pair_ok, starts, ends, next_e, xs, w_gate, w_up, w_down):
    ne, d, de = w_gate.shape
    rb = ROW_BLOCK
    n_pairs = pair_e.shape[0]
    anywhere = pl.BlockSpec(memory_space=pl.ANY)
    grid_spec = pltpu.PrefetchScalarGridSpec(
        num_scalar_prefetch=6,
        grid=(n_pairs,),
        in_specs=[anywhere, anywhere, anywhere, anywhere],
        out_specs=anywhere,
        scratch_shapes=[pltpu.VMEM((d, de), BF16), pltpu.VMEM((d, de), BF16),
                        pltpu.VMEM((de, d), BF16),
                        pltpu.VMEM((2, d, de), F32), pltpu.VMEM((2, d, de), F32),
                        pltpu.VMEM((2, de, d), F32),
                        pltpu.VMEM((2, PACK_ROWS, rb, LANES), I32),
                        pltpu.VMEM((2, PACK_ROWS, rb, LANES), I32),
                        pltpu.SemaphoreType.DMA((2,)), pltpu.SemaphoreType.DMA((2,)),
                        pltpu.SemaphoreType.DMA((2,)),
                        pltpu.SMEM((4,), I32)],
    )
    return pl.pallas_call(
        _experts_kernel,
        out_shape=jax.ShapeDtypeStruct(xs.shape, xs.dtype),
        grid_spec=grid_spec,
        compiler_params=pltpu.CompilerParams(
            dimension_semantics=("arbitrary",), vmem_limit_bytes=VMEM_LIMIT),
        name="experts",
    )(pair_e, pair_blk, pair_ok, starts, ends, next_e, xs, w_gate, w_up, w_down)


def _combine_kernel(y0_ref, y1_ref, y2_ref, y3_ref, x1_ref, hp_ref, w_ref, g2_ref, fg_ref,
                    wsg_ref, wsu_ref, wsd_ref, o_ref):
    tm = x1_ref.shape[0]
    w = w_ref[...].T
    parts = []
    for y_ref in (y0_ref, y1_ref, y2_ref, y3_ref):
        for half in range(2):
            acc = None
            for k in range(TOP_K):
                piece = pltpu.unpack_elementwise(y_ref[k], index=half, packed_dtype=BF16,
                                                 unpacked_dtype=F32) * w[:, k:k + 1]
                acc = piece if acc is None else acc + piece
            parts.append(acc)
    routed = jnp.concatenate(parts, axis=1)
    pieces = []
    for j in range(PACK_ROWS):
        word = hp_ref[pl.ds(j, tm, stride=PACK_ROWS), :]
        pieces.append(lax.bitcast_convert_type(word.astype(jnp.int16), BF16))
        pieces.append(lax.bitcast_convert_type(lax.shift_right_logical(word, 16).astype(jnp.int16), BF16))
    h2 = jnp.concatenate(pieces, axis=1)
    hid = _silu(jnp.dot(h2, wsg_ref[...], preferred_element_type=F32)) * jnp.dot(
        h2, wsu_ref[...], preferred_element_type=F32)
    shared = jnp.dot(hid.astype(BF16), wsd_ref[...], preferred_element_type=F32)
    x2 = x1_ref[...] + g2_ref[0] * (routed + shared)
    ms = jnp.mean(x2 * x2, axis=-1, keepdims=True)
    o_ref[...] = x2 * lax.rsqrt(ms + EPS) * fg_ref[...]


def _combine_kernel_into(*refs):
    _combine_kernel(*refs[:-2], refs[-1])


def _combine_call(y_tok, x1, hp, w_tok, g2, fgain, wsg, wsu, wsd, seq, part, prev_out):
    n, d = x1.shape
    dsh = wsg.shape[1]
    tm = MOVE_TILE
    per_b = seq // tm
    tiles = y_tok.shape[2] // tm
    t0 = part * tiles
    const = lambda i: (0, 0)

    def piece_spec(j):
        return pl.BlockSpec((None, TOP_K, tm, LANES), lambda i: (j, 0, i, 0))

    in_specs = [piece_spec(j) for j in range(PACK_ROWS)] + [
        pl.BlockSpec((tm, d), lambda i: (t0 + i, 0)),
        pl.BlockSpec((tm * PACK_ROWS, LANES), lambda i: (t0 + i, 0)),
        pl.BlockSpec((TOP_K, tm), lambda i: (0, t0 + i)),
        pl.BlockSpec((1, 1, d), lambda i: ((t0 + i) // per_b, 0, 0)),
        pl.BlockSpec((1, d), const),
        pl.BlockSpec((d, dsh), const),
        pl.BlockSpec((d, dsh), const),
        pl.BlockSpec((dsh, d), const)]
    args = [y_tok, y_tok, y_tok, y_tok, x1, hp, w_tok, g2, fgain, wsg, wsu, wsd]
    aliases = {}
    body = _combine_kernel
    if prev_out is not None:
        in_specs.append(pl.BlockSpec(memory_space=pl.ANY))
        args.append(prev_out)
        aliases = {len(args) - 1: 0}
        body = _combine_kernel_into
    return pl.pallas_call(
        body,
        out_shape=jax.ShapeDtypeStruct((n, d), F32),
        grid=(tiles,),
        in_specs=in_specs,
        out_specs=pl.BlockSpec((tm, d), lambda i: (t0 + i, 0)),
        input_output_aliases=aliases,
        compiler_params=pltpu.CompilerParams(
            dimension_semantics=("arbitrary",), vmem_limit_bytes=VMEM_LIMIT),
        name="combine",
    )(*args)


def _pair_tables(counts, n_rows):
    ne = counts.shape[0]
    sizes = counts.astype(I32)
    ends = jnp.cumsum(sizes)
    starts = ends - sizes
    first_blk = starts // ROW_BLOCK
    last_blk = (ends - 1) // ROW_BLOCK
    n_pairs = jnp.where(sizes > 0, last_blk - first_blk + 1, 0)
    pair_end = jnp.cumsum(n_pairs)
    pair_start = pair_end - n_pairs
    max_pairs = n_rows // ROW_BLOCK + ne
    p = jnp.arange(max_pairs, dtype=I32)
    ok = p < pair_end[-1]
    pc = jnp.minimum(p, pair_end[-1] - 1)
    pair_e = jnp.minimum(jnp.sum((pair_end[None, :] <= pc[:, None]).astype(I32), axis=1), ne - 1)
    pair_blk = (first_blk[pair_e] + pc - pair_start[pair_e]).astype(I32)
    eid = jnp.arange(ne, dtype=I32)
    later = lax.cummin(jnp.where(sizes > 0, eid, ne), reverse=True)
    next_e = jnp.concatenate([later[1:], jnp.full((1,), ne, I32)])
    next_e = jnp.where(next_e < ne, next_e, eid)
    return pair_e, pair_blk, ok.astype(I32), starts.astype(I32), ends.astype(I32), next_e


def kernel(x, c, w_ada, b_ada, w_in, lb_logits, hgrn_norm, lam_re, lam_im, log_dt, b_re, b_im,
           c_re, c_im, d_skip, w_glu, b_glu, w_out, w_router, router_bias, w_gate, w_up, w_down,
           ws_gate, ws_up, ws_down, final_gain):
    bsz, seq, d = x.shape
    n = bsz * seq
    fdim = HGRN_HEADS * HGRN_KDIM
    n_chunks = seq // CHUNK
    lb = jnp.cumsum(jax.nn.softmax(lb_logits.astype(F32), axis=0), axis=0)[0].reshape(1, fdim)

    mod = _mod_call(c, w_ada[0], b_ada[0]).reshape(bsz, 6, d)
    nb = w_in.shape[2] - 4 * fdim
    perm = jnp.concatenate([jnp.arange(0, nb, 2), jnp.arange(1, nb, 2)])
    w_in_b = w_in[0].astype(BF16)
    ut, out_a, ucm = _mix_front_call(x, mod, w_in_b[:, :4 * fdim], w_in_b[:, 4 * fdim:][:, perm].T, lb,
                                hgrn_norm[0].reshape(1, fdim))

    cb, lp2, p_tab, q_tab, a1, a2 = _s5_tables(lam_re[0], lam_im[0], log_dt[0], b_re[0], b_im[0],
                                             c_re[0], c_im[0])
    yt = _s5_call(ut, cb, lp2, p_tab, q_tab, a1, a2, n_chunks, bsz)

    wr_t = w_router[0].T
    wr_hi = wr_t.astype(BF16)
    wr_lo = (wr_t - wr_hi.astype(F32)).astype(BF16)
    w_out_b = w_out[0].astype(BF16)
    w_out_p = jnp.concatenate([w_out_b[:fdim], w_out_b[fdim:][perm]], axis=0)
    x1, hp, logits_t = _mix_back_call(
        x.reshape(n, d), out_a.reshape(n, fdim), yt, ucm, mod,
        d_skip[0][perm].reshape(nb, 1), w_glu[0][perm][:, perm].T.astype(BF16),
        b_glu[0][perm].reshape(nb, 1), w_out_p, wr_hi, wr_lo, seq)

    top_idx, top_w, rank, counts = _route_call(logits_t, router_bias[0])
    pair_e, pair_blk, pair_ok, starts, ends, next_e = _pair_tables(counts[:, 0], n * TOP_K)

    def windowed(a):
        return a.reshape(TOP_K, n // SC_WINDOW, SC_WINDOW).transpose(1, 0, 2)

    xs, pos_win = _dispatch_call(windowed(top_idx), windowed(rank), starts,
                                 hp.reshape(n, PACK_ROWS, LANES))
    y_sorted = _experts_call(pair_e, pair_blk, pair_ok, starts, ends, next_e, xs,
                             w_gate[0], w_up[0], w_down[0])
    w_tok = top_w
    g2 = mod[:, 5:6, :]
    fgain = final_gain.reshape(1, d)
    wsg, wsu, wsd = ws_gate[0].astype(BF16), ws_up[0].astype(BF16), ws_down[0].astype(BF16)
    wins = pos_win.shape[0] // TAIL_PARTS
    out = None
    for part in range(TAIL_PARTS):
        y_tok = _collect_call(pos_win[part * wins:(part + 1) * wins], y_sorted)
        out = _combine_call(y_tok, x1, hp, w_tok, g2, fgain, wsg, wsu, wsd, seq, part, out)
    return out.reshape(bsz, seq, d)
```

```python
import functools

import jax
import jax.numpy as jnp
from jax import lax
from jax.experimental import pallas as pl
from jax.experimental.pallas import tpu as pltpu
from jax.experimental.pallas import tpu_sc as plsc

F32 = jnp.float32
BF16 = jnp.bfloat16
I32 = jnp.int32

EPS = 1e-6
CHUNK = 64
HGRN_HEADS = 4
HGRN_KDIM = 128
S5_GROUP = 16
S5_STATE = 64
N_EXPERT_GROUPS = 8
TOPK_GROUPS = 4
TOP_K = 8
ROUTE_SCALE = 2.5
LANES = 128
PACK_ROWS = 4
SC_CORES = 2
SC_SUBCORES = 16
SC_LANES = 16
SC_WINDOW = 64

SEQ_TILE = 512
TOK_TILE = 512
ROUTE_TILE = 512
MOVE_TILE = 512
ROW_BLOCK = 2048
SUB_BLOCK = 512
TAIL_PARTS = (1, 3, 3, 1)
VMEM_LIMIT = 56 * 1024 * 1024

_NT = (((1,), (1,)), ((), ()))
_TN = (((0,), (0,)), ((), ()))


def _sigmoid(v):
    return 0.5 * jnp.tanh(0.5 * v) + 0.5


def _silu(v):
    return v * _sigmoid(v)


def _bdot(a, b):
    return jnp.dot(a.astype(BF16), b.astype(BF16), preferred_element_type=F32)


def _store_packed(ref, val, n_rows):
    for j in range(PACK_ROWS):
        lo = val[:, 2 * j * LANES:(2 * j + 1) * LANES]
        hi = val[:, (2 * j + 1) * LANES:(2 * j + 2) * LANES]
        ref[pl.ds(j, n_rows, stride=PACK_ROWS), :] = pltpu.pack_elementwise([lo, hi], packed_dtype=BF16)


def _mod_kernel(c_ref, w_ref, b_ref, o_ref):
    o_ref[...] = _bdot(_silu(c_ref[...]), w_ref[...]) + b_ref[...]


def _mod_call(c, w_ada, b_ada):
    bsz, d = c.shape
    n_out = w_ada.shape[1]
    return pl.pallas_call(
        _mod_kernel,
        out_shape=jax.ShapeDtypeStruct((bsz, n_out), F32),
        grid=(n_out // d,),
        in_specs=[pl.BlockSpec((bsz, d), lambda j: (0, 0)),
                  pl.BlockSpec((d, d), lambda j: (0, j)),
                  pl.BlockSpec((1, d), lambda j: (0, j))],
        out_specs=pl.BlockSpec((bsz, d), lambda j: (0, j)),
        compiler_params=pltpu.CompilerParams(vmem_limit_bytes=VMEM_LIMIT),
        name="mod",
    )(c, w_ada, b_ada.reshape(1, n_out))


def _split_chunk_pairs(tile_even, tile_odd):
    low = lax.broadcasted_iota(I32, tile_even.shape, 1) < CHUNK
    first = jnp.where(low, tile_even, pltpu.roll(tile_odd, CHUNK, 1))
    second = jnp.where(low, pltpu.roll(tile_even, CHUNK, 1), tile_odd)
    return first, second


def _mix_front_kernel(x_ref, mod_ref, win_ref, wut_ref, lb_ref, gn_ref, ltri_ref,
                      ut_ref, oa_ref, ucm_ref, proj_a, proj_b, st_ref, flat_ref, hb_ref, *, tiles_per_seq):
    fdim = HGRN_HEADS * HGRN_KDIM
    ts = x_ref.shape[1]
    pairs = wut_ref.shape[0] // 2
    rows = ts // CHUNK
    i = pl.program_id(0)

    @pl.when(i == 0)
    def _():
        proj_b[...] = jnp.zeros_like(proj_b)

    @pl.when(lax.rem(jnp.maximum(i - 1, 0), tiles_per_seq) == 0)
    def _():
        st_ref[...] = jnp.zeros_like(st_ref)

    def normalise():
        x = x_ref[0]
        ms = jnp.mean(x * x, axis=-1, keepdims=True)
        h = x * lax.rsqrt(ms + EPS) * (1.0 + mod_ref[0, 1:2, :]) + mod_ref[0, 0:1, :]
        hb_ref[...] = h.astype(BF16)

    def project_slab(proj_ref, n):
        width = proj_ref.shape[1] // rows
        cols = slice(n * width, (n + 1) * width)
        proj_ref[:, cols] = jnp.dot(hb_ref[...], win_ref[:, cols], preferred_element_type=F32)

    def s5_input():
        u_t = lax.dot_general(wut_ref[...], hb_ref[...], _NT, preferred_element_type=F32)
        ucm_ref[...] = u_t.astype(BF16)
        for m in range(ts // LANES):
            first, second = _split_chunk_pairs(u_t[:pairs, m * LANES:(m + 1) * LANES],
                                               u_t[pairs:, m * LANES:(m + 1) * LANES])
            flat_ref[2 * m * pairs:(2 * m + 1) * pairs, :] = first
            flat_ref[(2 * m + 1) * pairs:(2 * m + 2) * pairs, :] = second
        per_group = S5_GROUP // 2
        for q in range(pairs):
            ut_ref[q // per_group, :, (q % per_group) * LANES:(q % per_group + 1) * LANES] = (
                flat_ref[pl.ds(q, rows, stride=pairs), :])

    lb = lb_ref[...]
    gn = gn_ref[...]
    ltri = ltri_ref[...]
    row = lax.broadcasted_iota(I32, (CHUNK, CHUNK), 0)
    col = lax.broadcasted_iota(I32, (CHUNK, CHUNK), 1)
    causal = row >= col

    def step(proj_ref, next_ref):
        normalise()
        for ci in range(rows):
            project_slab(next_ref, ci)
            r0 = ci * CHUNK
            q = proj_ref[r0:r0 + CHUNK, 0:fdim]
            fl = proj_ref[r0:r0 + CHUNK, fdim:2 * fdim]
            iv = proj_ref[r0:r0 + CHUNK, 2 * fdim:3 * fdim]
            og = proj_ref[r0:r0 + CHUNK, 3 * fdim:4 * fdim]
            f = lb + (1.0 - lb) * _sigmoid(fl)
            lf_hi, lf_mid, lf_lo = _split3(jnp.log(f))
            b = (jnp.dot(ltri, lf_hi, preferred_element_type=F32)
                 + jnp.dot(ltri, lf_mid, preferred_element_type=F32)
                 + jnp.dot(ltri, lf_lo, preferred_element_type=F32))
            b_ref = b[CHUNK // 2 - 1:CHUNK // 2, :]
            b_last = b[CHUNK - 1:CHUNK, :]
            qs = _silu(q)
            kk = 1.0 - f
            qe = (qs * jnp.exp(b - b_ref)).astype(BF16)
            ke = (kk * jnp.exp(b_ref - b)).astype(BF16)
            qb = (qs * jnp.exp(b)).astype(BF16)
            k2 = (kk * jnp.exp(b_last - b)).astype(BF16)
            dec = jnp.exp(b_last)
            ivb = iv.astype(BF16)
            outs = []
            for hh in range(HGRN_HEADS):
                sl = slice(hh * HGRN_KDIM, (hh + 1) * HGRN_KDIM)
                att = lax.dot_general(qe[:, sl], ke[:, sl], _NT, preferred_element_type=F32)
                att = jnp.where(causal, att, 0.0)
                st = st_ref[hh]
                o = jnp.dot(att.astype(BF16), ivb[:, sl], preferred_element_type=F32)
                o = o + lax.dot_general(qb[:, sl], st.astype(BF16), _NT, preferred_element_type=F32)
                st_ref[hh] = st * dec[:, sl] + lax.dot_general(
                    ivb[:, sl], k2[:, sl], _TN, preferred_element_type=F32)
                outs.append(o * lax.rsqrt(jnp.mean(o * o, axis=-1, keepdims=True) + EPS))
            o = jnp.concatenate(outs, axis=1) * gn * _silu(og)
            oa_ref[0, r0:r0 + CHUNK, :] = o.astype(BF16)
        s5_input()

    @pl.when(lax.rem(i, 2) == 0)
    def _():
        step(proj_b, proj_a)

    @pl.when(lax.rem(i, 2) == 1)
    def _():
        step(proj_a, proj_b)


def _mix_front_call(x, mod, w_main, w_ut, lb, gn):
    bsz, seq, d = x.shape
    fdim = HGRN_HEADS * HGRN_KDIM
    ncols = w_main.shape[1]
    nb = w_ut.shape[0]
    groups = nb // S5_GROUP
    ltri = jnp.tril(jnp.ones((CHUNK, CHUNK), BF16))
    ts = SEQ_TILE
    tiles = seq // ts
    n_tiles = bsz * tiles
    rows = ts // CHUNK

    def cur(i):
        return jnp.minimum(i, n_tiles - 1)

    def prev(i):
        return jnp.maximum(i - 1, 0)

    return pl.pallas_call(
        functools.partial(_mix_front_kernel, tiles_per_seq=tiles),
        out_shape=(jax.ShapeDtypeStruct((groups, bsz * seq // CHUNK, S5_GROUP * CHUNK), F32),
                   jax.ShapeDtypeStruct((bsz, seq, fdim), BF16),
                   jax.ShapeDtypeStruct((nb, bsz * seq), BF16)),
        grid=(n_tiles + 1,),
        in_specs=[pl.BlockSpec((1, ts, d), lambda i: (cur(i) // tiles, cur(i) % tiles, 0)),
                  pl.BlockSpec((1, 6, d), lambda i: (cur(i) // tiles, 0, 0)),
                  pl.BlockSpec((d, ncols), lambda i: (0, 0)),
                  pl.BlockSpec((nb, d), lambda i: (0, 0)),
                  pl.BlockSpec((1, fdim), lambda i: (0, 0)),
                  pl.BlockSpec((1, fdim), lambda i: (0, 0)),
                  pl.BlockSpec((CHUNK, CHUNK), lambda i: (0, 0))],
        out_specs=(pl.BlockSpec((groups, rows, S5_GROUP * CHUNK), lambda i: (0, cur(i), 0)),
                   pl.BlockSpec((1, ts, fdim), lambda i: (prev(i) // tiles, prev(i) % tiles, 0)),
                   pl.BlockSpec((nb, ts), lambda i: (0, cur(i)))),
        scratch_shapes=[pltpu.VMEM((ts, ncols), F32), pltpu.VMEM((ts, ncols), F32),
                        pltpu.VMEM((HGRN_HEADS, fdim // HGRN_HEADS, HGRN_KDIM), F32),
                        pltpu.VMEM((rows * nb // 2, LANES), F32),
                        pltpu.VMEM((ts, d), BF16)],
        compiler_params=pltpu.CompilerParams(
            dimension_semantics=("arbitrary",), vmem_limit_bytes=VMEM_LIMIT),
        name="mix_front",
    )(x, mod, w_main, w_ut, lb, gn, ltri)


def _s5_tables(lam_re, lam_im, log_dt, b_re, b_im, c_re, c_im):
    t = CHUNK
    lam = lax.complex(jnp.minimum(lam_re, -1e-4), lam_im)
    lam_dt = lam * jnp.exp(log_dt)[:, None]
    lam_bar = jnp.exp(lam_dt)
    b_bar = ((lam_bar - 1.0) / lam)[..., None] * lax.complex(b_re, b_im)
    c_mat = lax.complex(c_re, c_im)
    taus = jnp.arange(t + 1, dtype=F32)
    lam_pow = jnp.exp(lam_dt[:, None, :] * taus[None, :, None])
    g, p = lam.shape
    c = b_re.shape[-1]
    cb = c_mat[:, None, :, :] * b_bar.transpose(0, 2, 1)[:, :, None, :]
    cb = jnp.concatenate([cb.real, -cb.imag], axis=-1)
    cb = cb.reshape(g, c, c // 2, 2 * 2 * p).reshape(g, c * c // 2, 4 * p)
    lp = jnp.concatenate([lam_pow[:, :t].real, lam_pow[:, :t].imag], axis=-1).transpose(0, 2, 1)
    zero = jnp.zeros_like(lp)
    lp2 = jnp.concatenate([jnp.concatenate([lp, zero], axis=2),
                           jnp.concatenate([zero, lp], axis=2)], axis=1)
    pc = lam_pow[:, t - 1::-1][:, :t, :, None] * b_bar[:, None, :, :]
    pc = pc.transpose(0, 3, 1, 2).reshape(g, c * t, p)
    p_tab = jnp.concatenate([pc.real, pc.imag], axis=-1)
    ql = c_mat[:, None, :, :] * lam_pow[:, 1:t + 1, None, :]
    ql = ql.transpose(0, 3, 2, 1).reshape(g, p, c * t)
    q_tab = jnp.concatenate([ql.real, -ql.imag], axis=1)
    lam_t = lam_pow[:, t]
    a1 = jnp.concatenate([lam_t.real, lam_t.real], axis=-1)[:, None, :]
    a2 = jnp.concatenate([-lam_t.imag, lam_t.imag], axis=-1)[:, None, :]
    return cb, lp2, p_tab.astype(BF16), q_tab.astype(BF16), a1, a2


def _split3(v):
    hi = v.astype(BF16)
    rem = v - hi.astype(F32)
    mid = rem.astype(BF16)
    return hi, mid, (rem - mid.astype(F32)).astype(BF16)


def _s5_kernel(u_ref, cb_ref, lp_ref, p_ref, q_ref, a1_ref, a2_ref, y_ref, v_ref, xs_ref, m_ref, k_ref,
               *, n_chunks, n_batch):
    c_hi, c_mid, c_lo = _split3(cb_ref[0])
    l_hi, l_mid, l_lo = _split3(lp_ref[0])
    k_ref[...] = (jnp.dot(c_hi, l_hi, preferred_element_type=F32)
                  + jnp.dot(c_hi, l_mid, preferred_element_type=F32)
                  + jnp.dot(c_mid, l_hi, preferred_element_type=F32)
                  + jnp.dot(c_hi, l_lo, preferred_element_type=F32)
                  + jnp.dot(c_mid, l_mid, preferred_element_type=F32)
                  + jnp.dot(c_lo, l_hi, preferred_element_type=F32))
    n_pairs = m_ref.shape[1] // LANES
    n_in = k_ref.shape[0] // n_pairs
    lane = lax.broadcasted_iota(I32, (CHUNK, LANES), 1)
    causal = (lane & (CHUNK - 1)) >= lax.broadcasted_iota(I32, (CHUNK, LANES), 0)
    for ci in range(n_in):
        for a in range(n_pairs):
            lags = jnp.broadcast_to(k_ref[ci * n_pairs + a:ci * n_pairs + a + 1, :], (CHUNK, LANES))
            tile = pltpu.roll(lags, 0, 1, stride=1, stride_axis=0)
            m_ref[ci * CHUNK:(ci + 1) * CHUNK, a * LANES:(a + 1) * LANES] = jnp.where(
                causal, tile, 0.0).astype(BF16)
    u = u_ref[0].astype(BF16)
    v_ref[...] = jnp.dot(u, p_ref[0], preferred_element_type=F32)
    a1 = a1_ref[0]
    a2 = a2_ref[0]
    half = xs_ref.shape[1] // 2

    def step(n, state):
        xs_ref[pl.ds(n, n_batch, stride=n_chunks), :] = state
        return (a1 * state + a2 * pltpu.roll(state, half, 1)
                + v_ref[pl.ds(n, n_batch, stride=n_chunks), :])

    lax.fori_loop(0, n_chunks, step, jnp.zeros((n_batch, xs_ref.shape[1]), F32))
    y = jnp.dot(u, m_ref[...], preferred_element_type=F32)
    y_ref[0] = y + jnp.dot(xs_ref[...].astype(BF16), q_ref[0], preferred_element_type=F32)


def _s5_call(ut, cb, lp2, p_tab, q_tab, a1, a2, n_chunks, n_batch):
    g, rows, width = ut.shape
    p2 = p_tab.shape[-1]
    return pl.pallas_call(
        functools.partial(_s5_kernel, n_chunks=n_chunks, n_batch=n_batch),
        out_shape=jax.ShapeDtypeStruct((g, rows, width), F32),
        grid=(g,),
        in_specs=[pl.BlockSpec((1, rows, width), lambda i: (i, 0, 0)),
                  pl.BlockSpec((1,) + cb.shape[1:], lambda i: (i, 0, 0)),
                  pl.BlockSpec((1,) + lp2.shape[1:], lambda i: (i, 0, 0)),
                  pl.BlockSpec((1, width, p2), lambda i: (i, 0, 0)),
                  pl.BlockSpec((1, p2, width), lambda i: (i, 0, 0)),
                  pl.BlockSpec((1, 1, p2), lambda i: (i, 0, 0)),
                  pl.BlockSpec((1, 1, p2), lambda i: (i, 0, 0))],
        out_specs=pl.BlockSpec((1, rows, width), lambda i: (i, 0, 0)),
        scratch_shapes=[pltpu.VMEM((rows, p2), F32), pltpu.VMEM((rows, p2), F32),
                        pltpu.VMEM((width, width), BF16),
                        pltpu.VMEM((cb.shape[1], lp2.shape[2]), F32)],
        compiler_params=pltpu.CompilerParams(
            dimension_semantics=("arbitrary",), vmem_limit_bytes=VMEM_LIMIT),
        name="s5",
    )(ut, cb, lp2, p_tab, q_tab, a1, a2)


def _token_major(flat_ref, src_ref):
    groups, rows, _ = src_ref.shape
    per_group = S5_GROUP // 2
    pairs = groups * per_group
    for q in range(pairs):
        flat_ref[pl.ds(q, rows, stride=pairs), :] = (
            src_ref[q // per_group, :, (q % per_group) * LANES:(q % per_group + 1) * LANES])
    tiles = []
    for m in range(rows // 2):
        even, odd = _split_chunk_pairs(flat_ref[2 * m * pairs:(2 * m + 1) * pairs, :],
                                       flat_ref[(2 * m + 1) * pairs:(2 * m + 2) * pairs, :])
        tiles.append(jnp.concatenate([even, odd], axis=0))
    return jnp.concatenate(tiles, axis=1)


def _mix_back_kernel(x_ref, oa_ref, yt_ref, ucm_ref, mod_ref, dskip_ref, wglu_ref, bglu_ref,
                     wout_ref, wrh_ref, wrl_ref,
                     x1_ref, hp_ref, lt_ref, flat_ref):
    na = oa_ref.shape[1]
    y_t = _token_major(flat_ref, yt_ref)
    u_t = ucm_ref[...].astype(F32)
    z_t = jax.nn.gelu(y_t + dskip_ref[...] * u_t)
    gate_t = _sigmoid(jnp.dot(wglu_ref[...], z_t.astype(BF16), preferred_element_type=F32)
                      + bglu_ref[...])
    ob_t = (z_t * gate_t).astype(BF16)
    mixed = (jnp.dot(oa_ref[...], wout_ref[0:na, :], preferred_element_type=F32)
             + lax.dot_general(ob_t, wout_ref[na:, :], _TN, preferred_element_type=F32))
    x1 = x_ref[...] + mod_ref[0, 2:3, :] * mixed
    ms = jnp.mean(x1 * x1, axis=-1, keepdims=True)
    h2 = x1 * lax.rsqrt(ms + EPS) * (1.0 + mod_ref[0, 4:5, :]) + mod_ref[0, 3:4, :]
    _store_packed(hp_ref, h2, h2.shape[0])
    h_hi = h2.astype(BF16)
    h_lo = (h2 - h_hi.astype(F32)).astype(BF16)
    lt = lax.dot_general(wrh_ref[...], h_hi, _NT, preferred_element_type=F32)
    lt = lt + lax.dot_general(wrl_ref[...], h_hi, _NT, preferred_element_type=F32)
    lt = lt + lax.dot_general(wrh_ref[...], h_lo, _NT, preferred_element_type=F32)
    lt_ref[...] = lt
    x1_ref[...] = x1


def _mix_back_call(x2d, oa, yt, ucm, mod, dskip, wglu, bglu, wout, wrh, wrl, seq):
    n, d = x2d.shape
    nb = oa.shape[1]
    ne = wrh.shape[0]
    tm = TOK_TILE
    per_b = seq // tm
    groups, _, width = yt.shape
    rows = tm // CHUNK
    const = lambda i: (0, 0)
    flat_block = pl.BlockSpec((groups, rows, width), lambda i: (0, i, 0))
    return pl.pallas_call(
        _mix_back_kernel,
        out_shape=(jax.ShapeDtypeStruct((n, d), F32),
                   jax.ShapeDtypeStruct((n * PACK_ROWS, LANES), I32),
                   jax.ShapeDtypeStruct((ne, n), F32)),
        grid=(n // tm,),
        in_specs=[pl.BlockSpec((tm, d), lambda i: (i, 0)),
                  pl.BlockSpec((tm, nb), lambda i: (i, 0)),
                  flat_block,
                  pl.BlockSpec((nb, tm), lambda i: (0, i)),
                  pl.BlockSpec((1, 6, d), lambda i: (i // per_b, 0, 0)),
                  pl.BlockSpec((nb, 1), const),
                  pl.BlockSpec((nb, nb), const),
                  pl.BlockSpec((nb, 1), const),
                  pl.BlockSpec((d, d), const),
                  pl.BlockSpec((ne, d), const),
                  pl.BlockSpec((ne, d), const)],
        out_specs=(pl.BlockSpec((tm, d), lambda i: (i, 0)),
                   pl.BlockSpec((tm * PACK_ROWS, LANES), lambda i: (i, 0)),
                   pl.BlockSpec((ne, tm), lambda i: (0, i))),
        scratch_shapes=[pltpu.VMEM((rows * nb // 2, LANES), F32)],
        compiler_params=pltpu.CompilerParams(
            dimension_semantics=("arbitrary",), vmem_limit_bytes=VMEM_LIMIT),
        name="mix_back",
    )(x2d, oa, yt, ucm, mod, dskip, wglu, bglu, wout, wrh, wrl)


def _route_kernel(lt_ref, bias_ref, su_ref, idx_ref, w_ref, rank_ref, cnt_ref, run_ref):
    ne, tr = lt_ref.shape
    per_group = ne // N_EXPERT_GROUPS
    neg = -jnp.inf

    @pl.when(pl.program_id(0) == 0)
    def _():
        run_ref[...] = jnp.zeros_like(run_ref)

    s = _sigmoid(lt_ref[...])
    sel = s + bias_ref[...]
    gio = lax.broadcasted_iota(I32, (per_group, tr), 0)
    gscore = []
    for g in range(N_EXPERT_GROUPS):
        v = sel[g * per_group:(g + 1) * per_group, :]
        m1 = jnp.max(v, axis=0, keepdims=True)
        i1 = jnp.min(jnp.where(v == m1, gio, per_group), axis=0, keepdims=True)
        m2 = jnp.max(jnp.where(gio == i1, neg, v), axis=0, keepdims=True)
        gscore.append(m1 + m2)
    masked = []
    for g in range(N_EXPERT_GROUPS):
        ahead = jnp.zeros((1, tr), I32)
        for o in range(N_EXPERT_GROUPS):
            if o == g:
                continue
            wins = (gscore[o] >= gscore[g]) if o < g else (gscore[o] > gscore[g])
            ahead = ahead + wins.astype(I32)
        keep = ahead < TOPK_GROUPS
        masked.append(jnp.where(keep, sel[g * per_group:(g + 1) * per_group, :], neg))
    selm = jnp.concatenate(masked, axis=0)
    eio = lax.broadcasted_iota(I32, (ne, tr), 0)
    candidate = selm > neg
    idxs, ws = [], []
    for k in range(TOP_K):
        m = jnp.max(selm, axis=0, keepdims=True)
        ik = jnp.min(jnp.where(selm == m, eio, ne), axis=0, keepdims=True)
        onehot = eio == ik
        ws.append(jnp.sum(jnp.where(onehot, s, 0.0), axis=0, keepdims=True))
        selm = jnp.where(onehot, neg, selm)
        idxs.append(ik)
    hits = jnp.where(jnp.logical_and(candidate, selm == neg), 1.0, 0.0)
    wsum = ws[0]
    for k in range(1, TOP_K):
        wsum = wsum + ws[k]
    scale = ROUTE_SCALE / wsum
    ranks = jnp.dot(hits.astype(BF16), su_ref[...], preferred_element_type=F32) + run_ref[...]
    for k in range(TOP_K):
        idx_ref[k:k + 1, :] = idxs[k]
        w_ref[k:k + 1, :] = ws[k] * scale
        rk = jnp.sum(jnp.where(eio == idxs[k], ranks, 0.0), axis=0, keepdims=True)
        rank_ref[k:k + 1, :] = rk.astype(I32)
    run_ref[...] = run_ref[...] + jnp.sum(hits, axis=1, keepdims=True)
    cnt_ref[...] = run_ref[...]


def _route_call(lt, bias):
    ne, n = lt.shape
    tr = ROUTE_TILE
    su = jnp.triu(jnp.ones((tr, tr), F32), k=1).astype(BF16)
    return pl.pallas_call(
        _route_kernel,
        out_shape=(jax.ShapeDtypeStruct((TOP_K, n), I32),
                   jax.ShapeDtypeStruct((TOP_K, n), F32),
                   jax.ShapeDtypeStruct((TOP_K, n), I32),
                   jax.ShapeDtypeStruct((ne, 1), F32)),
        grid=(n // tr,),
        in_specs=[pl.BlockSpec((ne, tr), lambda i: (0, i)),
                  pl.BlockSpec((ne, 1), lambda i: (0, 0)),
                  pl.BlockSpec((tr, tr), lambda i: (0, 0))],
        out_specs=(pl.BlockSpec((TOP_K, tr), lambda i: (0, i)),
                   pl.BlockSpec((TOP_K, tr), lambda i: (0, i)),
                   pl.BlockSpec((TOP_K, tr), lambda i: (0, i)),
                   pl.BlockSpec((ne, 1), lambda i: (0, 0))),
        scratch_shapes=[pltpu.VMEM((ne, 1), F32)],
        compiler_params=pltpu.CompilerParams(
            dimension_semantics=("arbitrary",), vmem_limit_bytes=VMEM_LIMIT),
        name="route",
    )(lt, bias.reshape(ne, 1), su)


def _sc_mesh():
    return plsc.VectorSubcoreMesh(core_axis_name="c", subcore_axis_name="s",
                                  num_cores=SC_CORES, num_subcores=SC_SUBCORES)


def _sc_worker():
    return lax.axis_index("s") * SC_CORES + lax.axis_index("c")


def _dispatch_call(idx_win, rank_win, starts, hp):
    n = hp.shape[0]
    n_workers = SC_CORES * SC_SUBCORES
    wins_per_worker = n // SC_WINDOW // n_workers
    lanes = SC_LANES

    def body(hp_hbm, idx_hbm, rank_hbm, st_hbm, xs_hbm, pos_hbm,
             idx_a, idx_b, rank_a, rank_b, pos_a, pos_b, rows_a, rows_b, st_v,
             load_a, load_b, scatter_sem, pos_sem):
        first_win = _sc_worker() * wins_per_worker
        idx_v, rank_v, pos_v = (idx_a, idx_b), (rank_a, rank_b), (pos_a, pos_b)
        rows_v, load_sem = (rows_a, rows_b), (load_a, load_b)
        pltpu.sync_copy(st_hbm, st_v)

        def loads(win, slot):
            return (pltpu.make_async_copy(hp_hbm.at[pl.ds(win * SC_WINDOW, SC_WINDOW)], rows_v[slot],
                                          load_sem[slot]),
                    pltpu.make_async_copy(idx_hbm.at[win], idx_v[slot], load_sem[slot]),
                    pltpu.make_async_copy(rank_hbm.at[win], rank_v[slot], load_sem[slot]))

        for cp in loads(first_win, 0):
            cp.start()

        @pl.loop(0, wins_per_worker, step=2)
        def _(w):
            for slot in range(2):
                win = first_win + w + slot
                for cp in loads(win, slot):
                    cp.wait()

                @pl.when(w + slot + 1 < wins_per_worker)
                def _():
                    for cp in loads(win + 1, 1 - slot):
                        cp.start()

                for k in range(TOP_K):
                    for c in range(SC_WINDOW // lanes):
                        seg = pl.ds(c * lanes, lanes)
                        base = plsc.load_gather(st_v, [idx_v[slot][k, seg]])
                        pos_v[slot][k, seg] = base + rank_v[slot][k, seg]
                pos_out = pltpu.async_copy(pos_v[slot], pos_hbm.at[win], pos_sem)
                copies = [pltpu.async_copy(rows_v[slot], xs_hbm.at[pos_v[slot].at[k]], scatter_sem)
                          for k in range(TOP_K)]
                for cp in copies:
                    cp.wait()
                pos_out.wait()

    idx_buf = pltpu.VMEM((TOP_K, SC_WINDOW), I32)
    row_buf = pltpu.VMEM((SC_WINDOW,) + hp.shape[1:], hp.dtype)
    return pl.kernel(
        body,
        out_type=(jax.ShapeDtypeStruct((n * TOP_K,) + hp.shape[1:], hp.dtype),
                  jax.ShapeDtypeStruct(idx_win.shape, I32)),
        mesh=_sc_mesh(),
        scratch_types=[idx_buf, idx_buf, idx_buf, idx_buf, idx_buf, idx_buf, row_buf, row_buf,
                       pltpu.VMEM(starts.shape, I32),
                       pltpu.SemaphoreType.DMA, pltpu.SemaphoreType.DMA, pltpu.SemaphoreType.DMA,
                       pltpu.SemaphoreType.DMA],
        compiler_params=pltpu.CompilerParams(needs_layout_passes=False),
        name="dispatch",
    )(hp, idx_win, rank_win, starts)


def _collect_call(pos_win, y_sorted):
    n = pos_win.shape[0] * SC_WINDOW
    n_workers = SC_CORES * SC_SUBCORES
    wins_per_worker = n // SC_WINDOW // n_workers

    def body(ys_hbm, pos_hbm, out_hbm, idx_v, rows_a, rows_b, gather_sem, write_sem):
        first_win = _sc_worker() * wins_per_worker
        bufs = (rows_a, rows_b)

        @pl.loop(0, wins_per_worker)
        def _(w):
            win = first_win + w
            pltpu.sync_copy(pos_hbm.at[win], idx_v)

            def gather(k):
                return pltpu.async_copy(ys_hbm.at[idx_v.at[k]], bufs[k % 2], gather_sem)

            pending_gather = gather(0)
            pending_write = None
            for k in range(TOP_K):
                pending_gather.wait()
                if pending_write is not None:
                    for cp in pending_write:
                        cp.wait()
                if k + 1 < TOP_K:
                    pending_gather = gather(k + 1)
                pending_write = [
                    pltpu.async_copy(bufs[k % 2].at[:, j],
                                     out_hbm.at[j, k, pl.ds(win * SC_WINDOW, SC_WINDOW)], write_sem)
                    for j in range(PACK_ROWS)]
            for cp in pending_write:
                cp.wait()

    row_buf = pltpu.VMEM((SC_WINDOW,) + y_sorted.shape[1:], y_sorted.dtype)
    return pl.kernel(
        body,
        out_type=jax.ShapeDtypeStruct((PACK_ROWS, TOP_K, n, LANES), y_sorted.dtype),
        mesh=_sc_mesh(),
        scratch_types=[pltpu.VMEM((TOP_K, SC_WINDOW), I32), row_buf, row_buf,
                       pltpu.SemaphoreType.DMA, pltpu.SemaphoreType.DMA],
        name="collect",
    )(y_sorted, pos_win)


def _experts_kernel(pe_ref, pb_ref, pv_ref, st_ref, en_ref, nx_ref,
                    xs_hbm, wg_hbm, wu_hbm, wd_hbm, y_hbm,
                    wgb_ref, wub_ref, wdb_ref, wgf_ref, wuf_ref, wdf_ref, xbuf_ref, stage_ref,
                    in_sems, out_sems, w_sems, state_ref):
    p = pl.program_id(0)
    n_pairs = pl.num_programs(0)
    e = pe_ref[p]
    blk = pb_ref[p]
    rb = xbuf_ref.shape[2]
    prev = jnp.maximum(p - 1, 0)
    nxt = jnp.minimum(p + 1, n_pairs - 1)
    live = pv_ref[p] == 1
    first = jnp.logical_or(p == 0, pb_ref[prev] != blk)
    block_ends = jnp.logical_or(pb_ref[nxt] != blk, pv_ref[nxt] == 0)
    last = jnp.logical_or(p == n_pairs - 1, block_ends)

    def in_copy(j, block, slot):
        return pltpu.make_async_copy(xs_hbm.at[pl.ds(block * rb, rb), j], xbuf_ref.at[slot, j],
                                     in_sems.at[slot])

    def out_copy(j, block, slot):
        return pltpu.make_async_copy(stage_ref.at[slot, j], y_hbm.at[pl.ds(block * rb, rb), j],
                                     out_sems.at[slot])

    def drain(slot):
        @pl.when(state_ref[1 + slot] == 1)
        def _():
            for j in range(PACK_ROWS):
                out_copy(j, 0, slot).wait()
            state_ref[1 + slot] = 0

    @pl.when(p == 0)
    def _():
        state_ref[0] = 0
        state_ref[1] = 0
        state_ref[2] = 0
        stage_ref[...] = jnp.zeros_like(stage_ref)
        for j in range(PACK_ROWS):
            in_copy(j, blk, 0).start()

    slot = jnp.where(jnp.logical_and(first, p > 0), 1 - state_ref[0], state_ref[0])
    state_ref[0] = slot

    @pl.when(jnp.logical_and(first, blk + 1 < y_hbm.shape[0] // rb))
    def _():
        for j in range(PACK_ROWS):
            in_copy(j, blk + 1, 1 - slot).start()

    @pl.when(first)
    def _():
        for j in range(PACK_ROWS):
            in_copy(j, blk, slot).wait()
        drain(slot)

    def w_copies(expert, wslot):
        return (pltpu.make_async_copy(wg_hbm.at[expert], wgf_ref.at[wslot], w_sems.at[wslot]),
                pltpu.make_async_copy(wu_hbm.at[expert], wuf_ref.at[wslot], w_sems.at[wslot]),
                pltpu.make_async_copy(wd_hbm.at[expert], wdf_ref.at[wslot], w_sems.at[wslot]))

    @pl.when(p == 0)
    def _():
        state_ref[3] = 0
        for cp in w_copies(e, 0):
            cp.start()

    new_expert = jnp.logical_or(p == 0, pe_ref[prev] != e)
    wslot = jnp.where(jnp.logical_and(new_expert, p > 0), 1 - state_ref[3], state_ref[3])
    state_ref[3] = wslot

    @pl.when(new_expert)
    def _():
        for cp in w_copies(e, wslot):
            cp.wait()
        wgb_ref[...] = wgf_ref[wslot].astype(BF16)
        wub_ref[...] = wuf_ref[wslot].astype(BF16)
        wdb_ref[...] = wdf_ref[wslot].astype(BF16)

        @pl.when(nx_ref[e] != e)
        def _():
            for cp in w_copies(nx_ref[e], 1 - wslot):
                cp.start()

    lo_row = st_ref[e]
    hi_row = en_ref[e]

    def sub_block(s, row0, shared):
        sub = pl.ds(s * SUB_BLOCK, SUB_BLOCK)
        pieces = []
        for j in range(PACK_ROWS):
            w = xbuf_ref[slot, j, sub, :]
            pieces.append(lax.bitcast_convert_type(w.astype(jnp.int16), BF16))
            pieces.append(lax.bitcast_convert_type(
                lax.shift_right_logical(w, 16).astype(jnp.int16), BF16))
        xb = jnp.concatenate(pieces, axis=1)
        gate = jnp.dot(xb, wgb_ref[...], preferred_element_type=F32)
        up = jnp.dot(xb, wub_ref[...], preferred_element_type=F32)
        yb = jnp.dot((_silu(gate) * up).astype(BF16), wdb_ref[...], preferred_element_type=F32)
        if shared:
            rows = row0 + lax.broadcasted_iota(I32, (SUB_BLOCK, 1), 0)
            mine = jnp.logical_and(rows >= lo_row, rows < hi_row)
        for j in range(PACK_ROWS):
            word = pltpu.pack_elementwise(
                [yb[:, 2 * j * LANES:(2 * j + 1) * LANES], yb[:, (2 * j + 1) * LANES:(2 * j + 2) * LANES]],
                packed_dtype=BF16)
            if shared:
                word = jnp.where(mine, word, stage_ref[slot, j, sub, :])
            stage_ref[slot, j, sub, :] = word

    def one(s):
        row0 = blk * rb + s * SUB_BLOCK
        touched = jnp.logical_and(live, jnp.logical_and(row0 < hi_row, row0 + SUB_BLOCK > lo_row))
        whole = jnp.logical_and(lo_row <= row0, hi_row >= row0 + SUB_BLOCK)

        @pl.when(jnp.logical_and(touched, whole))
        def _():
            sub_block(s, row0, shared=False)

        @pl.when(jnp.logical_and(touched, jnp.logical_not(whole)))
        def _():
            sub_block(s, row0, shared=True)

    for s in range(0, rb // SUB_BLOCK, 2):
        row0 = blk * rb + s * SUB_BLOCK
        both = jnp.logical_and(live, jnp.logical_and(lo_row <= row0, hi_row >= row0 + 2 * SUB_BLOCK))

        @pl.when(both)
        def _():
            sub_block(s, row0, shared=False)
            sub_block(s + 1, row0 + SUB_BLOCK, shared=False)

        @pl.when(jnp.logical_not(both))
        def _():
            one(s)
            one(s + 1)

    @pl.when(jnp.logical_and(live, last))
    def _():
        for j in range(PACK_ROWS):
            out_copy(j, blk, slot).start()
        state_ref[1 + slot] = 1

    @pl.when(p == n_pairs - 1)
    def _():
        drain(0)
        drain(1)


def _experts_call(pair_e, pair_blk, pair_ok, starts, ends, next_e, xs, w_gate, w_up, w_down):
    ne, d, de = w_gate.shape
    rb = ROW_BLOCK
    n_pairs = pair_e.shape[0]
    anywhere = pl.BlockSpec(memory_space=pl.ANY)
    grid_spec = pltpu.PrefetchScalarGridSpec(
        num_scalar_prefetch=6,
        grid=(n_pairs,),
        in_specs=[anywhere, anywhere, anywhere, anywhere],
        out_specs=anywhere,
        scratch_shapes=[pltpu.VMEM((d, de), BF16), pltpu.VMEM((d, de), BF16),
                        pltpu.VMEM((de, d), BF16),
                        pltpu.VMEM((2, d, de), F32), pltpu.VMEM((2, d, de), F32),
                        pltpu.VMEM((2, de, d), F32),
                        pltpu.VMEM((2, PACK_ROWS, rb, LANES), I32),
                        pltpu.VMEM((2, PACK_ROWS, rb, LANES), I32),
                        pltpu.SemaphoreType.DMA((2,)), pltpu.SemaphoreType.DMA((2,)),
                        pltpu.SemaphoreType.DMA((2,)),
                        pltpu.SMEM((4,), I32)],
    )
    return pl.pallas_call(
        _experts_kernel,
        out_shape=jax.ShapeDtypeStruct(xs.shape, xs.dtype),
        grid_spec=grid_spec,
        compiler_params=pltpu.CompilerParams(
            dimension_semantics=("arbitrary",), vmem_limit_bytes=VMEM_LIMIT),
        name="experts",
    )(pair_e, pair_blk, pair_ok, starts, ends, next_e, xs, w_gate, w_up, w_down)


def _combine_kernel(y0_ref, y1_ref, y2_ref, y3_ref, x1_ref, hp_ref, w_ref, g2_ref, fg_ref,
                    wsg_ref, wsu_ref, wsd_ref, o_ref):
    tm = x1_ref.shape[0]
    w = w_ref[...].T
    parts = []
    for y_ref in (y0_ref, y1_ref, y2_ref, y3_ref):
        for half in range(2):
            acc = None
            for k in range(TOP_K):
                piece = pltpu.unpack_elementwise(y_ref[k], index=half, packed_dtype=BF16,
                                                 unpacked_dtype=F32) * w[:, k:k + 1]
                acc = piece if acc is None else acc + piece
            parts.append(acc)
    routed = jnp.concatenate(parts, axis=1)
    pieces = []
    for j in range(PACK_ROWS):
        word = hp_ref[pl.ds(j, tm, stride=PACK_ROWS), :]
        pieces.append(lax.bitcast_convert_type(word.astype(jnp.int16), BF16))
        pieces.append(lax.bitcast_convert_type(lax.shift_right_logical(word, 16).astype(jnp.int16), BF16))
    h2 = jnp.concatenate(pieces, axis=1)
    hid = _silu(jnp.dot(h2, wsg_ref[...], preferred_element_type=F32)) * jnp.dot(
        h2, wsu_ref[...], preferred_element_type=F32)
    shared = jnp.dot(hid.astype(BF16), wsd_ref[...], preferred_element_type=F32)
    x2 = x1_ref[...] + g2_ref[0] * (routed + shared)
    ms = jnp.mean(x2 * x2, axis=-1, keepdims=True)
    o_ref[...] = x2 * lax.rsqrt(ms + EPS) * fg_ref[...]


def _combine_kernel_into(*refs):
    _combine_kernel(*refs[:-2], refs[-1])


def _combine_call(y_tok, x1, hp, w_tok, g2, fgain, wsg, wsu, wsd, seq, token0, prev_out):
    n, d = x1.shape
    dsh = wsg.shape[1]
    tm = MOVE_TILE
    per_b = seq // tm
    tiles = y_tok.shape[2] // tm
    t0 = token0 // tm
    const = lambda i: (0, 0)

    def piece_spec(j):
        return pl.BlockSpec((None, TOP_K, tm, LANES), lambda i: (j, 0, i, 0))

    in_specs = [piece_spec(j) for j in range(PACK_ROWS)] + [
        pl.BlockSpec((tm, d), lambda i: (t0 + i, 0)),
        pl.BlockSpec((tm * PACK_ROWS, LANES), lambda i: (t0 + i, 0)),
        pl.BlockSpec((TOP_K, tm), lambda i: (0, t0 + i)),
        pl.BlockSpec((1, 1, d), lambda i: ((t0 + i) // per_b, 0, 0)),
        pl.BlockSpec((1, d), const),
        pl.BlockSpec((d, dsh), const),
        pl.BlockSpec((d, dsh), const),
        pl.BlockSpec((dsh, d), const)]
    args = [y_tok, y_tok, y_tok, y_tok, x1, hp, w_tok, g2, fgain, wsg, wsu, wsd]
    aliases = {}
    body = _combine_kernel
    if prev_out is not None:
        in_specs.append(pl.BlockSpec(memory_space=pl.ANY))
        args.append(prev_out)
        aliases = {len(args) - 1: 0}
        body = _combine_kernel_into
    return pl.pallas_call(
        body,
        out_shape=jax.ShapeDtypeStruct((n, d), F32),
        grid=(tiles,),
        in_specs=in_specs,
        out_specs=pl.BlockSpec((tm, d), lambda i: (t0 + i, 0)),
        input_output_aliases=aliases,
        compiler_params=pltpu.CompilerParams(
            dimension_semantics=("arbitrary",), vmem_limit_bytes=VMEM_LIMIT),
        name="combine",
    )(*args)


def _pair_tables(counts, n_rows):
    ne = counts.shape[0]
    sizes = counts.astype(I32)
    ends = jnp.cumsum(sizes)
    starts = ends - sizes
    first_blk = starts // ROW_BLOCK
    last_blk = (ends - 1) // ROW_BLOCK
    n_pairs = jnp.where(sizes > 0, last_blk - first_blk + 1, 0)
    pair_end = jnp.cumsum(n_pairs)
    pair_start = pair_end - n_pairs
    max_pairs = n_rows // ROW_BLOCK + ne
    p = jnp.arange(max_pairs, dtype=I32)
    ok = p < pair_end[-1]
    pc = jnp.minimum(p, pair_end[-1] - 1)
    pair_e = jnp.minimum(jnp.sum((pair_end[None, :] <= pc[:, None]).astype(I32), axis=1), ne - 1)
    pair_blk = (first_blk[pair_e] + pc - pair_start[pair_e]).astype(I32)
    eid = jnp.arange(ne, dtype=I32)
    later = lax.cummin(jnp.where(sizes > 0, eid, ne), reverse=True)
    next_e = jnp.concatenate([later[1:], jnp.full((1,), ne, I32)])
    next_e = jnp.where(next_e < ne, next_e, eid)
    return pair_e, pair_blk, ok.astype(I32), starts.astype(I32), ends.astype(I32), next_e


def kernel(x, c, w_ada, b_ada, w_in, lb_logits, hgrn_norm, lam_re, lam_im, log_dt, b_re, b_im,
           c_re, c_im, d_skip, w_glu, b_glu, w_out, w_router, router_bias, w_gate, w_up, w_down,
           ws_gate, ws_up, ws_down, final_gain):
    bsz, seq, d = x.shape
    n = bsz * seq
    fdim = HGRN_HEADS * HGRN_KDIM
    n_chunks = seq // CHUNK
    lb = jnp.cumsum(jax.nn.softmax(lb_logits.astype(F32), axis=0), axis=0)[0].reshape(1, fdim)

    mod = _mod_call(c, w_ada[0], b_ada[0]).reshape(bsz, 6, d)
    nb = w_in.shape[2] - 4 * fdim
    perm = jnp.concatenate([jnp.arange(0, nb, 2), jnp.arange(1, nb, 2)])
    w_in_b = w_in[0].astype(BF16)
    ut, out_a, ucm = _mix_front_call(x, mod, w_in_b[:, :4 * fdim], w_in_b[:, 4 * fdim:][:, perm].T, lb,
                                hgrn_norm[0].reshape(1, fdim))

    cb, lp2, p_tab, q_tab, a1, a2 = _s5_tables(lam_re[0], lam_im[0], log_dt[0], b_re[0], b_im[0],
                                             c_re[0], c_im[0])
    yt = _s5_call(ut, cb, lp2, p_tab, q_tab, a1, a2, n_chunks, bsz)

    wr_t = w_router[0].T
    wr_hi = wr_t.astype(BF16)
    wr_lo = (wr_t - wr_hi.astype(F32)).astype(BF16)
    w_out_b = w_out[0].astype(BF16)
    w_out_p = jnp.concatenate([w_out_b[:fdim], w_out_b[fdim:][perm]], axis=0)
    x1, hp, logits_t = _mix_back_call(
        x.reshape(n, d), out_a.reshape(n, fdim), yt, ucm, mod,
        d_skip[0][perm].reshape(nb, 1), w_glu[0][perm][:, perm].T.astype(BF16),
        b_glu[0][perm].reshape(nb, 1), w_out_p, wr_hi, wr_lo, seq)

    top_idx, top_w, rank, counts = _route_call(logits_t, router_bias[0])
    pair_e, pair_blk, pair_ok, starts, ends, next_e = _pair_tables(counts[:, 0], n * TOP_K)

    def windowed(a):
        return a.reshape(TOP_K, n // SC_WINDOW, SC_WINDOW).transpose(1, 0, 2)

    xs, pos_win = _dispatch_call(windowed(top_idx), windowed(rank), starts,
                                 hp.reshape(n, PACK_ROWS, LANES))
    y_sorted = _experts_call(pair_e, pair_blk, pair_ok, starts, ends, next_e, xs,
                             w_gate[0], w_up[0], w_down[0])
    w_tok = top_w
    g2 = mod[:, 5:6, :]
    fgain = final_gain.reshape(1, d)
    wsg, wsu, wsd = ws_gate[0].astype(BF16), ws_up[0].astype(BF16), ws_down[0].astype(BF16)
    unit = pos_win.shape[0] // sum(TAIL_PARTS)
    out = None
    win0 = 0
    for share in TAIL_PARTS:
        y_tok = _collect_call(pos_win[win0:win0 + share * unit], y_sorted)
        out = _combine_call(y_tok, x1, hp, w_tok, g2, fgain, wsg, wsu, wsd, seq, win0 * SC_WINDOW, out)
        win0 += share * unit
    return out.reshape(bsz, seq, d)
```

```python
import functools

import jax
import jax.numpy as jnp
from jax import lax
from jax.experimental import pallas as pl
from jax.experimental.pallas import tpu as pltpu
from jax.experimental.pallas import tpu_sc as plsc

F32 = jnp.float32
BF16 = jnp.bfloat16
I32 = jnp.int32

EPS = 1e-6
CHUNK = 64
HGRN_HEADS = 4
HGRN_KDIM = 128
S5_GROUP = 16
S5_STATE = 64
N_EXPERT_GROUPS = 8
TOPK_GROUPS = 4
TOP_K = 8
ROUTE_SCALE = 2.5
LANES = 128
PACK_ROWS = 4
SC_CORES = 2
SC_SUBCORES = 16
SC_LANES = 16
SC_WINDOW = 64

SEQ_TILE = 512
TOK_TILE = 512
ROUTE_TILE = 512
MOVE_TILE = 512
ROW_BLOCK = 2048
SUB_BLOCK = 512
TAIL_PARTS = 4
VMEM_LIMIT = 56 * 1024 * 1024

_NT = (((1,), (1,)), ((), ()))
_TN = (((0,), (0,)), ((), ()))


def _sigmoid(v):
    return 0.5 * jnp.tanh(0.5 * v) + 0.5


def _silu(v):
    return v * _sigmoid(v)


def _bdot(a, b):
    return jnp.dot(a.astype(BF16), b.astype(BF16), preferred_element_type=F32)


def _store_packed(ref, val, n_rows):
    for j in range(PACK_ROWS):
        lo = val[:, 2 * j * LANES:(2 * j + 1) * LANES]
        hi = val[:, (2 * j + 1) * LANES:(2 * j + 2) * LANES]
        ref[pl.ds(j, n_rows, stride=PACK_ROWS), :] = pltpu.pack_elementwise([lo, hi], packed_dtype=BF16)


def _mod_kernel(c_ref, w_ref, b_ref, o_ref):
    o_ref[...] = _bdot(_silu(c_ref[...]), w_ref[...]) + b_ref[...]


def _mod_call(c, w_ada, b_ada):
    bsz, d = c.shape
    n_out = w_ada.shape[1]
    return pl.pallas_call(
        _mod_kernel,
        out_shape=jax.ShapeDtypeStruct((bsz, n_out), F32),
        grid=(n_out // d,),
        in_specs=[pl.BlockSpec((bsz, d), lambda j: (0, 0)),
                  pl.BlockSpec((d, d), lambda j: (0, j)),
                  pl.BlockSpec((1, d), lambda j: (0, j))],
        out_specs=pl.BlockSpec((bsz, d), lambda j: (0, j)),
        compiler_params=pltpu.CompilerParams(vmem_limit_bytes=VMEM_LIMIT),
        name="mod",
    )(c, w_ada, b_ada.reshape(1, n_out))


def _split_chunk_pairs(tile_even, tile_odd):
    low = lax.broadcasted_iota(I32, tile_even.shape, 1) < CHUNK
    first = jnp.where(low, tile_even, pltpu.roll(tile_odd, CHUNK, 1))
    second = jnp.where(low, pltpu.roll(tile_even, CHUNK, 1), tile_odd)
    return first, second


def _mix_front_kernel(x_ref, mod_ref, win_ref, wut_ref, lb_ref, gn_ref, ltri_ref,
                      ut_ref, oa_ref, ucm_ref, proj_a, proj_b, st_ref, flat_ref, hb_ref, *, tiles_per_seq):
    fdim = HGRN_HEADS * HGRN_KDIM
    ts = x_ref.shape[1]
    pairs = wut_ref.shape[0] // 2
    rows = ts // CHUNK
    i = pl.program_id(0)

    @pl.when(i == 0)
    def _():
        proj_b[...] = jnp.zeros_like(proj_b)

    @pl.when(lax.rem(jnp.maximum(i - 1, 0), tiles_per_seq) == 0)
    def _():
        st_ref[...] = jnp.zeros_like(st_ref)

    def normalise():
        x = x_ref[0]
        ms = jnp.mean(x * x, axis=-1, keepdims=True)
        h = x * lax.rsqrt(ms + EPS) * (1.0 + mod_ref[0, 1:2, :]) + mod_ref[0, 0:1, :]
        hb_ref[...] = h.astype(BF16)

    def project_slab(proj_ref, n):
        width = proj_ref.shape[1] // rows
        cols = slice(n * width, (n + 1) * width)
        proj_ref[:, cols] = jnp.dot(hb_ref[...], win_ref[:, cols], preferred_element_type=F32)

    def s5_input():
        u_t = lax.dot_general(wut_ref[...], hb_ref[...], _NT, preferred_element_type=F32)
        ucm_ref[...] = u_t.astype(BF16)
        for m in range(ts // LANES):
            first, second = _split_chunk_pairs(u_t[:pairs, m * LANES:(m + 1) * LANES],
                                               u_t[pairs:, m * LANES:(m + 1) * LANES])
            flat_ref[2 * m * pairs:(2 * m + 1) * pairs, :] = first
            flat_ref[(2 * m + 1) * pairs:(2 * m + 2) * pairs, :] = second
        per_group = S5_GROUP // 2
        for q in range(pairs):
            ut_ref[q // per_group, :, (q % per_group) * LANES:(q % per_group + 1) * LANES] = (
                flat_ref[pl.ds(q, rows, stride=pairs), :])

    lb = lb_ref[...]
    gn = gn_ref[...]
    ltri = ltri_ref[...]
    row = lax.broadcasted_iota(I32, (CHUNK, CHUNK), 0)
    col = lax.broadcasted_iota(I32, (CHUNK, CHUNK), 1)
    causal = row >= col

    def step(proj_ref, next_ref):
        normalise()
        for ci in range(rows):
            project_slab(next_ref, ci)
            r0 = ci * CHUNK
            q = proj_ref[r0:r0 + CHUNK, 0:fdim]
            fl = proj_ref[r0:r0 + CHUNK, fdim:2 * fdim]
            iv = proj_ref[r0:r0 + CHUNK, 2 * fdim:3 * fdim]
            og = proj_ref[r0:r0 + CHUNK, 3 * fdim:4 * fdim]
            f = lb + (1.0 - lb) * _sigmoid(fl)
            lf_hi, lf_mid, lf_lo = _split3(jnp.log(f))
            b = (jnp.dot(ltri, lf_hi, preferred_element_type=F32)
                 + jnp.dot(ltri, lf_mid, preferred_element_type=F32)
                 + jnp.dot(ltri, lf_lo, preferred_element_type=F32))
            b_ref = b[CHUNK // 2 - 1:CHUNK // 2, :]
            b_last = b[CHUNK - 1:CHUNK, :]
            qs = _silu(q)
            kk = 1.0 - f
            qe = (qs * jnp.exp(b - b_ref)).astype(BF16)
            ke = (kk * jnp.exp(b_ref - b)).astype(BF16)
            qb = (qs * jnp.exp(b)).astype(BF16)
            k2 = (kk * jnp.exp(b_last - b)).astype(BF16)
            dec = jnp.exp(b_last)
            ivb = iv.astype(BF16)
            outs = []
            for hh in range(HGRN_HEADS):
                sl = slice(hh * HGRN_KDIM, (hh + 1) * HGRN_KDIM)
                att = lax.dot_general(qe[:, sl], ke[:, sl], _NT, preferred_element_type=F32)
                att = jnp.where(causal, att, 0.0)
                st = st_ref[hh]
                o = jnp.dot(att.astype(BF16), ivb[:, sl], preferred_element_type=F32)
                o = o + lax.dot_general(qb[:, sl], st.astype(BF16), _NT, preferred_element_type=F32)
                st_ref[hh] = st * dec[:, sl] + lax.dot_general(
                    ivb[:, sl], k2[:, sl], _TN, preferred_element_type=F32)
                outs.append(o * lax.rsqrt(jnp.mean(o * o, axis=-1, keepdims=True) + EPS))
            o = jnp.concatenate(outs, axis=1) * gn * _silu(og)
            oa_ref[0, r0:r0 + CHUNK, :] = o.astype(BF16)
        s5_input()

    @pl.when(lax.rem(i, 2) == 0)
    def _():
        step(proj_b, proj_a)

    @pl.when(lax.rem(i, 2) == 1)
    def _():
        step(proj_a, proj_b)


def _mix_front_call(x, mod, w_main, w_ut, lb, gn):
    bsz, seq, d = x.shape
    fdim = HGRN_HEADS * HGRN_KDIM
    ncols = w_main.shape[1]
    nb = w_ut.shape[0]
    groups = nb // S5_GROUP
    ltri = jnp.tril(jnp.ones((CHUNK, CHUNK), BF16))
    ts = SEQ_TILE
    tiles = seq // ts
    n_tiles = bsz * tiles
    rows = ts // CHUNK

    def cur(i):
        return jnp.minimum(i, n_tiles - 1)

    def prev(i):
        return jnp.maximum(i - 1, 0)

    return pl.pallas_call(
        functools.partial(_mix_front_kernel, tiles_per_seq=tiles),
        out_shape=(jax.ShapeDtypeStruct((groups, bsz * seq // CHUNK, S5_GROUP * CHUNK), F32),
                   jax.ShapeDtypeStruct((bsz, seq, fdim), BF16),
                   jax.ShapeDtypeStruct((nb, bsz * seq), BF16)),
        grid=(n_tiles + 1,),
        in_specs=[pl.BlockSpec((1, ts, d), lambda i: (cur(i) // tiles, cur(i) % tiles, 0)),
                  pl.BlockSpec((1, 6, d), lambda i: (cur(i) // tiles, 0, 0)),
                  pl.BlockSpec((d, ncols), lambda i: (0, 0)),
                  pl.BlockSpec((nb, d), lambda i: (0, 0)),
                  pl.BlockSpec((1, fdim), lambda i: (0, 0)),
                  pl.BlockSpec((1, fdim), lambda i: (0, 0)),
                  pl.BlockSpec((CHUNK, CHUNK), lambda i: (0, 0))],
        out_specs=(pl.BlockSpec((groups, rows, S5_GROUP * CHUNK), lambda i: (0, cur(i), 0)),
                   pl.BlockSpec((1, ts, fdim), lambda i: (prev(i) // tiles, prev(i) % tiles, 0)),
                   pl.BlockSpec((nb, ts), lambda i: (0, cur(i)))),
        scratch_shapes=[pltpu.VMEM((ts, ncols), F32), pltpu.VMEM((ts, ncols), F32),
                        pltpu.VMEM((HGRN_HEADS, fdim // HGRN_HEADS, HGRN_KDIM), F32),
                        pltpu.VMEM((rows * nb // 2, LANES), F32),
                        pltpu.VMEM((ts, d), BF16)],
        compiler_params=pltpu.CompilerParams(
            dimension_semantics=("arbitrary",), vmem_limit_bytes=VMEM_LIMIT),
        name="mix_front",
    )(x, mod, w_main, w_ut, lb, gn, ltri)


def _s5_tables(lam_re, lam_im, log_dt, b_re, b_im, c_re, c_im):
    t = CHUNK
    lam = lax.complex(jnp.minimum(lam_re, -1e-4), lam_im)
    lam_dt = lam * jnp.exp(log_dt)[:, None]
    lam_bar = jnp.exp(lam_dt)
    b_bar = ((lam_bar - 1.0) / lam)[..., None] * lax.complex(b_re, b_im)
    c_mat = lax.complex(c_re, c_im)
    taus = jnp.arange(t + 1, dtype=F32)
    lam_pow = jnp.exp(lam_dt[:, None, :] * taus[None, :, None])
    g, p = lam.shape
    c = b_re.shape[-1]
    cb = c_mat[:, None, :, :] * b_bar.transpose(0, 2, 1)[:, :, None, :]
    cb = jnp.concatenate([cb.real, -cb.imag], axis=-1)
    cb = cb.reshape(g, c, c // 2, 2 * 2 * p).reshape(g, c * c // 2, 4 * p)
    lp = jnp.concatenate([lam_pow[:, :t].real, lam_pow[:, :t].imag], axis=-1).transpose(0, 2, 1)
    zero = jnp.zeros_like(lp)
    lp2 = jnp.concatenate([jnp.concatenate([lp, zero], axis=2),
                           jnp.concatenate([zero, lp], axis=2)], axis=1)
    pc = lam_pow[:, t - 1::-1][:, :t, :, None] * b_bar[:, None, :, :]
    pc = pc.transpose(0, 3, 1, 2).reshape(g, c * t, p)
    p_tab = jnp.concatenate([pc.real, pc.imag], axis=-1)
    ql = c_mat[:, None, :, :] * lam_pow[:, 1:t + 1, None, :]
    ql = ql.transpose(0, 3, 2, 1).reshape(g, p, c * t)
    q_tab = jnp.concatenate([ql.real, -ql.imag], axis=1)
    lam_t = lam_pow[:, t]
    a1 = jnp.concatenate([lam_t.real, lam_t.real], axis=-1)[:, None, :]
    a2 = jnp.concatenate([-lam_t.imag, lam_t.imag], axis=-1)[:, None, :]
    return cb, lp2, p_tab.astype(BF16), q_tab.astype(BF16), a1, a2


def _split3(v):
    hi = v.astype(BF16)
    rem = v - hi.astype(F32)
    mid = rem.astype(BF16)
    return hi, mid, (rem - mid.astype(F32)).astype(BF16)


def _s5_kernel(u_ref, cb_ref, lp_ref, p_ref, q_ref, a1_ref, a2_ref, y_ref, v_ref, xs_ref, m_ref, k_ref,
               *, n_chunks, n_batch):
    c_hi, c_mid, c_lo = _split3(cb_ref[0])
    l_hi, l_mid, l_lo = _split3(lp_ref[0])
    k_ref[...] = (jnp.dot(c_hi, l_hi, preferred_element_type=F32)
                  + jnp.dot(c_hi, l_mid, preferred_element_type=F32)
                  + jnp.dot(c_mid, l_hi, preferred_element_type=F32)
                  + jnp.dot(c_hi, l_lo, preferred_element_type=F32)
                  + jnp.dot(c_mid, l_mid, preferred_element_type=F32)
                  + jnp.dot(c_lo, l_hi, preferred_element_type=F32))
    n_pairs = m_ref.shape[1] // LANES
    n_in = k_ref.shape[0] // n_pairs
    lane = lax.broadcasted_iota(I32, (CHUNK, LANES), 1)
    causal = (lane & (CHUNK - 1)) >= lax.broadcasted_iota(I32, (CHUNK, LANES), 0)
    for ci in range(n_in):
        for a in range(n_pairs):
            lags = jnp.broadcast_to(k_ref[ci * n_pairs + a:ci * n_pairs + a + 1, :], (CHUNK, LANES))
            tile = pltpu.roll(lags, 0, 1, stride=1, stride_axis=0)
            m_ref[ci * CHUNK:(ci + 1) * CHUNK, a * LANES:(a + 1) * LANES] = jnp.where(
                causal, tile, 0.0).astype(BF16)
    u = u_ref[0].astype(BF16)
    v_ref[...] = jnp.dot(u, p_ref[0], preferred_element_type=F32)
    a1 = a1_ref[0]
    a2 = a2_ref[0]
    half = xs_ref.shape[1] // 2

    def step(n, state):
        xs_ref[pl.ds(n, n_batch, stride=n_chunks), :] = state
        return (a1 * state + a2 * pltpu.roll(state, half, 1)
                + v_ref[pl.ds(n, n_batch, stride=n_chunks), :])

    lax.fori_loop(0, n_chunks, step, jnp.zeros((n_batch, xs_ref.shape[1]), F32))
    y = jnp.dot(u, m_ref[...], preferred_element_type=F32)
    y_ref[0] = y + jnp.dot(xs_ref[...].astype(BF16), q_ref[0], preferred_element_type=F32)


def _s5_call(ut, cb, lp2, p_tab, q_tab, a1, a2, n_chunks, n_batch):
    g, rows, width = ut.shape
    p2 = p_tab.shape[-1]
    return pl.pallas_call(
        functools.partial(_s5_kernel, n_chunks=n_chunks, n_batch=n_batch),
        out_shape=jax.ShapeDtypeStruct((g, rows, width), F32),
        grid=(g,),
        in_specs=[pl.BlockSpec((1, rows, width), lambda i: (i, 0, 0)),
                  pl.BlockSpec((1,) + cb.shape[1:], lambda i: (i, 0, 0)),
                  pl.BlockSpec((1,) + lp2.shape[1:], lambda i: (i, 0, 0)),
                  pl.BlockSpec((1, width, p2), lambda i: (i, 0, 0)),
                  pl.BlockSpec((1, p2, width), lambda i: (i, 0, 0)),
                  pl.BlockSpec((1, 1, p2), lambda i: (i, 0, 0)),
                  pl.BlockSpec((1, 1, p2), lambda i: (i, 0, 0))],
        out_specs=pl.BlockSpec((1, rows, width), lambda i: (i, 0, 0)),
        scratch_shapes=[pltpu.VMEM((rows, p2), F32), pltpu.VMEM((rows, p2), F32),
                        pltpu.VMEM((width, width), BF16),
                        pltpu.VMEM((cb.shape[1], lp2.shape[2]), F32)],
        compiler_params=pltpu.CompilerParams(
            dimension_semantics=("arbitrary",), vmem_limit_bytes=VMEM_LIMIT),
        name="s5",
    )(ut, cb, lp2, p_tab, q_tab, a1, a2)


def _token_major(flat_ref, src_ref):
    groups, rows, _ = src_ref.shape
    per_group = S5_GROUP // 2
    pairs = groups * per_group
    for q in range(pairs):
        flat_ref[pl.ds(q, rows, stride=pairs), :] = (
            src_ref[q // per_group, :, (q % per_group) * LANES:(q % per_group + 1) * LANES])
    tiles = []
    for m in range(rows // 2):
        even, odd = _split_chunk_pairs(flat_ref[2 * m * pairs:(2 * m + 1) * pairs, :],
                                       flat_ref[(2 * m + 1) * pairs:(2 * m + 2) * pairs, :])
        tiles.append(jnp.concatenate([even, odd], axis=0))
    return jnp.concatenate(tiles, axis=1)


def _mix_back_kernel(x_ref, oa_ref, yt_ref, ucm_ref, mod_ref, dskip_ref, wglu_ref, bglu_ref,
                     wout_ref, wrh_ref, wrl_ref,
                     x1_ref, hp_ref, lt_ref, flat_ref):
    na = oa_ref.shape[1]
    y_t = _token_major(flat_ref, yt_ref)
    u_t = ucm_ref[...].astype(F32)
    z_t = jax.nn.gelu(y_t + dskip_ref[...] * u_t)
    gate_t = _sigmoid(jnp.dot(wglu_ref[...], z_t.astype(BF16), preferred_element_type=F32)
                      + bglu_ref[...])
    ob_t = (z_t * gate_t).astype(BF16)
    mixed = (jnp.dot(oa_ref[...], wout_ref[0:na, :], preferred_element_type=F32)
             + lax.dot_general(ob_t, wout_ref[na:, :], _TN, preferred_element_type=F32))
    x1 = x_ref[...] + mod_ref[0, 2:3, :] * mixed
    ms = jnp.mean(x1 * x1, axis=-1, keepdims=True)
    h2 = x1 * lax.rsqrt(ms + EPS) * (1.0 + mod_ref[0, 4:5, :]) + mod_ref[0, 3:4, :]
    _store_packed(hp_ref, h2, h2.shape[0])
    h_hi = h2.astype(BF16)
    h_lo = (h2 - h_hi.astype(F32)).astype(BF16)
    lt = lax.dot_general(wrh_ref[...], h_hi, _NT, preferred_element_type=F32)
    lt = lt + lax.dot_general(wrl_ref[...], h_hi, _NT, preferred_element_type=F32)
    lt = lt + lax.dot_general(wrh_ref[...], h_lo, _NT, preferred_element_type=F32)
    lt_ref[...] = lt
    x1_ref[...] = x1


def _mix_back_call(x2d, oa, yt, ucm, mod, dskip, wglu, bglu, wout, wrh, wrl, seq):
    n, d = x2d.shape
    nb = oa.shape[1]
    ne = wrh.shape[0]
    tm = TOK_TILE
    per_b = seq // tm
    groups, _, width = yt.shape
    rows = tm // CHUNK
    const = lambda i: (0, 0)
    flat_block = pl.BlockSpec((groups, rows, width), lambda i: (0, i, 0))
    return pl.pallas_call(
        _mix_back_kernel,
        out_shape=(jax.ShapeDtypeStruct((n, d), F32),
                   jax.ShapeDtypeStruct((n * PACK_ROWS, LANES), I32),
                   jax.ShapeDtypeStruct((ne, n), F32)),
        grid=(n // tm,),
        in_specs=[pl.BlockSpec((tm, d), lambda i: (i, 0)),
                  pl.BlockSpec((tm, nb), lambda i: (i, 0)),
                  flat_block,
                  pl.BlockSpec((nb, tm), lambda i: (0, i)),
                  pl.BlockSpec((1, 6, d), lambda i: (i // per_b, 0, 0)),
                  pl.BlockSpec((nb, 1), const),
                  pl.BlockSpec((nb, nb), const),
                  pl.BlockSpec((nb, 1), const),
                  pl.BlockSpec((d, d), const),
                  pl.BlockSpec((ne, d), const),
                  pl.BlockSpec((ne, d), const)],
        out_specs=(pl.BlockSpec((tm, d), lambda i: (i, 0)),
                   pl.BlockSpec((tm * PACK_ROWS, LANES), lambda i: (i, 0)),
                   pl.BlockSpec((ne, tm), lambda i: (0, i))),
        scratch_shapes=[pltpu.VMEM((rows * nb // 2, LANES), F32)],
        compiler_params=pltpu.CompilerParams(
            dimension_semantics=("arbitrary",), vmem_limit_bytes=VMEM_LIMIT),
        name="mix_back",
    )(x2d, oa, yt, ucm, mod, dskip, wglu, bglu, wout, wrh, wrl)


def _route_kernel(lt_ref, bias_ref, su_ref, idx_ref, w_ref, rank_ref, cnt_ref, run_ref):
    ne, tr = lt_ref.shape
    per_group = ne // N_EXPERT_GROUPS
    neg = -jnp.inf

    @pl.when(pl.program_id(0) == 0)
    def _():
        run_ref[...] = jnp.zeros_like(run_ref)

    s = _sigmoid(lt_ref[...])
    sel = s + bias_ref[...]
    gio = lax.broadcasted_iota(I32, (per_group, tr), 0)
    gscore = []
    for g in range(N_EXPERT_GROUPS):
        v = sel[g * per_group:(g + 1) * per_group, :]
        m1 = jnp.max(v, axis=0, keepdims=True)
        i1 = jnp.min(jnp.where(v == m1, gio, per_group), axis=0, keepdims=True)
        m2 = jnp.max(jnp.where(gio == i1, neg, v), axis=0, keepdims=True)
        gscore.append(m1 + m2)
    masked = []
    for g in range(N_EXPERT_GROUPS):
        ahead = jnp.zeros((1, tr), I32)
        for o in range(N_EXPERT_GROUPS):
            if o == g:
                continue
            wins = (gscore[o] >= gscore[g]) if o < g else (gscore[o] > gscore[g])
            ahead = ahead + wins.astype(I32)
        keep = ahead < TOPK_GROUPS
        masked.append(jnp.where(keep, sel[g * per_group:(g + 1) * per_group, :], neg))
    selm = jnp.concatenate(masked, axis=0)
    eio = lax.broadcasted_iota(I32, (ne, tr), 0)
    candidate = selm > neg
    idxs, ws = [], []
    for k in range(TOP_K):
        m = jnp.max(selm, axis=0, keepdims=True)
        ik = jnp.min(jnp.where(selm == m, eio, ne), axis=0, keepdims=True)
        onehot = eio == ik
        ws.append(jnp.sum(jnp.where(onehot, s, 0.0), axis=0, keepdims=True))
        selm = jnp.where(onehot, neg, selm)
        idxs.append(ik)
    hits = jnp.where(jnp.logical_and(candidate, selm == neg), 1.0, 0.0)
    wsum = ws[0]
    for k in range(1, TOP_K):
        wsum = wsum + ws[k]
    scale = ROUTE_SCALE / wsum
    ranks = jnp.dot(hits.astype(BF16), su_ref[...], preferred_element_type=F32) + run_ref[...]
    for k in range(TOP_K):
        idx_ref[k:k + 1, :] = idxs[k]
        w_ref[k:k + 1, :] = ws[k] * scale
        rk = jnp.sum(jnp.where(eio == idxs[k], ranks, 0.0), axis=0, keepdims=True)
        rank_ref[k:k + 1, :] = rk.astype(I32)
    run_ref[...] = run_ref[...] + jnp.sum(hits, axis=1, keepdims=True)
    cnt_ref[...] = run_ref[...]


def _route_call(lt, bias):
    ne, n = lt.shape
    tr = ROUTE_TILE
    su = jnp.triu(jnp.ones((tr, tr), F32), k=1).astype(BF16)
    return pl.pallas_call(
        _route_kernel,
        out_shape=(jax.ShapeDtypeStruct((TOP_K, n), I32),
                   jax.ShapeDtypeStruct((TOP_K, n), F32),
                   jax.ShapeDtypeStruct((TOP_K, n), I32),
                   jax.ShapeDtypeStruct((ne, 1), F32)),
        grid=(n // tr,),
        in_specs=[pl.BlockSpec((ne, tr), lambda i: (0, i)),
                  pl.BlockSpec((ne, 1), lambda i: (0, 0)),
                  pl.BlockSpec((tr, tr), lambda i: (0, 0))],
        out_specs=(pl.BlockSpec((TOP_K, tr), lambda i: (0, i)),
                   pl.BlockSpec((TOP_K, tr), lambda i: (0, i)),
                   pl.BlockSpec((TOP_K, tr), lambda i: (0, i)),
                   pl.BlockSpec((ne, 1), lambda i: (0, 0))),
        scratch_shapes=[pltpu.VMEM((ne, 1), F32)],
        compiler_params=pltpu.CompilerParams(
            dimension_semantics=("arbitrary",), vmem_limit_bytes=VMEM_LIMIT),
        name="route",
    )(lt, bias.reshape(ne, 1), su)


def _sc_mesh():
    return plsc.VectorSubcoreMesh(core_axis_name="c", subcore_axis_name="s",
                                  num_cores=SC_CORES, num_subcores=SC_SUBCORES)


def _sc_worker():
    return lax.axis_index("s") * SC_CORES + lax.axis_index("c")


def _dispatch_call(idx_win, rank_win, starts, hp):
    n = hp.shape[0]
    n_workers = SC_CORES * SC_SUBCORES
    wins_per_worker = n // SC_WINDOW // n_workers
    lanes = SC_LANES

    def body(hp_hbm, idx_hbm, rank_hbm, st_hbm, xs_hbm, pos_hbm,
             idx_a, idx_b, rank_a, rank_b, pos_a, pos_b, rows_a, rows_b, st_v,
             load_a, load_b, scatter_sem, pos_sem):
        first_win = _sc_worker() * wins_per_worker
        idx_v, rank_v, pos_v = (idx_a, idx_b), (rank_a, rank_b), (pos_a, pos_b)
        rows_v, load_sem = (rows_a, rows_b), (load_a, load_b)
        pltpu.sync_copy(st_hbm, st_v)

        def loads(win, slot):
            return (pltpu.make_async_copy(hp_hbm.at[pl.ds(win * SC_WINDOW, SC_WINDOW)], rows_v[slot],
                                          load_sem[slot]),
                    pltpu.make_async_copy(idx_hbm.at[win], idx_v[slot], load_sem[slot]),
                    pltpu.make_async_copy(rank_hbm.at[win], rank_v[slot], load_sem[slot]))

        for cp in loads(first_win, 0):
            cp.start()

        @pl.loop(0, wins_per_worker, step=2)
        def _(w):
            for slot in range(2):
                win = first_win + w + slot
                for cp in loads(win, slot):
                    cp.wait()

                @pl.when(w + slot + 1 < wins_per_worker)
                def _():
                    for cp in loads(win + 1, 1 - slot):
                        cp.start()

                for k in range(TOP_K):
                    for c in range(SC_WINDOW // lanes):
                        seg = pl.ds(c * lanes, lanes)
                        base = plsc.load_gather(st_v, [idx_v[slot][k, seg]])
                        pos_v[slot][k, seg] = base + rank_v[slot][k, seg]
                pos_out = pltpu.async_copy(pos_v[slot], pos_hbm.at[win], pos_sem)
                copies = [pltpu.async_copy(rows_v[slot], xs_hbm.at[pos_v[slot].at[k]], scatter_sem)
                          for k in range(TOP_K)]
                for cp in copies:
                    cp.wait()
                pos_out.wait()

    idx_buf = pltpu.VMEM((TOP_K, SC_WINDOW), I32)
    row_buf = pltpu.VMEM((SC_WINDOW,) + hp.shape[1:], hp.dtype)
    return pl.kernel(
        body,
        out_type=(jax.ShapeDtypeStruct((n * TOP_K,) + hp.shape[1:], hp.dtype),
                  jax.ShapeDtypeStruct(idx_win.shape, I32)),
        mesh=_sc_mesh(),
        scratch_types=[idx_buf, idx_buf, idx_buf, idx_buf, idx_buf, idx_buf, row_buf, row_buf,
                       pltpu.VMEM(starts.shape, I32),
                       pltpu.SemaphoreType.DMA, pltpu.SemaphoreType.DMA, pltpu.SemaphoreType.DMA,
                       pltpu.SemaphoreType.DMA],
        compiler_params=pltpu.CompilerParams(needs_layout_passes=False),
        name="dispatch",
    )(hp, idx_win, rank_win, starts)


def _collect_call(pos_win, y_sorted):
    n = pos_win.shape[0] * SC_WINDOW
    n_workers = SC_CORES * SC_SUBCORES
    wins_per_worker = n // SC_WINDOW // n_workers

    def body(ys_hbm, pos_hbm, out_hbm, idx_a, idx_b, rows_a, rows_b, idx_sem_a, idx_sem_b,
             gather_sem, write_sem):
        first_win = _sc_worker() * wins_per_worker
        bufs = (rows_a, rows_b)
        idx_v, idx_sem = (idx_a, idx_b), (idx_sem_a, idx_sem_b)

        def idx_load(win, slot):
            return pltpu.make_async_copy(pos_hbm.at[win], idx_v[slot], idx_sem[slot])

        idx_load(first_win, 0).start()

        @pl.loop(0, wins_per_worker, step=2)
        def _(w):
            for slot in range(2):
                win = first_win + w + slot
                idx_load(win, slot).wait()

                @pl.when(w + slot + 1 < wins_per_worker)
                def _():
                    idx_load(win + 1, 1 - slot).start()

                def gather(k):
                    return pltpu.async_copy(ys_hbm.at[idx_v[slot].at[k]], bufs[k % 2], gather_sem)

                pending_gather = gather(0)
                pending_write = None
                for k in range(TOP_K):
                    pending_gather.wait()
                    if pending_write is not None:
                        for cp in pending_write:
                            cp.wait()
                    if k + 1 < TOP_K:
                        pending_gather = gather(k + 1)
                    pending_write = [
                        pltpu.async_copy(bufs[k % 2].at[:, j],
                                         out_hbm.at[j, k, pl.ds(win * SC_WINDOW, SC_WINDOW)], write_sem)
                        for j in range(PACK_ROWS)]
                for cp in pending_write:
                    cp.wait()

    idx_buf = pltpu.VMEM((TOP_K, SC_WINDOW), I32)
    row_buf = pltpu.VMEM((SC_WINDOW,) + y_sorted.shape[1:], y_sorted.dtype)
    return pl.kernel(
        body,
        out_type=jax.ShapeDtypeStruct((PACK_ROWS, TOP_K, n, LANES), y_sorted.dtype),
        mesh=_sc_mesh(),
        scratch_types=[idx_buf, idx_buf, row_buf, row_buf,
                       pltpu.SemaphoreType.DMA, pltpu.SemaphoreType.DMA,
                       pltpu.SemaphoreType.DMA, pltpu.SemaphoreType.DMA],
        name="collect",
    )(y_sorted, pos_win)


def _experts_kernel(pe_ref, pb_ref, pv_ref, st_ref, en_ref, nx_ref,
                    xs_hbm, wg_hbm, wu_hbm, wd_hbm, y_hbm,
                    wgb_ref, wub_ref, wdb_ref, wgf_ref, wuf_ref, wdf_ref, xbuf_ref, stage_ref,
                    in_sems, out_sems, w_sems, state_ref):
    p = pl.program_id(0)
    n_pairs = pl.num_programs(0)
    e = pe_ref[p]
    blk = pb_ref[p]
    rb = xbuf_ref.shape[2]
    prev = jnp.maximum(p - 1, 0)
    nxt = jnp.minimum(p + 1, n_pairs - 1)
    live = pv_ref[p] == 1
    first = jnp.logical_or(p == 0, pb_ref[prev] != blk)
    block_ends = jnp.logical_or(pb_ref[nxt] != blk, pv_ref[nxt] == 0)
    last = jnp.logical_or(p == n_pairs - 1, block_ends)

    def in_copy(j, block, slot):
        return pltpu.make_async_copy(xs_hbm.at[pl.ds(block * rb, rb), j], xbuf_ref.at[slot, j],
                                     in_sems.at[slot])

    def out_copy(j, block, slot):
        return pltpu.make_async_copy(stage_ref.at[slot, j], y_hbm.at[pl.ds(block * rb, rb), j],
                                     out_sems.at[slot])

    def drain(slot):
        @pl.when(state_ref[1 + slot] == 1)
        def _():
            for j in range(PACK_ROWS):
                out_copy(j, 0, slot).wait()
            state_ref[1 + slot] = 0

    @pl.when(p == 0)
    def _():
        state_ref[0] = 0
        state_ref[1] = 0
        state_ref[2] = 0
        stage_ref[...] = jnp.zeros_like(stage_ref)
        for j in range(PACK_ROWS):
            in_copy(j, blk, 0).start()

    slot = jnp.where(jnp.logical_and(first, p > 0), 1 - state_ref[0], state_ref[0])
    state_ref[0] = slot

    @pl.when(jnp.logical_and(first, blk + 1 < y_hbm.shape[0] // rb))
    def _():
        for j in range(PACK_ROWS):
            in_copy(j, blk + 1, 1 - slot).start()

    @pl.when(first)
    def _():
        for j in range(PACK_ROWS):
            in_copy(j, blk, slot).wait()
        drain(slot)

    def w_copies(expert, wslot):
        return (pltpu.make_async_copy(wg_hbm.at[expert], wgf_ref.at[wslot], w_sems.at[wslot]),
                pltpu.make_async_copy(wu_hbm.at[expert], wuf_ref.at[wslot], w_sems.at[wslot]),
                pltpu.make_async_copy(wd_hbm.at[expert], wdf_ref.at[wslot], w_sems.at[wslot]))

    @pl.when(p == 0)
    def _():
        state_ref[3] = 0
        for cp in w_copies(e, 0):
            cp.start()

    new_expert = jnp.logical_or(p == 0, pe_ref[prev] != e)
    wslot = jnp.where(jnp.logical_and(new_expert, p > 0), 1 - state_ref[3], state_ref[3])
    state_ref[3] = wslot

    @pl.when(new_expert)
    def _():
        for cp in w_copies(e, wslot):
            cp.wait()
        wgb_ref[...] = wgf_ref[wslot].astype(BF16)
        wub_ref[...] = wuf_ref[wslot].astype(BF16)
        wdb_ref[...] = wdf_ref[wslot].astype(BF16)

        @pl.when(nx_ref[e] != e)
        def _():
            for cp in w_copies(nx_ref[e], 1 - wslot):
                cp.start()

    lo_row = st_ref[e]
    hi_row = en_ref[e]

    def sub_block(s, row0, shared):
        sub = pl.ds(s * SUB_BLOCK, SUB_BLOCK)
        pieces = []
        for j in range(PACK_ROWS):
            w = xbuf_ref[slot, j, sub, :]
            pieces.append(lax.bitcast_convert_type(w.astype(jnp.int16), BF16))
            pieces.append(lax.bitcast_convert_type(
                lax.shift_right_logical(w, 16).astype(jnp.int16), BF16))
        xb = jnp.concatenate(pieces, axis=1)
        gate = jnp.dot(xb, wgb_ref[...], preferred_element_type=F32)
        up = jnp.dot(xb, wub_ref[...], preferred_element_type=F32)
        yb = jnp.dot((_silu(gate) * up).astype(BF16), wdb_ref[...], preferred_element_type=F32)
        if shared:
            rows = row0 + lax.broadcasted_iota(I32, (SUB_BLOCK, 1), 0)
            mine = jnp.logical_and(rows >= lo_row, rows < hi_row)
        for j in range(PACK_ROWS):
            word = pltpu.pack_elementwise(
                [yb[:, 2 * j * LANES:(2 * j + 1) * LANES], yb[:, (2 * j + 1) * LANES:(2 * j + 2) * LANES]],
                packed_dtype=BF16)
            if shared:
                word = jnp.where(mine, word, stage_ref[slot, j, sub, :])
            stage_ref[slot, j, sub, :] = word

    def one(s):
        row0 = blk * rb + s * SUB_BLOCK
        touched = jnp.logical_and(live, jnp.logical_and(row0 < hi_row, row0 + SUB_BLOCK > lo_row))
        whole = jnp.logical_and(lo_row <= row0, hi_row >= row0 + SUB_BLOCK)

        @pl.when(jnp.logical_and(touched, whole))
        def _():
            sub_block(s, row0, shared=False)

        @pl.when(jnp.logical_and(touched, jnp.logical_not(whole)))
        def _():
            sub_block(s, row0, shared=True)

    for s in range(0, rb // SUB_BLOCK, 2):
        row0 = blk * rb + s * SUB_BLOCK
        both = jnp.logical_and(live, jnp.logical_and(lo_row <= row0, hi_row >= row0 + 2 * SUB_BLOCK))

        @pl.when(both)
        def _():
            sub_block(s, row0, shared=False)
            sub_block(s + 1, row0 + SUB_BLOCK, shared=False)

        @pl.when(jnp.logical_not(both))
        def _():
            one(s)
            one(s + 1)

    @pl.when(jnp.logical_and(live, last))
    def _():
        for j in range(PACK_ROWS):
            out_copy(j, blk, slot).start()
        state_ref[1 + slot] = 1

    @pl.when(p == n_pairs - 1)
    def _():
        drain(0)
        drain(1)


def _experts_call(pair_e, pair_blk, pair_ok, starts, ends, next_e, xs, w_gate, w_up, w_down):
    ne, d, de = w_gate.shape
    rb = ROW_BLOCK
    n_pairs = pair_e.shape[0]
    anywhere = pl.BlockSpec(memory_space=pl.ANY)
    grid_spec = pltpu.PrefetchScalarGridSpec(
        num_scalar_prefetch=6,
        grid=(n_pairs,),
        in_specs=[anywhere, anywhere, anywhere, anywhere],
        out_specs=anywhere,
        scratch_shapes=[pltpu.VMEM((d, de), BF16), pltpu.VMEM((d, de), BF16),
                        pltpu.VMEM((de, d), BF16),
                        pltpu.VMEM((2, d, de), F32), pltpu.VMEM((2, d, de), F32),
                        pltpu.VMEM((2, de, d), F32),
                        pltpu.VMEM((2, PACK_ROWS, rb, LANES), I32),
                        pltpu.VMEM((2, PACK_ROWS, rb, LANES), I32),
                        pltpu.SemaphoreType.DMA((2,)), pltpu.SemaphoreType.DMA((2,)),
                        pltpu.SemaphoreType.DMA((2,)),
                        pltpu.SMEM((4,), I32)],
    )
    return pl.pallas_call(
        _experts_kernel,
        out_shape=jax.ShapeDtypeStruct(xs.shape, xs.dtype),
        grid_spec=grid_spec,
        compiler_params=pltpu.CompilerParams(
            dimension_semantics=("arbitrary",), vmem_limit_bytes=VMEM_LIMIT),
        name="experts",
    )(pair_e, pair_blk, pair_ok, starts, ends, next_e, xs, w_gate, w_up, w_down)


def _combine_kernel(y0_ref, y1_ref, y2_ref, y3_ref, x1_ref, hp_ref, w_ref, g2_ref, fg_ref,
                    wsg_ref, wsu_ref, wsd_ref, o_ref):
    tm = x1_ref.shape[0]
    w = w_ref[...].T
    parts = []
    for y_ref in (y0_ref, y1_ref, y2_ref, y3_ref):
        for half in range(2):
            acc = None
            for k in range(TOP_K):
                piece = pltpu.unpack_elementwise(y_ref[k], index=half, packed_dtype=BF16,
                                                 unpacked_dtype=F32) * w[:, k:k + 1]
                acc = piece if acc is None else acc + piece
            parts.append(acc)
    routed = jnp.concatenate(parts, axis=1)
    pieces = []
    for j in range(PACK_ROWS):
        word = hp_ref[pl.ds(j, tm, stride=PACK_ROWS), :]
        pieces.append(lax.bitcast_convert_type(word.astype(jnp.int16), BF16))
        pieces.append(lax.bitcast_convert_type(lax.shift_right_logical(word, 16).astype(jnp.int16), BF16))
    h2 = jnp.concatenate(pieces, axis=1)
    hid = _silu(jnp.dot(h2, wsg_ref[...], preferred_element_type=F32)) * jnp.dot(
        h2, wsu_ref[...], preferred_element_type=F32)
    shared = jnp.dot(hid.astype(BF16), wsd_ref[...], preferred_element_type=F32)
    x2 = x1_ref[...] + g2_ref[0] * (routed + shared)
    ms = jnp.mean(x2 * x2, axis=-1, keepdims=True)
    o_ref[...] = x2 * lax.rsqrt(ms + EPS) * fg_ref[...]


def _combine_kernel_into(*refs):
    _combine_kernel(*refs[:-2], refs[-1])


def _combine_call(y_tok, x1, hp, w_tok, g2, fgain, wsg, wsu, wsd, seq, part, prev_out):
    n, d = x1.shape
    dsh = wsg.shape[1]
    tm = MOVE_TILE
    per_b = seq // tm
    tiles = y_tok.shape[2] // tm
    t0 = part * tiles
    const = lambda i: (0, 0)

    def piece_spec(j):
        return pl.BlockSpec((None, TOP_K, tm, LANES), lambda i: (j, 0, i, 0))

    in_specs = [piece_spec(j) for j in range(PACK_ROWS)] + [
        pl.BlockSpec((tm, d), lambda i: (t0 + i, 0)),
        pl.BlockSpec((tm * PACK_ROWS, LANES), lambda i: (t0 + i, 0)),
        pl.BlockSpec((TOP_K, tm), lambda i: (0, t0 + i)),
        pl.BlockSpec((1, 1, d), lambda i: ((t0 + i) // per_b, 0, 0)),
        pl.BlockSpec((1, d), const),
        pl.BlockSpec((d, dsh), const),
        pl.BlockSpec((d, dsh), const),
        pl.BlockSpec((dsh, d), const)]
    args = [y_tok, y_tok, y_tok, y_tok, x1, hp, w_tok, g2, fgain, wsg, wsu, wsd]
    aliases = {}
    body = _combine_kernel
    if prev_out is not None:
        in_specs.append(pl.BlockSpec(memory_space=pl.ANY))
        args.append(prev_out)
        aliases = {len(args) - 1: 0}
        body = _combine_kernel_into
    return pl.pallas_call(
        body,
        out_shape=jax.ShapeDtypeStruct((n, d), F32),
        grid=(tiles,),
        in_specs=in_specs,
        out_specs=pl.BlockSpec((tm, d), lambda i: (t0 + i, 0)),
        input_output_aliases=aliases,
        compiler_params=pltpu.CompilerParams(
            dimension_semantics=("arbitrary",), vmem_limit_bytes=VMEM_LIMIT),
        name="combine",
    )(*args)


def _pair_tables(counts, n_rows):
    ne = counts.shape[0]
    sizes = counts.astype(I32)
    ends = jnp.cumsum(sizes)
    starts = ends - sizes
    first_blk = starts // ROW_BLOCK
    last_blk = (ends - 1) // ROW_BLOCK
    n_pairs = jnp.where(sizes > 0, last_blk - first_blk + 1, 0)
    pair_end = jnp.cumsum(n_pairs)
    pair_start = pair_end - n_pairs
    max_pairs = n_rows // ROW_BLOCK + ne
    p = jnp.arange(max_pairs, dtype=I32)
    ok = p < pair_end[-1]
    pc = jnp.minimum(p, pair_end[-1] - 1)
    pair_e = jnp.minimum(jnp.sum((pair_end[None, :] <= pc[:, None]).astype(I32), axis=1), ne - 1)
    pair_blk = (first_blk[pair_e] + pc - pair_start[pair_e]).astype(I32)
    eid = jnp.arange(ne, dtype=I32)
    later = lax.cummin(jnp.where(sizes > 0, eid, ne), reverse=True)
    next_e = jnp.concatenate([later[1:], jnp.full((1,), ne, I32)])
    next_e = jnp.where(next_e < ne, next_e, eid)
    return pair_e, pair_blk, ok.astype(I32), starts.astype(I32), ends.astype(I32), next_e


def kernel(x, c, w_ada, b_ada, w_in, lb_logits, hgrn_norm, lam_re, lam_im, log_dt, b_re, b_im,
           c_re, c_im, d_skip, w_glu, b_glu, w_out, w_router, router_bias, w_gate, w_up, w_down,
           ws_gate, ws_up, ws_down, final_gain):
    bsz, seq, d = x.shape
    n = bsz * seq
    fdim = HGRN_HEADS * HGRN_KDIM
    n_chunks = seq // CHUNK
    lb = jnp.cumsum(jax.nn.softmax(lb_logits.astype(F32), axis=0), axis=0)[0].reshape(1, fdim)

    mod = _mod_call(c, w_ada[0], b_ada[0]).reshape(bsz, 6, d)
    nb = w_in.shape[2] - 4 * fdim
    perm = jnp.concatenate([jnp.arange(0, nb, 2), jnp.arange(1, nb, 2)])
    w_in_b = w_in[0].astype(BF16)
    ut, out_a, ucm = _mix_front_call(x, mod, w_in_b[:, :4 * fdim], w_in_b[:, 4 * fdim:][:, perm].T, lb,
                                hgrn_norm[0].reshape(1, fdim))

    cb, lp2, p_tab, q_tab, a1, a2 = _s5_tables(lam_re[0], lam_im[0], log_dt[0], b_re[0], b_im[0],
                                             c_re[0], c_im[0])
    yt = _s5_call(ut, cb, lp2, p_tab, q_tab, a1, a2, n_chunks, bsz)

    wr_t = w_router[0].T
    wr_hi = wr_t.astype(BF16)
    wr_lo = (wr_t - wr_hi.astype(F32)).astype(BF16)
    w_out_b = w_out[0].astype(BF16)
    w_out_p = jnp.concatenate([w_out_b[:fdim], w_out_b[fdim:][perm]], axis=0)
    x1, hp, logits_t = _mix_back_call(
        x.reshape(n, d), out_a.reshape(n, fdim), yt, ucm, mod,
        d_skip[0][perm].reshape(nb, 1), w_glu[0][perm][:, perm].T.astype(BF16),
        b_glu[0][perm].reshape(nb, 1), w_out_p, wr_hi, wr_lo, seq)

    top_idx, top_w, rank, counts = _route_call(logits_t, router_bias[0])
    pair_e, pair_blk, pair_ok, starts, ends, next_e = _pair_tables(counts[:, 0], n * TOP_K)

    def windowed(a):
        return a.reshape(TOP_K, n // SC_WINDOW, SC_WINDOW).transpose(1, 0, 2)

    xs, pos_win = _dispatch_call(windowed(top_idx), windowed(rank), starts,
                                 hp.reshape(n, PACK_ROWS, LANES))
    y_sorted = _experts_call(pair_e, pair_blk, pair_ok, starts, ends, next_e, xs,
                             w_gate[0], w_up[0], w_down[0])
    w_tok = top_w
    g2 = mod[:, 5:6, :]
    fgain = final_gain.reshape(1, d)
    wsg, wsu, wsd = ws_gate[0].astype(BF16), ws_up[0].astype(BF16), ws_down[0].astype(BF16)
    wins = pos_win.shape[0] // TAIL_PARTS
    out = None
    for part in range(TAIL_PARTS):
        y_tok = _collect_call(pos_win[part * wins:(part + 1) * wins], y_sorted)
        out = _combine_call(y_tok, x1, hp, w_tok, g2, fgain, wsg, wsu, wsd, seq, part, out)
    return out.reshape(bsz, seq, d)
```

```python
import functools

import jax
import jax.numpy as jnp
from jax import lax
from jax.experimental import pallas as pl
from jax.experimental.pallas import tpu as pltpu
from jax.experimental.pallas import tpu_sc as plsc

F32 = jnp.float32
BF16 = jnp.bfloat16
I32 = jnp.int32

EPS = 1e-6
CHUNK = 64
HGRN_HEADS = 4
HGRN_KDIM = 128
S5_GROUP = 16
S5_STATE = 64
N_EXPERT_GROUPS = 8
TOPK_GROUPS = 4
TOP_K = 8
ROUTE_SCALE = 2.5
LANES = 128
PACK_ROWS = 4
SC_CORES = 2
SC_SUBCORES = 16
SC_LANES = 16
SC_WINDOW = 64

SEQ_TILE = 512
TOK_TILE = 512
ROUTE_TILE = 512
MOVE_TILE = 512
ROW_BLOCK = 2048
SUB_BLOCK = 512
TAIL_PARTS = 4
VMEM_LIMIT = 56 * 1024 * 1024

_NT = (((1,), (1,)), ((), ()))
_TN = (((0,), (0,)), ((), ()))


def _sigmoid(v):
    return 0.5 * jnp.tanh(0.5 * v) + 0.5


def _silu(v):
    return v * _sigmoid(v)


def _bdot(a, b):
    return jnp.dot(a.astype(BF16), b.astype(BF16), preferred_element_type=F32)


def _store_packed_planes(ref, val):
    for j in range(PACK_ROWS):
        lo = val[:, 2 * j * LANES:(2 * j + 1) * LANES]
        hi = val[:, (2 * j + 1) * LANES:(2 * j + 2) * LANES]
        ref[j] = pltpu.pack_elementwise([lo, hi], packed_dtype=BF16)


def _mod_kernel(c_ref, w_ref, b_ref, o_ref):
    o_ref[...] = _bdot(_silu(c_ref[...]), w_ref[...]) + b_ref[...]


def _mod_call(c, w_ada, b_ada):
    bsz, d = c.shape
    n_out = w_ada.shape[1]
    return pl.pallas_call(
        _mod_kernel,
        out_shape=jax.ShapeDtypeStruct((bsz, n_out), F32),
        grid=(n_out // d,),
        in_specs=[pl.BlockSpec((bsz, d), lambda j: (0, 0)),
                  pl.BlockSpec((d, d), lambda j: (0, j)),
                  pl.BlockSpec((1, d), lambda j: (0, j))],
        out_specs=pl.BlockSpec((bsz, d), lambda j: (0, j)),
        compiler_params=pltpu.CompilerParams(vmem_limit_bytes=VMEM_LIMIT),
        name="mod",
    )(c, w_ada, b_ada.reshape(1, n_out))


def _split_chunk_pairs(tile_even, tile_odd):
    low = lax.broadcasted_iota(I32, tile_even.shape, 1) < CHUNK
    first = jnp.where(low, tile_even, pltpu.roll(tile_odd, CHUNK, 1))
    second = jnp.where(low, pltpu.roll(tile_even, CHUNK, 1), tile_odd)
    return first, second


def _mix_front_kernel(x_ref, mod_ref, win_ref, wut_ref, lb_ref, gn_ref, ltri_ref,
                      ut_ref, oa_ref, ucm_ref, proj_a, proj_b, st_ref, flat_ref, hb_ref, *, tiles_per_seq):
    fdim = HGRN_HEADS * HGRN_KDIM
    ts = x_ref.shape[1]
    pairs = wut_ref.shape[0] // 2
    rows = ts // CHUNK
    i = pl.program_id(0)

    @pl.when(i == 0)
    def _():
        proj_b[...] = jnp.zeros_like(proj_b)

    @pl.when(lax.rem(jnp.maximum(i - 1, 0), tiles_per_seq) == 0)
    def _():
        st_ref[...] = jnp.zeros_like(st_ref)

    def normalise():
        x = x_ref[0]
        ms = jnp.mean(x * x, axis=-1, keepdims=True)
        h = x * lax.rsqrt(ms + EPS) * (1.0 + mod_ref[0, 1:2, :]) + mod_ref[0, 0:1, :]
        hb_ref[...] = h.astype(BF16)

    def project_slab(proj_ref, n):
        width = proj_ref.shape[1] // rows
        cols = slice(n * width, (n + 1) * width)
        proj_ref[:, cols] = jnp.dot(hb_ref[...], win_ref[:, cols], preferred_element_type=F32)

    def s5_input():
        u_t = lax.dot_general(wut_ref[...], hb_ref[...], _NT, preferred_element_type=F32)
        ucm_ref[...] = u_t.astype(BF16)
        for m in range(ts // LANES):
            first, second = _split_chunk_pairs(u_t[:pairs, m * LANES:(m + 1) * LANES],
                                               u_t[pairs:, m * LANES:(m + 1) * LANES])
            flat_ref[2 * m * pairs:(2 * m + 1) * pairs, :] = first
            flat_ref[(2 * m + 1) * pairs:(2 * m + 2) * pairs, :] = second
        per_group = S5_GROUP // 2
        for q in range(pairs):
            ut_ref[q // per_group, :, (q % per_group) * LANES:(q % per_group + 1) * LANES] = (
                flat_ref[pl.ds(q, rows, stride=pairs), :])

    lb = lb_ref[...]
    gn = gn_ref[...]
    ltri = ltri_ref[...]
    row = lax.broadcasted_iota(I32, (CHUNK, CHUNK), 0)
    col = lax.broadcasted_iota(I32, (CHUNK, CHUNK), 1)
    causal = row >= col

    def step(proj_ref, next_ref):
        normalise()
        for ci in range(rows):
            project_slab(next_ref, ci)
            r0 = ci * CHUNK
            q = proj_ref[r0:r0 + CHUNK, 0:fdim]
            fl = proj_ref[r0:r0 + CHUNK, fdim:2 * fdim]
            iv = proj_ref[r0:r0 + CHUNK, 2 * fdim:3 * fdim]
            og = proj_ref[r0:r0 + CHUNK, 3 * fdim:4 * fdim]
            f = lb + (1.0 - lb) * _sigmoid(fl)
            lf_hi, lf_mid, lf_lo = _split3(jnp.log(f))
            b = (jnp.dot(ltri, lf_hi, preferred_element_type=F32)
                 + jnp.dot(ltri, lf_mid, preferred_element_type=F32)
                 + jnp.dot(ltri, lf_lo, preferred_element_type=F32))
            b_ref = b[CHUNK // 2 - 1:CHUNK // 2, :]
            b_last = b[CHUNK - 1:CHUNK, :]
            qs = _silu(q)
            kk = 1.0 - f
            qe = (qs * jnp.exp(b - b_ref)).astype(BF16)
            ke = (kk * jnp.exp(b_ref - b)).astype(BF16)
            qb = (qs * jnp.exp(b)).astype(BF16)
            k2 = (kk * jnp.exp(b_last - b)).astype(BF16)
            dec = jnp.exp(b_last)
            ivb = iv.astype(BF16)
            outs = []
            for hh in range(HGRN_HEADS):
                sl = slice(hh * HGRN_KDIM, (hh + 1) * HGRN_KDIM)
                att = lax.dot_general(qe[:, sl], ke[:, sl], _NT, preferred_element_type=F32)
                att = jnp.where(causal, att, 0.0)
                st = st_ref[hh]
                o = jnp.dot(att.astype(BF16), ivb[:, sl], preferred_element_type=F32)
                o = o + lax.dot_general(qb[:, sl], st.astype(BF16), _NT, preferred_element_type=F32)
                st_ref[hh] = st * dec[:, sl] + lax.dot_general(
                    ivb[:, sl], k2[:, sl], _TN, preferred_element_type=F32)
                outs.append(o * lax.rsqrt(jnp.mean(o * o, axis=-1, keepdims=True) + EPS))
            o = jnp.concatenate(outs, axis=1) * gn * _silu(og)
            oa_ref[0, r0:r0 + CHUNK, :] = o.astype(BF16)
        s5_input()

    @pl.when(lax.rem(i, 2) == 0)
    def _():
        step(proj_b, proj_a)

    @pl.when(lax.rem(i, 2) == 1)
    def _():
        step(proj_a, proj_b)


def _mix_front_call(x, mod, w_main, w_ut, lb, gn):
    bsz, seq, d = x.shape
    fdim = HGRN_HEADS * HGRN_KDIM
    ncols = w_main.shape[1]
    nb = w_ut.shape[0]
    groups = nb // S5_GROUP
    ltri = jnp.tril(jnp.ones((CHUNK, CHUNK), BF16))
    ts = SEQ_TILE
    tiles = seq // ts
    n_tiles = bsz * tiles
    rows = ts // CHUNK

    def cur(i):
        return jnp.minimum(i, n_tiles - 1)

    def prev(i):
        return jnp.maximum(i - 1, 0)

    return pl.pallas_call(
        functools.partial(_mix_front_kernel, tiles_per_seq=tiles),
        out_shape=(jax.ShapeDtypeStruct((groups, bsz * seq // CHUNK, S5_GROUP * CHUNK), F32),
                   jax.ShapeDtypeStruct((bsz, seq, fdim), BF16),
                   jax.ShapeDtypeStruct((nb, bsz * seq), BF16)),
        grid=(n_tiles + 1,),
        in_specs=[pl.BlockSpec((1, ts, d), lambda i: (cur(i) // tiles, cur(i) % tiles, 0)),
                  pl.BlockSpec((1, 6, d), lambda i: (cur(i) // tiles, 0, 0)),
                  pl.BlockSpec((d, ncols), lambda i: (0, 0)),
                  pl.BlockSpec((nb, d), lambda i: (0, 0)),
                  pl.BlockSpec((1, fdim), lambda i: (0, 0)),
                  pl.BlockSpec((1, fdim), lambda i: (0, 0)),
                  pl.BlockSpec((CHUNK, CHUNK), lambda i: (0, 0))],
        out_specs=(pl.BlockSpec((groups, rows, S5_GROUP * CHUNK), lambda i: (0, cur(i), 0)),
                   pl.BlockSpec((1, ts, fdim), lambda i: (prev(i) // tiles, prev(i) % tiles, 0)),
                   pl.BlockSpec((nb, ts), lambda i: (0, cur(i)))),
        scratch_shapes=[pltpu.VMEM((ts, ncols), F32), pltpu.VMEM((ts, ncols), F32),
                        pltpu.VMEM((HGRN_HEADS, fdim // HGRN_HEADS, HGRN_KDIM), F32),
                        pltpu.VMEM((rows * nb // 2, LANES), F32),
                        pltpu.VMEM((ts, d), BF16)],
        compiler_params=pltpu.CompilerParams(
            dimension_semantics=("arbitrary",), vmem_limit_bytes=VMEM_LIMIT),
        name="mix_front",
    )(x, mod, w_main, w_ut, lb, gn, ltri)


def _s5_tables(lam_re, lam_im, log_dt, b_re, b_im, c_re, c_im):
    t = CHUNK
    lam = lax.complex(jnp.minimum(lam_re, -1e-4), lam_im)
    lam_dt = lam * jnp.exp(log_dt)[:, None]
    lam_bar = jnp.exp(lam_dt)
    b_bar = ((lam_bar - 1.0) / lam)[..., None] * lax.complex(b_re, b_im)
    c_mat = lax.complex(c_re, c_im)
    taus = jnp.arange(t + 1, dtype=F32)
    lam_pow = jnp.exp(lam_dt[:, None, :] * taus[None, :, None])
    g, p = lam.shape
    c = b_re.shape[-1]
    cb = c_mat[:, None, :, :] * b_bar.transpose(0, 2, 1)[:, :, None, :]
    cb = jnp.concatenate([cb.real, -cb.imag], axis=-1)
    cb = cb.reshape(g, c, c // 2, 2 * 2 * p).reshape(g, c * c // 2, 4 * p)
    lp = jnp.concatenate([lam_pow[:, :t].real, lam_pow[:, :t].imag], axis=-1).transpose(0, 2, 1)
    zero = jnp.zeros_like(lp)
    lp2 = jnp.concatenate([jnp.concatenate([lp, zero], axis=2),
                           jnp.concatenate([zero, lp], axis=2)], axis=1)
    pc = lam_pow[:, t - 1::-1][:, :t, :, None] * b_bar[:, None, :, :]
    pc = pc.transpose(0, 3, 1, 2).reshape(g, c * t, p)
    p_tab = jnp.concatenate([pc.real, pc.imag], axis=-1)
    ql = c_mat[:, None, :, :] * lam_pow[:, 1:t + 1, None, :]
    ql = ql.transpose(0, 3, 2, 1).reshape(g, p, c * t)
    q_tab = jnp.concatenate([ql.real, -ql.imag], axis=1)
    lam_t = lam_pow[:, t]
    a1 = jnp.concatenate([lam_t.real, lam_t.real], axis=-1)[:, None, :]
    a2 = jnp.concatenate([-lam_t.imag, lam_t.imag], axis=-1)[:, None, :]
    return cb, lp2, p_tab.astype(BF16), q_tab.astype(BF16), a1, a2


def _split3(v):
    hi = v.astype(BF16)
    rem = v - hi.astype(F32)
    mid = rem.astype(BF16)
    return hi, mid, (rem - mid.astype(F32)).astype(BF16)


def _s5_kernel(u_ref, cb_ref, lp_ref, p_ref, q_ref, a1_ref, a2_ref, y_ref, v_ref, xs_ref, m_ref, k_ref,
               *, n_chunks, n_batch):
    c_hi, c_mid, c_lo = _split3(cb_ref[0])
    l_hi, l_mid, l_lo = _split3(lp_ref[0])
    k_ref[...] = (jnp.dot(c_hi, l_hi, preferred_element_type=F32)
                  + jnp.dot(c_hi, l_mid, preferred_element_type=F32)
                  + jnp.dot(c_mid, l_hi, preferred_element_type=F32)
                  + jnp.dot(c_hi, l_lo, preferred_element_type=F32)
                  + jnp.dot(c_mid, l_mid, preferred_element_type=F32)
                  + jnp.dot(c_lo, l_hi, preferred_element_type=F32))
    n_pairs = m_ref.shape[1] // LANES
    n_in = k_ref.shape[0] // n_pairs
    lane = lax.broadcasted_iota(I32, (CHUNK, LANES), 1)
    causal = (lane & (CHUNK - 1)) >= lax.broadcasted_iota(I32, (CHUNK, LANES), 0)
    for ci in range(n_in):
        for a in range(n_pairs):
            lags = jnp.broadcast_to(k_ref[ci * n_pairs + a:ci * n_pairs + a + 1, :], (CHUNK, LANES))
            tile = pltpu.roll(lags, 0, 1, stride=1, stride_axis=0)
            m_ref[ci * CHUNK:(ci + 1) * CHUNK, a * LANES:(a + 1) * LANES] = jnp.where(
                causal, tile, 0.0).astype(BF16)
    u = u_ref[0].astype(BF16)
    v_ref[...] = jnp.dot(u, p_ref[0], preferred_element_type=F32)
    a1 = a1_ref[0]
    a2 = a2_ref[0]
    half = xs_ref.shape[1] // 2

    def step(n, state):
        xs_ref[pl.ds(n, n_batch, stride=n_chunks), :] = state
        return (a1 * state + a2 * pltpu.roll(state, half, 1)
                + v_ref[pl.ds(n, n_batch, stride=n_chunks), :])

    lax.fori_loop(0, n_chunks, step, jnp.zeros((n_batch, xs_ref.shape[1]), F32))
    y = jnp.dot(u, m_ref[...], preferred_element_type=F32)
    y_ref[0] = y + jnp.dot(xs_ref[...].astype(BF16), q_ref[0], preferred_element_type=F32)


def _s5_call(ut, cb, lp2, p_tab, q_tab, a1, a2, n_chunks, n_batch):
    g, rows, width = ut.shape
    p2 = p_tab.shape[-1]
    return pl.pallas_call(
        functools.partial(_s5_kernel, n_chunks=n_chunks, n_batch=n_batch),
        out_shape=jax.ShapeDtypeStruct((g, rows, width), F32),
        grid=(g,),
        in_specs=[pl.BlockSpec((1, rows, width), lambda i: (i, 0, 0)),
                  pl.BlockSpec((1,) + cb.shape[1:], lambda i: (i, 0, 0)),
                  pl.BlockSpec((1,) + lp2.shape[1:], lambda i: (i, 0, 0)),
                  pl.BlockSpec((1, width, p2), lambda i: (i, 0, 0)),
                  pl.BlockSpec((1, p2, width), lambda i: (i, 0, 0)),
                  pl.BlockSpec((1, 1, p2), lambda i: (i, 0, 0)),
                  pl.BlockSpec((1, 1, p2), lambda i: (i, 0, 0))],
        out_specs=pl.BlockSpec((1, rows, width), lambda i: (i, 0, 0)),
        scratch_shapes=[pltpu.VMEM((rows, p2), F32), pltpu.VMEM((rows, p2), F32),
                        pltpu.VMEM((width, width), BF16),
                        pltpu.VMEM((cb.shape[1], lp2.shape[2]), F32)],
        compiler_params=pltpu.CompilerParams(
            dimension_semantics=("arbitrary",), vmem_limit_bytes=VMEM_LIMIT),
        name="s5",
    )(ut, cb, lp2, p_tab, q_tab, a1, a2)


def _token_major(flat_ref, src_ref):
    groups, rows, _ = src_ref.shape
    per_group = S5_GROUP // 2
    pairs = groups * per_group
    for q in range(pairs):
        flat_ref[pl.ds(q, rows, stride=pairs), :] = (
            src_ref[q // per_group, :, (q % per_group) * LANES:(q % per_group + 1) * LANES])
    tiles = []
    for m in range(rows // 2):
        even, odd = _split_chunk_pairs(flat_ref[2 * m * pairs:(2 * m + 1) * pairs, :],
                                       flat_ref[(2 * m + 1) * pairs:(2 * m + 2) * pairs, :])
        tiles.append(jnp.concatenate([even, odd], axis=0))
    return jnp.concatenate(tiles, axis=1)


def _mix_back_kernel(x_ref, oa_ref, yt_ref, ucm_ref, mod_ref, dskip_ref, wglu_ref, bglu_ref,
                     wout_ref, wrh_ref, wrl_ref,
                     x1_ref, hp_ref, lt_ref, flat_ref):
    na = oa_ref.shape[1]
    y_t = _token_major(flat_ref, yt_ref)
    u_t = ucm_ref[...].astype(F32)
    z_t = jax.nn.gelu(y_t + dskip_ref[...] * u_t)
    gate_t = _sigmoid(jnp.dot(wglu_ref[...], z_t.astype(BF16), preferred_element_type=F32)
                      + bglu_ref[...])
    ob_t = (z_t * gate_t).astype(BF16)
    mixed = (jnp.dot(oa_ref[...], wout_ref[0:na, :], preferred_element_type=F32)
             + lax.dot_general(ob_t, wout_ref[na:, :], _TN, preferred_element_type=F32))
    x1 = x_ref[...] + mod_ref[0, 2:3, :] * mixed
    ms = jnp.mean(x1 * x1, axis=-1, keepdims=True)
    h2 = x1 * lax.rsqrt(ms + EPS) * (1.0 + mod_ref[0, 4:5, :]) + mod_ref[0, 3:4, :]
    _store_packed_planes(hp_ref, h2)
    h_hi = h2.astype(BF16)
    h_lo = (h2 - h_hi.astype(F32)).astype(BF16)
    lt = lax.dot_general(wrh_ref[...], h_hi, _NT, preferred_element_type=F32)
    lt = lt + lax.dot_general(wrl_ref[...], h_hi, _NT, preferred_element_type=F32)
    lt = lt + lax.dot_general(wrh_ref[...], h_lo, _NT, preferred_element_type=F32)
    lt_ref[...] = lt
    x1_ref[...] = x1


def _mix_back_call(x2d, oa, yt, ucm, mod, dskip, wglu, bglu, wout, wrh, wrl, seq):
    n, d = x2d.shape
    nb = oa.shape[1]
    ne = wrh.shape[0]
    tm = TOK_TILE
    per_b = seq // tm
    groups, _, width = yt.shape
    rows = tm // CHUNK
    const = lambda i: (0, 0)
    flat_block = pl.BlockSpec((groups, rows, width), lambda i: (0, i, 0))
    return pl.pallas_call(
        _mix_back_kernel,
        out_shape=(jax.ShapeDtypeStruct((n, d), F32),
                   jax.ShapeDtypeStruct((PACK_ROWS, n, LANES), I32),
                   jax.ShapeDtypeStruct((ne, n), F32)),
        grid=(n // tm,),
        in_specs=[pl.BlockSpec((tm, d), lambda i: (i, 0)),
                  pl.BlockSpec((tm, nb), lambda i: (i, 0)),
                  flat_block,
                  pl.BlockSpec((nb, tm), lambda i: (0, i)),
                  pl.BlockSpec((1, 6, d), lambda i: (i // per_b, 0, 0)),
                  pl.BlockSpec((nb, 1), const),
                  pl.BlockSpec((nb, nb), const),
                  pl.BlockSpec((nb, 1), const),
                  pl.BlockSpec((d, d), const),
                  pl.BlockSpec((ne, d), const),
                  pl.BlockSpec((ne, d), const)],
        out_specs=(pl.BlockSpec((tm, d), lambda i: (i, 0)),
                   pl.BlockSpec((PACK_ROWS, tm, LANES), lambda i: (0, i, 0)),
                   pl.BlockSpec((ne, tm), lambda i: (0, i))),
        scratch_shapes=[pltpu.VMEM((rows * nb // 2, LANES), F32)],
        compiler_params=pltpu.CompilerParams(
            dimension_semantics=("arbitrary",), vmem_limit_bytes=VMEM_LIMIT),
        name="mix_back",
    )(x2d, oa, yt, ucm, mod, dskip, wglu, bglu, wout, wrh, wrl)


def _route_kernel(lt_ref, bias_ref, su_ref, idx_ref, w_ref, rank_ref, cnt_ref, run_ref):
    ne, tr = lt_ref.shape
    per_group = ne // N_EXPERT_GROUPS
    neg = -jnp.inf

    @pl.when(pl.program_id(0) == 0)
    def _():
        run_ref[...] = jnp.zeros_like(run_ref)

    s = _sigmoid(lt_ref[...])
    sel = s + bias_ref[...]
    gio = lax.broadcasted_iota(I32, (per_group, tr), 0)
    gscore = []
    for g in range(N_EXPERT_GROUPS):
        v = sel[g * per_group:(g + 1) * per_group, :]
        m1 = jnp.max(v, axis=0, keepdims=True)
        i1 = jnp.min(jnp.where(v == m1, gio, per_group), axis=0, keepdims=True)
        m2 = jnp.max(jnp.where(gio == i1, neg, v), axis=0, keepdims=True)
        gscore.append(m1 + m2)
    masked = []
    for g in range(N_EXPERT_GROUPS):
        ahead = jnp.zeros((1, tr), I32)
        for o in range(N_EXPERT_GROUPS):
            if o == g:
                continue
            wins = (gscore[o] >= gscore[g]) if o < g else (gscore[o] > gscore[g])
            ahead = ahead + wins.astype(I32)
        keep = ahead < TOPK_GROUPS
        masked.append(jnp.where(keep, sel[g * per_group:(g + 1) * per_group, :], neg))
    selm = jnp.concatenate(masked, axis=0)
    eio = lax.broadcasted_iota(I32, (ne, tr), 0)
    candidate = selm > neg
    idxs, ws = [], []
    for k in range(TOP_K):
        m = jnp.max(selm, axis=0, keepdims=True)
        ik = jnp.min(jnp.where(selm == m, eio, ne), axis=0, keepdims=True)
        onehot = eio == ik
        ws.append(jnp.sum(jnp.where(onehot, s, 0.0), axis=0, keepdims=True))
        selm = jnp.where(onehot, neg, selm)
        idxs.append(ik)
    hits = jnp.where(jnp.logical_and(candidate, selm == neg), 1.0, 0.0)
    wsum = ws[0]
    for k in range(1, TOP_K):
        wsum = wsum + ws[k]
    scale = ROUTE_SCALE / wsum
    ranks = jnp.dot(hits.astype(BF16), su_ref[...], preferred_element_type=F32) + run_ref[...]
    for k in range(TOP_K):
        idx_ref[k:k + 1, :] = idxs[k]
        w_ref[k:k + 1, :] = ws[k] * scale
        rk = jnp.sum(jnp.where(eio == idxs[k], ranks, 0.0), axis=0, keepdims=True)
        rank_ref[k:k + 1, :] = rk.astype(I32)
    run_ref[...] = run_ref[...] + jnp.sum(hits, axis=1, keepdims=True)
    cnt_ref[...] = run_ref[...]


def _route_call(lt, bias):
    ne, n = lt.shape
    tr = ROUTE_TILE
    su = jnp.triu(jnp.ones((tr, tr), F32), k=1).astype(BF16)
    return pl.pallas_call(
        _route_kernel,
        out_shape=(jax.ShapeDtypeStruct((TOP_K, n), I32),
                   jax.ShapeDtypeStruct((TOP_K, n), F32),
                   jax.ShapeDtypeStruct((TOP_K, n), I32),
                   jax.ShapeDtypeStruct((ne, 1), F32)),
        grid=(n // tr,),
        in_specs=[pl.BlockSpec((ne, tr), lambda i: (0, i)),
                  pl.BlockSpec((ne, 1), lambda i: (0, 0)),
                  pl.BlockSpec((tr, tr), lambda i: (0, 0))],
        out_specs=(pl.BlockSpec((TOP_K, tr), lambda i: (0, i)),
                   pl.BlockSpec((TOP_K, tr), lambda i: (0, i)),
                   pl.BlockSpec((TOP_K, tr), lambda i: (0, i)),
                   pl.BlockSpec((ne, 1), lambda i: (0, 0))),
        scratch_shapes=[pltpu.VMEM((ne, 1), F32)],
        compiler_params=pltpu.CompilerParams(
            dimension_semantics=("arbitrary",), vmem_limit_bytes=VMEM_LIMIT),
        name="route",
    )(lt, bias.reshape(ne, 1), su)


def _sc_mesh():
    return plsc.VectorSubcoreMesh(core_axis_name="c", subcore_axis_name="s",
                                  num_cores=SC_CORES, num_subcores=SC_SUBCORES)


def _sc_worker():
    return lax.axis_index("s") * SC_CORES + lax.axis_index("c")


def _dispatch_call(idx_win, rank_win, starts, hp):
    n = hp.shape[1]
    n_workers = SC_CORES * SC_SUBCORES
    wins_per_worker = n // SC_WINDOW // n_workers
    lanes = SC_LANES

    def body(hp_hbm, idx_hbm, rank_hbm, st_hbm, xs_hbm, pos_hbm,
             idx_a, idx_b, rank_a, rank_b, pos_a, pos_b, rows_a, rows_b, st_v,
             load_a, load_b, scatter_sem, pos_sem):
        first_win = _sc_worker() * wins_per_worker
        idx_v, rank_v, pos_v = (idx_a, idx_b), (rank_a, rank_b), (pos_a, pos_b)
        rows_v, load_sem = (rows_a, rows_b), (load_a, load_b)
        pltpu.sync_copy(st_hbm, st_v)

        def loads(win, slot):
            pieces = tuple(
                pltpu.make_async_copy(hp_hbm.at[j, pl.ds(win * SC_WINDOW, SC_WINDOW)],
                                      rows_v[slot].at[:, j], load_sem[slot])
                for j in range(PACK_ROWS))
            return pieces + (pltpu.make_async_copy(idx_hbm.at[win], idx_v[slot], load_sem[slot]),
                             pltpu.make_async_copy(rank_hbm.at[win], rank_v[slot], load_sem[slot]))

        for cp in loads(first_win, 0):
            cp.start()

        @pl.loop(0, wins_per_worker, step=2)
        def _(w):
            for slot in range(2):
                win = first_win + w + slot
                for cp in loads(win, slot):
                    cp.wait()

                @pl.when(w + slot + 1 < wins_per_worker)
                def _():
                    for cp in loads(win + 1, 1 - slot):
                        cp.start()

                for k in range(TOP_K):
                    for c in range(SC_WINDOW // lanes):
                        seg = pl.ds(c * lanes, lanes)
                        base = plsc.load_gather(st_v, [idx_v[slot][k, seg]])
                        pos_v[slot][k, seg] = base + rank_v[slot][k, seg]
                pos_out = pltpu.async_copy(pos_v[slot], pos_hbm.at[win], pos_sem)
                copies = [pltpu.async_copy(rows_v[slot], xs_hbm.at[pos_v[slot].at[k]], scatter_sem)
                          for k in range(TOP_K)]
                for cp in copies:
                    cp.wait()
                pos_out.wait()

    idx_buf = pltpu.VMEM((TOP_K, SC_WINDOW), I32)
    row_buf = pltpu.VMEM((SC_WINDOW, PACK_ROWS, LANES), hp.dtype)
    return pl.kernel(
        body,
        out_type=(jax.ShapeDtypeStruct((n * TOP_K, PACK_ROWS, LANES), hp.dtype),
                  jax.ShapeDtypeStruct(idx_win.shape, I32)),
        mesh=_sc_mesh(),
        scratch_types=[idx_buf, idx_buf, idx_buf, idx_buf, idx_buf, idx_buf, row_buf, row_buf,
                       pltpu.VMEM(starts.shape, I32),
                       pltpu.SemaphoreType.DMA, pltpu.SemaphoreType.DMA, pltpu.SemaphoreType.DMA,
                       pltpu.SemaphoreType.DMA],
        compiler_params=pltpu.CompilerParams(needs_layout_passes=False),
        name="dispatch",
    )(hp, idx_win, rank_win, starts)


def _collect_call(pos_win, y_sorted):
    n = pos_win.shape[0] * SC_WINDOW
    n_workers = SC_CORES * SC_SUBCORES
    wins_per_worker = n // SC_WINDOW // n_workers

    def body(ys_hbm, pos_hbm, out_hbm, idx_a, idx_b, rows_a, rows_b, idx_sem_a, idx_sem_b,
             gather_sem, write_sem):
        first_win = _sc_worker() * wins_per_worker
        bufs = (rows_a, rows_b)
        idx_v, idx_sem = (idx_a, idx_b), (idx_sem_a, idx_sem_b)

        def idx_load(win, slot):
            return pltpu.make_async_copy(pos_hbm.at[win], idx_v[slot], idx_sem[slot])

        idx_load(first_win, 0).start()

        @pl.loop(0, wins_per_worker, step=2)
        def _(w):
            for slot in range(2):
                win = first_win + w + slot
                idx_load(win, slot).wait()

                @pl.when(w + slot + 1 < wins_per_worker)
                def _():
                    idx_load(win + 1, 1 - slot).start()

                def gather(k):
                    return pltpu.async_copy(ys_hbm.at[idx_v[slot].at[k]], bufs[k % 2], gather_sem)

                pending_gather = gather(0)
                pending_write = None
                for k in range(TOP_K):
                    pending_gather.wait()
                    if pending_write is not None:
                        for cp in pending_write:
                            cp.wait()
                    if k + 1 < TOP_K:
                        pending_gather = gather(k + 1)
                    pending_write = [
                        pltpu.async_copy(bufs[k % 2].at[:, j],
                                         out_hbm.at[j, k, pl.ds(win * SC_WINDOW, SC_WINDOW)], write_sem)
                        for j in range(PACK_ROWS)]
                for cp in pending_write:
                    cp.wait()

    idx_buf = pltpu.VMEM((TOP_K, SC_WINDOW), I32)
    row_buf = pltpu.VMEM((SC_WINDOW,) + y_sorted.shape[1:], y_sorted.dtype)
    return pl.kernel(
        body,
        out_type=jax.ShapeDtypeStruct((PACK_ROWS, TOP_K, n, LANES), y_sorted.dtype),
        mesh=_sc_mesh(),
        scratch_types=[idx_buf, idx_buf, row_buf, row_buf,
                       pltpu.SemaphoreType.DMA, pltpu.SemaphoreType.DMA,
                       pltpu.SemaphoreType.DMA, pltpu.SemaphoreType.DMA],
        name="collect",
    )(y_sorted, pos_win)


def _experts_kernel(pe_ref, pb_ref, pv_ref, st_ref, en_ref, nx_ref,
                    xs_hbm, wg_hbm, wu_hbm, wd_hbm, y_hbm,
                    wgb_ref, wub_ref, wdb_ref, wgf_ref, wuf_ref, wdf_ref, xbuf_ref, stage_ref,
                    in_sems, out_sems, w_sems, state_ref):
    p = pl.program_id(0)
    n_pairs = pl.num_programs(0)
    e = pe_ref[p]
    blk = pb_ref[p]
    rb = xbuf_ref.shape[2]
    prev = jnp.maximum(p - 1, 0)
    nxt = jnp.minimum(p + 1, n_pairs - 1)
    live = pv_ref[p] == 1
    first = jnp.logical_or(p == 0, pb_ref[prev] != blk)
    block_ends = jnp.logical_or(pb_ref[nxt] != blk, pv_ref[nxt] == 0)
    last = jnp.logical_or(p == n_pairs - 1, block_ends)

    def in_copy(j, block, slot):
        return pltpu.make_async_copy(xs_hbm.at[pl.ds(block * rb, rb), j], xbuf_ref.at[slot, j],
                                     in_sems.at[slot])

    def out_copy(j, block, slot):
        return pltpu.make_async_copy(stage_ref.at[slot, j], y_hbm.at[pl.ds(block * rb, rb), j],
                                     out_sems.at[slot])

    def drain(slot):
        @pl.when(state_ref[1 + slot] == 1)
        def _():
            for j in range(PACK_ROWS):
                out_copy(j, 0, slot).wait()
            state_ref[1 + slot] = 0

    @pl.when(p == 0)
    def _():
        state_ref[0] = 0
        state_ref[1] = 0
        state_ref[2] = 0
        stage_ref[...] = jnp.zeros_like(stage_ref)
        for j in range(PACK_ROWS):
            in_copy(j, blk, 0).start()

    slot = jnp.where(jnp.logical_and(first, p > 0), 1 - state_ref[0], state_ref[0])
    state_ref[0] = slot

    @pl.when(jnp.logical_and(first, blk + 1 < y_hbm.shape[0] // rb))
    def _():
        for j in range(PACK_ROWS):
            in_copy(j, blk + 1, 1 - slot).start()

    @pl.when(first)
    def _():
        for j in range(PACK_ROWS):
            in_copy(j, blk, slot).wait()
        drain(slot)

    def w_copies(expert, wslot):
        return (pltpu.make_async_copy(wg_hbm.at[expert], wgf_ref.at[wslot], w_sems.at[wslot]),
                pltpu.make_async_copy(wu_hbm.at[expert], wuf_ref.at[wslot], w_sems.at[wslot]),
                pltpu.make_async_copy(wd_hbm.at[expert], wdf_ref.at[wslot], w_sems.at[wslot]))

    @pl.when(p == 0)
    def _():
        state_ref[3] = 0
        for cp in w_copies(e, 0):
            cp.start()

    new_expert = jnp.logical_or(p == 0, pe_ref[prev] != e)
    wslot = jnp.where(jnp.logical_and(new_expert, p > 0), 1 - state_ref[3], state_ref[3])
    state_ref[3] = wslot

    @pl.when(new_expert)
    def _():
        for cp in w_copies(e, wslot):
            cp.wait()
        wgb_ref[...] = wgf_ref[wslot].astype(BF16)
        wub_ref[...] = wuf_ref[wslot].astype(BF16)
        wdb_ref[...] = wdf_ref[wslot].astype(BF16)

        @pl.when(nx_ref[e] != e)
        def _():
            for cp in w_copies(nx_ref[e], 1 - wslot):
                cp.start()

    lo_row = st_ref[e]
    hi_row = en_ref[e]

    def sub_block(s, row0, shared):
        sub = pl.ds(s * SUB_BLOCK, SUB_BLOCK)
        pieces = []
        for j in range(PACK_ROWS):
            w = xbuf_ref[slot, j, sub, :]
            pieces.append(lax.bitcast_convert_type(w.astype(jnp.int16), BF16))
            pieces.append(lax.bitcast_convert_type(
                lax.shift_right_logical(w, 16).astype(jnp.int16), BF16))
        xb = jnp.concatenate(pieces, axis=1)
        gate = jnp.dot(xb, wgb_ref[...], preferred_element_type=F32)
        up = jnp.dot(xb, wub_ref[...], preferred_element_type=F32)
        yb = jnp.dot((_silu(gate) * up).astype(BF16), wdb_ref[...], preferred_element_type=F32)
        if shared:
            rows = row0 + lax.broadcasted_iota(I32, (SUB_BLOCK, 1), 0)
            mine = jnp.logical_and(rows >= lo_row, rows < hi_row)
        for j in range(PACK_ROWS):
            word = pltpu.pack_elementwise(
                [yb[:, 2 * j * LANES:(2 * j + 1) * LANES], yb[:, (2 * j + 1) * LANES:(2 * j + 2) * LANES]],
                packed_dtype=BF16)
            if shared:
                word = jnp.where(mine, word, stage_ref[slot, j, sub, :])
            stage_ref[slot, j, sub, :] = word

    def one(s):
        row0 = blk * rb + s * SUB_BLOCK
        touched = jnp.logical_and(live, jnp.logical_and(row0 < hi_row, row0 + SUB_BLOCK > lo_row))
        whole = jnp.logical_and(lo_row <= row0, hi_row >= row0 + SUB_BLOCK)

        @pl.when(jnp.logical_and(touched, whole))
        def _():
            sub_block(s, row0, shared=False)

        @pl.when(jnp.logical_and(touched, jnp.logical_not(whole)))
        def _():
            sub_block(s, row0, shared=True)

    for s in range(0, rb // SUB_BLOCK, 2):
        row0 = blk * rb + s * SUB_BLOCK
        both = jnp.logical_and(live, jnp.logical_and(lo_row <= row0, hi_row >= row0 + 2 * SUB_BLOCK))

        @pl.when(both)
        def _():
            sub_block(s, row0, shared=False)
            sub_block(s + 1, row0 + SUB_BLOCK, shared=False)

        @pl.when(jnp.logical_not(both))
        def _():
            one(s)
            one(s + 1)

    @pl.when(jnp.logical_and(live, last))
    def _():
        for j in range(PACK_ROWS):
            out_copy(j, blk, slot).start()
        state_ref[1 + slot] = 1

    @pl.when(p == n_pairs - 1)
    def _():
        drain(0)
        drain(1)


def _experts_call(pair_e, pair_blk, pair_ok, starts, ends, next_e, xs, w_gate, w_up, w_down):
    ne, d, de = w_gate.shape
    rb = ROW_BLOCK
    n_pairs = pair_e.shape[0]
    anywhere = pl.BlockSpec(memory_space=pl.ANY)
    grid_spec = pltpu.PrefetchScalarGridSpec(
        num_scalar_prefetch=6,
        grid=(n_pairs,),
        in_specs=[anywhere, anywhere, anywhere, anywhere],
        out_specs=anywhere,
        scratch_shapes=[pltpu.VMEM((d, de), BF16), pltpu.VMEM((d, de), BF16),
                        pltpu.VMEM((de, d), BF16),
                        pltpu.VMEM((2, d, de), F32), pltpu.VMEM((2, d, de), F32),
                        pltpu.VMEM((2, de, d), F32),
                        pltpu.VMEM((2, PACK_ROWS, rb, LANES), I32),
                        pltpu.VMEM((2, PACK_ROWS, rb, LANES), I32),
                        pltpu.SemaphoreType.DMA((2,)), pltpu.SemaphoreType.DMA((2,)),
                        pltpu.SemaphoreType.DMA((2,)),
                        pltpu.SMEM((4,), I32)],
    )
    return pl.pallas_call(
        _experts_kernel,
        out_shape=jax.ShapeDtypeStruct(xs.shape, xs.dtype),
        grid_spec=grid_spec,
        compiler_params=pltpu.CompilerParams(
            dimension_semantics=("arbitrary",), vmem_limit_bytes=VMEM_LIMIT),
        name="experts",
    )(pair_e, pair_blk, pair_ok, starts, ends, next_e, xs, w_gate, w_up, w_down)


def _combine_kernel(y0_ref, y1_ref, y2_ref, y3_ref, x1_ref, hp_ref, w_ref, g2_ref, fg_ref,
                    wsg_ref, wsu_ref, wsd_ref, o_ref):
    tm = x1_ref.shape[0]
    w = w_ref[...].T
    parts = []
    for y_ref in (y0_ref, y1_ref, y2_ref, y3_ref):
        for half in range(2):
            acc = None
            for k in range(TOP_K):
                piece = pltpu.unpack_elementwise(y_ref[k], index=half, packed_dtype=BF16,
                                                 unpacked_dtype=F32) * w[:, k:k + 1]
                acc = piece if acc is None else acc + piece
            parts.append(acc)
    routed = jnp.concatenate(parts, axis=1)
    pieces = []
    for j in range(PACK_ROWS):
        word = hp_ref[j]
        pieces.append(lax.bitcast_convert_type(word.astype(jnp.int16), BF16))
        pieces.append(lax.bitcast_convert_type(lax.shift_right_logical(word, 16).astype(jnp.int16), BF16))
    h2 = jnp.concatenate(pieces, axis=1)
    hid = _silu(jnp.dot(h2, wsg_ref[...], preferred_element_type=F32)) * jnp.dot(
        h2, wsu_ref[...], preferred_element_type=F32)
    shared = jnp.dot(hid.astype(BF16), wsd_ref[...], preferred_element_type=F32)
    x2 = x1_ref[...] + g2_ref[0] * (routed + shared)
    ms = jnp.mean(x2 * x2, axis=-1, keepdims=True)
    o_ref[...] = x2 * lax.rsqrt(ms + EPS) * fg_ref[...]


def _combine_kernel_into(*refs):
    _combine_kernel(*refs[:-2], refs[-1])


def _combine_call(y_tok, x1, hp, w_tok, g2, fgain, wsg, wsu, wsd, seq, part, prev_out):
    n, d = x1.shape
    dsh = wsg.shape[1]
    tm = MOVE_TILE
    per_b = seq // tm
    tiles = y_tok.shape[2] // tm
    t0 = part * tiles
    const = lambda i: (0, 0)

    def piece_spec(j):
        return pl.BlockSpec((None, TOP_K, tm, LANES), lambda i: (j, 0, i, 0))

    in_specs = [piece_spec(j) for j in range(PACK_ROWS)] + [
        pl.BlockSpec((tm, d), lambda i: (t0 + i, 0)),
        pl.BlockSpec((PACK_ROWS, tm, LANES), lambda i: (0, t0 + i, 0)),
        pl.BlockSpec((TOP_K, tm), lambda i: (0, t0 + i)),
        pl.BlockSpec((1, 1, d), lambda i: ((t0 + i) // per_b, 0, 0)),
        pl.BlockSpec((1, d), const),
        pl.BlockSpec((d, dsh), const),
        pl.BlockSpec((d, dsh), const),
        pl.BlockSpec((dsh, d), const)]
    args = [y_tok, y_tok, y_tok, y_tok, x1, hp, w_tok, g2, fgain, wsg, wsu, wsd]
    aliases = {}
    body = _combine_kernel
    if prev_out is not None:
        in_specs.append(pl.BlockSpec(memory_space=pl.ANY))
        args.append(prev_out)
        aliases = {len(args) - 1: 0}
        body = _combine_kernel_into
    return pl.pallas_call(
        body,
        out_shape=jax.ShapeDtypeStruct((n, d), F32),
        grid=(tiles,),
        in_specs=in_specs,
        out_specs=pl.BlockSpec((tm, d), lambda i: (t0 + i, 0)),
        input_output_aliases=aliases,
        compiler_params=pltpu.CompilerParams(
            dimension_semantics=("arbitrary",), vmem_limit_bytes=VMEM_LIMIT),
        name="combine",
    )(*args)


def _pair_tables(counts, n_rows):
    ne = counts.shape[0]
    sizes = counts.astype(I32)
    ends = jnp.cumsum(sizes)
    starts = ends - sizes
    first_blk = starts // ROW_BLOCK
    last_blk = (ends - 1) // ROW_BLOCK
    n_pairs = jnp.where(sizes > 0, last_blk - first_blk + 1, 0)
    pair_end = jnp.cumsum(n_pairs)
    pair_start = pair_end - n_pairs
    max_pairs = n_rows // ROW_BLOCK + ne
    p = jnp.arange(max_pairs, dtype=I32)
    ok = p < pair_end[-1]
    pc = jnp.minimum(p, pair_end[-1] - 1)
    pair_e = jnp.minimum(jnp.sum((pair_end[None, :] <= pc[:, None]).astype(I32), axis=1), ne - 1)
    pair_blk = (first_blk[pair_e] + pc - pair_start[pair_e]).astype(I32)
    eid = jnp.arange(ne, dtype=I32)
    later = lax.cummin(jnp.where(sizes > 0, eid, ne), reverse=True)
    next_e = jnp.concatenate([later[1:], jnp.full((1,), ne, I32)])
    next_e = jnp.where(next_e < ne, next_e, eid)
    return pair_e, pair_blk, ok.astype(I32), starts.astype(I32), ends.astype(I32), next_e


def kernel(x, c, w_ada, b_ada, w_in, lb_logits, hgrn_norm, lam_re, lam_im, log_dt, b_re, b_im,
           c_re, c_im, d_skip, w_glu, b_glu, w_out, w_router, router_bias, w_gate, w_up, w_down,
           ws_gate, ws_up, ws_down, final_gain):
    bsz, seq, d = x.shape
    n = bsz * seq
    fdim = HGRN_HEADS * HGRN_KDIM
    n_chunks = seq // CHUNK
    lb = jnp.cumsum(jax.nn.softmax(lb_logits.astype(F32), axis=0), axis=0)[0].reshape(1, fdim)

    mod = _mod_call(c, w_ada[0], b_ada[0]).reshape(bsz, 6, d)
    nb = w_in.shape[2] - 4 * fdim
    perm = jnp.concatenate([jnp.arange(0, nb, 2), jnp.arange(1, nb, 2)])
    w_in_b = w_in[0].astype(BF16)
    ut, out_a, ucm = _mix_front_call(x, mod, w_in_b[:, :4 * fdim], w_in_b[:, 4 * fdim:][:, perm].T, lb,
                                hgrn_norm[0].reshape(1, fdim))

    cb, lp2, p_tab, q_tab, a1, a2 = _s5_tables(lam_re[0], lam_im[0], log_dt[0], b_re[0], b_im[0],
                                             c_re[0], c_im[0])
    yt = _s5_call(ut, cb, lp2, p_tab, q_tab, a1, a2, n_chunks, bsz)

    wr_t = w_router[0].T
    wr_hi = wr_t.astype(BF16)
    wr_lo = (wr_t - wr_hi.astype(F32)).astype(BF16)
    w_out_b = w_out[0].astype(BF16)
    w_out_p = jnp.concatenate([w_out_b[:fdim], w_out_b[fdim:][perm]], axis=0)
    x1, hp, logits_t = _mix_back_call(
        x.reshape(n, d), out_a.reshape(n, fdim), yt, ucm, mod,
        d_skip[0][perm].reshape(nb, 1), w_glu[0][perm][:, perm].T.astype(BF16),
        b_glu[0][perm].reshape(nb, 1), w_out_p, wr_hi, wr_lo, seq)

    top_idx, top_w, rank, counts = _route_call(logits_t, router_bias[0])
    pair_e, pair_blk, pair_ok, starts, ends, next_e = _pair_tables(counts[:, 0], n * TOP_K)

    def windowed(a):
        return a.reshape(TOP_K, n // SC_WINDOW, SC_WINDOW).transpose(1, 0, 2)

    xs, pos_win = _dispatch_call(windowed(top_idx), windowed(rank), starts,
                                 hp)
    y_sorted = _experts_call(pair_e, pair_blk, pair_ok, starts, ends, next_e, xs,
                             w_gate[0], w_up[0], w_down[0])
    w_tok = top_w
    g2 = mod[:, 5:6, :]
    fgain = final_gain.reshape(1, d)
    wsg, wsu, wsd = ws_gate[0].astype(BF16), ws_up[0].astype(BF16), ws_down[0].astype(BF16)
    wins = pos_win.shape[0] // TAIL_PARTS
    out = None
    for part in range(TAIL_PARTS):
        y_tok = _collect_call(pos_win[part * wins:(part + 1) * wins], y_sorted)
        out = _combine_call(y_tok, x1, hp, w_tok, g2, fgain, wsg, wsu, wsd, seq, part, out)
    return out.reshape(bsz, seq, d)
```

```python
import functools

import jax
import jax.numpy as jnp
from jax import lax
from jax.experimental import pallas as pl
from jax.experimental.pallas import tpu as pltpu
from jax.experimental.pallas import tpu_sc as plsc

F32 = jnp.float32
BF16 = jnp.bfloat16
I32 = jnp.int32

EPS = 1e-6
CHUNK = 64
HGRN_HEADS = 4
HGRN_KDIM = 128
S5_GROUP = 16
S5_STATE = 64
N_EXPERT_GROUPS = 8
TOPK_GROUPS = 4
TOP_K = 8
ROUTE_SCALE = 2.5
LANES = 128
PACK_ROWS = 4
SC_CORES = 2
SC_SUBCORES = 16
SC_LANES = 16
SC_WINDOW = 64

SEQ_TILE = 512
TOK_TILE = 512
ROUTE_TILE = 512
MOVE_TILE = 512
ROW_BLOCK = 2048
SUB_BLOCK = 512
TAIL_PARTS = 4
VMEM_LIMIT = 56 * 1024 * 1024

_NT = (((1,), (1,)), ((), ()))
_TN = (((0,), (0,)), ((), ()))


def _sigmoid(v):
    return 0.5 * jnp.tanh(0.5 * v) + 0.5


def _silu(v):
    return v * _sigmoid(v)


def _bdot(a, b):
    return jnp.dot(a.astype(BF16), b.astype(BF16), preferred_element_type=F32)


def _store_packed_planes(ref, val):
    for j in range(PACK_ROWS):
        lo = val[:, 2 * j * LANES:(2 * j + 1) * LANES]
        hi = val[:, (2 * j + 1) * LANES:(2 * j + 2) * LANES]
        ref[j] = pltpu.pack_elementwise([lo, hi], packed_dtype=BF16)


def _mod_kernel(c_ref, w_ref, b_ref, o_ref):
    o_ref[...] = _bdot(_silu(c_ref[...]), w_ref[...]) + b_ref[...]


def _mod_call(c, w_ada, b_ada):
    bsz, d = c.shape
    n_out = w_ada.shape[1]
    return pl.pallas_call(
        _mod_kernel,
        out_shape=jax.ShapeDtypeStruct((bsz, n_out), F32),
        grid=(n_out // d,),
        in_specs=[pl.BlockSpec((bsz, d), lambda j: (0, 0)),
                  pl.BlockSpec((d, d), lambda j: (0, j)),
                  pl.BlockSpec((1, d), lambda j: (0, j))],
        out_specs=pl.BlockSpec((bsz, d), lambda j: (0, j)),
        compiler_params=pltpu.CompilerParams(vmem_limit_bytes=VMEM_LIMIT),
        name="mod",
    )(c, w_ada, b_ada.reshape(1, n_out))


def _split_chunk_pairs(tile_even, tile_odd):
    low = lax.broadcasted_iota(I32, tile_even.shape, 1) < CHUNK
    first = jnp.where(low, tile_even, pltpu.roll(tile_odd, CHUNK, 1))
    second = jnp.where(low, pltpu.roll(tile_even, CHUNK, 1), tile_odd)
    return first, second


def _mix_front_kernel(x_ref, mod_ref, win_ref, wut_ref, lb_ref, gn_ref, ltri_ref,
                      ut_ref, oa_ref, ucm_ref, proj_a, proj_b, st_ref, flat_ref, hb_ref, *, tiles_per_seq):
    fdim = HGRN_HEADS * HGRN_KDIM
    ts = x_ref.shape[1]
    pairs = wut_ref.shape[0] // 2
    rows = ts // CHUNK
    i = pl.program_id(0)

    @pl.when(i == 0)
    def _():
        proj_b[...] = jnp.zeros_like(proj_b)

    @pl.when(lax.rem(jnp.maximum(i - 1, 0), tiles_per_seq) == 0)
    def _():
        st_ref[...] = jnp.zeros_like(st_ref)

    def normalise():
        x = x_ref[0]
        ms = jnp.mean(x * x, axis=-1, keepdims=True)
        h = x * lax.rsqrt(ms + EPS) * (1.0 + mod_ref[0, 1:2, :]) + mod_ref[0, 0:1, :]
        hb_ref[...] = h.astype(BF16)

    def project_slab(proj_ref, n):
        width = proj_ref.shape[1] // rows
        cols = slice(n * width, (n + 1) * width)
        proj_ref[:, cols] = jnp.dot(hb_ref[...], win_ref[:, cols], preferred_element_type=F32)

    def s5_input():
        u_t = lax.dot_general(wut_ref[...], hb_ref[...], _NT, preferred_element_type=F32)
        ucm_ref[...] = u_t.astype(BF16)
        for m in range(ts // LANES):
            first, second = _split_chunk_pairs(u_t[:pairs, m * LANES:(m + 1) * LANES],
                                               u_t[pairs:, m * LANES:(m + 1) * LANES])
            flat_ref[2 * m * pairs:(2 * m + 1) * pairs, :] = first
            flat_ref[(2 * m + 1) * pairs:(2 * m + 2) * pairs, :] = second
        per_group = S5_GROUP // 2
        for g in range(pairs // per_group):
            slab = jnp.stack([flat_ref[c * pairs + g * per_group:c * pairs + (g + 1) * per_group, :]
                              for c in range(rows)], axis=0)
            ut_ref[g] = slab.reshape(rows, per_group * LANES)

    lb = lb_ref[...]
    gn = gn_ref[...]
    ltri = ltri_ref[...]
    row = lax.broadcasted_iota(I32, (CHUNK, CHUNK), 0)
    col = lax.broadcasted_iota(I32, (CHUNK, CHUNK), 1)
    causal = row >= col

    def step(proj_ref, next_ref):
        normalise()
        for ci in range(rows):
            project_slab(next_ref, ci)
            r0 = ci * CHUNK
            q = proj_ref[r0:r0 + CHUNK, 0:fdim]
            fl = proj_ref[r0:r0 + CHUNK, fdim:2 * fdim]
            iv = proj_ref[r0:r0 + CHUNK, 2 * fdim:3 * fdim]
            og = proj_ref[r0:r0 + CHUNK, 3 * fdim:4 * fdim]
            f = lb + (1.0 - lb) * _sigmoid(fl)
            lf_hi, lf_mid, lf_lo = _split3(jnp.log(f))
            b = (jnp.dot(ltri, lf_hi, preferred_element_type=F32)
                 + jnp.dot(ltri, lf_mid, preferred_element_type=F32)
                 + jnp.dot(ltri, lf_lo, preferred_element_type=F32))
            b_ref = b[CHUNK // 2 - 1:CHUNK // 2, :]
            b_last = b[CHUNK - 1:CHUNK, :]
            qs = _silu(q)
            kk = 1.0 - f
            qe = (qs * jnp.exp(b - b_ref)).astype(BF16)
            ke = (kk * jnp.exp(b_ref - b)).astype(BF16)
            qb = (qs * jnp.exp(b)).astype(BF16)
            k2 = (kk * jnp.exp(b_last - b)).astype(BF16)
            dec = jnp.exp(b_last)
            ivb = iv.astype(BF16)
            outs = []
            for hh in range(HGRN_HEADS):
                sl = slice(hh * HGRN_KDIM, (hh + 1) * HGRN_KDIM)
                att = lax.dot_general(qe[:, sl], ke[:, sl], _NT, preferred_element_type=F32)
                att = jnp.where(causal, att, 0.0)
                st = st_ref[hh]
                o = jnp.dot(att.astype(BF16), ivb[:, sl], preferred_element_type=F32)
                o = o + lax.dot_general(qb[:, sl], st.astype(BF16), _NT, preferred_element_type=F32)
                st_ref[hh] = st * dec[:, sl] + lax.dot_general(
                    ivb[:, sl], k2[:, sl], _TN, preferred_element_type=F32)
                outs.append(o * lax.rsqrt(jnp.mean(o * o, axis=-1, keepdims=True) + EPS))
            o = jnp.concatenate(outs, axis=1) * gn * _silu(og)
            oa_ref[0, r0:r0 + CHUNK, :] = o.astype(BF16)
        s5_input()

    @pl.when(lax.rem(i, 2) == 0)
    def _():
        step(proj_b, proj_a)

    @pl.when(lax.rem(i, 2) == 1)
    def _():
        step(proj_a, proj_b)


def _mix_front_call(x, mod, w_main, w_ut, lb, gn):
    bsz, seq, d = x.shape
    fdim = HGRN_HEADS * HGRN_KDIM
    ncols = w_main.shape[1]
    nb = w_ut.shape[0]
    groups = nb // S5_GROUP
    ltri = jnp.tril(jnp.ones((CHUNK, CHUNK), BF16))
    ts = SEQ_TILE
    tiles = seq // ts
    n_tiles = bsz * tiles
    rows = ts // CHUNK

    def cur(i):
        return jnp.minimum(i, n_tiles - 1)

    def prev(i):
        return jnp.maximum(i - 1, 0)

    return pl.pallas_call(
        functools.partial(_mix_front_kernel, tiles_per_seq=tiles),
        out_shape=(jax.ShapeDtypeStruct((groups, bsz * seq // CHUNK, S5_GROUP * CHUNK), F32),
                   jax.ShapeDtypeStruct((bsz, seq, fdim), BF16),
                   jax.ShapeDtypeStruct((nb, bsz * seq), BF16)),
        grid=(n_tiles + 1,),
        in_specs=[pl.BlockSpec((1, ts, d), lambda i: (cur(i) // tiles, cur(i) % tiles, 0)),
                  pl.BlockSpec((1, 6, d), lambda i: (cur(i) // tiles, 0, 0)),
                  pl.BlockSpec((d, ncols), lambda i: (0, 0)),
                  pl.BlockSpec((nb, d), lambda i: (0, 0)),
                  pl.BlockSpec((1, fdim), lambda i: (0, 0)),
                  pl.BlockSpec((1, fdim), lambda i: (0, 0)),
                  pl.BlockSpec((CHUNK, CHUNK), lambda i: (0, 0))],
        out_specs=(pl.BlockSpec((groups, rows, S5_GROUP * CHUNK), lambda i: (0, cur(i), 0)),
                   pl.BlockSpec((1, ts, fdim), lambda i: (prev(i) // tiles, prev(i) % tiles, 0)),
                   pl.BlockSpec((nb, ts), lambda i: (0, cur(i)))),
        scratch_shapes=[pltpu.VMEM((ts, ncols), F32), pltpu.VMEM((ts, ncols), F32),
                        pltpu.VMEM((HGRN_HEADS, fdim // HGRN_HEADS, HGRN_KDIM), F32),
                        pltpu.VMEM((rows * nb // 2, LANES), F32),
                        pltpu.VMEM((ts, d), BF16)],
        compiler_params=pltpu.CompilerParams(
            dimension_semantics=("arbitrary",), vmem_limit_bytes=VMEM_LIMIT),
        name="mix_front",
    )(x, mod, w_main, w_ut, lb, gn, ltri)


def _s5_tables(lam_re, lam_im, log_dt, b_re, b_im, c_re, c_im):
    t = CHUNK
    lam = lax.complex(jnp.minimum(lam_re, -1e-4), lam_im)
    lam_dt = lam * jnp.exp(log_dt)[:, None]
    lam_bar = jnp.exp(lam_dt)
    b_bar = ((lam_bar - 1.0) / lam)[..., None] * lax.complex(b_re, b_im)
    c_mat = lax.complex(c_re, c_im)
    taus = jnp.arange(t + 1, dtype=F32)
    lam_pow = jnp.exp(lam_dt[:, None, :] * taus[None, :, None])
    g, p = lam.shape
    c = b_re.shape[-1]
    cb = c_mat[:, None, :, :] * b_bar.transpose(0, 2, 1)[:, :, None, :]
    cb = jnp.concatenate([cb.real, -cb.imag], axis=-1)
    cb = cb.reshape(g, c, c // 2, 2 * 2 * p).reshape(g, c * c // 2, 4 * p)
    lp = jnp.concatenate([lam_pow[:, :t].real, lam_pow[:, :t].imag], axis=-1).transpose(0, 2, 1)
    zero = jnp.zeros_like(lp)
    lp2 = jnp.concatenate([jnp.concatenate([lp, zero], axis=2),
                           jnp.concatenate([zero, lp], axis=2)], axis=1)
    pc = lam_pow[:, t - 1::-1][:, :t, :, None] * b_bar[:, None, :, :]
    pc = pc.transpose(0, 3, 1, 2).reshape(g, c * t, p)
    p_tab = jnp.concatenate([pc.real, pc.imag], axis=-1)
    ql = c_mat[:, None, :, :] * lam_pow[:, 1:t + 1, None, :]
    ql = ql.transpose(0, 3, 2, 1).reshape(g, p, c * t)
    q_tab = jnp.concatenate([ql.real, -ql.imag], axis=1)
    lam_t = lam_pow[:, t]
    a1 = jnp.concatenate([lam_t.real, lam_t.real], axis=-1)[:, None, :]
    a2 = jnp.concatenate([-lam_t.imag, lam_t.imag], axis=-1)[:, None, :]
    return cb, lp2, p_tab.astype(BF16), q_tab.astype(BF16), a1, a2


def _split3(v):
    hi = v.astype(BF16)
    rem = v - hi.astype(F32)
    mid = rem.astype(BF16)
    return hi, mid, (rem - mid.astype(F32)).astype(BF16)


def _s5_kernel(u_ref, cb_ref, lp_ref, p_ref, q_ref, a1_ref, a2_ref, y_ref, v_ref, xs_ref, m_ref, k_ref,
               *, n_chunks, n_batch):
    c_hi, c_mid, c_lo = _split3(cb_ref[0])
    l_hi, l_mid, l_lo = _split3(lp_ref[0])
    k_ref[...] = (jnp.dot(c_hi, l_hi, preferred_element_type=F32)
                  + jnp.dot(c_hi, l_mid, preferred_element_type=F32)
                  + jnp.dot(c_mid, l_hi, preferred_element_type=F32)
                  + jnp.dot(c_hi, l_lo, preferred_element_type=F32)
                  + jnp.dot(c_mid, l_mid, preferred_element_type=F32)
                  + jnp.dot(c_lo, l_hi, preferred_element_type=F32))
    n_pairs = m_ref.shape[1] // LANES
    n_in = k_ref.shape[0] // n_pairs
    lane = lax.broadcasted_iota(I32, (CHUNK, LANES), 1)
    causal = (lane & (CHUNK - 1)) >= lax.broadcasted_iota(I32, (CHUNK, LANES), 0)
    for ci in range(n_in):
        for a in range(n_pairs):
            lags = jnp.broadcast_to(k_ref[ci * n_pairs + a:ci * n_pairs + a + 1, :], (CHUNK, LANES))
            tile = pltpu.roll(lags, 0, 1, stride=1, stride_axis=0)
            m_ref[ci * CHUNK:(ci + 1) * CHUNK, a * LANES:(a + 1) * LANES] = jnp.where(
                causal, tile, 0.0).astype(BF16)
    u = u_ref[0].astype(BF16)
    v_ref[...] = jnp.dot(u, p_ref[0], preferred_element_type=F32)
    a1 = a1_ref[0]
    a2 = a2_ref[0]
    half = xs_ref.shape[1] // 2

    def step(n, state):
        xs_ref[pl.ds(n, n_batch, stride=n_chunks), :] = state
        return (a1 * state + a2 * pltpu.roll(state, half, 1)
                + v_ref[pl.ds(n, n_batch, stride=n_chunks), :])

    lax.fori_loop(0, n_chunks, step, jnp.zeros((n_batch, xs_ref.shape[1]), F32))
    y = jnp.dot(u, m_ref[...], preferred_element_type=F32)
    y_ref[0] = y + jnp.dot(xs_ref[...].astype(BF16), q_ref[0], preferred_element_type=F32)


def _s5_call(ut, cb, lp2, p_tab, q_tab, a1, a2, n_chunks, n_batch):
    g, rows, width = ut.shape
    p2 = p_tab.shape[-1]
    return pl.pallas_call(
        functools.partial(_s5_kernel, n_chunks=n_chunks, n_batch=n_batch),
        out_shape=jax.ShapeDtypeStruct((g, rows, width), F32),
        grid=(g,),
        in_specs=[pl.BlockSpec((1, rows, width), lambda i: (i, 0, 0)),
                  pl.BlockSpec((1,) + cb.shape[1:], lambda i: (i, 0, 0)),
                  pl.BlockSpec((1,) + lp2.shape[1:], lambda i: (i, 0, 0)),
                  pl.BlockSpec((1, width, p2), lambda i: (i, 0, 0)),
                  pl.BlockSpec((1, p2, width), lambda i: (i, 0, 0)),
                  pl.BlockSpec((1, 1, p2), lambda i: (i, 0, 0)),
                  pl.BlockSpec((1, 1, p2), lambda i: (i, 0, 0))],
        out_specs=pl.BlockSpec((1, rows, width), lambda i: (i, 0, 0)),
        scratch_shapes=[pltpu.VMEM((rows, p2), F32), pltpu.VMEM((rows, p2), F32),
                        pltpu.VMEM((width, width), BF16),
                        pltpu.VMEM((cb.shape[1], lp2.shape[2]), F32)],
        compiler_params=pltpu.CompilerParams(
            dimension_semantics=("arbitrary",), vmem_limit_bytes=VMEM_LIMIT),
        name="s5",
    )(ut, cb, lp2, p_tab, q_tab, a1, a2)


def _token_major(flat_ref, src_ref):
    groups, rows, _ = src_ref.shape
    per_group = S5_GROUP // 2
    pairs = groups * per_group
    for g in range(groups):
        slab = src_ref[g].reshape(rows, per_group, LANES)
        for c in range(rows):
            flat_ref[c * pairs + g * per_group:c * pairs + (g + 1) * per_group, :] = slab[c]
    tiles = []
    for m in range(rows // 2):
        even, odd = _split_chunk_pairs(flat_ref[2 * m * pairs:(2 * m + 1) * pairs, :],
                                       flat_ref[(2 * m + 1) * pairs:(2 * m + 2) * pairs, :])
        tiles.append(jnp.concatenate([even, odd], axis=0))
    return jnp.concatenate(tiles, axis=1)


def _mix_back_kernel(x_ref, oa_ref, yt_ref, ucm_ref, mod_ref, dskip_ref, wglu_ref, bglu_ref,
                     wout_ref, wrh_ref, wrl_ref,
                     x1_ref, hp_ref, lt_ref, flat_ref):
    na = oa_ref.shape[1]
    y_t = _token_major(flat_ref, yt_ref)
    u_t = ucm_ref[...].astype(F32)
    z_t = jax.nn.gelu(y_t + dskip_ref[...] * u_t)
    gate_t = _sigmoid(jnp.dot(wglu_ref[...], z_t.astype(BF16), preferred_element_type=F32)
                      + bglu_ref[...])
    ob_t = (z_t * gate_t).astype(BF16)
    mixed = (jnp.dot(oa_ref[...], wout_ref[0:na, :], preferred_element_type=F32)
             + lax.dot_general(ob_t, wout_ref[na:, :], _TN, preferred_element_type=F32))
    x1 = x_ref[...] + mod_ref[0, 2:3, :] * mixed
    ms = jnp.mean(x1 * x1, axis=-1, keepdims=True)
    h2 = x1 * lax.rsqrt(ms + EPS) * (1.0 + mod_ref[0, 4:5, :]) + mod_ref[0, 3:4, :]
    _store_packed_planes(hp_ref, h2)
    h_hi = h2.astype(BF16)
    h_lo = (h2 - h_hi.astype(F32)).astype(BF16)
    lt = lax.dot_general(wrh_ref[...], h_hi, _NT, preferred_element_type=F32)
    lt = lt + lax.dot_general(wrl_ref[...], h_hi, _NT, preferred_element_type=F32)
    lt = lt + lax.dot_general(wrh_ref[...], h_lo, _NT, preferred_element_type=F32)
    lt_ref[...] = lt
    x1_ref[...] = x1


def _mix_back_call(x2d, oa, yt, ucm, mod, dskip, wglu, bglu, wout, wrh, wrl, seq):
    n, d = x2d.shape
    nb = oa.shape[1]
    ne = wrh.shape[0]
    tm = TOK_TILE
    per_b = seq // tm
    groups, _, width = yt.shape
    rows = tm // CHUNK
    const = lambda i: (0, 0)
    flat_block = pl.BlockSpec((groups, rows, width), lambda i: (0, i, 0))
    return pl.pallas_call(
        _mix_back_kernel,
        out_shape=(jax.ShapeDtypeStruct((n, d), F32),
                   jax.ShapeDtypeStruct((PACK_ROWS, n, LANES), I32),
                   jax.ShapeDtypeStruct((ne, n), F32)),
        grid=(n // tm,),
        in_specs=[pl.BlockSpec((tm, d), lambda i: (i, 0)),
                  pl.BlockSpec((tm, nb), lambda i: (i, 0)),
                  flat_block,
                  pl.BlockSpec((nb, tm), lambda i: (0, i)),
                  pl.BlockSpec((1, 6, d), lambda i: (i // per_b, 0, 0)),
                  pl.BlockSpec((nb, 1), const),
                  pl.BlockSpec((nb, nb), const),
                  pl.BlockSpec((nb, 1), const),
                  pl.BlockSpec((d, d), const),
                  pl.BlockSpec((ne, d), const),
                  pl.BlockSpec((ne, d), const)],
        out_specs=(pl.BlockSpec((tm, d), lambda i: (i, 0)),
                   pl.BlockSpec((PACK_ROWS, tm, LANES), lambda i: (0, i, 0)),
                   pl.BlockSpec((ne, tm), lambda i: (0, i))),
        scratch_shapes=[pltpu.VMEM((rows * nb // 2, LANES), F32)],
        compiler_params=pltpu.CompilerParams(
            dimension_semantics=("arbitrary",), vmem_limit_bytes=VMEM_LIMIT),
        name="mix_back",
    )(x2d, oa, yt, ucm, mod, dskip, wglu, bglu, wout, wrh, wrl)


def _route_kernel(lt_ref, bias_ref, su_ref, idx_ref, w_ref, rank_ref, cnt_ref, run_ref):
    ne, tr = lt_ref.shape
    per_group = ne // N_EXPERT_GROUPS
    neg = -jnp.inf

    @pl.when(pl.program_id(0) == 0)
    def _():
        run_ref[...] = jnp.zeros_like(run_ref)

    s = _sigmoid(lt_ref[...])
    sel = s + bias_ref[...]
    gio = lax.broadcasted_iota(I32, (per_group, tr), 0)
    gscore = []
    for g in range(N_EXPERT_GROUPS):
        v = sel[g * per_group:(g + 1) * per_group, :]
        m1 = jnp.max(v, axis=0, keepdims=True)
        i1 = jnp.min(jnp.where(v == m1, gio, per_group), axis=0, keepdims=True)
        m2 = jnp.max(jnp.where(gio == i1, neg, v), axis=0, keepdims=True)
        gscore.append(m1 + m2)
    masked = []
    for g in range(N_EXPERT_GROUPS):
        ahead = jnp.zeros((1, tr), I32)
        for o in range(N_EXPERT_GROUPS):
            if o == g:
                continue
            wins = (gscore[o] >= gscore[g]) if o < g else (gscore[o] > gscore[g])
            ahead = ahead + wins.astype(I32)
        keep = ahead < TOPK_GROUPS
        masked.append(jnp.where(keep, sel[g * per_group:(g + 1) * per_group, :], neg))
    selm = jnp.concatenate(masked, axis=0)
    eio = lax.broadcasted_iota(I32, (ne, tr), 0)
    candidate = selm > neg
    idxs, ws = [], []
    for k in range(TOP_K):
        m = jnp.max(selm, axis=0, keepdims=True)
        ik = jnp.min(jnp.where(selm == m, eio, ne), axis=0, keepdims=True)
        onehot = eio == ik
        ws.append(jnp.sum(jnp.where(onehot, s, 0.0), axis=0, keepdims=True))
        selm = jnp.where(onehot, neg, selm)
        idxs.append(ik)
    hits = jnp.where(jnp.logical_and(candidate, selm == neg), 1.0, 0.0)
    wsum = ws[0]
    for k in range(1, TOP_K):
        wsum = wsum + ws[k]
    scale = ROUTE_SCALE / wsum
    ranks = jnp.dot(hits.astype(BF16), su_ref[...], preferred_element_type=F32) + run_ref[...]
    for k in range(TOP_K):
        idx_ref[k:k + 1, :] = idxs[k]
        w_ref[k:k + 1, :] = ws[k] * scale
        rk = jnp.sum(jnp.where(eio == idxs[k], ranks, 0.0), axis=0, keepdims=True)
        rank_ref[k:k + 1, :] = rk.astype(I32)
    run_ref[...] = run_ref[...] + jnp.sum(hits, axis=1, keepdims=True)
    cnt_ref[...] = run_ref[...]


def _route_call(lt, bias):
    ne, n = lt.shape
    tr = ROUTE_TILE
    su = jnp.triu(jnp.ones((tr, tr), F32), k=1).astype(BF16)
    return pl.pallas_call(
        _route_kernel,
        out_shape=(jax.ShapeDtypeStruct((TOP_K, n), I32),
                   jax.ShapeDtypeStruct((TOP_K, n), F32),
                   jax.ShapeDtypeStruct((TOP_K, n), I32),
                   jax.ShapeDtypeStruct((ne, 1), F32)),
        grid=(n // tr,),
        in_specs=[pl.BlockSpec((ne, tr), lambda i: (0, i)),
                  pl.BlockSpec((ne, 1), lambda i: (0, 0)),
                  pl.BlockSpec((tr, tr), lambda i: (0, 0))],
        out_specs=(pl.BlockSpec((TOP_K, tr), lambda i: (0, i)),
                   pl.BlockSpec((TOP_K, tr), lambda i: (0, i)),
                   pl.BlockSpec((TOP_K, tr), lambda i: (0, i)),
                   pl.BlockSpec((ne, 1), lambda i: (0, 0))),
        scratch_shapes=[pltpu.VMEM((ne, 1), F32)],
        compiler_params=pltpu.CompilerParams(
            dimension_semantics=("arbitrary",), vmem_limit_bytes=VMEM_LIMIT),
        name="route",
    )(lt, bias.reshape(ne, 1), su)


def _sc_mesh():
    return plsc.VectorSubcoreMesh(core_axis_name="c", subcore_axis_name="s",
                                  num_cores=SC_CORES, num_subcores=SC_SUBCORES)


def _sc_worker():
    return lax.axis_index("s") * SC_CORES + lax.axis_index("c")


def _dispatch_call(idx_win, rank_win, starts, hp):
    n = hp.shape[1]
    n_workers = SC_CORES * SC_SUBCORES
    wins_per_worker = n // SC_WINDOW // n_workers
    lanes = SC_LANES

    def body(hp_hbm, idx_hbm, rank_hbm, st_hbm, xs_hbm, pos_hbm,
             idx_a, idx_b, rank_a, rank_b, pos_a, pos_b, rows_a, rows_b, st_v,
             load_a, load_b, scatter_sem, pos_sem):
        first_win = _sc_worker() * wins_per_worker
        idx_v, rank_v, pos_v = (idx_a, idx_b), (rank_a, rank_b), (pos_a, pos_b)
        rows_v, load_sem = (rows_a, rows_b), (load_a, load_b)
        pltpu.sync_copy(st_hbm, st_v)

        def loads(win, slot):
            pieces = tuple(
                pltpu.make_async_copy(hp_hbm.at[j, pl.ds(win * SC_WINDOW, SC_WINDOW)],
                                      rows_v[slot].at[:, j], load_sem[slot])
                for j in range(PACK_ROWS))
            return pieces + (pltpu.make_async_copy(idx_hbm.at[win], idx_v[slot], load_sem[slot]),
                             pltpu.make_async_copy(rank_hbm.at[win], rank_v[slot], load_sem[slot]))

        for cp in loads(first_win, 0):
            cp.start()

        @pl.loop(0, wins_per_worker, step=2)
        def _(w):
            for slot in range(2):
                win = first_win + w + slot
                for cp in loads(win, slot):
                    cp.wait()

                @pl.when(w + slot + 1 < wins_per_worker)
                def _():
                    for cp in loads(win + 1, 1 - slot):
                        cp.start()

                for k in range(TOP_K):
                    for c in range(SC_WINDOW // lanes):
                        seg = pl.ds(c * lanes, lanes)
                        base = plsc.load_gather(st_v, [idx_v[slot][k, seg]])
                        pos_v[slot][k, seg] = base + rank_v[slot][k, seg]
                pos_out = pltpu.async_copy(pos_v[slot], pos_hbm.at[win], pos_sem)
                copies = [pltpu.async_copy(rows_v[slot], xs_hbm.at[pos_v[slot].at[k]], scatter_sem)
                          for k in range(TOP_K)]
                for cp in copies:
                    cp.wait()
                pos_out.wait()

    idx_buf = pltpu.VMEM((TOP_K, SC_WINDOW), I32)
    row_buf = pltpu.VMEM((SC_WINDOW, PACK_ROWS, LANES), hp.dtype)
    return pl.kernel(
        body,
        out_type=(jax.ShapeDtypeStruct((n * TOP_K, PACK_ROWS, LANES), hp.dtype),
                  jax.ShapeDtypeStruct(idx_win.shape, I32)),
        mesh=_sc_mesh(),
        scratch_types=[idx_buf, idx_buf, idx_buf, idx_buf, idx_buf, idx_buf, row_buf, row_buf,
                       pltpu.VMEM(starts.shape, I32),
                       pltpu.SemaphoreType.DMA, pltpu.SemaphoreType.DMA, pltpu.SemaphoreType.DMA,
                       pltpu.SemaphoreType.DMA],
        compiler_params=pltpu.CompilerParams(needs_layout_passes=False),
        name="dispatch",
    )(hp, idx_win, rank_win, starts)


def _collect_call(pos_win, y_sorted):
    n = pos_win.shape[0] * SC_WINDOW
    n_workers = SC_CORES * SC_SUBCORES
    wins_per_worker = n // SC_WINDOW // n_workers

    def body(ys_hbm, pos_hbm, out_hbm, idx_a, idx_b, rows_a, rows_b, idx_sem_a, idx_sem_b,
             gather_sem, write_sem):
        first_win = _sc_worker() * wins_per_worker
        bufs = (rows_a, rows_b)
        idx_v, idx_sem = (idx_a, idx_b), (idx_sem_a, idx_sem_b)

        def idx_load(win, slot):
            return pltpu.make_async_copy(pos_hbm.at[win], idx_v[slot], idx_sem[slot])

        idx_load(first_win, 0).start()

        @pl.loop(0, wins_per_worker, step=2)
        def _(w):
            for slot in range(2):
                win = first_win + w + slot
                idx_load(win, slot).wait()

                @pl.when(w + slot + 1 < wins_per_worker)
                def _():
                    idx_load(win + 1, 1 - slot).start()

                def gather(k):
                    return pltpu.async_copy(ys_hbm.at[idx_v[slot].at[k]], bufs[k % 2], gather_sem)

                pending_gather = gather(0)
                pending_write = None
                for k in range(TOP_K):
                    pending_gather.wait()
                    if pending_write is not None:
                        for cp in pending_write:
                            cp.wait()
                    if k + 1 < TOP_K:
                        pending_gather = gather(k + 1)
                    pending_write = [
                        pltpu.async_copy(bufs[k % 2].at[:, j],
                                         out_hbm.at[j, k, pl.ds(win * SC_WINDOW, SC_WINDOW)], write_sem)
                        for j in range(PACK_ROWS)]
                for cp in pending_write:
                    cp.wait()

    idx_buf = pltpu.VMEM((TOP_K, SC_WINDOW), I32)
    row_buf = pltpu.VMEM((SC_WINDOW,) + y_sorted.shape[1:], y_sorted.dtype)
    return pl.kernel(
        body,
        out_type=jax.ShapeDtypeStruct((PACK_ROWS, TOP_K, n, LANES), y_sorted.dtype),
        mesh=_sc_mesh(),
        scratch_types=[idx_buf, idx_buf, row_buf, row_buf,
                       pltpu.SemaphoreType.DMA, pltpu.SemaphoreType.DMA,
                       pltpu.SemaphoreType.DMA, pltpu.SemaphoreType.DMA],
        name="collect",
    )(y_sorted, pos_win)


def _experts_kernel(pe_ref, pb_ref, pv_ref, st_ref, en_ref, nx_ref,
                    xs_hbm, wg_hbm, wu_hbm, wd_hbm, y_hbm,
                    wgb_ref, wub_ref, wdb_ref, wgf_ref, wuf_ref, wdf_ref, xbuf_ref, stage_ref,
                    in_sems, out_sems, w_sems, state_ref):
    p = pl.program_id(0)
    n_pairs = pl.num_programs(0)
    e = pe_ref[p]
    blk = pb_ref[p]
    rb = xbuf_ref.shape[2]
    prev = jnp.maximum(p - 1, 0)
    nxt = jnp.minimum(p + 1, n_pairs - 1)
    live = pv_ref[p] == 1
    first = jnp.logical_or(p == 0, pb_ref[prev] != blk)
    block_ends = jnp.logical_or(pb_ref[nxt] != blk, pv_ref[nxt] == 0)
    last = jnp.logical_or(p == n_pairs - 1, block_ends)

    def in_copy(j, block, slot):
        return pltpu.make_async_copy(xs_hbm.at[pl.ds(block * rb, rb), j], xbuf_ref.at[slot, j],
                                     in_sems.at[slot])

    def out_copy(j, block, slot):
        return pltpu.make_async_copy(stage_ref.at[slot, j], y_hbm.at[pl.ds(block * rb, rb), j],
                                     out_sems.at[slot])

    def drain(slot):
        @pl.when(state_ref[1 + slot] == 1)
        def _():
            for j in range(PACK_ROWS):
                out_copy(j, 0, slot).wait()
            state_ref[1 + slot] = 0

    @pl.when(p == 0)
    def _():
        state_ref[0] = 0
        state_ref[1] = 0
        state_ref[2] = 0
        stage_ref[...] = jnp.zeros_like(stage_ref)
        for j in range(PACK_ROWS):
            in_copy(j, blk, 0).start()

    slot = jnp.where(jnp.logical_and(first, p > 0), 1 - state_ref[0], state_ref[0])
    state_ref[0] = slot

    @pl.when(jnp.logical_and(first, blk + 1 < y_hbm.shape[0] // rb))
    def _():
        for j in range(PACK_ROWS):
            in_copy(j, blk + 1, 1 - slot).start()

    @pl.when(first)
    def _():
        for j in range(PACK_ROWS):
            in_copy(j, blk, slot).wait()
        drain(slot)

    def w_copies(expert, wslot):
        return (pltpu.make_async_copy(wg_hbm.at[expert], wgf_ref.at[wslot], w_sems.at[wslot]),
                pltpu.make_async_copy(wu_hbm.at[expert], wuf_ref.at[wslot], w_sems.at[wslot]),
                pltpu.make_async_copy(wd_hbm.at[expert], wdf_ref.at[wslot], w_sems.at[wslot]))

    @pl.when(p == 0)
    def _():
        state_ref[3] = 0
        for cp in w_copies(e, 0):
            cp.start()

    new_expert = jnp.logical_or(p == 0, pe_ref[prev] != e)
    wslot = jnp.where(jnp.logical_and(new_expert, p > 0), 1 - state_ref[3], state_ref[3])
    state_ref[3] = wslot

    @pl.when(new_expert)
    def _():
        for cp in w_copies(e, wslot):
            cp.wait()
        wgb_ref[...] = wgf_ref[wslot].astype(BF16)
        wub_ref[...] = wuf_ref[wslot].astype(BF16)
        wdb_ref[...] = wdf_ref[wslot].astype(BF16)

        @pl.when(nx_ref[e] != e)
        def _():
            for cp in w_copies(nx_ref[e], 1 - wslot):
                cp.start()

    lo_row = st_ref[e]
    hi_row = en_ref[e]

    def sub_block(s, row0, shared):
        sub = pl.ds(s * SUB_BLOCK, SUB_BLOCK)
        pieces = []
        for j in range(PACK_ROWS):
            w = xbuf_ref[slot, j, sub, :]
            pieces.append(lax.bitcast_convert_type(w.astype(jnp.int16), BF16))
            pieces.append(lax.bitcast_convert_type(
                lax.shift_right_logical(w, 16).astype(jnp.int16), BF16))
        xb = jnp.concatenate(pieces, axis=1)
        gate = jnp.dot(xb, wgb_ref[...], preferred_element_type=F32)
        up = jnp.dot(xb, wub_ref[...], preferred_element_type=F32)
        yb = jnp.dot((_silu(gate) * up).astype(BF16), wdb_ref[...], preferred_element_type=F32)
        if shared:
            rows = row0 + lax.broadcasted_iota(I32, (SUB_BLOCK, 1), 0)
            mine = jnp.logical_and(rows >= lo_row, rows < hi_row)
        for j in range(PACK_ROWS):
            word = pltpu.pack_elementwise(
                [yb[:, 2 * j * LANES:(2 * j + 1) * LANES], yb[:, (2 * j + 1) * LANES:(2 * j + 2) * LANES]],
                packed_dtype=BF16)
            if shared:
                word = jnp.where(mine, word, stage_ref[slot, j, sub, :])
            stage_ref[slot, j, sub, :] = word

    def one(s):
        row0 = blk * rb + s * SUB_BLOCK
        touched = jnp.logical_and(live, jnp.logical_and(row0 < hi_row, row0 + SUB_BLOCK > lo_row))
        whole = jnp.logical_and(lo_row <= row0, hi_row >= row0 + SUB_BLOCK)

        @pl.when(jnp.logical_and(touched, whole))
        def _():
            sub_block(s, row0, shared=False)

        @pl.when(jnp.logical_and(touched, jnp.logical_not(whole)))
        def _():
            sub_block(s, row0, shared=True)

    for s in range(0, rb // SUB_BLOCK, 2):
        row0 = blk * rb + s * SUB_BLOCK
        both = jnp.logical_and(live, jnp.logical_and(lo_row <= row0, hi_row >= row0 + 2 * SUB_BLOCK))

        @pl.when(both)
        def _():
            sub_block(s, row0, shared=False)
            sub_block(s + 1, row0 + SUB_BLOCK, shared=False)

        @pl.when(jnp.logical_not(both))
        def _():
            one(s)
            one(s + 1)

    @pl.when(jnp.logical_and(live, last))
    def _():
        for j in range(PACK_ROWS):
            out_copy(j, blk, slot).start()
        state_ref[1 + slot] = 1

    @pl.when(p == n_pairs - 1)
    def _():
        drain(0)
        drain(1)


def _experts_call(pair_e, pair_blk, pair_ok, starts, ends, next_e, xs, w_gate, w_up, w_down):
    ne, d, de = w_gate.shape
    rb = ROW_BLOCK
    n_pairs = pair_e.shape[0]
    anywhere = pl.BlockSpec(memory_space=pl.ANY)
    grid_spec = pltpu.PrefetchScalarGridSpec(
        num_scalar_prefetch=6,
        grid=(n_pairs,),
        in_specs=[anywhere, anywhere, anywhere, anywhere],
        out_specs=anywhere,
        scratch_shapes=[pltpu.VMEM((d, de), BF16), pltpu.VMEM((d, de), BF16),
                        pltpu.VMEM((de, d), BF16),
                        pltpu.VMEM((2, d, de), F32), pltpu.VMEM((2, d, de), F32),
                        pltpu.VMEM((2, de, d), F32),
                        pltpu.VMEM((2, PACK_ROWS, rb, LANES), I32),
                        pltpu.VMEM((2, PACK_ROWS, rb, LANES), I32),
                        pltpu.SemaphoreType.DMA((2,)), pltpu.SemaphoreType.DMA((2,)),
                        pltpu.SemaphoreType.DMA((2,)),
                        pltpu.SMEM((4,), I32)],
    )
    return pl.pallas_call(
        _experts_kernel,
        out_shape=jax.ShapeDtypeStruct(xs.shape, xs.dtype),
        grid_spec=grid_spec,
        compiler_params=pltpu.CompilerParams(
            dimension_semantics=("arbitrary",), vmem_limit_bytes=VMEM_LIMIT),
        name="experts",
    )(pair_e, pair_blk, pair_ok, starts, ends, next_e, xs, w_gate, w_up, w_down)


def _combine_kernel(y0_ref, y1_ref, y2_ref, y3_ref, x1_ref, hp_ref, w_ref, g2_ref, fg_ref,
                    wsg_ref, wsu_ref, wsd_ref, o_ref):
    tm = x1_ref.shape[0]
    w = w_ref[...].T
    parts = []
    for y_ref in (y0_ref, y1_ref, y2_ref, y3_ref):
        for half in range(2):
            acc = None
            for k in range(TOP_K):
                piece = pltpu.unpack_elementwise(y_ref[k], index=half, packed_dtype=BF16,
                                                 unpacked_dtype=F32) * w[:, k:k + 1]
                acc = piece if acc is None else acc + piece
            parts.append(acc)
    routed = jnp.concatenate(parts, axis=1)
    pieces = []
    for j in range(PACK_ROWS):
        word = hp_ref[j]
        pieces.append(lax.bitcast_convert_type(word.astype(jnp.int16), BF16))
        pieces.append(lax.bitcast_convert_type(lax.shift_right_logical(word, 16).astype(jnp.int16), BF16))
    h2 = jnp.concatenate(pieces, axis=1)
    hid = _silu(jnp.dot(h2, wsg_ref[...], preferred_element_type=F32)) * jnp.dot(
        h2, wsu_ref[...], preferred_element_type=F32)
    shared = jnp.dot(hid.astype(BF16), wsd_ref[...], preferred_element_type=F32)
    x2 = x1_ref[...] + g2_ref[0] * (routed + shared)
    ms = jnp.mean(x2 * x2, axis=-1, keepdims=True)
    o_ref[...] = x2 * lax.rsqrt(ms + EPS) * fg_ref[...]


def _combine_kernel_into(*refs):
    _combine_kernel(*refs[:-2], refs[-1])


def _combine_call(y_tok, x1, hp, w_tok, g2, fgain, wsg, wsu, wsd, seq, part, prev_out):
    n, d = x1.shape
    dsh = wsg.shape[1]
    tm = MOVE_TILE
    per_b = seq // tm
    tiles = y_tok.shape[2] // tm
    t0 = part * tiles
    const = lambda i: (0, 0)

    def piece_spec(j):
        return pl.BlockSpec((None, TOP_K, tm, LANES), lambda i: (j, 0, i, 0))

    in_specs = [piece_spec(j) for j in range(PACK_ROWS)] + [
        pl.BlockSpec((tm, d), lambda i: (t0 + i, 0)),
        pl.BlockSpec((PACK_ROWS, tm, LANES), lambda i: (0, t0 + i, 0)),
        pl.BlockSpec((TOP_K, tm), lambda i: (0, t0 + i)),
        pl.BlockSpec((1, 1, d), lambda i: ((t0 + i) // per_b, 0, 0)),
        pl.BlockSpec((1, d), const),
        pl.BlockSpec((d, dsh), const),
        pl.BlockSpec((d, dsh), const),
        pl.BlockSpec((dsh, d), const)]
    args = [y_tok, y_tok, y_tok, y_tok, x1, hp, w_tok, g2, fgain, wsg, wsu, wsd]
    aliases = {}
    body = _combine_kernel
    if prev_out is not None:
        in_specs.append(pl.BlockSpec(memory_space=pl.ANY))
        args.append(prev_out)
        aliases = {len(args) - 1: 0}
        body = _combine_kernel_into
    return pl.pallas_call(
        body,
        out_shape=jax.ShapeDtypeStruct((n, d), F32),
        grid=(tiles,),
        in_specs=in_specs,
        out_specs=pl.BlockSpec((tm, d), lambda i: (t0 + i, 0)),
        input_output_aliases=aliases,
        compiler_params=pltpu.CompilerParams(
            dimension_semantics=("arbitrary",), vmem_limit_bytes=VMEM_LIMIT),
        name="combine",
    )(*args)


def _pair_tables(counts, n_rows):
    ne = counts.shape[0]
    sizes = counts.astype(I32)
    ends = jnp.cumsum(sizes)
    starts = ends - sizes
    first_blk = starts // ROW_BLOCK
    last_blk = (ends - 1) // ROW_BLOCK
    n_pairs = jnp.where(sizes > 0, last_blk - first_blk + 1, 0)
    pair_end = jnp.cumsum(n_pairs)
    pair_start = pair_end - n_pairs
    max_pairs = n_rows // ROW_BLOCK + ne
    p = jnp.arange(max_pairs, dtype=I32)
    ok = p < pair_end[-1]
    pc = jnp.minimum(p, pair_end[-1] - 1)
    pair_e = jnp.minimum(jnp.sum((pair_end[None, :] <= pc[:, None]).astype(I32), axis=1), ne - 1)
    pair_blk = (first_blk[pair_e] + pc - pair_start[pair_e]).astype(I32)
    eid = jnp.arange(ne, dtype=I32)
    later = lax.cummin(jnp.where(sizes > 0, eid, ne), reverse=True)
    next_e = jnp.concatenate([later[1:], jnp.full((1,), ne, I32)])
    next_e = jnp.where(next_e < ne, next_e, eid)
    return pair_e, pair_blk, ok.astype(I32), starts.astype(I32), ends.astype(I32), next_e


def kernel(x, c, w_ada, b_ada, w_in, lb_logits, hgrn_norm, lam_re, lam_im, log_dt, b_re, b_im,
           c_re, c_im, d_skip, w_glu, b_glu, w_out, w_router, router_bias, w_gate, w_up, w_down,
           ws_gate, ws_up, ws_down, final_gain):
    bsz, seq, d = x.shape
    n = bsz * seq
    fdim = HGRN_HEADS * HGRN_KDIM
    n_chunks = seq // CHUNK
    lb = jnp.cumsum(jax.nn.softmax(lb_logits.astype(F32), axis=0), axis=0)[0].reshape(1, fdim)

    mod = _mod_call(c, w_ada[0], b_ada[0]).reshape(bsz, 6, d)
    nb = w_in.shape[2] - 4 * fdim
    perm = jnp.concatenate([jnp.arange(0, nb, 2), jnp.arange(1, nb, 2)])
    w_in_b = w_in[0].astype(BF16)
    ut, out_a, ucm = _mix_front_call(x, mod, w_in_b[:, :4 * fdim], w_in_b[:, 4 * fdim:][:, perm].T, lb,
                                hgrn_norm[0].reshape(1, fdim))

    cb, lp2, p_tab, q_tab, a1, a2 = _s5_tables(lam_re[0], lam_im[0], log_dt[0], b_re[0], b_im[0],
                                             c_re[0], c_im[0])
    yt = _s5_call(ut, cb, lp2, p_tab, q_tab, a1, a2, n_chunks, bsz)

    wr_t = w_router[0].T
    wr_hi = wr_t.astype(BF16)
    wr_lo = (wr_t - wr_hi.astype(F32)).astype(BF16)
    w_out_b = w_out[0].astype(BF16)
    w_out_p = jnp.concatenate([w_out_b[:fdim], w_out_b[fdim:][perm]], axis=0)
    x1, hp, logits_t = _mix_back_call(
        x.reshape(n, d), out_a.reshape(n, fdim), yt, ucm, mod,
        d_skip[0][perm].reshape(nb, 1), w_glu[0][perm][:, perm].T.astype(BF16),
        b_glu[0][perm].reshape(nb, 1), w_out_p, wr_hi, wr_lo, seq)

    top_idx, top_w, rank, counts = _route_call(logits_t, router_bias[0])
    pair_e, pair_blk, pair_ok, starts, ends, next_e = _pair_tables(counts[:, 0], n * TOP_K)

    def windowed(a):
        return a.reshape(TOP_K, n // SC_WINDOW, SC_WINDOW).transpose(1, 0, 2)

    xs, pos_win = _dispatch_call(windowed(top_idx), windowed(rank), starts,
                                 hp)
    y_sorted = _experts_call(pair_e, pair_blk, pair_ok, starts, ends, next_e, xs,
                             w_gate[0], w_up[0], w_down[0])
    w_tok = top_w
    g2 = mod[:, 5:6, :]
    fgain = final_gain.reshape(1, d)
    wsg, wsu, wsd = ws_gate[0].astype(BF16), ws_up[0].astype(BF16), ws_down[0].astype(BF16)
    wins = pos_win.shape[0] // TAIL_PARTS
    out = None
    for part in range(TAIL_PARTS):
        y_tok = _collect_call(pos_win[part * wins:(part + 1) * wins], y_sorted)
        out = _combine_call(y_tok, x1, hp, w_tok, g2, fgain, wsg, wsu, wsd, seq, part, out)
    return out.reshape(bsz, seq, d)
```

```python
import functools

import jax
import jax.numpy as jnp
from jax import lax
from jax.experimental import pallas as pl
from jax.experimental.pallas import tpu as pltpu
from jax.experimental.pallas import tpu_sc as plsc

F32 = jnp.float32
BF16 = jnp.bfloat16
I32 = jnp.int32

EPS = 1e-6
CHUNK = 64
HGRN_HEADS = 4
HGRN_KDIM = 128
S5_GROUP = 16
S5_STATE = 64
N_EXPERT_GROUPS = 8
TOPK_GROUPS = 4
TOP_K = 8
ROUTE_SCALE = 2.5
LANES = 128
PACK_ROWS = 4
SC_CORES = 2
SC_SUBCORES = 16
SC_LANES = 16
SC_WINDOW = 64

SEQ_TILE = 512
TOK_TILE = 512
ROUTE_TILE = 512
MOVE_TILE = 512
ROW_BLOCK = 2048
SUB_BLOCK = 512
TAIL_PARTS = 4
VMEM_LIMIT = 56 * 1024 * 1024

_NT = (((1,), (1,)), ((), ()))
_TN = (((0,), (0,)), ((), ()))


def _sigmoid(v):
    return 0.5 * jnp.tanh(0.5 * v) + 0.5


def _silu(v):
    return v * _sigmoid(v)


def _bdot(a, b):
    return jnp.dot(a.astype(BF16), b.astype(BF16), preferred_element_type=F32)


def _store_packed_planes(ref, val):
    for j in range(PACK_ROWS):
        lo = val[:, 2 * j * LANES:(2 * j + 1) * LANES]
        hi = val[:, (2 * j + 1) * LANES:(2 * j + 2) * LANES]
        ref[j] = pltpu.pack_elementwise([lo, hi], packed_dtype=BF16)


def _mod_kernel(c_ref, w_ref, b_ref, o_ref):
    o_ref[...] = _bdot(_silu(c_ref[...]), w_ref[...]) + b_ref[...]


def _mod_call(c, w_ada, b_ada):
    bsz, d = c.shape
    n_out = w_ada.shape[1]
    return pl.pallas_call(
        _mod_kernel,
        out_shape=jax.ShapeDtypeStruct((bsz, n_out), F32),
        grid=(n_out // d,),
        in_specs=[pl.BlockSpec((bsz, d), lambda j: (0, 0)),
                  pl.BlockSpec((d, d), lambda j: (0, j)),
                  pl.BlockSpec((1, d), lambda j: (0, j))],
        out_specs=pl.BlockSpec((bsz, d), lambda j: (0, j)),
        compiler_params=pltpu.CompilerParams(vmem_limit_bytes=VMEM_LIMIT),
        name="mod",
    )(c, w_ada, b_ada.reshape(1, n_out))


def _split_chunk_pairs(tile_even, tile_odd):
    low = lax.broadcasted_iota(I32, tile_even.shape, 1) < CHUNK
    first = jnp.where(low, tile_even, pltpu.roll(tile_odd, CHUNK, 1))
    second = jnp.where(low, pltpu.roll(tile_even, CHUNK, 1), tile_odd)
    return first, second


def _mix_front_kernel(x_ref, mod_ref, win_ref, wut_ref, lb_ref, gn_ref, ltri_ref,
                      ut_ref, oa_ref, ucm_ref, proj_a, proj_b, st_ref, flat_ref, hb_ref, *, tiles_per_seq):
    fdim = HGRN_HEADS * HGRN_KDIM
    ts = x_ref.shape[1]
    pairs = wut_ref.shape[0] // 2
    rows = ts // CHUNK
    i = pl.program_id(0)

    @pl.when(i == 0)
    def _():
        proj_b[...] = jnp.zeros_like(proj_b)

    @pl.when(lax.rem(jnp.maximum(i - 1, 0), tiles_per_seq) == 0)
    def _():
        st_ref[...] = jnp.zeros_like(st_ref)

    def normalise():
        x = x_ref[0]
        ms = jnp.mean(x * x, axis=-1, keepdims=True)
        h = x * lax.rsqrt(ms + EPS) * (1.0 + mod_ref[0, 1:2, :]) + mod_ref[0, 0:1, :]
        hb_ref[...] = h.astype(BF16)

    def project_slab(proj_ref, n):
        width = proj_ref.shape[1] // rows
        cols = slice(n * width, (n + 1) * width)
        proj_ref[:, cols] = jnp.dot(hb_ref[...], win_ref[:, cols], preferred_element_type=F32)

    def s5_input():
        u_t = lax.dot_general(wut_ref[...], hb_ref[...], _NT, preferred_element_type=F32)
        ucm_ref[...] = u_t.astype(BF16)
        for m in range(ts // LANES):
            first, second = _split_chunk_pairs(u_t[:pairs, m * LANES:(m + 1) * LANES],
                                               u_t[pairs:, m * LANES:(m + 1) * LANES])
            flat_ref[2 * m * pairs:(2 * m + 1) * pairs, :] = first
            flat_ref[(2 * m + 1) * pairs:(2 * m + 2) * pairs, :] = second
        per_group = S5_GROUP // 2
        for g in range(pairs // per_group):
            slab = jnp.stack([flat_ref[c * pairs + g * per_group:c * pairs + (g + 1) * per_group, :]
                              for c in range(rows)], axis=0)
            ut_ref[g] = slab.reshape(rows, per_group * LANES)

    lb = lb_ref[...]
    gn = gn_ref[...]
    ltri = ltri_ref[...]
    row = lax.broadcasted_iota(I32, (CHUNK, CHUNK), 0)
    col = lax.broadcasted_iota(I32, (CHUNK, CHUNK), 1)
    causal = row >= col

    def step(proj_ref, next_ref):
        normalise()
        for ci in range(rows):
            project_slab(next_ref, ci)
            r0 = ci * CHUNK
            q = proj_ref[r0:r0 + CHUNK, 0:fdim]
            fl = proj_ref[r0:r0 + CHUNK, fdim:2 * fdim]
            iv = proj_ref[r0:r0 + CHUNK, 2 * fdim:3 * fdim]
            og = proj_ref[r0:r0 + CHUNK, 3 * fdim:4 * fdim]
            f = lb + (1.0 - lb) * _sigmoid(fl)
            lf_hi, lf_mid, lf_lo = _split3(jnp.log(f))
            b = (jnp.dot(ltri, lf_hi, preferred_element_type=F32)
                 + jnp.dot(ltri, lf_mid, preferred_element_type=F32)
                 + jnp.dot(ltri, lf_lo, preferred_element_type=F32))
            b_ref = b[CHUNK // 2 - 1:CHUNK // 2, :]
            b_last = b[CHUNK - 1:CHUNK, :]
            qs = _silu(q)
            kk = 1.0 - f
            qe = (qs * jnp.exp(b - b_ref)).astype(BF16)
            ke = (kk * jnp.exp(b_ref - b)).astype(BF16)
            qb = (qs * jnp.exp(b)).astype(BF16)
            k2 = (kk * jnp.exp(b_last - b)).astype(BF16)
            dec = jnp.exp(b_last)
            ivb = iv.astype(BF16)
            outs = []
            for hh in range(HGRN_HEADS):
                sl = slice(hh * HGRN_KDIM, (hh + 1) * HGRN_KDIM)
                att = lax.dot_general(qe[:, sl], ke[:, sl], _NT, preferred_element_type=F32)
                att = jnp.where(causal, att, 0.0)
                st = st_ref[hh]
                o = jnp.dot(att.astype(BF16), ivb[:, sl], preferred_element_type=F32)
                o = o + lax.dot_general(qb[:, sl], st.astype(BF16), _NT, preferred_element_type=F32)
                st_ref[hh] = st * dec[:, sl] + lax.dot_general(
                    ivb[:, sl], k2[:, sl], _TN, preferred_element_type=F32)
                outs.append(o * lax.rsqrt(jnp.mean(o * o, axis=-1, keepdims=True) + EPS))
            o = jnp.concatenate(outs, axis=1) * gn * _silu(og)
            oa_ref[0, r0:r0 + CHUNK, :] = o.astype(BF16)
        s5_input()

    @pl.when(lax.rem(i, 2) == 0)
    def _():
        step(proj_b, proj_a)

    @pl.when(lax.rem(i, 2) == 1)
    def _():
        step(proj_a, proj_b)


def _mix_front_call(x, mod, w_main, w_ut, lb, gn):
    bsz, seq, d = x.shape
    fdim = HGRN_HEADS * HGRN_KDIM
    ncols = w_main.shape[1]
    nb = w_ut.shape[0]
    groups = nb // S5_GROUP
    ltri = jnp.tril(jnp.ones((CHUNK, CHUNK), BF16))
    ts = SEQ_TILE
    tiles = seq // ts
    n_tiles = bsz * tiles
    rows = ts // CHUNK

    def cur(i):
        return jnp.minimum(i, n_tiles - 1)

    def prev(i):
        return jnp.maximum(i - 1, 0)

    return pl.pallas_call(
        functools.partial(_mix_front_kernel, tiles_per_seq=tiles),
        out_shape=(jax.ShapeDtypeStruct((groups, bsz * seq // CHUNK, S5_GROUP * CHUNK), F32),
                   jax.ShapeDtypeStruct((bsz, seq, fdim), BF16),
                   jax.ShapeDtypeStruct((nb, bsz * seq), BF16)),
        grid=(n_tiles + 1,),
        in_specs=[pl.BlockSpec((1, ts, d), lambda i: (cur(i) // tiles, cur(i) % tiles, 0)),
                  pl.BlockSpec((1, 6, d), lambda i: (cur(i) // tiles, 0, 0)),
                  pl.BlockSpec((d, ncols), lambda i: (0, 0)),
                  pl.BlockSpec((nb, d), lambda i: (0, 0)),
                  pl.BlockSpec((1, fdim), lambda i: (0, 0)),
                  pl.BlockSpec((1, fdim), lambda i: (0, 0)),
                  pl.BlockSpec((CHUNK, CHUNK), lambda i: (0, 0))],
        out_specs=(pl.BlockSpec((groups, rows, S5_GROUP * CHUNK), lambda i: (0, cur(i), 0)),
                   pl.BlockSpec((1, ts, fdim), lambda i: (prev(i) // tiles, prev(i) % tiles, 0)),
                   pl.BlockSpec((nb, ts), lambda i: (0, cur(i)))),
        scratch_shapes=[pltpu.VMEM((ts, ncols), F32), pltpu.VMEM((ts, ncols), F32),
                        pltpu.VMEM((HGRN_HEADS, fdim // HGRN_HEADS, HGRN_KDIM), F32),
                        pltpu.VMEM((rows * nb // 2, LANES), F32),
                        pltpu.VMEM((ts, d), BF16)],
        compiler_params=pltpu.CompilerParams(
            dimension_semantics=("arbitrary",), vmem_limit_bytes=VMEM_LIMIT),
        name="mix_front",
    )(x, mod, w_main, w_ut, lb, gn, ltri)


def _s5_tables(lam_re, lam_im, log_dt, b_re, b_im, c_re, c_im):
    t = CHUNK
    lam = lax.complex(jnp.minimum(lam_re, -1e-4), lam_im)
    lam_dt = lam * jnp.exp(log_dt)[:, None]
    lam_bar = jnp.exp(lam_dt)
    b_bar = ((lam_bar - 1.0) / lam)[..., None] * lax.complex(b_re, b_im)
    c_mat = lax.complex(c_re, c_im)
    taus = jnp.arange(t + 1, dtype=F32)
    lam_pow = jnp.exp(lam_dt[:, None, :] * taus[None, :, None])
    g, p = lam.shape
    c = b_re.shape[-1]
    cb = c_mat[:, None, :, :] * b_bar.transpose(0, 2, 1)[:, :, None, :]
    cb = jnp.concatenate([cb.real, -cb.imag], axis=-1)
    cb = cb.reshape(g, c, c // 2, 2 * 2 * p).reshape(g, c * c // 2, 4 * p)
    lp = jnp.concatenate([lam_pow[:, :t].real, lam_pow[:, :t].imag], axis=-1).transpose(0, 2, 1)
    zero = jnp.zeros_like(lp)
    lp2 = jnp.concatenate([jnp.concatenate([lp, zero], axis=2),
                           jnp.concatenate([zero, lp], axis=2)], axis=1)
    pc = lam_pow[:, t - 1::-1][:, :t, :, None] * b_bar[:, None, :, :]
    pc = pc.transpose(0, 3, 1, 2).reshape(g, c * t, p)
    p_tab = jnp.concatenate([pc.real, pc.imag], axis=-1)
    ql = c_mat[:, None, :, :] * lam_pow[:, 1:t + 1, None, :]
    ql = ql.transpose(0, 3, 2, 1).reshape(g, p, c * t)
    q_tab = jnp.concatenate([ql.real, -ql.imag], axis=1)
    lam_t = lam_pow[:, t]
    a1 = jnp.concatenate([lam_t.real, lam_t.real], axis=-1)[:, None, :]
    a2 = jnp.concatenate([-lam_t.imag, lam_t.imag], axis=-1)[:, None, :]
    return cb, lp2, p_tab.astype(BF16), q_tab.astype(BF16), a1, a2


def _split3(v):
    hi = v.astype(BF16)
    rem = v - hi.astype(F32)
    mid = rem.astype(BF16)
    return hi, mid, (rem - mid.astype(F32)).astype(BF16)


S5_RING = 3


def _s5_kernel(u_hbm, cb_ref, lp_ref, p_ref, q_ref, a1_ref, a2_ref, y_ref, v_ref, xs_ref, m_ref, k_ref,
               ubuf_ref, u_sems, *, n_chunks, n_batch):
    g = pl.program_id(0)
    n_groups = pl.num_programs(0)

    def fetch(group, slot):
        return pltpu.make_async_copy(u_hbm.at[group], ubuf_ref.at[slot], u_sems.at[slot])

    @pl.when(g == 0)
    def _():
        for ahead in range(S5_RING - 1):
            fetch(ahead, ahead).start()

    @pl.when(g + S5_RING - 1 < n_groups)
    def _():
        fetch(g + S5_RING - 1, lax.rem(g + S5_RING - 1, S5_RING)).start()

    slot = lax.rem(g, S5_RING)
    fetch(g, slot).wait()
    u_ref = ubuf_ref.at[slot]

    c_hi, c_mid, c_lo = _split3(cb_ref[0])
    l_hi, l_mid, l_lo = _split3(lp_ref[0])
    k_ref[...] = (jnp.dot(c_hi, l_hi, preferred_element_type=F32)
                  + jnp.dot(c_hi, l_mid, preferred_element_type=F32)
                  + jnp.dot(c_mid, l_hi, preferred_element_type=F32)
                  + jnp.dot(c_hi, l_lo, preferred_element_type=F32)
                  + jnp.dot(c_mid, l_mid, preferred_element_type=F32)
                  + jnp.dot(c_lo, l_hi, preferred_element_type=F32))
    n_pairs = m_ref.shape[1] // LANES
    n_in = k_ref.shape[0] // n_pairs
    lane = lax.broadcasted_iota(I32, (CHUNK, LANES), 1)
    causal = (lane & (CHUNK - 1)) >= lax.broadcasted_iota(I32, (CHUNK, LANES), 0)
    for ci in range(n_in):
        for a in range(n_pairs):
            lags = jnp.broadcast_to(k_ref[ci * n_pairs + a:ci * n_pairs + a + 1, :], (CHUNK, LANES))
            tile = pltpu.roll(lags, 0, 1, stride=1, stride_axis=0)
            m_ref[ci * CHUNK:(ci + 1) * CHUNK, a * LANES:(a + 1) * LANES] = jnp.where(
                causal, tile, 0.0).astype(BF16)
    u = u_ref[...].astype(BF16)
    v_ref[...] = jnp.dot(u, p_ref[0], preferred_element_type=F32)
    a1 = a1_ref[0]
    a2 = a2_ref[0]
    half = xs_ref.shape[1] // 2

    def step(n, state):
        xs_ref[pl.ds(n, n_batch, stride=n_chunks), :] = state
        return (a1 * state + a2 * pltpu.roll(state, half, 1)
                + v_ref[pl.ds(n, n_batch, stride=n_chunks), :])

    lax.fori_loop(0, n_chunks, step, jnp.zeros((n_batch, xs_ref.shape[1]), F32))
    y = jnp.dot(u, m_ref[...], preferred_element_type=F32)
    y_ref[0] = y + jnp.dot(xs_ref[...].astype(BF16), q_ref[0], preferred_element_type=F32)


def _s5_call(ut, cb, lp2, p_tab, q_tab, a1, a2, n_chunks, n_batch):
    g, rows, width = ut.shape
    p2 = p_tab.shape[-1]
    return pl.pallas_call(
        functools.partial(_s5_kernel, n_chunks=n_chunks, n_batch=n_batch),
        out_shape=jax.ShapeDtypeStruct((g, rows, width), F32),
        grid=(g,),
        in_specs=[pl.BlockSpec(memory_space=pl.ANY),
                  pl.BlockSpec((1,) + cb.shape[1:], lambda i: (i, 0, 0)),
                  pl.BlockSpec((1,) + lp2.shape[1:], lambda i: (i, 0, 0)),
                  pl.BlockSpec((1, width, p2), lambda i: (i, 0, 0)),
                  pl.BlockSpec((1, p2, width), lambda i: (i, 0, 0)),
                  pl.BlockSpec((1, 1, p2), lambda i: (i, 0, 0)),
                  pl.BlockSpec((1, 1, p2), lambda i: (i, 0, 0))],
        out_specs=pl.BlockSpec((1, rows, width), lambda i: (i, 0, 0)),
        scratch_shapes=[pltpu.VMEM((rows, p2), F32), pltpu.VMEM((rows, p2), F32),
                        pltpu.VMEM((width, width), BF16),
                        pltpu.VMEM((cb.shape[1], lp2.shape[2]), F32),
                        pltpu.VMEM((S5_RING, rows, width), F32), pltpu.SemaphoreType.DMA((S5_RING,))],
        compiler_params=pltpu.CompilerParams(
            dimension_semantics=("arbitrary",), vmem_limit_bytes=VMEM_LIMIT),
        name="s5",
    )(ut, cb, lp2, p_tab, q_tab, a1, a2)


def _token_major(flat_ref, src_ref):
    groups, rows, _ = src_ref.shape
    per_group = S5_GROUP // 2
    pairs = groups * per_group
    for g in range(groups):
        slab = src_ref[g].reshape(rows, per_group, LANES)
        for c in range(rows):
            flat_ref[c * pairs + g * per_group:c * pairs + (g + 1) * per_group, :] = slab[c]
    tiles = []
    for m in range(rows // 2):
        even, odd = _split_chunk_pairs(flat_ref[2 * m * pairs:(2 * m + 1) * pairs, :],
                                       flat_ref[(2 * m + 1) * pairs:(2 * m + 2) * pairs, :])
        tiles.append(jnp.concatenate([even, odd], axis=0))
    return jnp.concatenate(tiles, axis=1)


def _mix_back_kernel(x_ref, oa_ref, yt_ref, ucm_ref, mod_ref, dskip_ref, wglu_ref, bglu_ref,
                     wout_ref, wrh_ref, wrl_ref,
                     x1_ref, hp_ref, lt_ref, flat_ref):
    na = oa_ref.shape[1]
    y_t = _token_major(flat_ref, yt_ref)
    u_t = ucm_ref[...].astype(F32)
    z_t = jax.nn.gelu(y_t + dskip_ref[...] * u_t)
    gate_t = _sigmoid(jnp.dot(wglu_ref[...], z_t.astype(BF16), preferred_element_type=F32)
                      + bglu_ref[...])
    ob_t = (z_t * gate_t).astype(BF16)
    mixed = (jnp.dot(oa_ref[...], wout_ref[0:na, :], preferred_element_type=F32)
             + lax.dot_general(ob_t, wout_ref[na:, :], _TN, preferred_element_type=F32))
    x1 = x_ref[...] + mod_ref[0, 2:3, :] * mixed
    ms = jnp.mean(x1 * x1, axis=-1, keepdims=True)
    h2 = x1 * lax.rsqrt(ms + EPS) * (1.0 + mod_ref[0, 4:5, :]) + mod_ref[0, 3:4, :]
    _store_packed_planes(hp_ref, h2)
    h_hi = h2.astype(BF16)
    h_lo = (h2 - h_hi.astype(F32)).astype(BF16)
    lt = lax.dot_general(wrh_ref[...], h_hi, _NT, preferred_element_type=F32)
    lt = lt + lax.dot_general(wrl_ref[...], h_hi, _NT, preferred_element_type=F32)
    lt = lt + lax.dot_general(wrh_ref[...], h_lo, _NT, preferred_element_type=F32)
    lt_ref[...] = lt
    x1_ref[...] = x1


def _mix_back_call(x2d, oa, yt, ucm, mod, dskip, wglu, bglu, wout, wrh, wrl, seq):
    n, d = x2d.shape
    nb = oa.shape[1]
    ne = wrh.shape[0]
    tm = TOK_TILE
    per_b = seq // tm
    groups, _, width = yt.shape
    rows = tm // CHUNK
    const = lambda i: (0, 0)
    flat_block = pl.BlockSpec((groups, rows, width), lambda i: (0, i, 0))
    return pl.pallas_call(
        _mix_back_kernel,
        out_shape=(jax.ShapeDtypeStruct((n, d), F32),
                   jax.ShapeDtypeStruct((PACK_ROWS, n, LANES), I32),
                   jax.ShapeDtypeStruct((ne, n), F32)),
        grid=(n // tm,),
        in_specs=[pl.BlockSpec((tm, d), lambda i: (i, 0)),
                  pl.BlockSpec((tm, nb), lambda i: (i, 0)),
                  flat_block,
                  pl.BlockSpec((nb, tm), lambda i: (0, i)),
                  pl.BlockSpec((1, 6, d), lambda i: (i // per_b, 0, 0)),
                  pl.BlockSpec((nb, 1), const),
                  pl.BlockSpec((nb, nb), const),
                  pl.BlockSpec((nb, 1), const),
                  pl.BlockSpec((d, d), const),
                  pl.BlockSpec((ne, d), const),
                  pl.BlockSpec((ne, d), const)],
        out_specs=(pl.BlockSpec((tm, d), lambda i: (i, 0)),
                   pl.BlockSpec((PACK_ROWS, tm, LANES), lambda i: (0, i, 0)),
                   pl.BlockSpec((ne, tm), lambda i: (0, i))),
        scratch_shapes=[pltpu.VMEM((rows * nb // 2, LANES), F32)],
        compiler_params=pltpu.CompilerParams(
            dimension_semantics=("arbitrary",), vmem_limit_bytes=VMEM_LIMIT),
        name="mix_back",
    )(x2d, oa, yt, ucm, mod, dskip, wglu, bglu, wout, wrh, wrl)


def _route_kernel(lt_ref, bias_ref, su_ref, idx_ref, w_ref, rank_ref, cnt_ref, run_ref):
    ne, tr = lt_ref.shape
    per_group = ne // N_EXPERT_GROUPS
    neg = -jnp.inf

    @pl.when(pl.program_id(0) == 0)
    def _():
        run_ref[...] = jnp.zeros_like(run_ref)

    s = _sigmoid(lt_ref[...])
    sel = s + bias_ref[...]
    gio = lax.broadcasted_iota(I32, (per_group, tr), 0)
    gscore = []
    for g in range(N_EXPERT_GROUPS):
        v = sel[g * per_group:(g + 1) * per_group, :]
        m1 = jnp.max(v, axis=0, keepdims=True)
        i1 = jnp.min(jnp.where(v == m1, gio, per_group), axis=0, keepdims=True)
        m2 = jnp.max(jnp.where(gio == i1, neg, v), axis=0, keepdims=True)
        gscore.append(m1 + m2)
    masked = []
    for g in range(N_EXPERT_GROUPS):
        ahead = jnp.zeros((1, tr), I32)
        for o in range(N_EXPERT_GROUPS):
            if o == g:
                continue
            wins = (gscore[o] >= gscore[g]) if o < g else (gscore[o] > gscore[g])
            ahead = ahead + wins.astype(I32)
        keep = ahead < TOPK_GROUPS
        masked.append(jnp.where(keep, sel[g * per_group:(g + 1) * per_group, :], neg))
    selm = jnp.concatenate(masked, axis=0)
    eio = lax.broadcasted_iota(I32, (ne, tr), 0)
    candidate = selm > neg
    idxs, ws = [], []
    for k in range(TOP_K):
        m = jnp.max(selm, axis=0, keepdims=True)
        ik = jnp.min(jnp.where(selm == m, eio, ne), axis=0, keepdims=True)
        onehot = eio == ik
        ws.append(jnp.sum(jnp.where(onehot, s, 0.0), axis=0, keepdims=True))
        selm = jnp.where(onehot, neg, selm)
        idxs.append(ik)
    hits = jnp.where(jnp.logical_and(candidate, selm == neg), 1.0, 0.0)
    wsum = ws[0]
    for k in range(1, TOP_K):
        wsum = wsum + ws[k]
    scale = ROUTE_SCALE / wsum
    ranks = jnp.dot(hits.astype(BF16), su_ref[...], preferred_element_type=F32) + run_ref[...]
    for k in range(TOP_K):
        idx_ref[k:k + 1, :] = idxs[k]
        w_ref[k:k + 1, :] = ws[k] * scale
        rk = jnp.sum(jnp.where(eio == idxs[k], ranks, 0.0), axis=0, keepdims=True)
        rank_ref[k:k + 1, :] = rk.astype(I32)
    run_ref[...] = run_ref[...] + jnp.sum(hits, axis=1, keepdims=True)
    cnt_ref[...] = run_ref[...]


def _route_call(lt, bias):
    ne, n = lt.shape
    tr = ROUTE_TILE
    su = jnp.triu(jnp.ones((tr, tr), F32), k=1).astype(BF16)
    return pl.pallas_call(
        _route_kernel,
        out_shape=(jax.ShapeDtypeStruct((TOP_K, n), I32),
                   jax.ShapeDtypeStruct((TOP_K, n), F32),
                   jax.ShapeDtypeStruct((TOP_K, n), I32),
                   jax.ShapeDtypeStruct((ne, 1), F32)),
        grid=(n // tr,),
        in_specs=[pl.BlockSpec((ne, tr), lambda i: (0, i)),
                  pl.BlockSpec((ne, 1), lambda i: (0, 0)),
                  pl.BlockSpec((tr, tr), lambda i: (0, 0))],
        out_specs=(pl.BlockSpec((TOP_K, tr), lambda i: (0, i)),
                   pl.BlockSpec((TOP_K, tr), lambda i: (0, i)),
                   pl.BlockSpec((TOP_K, tr), lambda i: (0, i)),
                   pl.BlockSpec((ne, 1), lambda i: (0, 0))),
        scratch_shapes=[pltpu.VMEM((ne, 1), F32)],
        compiler_params=pltpu.CompilerParams(
            dimension_semantics=("arbitrary",), vmem_limit_bytes=VMEM_LIMIT),
        name="route",
    )(lt, bias.reshape(ne, 1), su)


def _sc_mesh():
    return plsc.VectorSubcoreMesh(core_axis_name="c", subcore_axis_name="s",
                                  num_cores=SC_CORES, num_subcores=SC_SUBCORES)


def _sc_worker():
    return lax.axis_index("s") * SC_CORES + lax.axis_index("c")


def _dispatch_call(idx_win, rank_win, starts, hp):
    n = hp.shape[1]
    n_workers = SC_CORES * SC_SUBCORES
    wins_per_worker = n // SC_WINDOW // n_workers
    lanes = SC_LANES

    def body(hp_hbm, idx_hbm, rank_hbm, st_hbm, xs_hbm, pos_hbm,
             idx_a, idx_b, rank_a, rank_b, pos_a, pos_b, rows_a, rows_b, st_v,
             load_a, load_b, scatter_sem, pos_sem):
        first_win = _sc_worker() * wins_per_worker
        idx_v, rank_v, pos_v = (idx_a, idx_b), (rank_a, rank_b), (pos_a, pos_b)
        rows_v, load_sem = (rows_a, rows_b), (load_a, load_b)
        pltpu.sync_copy(st_hbm, st_v)

        def loads(win, slot):
            pieces = tuple(
                pltpu.make_async_copy(hp_hbm.at[j, pl.ds(win * SC_WINDOW, SC_WINDOW)],
                                      rows_v[slot].at[:, j], load_sem[slot])
                for j in range(PACK_ROWS))
            return pieces + (pltpu.make_async_copy(idx_hbm.at[win], idx_v[slot], load_sem[slot]),
                             pltpu.make_async_copy(rank_hbm.at[win], rank_v[slot], load_sem[slot]))

        for cp in loads(first_win, 0):
            cp.start()

        @pl.loop(0, wins_per_worker, step=2)
        def _(w):
            for slot in range(2):
                win = first_win + w + slot
                for cp in loads(win, slot):
                    cp.wait()

                @pl.when(w + slot + 1 < wins_per_worker)
                def _():
                    for cp in loads(win + 1, 1 - slot):
                        cp.start()

                for k in range(TOP_K):
                    for c in range(SC_WINDOW // lanes):
                        seg = pl.ds(c * lanes, lanes)
                        base = plsc.load_gather(st_v, [idx_v[slot][k, seg]])
                        pos_v[slot][k, seg] = base + rank_v[slot][k, seg]
                pos_out = pltpu.async_copy(pos_v[slot], pos_hbm.at[win], pos_sem)
                copies = [pltpu.async_copy(rows_v[slot], xs_hbm.at[pos_v[slot].at[k]], scatter_sem)
                          for k in range(TOP_K)]
                for cp in copies:
                    cp.wait()
                pos_out.wait()

    idx_buf = pltpu.VMEM((TOP_K, SC_WINDOW), I32)
    row_buf = pltpu.VMEM((SC_WINDOW, PACK_ROWS, LANES), hp.dtype)
    return pl.kernel(
        body,
        out_type=(jax.ShapeDtypeStruct((n * TOP_K, PACK_ROWS, LANES), hp.dtype),
                  jax.ShapeDtypeStruct(idx_win.shape, I32)),
        mesh=_sc_mesh(),
        scratch_types=[idx_buf, idx_buf, idx_buf, idx_buf, idx_buf, idx_buf, row_buf, row_buf,
                       pltpu.VMEM(starts.shape, I32),
                       pltpu.SemaphoreType.DMA, pltpu.SemaphoreType.DMA, pltpu.SemaphoreType.DMA,
                       pltpu.SemaphoreType.DMA],
        compiler_params=pltpu.CompilerParams(needs_layout_passes=False),
        name="dispatch",
    )(hp, idx_win, rank_win, starts)


def _collect_call(pos_win, y_sorted):
    n = pos_win.shape[0] * SC_WINDOW
    n_workers = SC_CORES * SC_SUBCORES
    wins_per_worker = n // SC_WINDOW // n_workers

    def body(ys_hbm, pos_hbm, out_hbm, idx_a, idx_b, rows_a, rows_b, idx_sem_a, idx_sem_b,
             gather_sem, write_sem):
        first_win = _sc_worker() * wins_per_worker
        bufs = (rows_a, rows_b)
        idx_v, idx_sem = (idx_a, idx_b), (idx_sem_a, idx_sem_b)

        def idx_load(win, slot):
            return pltpu.make_async_copy(pos_hbm.at[win], idx_v[slot], idx_sem[slot])

        idx_load(first_win, 0).start()

        @pl.loop(0, wins_per_worker, step=2)
        def _(w):
            for slot in range(2):
                win = first_win + w + slot
                idx_load(win, slot).wait()

                @pl.when(w + slot + 1 < wins_per_worker)
                def _():
                    idx_load(win + 1, 1 - slot).start()

                def gather(k):
                    return pltpu.async_copy(ys_hbm.at[idx_v[slot].at[k]], bufs[k % 2], gather_sem)

                pending_gather = gather(0)
                pending_write = None
                for k in range(TOP_K):
                    pending_gather.wait()
                    if pending_write is not None:
                        for cp in pending_write:
                            cp.wait()
                    if k + 1 < TOP_K:
                        pending_gather = gather(k + 1)
                    pending_write = [
                        pltpu.async_copy(bufs[k % 2].at[:, j],
                                         out_hbm.at[j, k, pl.ds(win * SC_WINDOW, SC_WINDOW)], write_sem)
                        for j in range(PACK_ROWS)]
                for cp in pending_write:
                    cp.wait()

    idx_buf = pltpu.VMEM((TOP_K, SC_WINDOW), I32)
    row_buf = pltpu.VMEM((SC_WINDOW,) + y_sorted.shape[1:], y_sorted.dtype)
    return pl.kernel(
        body,
        out_type=jax.ShapeDtypeStruct((PACK_ROWS, TOP_K, n, LANES), y_sorted.dtype),
        mesh=_sc_mesh(),
        scratch_types=[idx_buf, idx_buf, row_buf, row_buf,
                       pltpu.SemaphoreType.DMA, pltpu.SemaphoreType.DMA,
                       pltpu.SemaphoreType.DMA, pltpu.SemaphoreType.DMA],
        name="collect",
    )(y_sorted, pos_win)


def _experts_kernel(pe_ref, pb_ref, pv_ref, st_ref, en_ref, nx_ref,
                    xs_hbm, wg_hbm, wu_hbm, wd_hbm, y_hbm,
                    wgb_ref, wub_ref, wdb_ref, wgf_ref, wuf_ref, wdf_ref, xbuf_ref, stage_ref,
                    in_sems, out_sems, w_sems, state_ref):
    p = pl.program_id(0)
    n_pairs = pl.num_programs(0)
    e = pe_ref[p]
    blk = pb_ref[p]
    rb = xbuf_ref.shape[2]
    prev = jnp.maximum(p - 1, 0)
    nxt = jnp.minimum(p + 1, n_pairs - 1)
    live = pv_ref[p] == 1
    first = jnp.logical_or(p == 0, pb_ref[prev] != blk)
    block_ends = jnp.logical_or(pb_ref[nxt] != blk, pv_ref[nxt] == 0)
    last = jnp.logical_or(p == n_pairs - 1, block_ends)

    def in_copy(j, block, slot):
        return pltpu.make_async_copy(xs_hbm.at[pl.ds(block * rb, rb), j], xbuf_ref.at[slot, j],
                                     in_sems.at[slot])

    def out_copy(j, block, slot):
        return pltpu.make_async_copy(stage_ref.at[slot, j], y_hbm.at[pl.ds(block * rb, rb), j],
                                     out_sems.at[slot])

    def drain(slot):
        @pl.when(state_ref[1 + slot] == 1)
        def _():
            for j in range(PACK_ROWS):
                out_copy(j, 0, slot).wait()
            state_ref[1 + slot] = 0

    @pl.when(p == 0)
    def _():
        state_ref[0] = 0
        state_ref[1] = 0
        state_ref[2] = 0
        stage_ref[...] = jnp.zeros_like(stage_ref)
        for j in range(PACK_ROWS):
            in_copy(j, blk, 0).start()

    slot = jnp.where(jnp.logical_and(first, p > 0), 1 - state_ref[0], state_ref[0])
    state_ref[0] = slot

    @pl.when(jnp.logical_and(first, blk + 1 < y_hbm.shape[0] // rb))
    def _():
        for j in range(PACK_ROWS):
            in_copy(j, blk + 1, 1 - slot).start()

    @pl.when(first)
    def _():
        for j in range(PACK_ROWS):
            in_copy(j, blk, slot).wait()
        drain(slot)

    def w_copies(expert, wslot):
        return (pltpu.make_async_copy(wg_hbm.at[expert], wgf_ref.at[wslot], w_sems.at[wslot]),
                pltpu.make_async_copy(wu_hbm.at[expert], wuf_ref.at[wslot], w_sems.at[wslot]),
                pltpu.make_async_copy(wd_hbm.at[expert], wdf_ref.at[wslot], w_sems.at[wslot]))

    @pl.when(p == 0)
    def _():
        state_ref[3] = 0
        for cp in w_copies(e, 0):
            cp.start()

    new_expert = jnp.logical_or(p == 0, pe_ref[prev] != e)
    wslot = jnp.where(jnp.logical_and(new_expert, p > 0), 1 - state_ref[3], state_ref[3])
    state_ref[3] = wslot

    @pl.when(new_expert)
    def _():
        for cp in w_copies(e, wslot):
            cp.wait()
        wgb_ref[...] = wgf_ref[wslot].astype(BF16)
        wub_ref[...] = wuf_ref[wslot].astype(BF16)
        wdb_ref[...] = wdf_ref[wslot].astype(BF16)

        @pl.when(nx_ref[e] != e)
        def _():
            for cp in w_copies(nx_ref[e], 1 - wslot):
                cp.start()

    lo_row = st_ref[e]
    hi_row = en_ref[e]

    def sub_block(s, row0, shared):
        sub = pl.ds(s * SUB_BLOCK, SUB_BLOCK)
        pieces = []
        for j in range(PACK_ROWS):
            w = xbuf_ref[slot, j, sub, :]
            pieces.append(lax.bitcast_convert_type(w.astype(jnp.int16), BF16))
            pieces.append(lax.bitcast_convert_type(
                lax.shift_right_logical(w, 16).astype(jnp.int16), BF16))
        xb = jnp.concatenate(pieces, axis=1)
        gate = jnp.dot(xb, wgb_ref[...], preferred_element_type=F32)
        up = jnp.dot(xb, wub_ref[...], preferred_element_type=F32)
        yb = jnp.dot((_silu(gate) * up).astype(BF16), wdb_ref[...], preferred_element_type=F32)
        if shared:
            rows = row0 + lax.broadcasted_iota(I32, (SUB_BLOCK, 1), 0)
            mine = jnp.logical_and(rows >= lo_row, rows < hi_row)
        for j in range(PACK_ROWS):
            word = pltpu.pack_elementwise(
                [yb[:, 2 * j * LANES:(2 * j + 1) * LANES], yb[:, (2 * j + 1) * LANES:(2 * j + 2) * LANES]],
                packed_dtype=BF16)
            if shared:
                word = jnp.where(mine, word, stage_ref[slot, j, sub, :])
            stage_ref[slot, j, sub, :] = word

    def one(s):
        row0 = blk * rb + s * SUB_BLOCK
        touched = jnp.logical_and(live, jnp.logical_and(row0 < hi_row, row0 + SUB_BLOCK > lo_row))
        whole = jnp.logical_and(lo_row <= row0, hi_row >= row0 + SUB_BLOCK)

        @pl.when(jnp.logical_and(touched, whole))
        def _():
            sub_block(s, row0, shared=False)

        @pl.when(jnp.logical_and(touched, jnp.logical_not(whole)))
        def _():
            sub_block(s, row0, shared=True)

    for s in range(0, rb // SUB_BLOCK, 2):
        row0 = blk * rb + s * SUB_BLOCK
        both = jnp.logical_and(live, jnp.logical_and(lo_row <= row0, hi_row >= row0 + 2 * SUB_BLOCK))

        @pl.when(both)
        def _():
            sub_block(s, row0, shared=False)
            sub_block(s + 1, row0 + SUB_BLOCK, shared=False)

        @pl.when(jnp.logical_not(both))
        def _():
            one(s)
            one(s + 1)

    @pl.when(jnp.logical_and(live, last))
    def _():
        for j in range(PACK_ROWS):
            out_copy(j, blk, slot).start()
        state_ref[1 + slot] = 1

    @pl.when(p == n_pairs - 1)
    def _():
        drain(0)
        drain(1)


def _experts_call(pair_e, pair_blk, pair_ok, starts, ends, next_e, xs, w_gate, w_up, w_down):
    ne, d, de = w_gate.shape
    rb = ROW_BLOCK
    n_pairs = pair_e.shape[0]
    anywhere = pl.BlockSpec(memory_space=pl.ANY)
    grid_spec = pltpu.PrefetchScalarGridSpec(
        num_scalar_prefetch=6,
        grid=(n_pairs,),
        in_specs=[anywhere, anywhere, anywhere, anywhere],
        out_specs=anywhere,
        scratch_shapes=[pltpu.VMEM((d, de), BF16), pltpu.VMEM((d, de), BF16),
                        pltpu.VMEM((de, d), BF16),
                        pltpu.VMEM((2, d, de), F32), pltpu.VMEM((2, d, de), F32),
                        pltpu.VMEM((2, de, d), F32),
                        pltpu.VMEM((2, PACK_ROWS, rb, LANES), I32),
                        pltpu.VMEM((2, PACK_ROWS, rb, LANES), I32),
                        pltpu.SemaphoreType.DMA((2,)), pltpu.SemaphoreType.DMA((2,)),
                        pltpu.SemaphoreType.DMA((2,)),
                        pltpu.SMEM((4,), I32)],
    )
    return pl.pallas_call(
        _experts_kernel,
        out_shape=jax.ShapeDtypeStruct(xs.shape, xs.dtype),
        grid_spec=grid_spec,
        compiler_params=pltpu.CompilerParams(
            dimension_semantics=("arbitrary",), vmem_limit_bytes=VMEM_LIMIT),
        name="experts",
    )(pair_e, pair_blk, pair_ok, starts, ends, next_e, xs, w_gate, w_up, w_down)


def _combine_kernel(y0_ref, y1_ref, y2_ref, y3_ref, x1_ref, hp_ref, w_ref, g2_ref, fg_ref,
                    wsg_ref, wsu_ref, wsd_ref, o_ref):
    tm = x1_ref.shape[0]
    w = w_ref[...].T
    parts = []
    for y_ref in (y0_ref, y1_ref, y2_ref, y3_ref):
        for half in range(2):
            acc = None
            for k in range(TOP_K):
                piece = pltpu.unpack_elementwise(y_ref[k], index=half, packed_dtype=BF16,
                                                 unpacked_dtype=F32) * w[:, k:k + 1]
                acc = piece if acc is None else acc + piece
            parts.append(acc)
    routed = jnp.concatenate(parts, axis=1)
    pieces = []
    for j in range(PACK_ROWS):
        word = hp_ref[j]
        pieces.append(lax.bitcast_convert_type(word.astype(jnp.int16), BF16))
        pieces.append(lax.bitcast_convert_type(lax.shift_right_logical(word, 16).astype(jnp.int16), BF16))
    h2 = jnp.concatenate(pieces, axis=1)
    hid = _silu(jnp.dot(h2, wsg_ref[...], preferred_element_type=F32)) * jnp.dot(
        h2, wsu_ref[...], preferred_element_type=F32)
    shared = jnp.dot(hid.astype(BF16), wsd_ref[...], preferred_element_type=F32)
    x2 = x1_ref[...] + g2_ref[0] * (routed + shared)
    ms = jnp.mean(x2 * x2, axis=-1, keepdims=True)
    o_ref[...] = x2 * lax.rsqrt(ms + EPS) * fg_ref[...]


def _combine_kernel_into(*refs):
    _combine_kernel(*refs[:-2], refs[-1])


def _combine_call(y_tok, x1, hp, w_tok, g2, fgain, wsg, wsu, wsd, seq, part, prev_out):
    n, d = x1.shape
    dsh = wsg.shape[1]
    tm = MOVE_TILE
    per_b = seq // tm
    tiles = y_tok.shape[2] // tm
    t0 = part * tiles
    const = lambda i: (0, 0)

    def piece_spec(j):
        return pl.BlockSpec((None, TOP_K, tm, LANES), lambda i: (j, 0, i, 0))

    in_specs = [piece_spec(j) for j in range(PACK_ROWS)] + [
        pl.BlockSpec((tm, d), lambda i: (t0 + i, 0)),
        pl.BlockSpec((PACK_ROWS, tm, LANES), lambda i: (0, t0 + i, 0)),
        pl.BlockSpec((TOP_K, tm), lambda i: (0, t0 + i)),
        pl.BlockSpec((1, 1, d), lambda i: ((t0 + i) // per_b, 0, 0)),
        pl.BlockSpec((1, d), const),
        pl.BlockSpec((d, dsh), const),
        pl.BlockSpec((d, dsh), const),
        pl.BlockSpec((dsh, d), const)]
    args = [y_tok, y_tok, y_tok, y_tok, x1, hp, w_tok, g2, fgain, wsg, wsu, wsd]
    aliases = {}
    body = _combine_kernel
    if prev_out is not None:
        in_specs.append(pl.BlockSpec(memory_space=pl.ANY))
        args.append(prev_out)
        aliases = {len(args) - 1: 0}
        body = _combine_kernel_into
    return pl.pallas_call(
        body,
        out_shape=jax.ShapeDtypeStruct((n, d), F32),
        grid=(tiles,),
        in_specs=in_specs,
        out_specs=pl.BlockSpec((tm, d), lambda i: (t0 + i, 0)),
        input_output_aliases=aliases,
        compiler_params=pltpu.CompilerParams(
            dimension_semantics=("arbitrary",), vmem_limit_bytes=VMEM_LIMIT),
        name="combine",
    )(*args)


def _pair_tables(counts, n_rows):
    ne = counts.shape[0]
    sizes = counts.astype(I32)
    ends = jnp.cumsum(sizes)
    starts = ends - sizes
    first_blk = starts // ROW_BLOCK
    last_blk = (ends - 1) // ROW_BLOCK
    n_pairs = jnp.where(sizes > 0, last_blk - first_blk + 1, 0)
    pair_end = jnp.cumsum(n_pairs)
    pair_start = pair_end - n_pairs
    max_pairs = n_rows // ROW_BLOCK + ne
    p = jnp.arange(max_pairs, dtype=I32)
    ok = p < pair_end[-1]
    pc = jnp.minimum(p, pair_end[-1] - 1)
    pair_e = jnp.minimum(jnp.sum((pair_end[None, :] <= pc[:, None]).astype(I32), axis=1), ne - 1)
    pair_blk = (first_blk[pair_e] + pc - pair_start[pair_e]).astype(I32)
    eid = jnp.arange(ne, dtype=I32)
    later = lax.cummin(jnp.where(sizes > 0, eid, ne), reverse=True)
    next_e = jnp.concatenate([later[1:], jnp.full((1,), ne, I32)])
    next_e = jnp.where(next_e < ne, next_e, eid)
    return pair_e, pair_blk, ok.astype(I32), starts.astype(I32), ends.astype(I32), next_e


def kernel(x, c, w_ada, b_ada, w_in, lb_logits, hgrn_norm, lam_re, lam_im, log_dt, b_re, b_im,
           c_re, c_im, d_skip, w_glu, b_glu, w_out, w_router, router_bias, w_gate, w_up, w_down,
           ws_gate, ws_up, ws_down, final_gain):
    bsz, seq, d = x.shape
    n = bsz * seq
    fdim = HGRN_HEADS * HGRN_KDIM
    n_chunks = seq // CHUNK
    lb = jnp.cumsum(jax.nn.softmax(lb_logits.astype(F32), axis=0), axis=0)[0].reshape(1, fdim)

    mod = _mod_call(c, w_ada[0], b_ada[0]).reshape(bsz, 6, d)
    nb = w_in.shape[2] - 4 * fdim
    perm = jnp.concatenate([jnp.arange(0, nb, 2), jnp.arange(1, nb, 2)])
    w_in_b = w_in[0].astype(BF16)
    ut, out_a, ucm = _mix_front_call(x, mod, w_in_b[:, :4 * fdim], w_in_b[:, 4 * fdim:][:, perm].T, lb,
                                hgrn_norm[0].reshape(1, fdim))

    cb, lp2, p_tab, q_tab, a1, a2 = _s5_tables(lam_re[0], lam_im[0], log_dt[0], b_re[0], b_im[0],
                                             c_re[0], c_im[0])
    yt = _s5_call(ut, cb, lp2, p_tab, q_tab, a1, a2, n_chunks, bsz)

    wr_t = w_router[0].T
    wr_hi = wr_t.astype(BF16)
    wr_lo = (wr_t - wr_hi.astype(F32)).astype(BF16)
    w_out_b = w_out[0].astype(BF16)
    w_out_p = jnp.concatenate([w_out_b[:fdim], w_out_b[fdim:][perm]], axis=0)
    x1, hp, logits_t = _mix_back_call(
        x.reshape(n, d), out_a.reshape(n, fdim), yt, ucm, mod,
        d_skip[0][perm].reshape(nb, 1), w_glu[0][perm][:, perm].T.astype(BF16),
        b_glu[0][perm].reshape(nb, 1), w_out_p, wr_hi, wr_lo, seq)

    top_idx, top_w, rank, counts = _route_call(logits_t, router_bias[0])
    pair_e, pair_blk, pair_ok, starts, ends, next_e = _pair_tables(counts[:, 0], n * TOP_K)

    def windowed(a):
        return a.reshape(TOP_K, n // SC_WINDOW, SC_WINDOW).transpose(1, 0, 2)

    xs, pos_win = _dispatch_call(windowed(top_idx), windowed(rank), starts,
                                 hp)
    y_sorted = _experts_call(pair_e, pair_blk, pair_ok, starts, ends, next_e, xs,
                             w_gate[0], w_up[0], w_down[0])
    w_tok = top_w
    g2 = mod[:, 5:6, :]
    fgain = final_gain.reshape(1, d)
    wsg, wsu, wsd = ws_gate[0].astype(BF16), ws_up[0].astype(BF16), ws_down[0].astype(BF16)
    wins = pos_win.shape[0] // TAIL_PARTS
    out = None
    for part in range(TAIL_PARTS):
        y_tok = _collect_call(pos_win[part * wins:(part + 1) * wins], y_sorted)
        out = _combine_call(y_tok, x1, hp, w_tok, g2, fgain, wsg, wsu, wsd, seq, part, out)
    return out.reshape(bsz, seq, d)
```

```python
import functools

import jax
import jax.numpy as jnp
from jax import lax
from jax.experimental import pallas as pl
from jax.experimental.pallas import tpu as pltpu
from jax.experimental.pallas import tpu_sc as plsc

F32 = jnp.float32
BF16 = jnp.bfloat16
I32 = jnp.int32

EPS = 1e-6
CHUNK = 64
HGRN_HEADS = 4
HGRN_KDIM = 128
S5_GROUP = 16
S5_STATE = 64
N_EXPERT_GROUPS = 8
TOPK_GROUPS = 4
TOP_K = 8
ROUTE_SCALE = 2.5
LANES = 128
PACK_ROWS = 4
SC_CORES = 2
SC_SUBCORES = 16
SC_LANES = 16
SC_WINDOW = 64

SEQ_TILE = 512
TOK_TILE = 512
ROUTE_TILE = 512
MOVE_TILE = 512
ROW_BLOCK = 2048
SUB_BLOCK = 512
TAIL_PARTS = 4
VMEM_LIMIT = 56 * 1024 * 1024

_NT = (((1,), (1,)), ((), ()))
_TN = (((0,), (0,)), ((), ()))


def _sigmoid(v):
    return 0.5 * jnp.tanh(0.5 * v) + 0.5


def _silu(v):
    return v * _sigmoid(v)


def _bdot(a, b):
    return jnp.dot(a.astype(BF16), b.astype(BF16), preferred_element_type=F32)


def _store_packed_planes(ref, val):
    for j in range(PACK_ROWS):
        lo = val[:, 2 * j * LANES:(2 * j + 1) * LANES]
        hi = val[:, (2 * j + 1) * LANES:(2 * j + 2) * LANES]
        ref[j] = pltpu.pack_elementwise([lo, hi], packed_dtype=BF16)


def _mod_kernel(c_ref, w_ref, b_ref, o_ref):
    o_ref[...] = _bdot(_silu(c_ref[...]), w_ref[...]) + b_ref[...]


def _mod_call(c, w_ada, b_ada):
    bsz, d = c.shape
    n_out = w_ada.shape[1]
    return pl.pallas_call(
        _mod_kernel,
        out_shape=jax.ShapeDtypeStruct((bsz, n_out), F32),
        grid=(n_out // d,),
        in_specs=[pl.BlockSpec((bsz, d), lambda j: (0, 0)),
                  pl.BlockSpec((d, d), lambda j: (0, j)),
                  pl.BlockSpec((1, d), lambda j: (0, j))],
        out_specs=pl.BlockSpec((bsz, d), lambda j: (0, j)),
        compiler_params=pltpu.CompilerParams(vmem_limit_bytes=VMEM_LIMIT),
        name="mod",
    )(c, w_ada, b_ada.reshape(1, n_out))


def _split_chunk_pairs(tile_even, tile_odd):
    low = lax.broadcasted_iota(I32, tile_even.shape, 1) < CHUNK
    first = jnp.where(low, tile_even, pltpu.roll(tile_odd, CHUNK, 1))
    second = jnp.where(low, pltpu.roll(tile_even, CHUNK, 1), tile_odd)
    return first, second


def _mix_front_kernel(x_ref, mod_ref, win_ref, wut_ref, lb_ref, gn_ref, ltri_ref,
                      ut_ref, oa_ref, ucm_ref, proj_a, proj_b, st_ref, flat_ref, hb_ref, *, tiles_per_seq):
    fdim = HGRN_HEADS * HGRN_KDIM
    ts = x_ref.shape[1]
    pairs = wut_ref.shape[0] // 2
    rows = ts // CHUNK
    i = pl.program_id(0)

    @pl.when(i == 0)
    def _():
        proj_b[...] = jnp.zeros_like(proj_b)

    @pl.when(lax.rem(jnp.maximum(i - 1, 0), tiles_per_seq) == 0)
    def _():
        st_ref[...] = jnp.zeros_like(st_ref)

    def normalise():
        x = x_ref[0]
        ms = jnp.mean(x * x, axis=-1, keepdims=True)
        h = x * lax.rsqrt(ms + EPS) * (1.0 + mod_ref[0, 1:2, :]) + mod_ref[0, 0:1, :]
        hb_ref[...] = h.astype(BF16)

    def project_slab(proj_ref, n):
        width = proj_ref.shape[1] // rows
        cols = slice(n * width, (n + 1) * width)
        proj_ref[:, cols] = jnp.dot(hb_ref[...], win_ref[:, cols], preferred_element_type=F32)

    def s5_input():
        u_t = lax.dot_general(wut_ref[...], hb_ref[...], _NT, preferred_element_type=F32)
        ucm_ref[...] = u_t.astype(BF16)
        for m in range(ts // LANES):
            first, second = _split_chunk_pairs(u_t[:pairs, m * LANES:(m + 1) * LANES],
                                               u_t[pairs:, m * LANES:(m + 1) * LANES])
            flat_ref[2 * m * pairs:(2 * m + 1) * pairs, :] = first
            flat_ref[(2 * m + 1) * pairs:(2 * m + 2) * pairs, :] = second
        per_group = S5_GROUP // 2
        for g in range(pairs // per_group):
            slab = jnp.stack([flat_ref[c * pairs + g * per_group:c * pairs + (g + 1) * per_group, :]
                              for c in range(rows)], axis=0)
            ut_ref[g] = slab.reshape(rows, per_group * LANES)

    lb = lb_ref[...]
    gn = gn_ref[...]
    ltri = ltri_ref[...]
    row = lax.broadcasted_iota(I32, (CHUNK, CHUNK), 0)
    col = lax.broadcasted_iota(I32, (CHUNK, CHUNK), 1)
    causal = row >= col

    def step(proj_ref, next_ref):
        normalise()
        for ci in range(rows):
            project_slab(next_ref, ci)
            r0 = ci * CHUNK
            q = proj_ref[r0:r0 + CHUNK, 0:fdim]
            fl = proj_ref[r0:r0 + CHUNK, fdim:2 * fdim]
            iv = proj_ref[r0:r0 + CHUNK, 2 * fdim:3 * fdim]
            og = proj_ref[r0:r0 + CHUNK, 3 * fdim:4 * fdim]
            f = lb + (1.0 - lb) * _sigmoid(fl)
            lf_hi, lf_mid, lf_lo = _split3(jnp.log(f))
            b = (jnp.dot(ltri, lf_hi, preferred_element_type=F32)
                 + jnp.dot(ltri, lf_mid, preferred_element_type=F32)
                 + jnp.dot(ltri, lf_lo, preferred_element_type=F32))
            b_ref = b[CHUNK // 2 - 1:CHUNK // 2, :]
            b_last = b[CHUNK - 1:CHUNK, :]
            qs = _silu(q)
            kk = 1.0 - f
            qe = (qs * jnp.exp(b - b_ref)).astype(BF16)
            ke = (kk * jnp.exp(b_ref - b)).astype(BF16)
            qb = (qs * jnp.exp(b)).astype(BF16)
            k2 = (kk * jnp.exp(b_last - b)).astype(BF16)
            dec = jnp.exp(b_last)
            ivb = iv.astype(BF16)
            outs = []
            for hh in range(HGRN_HEADS):
                sl = slice(hh * HGRN_KDIM, (hh + 1) * HGRN_KDIM)
                att = lax.dot_general(qe[:, sl], ke[:, sl], _NT, preferred_element_type=F32)
                att = jnp.where(causal, att, 0.0)
                st = st_ref[hh]
                o = jnp.dot(att.astype(BF16), ivb[:, sl], preferred_element_type=F32)
                o = o + lax.dot_general(qb[:, sl], st.astype(BF16), _NT, preferred_element_type=F32)
                st_ref[hh] = st * dec[:, sl] + lax.dot_general(
                    ivb[:, sl], k2[:, sl], _TN, preferred_element_type=F32)
                outs.append(o * lax.rsqrt(jnp.mean(o * o, axis=-1, keepdims=True) + EPS))
            o = jnp.concatenate(outs, axis=1) * gn * _silu(og)
            oa_ref[0, r0:r0 + CHUNK, :] = o.astype(BF16)
        s5_input()

    @pl.when(lax.rem(i, 2) == 0)
    def _():
        step(proj_b, proj_a)

    @pl.when(lax.rem(i, 2) == 1)
    def _():
        step(proj_a, proj_b)


def _mix_front_call(x, mod, w_main, w_ut, lb, gn):
    bsz, seq, d = x.shape
    fdim = HGRN_HEADS * HGRN_KDIM
    ncols = w_main.shape[1]
    nb = w_ut.shape[0]
    groups = nb // S5_GROUP
    ltri = jnp.tril(jnp.ones((CHUNK, CHUNK), BF16))
    ts = SEQ_TILE
    tiles = seq // ts
    n_tiles = bsz * tiles
    rows = ts // CHUNK

    def cur(i):
        return jnp.minimum(i, n_tiles - 1)

    def prev(i):
        return jnp.maximum(i - 1, 0)

    return pl.pallas_call(
        functools.partial(_mix_front_kernel, tiles_per_seq=tiles),
        out_shape=(jax.ShapeDtypeStruct((groups, bsz * seq // CHUNK, S5_GROUP * CHUNK), F32),
                   jax.ShapeDtypeStruct((bsz, seq, fdim), BF16),
                   jax.ShapeDtypeStruct((nb, bsz * seq), BF16)),
        grid=(n_tiles + 1,),
        in_specs=[pl.BlockSpec((1, ts, d), lambda i: (cur(i) // tiles, cur(i) % tiles, 0)),
                  pl.BlockSpec((1, 6, d), lambda i: (cur(i) // tiles, 0, 0)),
                  pl.BlockSpec((d, ncols), lambda i: (0, 0)),
                  pl.BlockSpec((nb, d), lambda i: (0, 0)),
                  pl.BlockSpec((1, fdim), lambda i: (0, 0)),
                  pl.BlockSpec((1, fdim), lambda i: (0, 0)),
                  pl.BlockSpec((CHUNK, CHUNK), lambda i: (0, 0))],
        out_specs=(pl.BlockSpec((groups, rows, S5_GROUP * CHUNK), lambda i: (0, cur(i), 0)),
                   pl.BlockSpec((1, ts, fdim), lambda i: (prev(i) // tiles, prev(i) % tiles, 0)),
                   pl.BlockSpec((nb, ts), lambda i: (0, cur(i)))),
        scratch_shapes=[pltpu.VMEM((ts, ncols), F32), pltpu.VMEM((ts, ncols), F32),
                        pltpu.VMEM((HGRN_HEADS, fdim // HGRN_HEADS, HGRN_KDIM), F32),
                        pltpu.VMEM((rows * nb // 2, LANES), F32),
                        pltpu.VMEM((ts, d), BF16)],
        compiler_params=pltpu.CompilerParams(
            dimension_semantics=("arbitrary",), vmem_limit_bytes=VMEM_LIMIT),
        name="mix_front",
    )(x, mod, w_main, w_ut, lb, gn, ltri)


def _s5_tables(lam_re, lam_im, log_dt, b_re, b_im, c_re, c_im):
    t = CHUNK
    lam = lax.complex(jnp.minimum(lam_re, -1e-4), lam_im)
    lam_dt = lam * jnp.exp(log_dt)[:, None]
    lam_bar = jnp.exp(lam_dt)
    b_bar = ((lam_bar - 1.0) / lam)[..., None] * lax.complex(b_re, b_im)
    c_mat = lax.complex(c_re, c_im)
    taus = jnp.arange(t + 1, dtype=F32)
    lam_pow = jnp.exp(lam_dt[:, None, :] * taus[None, :, None])
    g, p = lam.shape
    c = b_re.shape[-1]
    cb = c_mat[:, None, :, :] * b_bar.transpose(0, 2, 1)[:, :, None, :]
    cb = jnp.concatenate([cb.real, -cb.imag], axis=-1)
    cb = cb.reshape(g, c, c // 2, 2 * 2 * p).reshape(g, c * c // 2, 4 * p)
    lp = jnp.concatenate([lam_pow[:, :t].real, lam_pow[:, :t].imag], axis=-1).transpose(0, 2, 1)
    zero = jnp.zeros_like(lp)
    lp2 = jnp.concatenate([jnp.concatenate([lp, zero], axis=2),
                           jnp.concatenate([zero, lp], axis=2)], axis=1)
    pc = lam_pow[:, t - 1::-1][:, :t, :, None] * b_bar[:, None, :, :]
    pc = pc.transpose(0, 3, 1, 2).reshape(g, c * t, p)
    p_tab = jnp.concatenate([pc.real, pc.imag], axis=-1)
    ql = c_mat[:, None, :, :] * lam_pow[:, 1:t + 1, None, :]
    ql = ql.transpose(0, 3, 2, 1).reshape(g, p, c * t)
    q_tab = jnp.concatenate([ql.real, -ql.imag], axis=1)
    lam_t = lam_pow[:, t]
    a1 = jnp.concatenate([lam_t.real, lam_t.real], axis=-1)[:, None, :]
    a2 = jnp.concatenate([-lam_t.imag, lam_t.imag], axis=-1)[:, None, :]
    return cb, lp2, p_tab.astype(BF16), q_tab.astype(BF16), a1, a2


def _split3(v):
    hi = v.astype(BF16)
    rem = v - hi.astype(F32)
    mid = rem.astype(BF16)
    return hi, mid, (rem - mid.astype(F32)).astype(BF16)


def _s5_kernel(u_ref, cb_ref, lp_ref, p_ref, q_ref, a1_ref, a2_ref, y_ref, v_ref, xs_ref, m_ref, k_ref,
               *, n_chunks, n_batch):
    c_hi, c_mid, c_lo = _split3(cb_ref[0])
    l_hi, l_mid, l_lo = _split3(lp_ref[0])
    k_ref[...] = (jnp.dot(c_hi, l_hi, preferred_element_type=F32)
                  + jnp.dot(c_hi, l_mid, preferred_element_type=F32)
                  + jnp.dot(c_mid, l_hi, preferred_element_type=F32)
                  + jnp.dot(c_hi, l_lo, preferred_element_type=F32)
                  + jnp.dot(c_mid, l_mid, preferred_element_type=F32)
                  + jnp.dot(c_lo, l_hi, preferred_element_type=F32))
    n_pairs = m_ref.shape[1] // LANES
    n_in = k_ref.shape[0] // n_pairs
    lane = lax.broadcasted_iota(I32, (CHUNK, LANES), 1)
    causal = (lane & (CHUNK - 1)) >= lax.broadcasted_iota(I32, (CHUNK, LANES), 0)
    for ci in range(n_in):
        for a in range(n_pairs):
            lags = jnp.broadcast_to(k_ref[ci * n_pairs + a:ci * n_pairs + a + 1, :], (8, LANES))
            base = pltpu.roll(lags, 0, 1, stride=1, stride_axis=0)
            tile = jnp.concatenate([base if q == 0 else pltpu.roll(base, 8 * q, 1)
                                    for q in range(CHUNK // 8)], axis=0)
            m_ref[ci * CHUNK:(ci + 1) * CHUNK, a * LANES:(a + 1) * LANES] = jnp.where(
                causal, tile, 0.0).astype(BF16)
    u = u_ref[0].astype(BF16)
    v_ref[...] = jnp.dot(u, p_ref[0], preferred_element_type=F32)
    a1 = a1_ref[0]
    a2 = a2_ref[0]
    half = xs_ref.shape[1] // 2

    def step(n, state):
        xs_ref[pl.ds(n, n_batch, stride=n_chunks), :] = state
        return (a1 * state + a2 * pltpu.roll(state, half, 1)
                + v_ref[pl.ds(n, n_batch, stride=n_chunks), :])

    lax.fori_loop(0, n_chunks, step, jnp.zeros((n_batch, xs_ref.shape[1]), F32))
    y = jnp.dot(u, m_ref[...], preferred_element_type=F32)
    y_ref[0] = y + jnp.dot(xs_ref[...].astype(BF16), q_ref[0], preferred_element_type=F32)


def _s5_call(ut, cb, lp2, p_tab, q_tab, a1, a2, n_chunks, n_batch):
    g, rows, width = ut.shape
    p2 = p_tab.shape[-1]
    return pl.pallas_call(
        functools.partial(_s5_kernel, n_chunks=n_chunks, n_batch=n_batch),
        out_shape=jax.ShapeDtypeStruct((g, rows, width), F32),
        grid=(g,),
        in_specs=[pl.BlockSpec((1, rows, width), lambda i: (i, 0, 0)),
                  pl.BlockSpec((1,) + cb.shape[1:], lambda i: (i, 0, 0)),
                  pl.BlockSpec((1,) + lp2.shape[1:], lambda i: (i, 0, 0)),
                  pl.BlockSpec((1, width, p2), lambda i: (i, 0, 0)),
                  pl.BlockSpec((1, p2, width), lambda i: (i, 0, 0)),
                  pl.BlockSpec((1, 1, p2), lambda i: (i, 0, 0)),
                  pl.BlockSpec((1, 1, p2), lambda i: (i, 0, 0))],
        out_specs=pl.BlockSpec((1, rows, width), lambda i: (i, 0, 0)),
        scratch_shapes=[pltpu.VMEM((rows, p2), F32), pltpu.VMEM((rows, p2), F32),
                        pltpu.VMEM((width, width), BF16),
                        pltpu.VMEM((cb.shape[1], lp2.shape[2]), F32)],
        compiler_params=pltpu.CompilerParams(
            dimension_semantics=("arbitrary",), vmem_limit_bytes=VMEM_LIMIT),
        name="s5",
    )(ut, cb, lp2, p_tab, q_tab, a1, a2)


def _token_major(flat_ref, src_ref):
    groups, rows, _ = src_ref.shape
    per_group = S5_GROUP // 2
    pairs = groups * per_group
    for g in range(groups):
        slab = src_ref[g].reshape(rows, per_group, LANES)
        for c in range(rows):
            flat_ref[c * pairs + g * per_group:c * pairs + (g + 1) * per_group, :] = slab[c]
    tiles = []
    for m in range(rows // 2):
        even, odd = _split_chunk_pairs(flat_ref[2 * m * pairs:(2 * m + 1) * pairs, :],
                                       flat_ref[(2 * m + 1) * pairs:(2 * m + 2) * pairs, :])
        tiles.append(jnp.concatenate([even, odd], axis=0))
    return jnp.concatenate(tiles, axis=1)


def _mix_back_kernel(x_ref, oa_ref, yt_ref, ucm_ref, mod_ref, dskip_ref, wglu_ref, bglu_ref,
                     wout_ref, wrh_ref, wrl_ref,
                     x1_ref, hp_ref, lt_ref, flat_ref):
    na = oa_ref.shape[1]
    y_t = _token_major(flat_ref, yt_ref)
    u_t = ucm_ref[...].astype(F32)
    z_t = jax.nn.gelu(y_t + dskip_ref[...] * u_t)
    gate_t = _sigmoid(jnp.dot(wglu_ref[...], z_t.astype(BF16), preferred_element_type=F32)
                      + bglu_ref[...])
    ob_t = (z_t * gate_t).astype(BF16)
    mixed = (jnp.dot(oa_ref[...], wout_ref[0:na, :], preferred_element_type=F32)
             + lax.dot_general(ob_t, wout_ref[na:, :], _TN, preferred_element_type=F32))
    x1 = x_ref[...] + mod_ref[0, 2:3, :] * mixed
    ms = jnp.mean(x1 * x1, axis=-1, keepdims=True)
    h2 = x1 * lax.rsqrt(ms + EPS) * (1.0 + mod_ref[0, 4:5, :]) + mod_ref[0, 3:4, :]
    _store_packed_planes(hp_ref, h2)
    h_hi = h2.astype(BF16)
    h_lo = (h2 - h_hi.astype(F32)).astype(BF16)
    lt = lax.dot_general(wrh_ref[...], h_hi, _NT, preferred_element_type=F32)
    lt = lt + lax.dot_general(wrl_ref[...], h_hi, _NT, preferred_element_type=F32)
    lt = lt + lax.dot_general(wrh_ref[...], h_lo, _NT, preferred_element_type=F32)
    lt_ref[...] = lt
    x1_ref[...] = x1


def _mix_back_call(x2d, oa, yt, ucm, mod, dskip, wglu, bglu, wout, wrh, wrl, seq):
    n, d = x2d.shape
    nb = oa.shape[1]
    ne = wrh.shape[0]
    tm = TOK_TILE
    per_b = seq // tm
    groups, _, width = yt.shape
    rows = tm // CHUNK
    const = lambda i: (0, 0)
    flat_block = pl.BlockSpec((groups, rows, width), lambda i: (0, i, 0))
    return pl.pallas_call(
        _mix_back_kernel,
        out_shape=(jax.ShapeDtypeStruct((n, d), F32),
                   jax.ShapeDtypeStruct((PACK_ROWS, n, LANES), I32),
                   jax.ShapeDtypeStruct((ne, n), F32)),
        grid=(n // tm,),
        in_specs=[pl.BlockSpec((tm, d), lambda i: (i, 0)),
                  pl.BlockSpec((tm, nb), lambda i: (i, 0)),
                  flat_block,
                  pl.BlockSpec((nb, tm), lambda i: (0, i)),
                  pl.BlockSpec((1, 6, d), lambda i: (i // per_b, 0, 0)),
                  pl.BlockSpec((nb, 1), const),
                  pl.BlockSpec((nb, nb), const),
                  pl.BlockSpec((nb, 1), const),
                  pl.BlockSpec((d, d), const),
                  pl.BlockSpec((ne, d), const),
                  pl.BlockSpec((ne, d), const)],
        out_specs=(pl.BlockSpec((tm, d), lambda i: (i, 0)),
                   pl.BlockSpec((PACK_ROWS, tm, LANES), lambda i: (0, i, 0)),
                   pl.BlockSpec((ne, tm), lambda i: (0, i))),
        scratch_shapes=[pltpu.VMEM((rows * nb // 2, LANES), F32)],
        compiler_params=pltpu.CompilerParams(
            dimension_semantics=("arbitrary",), vmem_limit_bytes=VMEM_LIMIT),
        name="mix_back",
    )(x2d, oa, yt, ucm, mod, dskip, wglu, bglu, wout, wrh, wrl)


def _route_kernel(lt_ref, bias_ref, su_ref, idx_ref, w_ref, rank_ref, cnt_ref, run_ref):
    ne, tr = lt_ref.shape
    per_group = ne // N_EXPERT_GROUPS
    neg = -jnp.inf

    @pl.when(pl.program_id(0) == 0)
    def _():
        run_ref[...] = jnp.zeros_like(run_ref)

    s = _sigmoid(lt_ref[...])
    sel = s + bias_ref[...]
    gio = lax.broadcasted_iota(I32, (per_group, tr), 0)
    gscore = []
    for g in range(N_EXPERT_GROUPS):
        v = sel[g * per_group:(g + 1) * per_group, :]
        m1 = jnp.max(v, axis=0, keepdims=True)
        i1 = jnp.min(jnp.where(v == m1, gio, per_group), axis=0, keepdims=True)
        m2 = jnp.max(jnp.where(gio == i1, neg, v), axis=0, keepdims=True)
        gscore.append(m1 + m2)
    masked = []
    for g in range(N_EXPERT_GROUPS):
        ahead = jnp.zeros((1, tr), I32)
        for o in range(N_EXPERT_GROUPS):
            if o == g:
                continue
            wins = (gscore[o] >= gscore[g]) if o < g else (gscore[o] > gscore[g])
            ahead = ahead + wins.astype(I32)
        keep = ahead < TOPK_GROUPS
        masked.append(jnp.where(keep, sel[g * per_group:(g + 1) * per_group, :], neg))
    selm = jnp.concatenate(masked, axis=0)
    eio = lax.broadcasted_iota(I32, (ne, tr), 0)
    candidate = selm > neg
    idxs, ws = [], []
    for k in range(TOP_K):
        m = jnp.max(selm, axis=0, keepdims=True)
        ik = jnp.min(jnp.where(selm == m, eio, ne), axis=0, keepdims=True)
        onehot = eio == ik
        ws.append(jnp.sum(jnp.where(onehot, s, 0.0), axis=0, keepdims=True))
        selm = jnp.where(onehot, neg, selm)
        idxs.append(ik)
    hits = jnp.where(jnp.logical_and(candidate, selm == neg), 1.0, 0.0)
    wsum = ws[0]
    for k in range(1, TOP_K):
        wsum = wsum + ws[k]
    scale = ROUTE_SCALE / wsum
    ranks = jnp.dot(hits.astype(BF16), su_ref[...], preferred_element_type=F32) + run_ref[...]
    for k in range(TOP_K):
        idx_ref[k:k + 1, :] = idxs[k]
        w_ref[k:k + 1, :] = ws[k] * scale
        rk = jnp.sum(jnp.where(eio == idxs[k], ranks, 0.0), axis=0, keepdims=True)
        rank_ref[k:k + 1, :] = rk.astype(I32)
    run_ref[...] = run_ref[...] + jnp.sum(hits, axis=1, keepdims=True)
    cnt_ref[...] = run_ref[...]


def _route_call(lt, bias):
    ne, n = lt.shape
    tr = ROUTE_TILE
    su = jnp.triu(jnp.ones((tr, tr), F32), k=1).astype(BF16)
    return pl.pallas_call(
        _route_kernel,
        out_shape=(jax.ShapeDtypeStruct((TOP_K, n), I32),
                   jax.ShapeDtypeStruct((TOP_K, n), F32),
                   jax.ShapeDtypeStruct((TOP_K, n), I32),
                   jax.ShapeDtypeStruct((ne, 1), F32)),
        grid=(n // tr,),
        in_specs=[pl.BlockSpec((ne, tr), lambda i: (0, i)),
                  pl.BlockSpec((ne, 1), lambda i: (0, 0)),
                  pl.BlockSpec((tr, tr), lambda i: (0, 0))],
        out_specs=(pl.BlockSpec((TOP_K, tr), lambda i: (0, i)),
                   pl.BlockSpec((TOP_K, tr), lambda i: (0, i)),
                   pl.BlockSpec((TOP_K, tr), lambda i: (0, i)),
                   pl.BlockSpec((ne, 1), lambda i: (0, 0))),
        scratch_shapes=[pltpu.VMEM((ne, 1), F32)],
        compiler_params=pltpu.CompilerParams(
            dimension_semantics=("arbitrary",), vmem_limit_bytes=VMEM_LIMIT),
        name="route",
    )(lt, bias.reshape(ne, 1), su)


def _sc_mesh():
    return plsc.VectorSubcoreMesh(core_axis_name="c", subcore_axis_name="s",
                                  num_cores=SC_CORES, num_subcores=SC_SUBCORES)


def _sc_worker():
    return lax.axis_index("s") * SC_CORES + lax.axis_index("c")


def _dispatch_call(idx_win, rank_win, starts, hp):
    n = hp.shape[1]
    n_workers = SC_CORES * SC_SUBCORES
    wins_per_worker = n // SC_WINDOW // n_workers
    lanes = SC_LANES

    def body(hp_hbm, idx_hbm, rank_hbm, st_hbm, xs_hbm, pos_hbm,
             idx_a, idx_b, rank_a, rank_b, pos_a, pos_b, rows_a, rows_b, st_v,
             load_a, load_b, scatter_sem, pos_sem):
        first_win = _sc_worker() * wins_per_worker
        idx_v, rank_v, pos_v = (idx_a, idx_b), (rank_a, rank_b), (pos_a, pos_b)
        rows_v, load_sem = (rows_a, rows_b), (load_a, load_b)
        pltpu.sync_copy(st_hbm, st_v)

        def loads(win, slot):
            pieces = tuple(
                pltpu.make_async_copy(hp_hbm.at[j, pl.ds(win * SC_WINDOW, SC_WINDOW)],
                                      rows_v[slot].at[:, j], load_sem[slot])
                for j in range(PACK_ROWS))
            return pieces + (pltpu.make_async_copy(idx_hbm.at[win], idx_v[slot], load_sem[slot]),
                             pltpu.make_async_copy(rank_hbm.at[win], rank_v[slot], load_sem[slot]))

        for cp in loads(first_win, 0):
            cp.start()

        @pl.loop(0, wins_per_worker, step=2)
        def _(w):
            for slot in range(2):
                win = first_win + w + slot
                for cp in loads(win, slot):
                    cp.wait()

                @pl.when(w + slot + 1 < wins_per_worker)
                def _():
                    for cp in loads(win + 1, 1 - slot):
                        cp.start()

                for k in range(TOP_K):
                    for c in range(SC_WINDOW // lanes):
                        seg = pl.ds(c * lanes, lanes)
                        base = plsc.load_gather(st_v, [idx_v[slot][k, seg]])
                        pos_v[slot][k, seg] = base + rank_v[slot][k, seg]
                pos_out = pltpu.async_copy(pos_v[slot], pos_hbm.at[win], pos_sem)
                copies = [pltpu.async_copy(rows_v[slot], xs_hbm.at[pos_v[slot].at[k]], scatter_sem)
                          for k in range(TOP_K)]
                for cp in copies:
                    cp.wait()
                pos_out.wait()

    idx_buf = pltpu.VMEM((TOP_K, SC_WINDOW), I32)
    row_buf = pltpu.VMEM((SC_WINDOW, PACK_ROWS, LANES), hp.dtype)
    return pl.kernel(
        body,
        out_type=(jax.ShapeDtypeStruct((n * TOP_K, PACK_ROWS, LANES), hp.dtype),
                  jax.ShapeDtypeStruct(idx_win.shape, I32)),
        mesh=_sc_mesh(),
        scratch_types=[idx_buf, idx_buf, idx_buf, idx_buf, idx_buf, idx_buf, row_buf, row_buf,
                       pltpu.VMEM(starts.shape, I32),
                       pltpu.SemaphoreType.DMA, pltpu.SemaphoreType.DMA, pltpu.SemaphoreType.DMA,
                       pltpu.SemaphoreType.DMA],
        compiler_params=pltpu.CompilerParams(needs_layout_passes=False),
        name="dispatch",
    )(hp, idx_win, rank_win, starts)


def _collect_call(pos_win, y_sorted):
    n = pos_win.shape[0] * SC_WINDOW
    n_workers = SC_CORES * SC_SUBCORES
    wins_per_worker = n // SC_WINDOW // n_workers

    def body(ys_hbm, pos_hbm, out_hbm, idx_a, idx_b, rows_a, rows_b, idx_sem_a, idx_sem_b,
             gather_sem, write_sem):
        first_win = _sc_worker() * wins_per_worker
        bufs = (rows_a, rows_b)
        idx_v, idx_sem = (idx_a, idx_b), (idx_sem_a, idx_sem_b)

        def idx_load(win, slot):
            return pltpu.make_async_copy(pos_hbm.at[win], idx_v[slot], idx_sem[slot])

        idx_load(first_win, 0).start()

        @pl.loop(0, wins_per_worker, step=2)
        def _(w):
            for slot in range(2):
                win = first_win + w + slot
                idx_load(win, slot).wait()

                @pl.when(w + slot + 1 < wins_per_worker)
                def _():
                    idx_load(win + 1, 1 - slot).start()

                def gather(k):
                    return pltpu.async_copy(ys_hbm.at[idx_v[slot].at[k]], bufs[k % 2], gather_sem)

                pending_gather = gather(0)
                pending_write = None
                for k in range(TOP_K):
                    pending_gather.wait()
                    if pending_write is not None:
                        for cp in pending_write:
                            cp.wait()
                    if k + 1 < TOP_K:
                        pending_gather = gather(k + 1)
                    pending_write = [
                        pltpu.async_copy(bufs[k % 2].at[:, j],
                                         out_hbm.at[j, k, pl.ds(win * SC_WINDOW, SC_WINDOW)], write_sem)
                        for j in range(PACK_ROWS)]
                for cp in pending_write:
                    cp.wait()

    idx_buf = pltpu.VMEM((TOP_K, SC_WINDOW), I32)
    row_buf = pltpu.VMEM((SC_WINDOW,) + y_sorted.shape[1:], y_sorted.dtype)
    return pl.kernel(
        body,
        out_type=jax.ShapeDtypeStruct((PACK_ROWS, TOP_K, n, LANES), y_sorted.dtype),
        mesh=_sc_mesh(),
        scratch_types=[idx_buf, idx_buf, row_buf, row_buf,
                       pltpu.SemaphoreType.DMA, pltpu.SemaphoreType.DMA,
                       pltpu.SemaphoreType.DMA, pltpu.SemaphoreType.DMA],
        name="collect",
    )(y_sorted, pos_win)


def _experts_kernel(pe_ref, pb_ref, pv_ref, st_ref, en_ref, nx_ref,
                    xs_hbm, wg_hbm, wu_hbm, wd_hbm, y_hbm,
                    wgb_ref, wub_ref, wdb_ref, wgf_ref, wuf_ref, wdf_ref, xbuf_ref, stage_ref,
                    in_sems, out_sems, w_sems, state_ref):
    p = pl.program_id(0)
    n_pairs = pl.num_programs(0)
    e = pe_ref[p]
    blk = pb_ref[p]
    rb = xbuf_ref.shape[2]
    prev = jnp.maximum(p - 1, 0)
    nxt = jnp.minimum(p + 1, n_pairs - 1)
    live = pv_ref[p] == 1
    first = jnp.logical_or(p == 0, pb_ref[prev] != blk)
    block_ends = jnp.logical_or(pb_ref[nxt] != blk, pv_ref[nxt] == 0)
    last = jnp.logical_or(p == n_pairs - 1, block_ends)

    def in_copy(j, block, slot):
        return pltpu.make_async_copy(xs_hbm.at[pl.ds(block * rb, rb), j], xbuf_ref.at[slot, j],
                                     in_sems.at[slot])

    def out_copy(j, block, slot):
        return pltpu.make_async_copy(stage_ref.at[slot, j], y_hbm.at[pl.ds(block * rb, rb), j],
                                     out_sems.at[slot])

    def drain(slot):
        @pl.when(state_ref[1 + slot] == 1)
        def _():
            for j in range(PACK_ROWS):
                out_copy(j, 0, slot).wait()
            state_ref[1 + slot] = 0

    @pl.when(p == 0)
    def _():
        state_ref[0] = 0
        state_ref[1] = 0
        state_ref[2] = 0
        stage_ref[...] = jnp.zeros_like(stage_ref)
        for j in range(PACK_ROWS):
            in_copy(j, blk, 0).start()

    slot = jnp.where(jnp.logical_and(first, p > 0), 1 - state_ref[0], state_ref[0])
    state_ref[0] = slot

    @pl.when(jnp.logical_and(first, blk + 1 < y_hbm.shape[0] // rb))
    def _():
        for j in range(PACK_ROWS):
            in_copy(j, blk + 1, 1 - slot).start()

    @pl.when(first)
    def _():
        for j in range(PACK_ROWS):
            in_copy(j, blk, slot).wait()
        drain(slot)

    def w_copies(expert, wslot):
        return (pltpu.make_async_copy(wg_hbm.at[expert], wgf_ref.at[wslot], w_sems.at[wslot]),
                pltpu.make_async_copy(wu_hbm.at[expert], wuf_ref.at[wslot], w_sems.at[wslot]),
                pltpu.make_async_copy(wd_hbm.at[expert], wdf_ref.at[wslot], w_sems.at[wslot]))

    @pl.when(p == 0)
    def _():
        state_ref[3] = 0
        for cp in w_copies(e, 0):
            cp.start()

    new_expert = jnp.logical_or(p == 0, pe_ref[prev] != e)
    wslot = jnp.where(jnp.logical_and(new_expert, p > 0), 1 - state_ref[3], state_ref[3])
    state_ref[3] = wslot

    @pl.when(new_expert)
    def _():
        for cp in w_copies(e, wslot):
            cp.wait()
        wgb_ref[...] = wgf_ref[wslot].astype(BF16)
        wub_ref[...] = wuf_ref[wslot].astype(BF16)
        wdb_ref[...] = wdf_ref[wslot].astype(BF16)

        @pl.when(nx_ref[e] != e)
        def _():
            for cp in w_copies(nx_ref[e], 1 - wslot):
                cp.start()

    lo_row = st_ref[e]
    hi_row = en_ref[e]

    def sub_block(s, row0, shared):
        sub = pl.ds(s * SUB_BLOCK, SUB_BLOCK)
        pieces = []
        for j in range(PACK_ROWS):
            w = xbuf_ref[slot, j, sub, :]
            pieces.append(lax.bitcast_convert_type(w.astype(jnp.int16), BF16))
            pieces.append(lax.bitcast_convert_type(
                lax.shift_right_logical(w, 16).astype(jnp.int16), BF16))
        xb = jnp.concatenate(pieces, axis=1)
        gate = jnp.dot(xb, wgb_ref[...], preferred_element_type=F32)
        up = jnp.dot(xb, wub_ref[...], preferred_element_type=F32)
        yb = jnp.dot((_silu(gate) * up).astype(BF16), wdb_ref[...], preferred_element_type=F32)
        if shared:
            rows = row0 + lax.broadcasted_iota(I32, (SUB_BLOCK, 1), 0)
            mine = jnp.logical_and(rows >= lo_row, rows < hi_row)
        for j in range(PACK_ROWS):
            word = pltpu.pack_elementwise(
                [yb[:, 2 * j * LANES:(2 * j + 1) * LANES], yb[:, (2 * j + 1) * LANES:(2 * j + 2) * LANES]],
                packed_dtype=BF16)
            if shared:
                word = jnp.where(mine, word, stage_ref[slot, j, sub, :])
            stage_ref[slot, j, sub, :] = word

    def one(s):
        row0 = blk * rb + s * SUB_BLOCK
        touched = jnp.logical_and(live, jnp.logical_and(row0 < hi_row, row0 + SUB_BLOCK > lo_row))
        whole = jnp.logical_and(lo_row <= row0, hi_row >= row0 + SUB_BLOCK)

        @pl.when(jnp.logical_and(touched, whole))
        def _():
            sub_block(s, row0, shared=False)

        @pl.when(jnp.logical_and(touched, jnp.logical_not(whole)))
        def _():
            sub_block(s, row0, shared=True)

    for s in range(0, rb // SUB_BLOCK, 2):
        row0 = blk * rb + s * SUB_BLOCK
        both = jnp.logical_and(live, jnp.logical_and(lo_row <= row0, hi_row >= row0 + 2 * SUB_BLOCK))

        @pl.when(both)
        def _():
            sub_block(s, row0, shared=False)
            sub_block(s + 1, row0 + SUB_BLOCK, shared=False)

        @pl.when(jnp.logical_not(both))
        def _():
            one(s)
            one(s + 1)

    @pl.when(jnp.logical_and(live, last))
    def _():
        for j in range(PACK_ROWS):
            out_copy(j, blk, slot).start()
        state_ref[1 + slot] = 1

    @pl.when(p == n_pairs - 1)
    def _():
        drain(0)
        drain(1)


def _experts_call(pair_e, pair_blk, pair_ok, starts, ends, next_e, xs, w_gate, w_up, w_down):
    ne, d, de = w_gate.shape
    rb = ROW_BLOCK
    n_pairs = pair_e.shape[0]
    anywhere = pl.BlockSpec(memory_space=pl.ANY)
    grid_spec = pltpu.PrefetchScalarGridSpec(
        num_scalar_prefetch=6,
        grid=(n_pairs,),
        in_specs=[anywhere, anywhere, anywhere, anywhere],
        out_specs=anywhere,
        scratch_shapes=[pltpu.VMEM((d, de), BF16), pltpu.VMEM((d, de), BF16),
                        pltpu.VMEM((de, d), BF16),
                        pltpu.VMEM((2, d, de), F32), pltpu.VMEM((2, d, de), F32),
                        pltpu.VMEM((2, de, d), F32),
                        pltpu.VMEM((2, PACK_ROWS, rb, LANES), I32),
                        pltpu.VMEM((2, PACK_ROWS, rb, LANES), I32),
                        pltpu.SemaphoreType.DMA((2,)), pltpu.SemaphoreType.DMA((2,)),
                        pltpu.SemaphoreType.DMA((2,)),
                        pltpu.SMEM((4,), I32)],
    )
    return pl.pallas_call(
        _experts_kernel,
        out_shape=jax.ShapeDtypeStruct(xs.shape, xs.dtype),
        grid_spec=grid_spec,
        compiler_params=pltpu.CompilerParams(
            dimension_semantics=("arbitrary",), vmem_limit_bytes=VMEM_LIMIT),
        name="experts",
    )(pair_e, pair_blk, pair_ok, starts, ends, next_e, xs, w_gate, w_up, w_down)


def _combine_kernel(y0_ref, y1_ref, y2_ref, y3_ref, x1_ref, hp_ref, w_ref, g2_ref, fg_ref,
                    wsg_ref, wsu_ref, wsd_ref, o_ref):
    tm = x1_ref.shape[0]
    w = w_ref[...].T
    parts = []
    for y_ref in (y0_ref, y1_ref, y2_ref, y3_ref):
        for half in range(2):
            acc = None
            for k in range(TOP_K):
                piece = pltpu.unpack_elementwise(y_ref[k], index=half, packed_dtype=BF16,
                                                 unpacked_dtype=F32) * w[:, k:k + 1]
                acc = piece if acc is None else acc + piece
            parts.append(acc)
    routed = jnp.concatenate(parts, axis=1)
    pieces = []
    for j in range(PACK_ROWS):
        word = hp_ref[j]
        pieces.append(lax.bitcast_convert_type(word.astype(jnp.int16), BF16))
        pieces.append(lax.bitcast_convert_type(lax.shift_right_logical(word, 16).astype(jnp.int16), BF16))
    h2 = jnp.concatenate(pieces, axis=1)
    hid = _silu(jnp.dot(h2, wsg_ref[...], preferred_element_type=F32)) * jnp.dot(
        h2, wsu_ref[...], preferred_element_type=F32)
    shared = jnp.dot(hid.astype(BF16), wsd_ref[...], preferred_element_type=F32)
    x2 = x1_ref[...] + g2_ref[0] * (routed + shared)
    ms = jnp.mean(x2 * x2, axis=-1, keepdims=True)
    o_ref[...] = x2 * lax.rsqrt(ms + EPS) * fg_ref[...]


def _combine_kernel_into(*refs):
    _combine_kernel(*refs[:-2], refs[-1])


def _combine_call(y_tok, x1, hp, w_tok, g2, fgain, wsg, wsu, wsd, seq, part, prev_out):
    n, d = x1.shape
    dsh = wsg.shape[1]
    tm = MOVE_TILE
    per_b = seq // tm
    tiles = y_tok.shape[2] // tm
    t0 = part * tiles
    const = lambda i: (0, 0)

    def piece_spec(j):
        return pl.BlockSpec((None, TOP_K, tm, LANES), lambda i: (j, 0, i, 0))

    in_specs = [piece_spec(j) for j in range(PACK_ROWS)] + [
        pl.BlockSpec((tm, d), lambda i: (t0 + i, 0)),
        pl.BlockSpec((PACK_ROWS, tm, LANES), lambda i: (0, t0 + i, 0)),
        pl.BlockSpec((TOP_K, tm), lambda i: (0, t0 + i)),
        pl.BlockSpec((1, 1, d), lambda i: ((t0 + i) // per_b, 0, 0)),
        pl.BlockSpec((1, d), const),
        pl.BlockSpec((d, dsh), const),
        pl.BlockSpec((d, dsh), const),
        pl.BlockSpec((dsh, d), const)]
    args = [y_tok, y_tok, y_tok, y_tok, x1, hp, w_tok, g2, fgain, wsg, wsu, wsd]
    aliases = {}
    body = _combine_kernel
    if prev_out is not None:
        in_specs.append(pl.BlockSpec(memory_space=pl.ANY))
        args.append(prev_out)
        aliases = {len(args) - 1: 0}
        body = _combine_kernel_into
    return pl.pallas_call(
        body,
        out_shape=jax.ShapeDtypeStruct((n, d), F32),
        grid=(tiles,),
        in_specs=in_specs,
        out_specs=pl.BlockSpec((tm, d), lambda i: (t0 + i, 0)),
        input_output_aliases=aliases,
        compiler_params=pltpu.CompilerParams(
            dimension_semantics=("arbitrary",), vmem_limit_bytes=VMEM_LIMIT),
        name="combine",
    )(*args)


def _pair_tables(counts, n_rows):
    ne = counts.shape[0]
    sizes = counts.astype(I32)
    ends = jnp.cumsum(sizes)
    starts = ends - sizes
    first_blk = starts // ROW_BLOCK
    last_blk = (ends - 1) // ROW_BLOCK
    n_pairs = jnp.where(sizes > 0, last_blk - first_blk + 1, 0)
    pair_end = jnp.cumsum(n_pairs)
    pair_start = pair_end - n_pairs
    max_pairs = n_rows // ROW_BLOCK + ne
    p = jnp.arange(max_pairs, dtype=I32)
    ok = p < pair_end[-1]
    pc = jnp.minimum(p, pair_end[-1] - 1)
    pair_e = jnp.minimum(jnp.sum((pair_end[None, :] <= pc[:, None]).astype(I32), axis=1), ne - 1)
    pair_blk = (first_blk[pair_e] + pc - pair_start[pair_e]).astype(I32)
    eid = jnp.arange(ne, dtype=I32)
    later = lax.cummin(jnp.where(sizes > 0, eid, ne), reverse=True)
    next_e = jnp.concatenate([later[1:], jnp.full((1,), ne, I32)])
    next_e = jnp.where(next_e < ne, next_e, eid)
    return pair_e, pair_blk, ok.astype(I32), starts.astype(I32), ends.astype(I32), next_e


def kernel(x, c, w_ada, b_ada, w_in, lb_logits, hgrn_norm, lam_re, lam_im, log_dt, b_re, b_im,
           c_re, c_im, d_skip, w_glu, b_glu, w_out, w_router, router_bias, w_gate, w_up, w_down,
           ws_gate, ws_up, ws_down, final_gain):
    bsz, seq, d = x.shape
    n = bsz * seq
    fdim = HGRN_HEADS * HGRN_KDIM
    n_chunks = seq // CHUNK
    lb = jnp.cumsum(jax.nn.softmax(lb_logits.astype(F32), axis=0), axis=0)[0].reshape(1, fdim)

    mod = _mod_call(c, w_ada[0], b_ada[0]).reshape(bsz, 6, d)
    nb = w_in.shape[2] - 4 * fdim
    perm = jnp.concatenate([jnp.arange(0, nb, 2), jnp.arange(1, nb, 2)])
    w_in_b = w_in[0].astype(BF16)
    ut, out_a, ucm = _mix_front_call(x, mod, w_in_b[:, :4 * fdim], w_in_b[:, 4 * fdim:][:, perm].T, lb,
                                hgrn_norm[0].reshape(1, fdim))

    cb, lp2, p_tab, q_tab, a1, a2 = _s5_tables(lam_re[0], lam_im[0], log_dt[0], b_re[0], b_im[0],
                                             c_re[0], c_im[0])
    yt = _s5_call(ut, cb, lp2, p_tab, q_tab, a1, a2, n_chunks, bsz)

    wr_t = w_router[0].T
    wr_hi = wr_t.astype(BF16)
    wr_lo = (wr_t - wr_hi.astype(F32)).astype(BF16)
    w_out_b = w_out[0].astype(BF16)
    w_out_p = jnp.concatenate([w_out_b[:fdim], w_out_b[fdim:][perm]], axis=0)
    x1, hp, logits_t = _mix_back_call(
        x.reshape(n, d), out_a.reshape(n, fdim), yt, ucm, mod,
        d_skip[0][perm].reshape(nb, 1), w_glu[0][perm][:, perm].T.astype(BF16),
        b_glu[0][perm].reshape(nb, 1), w_out_p, wr_hi, wr_lo, seq)

    top_idx, top_w, rank, counts = _route_call(logits_t, router_bias[0])
    pair_e, pair_blk, pair_ok, starts, ends, next_e = _pair_tables(counts[:, 0], n * TOP_K)

    def windowed(a):
        return a.reshape(TOP_K, n // SC_WINDOW, SC_WINDOW).transpose(1, 0, 2)

    xs, pos_win = _dispatch_call(windowed(top_idx), windowed(rank), starts,
                                 hp)
    y_sorted = _experts_call(pair_e, pair_blk, pair_ok, starts, ends, next_e, xs,
                             w_gate[0], w_up[0], w_down[0])
    w_tok = top_w
    g2 = mod[:, 5:6, :]
    fgain = final_gain.reshape(1, d)
    wsg, wsu, wsd = ws_gate[0].astype(BF16), ws_up[0].astype(BF16), ws_down[0].astype(BF16)
    wins = pos_win.shape[0] // TAIL_PARTS
    out = None
    for part in range(TAIL_PARTS):
        y_tok = _collect_call(pos_win[part * wins:(part + 1) * wins], y_sorted)
        out = _combine_call(y_tok, x1, hp, w_tok, g2, fgain, wsg, wsu, wsd, seq, part, out)
    return out.reshape(bsz, seq, d)
```

```python
import functools

import jax
import jax.numpy as jnp
from jax import lax
from jax.experimental import pallas as pl
from jax.experimental.pallas import tpu as pltpu
from jax.experimental.pallas import tpu_sc as plsc

F32 = jnp.float32
BF16 = jnp.bfloat16
I32 = jnp.int32

EPS = 1e-6
CHUNK = 64
HGRN_HEADS = 4
HGRN_KDIM = 128
S5_GROUP = 16
S5_STATE = 64
N_EXPERT_GROUPS = 8
TOPK_GROUPS = 4
TOP_K = 8
ROUTE_SCALE = 2.5
LANES = 128
PACK_ROWS = 4
SC_CORES = 2
SC_SUBCORES = 16
SC_LANES = 16
SC_WINDOW = 64

SEQ_TILE = 512
TOK_TILE = 512
ROUTE_TILE = 512
MOVE_TILE = 512
ROW_BLOCK = 2048
SUB_BLOCK = 512
TAIL_PARTS = 4
VMEM_LIMIT = 56 * 1024 * 1024

_NT = (((1,), (1,)), ((), ()))
_TN = (((0,), (0,)), ((), ()))


def _sigmoid(v):
    return 0.5 * jnp.tanh(0.5 * v) + 0.5


def _silu(v):
    return v * _sigmoid(v)


def _bdot(a, b):
    return jnp.dot(a.astype(BF16), b.astype(BF16), preferred_element_type=F32)


def _store_packed_planes(ref, val):
    for j in range(PACK_ROWS):
        lo = val[:, 2 * j * LANES:(2 * j + 1) * LANES]
        hi = val[:, (2 * j + 1) * LANES:(2 * j + 2) * LANES]
        ref[j] = pltpu.pack_elementwise([lo, hi], packed_dtype=BF16)


def _mod_kernel(c_ref, w_ref, b_ref, o_ref):
    o_ref[...] = _bdot(_silu(c_ref[...]), w_ref[...]) + b_ref[...]


def _mod_call(c, w_ada, b_ada):
    bsz, d = c.shape
    n_out = w_ada.shape[1]
    return pl.pallas_call(
        _mod_kernel,
        out_shape=jax.ShapeDtypeStruct((bsz, n_out), F32),
        grid=(n_out // d,),
        in_specs=[pl.BlockSpec((bsz, d), lambda j: (0, 0)),
                  pl.BlockSpec((d, d), lambda j: (0, j)),
                  pl.BlockSpec((1, d), lambda j: (0, j))],
        out_specs=pl.BlockSpec((bsz, d), lambda j: (0, j)),
        compiler_params=pltpu.CompilerParams(vmem_limit_bytes=VMEM_LIMIT),
        name="mod",
    )(c, w_ada, b_ada.reshape(1, n_out))


def _split_chunk_pairs(tile_even, tile_odd):
    low = lax.broadcasted_iota(I32, tile_even.shape, 1) < CHUNK
    first = jnp.where(low, tile_even, pltpu.roll(tile_odd, CHUNK, 1))
    second = jnp.where(low, pltpu.roll(tile_even, CHUNK, 1), tile_odd)
    return first, second


def _mix_front_kernel(x_ref, mod_ref, win_ref, wut_ref, lb_ref, gn_ref, ltri_ref,
                      ut_ref, oa_ref, ucm_ref, proj_a, proj_b, st_ref, flat_ref, hb_ref, *, tiles_per_seq):
    fdim = HGRN_HEADS * HGRN_KDIM
    ts = x_ref.shape[1]
    pairs = wut_ref.shape[0] // 2
    rows = ts // CHUNK
    i = pl.program_id(0)

    @pl.when(i == 0)
    def _():
        proj_b[...] = jnp.zeros_like(proj_b)

    @pl.when(lax.rem(jnp.maximum(i - 1, 0), tiles_per_seq) == 0)
    def _():
        st_ref[...] = jnp.zeros_like(st_ref)

    def normalise():
        x = x_ref[0]
        ms = jnp.mean(x * x, axis=-1, keepdims=True)
        h = x * lax.rsqrt(ms + EPS) * (1.0 + mod_ref[0, 1:2, :]) + mod_ref[0, 0:1, :]
        hb_ref[...] = h.astype(BF16)

    def project_slab(proj_ref, n):
        width = proj_ref.shape[1] // rows
        cols = slice(n * width, (n + 1) * width)
        proj_ref[:, cols] = jnp.dot(hb_ref[...], win_ref[:, cols], preferred_element_type=F32)

    def s5_input():
        u_t = lax.dot_general(wut_ref[...], hb_ref[...], _NT, preferred_element_type=F32)
        ucm_ref[...] = u_t.astype(BF16)
        for m in range(ts // LANES):
            first, second = _split_chunk_pairs(u_t[:pairs, m * LANES:(m + 1) * LANES],
                                               u_t[pairs:, m * LANES:(m + 1) * LANES])
            flat_ref[2 * m * pairs:(2 * m + 1) * pairs, :] = first
            flat_ref[(2 * m + 1) * pairs:(2 * m + 2) * pairs, :] = second
        per_group = S5_GROUP // 2
        for g in range(pairs // per_group):
            slab = jnp.stack([flat_ref[c * pairs + g * per_group:c * pairs + (g + 1) * per_group, :]
                              for c in range(rows)], axis=0)
            ut_ref[g] = slab.reshape(rows, per_group * LANES)

    lb = lb_ref[...]
    gn = gn_ref[...]
    ltri = ltri_ref[...]
    row = lax.broadcasted_iota(I32, (CHUNK, CHUNK), 0)
    col = lax.broadcasted_iota(I32, (CHUNK, CHUNK), 1)
    causal = row >= col

    def step(proj_ref, next_ref):
        normalise()
        for ci in range(rows):
            project_slab(next_ref, ci)
            r0 = ci * CHUNK
            for half in range(2):
                hw = fdim // 2
                c0 = half * hw
                q = proj_ref[r0:r0 + CHUNK, c0:c0 + hw]
                fl = proj_ref[r0:r0 + CHUNK, fdim + c0:fdim + c0 + hw]
                iv = proj_ref[r0:r0 + CHUNK, 2 * fdim + c0:2 * fdim + c0 + hw]
                og = proj_ref[r0:r0 + CHUNK, 3 * fdim + c0:3 * fdim + c0 + hw]
                lbh = lb[:, c0:c0 + hw]
                f = lbh + (1.0 - lbh) * _sigmoid(fl)
                lf_hi, lf_mid, lf_lo = _split3(jnp.log(f))
                b = (jnp.dot(ltri, lf_hi, preferred_element_type=F32)
                     + jnp.dot(ltri, lf_mid, preferred_element_type=F32)
                     + jnp.dot(ltri, lf_lo, preferred_element_type=F32))
                b_ref = b[CHUNK // 2 - 1:CHUNK // 2, :]
                b_last = b[CHUNK - 1:CHUNK, :]
                qs = _silu(q)
                kk = 1.0 - f
                qe = (qs * jnp.exp(b - b_ref)).astype(BF16)
                ke = (kk * jnp.exp(b_ref - b)).astype(BF16)
                qb = (qs * jnp.exp(b)).astype(BF16)
                k2 = (kk * jnp.exp(b_last - b)).astype(BF16)
                dec = jnp.exp(b_last)
                ivb = iv.astype(BF16)
                outs = []
                for hl in range(HGRN_HEADS // 2):
                    hh = half * (HGRN_HEADS // 2) + hl
                    sl = slice(hl * HGRN_KDIM, (hl + 1) * HGRN_KDIM)
                    att = lax.dot_general(qe[:, sl], ke[:, sl], _NT, preferred_element_type=F32)
                    att = jnp.where(causal, att, 0.0)
                    st = st_ref[hh]
                    o = jnp.dot(att.astype(BF16), ivb[:, sl], preferred_element_type=F32)
                    o = o + lax.dot_general(qb[:, sl], st.astype(BF16), _NT, preferred_element_type=F32)
                    st_ref[hh] = st * dec[:, sl] + lax.dot_general(
                        ivb[:, sl], k2[:, sl], _TN, preferred_element_type=F32)
                    outs.append(o * lax.rsqrt(jnp.mean(o * o, axis=-1, keepdims=True) + EPS))
                o = jnp.concatenate(outs, axis=1) * gn[:, c0:c0 + hw] * _silu(og)
                oa_ref[0, r0:r0 + CHUNK, c0:c0 + hw] = o.astype(BF16)
        s5_input()

    @pl.when(lax.rem(i, 2) == 0)
    def _():
        step(proj_b, proj_a)

    @pl.when(lax.rem(i, 2) == 1)
    def _():
        step(proj_a, proj_b)


def _mix_front_call(x, mod, w_main, w_ut, lb, gn):
    bsz, seq, d = x.shape
    fdim = HGRN_HEADS * HGRN_KDIM
    ncols = w_main.shape[1]
    nb = w_ut.shape[0]
    groups = nb // S5_GROUP
    ltri = jnp.tril(jnp.ones((CHUNK, CHUNK), BF16))
    ts = SEQ_TILE
    tiles = seq // ts
    n_tiles = bsz * tiles
    rows = ts // CHUNK

    def cur(i):
        return jnp.minimum(i, n_tiles - 1)

    def prev(i):
        return jnp.maximum(i - 1, 0)

    return pl.pallas_call(
        functools.partial(_mix_front_kernel, tiles_per_seq=tiles),
        out_shape=(jax.ShapeDtypeStruct((groups, bsz * seq // CHUNK, S5_GROUP * CHUNK), F32),
                   jax.ShapeDtypeStruct((bsz, seq, fdim), BF16),
                   jax.ShapeDtypeStruct((nb, bsz * seq), BF16)),
        grid=(n_tiles + 1,),
        in_specs=[pl.BlockSpec((1, ts, d), lambda i: (cur(i) // tiles, cur(i) % tiles, 0)),
                  pl.BlockSpec((1, 6, d), lambda i: (cur(i) // tiles, 0, 0)),
                  pl.BlockSpec((d, ncols), lambda i: (0, 0)),
                  pl.BlockSpec((nb, d), lambda i: (0, 0)),
                  pl.BlockSpec((1, fdim), lambda i: (0, 0)),
                  pl.BlockSpec((1, fdim), lambda i: (0, 0)),
                  pl.BlockSpec((CHUNK, CHUNK), lambda i: (0, 0))],
        out_specs=(pl.BlockSpec((groups, rows, S5_GROUP * CHUNK), lambda i: (0, cur(i), 0)),
                   pl.BlockSpec((1, ts, fdim), lambda i: (prev(i) // tiles, prev(i) % tiles, 0)),
                   pl.BlockSpec((nb, ts), lambda i: (0, cur(i)))),
        scratch_shapes=[pltpu.VMEM((ts, ncols), F32), pltpu.VMEM((ts, ncols), F32),
                        pltpu.VMEM((HGRN_HEADS, fdim // HGRN_HEADS, HGRN_KDIM), F32),
                        pltpu.VMEM((rows * nb // 2, LANES), F32),
                        pltpu.VMEM((ts, d), BF16)],
        compiler_params=pltpu.CompilerParams(
            dimension_semantics=("arbitrary",), vmem_limit_bytes=VMEM_LIMIT),
        name="mix_front",
    )(x, mod, w_main, w_ut, lb, gn, ltri)


def _s5_tables(lam_re, lam_im, log_dt, b_re, b_im, c_re, c_im):
    t = CHUNK
    lam = lax.complex(jnp.minimum(lam_re, -1e-4), lam_im)
    lam_dt = lam * jnp.exp(log_dt)[:, None]
    lam_bar = jnp.exp(lam_dt)
    b_bar = ((lam_bar - 1.0) / lam)[..., None] * lax.complex(b_re, b_im)
    c_mat = lax.complex(c_re, c_im)
    taus = jnp.arange(t + 1, dtype=F32)
    lam_pow = jnp.exp(lam_dt[:, None, :] * taus[None, :, None])
    g, p = lam.shape
    c = b_re.shape[-1]
    cb = c_mat[:, None, :, :] * b_bar.transpose(0, 2, 1)[:, :, None, :]
    cb = jnp.concatenate([cb.real, -cb.imag], axis=-1)
    cb = cb.reshape(g, c, c // 2, 2 * 2 * p).reshape(g, c * c // 2, 4 * p)
    lp = jnp.concatenate([lam_pow[:, :t].real, lam_pow[:, :t].imag], axis=-1).transpose(0, 2, 1)
    zero = jnp.zeros_like(lp)
    lp2 = jnp.concatenate([jnp.concatenate([lp, zero], axis=2),
                           jnp.concatenate([zero, lp], axis=2)], axis=1)
    pc = lam_pow[:, t - 1::-1][:, :t, :, None] * b_bar[:, None, :, :]
    pc = pc.transpose(0, 3, 1, 2).reshape(g, c * t, p)
    p_tab = jnp.concatenate([pc.real, pc.imag], axis=-1)
    ql = c_mat[:, None, :, :] * lam_pow[:, 1:t + 1, None, :]
    ql = ql.transpose(0, 3, 2, 1).reshape(g, p, c * t)
    q_tab = jnp.concatenate([ql.real, -ql.imag], axis=1)
    lam_t = lam_pow[:, t]
    a1 = jnp.concatenate([lam_t.real, lam_t.real], axis=-1)[:, None, :]
    a2 = jnp.concatenate([-lam_t.imag, lam_t.imag], axis=-1)[:, None, :]
    return cb, lp2, p_tab.astype(BF16), q_tab.astype(BF16), a1, a2


def _split3(v):
    hi = v.astype(BF16)
    rem = v - hi.astype(F32)
    mid = rem.astype(BF16)
    return hi, mid, (rem - mid.astype(F32)).astype(BF16)


def _s5_kernel(u_ref, cb_ref, lp_ref, p_ref, q_ref, a1_ref, a2_ref, y_ref, v_ref, xs_ref, m_ref, k_ref,
               *, n_chunks, n_batch):
    c_hi, c_mid, c_lo = _split3(cb_ref[0])
    l_hi, l_mid, l_lo = _split3(lp_ref[0])
    k_ref[...] = (jnp.dot(c_hi, l_hi, preferred_element_type=F32)
                  + jnp.dot(c_hi, l_mid, preferred_element_type=F32)
                  + jnp.dot(c_mid, l_hi, preferred_element_type=F32)
                  + jnp.dot(c_hi, l_lo, preferred_element_type=F32)
                  + jnp.dot(c_mid, l_mid, preferred_element_type=F32)
                  + jnp.dot(c_lo, l_hi, preferred_element_type=F32))
    n_pairs = m_ref.shape[1] // LANES
    n_in = k_ref.shape[0] // n_pairs
    lane = lax.broadcasted_iota(I32, (CHUNK, LANES), 1)
    causal = (lane & (CHUNK - 1)) >= lax.broadcasted_iota(I32, (CHUNK, LANES), 0)
    for ci in range(n_in):
        for a in range(n_pairs):
            lags = jnp.broadcast_to(k_ref[ci * n_pairs + a:ci * n_pairs + a + 1, :], (CHUNK, LANES))
            tile = pltpu.roll(lags, 0, 1, stride=1, stride_axis=0)
            m_ref[ci * CHUNK:(ci + 1) * CHUNK, a * LANES:(a + 1) * LANES] = jnp.where(
                causal, tile, 0.0).astype(BF16)
    u = u_ref[0].astype(BF16)
    v_ref[...] = jnp.dot(u, p_ref[0], preferred_element_type=F32)
    a1 = a1_ref[0]
    a2 = a2_ref[0]
    half = xs_ref.shape[1] // 2

    def step(n, state):
        xs_ref[pl.ds(n, n_batch, stride=n_chunks), :] = state
        return (a1 * state + a2 * pltpu.roll(state, half, 1)
                + v_ref[pl.ds(n, n_batch, stride=n_chunks), :])

    lax.fori_loop(0, n_chunks, step, jnp.zeros((n_batch, xs_ref.shape[1]), F32))
    y = jnp.dot(u, m_ref[...], preferred_element_type=F32)
    y_ref[0] = y + jnp.dot(xs_ref[...].astype(BF16), q_ref[0], preferred_element_type=F32)


def _s5_call(ut, cb, lp2, p_tab, q_tab, a1, a2, n_chunks, n_batch):
    g, rows, width = ut.shape
    p2 = p_tab.shape[-1]
    return pl.pallas_call(
        functools.partial(_s5_kernel, n_chunks=n_chunks, n_batch=n_batch),
        out_shape=jax.ShapeDtypeStruct((g, rows, width), F32),
        grid=(g,),
        in_specs=[pl.BlockSpec((1, rows, width), lambda i: (i, 0, 0)),
                  pl.BlockSpec((1,) + cb.shape[1:], lambda i: (i, 0, 0)),
                  pl.BlockSpec((1,) + lp2.shape[1:], lambda i: (i, 0, 0)),
                  pl.BlockSpec((1, width, p2), lambda i: (i, 0, 0)),
                  pl.BlockSpec((1, p2, width), lambda i: (i, 0, 0)),
                  pl.BlockSpec((1, 1, p2), lambda i: (i, 0, 0)),
                  pl.BlockSpec((1, 1, p2), lambda i: (i, 0, 0))],
        out_specs=pl.BlockSpec((1, rows, width), lambda i: (i, 0, 0)),
        scratch_shapes=[pltpu.VMEM((rows, p2), F32), pltpu.VMEM((rows, p2), F32),
                        pltpu.VMEM((width, width), BF16),
                        pltpu.VMEM((cb.shape[1], lp2.shape[2]), F32)],
        compiler_params=pltpu.CompilerParams(
            dimension_semantics=("arbitrary",), vmem_limit_bytes=VMEM_LIMIT),
        name="s5",
    )(ut, cb, lp2, p_tab, q_tab, a1, a2)


def _token_major(flat_ref, src_ref):
    groups, rows, _ = src_ref.shape
    per_group = S5_GROUP // 2
    pairs = groups * per_group
    for g in range(groups):
        slab = src_ref[g].reshape(rows, per_group, LANES)
        for c in range(rows):
            flat_ref[c * pairs + g * per_group:c * pairs + (g + 1) * per_group, :] = slab[c]
    tiles = []
    for m in range(rows // 2):
        even, odd = _split_chunk_pairs(flat_ref[2 * m * pairs:(2 * m + 1) * pairs, :],
                                       flat_ref[(2 * m + 1) * pairs:(2 * m + 2) * pairs, :])
        tiles.append(jnp.concatenate([even, odd], axis=0))
    return jnp.concatenate(tiles, axis=1)


def _mix_back_kernel(x_ref, oa_ref, yt_ref, ucm_ref, mod_ref, dskip_ref, wglu_ref, bglu_ref,
                     wout_ref, wrh_ref, wrl_ref,
                     x1_ref, hp_ref, lt_ref, flat_ref):
    na = oa_ref.shape[1]
    y_t = _token_major(flat_ref, yt_ref)
    u_t = ucm_ref[...].astype(F32)
    z_t = jax.nn.gelu(y_t + dskip_ref[...] * u_t)
    gate_t = _sigmoid(jnp.dot(wglu_ref[...], z_t.astype(BF16), preferred_element_type=F32)
                      + bglu_ref[...])
    ob_t = (z_t * gate_t).astype(BF16)
    mixed = (jnp.dot(oa_ref[...], wout_ref[0:na, :], preferred_element_type=F32)
             + lax.dot_general(ob_t, wout_ref[na:, :], _TN, preferred_element_type=F32))
    x1 = x_ref[...] + mod_ref[0, 2:3, :] * mixed
    ms = jnp.mean(x1 * x1, axis=-1, keepdims=True)
    h2 = x1 * lax.rsqrt(ms + EPS) * (1.0 + mod_ref[0, 4:5, :]) + mod_ref[0, 3:4, :]
    _store_packed_planes(hp_ref, h2)
    h_hi = h2.astype(BF16)
    h_lo = (h2 - h_hi.astype(F32)).astype(BF16)
    lt = lax.dot_general(wrh_ref[...], h_hi, _NT, preferred_element_type=F32)
    lt = lt + lax.dot_general(wrl_ref[...], h_hi, _NT, preferred_element_type=F32)
    lt = lt + lax.dot_general(wrh_ref[...], h_lo, _NT, preferred_element_type=F32)
    lt_ref[...] = lt
    x1_ref[...] = x1


def _mix_back_call(x2d, oa, yt, ucm, mod, dskip, wglu, bglu, wout, wrh, wrl, seq):
    n, d = x2d.shape
    nb = oa.shape[1]
    ne = wrh.shape[0]
    tm = TOK_TILE
    per_b = seq // tm
    groups, _, width = yt.shape
    rows = tm // CHUNK
    const = lambda i: (0, 0)
    flat_block = pl.BlockSpec((groups, rows, width), lambda i: (0, i, 0))
    return pl.pallas_call(
        _mix_back_kernel,
        out_shape=(jax.ShapeDtypeStruct((n, d), F32),
                   jax.ShapeDtypeStruct((PACK_ROWS, n, LANES), I32),
                   jax.ShapeDtypeStruct((ne, n), F32)),
        grid=(n // tm,),
        in_specs=[pl.BlockSpec((tm, d), lambda i: (i, 0)),
                  pl.BlockSpec((tm, nb), lambda i: (i, 0)),
                  flat_block,
                  pl.BlockSpec((nb, tm), lambda i: (0, i)),
                  pl.BlockSpec((1, 6, d), lambda i: (i // per_b, 0, 0)),
                  pl.BlockSpec((nb, 1), const),
                  pl.BlockSpec((nb, nb), const),
                  pl.BlockSpec((nb, 1), const),
                  pl.BlockSpec((d, d), const),
                  pl.BlockSpec((ne, d), const),
                  pl.BlockSpec((ne, d), const)],
        out_specs=(pl.BlockSpec((tm, d), lambda i: (i, 0)),
                   pl.BlockSpec((PACK_ROWS, tm, LANES), lambda i: (0, i, 0)),
                   pl.BlockSpec((ne, tm), lambda i: (0, i))),
        scratch_shapes=[pltpu.VMEM((rows * nb // 2, LANES), F32)],
        compiler_params=pltpu.CompilerParams(
            dimension_semantics=("arbitrary",), vmem_limit_bytes=VMEM_LIMIT),
        name="mix_back",
    )(x2d, oa, yt, ucm, mod, dskip, wglu, bglu, wout, wrh, wrl)


def _route_kernel(lt_ref, bias_ref, su_ref, idx_ref, w_ref, rank_ref, cnt_ref, run_ref):
    ne, tr = lt_ref.shape
    per_group = ne // N_EXPERT_GROUPS
    neg = -jnp.inf

    @pl.when(pl.program_id(0) == 0)
    def _():
        run_ref[...] = jnp.zeros_like(run_ref)

    s = _sigmoid(lt_ref[...])
    sel = s + bias_ref[...]
    gio = lax.broadcasted_iota(I32, (per_group, tr), 0)
    gscore = []
    for g in range(N_EXPERT_GROUPS):
        v = sel[g * per_group:(g + 1) * per_group, :]
        m1 = jnp.max(v, axis=0, keepdims=True)
        i1 = jnp.min(jnp.where(v == m1, gio, per_group), axis=0, keepdims=True)
        m2 = jnp.max(jnp.where(gio == i1, neg, v), axis=0, keepdims=True)
        gscore.append(m1 + m2)
    masked = []
    for g in range(N_EXPERT_GROUPS):
        ahead = jnp.zeros((1, tr), I32)
        for o in range(N_EXPERT_GROUPS):
            if o == g:
                continue
            wins = (gscore[o] >= gscore[g]) if o < g else (gscore[o] > gscore[g])
            ahead = ahead + wins.astype(I32)
        keep = ahead < TOPK_GROUPS
        masked.append(jnp.where(keep, sel[g * per_group:(g + 1) * per_group, :], neg))
    selm = jnp.concatenate(masked, axis=0)
    eio = lax.broadcasted_iota(I32, (ne, tr), 0)
    candidate = selm > neg
    idxs, ws = [], []
    for k in range(TOP_K):
        m = jnp.max(selm, axis=0, keepdims=True)
        ik = jnp.min(jnp.where(selm == m, eio, ne), axis=0, keepdims=True)
        onehot = eio == ik
        ws.append(jnp.sum(jnp.where(onehot, s, 0.0), axis=0, keepdims=True))
        selm = jnp.where(onehot, neg, selm)
        idxs.append(ik)
    hits = jnp.where(jnp.logical_and(candidate, selm == neg), 1.0, 0.0)
    wsum = ws[0]
    for k in range(1, TOP_K):
        wsum = wsum + ws[k]
    scale = ROUTE_SCALE / wsum
    ranks = jnp.dot(hits.astype(BF16), su_ref[...], preferred_element_type=F32) + run_ref[...]
    for k in range(TOP_K):
        idx_ref[k:k + 1, :] = idxs[k]
        w_ref[k:k + 1, :] = ws[k] * scale
        rk = jnp.sum(jnp.where(eio == idxs[k], ranks, 0.0), axis=0, keepdims=True)
        rank_ref[k:k + 1, :] = rk.astype(I32)
    run_ref[...] = run_ref[...] + jnp.sum(hits, axis=1, keepdims=True)
    cnt_ref[...] = run_ref[...]


def _route_call(lt, bias):
    ne, n = lt.shape
    tr = ROUTE_TILE
    su = jnp.triu(jnp.ones((tr, tr), F32), k=1).astype(BF16)
    return pl.pallas_call(
        _route_kernel,
        out_shape=(jax.ShapeDtypeStruct((TOP_K, n), I32),
                   jax.ShapeDtypeStruct((TOP_K, n), F32),
                   jax.ShapeDtypeStruct((TOP_K, n), I32),
                   jax.ShapeDtypeStruct((ne, 1), F32)),
        grid=(n // tr,),
        in_specs=[pl.BlockSpec((ne, tr), lambda i: (0, i)),
                  pl.BlockSpec((ne, 1), lambda i: (0, 0)),
                  pl.BlockSpec((tr, tr), lambda i: (0, 0))],
        out_specs=(pl.BlockSpec((TOP_K, tr), lambda i: (0, i)),
                   pl.BlockSpec((TOP_K, tr), lambda i: (0, i)),
                   pl.BlockSpec((TOP_K, tr), lambda i: (0, i)),
                   pl.BlockSpec((ne, 1), lambda i: (0, 0))),
        scratch_shapes=[pltpu.VMEM((ne, 1), F32)],
        compiler_params=pltpu.CompilerParams(
            dimension_semantics=("arbitrary",), vmem_limit_bytes=VMEM_LIMIT),
        name="route",
    )(lt, bias.reshape(ne, 1), su)


def _sc_mesh():
    return plsc.VectorSubcoreMesh(core_axis_name="c", subcore_axis_name="s",
                                  num_cores=SC_CORES, num_subcores=SC_SUBCORES)


def _sc_worker():
    return lax.axis_index("s") * SC_CORES + lax.axis_index("c")


def _dispatch_call(idx_win, rank_win, starts, hp):
    n = hp.shape[1]
    n_workers = SC_CORES * SC_SUBCORES
    wins_per_worker = n // SC_WINDOW // n_workers
    lanes = SC_LANES

    def body(hp_hbm, idx_hbm, rank_hbm, st_hbm, xs_hbm, pos_hbm,
             idx_a, idx_b, rank_a, rank_b, pos_a, pos_b, rows_a, rows_b, st_v,
             load_a, load_b, scatter_sem, pos_sem):
        first_win = _sc_worker() * wins_per_worker
        idx_v, rank_v, pos_v = (idx_a, idx_b), (rank_a, rank_b), (pos_a, pos_b)
        rows_v, load_sem = (rows_a, rows_b), (load_a, load_b)
        pltpu.sync_copy(st_hbm, st_v)

        def loads(win, slot):
            pieces = tuple(
                pltpu.make_async_copy(hp_hbm.at[j, pl.ds(win * SC_WINDOW, SC_WINDOW)],
                                      rows_v[slot].at[:, j], load_sem[slot])
                for j in range(PACK_ROWS))
            return pieces + (pltpu.make_async_copy(idx_hbm.at[win], idx_v[slot], load_sem[slot]),
                             pltpu.make_async_copy(rank_hbm.at[win], rank_v[slot], load_sem[slot]))

        for cp in loads(first_win, 0):
            cp.start()

        @pl.loop(0, wins_per_worker, step=2)
        def _(w):
            for slot in range(2):
                win = first_win + w + slot
                for cp in loads(win, slot):
                    cp.wait()

                @pl.when(w + slot + 1 < wins_per_worker)
                def _():
                    for cp in loads(win + 1, 1 - slot):
                        cp.start()

                for k in range(TOP_K):
                    for c in range(SC_WINDOW // lanes):
                        seg = pl.ds(c * lanes, lanes)
                        base = plsc.load_gather(st_v, [idx_v[slot][k, seg]])
                        pos_v[slot][k, seg] = base + rank_v[slot][k, seg]
                pos_out = pltpu.async_copy(pos_v[slot], pos_hbm.at[win], pos_sem)
                copies = [pltpu.async_copy(rows_v[slot], xs_hbm.at[pos_v[slot].at[k]], scatter_sem)
                          for k in range(TOP_K)]
                for cp in copies:
                    cp.wait()
                pos_out.wait()

    idx_buf = pltpu.VMEM((TOP_K, SC_WINDOW), I32)
    row_buf = pltpu.VMEM((SC_WINDOW, PACK_ROWS, LANES), hp.dtype)
    return pl.kernel(
        body,
        out_type=(jax.ShapeDtypeStruct((n * TOP_K, PACK_ROWS, LANES), hp.dtype),
                  jax.ShapeDtypeStruct(idx_win.shape, I32)),
        mesh=_sc_mesh(),
        scratch_types=[idx_buf, idx_buf, idx_buf, idx_buf, idx_buf, idx_buf, row_buf, row_buf,
                       pltpu.VMEM(starts.shape, I32),
                       pltpu.SemaphoreType.DMA, pltpu.SemaphoreType.DMA, pltpu.SemaphoreType.DMA,
                       pltpu.SemaphoreType.DMA],
        compiler_params=pltpu.CompilerParams(needs_layout_passes=False),
        name="dispatch",
    )(hp, idx_win, rank_win, starts)


def _collect_call(pos_win, y_sorted):
    n = pos_win.shape[0] * SC_WINDOW
    n_workers = SC_CORES * SC_SUBCORES
    wins_per_worker = n // SC_WINDOW // n_workers

    def body(ys_hbm, pos_hbm, out_hbm, idx_a, idx_b, rows_a, rows_b, idx_sem_a, idx_sem_b,
             gather_sem, write_sem):
        first_win = _sc_worker() * wins_per_worker
        bufs = (rows_a, rows_b)
        idx_v, idx_sem = (idx_a, idx_b), (idx_sem_a, idx_sem_b)

        def idx_load(win, slot):
            return pltpu.make_async_copy(pos_hbm.at[win], idx_v[slot], idx_sem[slot])

        idx_load(first_win, 0).start()

        @pl.loop(0, wins_per_worker, step=2)
        def _(w):
            for slot in range(2):
                win = first_win + w + slot
                idx_load(win, slot).wait()

                @pl.when(w + slot + 1 < wins_per_worker)
                def _():
                    idx_load(win + 1, 1 - slot).start()

                def gather(k):
                    return pltpu.async_copy(ys_hbm.at[idx_v[slot].at[k]], bufs[k % 2], gather_sem)

                pending_gather = gather(0)
                pending_write = None
                for k in range(TOP_K):
                    pending_gather.wait()
                    if pending_write is not None:
                        for cp in pending_write:
                            cp.wait()
                    if k + 1 < TOP_K:
                        pending_gather = gather(k + 1)
                    pending_write = [
                        pltpu.async_copy(bufs[k % 2].at[:, j],
                                         out_hbm.at[j, k, pl.ds(win * SC_WINDOW, SC_WINDOW)], write_sem)
                        for j in range(PACK_ROWS)]
                for cp in pending_write:
                    cp.wait()

    idx_buf = pltpu.VMEM((TOP_K, SC_WINDOW), I32)
    row_buf = pltpu.VMEM((SC_WINDOW,) + y_sorted.shape[1:], y_sorted.dtype)
    return pl.kernel(
        body,
        out_type=jax.ShapeDtypeStruct((PACK_ROWS, TOP_K, n, LANES), y_sorted.dtype),
        mesh=_sc_mesh(),
        scratch_types=[idx_buf, idx_buf, row_buf, row_buf,
                       pltpu.SemaphoreType.DMA, pltpu.SemaphoreType.DMA,
                       pltpu.SemaphoreType.DMA, pltpu.SemaphoreType.DMA],
        name="collect",
    )(y_sorted, pos_win)


def _experts_kernel(pe_ref, pb_ref, pv_ref, st_ref, en_ref, nx_ref,
                    xs_hbm, wg_hbm, wu_hbm, wd_hbm, y_hbm,
                    wgb_ref, wub_ref, wdb_ref, wgf_ref, wuf_ref, wdf_ref, xbuf_ref, stage_ref,
                    in_sems, out_sems, w_sems, state_ref):
    p = pl.program_id(0)
    n_pairs = pl.num_programs(0)
    e = pe_ref[p]
    blk = pb_ref[p]
    rb = xbuf_ref.shape[2]
    prev = jnp.maximum(p - 1, 0)
    nxt = jnp.minimum(p + 1, n_pairs - 1)
    live = pv_ref[p] == 1
    first = jnp.logical_or(p == 0, pb_ref[prev] != blk)
    block_ends = jnp.logical_or(pb_ref[nxt] != blk, pv_ref[nxt] == 0)
    last = jnp.logical_or(p == n_pairs - 1, block_ends)

    def in_copy(j, block, slot):
        return pltpu.make_async_copy(xs_hbm.at[pl.ds(block * rb, rb), j], xbuf_ref.at[slot, j],
                                     in_sems.at[slot])

    def out_copy(j, block, slot):
        return pltpu.make_async_copy(stage_ref.at[slot, j], y_hbm.at[pl.ds(block * rb, rb), j],
                                     out_sems.at[slot])

    def drain(slot):
        @pl.when(state_ref[1 + slot] == 1)
        def _():
            for j in range(PACK_ROWS):
                out_copy(j, 0, slot).wait()
            state_ref[1 + slot] = 0

    @pl.when(p == 0)
    def _():
        state_ref[0] = 0
        state_ref[1] = 0
        state_ref[2] = 0
        stage_ref[...] = jnp.zeros_like(stage_ref)
        for j in range(PACK_ROWS):
            in_copy(j, blk, 0).start()

    slot = jnp.where(jnp.logical_and(first, p > 0), 1 - state_ref[0], state_ref[0])
    state_ref[0] = slot

    @pl.when(jnp.logical_and(first, blk + 1 < y_hbm.shape[0] // rb))
    def _():
        for j in range(PACK_ROWS):
            in_copy(j, blk + 1, 1 - slot).start()

    @pl.when(first)
    def _():
        for j in range(PACK_ROWS):
            in_copy(j, blk, slot).wait()
        drain(slot)

    def w_copies(expert, wslot):
        return (pltpu.make_async_copy(wg_hbm.at[expert], wgf_ref.at[wslot], w_sems.at[wslot]),
                pltpu.make_async_copy(wu_hbm.at[expert], wuf_ref.at[wslot], w_sems.at[wslot]),
                pltpu.make_async_copy(wd_hbm.at[expert], wdf_ref.at[wslot], w_sems.at[wslot]))

    @pl.when(p == 0)
    def _():
        state_ref[3] = 0
        for cp in w_copies(e, 0):
            cp.start()

    new_expert = jnp.logical_or(p == 0, pe_ref[prev] != e)
    wslot = jnp.where(jnp.logical_and(new_expert, p > 0), 1 - state_ref[3], state_ref[3])
    state_ref[3] = wslot

    @pl.when(new_expert)
    def _():
        for cp in w_copies(e, wslot):
            cp.wait()
        wgb_ref[...] = wgf_ref[wslot].astype(BF16)
        wub_ref[...] = wuf_ref[wslot].astype(BF16)
        wdb_ref[...] = wdf_ref[wslot].astype(BF16)

        @pl.when(nx_ref[e] != e)
        def _():
            for cp in w_copies(nx_ref[e], 1 - wslot):
                cp.start()

    lo_row = st_ref[e]
    hi_row = en_ref[e]

    def sub_block(s, row0, shared):
        sub = pl.ds(s * SUB_BLOCK, SUB_BLOCK)
        pieces = []
        for j in range(PACK_ROWS):
            w = xbuf_ref[slot, j, sub, :]
            pieces.append(lax.bitcast_convert_type(w.astype(jnp.int16), BF16))
            pieces.append(lax.bitcast_convert_type(
                lax.shift_right_logical(w, 16).astype(jnp.int16), BF16))
        xb = jnp.concatenate(pieces, axis=1)
        gate = jnp.dot(xb, wgb_ref[...], preferred_element_type=F32)
        up = jnp.dot(xb, wub_ref[...], preferred_element_type=F32)
        yb = jnp.dot((_silu(gate) * up).astype(BF16), wdb_ref[...], preferred_element_type=F32)
        if shared:
            rows = row0 + lax.broadcasted_iota(I32, (SUB_BLOCK, 1), 0)
            mine = jnp.logical_and(rows >= lo_row, rows < hi_row)
        for j in range(PACK_ROWS):
            word = pltpu.pack_elementwise(
                [yb[:, 2 * j * LANES:(2 * j + 1) * LANES], yb[:, (2 * j + 1) * LANES:(2 * j + 2) * LANES]],
                packed_dtype=BF16)
            if shared:
                word = jnp.where(mine, word, stage_ref[slot, j, sub, :])
            stage_ref[slot, j, sub, :] = word

    def one(s):
        row0 = blk * rb + s * SUB_BLOCK
        touched = jnp.logical_and(live, jnp.logical_and(row0 < hi_row, row0 + SUB_BLOCK > lo_row))
        whole = jnp.logical_and(lo_row <= row0, hi_row >= row0 + SUB_BLOCK)

        @pl.when(jnp.logical_and(touched, whole))
        def _():
            sub_block(s, row0, shared=False)

        @pl.when(jnp.logical_and(touched, jnp.logical_not(whole)))
        def _():
            sub_block(s, row0, shared=True)

    for s in range(0, rb // SUB_BLOCK, 2):
        row0 = blk * rb + s * SUB_BLOCK
        both = jnp.logical_and(live, jnp.logical_and(lo_row <= row0, hi_row >= row0 + 2 * SUB_BLOCK))

        @pl.when(both)
        def _():
            sub_block(s, row0, shared=False)
            sub_block(s + 1, row0 + SUB_BLOCK, shared=False)

        @pl.when(jnp.logical_not(both))
        def _():
            one(s)
            one(s + 1)

    @pl.when(jnp.logical_and(live, last))
    def _():
        for j in range(PACK_ROWS):
            out_copy(j, blk, slot).start()
        state_ref[1 + slot] = 1

    @pl.when(p == n_pairs - 1)
    def _():
        drain(0)
        drain(1)


def _experts_call(pair_e, pair_blk, pair_ok, starts, ends, next_e, xs, w_gate, w_up, w_down):
    ne, d, de = w_gate.shape
    rb = ROW_BLOCK
    n_pairs = pair_e.shape[0]
    anywhere = pl.BlockSpec(memory_space=pl.ANY)
    grid_spec = pltpu.PrefetchScalarGridSpec(
        num_scalar_prefetch=6,
        grid=(n_pairs,),
        in_specs=[anywhere, anywhere, anywhere, anywhere],
        out_specs=anywhere,
        scratch_shapes=[pltpu.VMEM((d, de), BF16), pltpu.VMEM((d, de), BF16),
                        pltpu.VMEM((de, d), BF16),
                        pltpu.VMEM((2, d, de), F32), pltpu.VMEM((2, d, de), F32),
                        pltpu.VMEM((2, de, d), F32),
                        pltpu.VMEM((2, PACK_ROWS, rb, LANES), I32),
                        pltpu.VMEM((2, PACK_ROWS, rb, LANES), I32),
                        pltpu.SemaphoreType.DMA((2,)), pltpu.SemaphoreType.DMA((2,)),
                        pltpu.SemaphoreType.DMA((2,)),
                        pltpu.SMEM((4,), I32)],
    )
    return pl.pallas_call(
        _experts_kernel,
        out_shape=jax.ShapeDtypeStruct(xs.shape, xs.dtype),
        grid_spec=grid_spec,
        compiler_params=pltpu.CompilerParams(
            dimension_semantics=("arbitrary",), vmem_limit_bytes=VMEM_LIMIT),
        name="experts",
    )(pair_e, pair_blk, pair_ok, starts, ends, next_e, xs, w_gate, w_up, w_down)


def _combine_kernel(y0_ref, y1_ref, y2_ref, y3_ref, x1_ref, hp_ref, w_ref, g2_ref, fg_ref,
                    wsg_ref, wsu_ref, wsd_ref, o_ref):
    tm = x1_ref.shape[0]
    w = w_ref[...].T
    parts = []
    for y_ref in (y0_ref, y1_ref, y2_ref, y3_ref):
        for half in range(2):
            acc = None
            for k in range(TOP_K):
                piece = pltpu.unpack_elementwise(y_ref[k], index=half, packed_dtype=BF16,
                                                 unpacked_dtype=F32) * w[:, k:k + 1]
                acc = piece if acc is None else acc + piece
            parts.append(acc)
    routed = jnp.concatenate(parts, axis=1)
    pieces = []
    for j in range(PACK_ROWS):
        word = hp_ref[j]
        pieces.append(lax.bitcast_convert_type(word.astype(jnp.int16), BF16))
        pieces.append(lax.bitcast_convert_type(lax.shift_right_logical(word, 16).astype(jnp.int16), BF16))
    h2 = jnp.concatenate(pieces, axis=1)
    hid = _silu(jnp.dot(h2, wsg_ref[...], preferred_element_type=F32)) * jnp.dot(
        h2, wsu_ref[...], preferred_element_type=F32)
    shared = jnp.dot(hid.astype(BF16), wsd_ref[...], preferred_element_type=F32)
    x2 = x1_ref[...] + g2_ref[0] * (routed + shared)
    ms = jnp.mean(x2 * x2, axis=-1, keepdims=True)
    o_ref[...] = x2 * lax.rsqrt(ms + EPS) * fg_ref[...]


def _combine_kernel_into(*refs):
    _combine_kernel(*refs[:-2], refs[-1])


def _combine_call(y_tok, x1, hp, w_tok, g2, fgain, wsg, wsu, wsd, seq, part, prev_out):
    n, d = x1.shape
    dsh = wsg.shape[1]
    tm = MOVE_TILE
    per_b = seq // tm
    tiles = y_tok.shape[2] // tm
    t0 = part * tiles
    const = lambda i: (0, 0)

    def piece_spec(j):
        return pl.BlockSpec((None, TOP_K, tm, LANES), lambda i: (j, 0, i, 0))

    in_specs = [piece_spec(j) for j in range(PACK_ROWS)] + [
        pl.BlockSpec((tm, d), lambda i: (t0 + i, 0)),
        pl.BlockSpec((PACK_ROWS, tm, LANES), lambda i: (0, t0 + i, 0)),
        pl.BlockSpec((TOP_K, tm), lambda i: (0, t0 + i)),
        pl.BlockSpec((1, 1, d), lambda i: ((t0 + i) // per_b, 0, 0)),
        pl.BlockSpec((1, d), const),
        pl.BlockSpec((d, dsh), const),
        pl.BlockSpec((d, dsh), const),
        pl.BlockSpec((dsh, d), const)]
    args = [y_tok, y_tok, y_tok, y_tok, x1, hp, w_tok, g2, fgain, wsg, wsu, wsd]
    aliases = {}
    body = _combine_kernel
    if prev_out is not None:
        in_specs.append(pl.BlockSpec(memory_space=pl.ANY))
        args.append(prev_out)
        aliases = {len(args) - 1: 0}
        body = _combine_kernel_into
    return pl.pallas_call(
        body,
        out_shape=jax.ShapeDtypeStruct((n, d), F32),
        grid=(tiles,),
        in_specs=in_specs,
        out_specs=pl.BlockSpec((tm, d), lambda i: (t0 + i, 0)),
        input_output_aliases=aliases,
        compiler_params=pltpu.CompilerParams(
            dimension_semantics=("arbitrary",), vmem_limit_bytes=VMEM_LIMIT),
        name="combine",
    )(*args)


def _pair_tables(counts, n_rows):
    ne = counts.shape[0]
    sizes = counts.astype(I32)
    ends = jnp.cumsum(sizes)
    starts = ends - sizes
    first_blk = starts // ROW_BLOCK
    last_blk = (ends - 1) // ROW_BLOCK
    n_pairs = jnp.where(sizes > 0, last_blk - first_blk + 1, 0)
    pair_end = jnp.cumsum(n_pairs)
    pair_start = pair_end - n_pairs
    max_pairs = n_rows // ROW_BLOCK + ne
    p = jnp.arange(max_pairs, dtype=I32)
    ok = p < pair_end[-1]
    pc = jnp.minimum(p, pair_end[-1] - 1)
    pair_e = jnp.minimum(jnp.sum((pair_end[None, :] <= pc[:, None]).astype(I32), axis=1), ne - 1)
    pair_blk = (first_blk[pair_e] + pc - pair_start[pair_e]).astype(I32)
    eid = jnp.arange(ne, dtype=I32)
    later = lax.cummin(jnp.where(sizes > 0, eid, ne), reverse=True)
    next_e = jnp.concatenate([later[1:], jnp.full((1,), ne, I32)])
    next_e = jnp.where(next_e < ne, next_e, eid)
    return pair_e, pair_blk, ok.astype(I32), starts.astype(I32), ends.astype(I32), next_e


def kernel(x, c, w_ada, b_ada, w_in, lb_logits, hgrn_norm, lam_re, lam_im, log_dt, b_re, b_im,
           c_re, c_im, d_skip, w_glu, b_glu, w_out, w_router, router_bias, w_gate, w_up, w_down,
           ws_gate, ws_up, ws_down, final_gain):
    bsz, seq, d = x.shape
    n = bsz * seq
    fdim = HGRN_HEADS * HGRN_KDIM
    n_chunks = seq // CHUNK
    lb = jnp.cumsum(jax.nn.softmax(lb_logits.astype(F32), axis=0), axis=0)[0].reshape(1, fdim)

    mod = _mod_call(c, w_ada[0], b_ada[0]).reshape(bsz, 6, d)
    nb = w_in.shape[2] - 4 * fdim
    perm = jnp.concatenate([jnp.arange(0, nb, 2), jnp.arange(1, nb, 2)])
    w_in_b = w_in[0].astype(BF16)
    ut, out_a, ucm = _mix_front_call(x, mod, w_in_b[:, :4 * fdim], w_in_b[:, 4 * fdim:][:, perm].T, lb,
                                hgrn_norm[0].reshape(1, fdim))

    cb, lp2, p_tab, q_tab, a1, a2 = _s5_tables(lam_re[0], lam_im[0], log_dt[0], b_re[0], b_im[0],
                                             c_re[0], c_im[0])
    yt = _s5_call(ut, cb, lp2, p_tab, q_tab, a1, a2, n_chunks, bsz)

    wr_t = w_router[0].T
    wr_hi = wr_t.astype(BF16)
    wr_lo = (wr_t - wr_hi.astype(F32)).astype(BF16)
    w_out_b = w_out[0].astype(BF16)
    w_out_p = jnp.concatenate([w_out_b[:fdim], w_out_b[fdim:][perm]], axis=0)
    x1, hp, logits_t = _mix_back_call(
        x.reshape(n, d), out_a.reshape(n, fdim), yt, ucm, mod,
        d_skip[0][perm].reshape(nb, 1), w_glu[0][perm][:, perm].T.astype(BF16),
        b_glu[0][perm].reshape(nb, 1), w_out_p, wr_hi, wr_lo, seq)

    top_idx, top_w, rank, counts = _route_call(logits_t, router_bias[0])
    pair_e, pair_blk, pair_ok, starts, ends, next_e = _pair_tables(counts[:, 0], n * TOP_K)

    def windowed(a):
        return a.reshape(TOP_K, n // SC_WINDOW, SC_WINDOW).transpose(1, 0, 2)

    xs, pos_win = _dispatch_call(windowed(top_idx), windowed(rank), starts,
                                 hp)
    y_sorted = _experts_call(pair_e, pair_blk, pair_ok, starts, ends, next_e, xs,
                             w_gate[0], w_up[0], w_down[0])
    w_tok = top_w
    g2 = mod[:, 5:6, :]
    fgain = final_gain.reshape(1, d)
    wsg, wsu, wsd = ws_gate[0].astype(BF16), ws_up[0].astype(BF16), ws_down[0].astype(BF16)
    wins = pos_win.shape[0] // TAIL_PARTS
    out = None
    for part in range(TAIL_PARTS):
        y_tok = _collect_call(pos_win[part * wins:(part + 1) * wins], y_sorted)
        out = _combine_call(y_tok, x1, hp, w_tok, g2, fgain, wsg, wsu, wsd, seq, part, out)
    return out.reshape(bsz, seq, d)
```
